```python
import jax, jax.numpy as jnp
from jax import lax
import numpy as np

D_MODEL = 1024
BATCH = 8
SEQ = 8192
DEPTH = 1

CHUNK = 64
N_LEFT_CHUNKS = 8
BAND = (N_LEFT_CHUNKS + 1) * CHUNK
MIX_WIDTH = D_MODEL
CONV_WIDTH = MIX_WIDTH // 2
ATTN_WIDTH = MIX_WIDTH - CONV_WIDTH
HEAD_DIM = 64
N_HEADS = ATTN_WIDTH // HEAD_DIM
CONV_KERNEL = 31
MAX_REL = 128
D_FF = 2816
FFN_CONV_KERNEL = 3
IN_COLS = 2 * CONV_WIDTH + 3 * ATTN_WIDTH
EPS = 1e-6
NEG_INF = -1e30

kernel_name = "chunk_causal_conformer_hybrid_block"


def rms_norm(x, g):
    xf = x.astype(jnp.float32)
    y = xf * lax.rsqrt(jnp.mean(xf * xf, axis=-1, keepdims=True) + EPS)
    return (y * g.astype(jnp.float32)).astype(x.dtype)


def layer_norm(x, g, b):
    xf = x.astype(jnp.float32)
    mu = jnp.mean(xf, axis=-1, keepdims=True)
    xc = xf - mu
    var = jnp.mean(xc * xc, axis=-1, keepdims=True)
    y = xc * lax.rsqrt(var + EPS) * g.astype(jnp.float32) + b.astype(jnp.float32)
    return y.astype(x.dtype)


def causal_depthwise_conv(x, w, b):
    k = w.shape[0]
    c = x.shape[-1]
    y = lax.conv_general_dilated(
        x, w[:, None, :].astype(x.dtype), window_strides=(1,), padding=[(k - 1, 0)],
        dimension_numbers=("NWC", "WIO", "NWC"), feature_group_count=c)
    return y + b.astype(x.dtype)


def conformer_conv_group(a_val, a_gate, dw_w, dw_b, ln_g, ln_b):
    h = a_val * jax.nn.sigmoid(a_gate)
    h = causal_depthwise_conv(h, dw_w, dw_b)
    h = layer_norm(h, ln_g, ln_b)
    return jax.nn.silu(h)


def rel_bias_band(rel_table):
    i = np.arange(CHUNK)[:, None]
    j = np.arange(BAND)[None, :]
    rel = N_LEFT_CHUNKS * CHUNK + i - j
    idx = np.clip(rel, -MAX_REL, MAX_REL) + MAX_REL
    return rel_table[:, idx]


def chunked_band_attention(q, k, v, rel_table):
    b, t, _ = q.shape
    nc = t // CHUNK
    pad = N_LEFT_CHUNKS * CHUNK
    q = q.reshape(b, nc, CHUNK, N_HEADS, HEAD_DIM)

    def band(z):
        z = z.reshape(b, t, N_HEADS, HEAD_DIM)
        z = jnp.pad(z, ((0, 0), (pad, 0), (0, 0), (0, 0)))
        zc = z.reshape(b, nc + N_LEFT_CHUNKS, CHUNK, N_HEADS, HEAD_DIM)
        return jnp.concatenate([zc[:, s:s + nc] for s in range(N_LEFT_CHUNKS + 1)], axis=2)

    kb = band(k)
    vb = band(v)
    scale = HEAD_DIM ** -0.5
    scores = jnp.einsum("bnqhd,bnkhd->bhnqk", q, kb,
                        preferred_element_type=jnp.float32) * scale
    scores = scores + rel_bias_band(rel_table).astype(jnp.float32)[None, :, None]
    key_pos = (jnp.arange(nc)[:, None] - N_LEFT_CHUNKS) * CHUNK + jnp.arange(BAND)[None, :]
    valid = (key_pos >= 0)[None, None, :, None, :]
    scores = jnp.where(valid, scores, NEG_INF)
    p = jax.nn.softmax(scores, axis=-1)
    out = jnp.einsum("bhnqk,bnkhd->bnqhd", p.astype(vb.dtype), vb)
    return out.reshape(b, t, ATTN_WIDTH)


def conv_gated_ffn(u, w_up, dw_w, dw_b, w_down):
    h = u @ w_up
    h = causal_depthwise_conv(h, dw_w, dw_b)
    gate, val = jnp.split(h, 2, axis=-1)
    return (jax.nn.gelu(gate) * val) @ w_down


def _fwd_setup_inputs(seed: int = 0) -> dict:
    key = jax.random.key(seed)
    ks = jax.random.split(key, 20)
    L = DEPTH
    nrm = jax.random.normal
    f32 = jnp.float32
    return {
        "x": nrm(ks[0], (BATCH, SEQ, D_MODEL), f32),
        "norm_mix_pre": 1.0 + 0.05 * nrm(ks[1], (L, D_MODEL), f32),
        "w_in": nrm(ks[2], (L, D_MODEL, IN_COLS), f32) * D_MODEL ** -0.5,
        "conv_dw_w": nrm(ks[3], (L, CONV_KERNEL, CONV_WIDTH), f32) * CONV_KERNEL ** -0.5,
        "conv_dw_b": 0.01 * nrm(ks[4], (L, CONV_WIDTH), f32),
        "conv_ln_g": 1.0 + 0.05 * nrm(ks[5], (L, CONV_WIDTH), f32),
        "conv_ln_b": 0.01 * nrm(ks[6], (L, CONV_WIDTH), f32),
        "rel_bias": 0.5 * nrm(ks[7], (L, N_HEADS, 2 * MAX_REL + 1), f32),
        "w_out": nrm(ks[8], (L, MIX_WIDTH, D_MODEL), f32) * MIX_WIDTH ** -0.5,
        "norm_mix_post": 1.0 + 0.05 * nrm(ks[9], (L, D_MODEL), f32),
        "norm_ffn_pre": 1.0 + 0.05 * nrm(ks[10], (L, D_MODEL), f32),
        "w_up": nrm(ks[11], (L, D_MODEL, 2 * D_FF), f32) * D_MODEL ** -0.5,
        "ffn_dw_w": nrm(ks[12], (L, FFN_CONV_KERNEL, 2 * D_FF), f32) * FFN_CONV_KERNEL ** -0.5,
        "ffn_dw_b": 0.01 * nrm(ks[13], (L, 2 * D_FF), f32),
        "w_down": nrm(ks[14], (L, D_FF, D_MODEL), f32) * D_FF ** -0.5,
        "norm_ffn_post": 1.0 + 0.05 * nrm(ks[15], (L, D_MODEL), f32),
    }


def _fwd_reference(x, norm_mix_pre, w_in, conv_dw_w, conv_dw_b, conv_ln_g, conv_ln_b,
              rel_bias, w_out, norm_mix_post, norm_ffn_pre, w_up, ffn_dw_w, ffn_dw_b,
              w_down, norm_ffn_post):
    h = x
    for l in range(DEPTH):
        u = rms_norm(h, norm_mix_pre[l])
        proj = u @ w_in[l]
        a_val, a_gate, q, k, v = jnp.split(
            proj, np.cumsum([CONV_WIDTH, CONV_WIDTH, ATTN_WIDTH, ATTN_WIDTH]).tolist(), axis=-1)
        conv_out = conformer_conv_group(a_val, a_gate, conv_dw_w[l], conv_dw_b[l],
                                        conv_ln_g[l], conv_ln_b[l])
        attn_out = chunked_band_attention(q, k, v, rel_bias[l])
        mixed = jnp.concatenate([conv_out, attn_out], axis=-1) @ w_out[l]
        h = h + rms_norm(mixed, norm_mix_post[l])
        u = rms_norm(h, norm_ffn_pre[l])
        f = conv_gated_ffn(u, w_up[l], ffn_dw_w[l], ffn_dw_b[l], w_down[l])
        h = h + rms_norm(f, norm_ffn_post[l])
    return h


import jax as _jax
import jax.numpy as _jnp

TWIN_FORMAT = 'train_step'
FWD_PARAMS = ['x', 'norm_mix_pre', 'w_in', 'conv_dw_w', 'conv_dw_b', 'conv_ln_g', 'conv_ln_b', 'rel_bias', 'w_out', 'norm_mix_post', 'norm_ffn_pre', 'w_up', 'ffn_dw_w', 'ffn_dw_b', 'w_down', 'norm_ffn_post']
TWIN_WEIGHTS = ['norm_mix_pre', 'w_in', 'conv_dw_w', 'conv_dw_b', 'conv_ln_g', 'conv_ln_b', 'rel_bias', 'w_out', 'norm_mix_post', 'norm_ffn_pre', 'w_up', 'ffn_dw_w', 'ffn_dw_b', 'w_down', 'norm_ffn_post']
TWIN_DIFF_INPUT = 'x'
TWIN_INPUTS = ['x', 'norm_mix_pre', 'w_in', 'conv_dw_w', 'conv_dw_b', 'conv_ln_g', 'conv_ln_b', 'rel_bias', 'w_out', 'norm_mix_post', 'norm_ffn_pre', 'w_up', 'ffn_dw_w', 'ffn_dw_b', 'w_down', 'norm_ffn_post', 'loss_target', 'm_norm_mix_pre', 'm_w_in', 'm_conv_dw_w', 'm_conv_dw_b', 'm_conv_ln_g', 'm_conv_ln_b', 'm_rel_bias', 'm_w_out', 'm_norm_mix_post', 'm_norm_ffn_pre', 'm_w_up', 'm_ffn_dw_w', 'm_ffn_dw_b', 'm_w_down', 'm_norm_ffn_post', 'v_norm_mix_pre', 'v_w_in', 'v_conv_dw_w', 'v_conv_dw_b', 'v_conv_ln_g', 'v_conv_ln_b', 'v_rel_bias', 'v_w_out', 'v_norm_mix_post', 'v_norm_ffn_pre', 'v_w_up', 'v_ffn_dw_w', 'v_ffn_dw_b', 'v_w_down', 'v_norm_ffn_post']
TWIN_OUTPUTS = ['loss', 'grad_x', 'grad_norm_mix_pre', 'grad_w_in', 'grad_conv_dw_w', 'grad_conv_dw_b', 'grad_conv_ln_g', 'grad_conv_ln_b', 'grad_rel_bias', 'grad_w_out', 'grad_norm_mix_post', 'grad_norm_ffn_pre', 'grad_w_up', 'grad_ffn_dw_w', 'grad_ffn_dw_b', 'grad_w_down', 'grad_norm_ffn_post', 'delta_norm_mix_pre', 'delta_w_in', 'delta_conv_dw_w', 'delta_conv_dw_b', 'delta_conv_ln_g', 'delta_conv_ln_b', 'delta_rel_bias', 'delta_w_out', 'delta_norm_mix_post', 'delta_norm_ffn_pre', 'delta_w_up', 'delta_ffn_dw_w', 'delta_ffn_dw_b', 'delta_w_down', 'delta_norm_ffn_post', 'new_m_norm_mix_pre', 'new_m_w_in', 'new_m_conv_dw_w', 'new_m_conv_dw_b', 'new_m_conv_ln_g', 'new_m_conv_ln_b', 'new_m_rel_bias', 'new_m_w_out', 'new_m_norm_mix_post', 'new_m_norm_ffn_pre', 'new_m_w_up', 'new_m_ffn_dw_w', 'new_m_ffn_dw_b', 'new_m_w_down', 'new_m_norm_ffn_post', 'new_v_norm_mix_pre', 'new_v_w_in', 'new_v_conv_dw_w', 'new_v_conv_dw_b', 'new_v_conv_ln_g', 'new_v_conv_ln_b', 'new_v_rel_bias', 'new_v_w_out', 'new_v_norm_mix_post', 'new_v_norm_ffn_pre', 'new_v_w_up', 'new_v_ffn_dw_w', 'new_v_ffn_dw_b', 'new_v_w_down', 'new_v_norm_ffn_post']
TWIN_LEAF_KINDS = {'loss': 'loss', 'grad_x': 'grad_x', 'grad_norm_mix_pre': 'grad_w', 'grad_w_in': 'grad_w', 'grad_conv_dw_w': 'grad_w', 'grad_conv_dw_b': 'grad_w', 'grad_conv_ln_g': 'grad_w', 'grad_conv_ln_b': 'grad_w', 'grad_rel_bias': 'grad_w', 'grad_w_out': 'grad_w', 'grad_norm_mix_post': 'grad_w', 'grad_norm_ffn_pre': 'grad_w', 'grad_w_up': 'grad_w', 'grad_ffn_dw_w': 'grad_w', 'grad_ffn_dw_b': 'grad_w', 'grad_w_down': 'grad_w', 'grad_norm_ffn_post': 'grad_w', 'delta_norm_mix_pre': 'delta_w', 'delta_w_in': 'delta_w', 'delta_conv_dw_w': 'delta_w', 'delta_conv_dw_b': 'delta_w', 'delta_conv_ln_g': 'delta_w', 'delta_conv_ln_b': 'delta_w', 'delta_rel_bias': 'delta_w', 'delta_w_out': 'delta_w', 'delta_norm_mix_post': 'delta_w', 'delta_norm_ffn_pre': 'delta_w', 'delta_w_up': 'delta_w', 'delta_ffn_dw_w': 'delta_w', 'delta_ffn_dw_b': 'delta_w', 'delta_w_down': 'delta_w', 'delta_norm_ffn_post': 'delta_w', 'new_m_norm_mix_pre': 'new_m', 'new_m_w_in': 'new_m', 'new_m_conv_dw_w': 'new_m', 'new_m_conv_dw_b': 'new_m', 'new_m_conv_ln_g': 'new_m', 'new_m_conv_ln_b': 'new_m', 'new_m_rel_bias': 'new_m', 'new_m_w_out': 'new_m', 'new_m_norm_mix_post': 'new_m', 'new_m_norm_ffn_pre': 'new_m', 'new_m_w_up': 'new_m', 'new_m_ffn_dw_w': 'new_m', 'new_m_ffn_dw_b': 'new_m', 'new_m_w_down': 'new_m', 'new_m_norm_ffn_post': 'new_m', 'new_v_norm_mix_pre': 'new_v', 'new_v_w_in': 'new_v', 'new_v_conv_dw_w': 'new_v', 'new_v_conv_dw_b': 'new_v', 'new_v_conv_ln_g': 'new_v', 'new_v_conv_ln_b': 'new_v', 'new_v_rel_bias': 'new_v', 'new_v_w_out': 'new_v', 'new_v_norm_mix_post': 'new_v', 'new_v_norm_ffn_pre': 'new_v', 'new_v_w_up': 'new_v', 'new_v_ffn_dw_w': 'new_v', 'new_v_ffn_dw_b': 'new_v', 'new_v_w_down': 'new_v', 'new_v_norm_ffn_post': 'new_v'}


def _forward(args):
    return _fwd_reference(*[args[k] for k in FWD_PARAMS])


def _output_shape():
    def fwd():
        inp = _fwd_setup_inputs(0)
        return _fwd_reference(*[inp[k] for k in FWD_PARAMS])
    out = _jax.eval_shape(fwd)
    return out.shape, out.dtype

N_MICROBATCH = 1
ADAM_LR = 0.001
ADAM_B1 = 0.9
ADAM_B2 = 0.999
ADAM_EPS = 1e-08
ADAM_WD = 0.01
ADAM_STEP = 10
PER_EXAMPLE_BATCH_AXIS = {'x': 0, 'loss_target': 0}
SHARED_INPUTS = []
_WEIGHT_DTYPES = {'norm_mix_pre': _jnp.float32, 'w_in': _jnp.float32, 'conv_dw_w': _jnp.float32, 'conv_dw_b': _jnp.float32, 'conv_ln_g': _jnp.float32, 'conv_ln_b': _jnp.float32, 'rel_bias': _jnp.float32, 'w_out': _jnp.float32, 'norm_mix_post': _jnp.float32, 'norm_ffn_pre': _jnp.float32, 'w_up': _jnp.float32, 'ffn_dw_w': _jnp.float32, 'ffn_dw_b': _jnp.float32, 'w_down': _jnp.float32, 'norm_ffn_post': _jnp.float32}
MOMENT_SCALE = {'norm_mix_pre': 7.626015e-01, 'w_in': 4.623750e-01, 'conv_dw_w': 1.396436e+00, 'conv_dw_b': 2.250283e+01, 'conv_ln_g': 8.912149e+00, 'conv_ln_b': 1.312897e+01, 'rel_bias': 1.047122e-01, 'w_out': 3.519784e+00, 'norm_mix_post': 6.508826e+01, 'norm_ffn_pre': 2.489028e+00, 'w_up': 1.056895e+00, 'ffn_dw_w': 1.323456e+00, 'ffn_dw_b': 5.063653e+00, 'w_down': 2.433597e+00, 'norm_ffn_post': 6.420749e+01}


def _to_microbatches(a, axis):
    t = _jnp.moveaxis(a, axis, 0)
    t = t.reshape((N_MICROBATCH, t.shape[0] // N_MICROBATCH) + t.shape[1:])
    return _jnp.moveaxis(t, 1, axis + 1)


def setup_inputs(seed: int = 0) -> dict:
    inp = _fwd_setup_inputs(seed)
    key = _jax.random.fold_in(_jax.random.key(seed), 7919)
    shape, _ = _output_shape()
    out = dict(inp)
    out["loss_target"] = _jax.random.normal(_jax.random.fold_in(key, 0), shape, _jnp.float32)
    for i, name in enumerate(TWIN_WEIGHTS):
        w = inp[name].astype(_jnp.float32)
        if MOMENT_SCALE is None:
            s = _jnp.sqrt(_jnp.mean(_jnp.square(w)) + 1e-30)
        else:
            s = MOMENT_SCALE[name]
        km, kv = _jax.random.split(_jax.random.fold_in(key, i + 1))
        out[name] = w
        out["m_" + name] = s * _jax.random.normal(km, w.shape, _jnp.float32)
        out["v_" + name] = (s * s) * _jax.random.uniform(kv, w.shape, _jnp.float32, 0.5, 1.5)
    if N_MICROBATCH > 1:
        for name, axis in PER_EXAMPLE_BATCH_AXIS.items():
            out[name] = _to_microbatches(out[name], axis)
    return {'x': out['x'], 'norm_mix_pre': out['norm_mix_pre'], 'w_in': out['w_in'], 'conv_dw_w': out['conv_dw_w'], 'conv_dw_b': out['conv_dw_b'], 'conv_ln_g': out['conv_ln_g'], 'conv_ln_b': out['conv_ln_b'], 'rel_bias': out['rel_bias'], 'w_out': out['w_out'], 'norm_mix_post': out['norm_mix_post'], 'norm_ffn_pre': out['norm_ffn_pre'], 'w_up': out['w_up'], 'ffn_dw_w': out['ffn_dw_w'], 'ffn_dw_b': out['ffn_dw_b'], 'w_down': out['w_down'], 'norm_ffn_post': out['norm_ffn_post'], 'loss_target': out['loss_target'], 'm_norm_mix_pre': out['m_norm_mix_pre'], 'm_w_in': out['m_w_in'], 'm_conv_dw_w': out['m_conv_dw_w'], 'm_conv_dw_b': out['m_conv_dw_b'], 'm_conv_ln_g': out['m_conv_ln_g'], 'm_conv_ln_b': out['m_conv_ln_b'], 'm_rel_bias': out['m_rel_bias'], 'm_w_out': out['m_w_out'], 'm_norm_mix_post': out['m_norm_mix_post'], 'm_norm_ffn_pre': out['m_norm_ffn_pre'], 'm_w_up': out['m_w_up'], 'm_ffn_dw_w': out['m_ffn_dw_w'], 'm_ffn_dw_b': out['m_ffn_dw_b'], 'm_w_down': out['m_w_down'], 'm_norm_ffn_post': out['m_norm_ffn_post'], 'v_norm_mix_pre': out['v_norm_mix_pre'], 'v_w_in': out['v_w_in'], 'v_conv_dw_w': out['v_conv_dw_w'], 'v_conv_dw_b': out['v_conv_dw_b'], 'v_conv_ln_g': out['v_conv_ln_g'], 'v_conv_ln_b': out['v_conv_ln_b'], 'v_rel_bias': out['v_rel_bias'], 'v_w_out': out['v_w_out'], 'v_norm_mix_post': out['v_norm_mix_post'], 'v_norm_ffn_pre': out['v_norm_ffn_pre'], 'v_w_up': out['v_w_up'], 'v_ffn_dw_w': out['v_ffn_dw_w'], 'v_ffn_dw_b': out['v_ffn_dw_b'], 'v_w_down': out['v_w_down'], 'v_norm_ffn_post': out['v_norm_ffn_post']}


def _loss(weights, diff, rest, loss_target):
    with _jax.named_scope("forward"):
        args = {**rest, TWIN_DIFF_INPUT: diff, **{k: w.astype(_WEIGHT_DTYPES[k]) for k, w in weights.items()}}
        y = _forward(args)
    with _jax.named_scope("loss_head"):
        err = _jnp.square(y.astype(_jnp.float32) - loss_target)
        return 0.5 * _jnp.sum(_jnp.mean(err, axis=-1)) if err.ndim else 0.5 * err


def _adamw(w, g, m, v):
    m = ADAM_B1 * m + (1.0 - ADAM_B1) * g
    v = ADAM_B2 * v + (1.0 - ADAM_B2) * _jnp.square(g)
    m_hat = m / (1.0 - ADAM_B1 ** ADAM_STEP)
    v_hat = v / (1.0 - ADAM_B2 ** ADAM_STEP)
    delta = -ADAM_LR * (m_hat / (_jnp.sqrt(v_hat) + ADAM_EPS) + ADAM_WD * w)
    return delta, m, v


def reference(x, norm_mix_pre, w_in, conv_dw_w, conv_dw_b, conv_ln_g, conv_ln_b, rel_bias, w_out, norm_mix_post, norm_ffn_pre, w_up, ffn_dw_w, ffn_dw_b, w_down, norm_ffn_post, loss_target, m_norm_mix_pre, m_w_in, m_conv_dw_w, m_conv_dw_b, m_conv_ln_g, m_conv_ln_b, m_rel_bias, m_w_out, m_norm_mix_post, m_norm_ffn_pre, m_w_up, m_ffn_dw_w, m_ffn_dw_b, m_w_down, m_norm_ffn_post, v_norm_mix_pre, v_w_in, v_conv_dw_w, v_conv_dw_b, v_conv_ln_g, v_conv_ln_b, v_rel_bias, v_w_out, v_norm_mix_post, v_norm_ffn_pre, v_w_up, v_ffn_dw_w, v_ffn_dw_b, v_w_down, v_norm_ffn_post):
    given = dict(x=x, norm_mix_pre=norm_mix_pre, w_in=w_in, conv_dw_w=conv_dw_w, conv_dw_b=conv_dw_b, conv_ln_g=conv_ln_g, conv_ln_b=conv_ln_b, rel_bias=rel_bias, w_out=w_out, norm_mix_post=norm_mix_post, norm_ffn_pre=norm_ffn_pre, w_up=w_up, ffn_dw_w=ffn_dw_w, ffn_dw_b=ffn_dw_b, w_down=w_down, norm_ffn_post=norm_ffn_post, loss_target=loss_target, m_norm_mix_pre=m_norm_mix_pre, m_w_in=m_w_in, m_conv_dw_w=m_conv_dw_w, m_conv_dw_b=m_conv_dw_b, m_conv_ln_g=m_conv_ln_g, m_conv_ln_b=m_conv_ln_b, m_rel_bias=m_rel_bias, m_w_out=m_w_out, m_norm_mix_post=m_norm_mix_post, m_norm_ffn_pre=m_norm_ffn_pre, m_w_up=m_w_up, m_ffn_dw_w=m_ffn_dw_w, m_ffn_dw_b=m_ffn_dw_b, m_w_down=m_w_down, m_norm_ffn_post=m_norm_ffn_post, v_norm_mix_pre=v_norm_mix_pre, v_w_in=v_w_in, v_conv_dw_w=v_conv_dw_w, v_conv_dw_b=v_conv_dw_b, v_conv_ln_g=v_conv_ln_g, v_conv_ln_b=v_conv_ln_b, v_rel_bias=v_rel_bias, v_w_out=v_w_out, v_norm_mix_post=v_norm_mix_post, v_norm_ffn_pre=v_norm_ffn_pre, v_w_up=v_w_up, v_ffn_dw_w=v_ffn_dw_w, v_ffn_dw_b=v_ffn_dw_b, v_w_down=v_w_down, v_norm_ffn_post=v_norm_ffn_post)
    weights = {n: given[n] for n in TWIN_WEIGHTS}
    shared = {n: given[n] for n in SHARED_INPUTS}
    per_example = {n: given[n] for n in ['x']}
    grad_fn = _jax.value_and_grad(_loss, argnums=(0, 1))

    def one_microbatch(ex, loss_target):
        ex = dict(ex)
        diff = ex.pop(TWIN_DIFF_INPUT)
        return grad_fn(weights, diff, {**shared, **ex}, loss_target)

    if N_MICROBATCH == 1:
        loss, (grad_w, grad_x) = one_microbatch(per_example, given["loss_target"])
    else:
        def body(carry, xs):
            loss_sum, grad_sum = carry
            l_k, (gw_k, gx_k) = one_microbatch(xs[0], xs[1])
            with _jax.named_scope("update"):
                return (loss_sum + l_k, _jax.tree.map(_jnp.add, grad_sum, gw_k)), gx_k

        init = (_jnp.zeros((), _jnp.float32), _jax.tree.map(_jnp.zeros_like, weights))
        (loss, grad_w), grad_x = _jax.lax.scan(body, init, (per_example, given["loss_target"]))
    with _jax.named_scope("update"):
        delta_w, new_m, new_v = {}, {}, {}
        for n in TWIN_WEIGHTS:
            delta_w[n], new_m[n], new_v[n] = _adamw(weights[n], grad_w[n], given["m_" + n], given["v_" + n])
    return (loss, grad_x, *[grad_w[n] for n in TWIN_WEIGHTS], *[delta_w[n] for n in TWIN_WEIGHTS],
            *[new_m[n] for n in TWIN_WEIGHTS], *[new_v[n] for n in TWIN_WEIGHTS])
```

```python
import jax
import jax.numpy as jnp
from jax import lax
from jax.experimental import pallas as pl
from jax.experimental.pallas import tpu as pltpu

F32 = jnp.float32
BF16 = jnp.bfloat16
MESH = pl.DeviceIdType.MESH
AXES = ("x", "y", "c")
N_DEV = 8

EPS = 1e-6
NEG_INF = -1e30
D_MODEL = 1024
CONV_WIDTH = 512
ATTN_WIDTH = 512
N_HEADS = 8
HEAD_DIM = 64
CHUNK = 64
LEFT = 8 * CHUNK
QBLK = 2 * CHUNK
WIN = LEFT + QBLK
CONV_K = 31
CONV_HALO = 32
FFN_K = 3
FFN_HALO = 8
D_FF = 2816
MAX_REL = 128
SCALE = HEAD_DIM ** -0.5
ADAM_LR, ADAM_B1, ADAM_B2, ADAM_EPS, ADAM_WD, ADAM_STEP = 0.001, 0.9, 0.999, 1e-08, 0.01, 10

V7X_VMEM_BYTES = 64 * 2**20
VMEM_LIMIT_BYTES = V7X_VMEM_BYTES - 8 * 2**20
LANES = 128


def _params(*sem):
    return pltpu.CompilerParams(dimension_semantics=sem or None, vmem_limit_bytes=VMEM_LIMIT_BYTES)


_DOT_DIMS = {"nn": (((1,), (0,)), ((), ())), "nt": (((1,), (1,)), ((), ())), "tn": (((0,), (0,)), ((), ()))}


def _matmul(a, b, *, mode, m, n, k, tm, tn, tk, out_dtype, name, a_m0=0, b_n0=0, b_k0=0, add=None):
    tm, tn, tk = min(tm, m), min(tn, n), min(tk, k)
    assert m % tm == 0 and n % tn == 0 and k % tk == 0, (name, m, n, k, tm, tn, tk)
    assert a_m0 % tm == 0 and b_n0 % tn == 0 and b_k0 % tk == 0, name
    am, bn, bk = a_m0 // tm, b_n0 // tn, b_k0 // tk
    gk = k // tk
    dims = _DOT_DIMS[mode]

    if mode == "tn":
        a_spec = pl.BlockSpec((tk, tm), lambda i, j, kk: (kk, i + am))
    else:
        a_spec = pl.BlockSpec((tm, tk), lambda i, j, kk: (i + am, kk))
    if mode == "nt":
        b_spec = pl.BlockSpec((tn, tk), lambda i, j, kk: (j + bn, kk + bk))
    else:
        b_spec = pl.BlockSpec((tk, tn), lambda i, j, kk: (kk + bk, j + bn))
    o_spec = pl.BlockSpec((tm, tn), lambda i, j, kk: (i, j))
    in_specs = [a_spec, b_spec]
    operands = [a, b]
    if add is not None:
        in_specs.append(o_spec)
        operands.append(add)

    def body(*refs):
        a_ref, b_ref = refs[0], refs[1]
        add_ref = refs[2] if add is not None else None
        o_ref = refs[3] if add is not None else refs[2]
        part = lax.dot_general(a_ref[...].astype(BF16), b_ref[...].astype(BF16), dims,
                               preferred_element_type=F32)

        def finish(total):
            if add_ref is not None:
                total = total + add_ref[...]
            o_ref[...] = total.astype(out_dtype)

        if gk == 1:
            finish(part)
        else:
            acc_ref = refs[-1]
            kk = pl.program_id(2)

            @pl.when(kk == 0)
            def _():
                acc_ref[...] = part

            @pl.when(kk > 0)
            def _():
                acc_ref[...] += part

            @pl.when(kk == gk - 1)
            def _():
                finish(acc_ref[...])

    return pl.pallas_call(
        body, name=name,
        grid=(m // tm, n // tn, gk),
        in_specs=in_specs, out_specs=o_spec,
        out_shape=jax.ShapeDtypeStruct((m, n), out_dtype),
        scratch_shapes=[pltpu.VMEM((tm, tn), F32)] if gk > 1 else [],
        compiler_params=_params("parallel", "parallel", "arbitrary"),
    )(*operands)


def _rms_hat(v):
    r = lax.rsqrt(jnp.mean(v * v, axis=-1, keepdims=True) + EPS)
    return v * r, r


def _rms_bwd(dn, hat, r):
    return r * (dn - hat * jnp.mean(dn * hat, axis=-1, keepdims=True))


def _sigmoid(v):
    return 1.0 / (1.0 + jnp.exp(-v))


_GELU_C = 0.7978845608028654


def _gelu_parts(v):
    v2 = v * v
    t = jnp.tanh(_GELU_C * (v + 0.044715 * v * v2))
    cdf = 0.5 * (1.0 + t)
    dcdf = 0.5 * (1.0 - t * t) * _GELU_C * (1.0 + 3.0 * 0.044715 * v2)
    return v * cdf, cdf + v * dcdf


def _row_tile(t_rows, want):
    tile = min(want, t_rows)
    assert t_rows % tile == 0
    return tile


def _pre_norm(x, g, name):
    t_rows, d = x.shape
    tm = _row_tile(t_rows, 512)

    def body(x_ref, g_ref, u_ref):
        hat, _ = _rms_hat(x_ref[...])
        u_ref[...] = (hat * g_ref[...]).astype(BF16)

    return pl.pallas_call(
        body, name=name, grid=(t_rows // tm,),
        in_specs=[pl.BlockSpec((tm, d), lambda i: (i, 0)), pl.BlockSpec((1, d), lambda i: (0, 0))],
        out_specs=pl.BlockSpec((tm, d), lambda i: (i, 0)),
        out_shape=jax.ShapeDtypeStruct((t_rows, d), BF16),
        compiler_params=_params("parallel"),
    )(x, g)


def _mid_forward(x, mixed, g_post, g_pre):
    t_rows, d = x.shape
    tm = _row_tile(t_rows, 512)

    def body(x_ref, mixed_ref, gpost_ref, gpre_ref, h1_ref, u2_ref):
        hat, _ = _rms_hat(mixed_ref[...])
        h1 = x_ref[...] + hat * gpost_ref[...]
        h1_ref[...] = h1
        hat1, _ = _rms_hat(h1)
        u2_ref[...] = (hat1 * gpre_ref[...]).astype(BF16)

    row = pl.BlockSpec((tm, d), lambda i: (i, 0))
    vec = pl.BlockSpec((1, d), lambda i: (0, 0))
    return pl.pallas_call(
        body, name="mid_forward", grid=(t_rows // tm,),
        in_specs=[row, row, vec, vec], out_specs=[row, row],
        out_shape=[jax.ShapeDtypeStruct((t_rows, d), F32), jax.ShapeDtypeStruct((t_rows, d), BF16)],
        compiler_params=_params("parallel"),
    )(x, mixed, g_post, g_pre)


def _loss_and_head_backward(h1, f, target, g_post):
    t_rows, d = h1.shape
    tm = _row_tile(t_rows, 512)
    nt = t_rows // tm

    def body(h1_ref, f_ref, tgt_ref, g_ref, loss_ref, dy_ref, df_ref, dg_ref, sq_ref):
        i = pl.program_id(0)

        @pl.when(i == 0)
        def _():
            sq_ref[...] = jnp.zeros_like(sq_ref)
            dg_ref[...] = jnp.zeros_like(dg_ref)

        g = g_ref[...]
        hat, r = _rms_hat(f_ref[...])
        err = h1_ref[...] + hat * g - tgt_ref[...]
        sq_ref[...] += jnp.sum(err * err, axis=0, keepdims=True)
        dy = err * (1.0 / d)
        dy_ref[...] = dy
        dg_ref[...] += jnp.sum(dy * hat, axis=0, keepdims=True)
        df_ref[...] = _rms_bwd(dy * g, hat, r).astype(BF16)

        @pl.when(i == nt - 1)
        def _():
            loss_ref[...] = (0.5 / d) * jnp.sum(sq_ref[...], axis=1, keepdims=True)

    row = pl.BlockSpec((tm, d), lambda i: (i, 0))
    vec = pl.BlockSpec((1, d), lambda i: (0, 0))
    return pl.pallas_call(
        body, name="loss_head_backward", grid=(nt,),
        in_specs=[row, row, row, vec],
        out_specs=[pl.BlockSpec((1, 1), lambda i: (0, 0)), row, row, vec],
        out_shape=[jax.ShapeDtypeStruct((1, 1), F32), jax.ShapeDtypeStruct((t_rows, d), F32),
                   jax.ShapeDtypeStruct((t_rows, d), BF16), jax.ShapeDtypeStruct((1, d), F32)],
        scratch_shapes=[pltpu.VMEM((1, d), F32)],
        compiler_params=_params("arbitrary"),
    )(h1, f, target, g_post)


def _mid_backward(dy, du2, h1, mixed, g_pre, g_post):
    t_rows, d = dy.shape
    tm = _row_tile(t_rows, 512)

    def body(dy_ref, du2_ref, h1_ref, mixed_ref, gpre_ref, gpost_ref, dh1_ref, dmixed_ref, dgpre_ref, dgpost_ref):
        @pl.when(pl.program_id(0) == 0)
        def _():
            dgpre_ref[...] = jnp.zeros_like(dgpre_ref)
            dgpost_ref[...] = jnp.zeros_like(dgpost_ref)

        du2 = du2_ref[...]
        hat1, r1 = _rms_hat(h1_ref[...])
        dgpre_ref[...] += jnp.sum(du2 * hat1, axis=0, keepdims=True)
        dh1 = dy_ref[...] + _rms_bwd(du2 * gpre_ref[...], hat1, r1)
        dh1_ref[...] = dh1
        hatm, rm = _rms_hat(mixed_ref[...])
        dgpost_ref[...] += jnp.sum(dh1 * hatm, axis=0, keepdims=True)
        dmixed_ref[...] = _rms_bwd(dh1 * gpost_ref[...], hatm, rm).astype(BF16)

    row = pl.BlockSpec((tm, d), lambda i: (i, 0))
    vec = pl.BlockSpec((1, d), lambda i: (0, 0))
    return pl.pallas_call(
        body, name="mid_backward", grid=(t_rows // tm,),
        in_specs=[row, row, row, row, vec, vec], out_specs=[row, row, vec, vec],
        out_shape=[jax.ShapeDtypeStruct((t_rows, d), F32), jax.ShapeDtypeStruct((t_rows, d), BF16),
                   jax.ShapeDtypeStruct((1, d), F32), jax.ShapeDtypeStruct((1, d), F32)],
        compiler_params=_params("arbitrary"),
    )(dy, du2, h1, mixed, g_pre, g_post)


def _input_backward(dh1, du1, x, g_pre):
    t_rows, d = x.shape
    tm = _row_tile(t_rows, 512)

    def body(dh1_ref, du1_ref, x_ref, g_ref, dx_ref, dg_ref):
        @pl.when(pl.program_id(0) == 0)
        def _():
            dg_ref[...] = jnp.zeros_like(dg_ref)

        du1 = du1_ref[...]
        hat, r = _rms_hat(x_ref[...])
        dg_ref[...] += jnp.sum(du1 * hat, axis=0, keepdims=True)
        dx_ref[...] = dh1_ref[...] + _rms_bwd(du1 * g_ref[...], hat, r)

    row = pl.BlockSpec((tm, d), lambda i: (i, 0))
    vec = pl.BlockSpec((1, d), lambda i: (0, 0))
    return pl.pallas_call(
        body, name="input_backward", grid=(t_rows // tm,),
        in_specs=[row, row, row, vec], out_specs=[row, vec],
        out_shape=[jax.ShapeDtypeStruct((t_rows, d), F32), jax.ShapeDtypeStruct((1, d), F32)],
        compiler_params=_params("arbitrary"),
    )(dh1, du1, x, g_pre)


CONV_STRIP = 32


def _glu(block):
    return block[:, :CONV_WIDTH] * _sigmoid(block[:, CONV_WIDTH:])


def _layer_norm_parts(c):
    mu = jnp.mean(c, axis=-1, keepdims=True)
    xc = c - mu
    r = lax.rsqrt(jnp.mean(xc * xc, axis=-1, keepdims=True) + EPS)
    return xc * r, r


def _conv_forward(proj_a, w, b, ln_g, ln_b):
    t_rows = proj_a.shape[0]
    tm = _row_tile(t_rows, 512)
    hb = tm // CONV_HALO
    cw = CONV_WIDTH

    def body(cur_ref, prev_ref, w_ref, b_ref, g_ref, beta_ref, c_ref, out_ref, hbuf):
        i = pl.program_id(0)
        hbuf[0:CONV_HALO, :] = jnp.where(i > 0, _glu(prev_ref[...]), 0.0)
        hbuf[CONV_HALO:, :] = _glu(cur_ref[...])

        def strip(s, carry):
            base = pl.multiple_of(s * CONV_STRIP, CONV_STRIP)
            v = hbuf[pl.ds(base, 2 * CONV_STRIP), :]
            acc = jnp.broadcast_to(b_ref[...], (CONV_STRIP, cw))
            off = CONV_HALO - (CONV_K - 1)
            for kk in range(CONV_K):
                acc = acc + w_ref[kk:kk + 1, :] * v[off + kk:off + kk + CONV_STRIP, :]
            c_ref[pl.ds(base, CONV_STRIP), :] = acc
            hat, _ = _layer_norm_parts(acc)
            z = hat * g_ref[...] + beta_ref[...]
            out_ref[pl.ds(base, CONV_STRIP), :] = (z * _sigmoid(z)).astype(BF16)
            return carry

        lax.fori_loop(0, tm // CONV_STRIP, strip, 0)

    vec = pl.BlockSpec((1, cw), lambda i: (0, 0))
    return pl.pallas_call(
        body, name="conv_forward", grid=(t_rows // tm,),
        in_specs=[pl.BlockSpec((tm, 2 * cw), lambda i: (i, 0)),
                  pl.BlockSpec((CONV_HALO, 2 * cw), lambda i: (jnp.maximum(i * hb - 1, 0), 0)),
                  pl.BlockSpec((CONV_K, cw), lambda i: (0, 0)), vec, vec, vec],
        out_specs=[pl.BlockSpec((tm, cw), lambda i: (i, 0)), pl.BlockSpec((tm, cw), lambda i: (i, 0))],
        out_shape=[jax.ShapeDtypeStruct((t_rows, cw), F32), jax.ShapeDtypeStruct((t_rows, cw), BF16)],
        scratch_shapes=[pltpu.VMEM((tm + CONV_HALO, cw), F32)],
        compiler_params=_params("parallel"),
    )(proj_a, proj_a, w, b, ln_g, ln_b)


def _conv_backward(dout, c, proj_a, w, ln_g, ln_b):
    t_rows = c.shape[0]
    tm = _row_tile(t_rows, 512)
    hb = tm // CONV_HALO
    nt = t_rows // tm
    last_halo = t_rows // CONV_HALO - 1
    cw = CONV_WIDTH

    def body(dout_ref, dout_next_ref, c_ref, c_next_ref, cur_ref, prev_ref, w_ref, g_ref, beta_ref,
             dproj_ref, dw_ref, db_ref, dg_ref, dbeta_ref, hbuf, dcbuf, dwacc):
        i = pl.program_id(0)

        @pl.when(i == 0)
        def _():
            dwacc[...] = jnp.zeros_like(dwacc)
            db_ref[...] = jnp.zeros_like(db_ref)
            dg_ref[...] = jnp.zeros_like(dg_ref)
            dbeta_ref[...] = jnp.zeros_like(dbeta_ref)

        def ln_swish_backward(dout_v, c_v):
            hat, r = _layer_norm_parts(c_v)
            g = g_ref[...]
            z = hat * g + beta_ref[...]
            sg = _sigmoid(z)
            dz = dout_v * (sg * (1.0 + z * (1.0 - sg)))
            dhat = dz * g
            dc = r * (dhat - jnp.mean(dhat, axis=-1, keepdims=True)
                      - hat * jnp.mean(dhat * hat, axis=-1, keepdims=True))
            return dc, dz, hat

        dc, dz, hat = ln_swish_backward(dout_ref[...], c_ref[...])
        dg_ref[...] += jnp.sum(dz * hat, axis=0, keepdims=True)
        dbeta_ref[...] += jnp.sum(dz, axis=0, keepdims=True)
        db_ref[...] += jnp.sum(dc, axis=0, keepdims=True)
        dcbuf[0:tm, :] = dc
        dc_next, _, _ = ln_swish_backward(dout_next_ref[...], c_next_ref[...])
        dcbuf[tm:, :] = jnp.where(i < nt - 1, dc_next, 0.0)

        hbuf[0:CONV_HALO, :] = jnp.where(i > 0, _glu(prev_ref[...]), 0.0)
        hbuf[CONV_HALO:, :] = _glu(cur_ref[...])

        def strip(s, carry):
            base = pl.multiple_of(s * CONV_STRIP, CONV_STRIP)
            dv = dcbuf[pl.ds(base, 2 * CONV_STRIP), :]
            hv = hbuf[pl.ds(base, 2 * CONV_STRIP), :]
            dcs = dv[0:CONV_STRIP, :]
            dh = jnp.zeros((CONV_STRIP, cw), F32)
            off = CONV_HALO - (CONV_K - 1)
            for kk in range(CONV_K):
                back = CONV_K - 1 - kk
                dh = dh + w_ref[kk:kk + 1, :] * dv[back:back + CONV_STRIP, :]
                prod = dcs * hv[off + kk:off + kk + CONV_STRIP, :]
                dwacc[kk] += jnp.sum(prod.reshape(CONV_STRIP // 8, 8, cw), axis=0)
            blk = cur_ref[pl.ds(base, CONV_STRIP), :]
            val, sg = blk[:, :cw], _sigmoid(blk[:, cw:])
            dproj_ref[pl.ds(base, CONV_STRIP), 0:cw] = (dh * sg).astype(BF16)
            dproj_ref[pl.ds(base, CONV_STRIP), cw:2 * cw] = (dh * val * sg * (1.0 - sg)).astype(BF16)
            return carry

        lax.fori_loop(0, tm // CONV_STRIP, strip, 0)

        @pl.when(i == nt - 1)
        def _():
            for kk in range(CONV_K):
                dw_ref[kk:kk + 1, :] = jnp.sum(dwacc[kk], axis=0, keepdims=True)

    vec = pl.BlockSpec((1, cw), lambda i: (0, 0))
    cur = lambda width: pl.BlockSpec((tm, width), lambda i: (i, 0))
    nxt = lambda width: pl.BlockSpec((CONV_HALO, width), lambda i: (jnp.minimum((i + 1) * hb, last_halo), 0))
    return pl.pallas_call(
        body, name="conv_backward", grid=(nt,),
        in_specs=[cur(cw), nxt(cw), cur(cw), nxt(cw), cur(2 * cw),
                  pl.BlockSpec((CONV_HALO, 2 * cw), lambda i: (jnp.maximum(i * hb - 1, 0), 0)),
                  pl.BlockSpec((CONV_K, cw), lambda i: (0, 0)), vec, vec],
        out_specs=[cur(2 * cw), pl.BlockSpec((CONV_K, cw), lambda i: (0, 0)), vec, vec, vec],
        out_shape=[jax.ShapeDtypeStruct((t_rows, 2 * cw), BF16), jax.ShapeDtypeStruct((CONV_K, cw), F32),
                   jax.ShapeDtypeStruct((1, cw), F32), jax.ShapeDtypeStruct((1, cw), F32),
                   jax.ShapeDtypeStruct((1, cw), F32)],
        scratch_shapes=[pltpu.VMEM((tm + CONV_HALO, cw), F32), pltpu.VMEM((tm + CONV_HALO, cw), F32),
                        pltpu.VMEM((CONV_K, 8, cw), F32)],
        compiler_params=_params("arbitrary"),
    )(dout, dout, c, c, proj_a, proj_a, w, ln_g, ln_b)


def _attn_load_kv(kv_hbm, k_pad, v_pad, sem, t_cols):
    k_pad[:, 0:LEFT] = jnp.zeros((ATTN_WIDTH, LEFT), BF16)
    v_pad[:, 0:LEFT] = jnp.zeros((ATTN_WIDTH, LEFT), BF16)
    ck = pltpu.make_async_copy(kv_hbm.at[pl.ds(ATTN_WIDTH, ATTN_WIDTH), :], k_pad.at[:, pl.ds(LEFT, t_cols)], sem.at[0])
    cv = pltpu.make_async_copy(kv_hbm.at[pl.ds(2 * ATTN_WIDTH, ATTN_WIDTH), :], v_pad.at[:, pl.ds(LEFT, t_cols)], sem.at[1])
    ck.start()
    cv.start()
    ck.wait()
    cv.wait()


def _attn_build_bias(tab_ref, bias_t):
    row = lax.broadcasted_iota(jnp.int32, (LANES, LANES), 0)
    lane = lax.broadcasted_iota(jnp.int32, (LANES, LANES), 1)
    upper = lane >= row
    lane64 = lax.broadcasted_iota(jnp.int32, (CHUNK, LANES), 1)
    for h in range(N_HEADS):
        far = jnp.broadcast_to(tab_ref[h:h + 1, 2 * MAX_REL:2 * MAX_REL + 1], (LANES, LANES))
        hi = jnp.broadcast_to(tab_ref[h:h + 1, MAX_REL:2 * MAX_REL], (LANES, LANES))
        lo = jnp.broadcast_to(tab_ref[h:h + 1, 0:MAX_REL], (LANES, LANES))
        hi_d = pltpu.roll(hi, 0, 1, stride=1, stride_axis=0)
        lo_d = pltpu.roll(lo, 0, 1, stride=1, stride_axis=0)
        bias_t[h, 0:WIN - 2 * LANES, :] = jnp.broadcast_to(far[0:1, :], (WIN - 2 * LANES, LANES))
        bias_t[h, WIN - 2 * LANES:WIN - LANES, :] = jnp.where(upper, far, hi_d)
        bias_t[h, WIN - LANES:WIN, :] = jnp.where(upper, hi_d, lo_d)
        bias_t[h, 0:CHUNK, :] = jnp.where(lane64 < CHUNK, bias_t[h, 0:CHUNK, :], NEG_INF)
        bias_t[h, WIN - CHUNK:WIN, :] = jnp.where(lane64 >= CHUNK, bias_t[h, WIN - CHUNK:WIN, :], NEG_INF)


def _attn_probs(k_h, q_h, bias_h, first_valid):
    s = lax.dot_general(k_h, q_h, _DOT_DIMS["tn"], preferred_element_type=F32) * SCALE + bias_h
    key = lax.broadcasted_iota(jnp.int32, s.shape, 0)
    s = jnp.where(key >= first_valid, s, NEG_INF)
    e = jnp.exp(s - jnp.max(s, axis=0, keepdims=True))
    return e * (1.0 / jnp.sum(e, axis=0, keepdims=True))


def _attn_forward(qkv_t, rel_bias):
    t_cols = qkv_t.shape[1]
    steps = t_cols // QBLK

    def body(q_ref, kv_hbm, tab_ref, o_ref, k_pad, v_pad, bias_t, sem):
        m = pl.program_id(0)

        @pl.when(m == 0)
        def _():
            _attn_build_bias(tab_ref, bias_t)
            _attn_load_kv(kv_hbm, k_pad, v_pad, sem, t_cols)

        w0 = pl.multiple_of(m * QBLK, QBLK)
        first_valid = LEFT - m * QBLK
        for h in range(N_HEADS):
            rows = slice(h * HEAD_DIM, (h + 1) * HEAD_DIM)
            k_h = k_pad[rows, pl.ds(w0, WIN)]
            v_h = v_pad[rows, pl.ds(w0, WIN)]
            p = _attn_probs(k_h, q_ref[rows, :], bias_t[h], first_valid)
            o_h = lax.dot_general(v_h, p.astype(BF16), _DOT_DIMS["nn"], preferred_element_type=F32)
            o_ref[rows, :] = o_h.astype(BF16)

    return pl.pallas_call(
        body, name="attn_forward", grid=(steps,),
        in_specs=[pl.BlockSpec((ATTN_WIDTH, QBLK), lambda m: (0, m)),
                  pl.BlockSpec(memory_space=pl.ANY),
                  pl.BlockSpec((N_HEADS, 2 * MAX_REL + 1), lambda m: (0, 0))],
        out_specs=pl.BlockSpec((ATTN_WIDTH, QBLK), lambda m: (0, m)),
        out_shape=jax.ShapeDtypeStruct((ATTN_WIDTH, t_cols), BF16),
        scratch_shapes=[pltpu.VMEM((ATTN_WIDTH, LEFT + t_cols), BF16), pltpu.VMEM((ATTN_WIDTH, LEFT + t_cols), BF16),
                        pltpu.VMEM((N_HEADS, WIN, QBLK), F32), pltpu.SemaphoreType.DMA((2,))],
        compiler_params=_params("arbitrary"),
    )(qkv_t, qkv_t, rel_bias)


def _reverse_lanes(v, flip):
    out = jnp.zeros(v.shape, F32)
    rest = v
    for _ in range(3):
        piece = rest.astype(BF16)
        out = out + lax.dot_general(piece, flip, _DOT_DIMS["nn"], preferred_element_type=F32)
        rest = rest - piece.astype(F32)
    return out


def _attn_bias_grad(dbias_t, drel_ref):
    row = lax.broadcasted_iota(jnp.int32, (LANES, LANES), 0)
    lane = lax.broadcasted_iota(jnp.int32, (LANES, LANES), 1)
    flip = (row + lane == LANES - 1).astype(BF16)
    head = lax.broadcasted_iota(jnp.int32, (N_HEADS, LANES), 0)
    lane8 = lax.broadcasted_iota(jnp.int32, (N_HEADS, LANES), 1)
    upper_rev = jnp.zeros((N_HEADS, LANES), F32)
    lower_rev = jnp.zeros((N_HEADS, LANES), F32)
    far = jnp.zeros((N_HEADS, LANES), F32)
    for h in range(N_HEADS):
        def diagonals(block):
            skew = pltpu.roll(_reverse_lanes(block, flip), 0, 1, stride=1, stride_axis=0)
            pos = jnp.sum(jnp.where(lane >= row, skew, 0.0), axis=0, keepdims=True)
            neg = jnp.sum(jnp.where(lane < row, skew, 0.0), axis=0, keepdims=True)
            return pos, neg

        pos4, neg4 = diagonals(dbias_t[h, WIN - LANES:WIN, :])
        pos3, neg3 = diagonals(dbias_t[h, WIN - 2 * LANES:WIN - LANES, :])
        far_h = jnp.sum(jnp.sum(dbias_t[h, 0:WIN - 2 * LANES, :], axis=0, keepdims=True), axis=1, keepdims=True)
        far_h = far_h + jnp.sum(pos3, axis=1, keepdims=True)
        upper_rev = jnp.where(head == h, pos4 + neg3, upper_rev)
        lower_rev = jnp.where(head == h, neg4, lower_rev)
        far = jnp.where((head == h) & (lane8 == 0), far_h, far)
    drel_ref[:, 0:LANES] = _reverse_lanes(lower_rev, flip)
    drel_ref[:, LANES:2 * LANES] = _reverse_lanes(upper_rev, flip)
    drel_ref[:, 2 * LANES:3 * LANES] = far


def _attn_backward(qkv_t, o_t, do_t, rel_bias):
    t_cols = qkv_t.shape[1]
    steps = t_cols // QBLK
    flush = LEFT // QBLK
    total = steps + flush

    def body(q_ref, o_ref, do_ref, kv_hbm, tab_ref, dq_ref, dk_ref, dv_ref, drel_ref,
             k_pad, v_pad, bias_t, dbias_t, dk_acc, dv_acc, sem):
        m = pl.program_id(0)

        @pl.when(m == 0)
        def _():
            _attn_build_bias(tab_ref, bias_t)
            _attn_load_kv(kv_hbm, k_pad, v_pad, sem, t_cols)
            dbias_t[...] = jnp.zeros_like(dbias_t)
            dk_acc[...] = jnp.zeros_like(dk_acc)
            dv_acc[...] = jnp.zeros_like(dv_acc)

        @pl.when(m < steps)
        def _():
            w0 = pl.multiple_of(m * QBLK, QBLK)
            first_valid = LEFT - m * QBLK
            for h in range(N_HEADS):
                rows = slice(h * HEAD_DIM, (h + 1) * HEAD_DIM)
                k_h = k_pad[rows, pl.ds(w0, WIN)]
                v_h = v_pad[rows, pl.ds(w0, WIN)]
                q_h = q_ref[rows, :]
                do_h = do_ref[rows, :]
                p = _attn_probs(k_h, q_h, bias_t[h], first_valid)
                dp = lax.dot_general(v_h, do_h, _DOT_DIMS["tn"], preferred_element_type=F32)
                delta = jnp.sum(do_h.astype(F32) * o_ref[rows, :].astype(F32), axis=0, keepdims=True)
                ds = p * (dp - delta)
                dbias_t[h] += ds
                ds_b = (ds * SCALE).astype(BF16)
                dq_h = lax.dot_general(k_h, ds_b, _DOT_DIMS["nn"], preferred_element_type=F32)
                dq_ref[rows, :] = dq_h.astype(BF16)
                dk_acc[rows, :] += lax.dot_general(q_h, ds_b, _DOT_DIMS["nt"], preferred_element_type=F32)
                dv_acc[rows, :] += lax.dot_general(do_h, p.astype(BF16), _DOT_DIMS["nt"], preferred_element_type=F32)

        dk_ref[...] = dk_acc[:, 0:QBLK].astype(BF16)
        dv_ref[...] = dv_acc[:, 0:QBLK].astype(BF16)
        for acc in (dk_acc, dv_acc):
            rest = acc[:, QBLK:WIN]
            acc[:, 0:LEFT] = rest
            acc[:, LEFT:WIN] = jnp.zeros((ATTN_WIDTH, QBLK), F32)

        @pl.when(m == total - 1)
        def _():
            _attn_bias_grad(dbias_t, drel_ref)

    qblk = pl.BlockSpec((ATTN_WIDTH, QBLK), lambda m: (0, jnp.minimum(m, steps - 1)))
    kblk = pl.BlockSpec((ATTN_WIDTH, QBLK), lambda m: (0, jnp.maximum(m - flush, 0)))
    dq, dk, dv, drel = pl.pallas_call(
        body, name="attn_backward", grid=(total,),
        in_specs=[qblk, qblk, qblk, pl.BlockSpec(memory_space=pl.ANY),
                  pl.BlockSpec((N_HEADS, 2 * MAX_REL + 1), lambda m: (0, 0))],
        out_specs=[qblk, kblk, kblk, pl.BlockSpec((N_HEADS, 3 * LANES), lambda m: (0, 0))],
        out_shape=[jax.ShapeDtypeStruct((ATTN_WIDTH, t_cols), BF16)] * 3
        + [jax.ShapeDtypeStruct((N_HEADS, 3 * LANES), F32)],
        scratch_shapes=[pltpu.VMEM((ATTN_WIDTH, LEFT + t_cols), BF16), pltpu.VMEM((ATTN_WIDTH, LEFT + t_cols), BF16),
                        pltpu.VMEM((N_HEADS, WIN, QBLK), F32), pltpu.VMEM((N_HEADS, WIN, QBLK), F32),
                        pltpu.VMEM((ATTN_WIDTH, WIN), F32), pltpu.VMEM((ATTN_WIDTH, WIN), F32),
                        pltpu.SemaphoreType.DMA((2,))],
        compiler_params=_params("arbitrary"),
    )(qkv_t, o_t, do_t, qkv_t, rel_bias)
    return dq, dk, dv, drel


FFN_TC = 256


def _ffn_specs(t_rows, tr):
    nj = D_FF // FFN_TC
    hb = tr // FFN_HALO
    last_halo = t_rows // FFN_HALO - 1
    cur = lambda off: pl.BlockSpec((tr, FFN_TC), lambda j, i: (i, j + off))
    prev = lambda off: pl.BlockSpec((FFN_HALO, FFN_TC), lambda j, i: (jnp.maximum(i * hb - 1, 0), j + off))
    nxt = lambda off: pl.BlockSpec((FFN_HALO, FFN_TC), lambda j, i: (jnp.minimum((i + 1) * hb, last_halo), j + off))
    wspec = lambda off: pl.BlockSpec((FFN_K, FFN_TC), lambda j, i: (0, j + off))
    bspec = lambda off: pl.BlockSpec((1, FFN_TC), lambda j, i: (0, j + off))
    return nj, cur, prev, nxt, wspec, bspec


def _ffn_conv(buf, w_ref, b_ref, rows):
    out = b_ref[...] + w_ref[2:3, :] * buf[FFN_HALO:FFN_HALO + rows, :]
    out = out + w_ref[1:2, :] * buf[FFN_HALO - 1:FFN_HALO - 1 + rows, :]
    return out + w_ref[0:1, :] * buf[FFN_HALO - 2:FFN_HALO - 2 + rows, :]


def _ffn_activation(hup, w, b):
    t_rows = hup.shape[0]
    tr = _row_tile(t_rows, 512)
    nj, cur, prev, nxt, wspec, bspec = _ffn_specs(t_rows, tr)

    def body(g_ref, gprev_ref, v_ref, vprev_ref, wg_ref, wv_ref, bg_ref, bv_ref, act_ref, gbuf, vbuf):
        i = pl.program_id(1)
        for buf, c_ref, p_ref in ((gbuf, g_ref, gprev_ref), (vbuf, v_ref, vprev_ref)):
            buf[0:FFN_HALO, :] = jnp.where(i > 0, p_ref[...], 0.0)
            buf[FFN_HALO:, :] = c_ref[...]
        gel, _ = _gelu_parts(_ffn_conv(gbuf, wg_ref, bg_ref, tr))
        act_ref[...] = (gel * _ffn_conv(vbuf, wv_ref, bv_ref, tr)).astype(BF16)

    return pl.pallas_call(
        body, name="ffn_activation", grid=(nj, t_rows // tr),
        in_specs=[cur(0), prev(0), cur(nj), prev(nj), wspec(0), wspec(nj), bspec(0), bspec(nj)],
        out_specs=cur(0),
        out_shape=jax.ShapeDtypeStruct((t_rows, D_FF), BF16),
        scratch_shapes=[pltpu.VMEM((tr + FFN_HALO, FFN_TC), F32)] * 2,
        compiler_params=_params("parallel", "parallel"),
    )(hup, hup, hup, hup, w, w, b, b)


def _ffn_backward(dact, hup, w, b):
    t_rows = hup.shape[0]
    tr = _row_tile(t_rows, 512)
    nj, cur, prev, nxt, wspec, bspec = _ffn_specs(t_rows, tr)
    ni = t_rows // tr
    ext = tr + FFN_HALO

    def body(da_ref, danext_ref, g_ref, gprev_ref, gnext_ref, v_ref, vprev_ref, vnext_ref,
             wg_ref, wv_ref, bg_ref, bv_ref,
             dhg_ref, dhv_ref, dwg_ref, dwv_ref, dbg_ref, dbv_ref, gbuf, vbuf, dgbuf, dvbuf):
        i = pl.program_id(1)

        @pl.when(i == 0)
        def _():
            for ref in (dwg_ref, dwv_ref, dbg_ref, dbv_ref):
                ref[...] = jnp.zeros_like(ref)

        for buf, c_ref, p_ref, n_ref in ((gbuf, g_ref, gprev_ref, gnext_ref), (vbuf, v_ref, vprev_ref, vnext_ref)):
            buf[0:FFN_HALO, :] = jnp.where(i > 0, p_ref[...], 0.0)
            buf[FFN_HALO:FFN_HALO + tr, :] = c_ref[...]
            buf[FFN_HALO + tr:, :] = n_ref[...]
        cg = _ffn_conv(gbuf, wg_ref, bg_ref, ext)
        cv = _ffn_conv(vbuf, wv_ref, bv_ref, ext)
        da = jnp.concatenate([da_ref[...], jnp.where(i < ni - 1, danext_ref[...], 0.0)], axis=0)
        gel, dgel = _gelu_parts(cg)
        dgbuf[...] = da * cv * dgel
        dvbuf[...] = da * gel

        for dbuf, hbuf, w_ref, dh_ref, dw_ref, db_ref in ((dgbuf, gbuf, wg_ref, dhg_ref, dwg_ref, dbg_ref),
                                                          (dvbuf, vbuf, wv_ref, dhv_ref, dwv_ref, dbv_ref)):
            dc = dbuf[0:tr, :]
            dh = w_ref[2:3, :] * dc + w_ref[1:2, :] * dbuf[1:1 + tr, :] + w_ref[0:1, :] * dbuf[2:2 + tr, :]
            dh_ref[...] = dh.astype(BF16)
            db_ref[...] += jnp.sum(dc, axis=0, keepdims=True)
            for kk in range(FFN_K):
                shifted = hbuf[FFN_HALO - 2 + kk:FFN_HALO - 2 + kk + tr, :]
                dw_ref[kk:kk + 1, :] += jnp.sum(dc * shifted, axis=0, keepdims=True)

    half = jax.ShapeDtypeStruct((t_rows, D_FF), BF16)
    return pl.pallas_call(
        body, name="ffn_backward", grid=(nj, ni),
        in_specs=[cur(0), nxt(0), cur(0), prev(0), nxt(0), cur(nj), prev(nj), nxt(nj),
                  wspec(0), wspec(nj), bspec(0), bspec(nj)],
        out_specs=[cur(0), cur(0), wspec(0), wspec(0), bspec(0), bspec(0)],
        out_shape=[half, half, jax.ShapeDtypeStruct((FFN_K, D_FF), F32), jax.ShapeDtypeStruct((FFN_K, D_FF), F32),
                   jax.ShapeDtypeStruct((1, D_FF), F32), jax.ShapeDtypeStruct((1, D_FF), F32)],
        scratch_shapes=[pltpu.VMEM((tr + 2 * FFN_HALO, FFN_TC), F32)] * 2 + [pltpu.VMEM((ext, FFN_TC), F32)] * 2,
        compiler_params=_params("parallel", "arbitrary"),
    )(dact, dact, hup, hup, hup, hup, hup, hup, w, w, b, b)


def _mesh_position():
    return lax.axis_index("x"), lax.axis_index("y"), lax.axis_index("c")


def _hbm_specs(n):
    return [pl.BlockSpec(memory_space=pl.ANY)] * n


def _all_gather(shards, name):
    n = len(shards)

    def body(*refs):
        ins, outs = refs[:n], refs[n:2 * n]
        send_sems, recv_sems, local_sems = refs[2 * n:]
        x, y, c = _mesh_position()
        me, sibling = (x, y, c), (x, y, 1 - c)
        chips = [(1 - x, y), (x, 1 - y), (1 - x, 1 - y)]

        def copy(a, slot, block, to, src=None):
            dst = outs[a].at[4 * block[0] + 2 * block[1] + block[2]]
            return pltpu.make_async_remote_copy(
                src_ref=dst if src is None else src, dst_ref=dst,
                send_sem=send_sems.at[a, slot], recv_sem=recv_sems.at[a, slot],
                device_id=to, device_id_type=MESH)

        started = []
        for a in range(n):
            mine = pltpu.make_async_copy(ins[a], outs[a].at[4 * x + 2 * y + c], local_sems.at[a])
            mine.start()
            started.append(mine)
        first = []
        for a in range(n):
            first.append(copy(a, 0, me, sibling, src=ins[a]))
            first += [copy(a, 1 + j, me, (*chip, c), src=ins[a]) for j, chip in enumerate(chips)]
        for cp in first:
            cp.start()
        passed = []
        for j, chip in enumerate(chips):
            for a in range(n):
                copy(a, 1 + j, (*chip, c), me).wait_recv()
                fwd = copy(a, 4 + j, (*chip, c), sibling)
                fwd.start()
                passed.append(fwd)
        for a in range(n):
            copy(a, 0, sibling, me).wait_recv()
            for j, chip in enumerate(chips):
                copy(a, 4 + j, (*chip, 1 - c), me).wait_recv()
        for cp in first + passed:
            cp.wait_send()
        for mine in started:
            mine.wait()

    return pl.pallas_call(
        body, name=name,
        in_specs=_hbm_specs(n), out_specs=_hbm_specs(n),
        out_shape=[jax.ShapeDtypeStruct((N_DEV,) + s.shape, s.dtype) for s in shards],
        scratch_shapes=[pltpu.SemaphoreType.DMA((n, 7)), pltpu.SemaphoreType.DMA((n, 7)),
                        pltpu.SemaphoreType.DMA((n,))],
        compiler_params=pltpu.CompilerParams(has_side_effects=True),
    )(*shards)


def _exchange_in_chip(grads):
    n = len(grads)

    def body(*refs):
        ins, outs = refs[:n], refs[n:2 * n]
        send_sems, recv_sems = refs[2 * n:]
        x, y, c = _mesh_position()
        copies = []
        for a in range(n):
            for q in range(4):
                copies.append(pltpu.make_async_remote_copy(
                    src_ref=ins[a].at[2 * q + (1 - c)], dst_ref=outs[a].at[q],
                    send_sem=send_sems.at[a, q], recv_sem=recv_sems.at[a, q],
                    device_id=(x, y, 1 - c), device_id_type=MESH))
        for cp in copies:
            cp.start()
        for cp in copies:
            cp.wait_recv()
        for cp in copies:
            cp.wait_send()

    return pl.pallas_call(
        body, name="exchange_in_chip",
        in_specs=_hbm_specs(n), out_specs=_hbm_specs(n),
        out_shape=[jax.ShapeDtypeStruct((4,) + g.shape[1:], g.dtype) for g in grads],
        scratch_shapes=[pltpu.SemaphoreType.DMA((n, 4)), pltpu.SemaphoreType.DMA((n, 4))],
        compiler_params=pltpu.CompilerParams(has_side_effects=True),
    )(*grads)


def _exchange_between_chips(partials):
    n = len(partials)

    def body(*refs):
        ins, outs = refs[:n], refs[n:2 * n]
        send_sems, recv_sems = refs[2 * n:]
        x, y, c = _mesh_position()
        chips = [(1 - x, y), (x, 1 - y), (1 - x, 1 - y)]
        copies = []
        for a in range(n):
            for j, (px, py) in enumerate(chips):
                copies.append(pltpu.make_async_remote_copy(
                    src_ref=ins[a].at[2 * px + py], dst_ref=outs[a].at[j],
                    send_sem=send_sems.at[a, j], recv_sem=recv_sems.at[a, j],
                    device_id=(px, py, c), device_id_type=MESH))
        for cp in copies:
            cp.start()
        for cp in copies:
            cp.wait_recv()
        for cp in copies:
            cp.wait_send()

    return pl.pallas_call(
        body, name="exchange_between_chips",
        in_specs=_hbm_specs(n), out_specs=_hbm_specs(n),
        out_shape=[jax.ShapeDtypeStruct((3,) + p.shape[1:], p.dtype) for p in partials],
        scratch_shapes=[pltpu.SemaphoreType.DMA((n, 3)), pltpu.SemaphoreType.DMA((n, 3))],
        compiler_params=pltpu.CompilerParams(has_side_effects=True),
    )(*partials)


def _add_in_chip(grad, received, core, name):
    _, rows, cols = grad.shape

    def body(core_ref, g_ref, r_ref, o_ref):
        o_ref[...] = g_ref[...] + r_ref[...]

    blk = (1, rows, cols)
    return pl.pallas_call(
        body, name=name,
        grid_spec=pltpu.PrefetchScalarGridSpec(
            num_scalar_prefetch=1, grid=(4,),
            in_specs=[pl.BlockSpec(blk, lambda q, core_ref: (2 * q + core_ref[0], 0, 0)),
                      pl.BlockSpec(blk, lambda q, core_ref: (q, 0, 0))],
            out_specs=pl.BlockSpec(blk, lambda q, core_ref: (q, 0, 0))),
        out_shape=jax.ShapeDtypeStruct((4, rows, cols), F32),
        compiler_params=_params("parallel"),
    )(core, grad, received)


def _add_between_chips(partial, received, chip, name):
    _, rows, cols = partial.shape

    def body(chip_ref, p_ref, r_ref, o_ref):
        o_ref[...] = ((p_ref[0] + r_ref[0]) + r_ref[1]) + r_ref[2]

    return pl.pallas_call(
        body, name=name,
        grid_spec=pltpu.PrefetchScalarGridSpec(
            num_scalar_prefetch=1, grid=(1,),
            in_specs=[pl.BlockSpec((1, rows, cols), lambda i, chip_ref: (chip_ref[0], 0, 0)),
                      pl.BlockSpec((3, rows, cols), lambda i, chip_ref: (0, 0, 0))],
            out_specs=pl.BlockSpec((rows, cols), lambda i, chip_ref: (0, 0))),
        out_shape=jax.ShapeDtypeStruct((rows, cols), F32),
        compiler_params=_params("arbitrary"),
    )(chip, partial, received)


def _sum_devices(gathered):
    _, rows, cols = gathered.shape

    def body(g_ref, o_ref):
        total = g_ref[0]
        for d in range(1, N_DEV):
            total = total + g_ref[d]
        o_ref[...] = total

    return pl.pallas_call(
        body, name="sum_small_grads",
        out_shape=jax.ShapeDtypeStruct((rows, cols), F32),
        compiler_params=_params(),
    )(gathered)


def _adamw(w, g, m, v, name):
    rows, cols = w.shape
    tr = rows
    for cand in (256, 128, 64, 32, 16, 8):
        if rows > cand and rows % cand == 0:
            tr = cand
            break

    def body(w_ref, g_ref, m_ref, v_ref, delta_ref, newm_ref, newv_ref):
        g_v = g_ref[...]
        new_m = ADAM_B1 * m_ref[...] + (1.0 - ADAM_B1) * g_v
        new_v = ADAM_B2 * v_ref[...] + (1.0 - ADAM_B2) * (g_v * g_v)
        m_hat = new_m / (1.0 - ADAM_B1 ** ADAM_STEP)
        v_hat = new_v / (1.0 - ADAM_B2 ** ADAM_STEP)
        delta_ref[...] = -ADAM_LR * (m_hat / (jnp.sqrt(v_hat) + ADAM_EPS) + ADAM_WD * w_ref[...])
        newm_ref[...] = new_m
        newv_ref[...] = new_v

    blk = pl.BlockSpec((tr, cols), lambda i: (i, 0))
    shape = jax.ShapeDtypeStruct((rows, cols), F32)
    return pl.pallas_call(
        body, name=name, grid=(rows // tr,),
        in_specs=[blk] * 4, out_specs=[blk] * 3, out_shape=[shape] * 3,
        compiler_params=_params("parallel"),
    )(w, g, m, v)


def _pack(pieces, rows):
    flat = jnp.concatenate([p.reshape(-1) for p in pieces])
    return jnp.pad(flat, (0, rows * LANES - flat.shape[0])).reshape(rows, LANES)


def _unpack(packed, shapes):
    flat = packed.reshape(-1)
    out, pos = [], 0
    for shape in shapes:
        size = 1
        for s in shape:
            size *= s
        out.append(flat[pos:pos + size].reshape(shape))
        pos += size
    return out


def _rows_for(count):
    return -(-count // (8 * LANES)) * 8


SMALL = ("norm_mix_pre", "conv_dw_b", "conv_ln_g", "conv_ln_b", "rel_bias", "norm_mix_post", "norm_ffn_pre",
         "ffn_dw_b", "norm_ffn_post")
SHARDED_SMALL = ("conv_dw_w", "ffn_dw_w")
LARGE = ("w_in", "w_out", "w_up", "w_down")
WEIGHTS = ("norm_mix_pre", "w_in", "conv_dw_w", "conv_dw_b", "conv_ln_g", "conv_ln_b", "rel_bias", "w_out",
           "norm_mix_post", "norm_ffn_pre", "w_up", "ffn_dw_w", "ffn_dw_b", "w_down", "norm_ffn_post")


def kernel(x, norm_mix_pre, w_in, conv_dw_w, conv_dw_b, conv_ln_g, conv_ln_b, rel_bias, w_out, norm_mix_post, norm_ffn_pre, w_up, ffn_dw_w, ffn_dw_b, w_down, norm_ffn_post, loss_target, m_norm_mix_pre, m_w_in, m_conv_dw_w, m_conv_dw_b, m_conv_ln_g, m_conv_ln_b, m_rel_bias, m_w_out, m_norm_mix_post, m_norm_ffn_pre, m_w_up, m_ffn_dw_w, m_ffn_dw_b, m_w_down, m_norm_ffn_post, v_norm_mix_pre, v_w_in, v_conv_dw_w, v_conv_dw_b, v_conv_ln_g, v_conv_ln_b, v_rel_bias, v_w_out, v_norm_mix_post, v_norm_ffn_pre, v_w_up, v_ffn_dw_w, v_ffn_dw_b, v_w_down, v_norm_ffn_post):
    weights = dict(norm_mix_pre=norm_mix_pre, w_in=w_in, conv_dw_w=conv_dw_w, conv_dw_b=conv_dw_b, conv_ln_g=conv_ln_g,
                   conv_ln_b=conv_ln_b, rel_bias=rel_bias, w_out=w_out, norm_mix_post=norm_mix_post,
                   norm_ffn_pre=norm_ffn_pre, w_up=w_up, ffn_dw_w=ffn_dw_w, ffn_dw_b=ffn_dw_b, w_down=w_down,
                   norm_ffn_post=norm_ffn_post)
    mom1 = dict(norm_mix_pre=m_norm_mix_pre, w_in=m_w_in, conv_dw_w=m_conv_dw_w, conv_dw_b=m_conv_dw_b,
                conv_ln_g=m_conv_ln_g, conv_ln_b=m_conv_ln_b, rel_bias=m_rel_bias, w_out=m_w_out,
                norm_mix_post=m_norm_mix_post, norm_ffn_pre=m_norm_ffn_pre, w_up=m_w_up, ffn_dw_w=m_ffn_dw_w,
                ffn_dw_b=m_ffn_dw_b, w_down=m_w_down, norm_ffn_post=m_norm_ffn_post)
    mom2 = dict(norm_mix_pre=v_norm_mix_pre, w_in=v_w_in, conv_dw_w=v_conv_dw_w, conv_dw_b=v_conv_dw_b,
                conv_ln_g=v_conv_ln_g, conv_ln_b=v_conv_ln_b, rel_bias=v_rel_bias, w_out=v_w_out,
                norm_mix_post=v_norm_mix_post, norm_ffn_pre=v_norm_ffn_pre, w_up=v_w_up, ffn_dw_w=v_ffn_dw_w,
                ffn_dw_b=v_ffn_dw_b, w_down=v_w_down, norm_ffn_post=v_norm_ffn_post)

    x2 = x[0]
    target = loss_target[0]
    t_rows = x2.shape[0]
    d = D_MODEL
    in_cols = 2 * CONV_WIDTH + 3 * ATTN_WIDTH
    my_x, my_y, my_c = _mesh_position()
    my_dev = 4 * my_x + 2 * my_y + my_c

    small_conv = _pack([conv_dw_w[0], ffn_dw_w[0]], 32)
    win_t, wout_g, wup_t, wdown_g, conv_g = _all_gather(
        [w_in[0].T.astype(BF16), w_out[0].astype(BF16), w_up[0].T.astype(BF16), w_down[0].astype(BF16), small_conv],
        "all_gather_weights")
    win_t = win_t.reshape(in_cols, d)
    wout_g = wout_g.reshape(d, d)
    wup_t = wup_t.reshape(2 * D_FF, d)
    wdown_g = wdown_g.reshape(D_FF, d)
    conv_flat = conv_g.reshape(N_DEV, 32 * LANES)
    n_cw = CONV_K * (CONV_WIDTH // N_DEV)
    conv_w_full = conv_flat[:, :n_cw].reshape(N_DEV, CONV_K, CONV_WIDTH // N_DEV).transpose(1, 0, 2).reshape(CONV_K, CONV_WIDTH)
    ffn_w_full = conv_flat[:, n_cw:].reshape(N_DEV, FFN_K, 2 * D_FF // N_DEV).transpose(1, 0, 2).reshape(FFN_K, 2 * D_FF)

    u1 = _pre_norm(x2, norm_mix_pre, "pre_norm_mix")
    proj_a = _matmul(u1, win_t, mode="nt", m=t_rows, n=2 * CONV_WIDTH, k=d, tm=1024, tn=512, tk=d,
                     out_dtype=F32, name="proj_conv")
    qkv_t = _matmul(win_t, u1, mode="nt", m=3 * ATTN_WIDTH, n=t_rows, k=d, tm=512, tn=1024, tk=d,
                    out_dtype=BF16, name="proj_qkv", a_m0=2 * CONV_WIDTH)
    conv_c, conv_out = _conv_forward(proj_a, conv_w_full, conv_dw_b, conv_ln_g, conv_ln_b)
    o_t = _attn_forward(qkv_t, rel_bias[0])
    mixed = _matmul(conv_out, wout_g, mode="nn", m=t_rows, n=d, k=CONV_WIDTH, tm=1024, tn=1024, tk=CONV_WIDTH,
                    out_dtype=F32, name="out_proj_conv")
    mixed = _matmul(o_t, wout_g, mode="tn", m=t_rows, n=d, k=ATTN_WIDTH, tm=1024, tn=1024, tk=ATTN_WIDTH,
                    out_dtype=F32, name="out_proj_attn", b_k0=CONV_WIDTH, add=mixed)
    h1, u2 = _mid_forward(x2, mixed, norm_mix_post, norm_ffn_pre)
    hup = _matmul(u2, wup_t, mode="nt", m=t_rows, n=2 * D_FF, k=d, tm=512, tn=1408, tk=d,
                  out_dtype=F32, name="ffn_up")
    act = _ffn_activation(hup, ffn_w_full, ffn_dw_b)
    f = _matmul(act, wdown_g, mode="nn", m=t_rows, n=d, k=D_FF, tm=1024, tn=1024, tk=1408,
                out_dtype=F32, name="ffn_down")
    loss, dy, df, d_norm_ffn_post = _loss_and_head_backward(h1, f, target, norm_ffn_post)

    dact = _matmul(df, wdown_g, mode="nt", m=t_rows, n=D_FF, k=d, tm=512, tn=1408, tk=d,
                   out_dtype=F32, name="ffn_down_dx")
    g_wdown = _matmul(act, df, mode="tn", m=D_FF, n=d, k=t_rows, tm=1408, tn=1024, tk=512,
                      out_dtype=F32, name="ffn_down_dw")
    dhg, dhv, dwg, dwv, dbg, dbv = _ffn_backward(dact, hup, ffn_w_full, ffn_dw_b)
    du2 = _matmul(dhg, wup_t, mode="nn", m=t_rows, n=d, k=D_FF, tm=1024, tn=1024, tk=1408,
                  out_dtype=F32, name="ffn_up_dx_gate")
    du2 = _matmul(dhv, wup_t, mode="nn", m=t_rows, n=d, k=D_FF, tm=1024, tn=1024, tk=1408,
                  out_dtype=F32, name="ffn_up_dx_value", b_k0=D_FF, add=du2)
    g_wup_t = jnp.concatenate([
        _matmul(dhg, u2, mode="tn", m=D_FF, n=d, k=t_rows, tm=1408, tn=1024, tk=512, out_dtype=F32, name="ffn_up_dw_gate"),
        _matmul(dhv, u2, mode="tn", m=D_FF, n=d, k=t_rows, tm=1408, tn=1024, tk=512, out_dtype=F32, name="ffn_up_dw_value"),
    ], axis=0)
    dh1, dmixed, d_norm_ffn_pre, d_norm_mix_post = _mid_backward(dy, du2, h1, mixed, norm_ffn_pre, norm_mix_post)
    dconv_out = _matmul(dmixed, wout_g, mode="nt", m=t_rows, n=CONV_WIDTH, k=d, tm=1024, tn=512, tk=d,
                        out_dtype=F32, name="out_proj_dx_conv")
    do_t = _matmul(wout_g, dmixed, mode="nt", m=ATTN_WIDTH, n=t_rows, k=d, tm=512, tn=1024, tk=d,
                   out_dtype=BF16, name="out_proj_dx_attn", a_m0=CONV_WIDTH)
    g_wout = jnp.concatenate([
        _matmul(conv_out, dmixed, mode="tn", m=CONV_WIDTH, n=d, k=t_rows, tm=512, tn=1024, tk=1024, out_dtype=F32, name="out_proj_dw_conv"),
        _matmul(o_t, dmixed, mode="nn", m=ATTN_WIDTH, n=d, k=t_rows, tm=512, tn=1024, tk=1024, out_dtype=F32, name="out_proj_dw_attn"),
    ], axis=0)
    dproj_a, d_conv_w, d_conv_b, d_ln_g, d_ln_b = _conv_backward(dconv_out, conv_c, proj_a, conv_w_full, conv_ln_g, conv_ln_b)
    dq_t, dk_t, dv_t, drel = _attn_backward(qkv_t, o_t, do_t, rel_bias[0])
    dqkv_t = jnp.concatenate([dq_t, dk_t, dv_t], axis=0)
    du1 = _matmul(dproj_a, win_t, mode="nn", m=t_rows, n=d, k=2 * CONV_WIDTH, tm=1024, tn=1024, tk=1024,
                  out_dtype=F32, name="proj_dx_conv")
    du1 = _matmul(dqkv_t, win_t, mode="tn", m=t_rows, n=d, k=3 * ATTN_WIDTH, tm=1024, tn=1024, tk=512,
                  out_dtype=F32, name="proj_dx_qkv", b_k0=2 * CONV_WIDTH, add=du1)
    g_win_t = jnp.concatenate([
        _matmul(dproj_a, u1, mode="tn", m=2 * CONV_WIDTH, n=d, k=t_rows, tm=512, tn=1024, tk=1024, out_dtype=F32, name="proj_dw_conv"),
        _matmul(dqkv_t, u1, mode="nn", m=3 * ATTN_WIDTH, n=d, k=t_rows, tm=512, tn=1024, tk=1024, out_dtype=F32, name="proj_dw_qkv"),
    ], axis=0)
    dx, d_norm_mix_pre = _input_backward(dh1, du1, x2, norm_mix_pre)

    small_grads = dict(norm_mix_pre=d_norm_mix_pre, conv_dw_b=d_conv_b, conv_ln_g=d_ln_g, conv_ln_b=d_ln_b,
                       rel_bias=drel[:, :2 * MAX_REL + 1], norm_mix_post=d_norm_mix_post, norm_ffn_pre=d_norm_ffn_pre,
                       ffn_dw_b=jnp.concatenate([dbg, dbv], axis=1), norm_ffn_post=d_norm_ffn_post)
    pieces = [small_grads[nm] for nm in SMALL] + [d_conv_w, jnp.concatenate([dwg, dwv], axis=1)]
    count = sum(p.size for p in pieces)
    (gathered_small,) = _all_gather([_pack(pieces, _rows_for(count))], "all_gather_small_grads")
    summed = _sum_devices(gathered_small)
    shapes = [weights[nm].shape for nm in SMALL] + [(CONV_K, CONV_WIDTH), (FFN_K, 2 * D_FF)]
    unpacked = _unpack(summed, shapes)
    grads = dict(zip(SMALL, unpacked[:len(SMALL)]))
    cw_shard, fw_shard = CONV_WIDTH // N_DEV, 2 * D_FF // N_DEV
    grads["conv_dw_w"] = lax.dynamic_slice_in_dim(unpacked[-2], my_dev * cw_shard, cw_shard, axis=1)[None]
    grads["ffn_dw_w"] = lax.dynamic_slice_in_dim(unpacked[-1], my_dev * fw_shard, fw_shard, axis=1)[None]

    full = [g_win_t.reshape(N_DEV, in_cols // N_DEV, d), g_wout.reshape(N_DEV, d // N_DEV, d),
            g_wup_t.reshape(N_DEV, 2 * D_FF // N_DEV, d), g_wdown.reshape(N_DEV, D_FF // N_DEV, d)]
    from_sibling = _exchange_in_chip(full)
    core = jnp.reshape(my_c, (1,)).astype(jnp.int32)
    chip = jnp.reshape(2 * my_x + my_y, (1,)).astype(jnp.int32)
    partials = [_add_in_chip(g, r, core, "add_in_chip_" + nm) for g, r, nm in zip(full, from_sibling, LARGE)]
    from_chips = _exchange_between_chips(partials)
    reduced = [_add_between_chips(p, r, chip, "add_between_chips_" + nm) for p, r, nm in zip(partials, from_chips, LARGE)]
    grads["w_in"] = reduced[0].T[None]
    grads["w_out"] = reduced[1][None]
    grads["w_up"] = reduced[2].T[None]
    grads["w_down"] = reduced[3][None]

    delta, new_m, new_v = {}, {}, {}
    for nm in LARGE:
        dl, nm1, nv1 = _adamw(weights[nm][0], grads[nm][0], mom1[nm][0], mom2[nm][0], "adamw_" + nm)
        delta[nm], new_m[nm], new_v[nm] = dl[None], nm1[None], nv1[None]
    small_names = SMALL + SHARDED_SMALL
    small_count = sum(weights[nm].size for nm in small_names)
    small_rows = _rows_for(small_count)
    packed = [_pack([src[nm] for nm in small_names], small_rows) for src in (weights, grads, mom1, mom2)]
    outs = _adamw(*packed, "adamw_small")
    small_shapes = [weights[nm].shape for nm in small_names]
    for store, arr in zip((delta, new_m, new_v), outs):
        store.update(zip(small_names, _unpack(arr, small_shapes)))

    total_loss = lax.psum(loss[0, 0], AXES)
    return (total_loss, dx[None], *[grads[nm] for nm in WEIGHTS], *[delta[nm] for nm in WEIGHTS],
            *[new_m[nm] for nm in WEIGHTS], *[new_v[nm] for nm in WEIGHTS])
```

```python
import jax
import jax.numpy as jnp
from jax import lax
from jax.experimental import pallas as pl
from jax.experimental.pallas import tpu as pltpu

F32 = jnp.float32
BF16 = jnp.bfloat16
MESH = pl.DeviceIdType.MESH
AXES = ("x", "y", "c")
N_DEV = 8

EPS = 1e-6
NEG_INF = -1e30
D_MODEL = 1024
CONV_WIDTH = 512
ATTN_WIDTH = 512
N_HEADS = 8
HEAD_DIM = 64
CHUNK = 64
LEFT = 8 * CHUNK
QBLK = 2 * CHUNK
WIN = LEFT + QBLK
CONV_K = 31
CONV_HALO = 32
FFN_K = 3
FFN_HALO = 8
D_FF = 2816
MAX_REL = 128
SCALE = HEAD_DIM ** -0.5
ADAM_LR, ADAM_B1, ADAM_B2, ADAM_EPS, ADAM_WD, ADAM_STEP = 0.001, 0.9, 0.999, 1e-08, 0.01, 10

V7X_VMEM_BYTES = 64 * 2**20
VMEM_LIMIT_BYTES = V7X_VMEM_BYTES - 8 * 2**20
LANES = 128


def _params(*sem):
    return pltpu.CompilerParams(dimension_semantics=sem or None, vmem_limit_bytes=VMEM_LIMIT_BYTES)


_DOT_DIMS = {"nn": (((1,), (0,)), ((), ())), "nt": (((1,), (1,)), ((), ())), "tn": (((0,), (0,)), ((), ()))}


def _matmul(a, b, *, mode, m, n, k, tm, tn, tk, out_dtype, name, a_m0=0, b_n0=0, b_k0=0, add=None,
            out_rows=None, out_m0=0, into=None):
    tm, tn, tk = min(tm, m), min(tn, n), min(tk, k)
    out_rows = m if out_rows is None else out_rows
    assert m % tm == 0 and n % tn == 0 and k % tk == 0, (name, m, n, k, tm, tn, tk)
    assert a_m0 % tm == 0 and b_n0 % tn == 0 and b_k0 % tk == 0 and out_m0 % tm == 0, name
    am, bn, bk, om = a_m0 // tm, b_n0 // tn, b_k0 // tk, out_m0 // tm
    gk = k // tk
    dims = _DOT_DIMS[mode]

    if mode == "tn":
        a_spec = pl.BlockSpec((tk, tm), lambda i, j, kk: (kk, i + am))
    else:
        a_spec = pl.BlockSpec((tm, tk), lambda i, j, kk: (i + am, kk))
    if mode == "nt":
        b_spec = pl.BlockSpec((tn, tk), lambda i, j, kk: (j + bn, kk + bk))
    else:
        b_spec = pl.BlockSpec((tk, tn), lambda i, j, kk: (kk + bk, j + bn))
    o_spec = pl.BlockSpec((tm, tn), lambda i, j, kk: (i + om, j))
    in_specs = [a_spec, b_spec]
    operands = [a, b]
    if add is not None:
        assert out_rows == m
        in_specs.append(o_spec)
        operands.append(add)
    aliases = {}
    if into is not None:
        aliases = {len(operands): 0}
        in_specs.append(pl.BlockSpec(memory_space=pl.ANY))
        operands.append(into)

    def body(*refs):
        a_ref, b_ref = refs[0], refs[1]
        add_ref = refs[2] if add is not None else None
        o_ref = refs[len(operands)]
        part = lax.dot_general(a_ref[...].astype(BF16), b_ref[...].astype(BF16), dims,
                               preferred_element_type=F32)

        def finish(total):
            if add_ref is not None:
                total = total + add_ref[...]
            o_ref[...] = total.astype(out_dtype)

        if gk == 1:
            finish(part)
        else:
            acc_ref = refs[-1]
            kk = pl.program_id(2)

            @pl.when(kk == 0)
            def _():
                acc_ref[...] = part

            @pl.when(kk > 0)
            def _():
                acc_ref[...] += part

            @pl.when(kk == gk - 1)
            def _():
                finish(acc_ref[...])

    return pl.pallas_call(
        body, name=name,
        grid=(m // tm, n // tn, gk),
        in_specs=in_specs, out_specs=o_spec,
        out_shape=jax.ShapeDtypeStruct((out_rows, n), out_dtype),
        scratch_shapes=[pltpu.VMEM((tm, tn), F32)] if gk > 1 else [],
        input_output_aliases=aliases,
        compiler_params=_params("parallel", "parallel", "arbitrary"),
    )(*operands)


def _rms_hat(v):
    r = lax.rsqrt(jnp.mean(v * v, axis=-1, keepdims=True) + EPS)
    return v * r, r


def _rms_bwd(dn, hat, r):
    return r * (dn - hat * jnp.mean(dn * hat, axis=-1, keepdims=True))


def _sigmoid(v):
    return 1.0 / (1.0 + jnp.exp(-v))


_GELU_C = 0.7978845608028654


def _gelu(v):
    return 0.5 * v * (1.0 + jnp.tanh(_GELU_C * (v + 0.044715 * v * (v * v))))


def _gelu_parts(v):
    v2 = v * v
    t = jnp.tanh(_GELU_C * (v + 0.044715 * v * v2))
    cdf = 0.5 * (1.0 + t)
    dcdf = 0.5 * (1.0 - t * t) * _GELU_C * (1.0 + 3.0 * 0.044715 * v2)
    return v * cdf, cdf + v * dcdf


def _row_tile(t_rows, want):
    tile = min(want, t_rows)
    assert t_rows % tile == 0
    return tile


def _pre_norm(x, g, name):
    t_rows, d = x.shape
    tm = _row_tile(t_rows, 512)

    def body(x_ref, g_ref, u_ref):
        hat, _ = _rms_hat(x_ref[...])
        u_ref[...] = (hat * g_ref[...]).astype(BF16)

    return pl.pallas_call(
        body, name=name, grid=(t_rows // tm,),
        in_specs=[pl.BlockSpec((tm, d), lambda i: (i, 0)), pl.BlockSpec((1, d), lambda i: (0, 0))],
        out_specs=pl.BlockSpec((tm, d), lambda i: (i, 0)),
        out_shape=jax.ShapeDtypeStruct((t_rows, d), BF16),
        compiler_params=_params("parallel"),
    )(x, g)


def _mid_forward(x, mixed, g_post, g_pre):
    t_rows, d = x.shape
    tm = _row_tile(t_rows, 512)

    def body(x_ref, mixed_ref, gpost_ref, gpre_ref, h1_ref, u2_ref):
        hat, _ = _rms_hat(mixed_ref[...])
        h1 = x_ref[...] + hat * gpost_ref[...]
        h1_ref[...] = h1
        hat1, _ = _rms_hat(h1)
        u2_ref[...] = (hat1 * gpre_ref[...]).astype(BF16)

    row = pl.BlockSpec((tm, d), lambda i: (i, 0))
    vec = pl.BlockSpec((1, d), lambda i: (0, 0))
    return pl.pallas_call(
        body, name="mid_forward", grid=(t_rows // tm,),
        in_specs=[row, row, vec, vec], out_specs=[row, row],
        out_shape=[jax.ShapeDtypeStruct((t_rows, d), F32), jax.ShapeDtypeStruct((t_rows, d), BF16)],
        compiler_params=_params("parallel"),
    )(x, mixed, g_post, g_pre)


def _loss_and_head_backward(h1, f, target, g_post):
    t_rows, d = h1.shape
    tm = _row_tile(t_rows, 512)
    nt = t_rows // tm

    def body(h1_ref, f_ref, tgt_ref, g_ref, loss_ref, dy_ref, df_ref, dg_ref, sq_ref):
        i = pl.program_id(0)

        @pl.when(i == 0)
        def _():
            sq_ref[...] = jnp.zeros_like(sq_ref)
            dg_ref[...] = jnp.zeros_like(dg_ref)

        g = g_ref[...]
        hat, r = _rms_hat(f_ref[...])
        err = h1_ref[...] + hat * g - tgt_ref[...]
        sq_ref[...] += jnp.sum(err * err, axis=0, keepdims=True)
        dy = err * (1.0 / d)
        dy_ref[...] = dy
        dg_ref[...] += jnp.sum(dy * hat, axis=0, keepdims=True)
        df_ref[...] = _rms_bwd(dy * g, hat, r).astype(BF16)

        @pl.when(i == nt - 1)
        def _():
            loss_ref[...] = (0.5 / d) * jnp.sum(sq_ref[...], axis=1, keepdims=True)

    row = pl.BlockSpec((tm, d), lambda i: (i, 0))
    vec = pl.BlockSpec((1, d), lambda i: (0, 0))
    return pl.pallas_call(
        body, name="loss_head_backward", grid=(nt,),
        in_specs=[row, row, row, vec],
        out_specs=[pl.BlockSpec((1, 1), lambda i: (0, 0)), row, row, vec],
        out_shape=[jax.ShapeDtypeStruct((1, 1), F32), jax.ShapeDtypeStruct((t_rows, d), F32),
                   jax.ShapeDtypeStruct((t_rows, d), BF16), jax.ShapeDtypeStruct((1, d), F32)],
        scratch_shapes=[pltpu.VMEM((1, d), F32)],
        compiler_params=_params("arbitrary"),
    )(h1, f, target, g_post)


def _mid_backward(dy, du2, h1, mixed, g_pre, g_post):
    t_rows, d = dy.shape
    tm = _row_tile(t_rows, 512)

    def body(dy_ref, du2_ref, h1_ref, mixed_ref, gpre_ref, gpost_ref, dh1_ref, dmixed_ref, dgpre_ref, dgpost_ref):
        @pl.when(pl.program_id(0) == 0)
        def _():
            dgpre_ref[...] = jnp.zeros_like(dgpre_ref)
            dgpost_ref[...] = jnp.zeros_like(dgpost_ref)

        du2 = du2_ref[...]
        hat1, r1 = _rms_hat(h1_ref[...])
        dgpre_ref[...] += jnp.sum(du2 * hat1, axis=0, keepdims=True)
        dh1 = dy_ref[...] + _rms_bwd(du2 * gpre_ref[...], hat1, r1)
        dh1_ref[...] = dh1
        hatm, rm = _rms_hat(mixed_ref[...])
        dgpost_ref[...] += jnp.sum(dh1 * hatm, axis=0, keepdims=True)
        dmixed_ref[...] = _rms_bwd(dh1 * gpost_ref[...], hatm, rm).astype(BF16)

    row = pl.BlockSpec((tm, d), lambda i: (i, 0))
    vec = pl.BlockSpec((1, d), lambda i: (0, 0))
    return pl.pallas_call(
        body, name="mid_backward", grid=(t_rows // tm,),
        in_specs=[row, row, row, row, vec, vec], out_specs=[row, row, vec, vec],
        out_shape=[jax.ShapeDtypeStruct((t_rows, d), F32), jax.ShapeDtypeStruct((t_rows, d), BF16),
                   jax.ShapeDtypeStruct((1, d), F32), jax.ShapeDtypeStruct((1, d), F32)],
        compiler_params=_params("arbitrary"),
    )(dy, du2, h1, mixed, g_pre, g_post)


def _input_backward(dh1, du1, x, g_pre):
    t_rows, d = x.shape
    tm = _row_tile(t_rows, 512)

    def body(dh1_ref, du1_ref, x_ref, g_ref, dx_ref, dg_ref):
        @pl.when(pl.program_id(0) == 0)
        def _():
            dg_ref[...] = jnp.zeros_like(dg_ref)

        du1 = du1_ref[...]
        hat, r = _rms_hat(x_ref[...])
        dg_ref[...] += jnp.sum(du1 * hat, axis=0, keepdims=True)
        dx_ref[...] = dh1_ref[...] + _rms_bwd(du1 * g_ref[...], hat, r)

    row = pl.BlockSpec((tm, d), lambda i: (i, 0))
    vec = pl.BlockSpec((1, d), lambda i: (0, 0))
    return pl.pallas_call(
        body, name="input_backward", grid=(t_rows // tm,),
        in_specs=[row, row, row, vec], out_specs=[row, vec],
        out_shape=[jax.ShapeDtypeStruct((t_rows, d), F32), jax.ShapeDtypeStruct((1, d), F32)],
        compiler_params=_params("arbitrary"),
    )(dh1, du1, x, g_pre)


CONV_STRIP = 32


def _glu(block):
    return block[:, :CONV_WIDTH] * _sigmoid(block[:, CONV_WIDTH:])


def _layer_norm_parts(c):
    mu = jnp.mean(c, axis=-1, keepdims=True)
    xc = c - mu
    r = lax.rsqrt(jnp.mean(xc * xc, axis=-1, keepdims=True) + EPS)
    return xc * r, r


def _conv_forward(proj_a, w, b, ln_g, ln_b):
    t_rows = proj_a.shape[0]
    tm = _row_tile(t_rows, 512)
    hb = tm // CONV_HALO
    cw = CONV_WIDTH

    def body(cur_ref, prev_ref, w_ref, b_ref, g_ref, beta_ref, c_ref, out_ref, hbuf):
        i = pl.program_id(0)
        hbuf[0:CONV_HALO, :] = jnp.where(i > 0, _glu(prev_ref[...]), 0.0)
        hbuf[CONV_HALO:, :] = _glu(cur_ref[...])

        def strip(s, carry):
            base = pl.multiple_of(s * CONV_STRIP, CONV_STRIP)
            v = hbuf[pl.ds(base, 2 * CONV_STRIP), :]
            acc = jnp.broadcast_to(b_ref[...], (CONV_STRIP, cw))
            off = CONV_HALO - (CONV_K - 1)
            for kk in range(CONV_K):
                acc = acc + w_ref[kk:kk + 1, :] * v[off + kk:off + kk + CONV_STRIP, :]
            c_ref[pl.ds(base, CONV_STRIP), :] = acc
            hat, _ = _layer_norm_parts(acc)
            z = hat * g_ref[...] + beta_ref[...]
            out_ref[pl.ds(base, CONV_STRIP), :] = (z * _sigmoid(z)).astype(BF16)
            return carry

        lax.fori_loop(0, tm // CONV_STRIP, strip, 0)

    vec = pl.BlockSpec((1, cw), lambda i: (0, 0))
    return pl.pallas_call(
        body, name="conv_forward", grid=(t_rows // tm,),
        in_specs=[pl.BlockSpec((tm, 2 * cw), lambda i: (i, 0)),
                  pl.BlockSpec((CONV_HALO, 2 * cw), lambda i: (jnp.maximum(i * hb - 1, 0), 0)),
                  pl.BlockSpec((CONV_K, cw), lambda i: (0, 0)), vec, vec, vec],
        out_specs=[pl.BlockSpec((tm, cw), lambda i: (i, 0)), pl.BlockSpec((tm, cw), lambda i: (i, 0))],
        out_shape=[jax.ShapeDtypeStruct((t_rows, cw), F32), jax.ShapeDtypeStruct((t_rows, cw), BF16)],
        scratch_shapes=[pltpu.VMEM((tm + CONV_HALO, cw), F32)],
        compiler_params=_params("parallel"),
    )(proj_a, proj_a, w, b, ln_g, ln_b)


def _conv_backward(dout, c, proj_a, w, ln_g, ln_b):
    t_rows = c.shape[0]
    tm = _row_tile(t_rows, 512)
    hb = tm // CONV_HALO
    nt = t_rows // tm
    last_halo = t_rows // CONV_HALO - 1
    cw = CONV_WIDTH

    def body(dout_ref, dout_next_ref, c_ref, c_next_ref, cur_ref, prev_ref, w_ref, g_ref, beta_ref,
             dproj_ref, dw_ref, db_ref, dg_ref, dbeta_ref, hbuf, dcbuf, dwacc):
        i = pl.program_id(0)

        @pl.when(i == 0)
        def _():
            dwacc[...] = jnp.zeros_like(dwacc)
            db_ref[...] = jnp.zeros_like(db_ref)
            dg_ref[...] = jnp.zeros_like(dg_ref)
            dbeta_ref[...] = jnp.zeros_like(dbeta_ref)

        def ln_swish_backward(dout_v, c_v):
            hat, r = _layer_norm_parts(c_v)
            g = g_ref[...]
            z = hat * g + beta_ref[...]
            sg = _sigmoid(z)
            dz = dout_v * (sg * (1.0 + z * (1.0 - sg)))
            dhat = dz * g
            dc = r * (dhat - jnp.mean(dhat, axis=-1, keepdims=True)
                      - hat * jnp.mean(dhat * hat, axis=-1, keepdims=True))
            return dc, dz, hat

        dc, dz, hat = ln_swish_backward(dout_ref[...], c_ref[...])
        dg_ref[...] += jnp.sum(dz * hat, axis=0, keepdims=True)
        dbeta_ref[...] += jnp.sum(dz, axis=0, keepdims=True)
        db_ref[...] += jnp.sum(dc, axis=0, keepdims=True)
        dcbuf[0:tm, :] = dc
        dc_next, _, _ = ln_swish_backward(dout_next_ref[...], c_next_ref[...])
        dcbuf[tm:, :] = jnp.where(i < nt - 1, dc_next, 0.0)

        hbuf[0:CONV_HALO, :] = jnp.where(i > 0, _glu(prev_ref[...]), 0.0)
        hbuf[CONV_HALO:, :] = _glu(cur_ref[...])

        def strip(s, carry):
            base = pl.multiple_of(s * CONV_STRIP, CONV_STRIP)
            dv = dcbuf[pl.ds(base, 2 * CONV_STRIP), :]
            hv = hbuf[pl.ds(base, 2 * CONV_STRIP), :]
            dcs = dv[0:CONV_STRIP, :]
            dh = jnp.zeros((CONV_STRIP, cw), F32)
            off = CONV_HALO - (CONV_K - 1)
            for kk in range(CONV_K):
                back = CONV_K - 1 - kk
                dh = dh + w_ref[kk:kk + 1, :] * dv[back:back + CONV_STRIP, :]
                prod = dcs * hv[off + kk:off + kk + CONV_STRIP, :]
                dwacc[kk] += jnp.sum(prod.reshape(CONV_STRIP // 8, 8, cw), axis=0)
            blk = cur_ref[pl.ds(base, CONV_STRIP), :]
            val, sg = blk[:, :cw], _sigmoid(blk[:, cw:])
            dproj_ref[pl.ds(base, CONV_STRIP), 0:cw] = (dh * sg).astype(BF16)
            dproj_ref[pl.ds(base, CONV_STRIP), cw:2 * cw] = (dh * val * sg * (1.0 - sg)).astype(BF16)
            return carry

        lax.fori_loop(0, tm // CONV_STRIP, strip, 0)

        @pl.when(i == nt - 1)
        def _():
            for kk in range(CONV_K):
                dw_ref[kk:kk + 1, :] = jnp.sum(dwacc[kk], axis=0, keepdims=True)

    vec = pl.BlockSpec((1, cw), lambda i: (0, 0))
    cur = lambda width: pl.BlockSpec((tm, width), lambda i: (i, 0))
    nxt = lambda width: pl.BlockSpec((CONV_HALO, width), lambda i: (jnp.minimum((i + 1) * hb, last_halo), 0))
    return pl.pallas_call(
        body, name="conv_backward", grid=(nt,),
        in_specs=[cur(cw), nxt(cw), cur(cw), nxt(cw), cur(2 * cw),
                  pl.BlockSpec((CONV_HALO, 2 * cw), lambda i: (jnp.maximum(i * hb - 1, 0), 0)),
                  pl.BlockSpec((CONV_K, cw), lambda i: (0, 0)), vec, vec],
        out_specs=[cur(2 * cw), pl.BlockSpec((CONV_K, cw), lambda i: (0, 0)), vec, vec, vec],
        out_shape=[jax.ShapeDtypeStruct((t_rows, 2 * cw), BF16), jax.ShapeDtypeStruct((CONV_K, cw), F32),
                   jax.ShapeDtypeStruct((1, cw), F32), jax.ShapeDtypeStruct((1, cw), F32),
                   jax.ShapeDtypeStruct((1, cw), F32)],
        scratch_shapes=[pltpu.VMEM((tm + CONV_HALO, cw), F32), pltpu.VMEM((tm + CONV_HALO, cw), F32),
                        pltpu.VMEM((CONV_K, 8, cw), F32)],
        compiler_params=_params("arbitrary"),
    )(dout, dout, c, c, proj_a, proj_a, w, ln_g, ln_b)


def _attn_load_kv(kv_hbm, k_pad, v_pad, sem, t_cols):
    k_pad[:, 0:LEFT] = jnp.zeros((ATTN_WIDTH, LEFT), BF16)
    v_pad[:, 0:LEFT] = jnp.zeros((ATTN_WIDTH, LEFT), BF16)
    ck = pltpu.make_async_copy(kv_hbm.at[pl.ds(ATTN_WIDTH, ATTN_WIDTH), :], k_pad.at[:, pl.ds(LEFT, t_cols)], sem.at[0])
    cv = pltpu.make_async_copy(kv_hbm.at[pl.ds(2 * ATTN_WIDTH, ATTN_WIDTH), :], v_pad.at[:, pl.ds(LEFT, t_cols)], sem.at[1])
    ck.start()
    cv.start()
    ck.wait()
    cv.wait()


def _attn_build_bias(tab_ref, bias_t):
    row = lax.broadcasted_iota(jnp.int32, (LANES, LANES), 0)
    lane = lax.broadcasted_iota(jnp.int32, (LANES, LANES), 1)
    upper = lane >= row
    lane64 = lax.broadcasted_iota(jnp.int32, (CHUNK, LANES), 1)
    for h in range(N_HEADS):
        far = jnp.broadcast_to(tab_ref[h:h + 1, 2 * MAX_REL:2 * MAX_REL + 1], (LANES, LANES))
        hi = jnp.broadcast_to(tab_ref[h:h + 1, MAX_REL:2 * MAX_REL], (LANES, LANES))
        lo = jnp.broadcast_to(tab_ref[h:h + 1, 0:MAX_REL], (LANES, LANES))
        hi_d = pltpu.roll(hi, 0, 1, stride=1, stride_axis=0)
        lo_d = pltpu.roll(lo, 0, 1, stride=1, stride_axis=0)
        bias_t[h, 0:WIN - 2 * LANES, :] = jnp.broadcast_to(far[0:1, :], (WIN - 2 * LANES, LANES))
        bias_t[h, WIN - 2 * LANES:WIN - LANES, :] = jnp.where(upper, far, hi_d)
        bias_t[h, WIN - LANES:WIN, :] = jnp.where(upper, hi_d, lo_d)
        bias_t[h, 0:CHUNK, :] = jnp.where(lane64 < CHUNK, bias_t[h, 0:CHUNK, :], NEG_INF)
        bias_t[h, WIN - CHUNK:WIN, :] = jnp.where(lane64 >= CHUNK, bias_t[h, WIN - CHUNK:WIN, :], NEG_INF)


def _attn_probs(k_h, q_h, bias_h, first_valid):
    s = lax.dot_general(k_h, q_h, _DOT_DIMS["tn"], preferred_element_type=F32) * SCALE + bias_h
    key = lax.broadcasted_iota(jnp.int32, s.shape, 0)
    s = jnp.where(key >= first_valid, s, NEG_INF)
    e = jnp.exp(s - jnp.max(s, axis=0, keepdims=True))
    return e * (1.0 / jnp.sum(e, axis=0, keepdims=True))


def _attn_forward(qkv_t, rel_bias):
    t_cols = qkv_t.shape[1]
    steps = t_cols // QBLK

    def body(q_ref, kv_hbm, tab_ref, o_ref, k_pad, v_pad, bias_t, sem):
        m = pl.program_id(0)

        @pl.when(m == 0)
        def _():
            _attn_build_bias(tab_ref, bias_t)
            _attn_load_kv(kv_hbm, k_pad, v_pad, sem, t_cols)

        w0 = pl.multiple_of(m * QBLK, QBLK)
        first_valid = LEFT - m * QBLK
        for h in range(N_HEADS):
            rows = slice(h * HEAD_DIM, (h + 1) * HEAD_DIM)
            k_h = k_pad[rows, pl.ds(w0, WIN)]
            v_h = v_pad[rows, pl.ds(w0, WIN)]
            p = _attn_probs(k_h, q_ref[rows, :], bias_t[h], first_valid)
            o_h = lax.dot_general(v_h, p.astype(BF16), _DOT_DIMS["nn"], preferred_element_type=F32)
            o_ref[rows, :] = o_h.astype(BF16)

    return pl.pallas_call(
        body, name="attn_forward", grid=(steps,),
        in_specs=[pl.BlockSpec((ATTN_WIDTH, QBLK), lambda m: (0, m)),
                  pl.BlockSpec(memory_space=pl.ANY),
                  pl.BlockSpec((N_HEADS, 2 * MAX_REL + 1), lambda m: (0, 0))],
        out_specs=pl.BlockSpec((ATTN_WIDTH, QBLK), lambda m: (0, m)),
        out_shape=jax.ShapeDtypeStruct((ATTN_WIDTH, t_cols), BF16),
        scratch_shapes=[pltpu.VMEM((ATTN_WIDTH, LEFT + t_cols), BF16), pltpu.VMEM((ATTN_WIDTH, LEFT + t_cols), BF16),
                        pltpu.VMEM((N_HEADS, WIN, QBLK), F32), pltpu.SemaphoreType.DMA((2,))],
        compiler_params=_params("arbitrary"),
    )(qkv_t, qkv_t, rel_bias)


def _reverse_lanes(v, flip):
    out = jnp.zeros(v.shape, F32)
    rest = v
    for _ in range(3):
        piece = rest.astype(BF16)
        out = out + lax.dot_general(piece, flip, _DOT_DIMS["nn"], preferred_element_type=F32)
        rest = rest - piece.astype(F32)
    return out


def _attn_bias_grad(dbias_t, drel_ref):
    row = lax.broadcasted_iota(jnp.int32, (LANES, LANES), 0)
    lane = lax.broadcasted_iota(jnp.int32, (LANES, LANES), 1)
    flip = (row + lane == LANES - 1).astype(BF16)
    head = lax.broadcasted_iota(jnp.int32, (N_HEADS, LANES), 0)
    lane8 = lax.broadcasted_iota(jnp.int32, (N_HEADS, LANES), 1)
    upper_rev = jnp.zeros((N_HEADS, LANES), F32)
    lower_rev = jnp.zeros((N_HEADS, LANES), F32)
    far = jnp.zeros((N_HEADS, LANES), F32)
    for h in range(N_HEADS):
        def diagonals(block):
            skew = pltpu.roll(_reverse_lanes(block, flip), 0, 1, stride=1, stride_axis=0)
            pos = jnp.sum(jnp.where(lane >= row, skew, 0.0), axis=0, keepdims=True)
            neg = jnp.sum(jnp.where(lane < row, skew, 0.0), axis=0, keepdims=True)
            return pos, neg

        pos4, neg4 = diagonals(dbias_t[h, WIN - LANES:WIN, :])
        pos3, neg3 = diagonals(dbias_t[h, WIN - 2 * LANES:WIN - LANES, :])
        far_h = jnp.sum(jnp.sum(dbias_t[h, 0:WIN - 2 * LANES, :], axis=0, keepdims=True), axis=1, keepdims=True)
        far_h = far_h + jnp.sum(pos3, axis=1, keepdims=True)
        upper_rev = jnp.where(head == h, pos4 + neg3, upper_rev)
        lower_rev = jnp.where(head == h, neg4, lower_rev)
        far = jnp.where((head == h) & (lane8 == 0), far_h, far)
    drel_ref[:, 0:LANES] = _reverse_lanes(lower_rev, flip)
    drel_ref[:, LANES:2 * LANES] = _reverse_lanes(upper_rev, flip)
    drel_ref[:, 2 * LANES:3 * LANES] = far


def _attn_backward(qkv_t, o_t, do_t, rel_bias):
    t_cols = qkv_t.shape[1]
    steps = t_cols // QBLK
    flush = LEFT // QBLK
    total = steps + flush

    def body(q_ref, o_ref, do_ref, kv_hbm, tab_ref, dq_ref, dk_ref, dv_ref, drel_ref,
             k_pad, v_pad, bias_t, dbias_t, dk_acc, dv_acc, sem):
        m = pl.program_id(0)

        @pl.when(m == 0)
        def _():
            _attn_build_bias(tab_ref, bias_t)
            _attn_load_kv(kv_hbm, k_pad, v_pad, sem, t_cols)
            dbias_t[...] = jnp.zeros_like(dbias_t)
            dk_acc[...] = jnp.zeros_like(dk_acc)
            dv_acc[...] = jnp.zeros_like(dv_acc)

        @pl.when(m < steps)
        def _():
            w0 = pl.multiple_of(m * QBLK, QBLK)
            first_valid = LEFT - m * QBLK
            for h in range(N_HEADS):
                rows = slice(h * HEAD_DIM, (h + 1) * HEAD_DIM)
                k_h = k_pad[rows, pl.ds(w0, WIN)]
                v_h = v_pad[rows, pl.ds(w0, WIN)]
                q_h = q_ref[rows, :]
                do_h = do_ref[rows, :]
                p = _attn_probs(k_h, q_h, bias_t[h], first_valid)
                dp = lax.dot_general(v_h, do_h, _DOT_DIMS["tn"], preferred_element_type=F32)
                delta = jnp.sum(do_h.astype(F32) * o_ref[rows, :].astype(F32), axis=0, keepdims=True)
                ds = p * (dp - delta)
                dbias_t[h] += ds
                ds_b = (ds * SCALE).astype(BF16)
                dq_h = lax.dot_general(k_h, ds_b, _DOT_DIMS["nn"], preferred_element_type=F32)
                dq_ref[rows, :] = dq_h.astype(BF16)
                dk_acc[rows, :] += lax.dot_general(q_h, ds_b, _DOT_DIMS["nt"], preferred_element_type=F32)
                dv_acc[rows, :] += lax.dot_general(do_h, p.astype(BF16), _DOT_DIMS["nt"], preferred_element_type=F32)

        dk_ref[...] = dk_acc[:, 0:QBLK].astype(BF16)
        dv_ref[...] = dv_acc[:, 0:QBLK].astype(BF16)
        for acc in (dk_acc, dv_acc):
            rest = acc[:, QBLK:WIN]
            acc[:, 0:LEFT] = rest
            acc[:, LEFT:WIN] = jnp.zeros((ATTN_WIDTH, QBLK), F32)

        @pl.when(m == total - 1)
        def _():
            _attn_bias_grad(dbias_t, drel_ref)

    qblk = pl.BlockSpec((ATTN_WIDTH, QBLK), lambda m: (0, jnp.minimum(m, steps - 1)))
    kblk = pl.BlockSpec((ATTN_WIDTH, QBLK), lambda m: (0, jnp.maximum(m - flush, 0)))
    dq, dk, dv, drel = pl.pallas_call(
        body, name="attn_backward", grid=(total,),
        in_specs=[qblk, qblk, qblk, pl.BlockSpec(memory_space=pl.ANY),
                  pl.BlockSpec((N_HEADS, 2 * MAX_REL + 1), lambda m: (0, 0))],
        out_specs=[qblk, kblk, kblk, pl.BlockSpec((N_HEADS, 3 * LANES), lambda m: (0, 0))],
        out_shape=[jax.ShapeDtypeStruct((ATTN_WIDTH, t_cols), BF16)] * 3
        + [jax.ShapeDtypeStruct((N_HEADS, 3 * LANES), F32)],
        scratch_shapes=[pltpu.VMEM((ATTN_WIDTH, LEFT + t_cols), BF16), pltpu.VMEM((ATTN_WIDTH, LEFT + t_cols), BF16),
                        pltpu.VMEM((N_HEADS, WIN, QBLK), F32), pltpu.VMEM((N_HEADS, WIN, QBLK), F32),
                        pltpu.VMEM((ATTN_WIDTH, WIN), F32), pltpu.VMEM((ATTN_WIDTH, WIN), F32),
                        pltpu.SemaphoreType.DMA((2,))],
        compiler_params=_params("arbitrary"),
    )(qkv_t, o_t, do_t, qkv_t, rel_bias)
    return dq, dk, dv, drel


FFN_TC = 256


def _ffn_specs(t_rows, tr):
    nj = D_FF // FFN_TC
    hb = tr // FFN_HALO
    last_halo = t_rows // FFN_HALO - 1
    cur = lambda off: pl.BlockSpec((tr, FFN_TC), lambda j, i: (i, j + off))
    prev = lambda off: pl.BlockSpec((FFN_HALO, FFN_TC), lambda j, i: (jnp.maximum(i * hb - 1, 0), j + off))
    nxt = lambda off: pl.BlockSpec((FFN_HALO, FFN_TC), lambda j, i: (jnp.minimum((i + 1) * hb, last_halo), j + off))
    wspec = lambda off: pl.BlockSpec((FFN_K, FFN_TC), lambda j, i: (0, j + off))
    bspec = lambda off: pl.BlockSpec((1, FFN_TC), lambda j, i: (0, j + off))
    return nj, cur, prev, nxt, wspec, bspec


FFN_STRIP = 16


def _ffn_conv(win, w, b, rows):
    out = b + w[2] * win[FFN_HALO:FFN_HALO + rows, :]
    out = out + w[1] * win[FFN_HALO - 1:FFN_HALO - 1 + rows, :]
    return out + w[0] * win[FFN_HALO - 2:FFN_HALO - 2 + rows, :]


def _taps(w_ref):
    return [w_ref[kk:kk + 1, :] for kk in range(FFN_K)]


def _ffn_first_window(prev_ref, cur_ref, tile, rows):
    return jnp.concatenate([jnp.where(tile > 0, prev_ref[...], 0.0), cur_ref[0:rows, :]], axis=0)


def _fold8(v):
    return jnp.sum(v.reshape(v.shape[0] // 8, 8, v.shape[1]), axis=0)


def _ffn_activation(hup, w, b):
    t_rows = hup.shape[0]
    tr = _row_tile(t_rows, 512)
    nj, cur, prev, nxt, wspec, bspec = _ffn_specs(t_rows, tr)
    rs = 2 * FFN_STRIP

    def body(g_ref, gprev_ref, v_ref, vprev_ref, wg_ref, wv_ref, bg_ref, bv_ref, act_ref):
        i = pl.program_id(1)
        wg, wv, bg, bv = _taps(wg_ref), _taps(wv_ref), bg_ref[...], bv_ref[...]

        def emit(base, g_win, v_win):
            act = _gelu(_ffn_conv(g_win, wg, bg, rs)) * _ffn_conv(v_win, wv, bv, rs)
            act_ref[pl.ds(base, rs), :] = act.astype(BF16)

        def strip(s, carry):
            base = pl.multiple_of(s * rs, rs)
            emit(base, g_ref[pl.ds(base - FFN_HALO, rs + FFN_HALO), :], v_ref[pl.ds(base - FFN_HALO, rs + FFN_HALO), :])
            return carry

        emit(0, _ffn_first_window(gprev_ref, g_ref, i, rs), _ffn_first_window(vprev_ref, v_ref, i, rs))
        lax.fori_loop(1, tr // rs, strip, 0)

    return pl.pallas_call(
        body, name="ffn_activation", grid=(nj, t_rows // tr),
        in_specs=[cur(0), prev(0), cur(nj), prev(nj), wspec(0), wspec(nj), bspec(0), bspec(nj)],
        out_specs=cur(0),
        out_shape=jax.ShapeDtypeStruct((t_rows, D_FF), BF16),
        compiler_params=_params("parallel", "parallel"),
    )(hup, hup, hup, hup, w, w, b, b)


def _ffn_backward(dact, hup, w, b):
    t_rows = hup.shape[0]
    tr = _row_tile(t_rows, 512)
    nj, cur, prev, nxt, wspec, bspec = _ffn_specs(t_rows, tr)
    ni = t_rows // tr
    rs = FFN_STRIP
    ns = tr // rs

    def body(da_ref, danext_ref, g_ref, gprev_ref, gnext_ref, v_ref, vprev_ref, vnext_ref,
             wg_ref, wv_ref, bg_ref, bv_ref,
             dhg_ref, dhv_ref, dwg_ref, dwv_ref, dbg_ref, dbv_ref):
        i = pl.program_id(1)

        @pl.when(i == 0)
        def _():
            for ref in (dwg_ref, dwv_ref, dbg_ref, dbv_ref):
                ref[...] = jnp.zeros_like(ref)

        wg, wv, bg, bv = _taps(wg_ref), _taps(wv_ref), bg_ref[...], bv_ref[...]

        def conv_grads(da, g_win, v_win, rows):
            gel, dgel = _gelu_parts(_ffn_conv(g_win, wg, bg, rows))
            return da * _ffn_conv(v_win, wv, bv, rows) * dgel, da * gel

        da_after = jnp.where(i < ni - 1, danext_ref[...], 0.0)
        g_after = jnp.concatenate([g_ref[tr - FFN_HALO:tr, :], gnext_ref[...]], axis=0)
        v_after = jnp.concatenate([v_ref[tr - FFN_HALO:tr, :], vnext_ref[...]], axis=0)
        dcg_after, dcv_after = conv_grads(da_after, g_after, v_after, FFN_HALO)
        zero8 = jnp.zeros((8, FFN_TC), F32)

        def strip_at(base, g_win, v_win, carry):
            dcg_after, dcv_after, dbg, dbv, dwg, dwv = carry
            dcg, dcv = conv_grads(da_ref[pl.ds(base, rs), :], g_win, v_win, rs)
            out = []
            for dc, after, taps, win, dh_ref, db, dw in ((dcg, dcg_after, wg, g_win, dhg_ref, dbg, dwg),
                                                         (dcv, dcv_after, wv, v_win, dhv_ref, dbv, dwv)):
                ext = jnp.concatenate([dc, after], axis=0)
                dh = taps[2] * dc + taps[1] * ext[1:1 + rs, :] + taps[0] * ext[2:2 + rs, :]
                dh_ref[pl.ds(base, rs), :] = dh.astype(BF16)
                db = db + _fold8(dc)
                dw = tuple(dw[kk] + _fold8(dc * win[FFN_HALO - 2 + kk:FFN_HALO - 2 + kk + rs, :]) for kk in range(FFN_K))
                out.append((dc[0:FFN_HALO, :], db, dw))
            return out[0][0], out[1][0], out[0][1], out[1][1], out[0][2], out[1][2]

        def strip(s, carry):
            base = pl.multiple_of((ns - 1 - s) * rs, rs)
            return strip_at(base, g_ref[pl.ds(base - FFN_HALO, rs + FFN_HALO), :],
                            v_ref[pl.ds(base - FFN_HALO, rs + FFN_HALO), :], carry)

        init = (dcg_after, dcv_after, zero8, zero8, (zero8,) * FFN_K, (zero8,) * FFN_K)
        carry = lax.fori_loop(0, ns - 1, strip, init)
        _, _, dbg, dbv, dwg, dwv = strip_at(0, _ffn_first_window(gprev_ref, g_ref, i, rs),
                                            _ffn_first_window(vprev_ref, v_ref, i, rs), carry)
        dbg_ref[...] += jnp.sum(dbg, axis=0, keepdims=True)
        dbv_ref[...] += jnp.sum(dbv, axis=0, keepdims=True)
        for kk in range(FFN_K):
            dwg_ref[kk:kk + 1, :] += jnp.sum(dwg[kk], axis=0, keepdims=True)
            dwv_ref[kk:kk + 1, :] += jnp.sum(dwv[kk], axis=0, keepdims=True)

    half = jax.ShapeDtypeStruct((t_rows, D_FF), BF16)
    return pl.pallas_call(
        body, name="ffn_backward", grid=(nj, ni),
        in_specs=[cur(0), nxt(0), cur(0), prev(0), nxt(0), cur(nj), prev(nj), nxt(nj),
                  wspec(0), wspec(nj), bspec(0), bspec(nj)],
        out_specs=[cur(0), cur(0), wspec(0), wspec(0), bspec(0), bspec(0)],
        out_shape=[half, half, jax.ShapeDtypeStruct((FFN_K, D_FF), F32), jax.ShapeDtypeStruct((FFN_K, D_FF), F32),
                   jax.ShapeDtypeStruct((1, D_FF), F32), jax.ShapeDtypeStruct((1, D_FF), F32)],
        compiler_params=_params("parallel", "arbitrary"),
    )(dact, dact, hup, hup, hup, hup, hup, hup, w, w, b, b)


def _mesh_position():
    return lax.axis_index("x"), lax.axis_index("y"), lax.axis_index("c")


def _hbm_specs(n):
    return [pl.BlockSpec(memory_space=pl.ANY)] * n


def _all_gather(shards, name):
    n = len(shards)

    def body(*refs):
        ins, outs = refs[:n], refs[n:2 * n]
        send_sems, recv_sems, local_sems = refs[2 * n:]
        x, y, c = _mesh_position()
        me, sibling = (x, y, c), (x, y, 1 - c)
        chips = [(1 - x, y), (x, 1 - y), (1 - x, 1 - y)]

        def copy(a, slot, block, to, src=None):
            dst = outs[a].at[4 * block[0] + 2 * block[1] + block[2]]
            return pltpu.make_async_remote_copy(
                src_ref=dst if src is None else src, dst_ref=dst,
                send_sem=send_sems.at[a, slot], recv_sem=recv_sems.at[a, slot],
                device_id=to, device_id_type=MESH)

        started = []
        for a in range(n):
            mine = pltpu.make_async_copy(ins[a], outs[a].at[4 * x + 2 * y + c], local_sems.at[a])
            mine.start()
            started.append(mine)
        first = []
        for a in range(n):
            first.append(copy(a, 0, me, sibling, src=ins[a]))
            first += [copy(a, 1 + j, me, (*chip, c), src=ins[a]) for j, chip in enumerate(chips)]
        for cp in first:
            cp.start()
        passed = []
        for j, chip in enumerate(chips):
            for a in range(n):
                copy(a, 1 + j, (*chip, c), me).wait_recv()
                fwd = copy(a, 4 + j, (*chip, c), sibling)
                fwd.start()
                passed.append(fwd)
        for a in range(n):
            copy(a, 0, sibling, me).wait_recv()
            for j, chip in enumerate(chips):
                copy(a, 4 + j, (*chip, 1 - c), me).wait_recv()
        for cp in first + passed:
            cp.wait_send()
        for mine in started:
            mine.wait()

    return pl.pallas_call(
        body, name=name,
        in_specs=_hbm_specs(n), out_specs=_hbm_specs(n),
        out_shape=[jax.ShapeDtypeStruct((N_DEV,) + s.shape, s.dtype) for s in shards],
        scratch_shapes=[pltpu.SemaphoreType.DMA((n, 7)), pltpu.SemaphoreType.DMA((n, 7)),
                        pltpu.SemaphoreType.DMA((n,))],
        compiler_params=pltpu.CompilerParams(has_side_effects=True),
    )(*shards)


def _exchange_in_chip(grads):
    n = len(grads)

    def body(*refs):
        ins, outs = refs[:n], refs[n:2 * n]
        send_sems, recv_sems = refs[2 * n:]
        x, y, c = _mesh_position()
        copies = []
        for a in range(n):
            for q in range(4):
                copies.append(pltpu.make_async_remote_copy(
                    src_ref=ins[a].at[2 * q + (1 - c)], dst_ref=outs[a].at[q],
                    send_sem=send_sems.at[a, q], recv_sem=recv_sems.at[a, q],
                    device_id=(x, y, 1 - c), device_id_type=MESH))
        for cp in copies:
            cp.start()
        for cp in copies:
            cp.wait_recv()
        for cp in copies:
            cp.wait_send()

    return pl.pallas_call(
        body, name="exchange_in_chip",
        in_specs=_hbm_specs(n), out_specs=_hbm_specs(n),
        out_shape=[jax.ShapeDtypeStruct((4,) + g.shape[1:], g.dtype) for g in grads],
        scratch_shapes=[pltpu.SemaphoreType.DMA((n, 4)), pltpu.SemaphoreType.DMA((n, 4))],
        compiler_params=pltpu.CompilerParams(has_side_effects=True),
    )(*grads)


def _exchange_between_chips(partials):
    n = len(partials)

    def body(*refs):
        ins, outs = refs[:n], refs[n:2 * n]
        send_sems, recv_sems = refs[2 * n:]
        x, y, c = _mesh_position()
        chips = [(1 - x, y), (x, 1 - y), (1 - x, 1 - y)]
        copies = []
        for a in range(n):
            for j, (px, py) in enumerate(chips):
                copies.append(pltpu.make_async_remote_copy(
                    src_ref=ins[a].at[2 * px + py], dst_ref=outs[a].at[j],
                    send_sem=send_sems.at[a, j], recv_sem=recv_sems.at[a, j],
                    device_id=(px, py, c), device_id_type=MESH))
        for cp in copies:
            cp.start()
        for cp in copies:
            cp.wait_recv()
        for cp in copies:
            cp.wait_send()

    return pl.pallas_call(
        body, name="exchange_between_chips",
        in_specs=_hbm_specs(n), out_specs=_hbm_specs(n),
        out_shape=[jax.ShapeDtypeStruct((3,) + p.shape[1:], p.dtype) for p in partials],
        scratch_shapes=[pltpu.SemaphoreType.DMA((n, 3)), pltpu.SemaphoreType.DMA((n, 3))],
        compiler_params=pltpu.CompilerParams(has_side_effects=True),
    )(*partials)


def _add_in_chip(grad, received, core, name):
    _, rows, cols = grad.shape

    def body(core_ref, g_ref, r_ref, o_ref):
        o_ref[...] = g_ref[...] + r_ref[...]

    blk = (1, rows, cols)
    return pl.pallas_call(
        body, name=name,
        grid_spec=pltpu.PrefetchScalarGridSpec(
            num_scalar_prefetch=1, grid=(4,),
            in_specs=[pl.BlockSpec(blk, lambda q, core_ref: (2 * q + core_ref[0], 0, 0)),
                      pl.BlockSpec(blk, lambda q, core_ref: (q, 0, 0))],
            out_specs=pl.BlockSpec(blk, lambda q, core_ref: (q, 0, 0))),
        out_shape=jax.ShapeDtypeStruct((4, rows, cols), F32),
        compiler_params=_params("parallel"),
    )(core, grad, received)


def _add_between_chips(partial, received, chip, name):
    _, rows, cols = partial.shape

    def body(chip_ref, p_ref, r_ref, o_ref):
        o_ref[...] = ((p_ref[0] + r_ref[0]) + r_ref[1]) + r_ref[2]

    return pl.pallas_call(
        body, name=name,
        grid_spec=pltpu.PrefetchScalarGridSpec(
            num_scalar_prefetch=1, grid=(1,),
            in_specs=[pl.BlockSpec((1, rows, cols), lambda i, chip_ref: (chip_ref[0], 0, 0)),
                      pl.BlockSpec((3, rows, cols), lambda i, chip_ref: (0, 0, 0))],
            out_specs=pl.BlockSpec((rows, cols), lambda i, chip_ref: (0, 0))),
        out_shape=jax.ShapeDtypeStruct((rows, cols), F32),
        compiler_params=_params("arbitrary"),
    )(chip, partial, received)


def _sum_devices(gathered):
    _, rows, cols = gathered.shape

    def body(g_ref, o_ref):
        total = g_ref[0]
        for d in range(1, N_DEV):
            total = total + g_ref[d]
        o_ref[...] = total

    return pl.pallas_call(
        body, name="sum_small_grads",
        out_shape=jax.ShapeDtypeStruct((rows, cols), F32),
        compiler_params=_params(),
    )(gathered)


def _adamw(w, g, m, v, name):
    rows, cols = w.shape
    tr = rows
    for cand in (256, 128, 64, 32, 16, 8):
        if rows > cand and rows % cand == 0:
            tr = cand
            break

    def body(w_ref, g_ref, m_ref, v_ref, delta_ref, newm_ref, newv_ref):
        g_v = g_ref[...]
        new_m = ADAM_B1 * m_ref[...] + (1.0 - ADAM_B1) * g_v
        new_v = ADAM_B2 * v_ref[...] + (1.0 - ADAM_B2) * (g_v * g_v)
        m_hat = new_m / (1.0 - ADAM_B1 ** ADAM_STEP)
        v_hat = new_v / (1.0 - ADAM_B2 ** ADAM_STEP)
        delta_ref[...] = -ADAM_LR * (m_hat / (jnp.sqrt(v_hat) + ADAM_EPS) + ADAM_WD * w_ref[...])
        newm_ref[...] = new_m
        newv_ref[...] = new_v

    blk = pl.BlockSpec((tr, cols), lambda i: (i, 0))
    shape = jax.ShapeDtypeStruct((rows, cols), F32)
    return pl.pallas_call(
        body, name=name, grid=(rows // tr,),
        in_specs=[blk] * 4, out_specs=[blk] * 3, out_shape=[shape] * 3,
        compiler_params=_params("parallel"),
    )(w, g, m, v)


def _pack(pieces, rows):
    flat = jnp.concatenate([p.reshape(-1) for p in pieces])
    return jnp.pad(flat, (0, rows * LANES - flat.shape[0])).reshape(rows, LANES)


def _unpack(packed, shapes):
    flat = packed.reshape(-1)
    out, pos = [], 0
    for shape in shapes:
        size = 1
        for s in shape:
            size *= s
        out.append(flat[pos:pos + size].reshape(shape))
        pos += size
    return out


def _rows_for(count):
    return -(-count // (8 * LANES)) * 8


SMALL = ("norm_mix_pre", "conv_dw_b", "conv_ln_g", "conv_ln_b", "rel_bias", "norm_mix_post", "norm_ffn_pre",
         "ffn_dw_b", "norm_ffn_post")
SHARDED_SMALL = ("conv_dw_w", "ffn_dw_w")
LARGE = ("w_in", "w_out", "w_up", "w_down")
WEIGHTS = ("norm_mix_pre", "w_in", "conv_dw_w", "conv_dw_b", "conv_ln_g", "conv_ln_b", "rel_bias", "w_out",
           "norm_mix_post", "norm_ffn_pre", "w_up", "ffn_dw_w", "ffn_dw_b", "w_down", "norm_ffn_post")


def kernel(x, norm_mix_pre, w_in, conv_dw_w, conv_dw_b, conv_ln_g, conv_ln_b, rel_bias, w_out, norm_mix_post, norm_ffn_pre, w_up, ffn_dw_w, ffn_dw_b, w_down, norm_ffn_post, loss_target, m_norm_mix_pre, m_w_in, m_conv_dw_w, m_conv_dw_b, m_conv_ln_g, m_conv_ln_b, m_rel_bias, m_w_out, m_norm_mix_post, m_norm_ffn_pre, m_w_up, m_ffn_dw_w, m_ffn_dw_b, m_w_down, m_norm_ffn_post, v_norm_mix_pre, v_w_in, v_conv_dw_w, v_conv_dw_b, v_conv_ln_g, v_conv_ln_b, v_rel_bias, v_w_out, v_norm_mix_post, v_norm_ffn_pre, v_w_up, v_ffn_dw_w, v_ffn_dw_b, v_w_down, v_norm_ffn_post):
    weights = dict(norm_mix_pre=norm_mix_pre, w_in=w_in, conv_dw_w=conv_dw_w, conv_dw_b=conv_dw_b, conv_ln_g=conv_ln_g,
                   conv_ln_b=conv_ln_b, rel_bias=rel_bias, w_out=w_out, norm_mix_post=norm_mix_post,
                   norm_ffn_pre=norm_ffn_pre, w_up=w_up, ffn_dw_w=ffn_dw_w, ffn_dw_b=ffn_dw_b, w_down=w_down,
                   norm_ffn_post=norm_ffn_post)
    mom1 = dict(norm_mix_pre=m_norm_mix_pre, w_in=m_w_in, conv_dw_w=m_conv_dw_w, conv_dw_b=m_conv_dw_b,
                conv_ln_g=m_conv_ln_g, conv_ln_b=m_conv_ln_b, rel_bias=m_rel_bias, w_out=m_w_out,
                norm_mix_post=m_norm_mix_post, norm_ffn_pre=m_norm_ffn_pre, w_up=m_w_up, ffn_dw_w=m_ffn_dw_w,
                ffn_dw_b=m_ffn_dw_b, w_down=m_w_down, norm_ffn_post=m_norm_ffn_post)
    mom2 = dict(norm_mix_pre=v_norm_mix_pre, w_in=v_w_in, conv_dw_w=v_conv_dw_w, conv_dw_b=v_conv_dw_b,
                conv_ln_g=v_conv_ln_g, conv_ln_b=v_conv_ln_b, rel_bias=v_rel_bias, w_out=v_w_out,
                norm_mix_post=v_norm_mix_post, norm_ffn_pre=v_norm_ffn_pre, w_up=v_w_up, ffn_dw_w=v_ffn_dw_w,
                ffn_dw_b=v_ffn_dw_b, w_down=v_w_down, norm_ffn_post=v_norm_ffn_post)

    x2 = x[0]
    target = loss_target[0]
    t_rows = x2.shape[0]
    d = D_MODEL
    in_cols = 2 * CONV_WIDTH + 3 * ATTN_WIDTH
    my_x, my_y, my_c = _mesh_position()
    my_dev = 4 * my_x + 2 * my_y + my_c

    small_conv = _pack([conv_dw_w[0], ffn_dw_w[0]], 32)
    win_t, wout_g, wup_t, wdown_g, conv_g = _all_gather(
        [w_in[0].T.astype(BF16), w_out[0].astype(BF16), w_up[0].T.astype(BF16), w_down[0].astype(BF16), small_conv],
        "all_gather_weights")
    win_t = win_t.reshape(in_cols, d)
    wout_g = wout_g.reshape(d, d)
    wup_t = wup_t.reshape(2 * D_FF, d)
    wdown_g = wdown_g.reshape(D_FF, d)
    conv_flat = conv_g.reshape(N_DEV, 32 * LANES)
    n_cw = CONV_K * (CONV_WIDTH // N_DEV)
    conv_w_full = conv_flat[:, :n_cw].reshape(N_DEV, CONV_K, CONV_WIDTH // N_DEV).transpose(1, 0, 2).reshape(CONV_K, CONV_WIDTH)
    ffn_w_full = conv_flat[:, n_cw:].reshape(N_DEV, FFN_K, 2 * D_FF // N_DEV).transpose(1, 0, 2).reshape(FFN_K, 2 * D_FF)

    u1 = _pre_norm(x2, norm_mix_pre, "pre_norm_mix")
    proj_a = _matmul(u1, win_t, mode="nt", m=t_rows, n=2 * CONV_WIDTH, k=d, tm=2048, tn=1024, tk=d,
                     out_dtype=F32, name="proj_conv")
    qkv_t = _matmul(win_t, u1, mode="nt", m=3 * ATTN_WIDTH, n=t_rows, k=d, tm=512, tn=2048, tk=d,
                    out_dtype=BF16, name="proj_qkv", a_m0=2 * CONV_WIDTH)
    conv_c, conv_out = _conv_forward(proj_a, conv_w_full, conv_dw_b, conv_ln_g, conv_ln_b)
    o_t = _attn_forward(qkv_t, rel_bias[0])
    mixed = _matmul(conv_out, wout_g, mode="nn", m=t_rows, n=d, k=CONV_WIDTH, tm=2048, tn=1024, tk=CONV_WIDTH,
                    out_dtype=F32, name="out_proj_conv")
    mixed = _matmul(o_t, wout_g, mode="tn", m=t_rows, n=d, k=ATTN_WIDTH, tm=1024, tn=1024, tk=ATTN_WIDTH,
                    out_dtype=F32, name="out_proj_attn", b_k0=CONV_WIDTH, add=mixed)
    h1, u2 = _mid_forward(x2, mixed, norm_mix_post, norm_ffn_pre)
    hup = _matmul(u2, wup_t, mode="nt", m=t_rows, n=2 * D_FF, k=d, tm=2048, tn=1408, tk=d,
                  out_dtype=F32, name="ffn_up")
    act = _ffn_activation(hup, ffn_w_full, ffn_dw_b)
    f = _matmul(act, wdown_g, mode="nn", m=t_rows, n=d, k=D_FF, tm=1024, tn=1024, tk=D_FF,
                out_dtype=F32, name="ffn_down")
    loss, dy, df, d_norm_ffn_post = _loss_and_head_backward(h1, f, target, norm_ffn_post)

    dact = _matmul(df, wdown_g, mode="nt", m=t_rows, n=D_FF, k=d, tm=2048, tn=1408, tk=d,
                   out_dtype=F32, name="ffn_down_dx")
    g_wdown = _matmul(act, df, mode="tn", m=D_FF, n=d, k=t_rows, tm=1408, tn=1024, tk=2048,
                      out_dtype=F32, name="ffn_down_dw")
    dhg, dhv, dwg, dwv, dbg, dbv = _ffn_backward(dact, hup, ffn_w_full, ffn_dw_b)
    du2 = _matmul(dhg, wup_t, mode="nn", m=t_rows, n=d, k=D_FF, tm=1024, tn=1024, tk=D_FF,
                  out_dtype=F32, name="ffn_up_dx_gate")
    du2 = _matmul(dhv, wup_t, mode="nn", m=t_rows, n=d, k=D_FF, tm=1024, tn=1024, tk=D_FF,
                  out_dtype=F32, name="ffn_up_dx_value", b_k0=D_FF, add=du2)
    g_wup_t = _matmul(dhg, u2, mode="tn", m=D_FF, n=d, k=t_rows, tm=1408, tn=1024, tk=2048, out_dtype=F32,
                      name="ffn_up_dw_gate", out_rows=2 * D_FF)
    g_wup_t = _matmul(dhv, u2, mode="tn", m=D_FF, n=d, k=t_rows, tm=1408, tn=1024, tk=2048, out_dtype=F32,
                      name="ffn_up_dw_value", out_rows=2 * D_FF, out_m0=D_FF, into=g_wup_t)
    dh1, dmixed, d_norm_ffn_pre, d_norm_mix_post = _mid_backward(dy, du2, h1, mixed, norm_ffn_pre, norm_mix_post)
    dconv_out = _matmul(dmixed, wout_g, mode="nt", m=t_rows, n=CONV_WIDTH, k=d, tm=2048, tn=512, tk=d,
                        out_dtype=F32, name="out_proj_dx_conv")
    do_t = _matmul(wout_g, dmixed, mode="nt", m=ATTN_WIDTH, n=t_rows, k=d, tm=512, tn=2048, tk=d,
                   out_dtype=BF16, name="out_proj_dx_attn", a_m0=CONV_WIDTH)
    g_wout = _matmul(conv_out, dmixed, mode="tn", m=CONV_WIDTH, n=d, k=t_rows, tm=512, tn=1024, tk=2048, out_dtype=F32,
                     name="out_proj_dw_conv", out_rows=d)
    g_wout = _matmul(o_t, dmixed, mode="nn", m=ATTN_WIDTH, n=d, k=t_rows, tm=512, tn=1024, tk=2048, out_dtype=F32,
                     name="out_proj_dw_attn", out_rows=d, out_m0=CONV_WIDTH, into=g_wout)
    dproj_a, d_conv_w, d_conv_b, d_ln_g, d_ln_b = _conv_backward(dconv_out, conv_c, proj_a, conv_w_full, conv_ln_g, conv_ln_b)
    dqkv_parts = _attn_backward(qkv_t, o_t, do_t, rel_bias[0])
    drel = dqkv_parts[3]
    du1 = _matmul(dproj_a, win_t, mode="nn", m=t_rows, n=d, k=2 * CONV_WIDTH, tm=2048, tn=1024, tk=1024,
                  out_dtype=F32, name="proj_dx_conv")
    g_win_t = _matmul(dproj_a, u1, mode="tn", m=2 * CONV_WIDTH, n=d, k=t_rows, tm=1024, tn=1024, tk=2048, out_dtype=F32,
                      name="proj_dw_conv", out_rows=in_cols)
    for j, part in enumerate("qkv"):
        row0 = 2 * CONV_WIDTH + j * ATTN_WIDTH
        du1 = _matmul(dqkv_parts[j], win_t, mode="tn", m=t_rows, n=d, k=ATTN_WIDTH, tm=1024, tn=1024, tk=ATTN_WIDTH,
                      out_dtype=F32, name="proj_dx_" + part, b_k0=row0, add=du1)
        g_win_t = _matmul(dqkv_parts[j], u1, mode="nn", m=ATTN_WIDTH, n=d, k=t_rows, tm=512, tn=1024, tk=2048,
                          out_dtype=F32, name="proj_dw_" + part, out_rows=in_cols, out_m0=row0, into=g_win_t)
    dx, d_norm_mix_pre = _input_backward(dh1, du1, x2, norm_mix_pre)

    small_grads = dict(norm_mix_pre=d_norm_mix_pre, conv_dw_b=d_conv_b, conv_ln_g=d_ln_g, conv_ln_b=d_ln_b,
                       rel_bias=drel[:, :2 * MAX_REL + 1], norm_mix_post=d_norm_mix_post, norm_ffn_pre=d_norm_ffn_pre,
                       ffn_dw_b=jnp.concatenate([dbg, dbv], axis=1), norm_ffn_post=d_norm_ffn_post)
    pieces = [small_grads[nm] for nm in SMALL] + [d_conv_w, jnp.concatenate([dwg, dwv], axis=1)]
    count = sum(p.size for p in pieces)
    (gathered_small,) = _all_gather([_pack(pieces, _rows_for(count))], "all_gather_small_grads")
    summed = _sum_devices(gathered_small)
    shapes = [weights[nm].shape for nm in SMALL] + [(CONV_K, CONV_WIDTH), (FFN_K, 2 * D_FF)]
    unpacked = _unpack(summed, shapes)
    grads = dict(zip(SMALL, unpacked[:len(SMALL)]))
    cw_shard, fw_shard = CONV_WIDTH // N_DEV, 2 * D_FF // N_DEV
    grads["conv_dw_w"] = lax.dynamic_slice_in_dim(unpacked[-2], my_dev * cw_shard, cw_shard, axis=1)[None]
    grads["ffn_dw_w"] = lax.dynamic_slice_in_dim(unpacked[-1], my_dev * fw_shard, fw_shard, axis=1)[None]

    full = [g_win_t.reshape(N_DEV, in_cols // N_DEV, d), g_wout.reshape(N_DEV, d // N_DEV, d),
            g_wup_t.reshape(N_DEV, 2 * D_FF // N_DEV, d), g_wdown.reshape(N_DEV, D_FF // N_DEV, d)]
    from_sibling = _exchange_in_chip(full)
    core = jnp.reshape(my_c, (1,)).astype(jnp.int32)
    chip = jnp.reshape(2 * my_x + my_y, (1,)).astype(jnp.int32)
    partials = [_add_in_chip(g, r, core, "add_in_chip_" + nm) for g, r, nm in zip(full, from_sibling, LARGE)]
    from_chips = _exchange_between_chips(partials)
    reduced = [_add_between_chips(p, r, chip, "add_between_chips_" + nm) for p, r, nm in zip(partials, from_chips, LARGE)]
    grads["w_in"] = reduced[0].T[None]
    grads["w_out"] = reduced[1][None]
    grads["w_up"] = reduced[2].T[None]
    grads["w_down"] = reduced[3][None]

    delta, new_m, new_v = {}, {}, {}
    for nm in LARGE:
        dl, nm1, nv1 = _adamw(weights[nm][0], grads[nm][0], mom1[nm][0], mom2[nm][0], "adamw_" + nm)
        delta[nm], new_m[nm], new_v[nm] = dl[None], nm1[None], nv1[None]
    small_names = SMALL + SHARDED_SMALL
    small_count = sum(weights[nm].size for nm in small_names)
    small_rows = _rows_for(small_count)
    packed = [_pack([src[nm] for nm in small_names], small_rows) for src in (weights, grads, mom1, mom2)]
    outs = _adamw(*packed, "adamw_small")
    small_shapes = [weights[nm].shape for nm in small_names]
    for store, arr in zip((delta, new_m, new_v), outs):
        store.update(zip(small_names, _unpack(arr, small_shapes)))

    total_loss = lax.psum(loss[0, 0], AXES)
    return (total_loss, dx[None], *[grads[nm] for nm in WEIGHTS], *[delta[nm] for nm in WEIGHTS],
            *[new_m[nm] for nm in WEIGHTS], *[new_v[nm] for nm in WEIGHTS])
```

```python
import jax
import jax.numpy as jnp
from jax import lax
from jax.experimental import pallas as pl
from jax.experimental.pallas import tpu as pltpu

F32 = jnp.float32
BF16 = jnp.bfloat16
MESH = pl.DeviceIdType.MESH
AXES = ("x", "y", "c")
N_DEV = 8

EPS = 1e-6
NEG_INF = -1e30
D_MODEL = 1024
CONV_WIDTH = 512
ATTN_WIDTH = 512
N_HEADS = 8
HEAD_DIM = 64
CHUNK = 64
LEFT = 8 * CHUNK
QBLK = 2 * CHUNK
WIN = LEFT + QBLK
CONV_K = 31
CONV_HALO = 32
FFN_K = 3
FFN_HALO = 8
D_FF = 2816
MAX_REL = 128
SCALE = HEAD_DIM ** -0.5
ADAM_LR, ADAM_B1, ADAM_B2, ADAM_EPS, ADAM_WD, ADAM_STEP = 0.001, 0.9, 0.999, 1e-08, 0.01, 10

V7X_VMEM_BYTES = 64 * 2**20
VMEM_LIMIT_BYTES = V7X_VMEM_BYTES - 8 * 2**20
LANES = 128


def _params(*sem):
    return pltpu.CompilerParams(dimension_semantics=sem or None, vmem_limit_bytes=VMEM_LIMIT_BYTES)


_DOT_DIMS = {"nn": (((1,), (0,)), ((), ())), "nt": (((1,), (1,)), ((), ())), "tn": (((0,), (0,)), ((), ()))}


def _matmul(a, b, *, mode, m, n, k, tm, tn, tk, out_dtype, name, a_m0=0, b_n0=0, b_k0=0, add=None,
            out_rows=None, out_m0=0, into=None):
    tm, tn, tk = min(tm, m), min(tn, n), min(tk, k)
    out_rows = m if out_rows is None else out_rows
    assert m % tm == 0 and n % tn == 0 and k % tk == 0, (name, m, n, k, tm, tn, tk)
    assert a_m0 % tm == 0 and b_n0 % tn == 0 and b_k0 % tk == 0 and out_m0 % tm == 0, name
    am, bn, bk, om = a_m0 // tm, b_n0 // tn, b_k0 // tk, out_m0 // tm
    gk = k // tk
    dims = _DOT_DIMS[mode]

    if mode == "tn":
        a_spec = pl.BlockSpec((tk, tm), lambda i, j, kk: (kk, i + am))
    else:
        a_spec = pl.BlockSpec((tm, tk), lambda i, j, kk: (i + am, kk))
    if mode == "nt":
        b_spec = pl.BlockSpec((tn, tk), lambda i, j, kk: (j + bn, kk + bk))
    else:
        b_spec = pl.BlockSpec((tk, tn), lambda i, j, kk: (kk + bk, j + bn))
    o_spec = pl.BlockSpec((tm, tn), lambda i, j, kk: (i + om, j))
    in_specs = [a_spec, b_spec]
    operands = [a, b]
    if add is not None:
        assert out_rows == m
        in_specs.append(o_spec)
        operands.append(add)
    aliases = {}
    if into is not None:
        aliases = {len(operands): 0}
        in_specs.append(pl.BlockSpec(memory_space=pl.ANY))
        operands.append(into)

    def body(*refs):
        a_ref, b_ref = refs[0], refs[1]
        add_ref = refs[2] if add is not None else None
        o_ref = refs[len(operands)]
        part = lax.dot_general(a_ref[...].astype(BF16), b_ref[...].astype(BF16), dims,
                               preferred_element_type=F32)

        def finish(total):
            if add_ref is not None:
                total = total + add_ref[...]
            o_ref[...] = total.astype(out_dtype)

        if gk == 1:
            finish(part)
        else:
            acc_ref = refs[-1]
            kk = pl.program_id(2)

            @pl.when(kk == 0)
            def _():
                acc_ref[...] = part

            @pl.when(kk > 0)
            def _():
                acc_ref[...] += part

            @pl.when(kk == gk - 1)
            def _():
                finish(acc_ref[...])

    return pl.pallas_call(
        body, name=name,
        grid=(m // tm, n // tn, gk),
        in_specs=in_specs, out_specs=o_spec,
        out_shape=jax.ShapeDtypeStruct((out_rows, n), out_dtype),
        scratch_shapes=[pltpu.VMEM((tm, tn), F32)] if gk > 1 else [],
        input_output_aliases=aliases,
        compiler_params=_params("parallel", "parallel", "arbitrary"),
    )(*operands)


def _rms_hat(v):
    r = lax.rsqrt(jnp.mean(v * v, axis=-1, keepdims=True) + EPS)
    return v * r, r


def _rms_bwd(dn, hat, r):
    return r * (dn - hat * jnp.mean(dn * hat, axis=-1, keepdims=True))


def _sigmoid(v):
    return 1.0 / (1.0 + jnp.exp(-v))


_GELU_C = 0.7978845608028654


def _gelu(v):
    return 0.5 * v * (1.0 + jnp.tanh(_GELU_C * (v + 0.044715 * v * (v * v))))


def _gelu_parts(v):
    v2 = v * v
    t = jnp.tanh(_GELU_C * (v + 0.044715 * v * v2))
    cdf = 0.5 * (1.0 + t)
    dcdf = 0.5 * (1.0 - t * t) * _GELU_C * (1.0 + 3.0 * 0.044715 * v2)
    return v * cdf, cdf + v * dcdf


def _row_tile(t_rows, want):
    tile = min(want, t_rows)
    assert t_rows % tile == 0
    return tile


def _pre_norm(x, g, name):
    t_rows, d = x.shape
    tm = _row_tile(t_rows, 512)

    def body(x_ref, g_ref, u_ref):
        hat, _ = _rms_hat(x_ref[...])
        u_ref[...] = (hat * g_ref[...]).astype(BF16)

    return pl.pallas_call(
        body, name=name, grid=(t_rows // tm,),
        in_specs=[pl.BlockSpec((tm, d), lambda i: (i, 0)), pl.BlockSpec((1, d), lambda i: (0, 0))],
        out_specs=pl.BlockSpec((tm, d), lambda i: (i, 0)),
        out_shape=jax.ShapeDtypeStruct((t_rows, d), BF16),
        compiler_params=_params("parallel"),
    )(x, g)


def _mid_forward(x, mixed, g_post, g_pre):
    t_rows, d = x.shape
    tm = _row_tile(t_rows, 512)

    def body(x_ref, mixed_ref, gpost_ref, gpre_ref, h1_ref, u2_ref):
        hat, _ = _rms_hat(mixed_ref[...])
        h1 = x_ref[...] + hat * gpost_ref[...]
        h1_ref[...] = h1
        hat1, _ = _rms_hat(h1)
        u2_ref[...] = (hat1 * gpre_ref[...]).astype(BF16)

    row = pl.BlockSpec((tm, d), lambda i: (i, 0))
    vec = pl.BlockSpec((1, d), lambda i: (0, 0))
    return pl.pallas_call(
        body, name="mid_forward", grid=(t_rows // tm,),
        in_specs=[row, row, vec, vec], out_specs=[row, row],
        out_shape=[jax.ShapeDtypeStruct((t_rows, d), F32), jax.ShapeDtypeStruct((t_rows, d), BF16)],
        compiler_params=_params("parallel"),
    )(x, mixed, g_post, g_pre)


def _loss_and_head_backward(h1, f, target, g_post):
    t_rows, d = h1.shape
    tm = _row_tile(t_rows, 512)
    nt = t_rows // tm

    def body(h1_ref, f_ref, tgt_ref, g_ref, loss_ref, dy_ref, df_ref, dg_ref, sq_ref):
        i = pl.program_id(0)

        @pl.when(i == 0)
        def _():
            sq_ref[...] = jnp.zeros_like(sq_ref)
            dg_ref[...] = jnp.zeros_like(dg_ref)

        g = g_ref[...]
        hat, r = _rms_hat(f_ref[...])
        err = h1_ref[...] + hat * g - tgt_ref[...]
        sq_ref[...] += jnp.sum(err * err, axis=0, keepdims=True)
        dy = err * (1.0 / d)
        dy_ref[...] = dy
        dg_ref[...] += jnp.sum(dy * hat, axis=0, keepdims=True)
        df_ref[...] = _rms_bwd(dy * g, hat, r).astype(BF16)

        @pl.when(i == nt - 1)
        def _():
            loss_ref[...] = (0.5 / d) * jnp.sum(sq_ref[...], axis=1, keepdims=True)

    row = pl.BlockSpec((tm, d), lambda i: (i, 0))
    vec = pl.BlockSpec((1, d), lambda i: (0, 0))
    return pl.pallas_call(
        body, name="loss_head_backward", grid=(nt,),
        in_specs=[row, row, row, vec],
        out_specs=[pl.BlockSpec((1, 1), lambda i: (0, 0)), row, row, vec],
        out_shape=[jax.ShapeDtypeStruct((1, 1), F32), jax.ShapeDtypeStruct((t_rows, d), F32),
                   jax.ShapeDtypeStruct((t_rows, d), BF16), jax.ShapeDtypeStruct((1, d), F32)],
        scratch_shapes=[pltpu.VMEM((1, d), F32)],
        compiler_params=_params("arbitrary"),
    )(h1, f, target, g_post)


def _mid_backward(dy, du2, h1, mixed, g_pre, g_post):
    t_rows, d = dy.shape
    tm = _row_tile(t_rows, 512)

    def body(dy_ref, du2_ref, h1_ref, mixed_ref, gpre_ref, gpost_ref, dh1_ref, dmixed_ref, dgpre_ref, dgpost_ref):
        @pl.when(pl.program_id(0) == 0)
        def _():
            dgpre_ref[...] = jnp.zeros_like(dgpre_ref)
            dgpost_ref[...] = jnp.zeros_like(dgpost_ref)

        du2 = du2_ref[...]
        hat1, r1 = _rms_hat(h1_ref[...])
        dgpre_ref[...] += jnp.sum(du2 * hat1, axis=0, keepdims=True)
        dh1 = dy_ref[...] + _rms_bwd(du2 * gpre_ref[...], hat1, r1)
        dh1_ref[...] = dh1
        hatm, rm = _rms_hat(mixed_ref[...])
        dgpost_ref[...] += jnp.sum(dh1 * hatm, axis=0, keepdims=True)
        dmixed_ref[...] = _rms_bwd(dh1 * gpost_ref[...], hatm, rm).astype(BF16)

    row = pl.BlockSpec((tm, d), lambda i: (i, 0))
    vec = pl.BlockSpec((1, d), lambda i: (0, 0))
    return pl.pallas_call(
        body, name="mid_backward", grid=(t_rows // tm,),
        in_specs=[row, row, row, row, vec, vec], out_specs=[row, row, vec, vec],
        out_shape=[jax.ShapeDtypeStruct((t_rows, d), F32), jax.ShapeDtypeStruct((t_rows, d), BF16),
                   jax.ShapeDtypeStruct((1, d), F32), jax.ShapeDtypeStruct((1, d), F32)],
        compiler_params=_params("arbitrary"),
    )(dy, du2, h1, mixed, g_pre, g_post)


def _input_backward(dh1, du1, x, g_pre):
    t_rows, d = x.shape
    tm = _row_tile(t_rows, 512)

    def body(dh1_ref, du1_ref, x_ref, g_ref, dx_ref, dg_ref):
        @pl.when(pl.program_id(0) == 0)
        def _():
            dg_ref[...] = jnp.zeros_like(dg_ref)

        du1 = du1_ref[...]
        hat, r = _rms_hat(x_ref[...])
        dg_ref[...] += jnp.sum(du1 * hat, axis=0, keepdims=True)
        dx_ref[...] = dh1_ref[...] + _rms_bwd(du1 * g_ref[...], hat, r)

    row = pl.BlockSpec((tm, d), lambda i: (i, 0))
    vec = pl.BlockSpec((1, d), lambda i: (0, 0))
    return pl.pallas_call(
        body, name="input_backward", grid=(t_rows // tm,),
        in_specs=[row, row, row, vec], out_specs=[row, vec],
        out_shape=[jax.ShapeDtypeStruct((t_rows, d), F32), jax.ShapeDtypeStruct((1, d), F32)],
        compiler_params=_params("arbitrary"),
    )(dh1, du1, x, g_pre)


CONV_STRIP = 32


def _glu(block):
    return block[:, :CONV_WIDTH] * _sigmoid(block[:, CONV_WIDTH:])


def _layer_norm_parts(c):
    mu = jnp.mean(c, axis=-1, keepdims=True)
    xc = c - mu
    r = lax.rsqrt(jnp.mean(xc * xc, axis=-1, keepdims=True) + EPS)
    return xc * r, r


def _conv_forward(proj_a, w, b, ln_g, ln_b):
    t_rows = proj_a.shape[0]
    tm = _row_tile(t_rows, 512)
    hb = tm // CONV_HALO
    cw = CONV_WIDTH

    def body(cur_ref, prev_ref, w_ref, b_ref, g_ref, beta_ref, c_ref, out_ref, hbuf):
        i = pl.program_id(0)
        hbuf[0:CONV_HALO, :] = jnp.where(i > 0, _glu(prev_ref[...]), 0.0)
        hbuf[CONV_HALO:, :] = _glu(cur_ref[...])

        def strip(s, carry):
            base = pl.multiple_of(s * CONV_STRIP, CONV_STRIP)
            v = hbuf[pl.ds(base, 2 * CONV_STRIP), :]
            acc = jnp.broadcast_to(b_ref[...], (CONV_STRIP, cw))
            off = CONV_HALO - (CONV_K - 1)
            for kk in range(CONV_K):
                acc = acc + w_ref[kk:kk + 1, :] * v[off + kk:off + kk + CONV_STRIP, :]
            c_ref[pl.ds(base, CONV_STRIP), :] = acc
            hat, _ = _layer_norm_parts(acc)
            z = hat * g_ref[...] + beta_ref[...]
            out_ref[pl.ds(base, CONV_STRIP), :] = (z * _sigmoid(z)).astype(BF16)
            return carry

        lax.fori_loop(0, tm // CONV_STRIP, strip, 0)

    vec = pl.BlockSpec((1, cw), lambda i: (0, 0))
    return pl.pallas_call(
        body, name="conv_forward", grid=(t_rows // tm,),
        in_specs=[pl.BlockSpec((tm, 2 * cw), lambda i: (i, 0)),
                  pl.BlockSpec((CONV_HALO, 2 * cw), lambda i: (jnp.maximum(i * hb - 1, 0), 0)),
                  pl.BlockSpec((CONV_K, cw), lambda i: (0, 0)), vec, vec, vec],
        out_specs=[pl.BlockSpec((tm, cw), lambda i: (i, 0)), pl.BlockSpec((tm, cw), lambda i: (i, 0))],
        out_shape=[jax.ShapeDtypeStruct((t_rows, cw), F32), jax.ShapeDtypeStruct((t_rows, cw), BF16)],
        scratch_shapes=[pltpu.VMEM((tm + CONV_HALO, cw), F32)],
        compiler_params=_params("parallel"),
    )(proj_a, proj_a, w, b, ln_g, ln_b)


def _conv_backward(dout, c, proj_a, w, ln_g, ln_b):
    t_rows = c.shape[0]
    tm = _row_tile(t_rows, 512)
    hb = tm // CONV_HALO
    nt = t_rows // tm
    last_halo = t_rows // CONV_HALO - 1
    cw = CONV_WIDTH

    def body(dout_ref, dout_next_ref, c_ref, c_next_ref, cur_ref, prev_ref, w_ref, g_ref, beta_ref,
             dproj_ref, dw_ref, db_ref, dg_ref, dbeta_ref, hbuf, dcbuf, dwacc):
        i = pl.program_id(0)

        @pl.when(i == 0)
        def _():
            dwacc[...] = jnp.zeros_like(dwacc)
            db_ref[...] = jnp.zeros_like(db_ref)
            dg_ref[...] = jnp.zeros_like(dg_ref)
            dbeta_ref[...] = jnp.zeros_like(dbeta_ref)

        def ln_swish_backward(dout_v, c_v):
            hat, r = _layer_norm_parts(c_v)
            g = g_ref[...]
            z = hat * g + beta_ref[...]
            sg = _sigmoid(z)
            dz = dout_v * (sg * (1.0 + z * (1.0 - sg)))
            dhat = dz * g
            dc = r * (dhat - jnp.mean(dhat, axis=-1, keepdims=True)
                      - hat * jnp.mean(dhat * hat, axis=-1, keepdims=True))
            return dc, dz, hat

        dc, dz, hat = ln_swish_backward(dout_ref[...], c_ref[...])
        dg_ref[...] += jnp.sum(dz * hat, axis=0, keepdims=True)
        dbeta_ref[...] += jnp.sum(dz, axis=0, keepdims=True)
        db_ref[...] += jnp.sum(dc, axis=0, keepdims=True)
        dcbuf[0:tm, :] = dc
        dc_next, _, _ = ln_swish_backward(dout_next_ref[...], c_next_ref[...])
        dcbuf[tm:, :] = jnp.where(i < nt - 1, dc_next, 0.0)

        hbuf[0:CONV_HALO, :] = jnp.where(i > 0, _glu(prev_ref[...]), 0.0)
        hbuf[CONV_HALO:, :] = _glu(cur_ref[...])

        def strip(s, carry):
            base = pl.multiple_of(s * CONV_STRIP, CONV_STRIP)
            dv = dcbuf[pl.ds(base, 2 * CONV_STRIP), :]
            hv = hbuf[pl.ds(base, 2 * CONV_STRIP), :]
            dcs = dv[0:CONV_STRIP, :]
            dh = jnp.zeros((CONV_STRIP, cw), F32)
            off = CONV_HALO - (CONV_K - 1)
            for kk in range(CONV_K):
                back = CONV_K - 1 - kk
                dh = dh + w_ref[kk:kk + 1, :] * dv[back:back + CONV_STRIP, :]
                prod = dcs * hv[off + kk:off + kk + CONV_STRIP, :]
                dwacc[kk] += jnp.sum(prod.reshape(CONV_STRIP // 8, 8, cw), axis=0)
            blk = cur_ref[pl.ds(base, CONV_STRIP), :]
            val, sg = blk[:, :cw], _sigmoid(blk[:, cw:])
            dproj_ref[pl.ds(base, CONV_STRIP), 0:cw] = (dh * sg).astype(BF16)
            dproj_ref[pl.ds(base, CONV_STRIP), cw:2 * cw] = (dh * val * sg * (1.0 - sg)).astype(BF16)
            return carry

        lax.fori_loop(0, tm // CONV_STRIP, strip, 0)

        @pl.when(i == nt - 1)
        def _():
            for kk in range(CONV_K):
                dw_ref[kk:kk + 1, :] = jnp.sum(dwacc[kk], axis=0, keepdims=True)

    vec = pl.BlockSpec((1, cw), lambda i: (0, 0))
    cur = lambda width: pl.BlockSpec((tm, width), lambda i: (i, 0))
    nxt = lambda width: pl.BlockSpec((CONV_HALO, width), lambda i: (jnp.minimum((i + 1) * hb, last_halo), 0))
    return pl.pallas_call(
        body, name="conv_backward", grid=(nt,),
        in_specs=[cur(cw), nxt(cw), cur(cw), nxt(cw), cur(2 * cw),
                  pl.BlockSpec((CONV_HALO, 2 * cw), lambda i: (jnp.maximum(i * hb - 1, 0), 0)),
                  pl.BlockSpec((CONV_K, cw), lambda i: (0, 0)), vec, vec],
        out_specs=[cur(2 * cw), pl.BlockSpec((CONV_K, cw), lambda i: (0, 0)), vec, vec, vec],
        out_shape=[jax.ShapeDtypeStruct((t_rows, 2 * cw), BF16), jax.ShapeDtypeStruct((CONV_K, cw), F32),
                   jax.ShapeDtypeStruct((1, cw), F32), jax.ShapeDtypeStruct((1, cw), F32),
                   jax.ShapeDtypeStruct((1, cw), F32)],
        scratch_shapes=[pltpu.VMEM((tm + CONV_HALO, cw), F32), pltpu.VMEM((tm + CONV_HALO, cw), F32),
                        pltpu.VMEM((CONV_K, 8, cw), F32)],
        compiler_params=_params("arbitrary"),
    )(dout, dout, c, c, proj_a, proj_a, w, ln_g, ln_b)


def _attn_load_kv(kv_hbm, k_pad, v_pad, sem, t_cols):
    k_pad[:, 0:LEFT] = jnp.zeros((ATTN_WIDTH, LEFT), BF16)
    v_pad[:, 0:LEFT] = jnp.zeros((ATTN_WIDTH, LEFT), BF16)
    ck = pltpu.make_async_copy(kv_hbm.at[pl.ds(ATTN_WIDTH, ATTN_WIDTH), :], k_pad.at[:, pl.ds(LEFT, t_cols)], sem.at[0])
    cv = pltpu.make_async_copy(kv_hbm.at[pl.ds(2 * ATTN_WIDTH, ATTN_WIDTH), :], v_pad.at[:, pl.ds(LEFT, t_cols)], sem.at[1])
    ck.start()
    cv.start()
    ck.wait()
    cv.wait()


def _attn_build_bias(tab_ref, bias_t):
    row = lax.broadcasted_iota(jnp.int32, (LANES, LANES), 0)
    lane = lax.broadcasted_iota(jnp.int32, (LANES, LANES), 1)
    upper = lane >= row
    lane64 = lax.broadcasted_iota(jnp.int32, (CHUNK, LANES), 1)
    for h in range(N_HEADS):
        far = jnp.broadcast_to(tab_ref[h:h + 1, 2 * MAX_REL:2 * MAX_REL + 1], (LANES, LANES))
        hi = jnp.broadcast_to(tab_ref[h:h + 1, MAX_REL:2 * MAX_REL], (LANES, LANES))
        lo = jnp.broadcast_to(tab_ref[h:h + 1, 0:MAX_REL], (LANES, LANES))
        hi_d = pltpu.roll(hi, 0, 1, stride=1, stride_axis=0)
        lo_d = pltpu.roll(lo, 0, 1, stride=1, stride_axis=0)
        bias_t[h, 0:WIN - 2 * LANES, :] = jnp.broadcast_to(far[0:1, :], (WIN - 2 * LANES, LANES))
        bias_t[h, WIN - 2 * LANES:WIN - LANES, :] = jnp.where(upper, far, hi_d)
        bias_t[h, WIN - LANES:WIN, :] = jnp.where(upper, hi_d, lo_d)
        bias_t[h, 0:CHUNK, :] = jnp.where(lane64 < CHUNK, bias_t[h, 0:CHUNK, :], NEG_INF)
        bias_t[h, WIN - CHUNK:WIN, :] = jnp.where(lane64 >= CHUNK, bias_t[h, WIN - CHUNK:WIN, :], NEG_INF)


def _attn_probs(k_h, q_h, bias_h, first_valid):
    s = lax.dot_general(k_h, q_h, _DOT_DIMS["tn"], preferred_element_type=F32) * SCALE + bias_h
    key = lax.broadcasted_iota(jnp.int32, s.shape, 0)
    s = jnp.where(key >= first_valid, s, NEG_INF)
    e = jnp.exp(s - jnp.max(s, axis=0, keepdims=True))
    return e * (1.0 / jnp.sum(e, axis=0, keepdims=True))


def _attn_forward(qkv_t, rel_bias):
    t_cols = qkv_t.shape[1]
    steps = t_cols // QBLK

    def body(q_ref, kv_hbm, tab_ref, o_ref, k_pad, v_pad, bias_t, sem):
        m = pl.program_id(0)

        @pl.when(m == 0)
        def _():
            _attn_build_bias(tab_ref, bias_t)
            _attn_load_kv(kv_hbm, k_pad, v_pad, sem, t_cols)

        w0 = pl.multiple_of(m * QBLK, QBLK)
        first_valid = LEFT - m * QBLK
        for h in range(N_HEADS):
            rows = slice(h * HEAD_DIM, (h + 1) * HEAD_DIM)
            k_h = k_pad[rows, pl.ds(w0, WIN)]
            v_h = v_pad[rows, pl.ds(w0, WIN)]
            p = _attn_probs(k_h, q_ref[rows, :], bias_t[h], first_valid)
            o_h = lax.dot_general(v_h, p.astype(BF16), _DOT_DIMS["nn"], preferred_element_type=F32)
            o_ref[rows, :] = o_h.astype(BF16)

    return pl.pallas_call(
        body, name="attn_forward", grid=(steps,),
        in_specs=[pl.BlockSpec((ATTN_WIDTH, QBLK), lambda m: (0, m)),
                  pl.BlockSpec(memory_space=pl.ANY),
                  pl.BlockSpec((N_HEADS, 2 * MAX_REL + 1), lambda m: (0, 0))],
        out_specs=pl.BlockSpec((ATTN_WIDTH, QBLK), lambda m: (0, m)),
        out_shape=jax.ShapeDtypeStruct((ATTN_WIDTH, t_cols), BF16),
        scratch_shapes=[pltpu.VMEM((ATTN_WIDTH, LEFT + t_cols), BF16), pltpu.VMEM((ATTN_WIDTH, LEFT + t_cols), BF16),
                        pltpu.VMEM((N_HEADS, WIN, QBLK), F32), pltpu.SemaphoreType.DMA((2,))],
        compiler_params=_params("arbitrary"),
    )(qkv_t, qkv_t, rel_bias)


def _reverse_lanes(v, flip):
    out = jnp.zeros(v.shape, F32)
    rest = v
    for _ in range(3):
        piece = rest.astype(BF16)
        out = out + lax.dot_general(piece, flip, _DOT_DIMS["nn"], preferred_element_type=F32)
        rest = rest - piece.astype(F32)
    return out


def _attn_bias_grad(dbias_t, drel_ref):
    row = lax.broadcasted_iota(jnp.int32, (LANES, LANES), 0)
    lane = lax.broadcasted_iota(jnp.int32, (LANES, LANES), 1)
    flip = (row + lane == LANES - 1).astype(BF16)
    head = lax.broadcasted_iota(jnp.int32, (N_HEADS, LANES), 0)
    lane8 = lax.broadcasted_iota(jnp.int32, (N_HEADS, LANES), 1)
    upper_rev = jnp.zeros((N_HEADS, LANES), F32)
    lower_rev = jnp.zeros((N_HEADS, LANES), F32)
    far = jnp.zeros((N_HEADS, LANES), F32)
    for h in range(N_HEADS):
        def diagonals(block):
            skew = pltpu.roll(_reverse_lanes(block, flip), 0, 1, stride=1, stride_axis=0)
            pos = jnp.sum(jnp.where(lane >= row, skew, 0.0), axis=0, keepdims=True)
            neg = jnp.sum(jnp.where(lane < row, skew, 0.0), axis=0, keepdims=True)
            return pos, neg

        pos4, neg4 = diagonals(dbias_t[h, WIN - LANES:WIN, :])
        pos3, neg3 = diagonals(dbias_t[h, WIN - 2 * LANES:WIN - LANES, :])
        far_h = jnp.sum(jnp.sum(dbias_t[h, 0:WIN - 2 * LANES, :], axis=0, keepdims=True), axis=1, keepdims=True)
        far_h = far_h + jnp.sum(pos3, axis=1, keepdims=True)
        upper_rev = jnp.where(head == h, pos4 + neg3, upper_rev)
        lower_rev = jnp.where(head == h, neg4, lower_rev)
        far = jnp.where((head == h) & (lane8 == 0), far_h, far)
    drel_ref[:, 0:LANES] = _reverse_lanes(lower_rev, flip)
    drel_ref[:, LANES:2 * LANES] = _reverse_lanes(upper_rev, flip)
    drel_ref[:, 2 * LANES:3 * LANES] = far


def _attn_backward(qkv_t, o_t, do_t, rel_bias):
    t_cols = qkv_t.shape[1]
    steps = t_cols // QBLK
    flush = LEFT // QBLK
    total = steps + flush

    def body(q_ref, o_ref, do_ref, kv_hbm, tab_ref, dq_ref, dk_ref, dv_ref, drel_ref,
             k_pad, v_pad, bias_t, dbias_t, dk_acc, dv_acc, sem):
        m = pl.program_id(0)

        @pl.when(m == 0)
        def _():
            _attn_build_bias(tab_ref, bias_t)
            _attn_load_kv(kv_hbm, k_pad, v_pad, sem, t_cols)
            dbias_t[...] = jnp.zeros_like(dbias_t)
            dk_acc[...] = jnp.zeros_like(dk_acc)
            dv_acc[...] = jnp.zeros_like(dv_acc)

        @pl.when(m < steps)
        def _():
            w0 = pl.multiple_of(m * QBLK, QBLK)
            first_valid = LEFT - m * QBLK
            for h in range(N_HEADS):
                rows = slice(h * HEAD_DIM, (h + 1) * HEAD_DIM)
                k_h = k_pad[rows, pl.ds(w0, WIN)]
                v_h = v_pad[rows, pl.ds(w0, WIN)]
                q_h = q_ref[rows, :]
                do_h = do_ref[rows, :]
                p = _attn_probs(k_h, q_h, bias_t[h], first_valid)
                dp = lax.dot_general(v_h, do_h, _DOT_DIMS["tn"], preferred_element_type=F32)
                delta = jnp.sum(do_h.astype(F32) * o_ref[rows, :].astype(F32), axis=0, keepdims=True)
                ds = p * (dp - delta)
                dbias_t[h] += ds
                ds_b = (ds * SCALE).astype(BF16)
                dq_h = lax.dot_general(k_h, ds_b, _DOT_DIMS["nn"], preferred_element_type=F32)
                dq_ref[rows, :] = dq_h.astype(BF16)
                dk_acc[rows, :] += lax.dot_general(q_h, ds_b, _DOT_DIMS["nt"], preferred_element_type=F32)
                dv_acc[rows, :] += lax.dot_general(do_h, p.astype(BF16), _DOT_DIMS["nt"], preferred_element_type=F32)

        dk_ref[...] = dk_acc[:, 0:QBLK].astype(BF16)
        dv_ref[...] = dv_acc[:, 0:QBLK].astype(BF16)
        for acc in (dk_acc, dv_acc):
            rest = acc[:, QBLK:WIN]
            acc[:, 0:LEFT] = rest
            acc[:, LEFT:WIN] = jnp.zeros((ATTN_WIDTH, QBLK), F32)

        @pl.when(m == total - 1)
        def _():
            _attn_bias_grad(dbias_t, drel_ref)

    qblk = pl.BlockSpec((ATTN_WIDTH, QBLK), lambda m: (0, jnp.minimum(m, steps - 1)))
    kblk = pl.BlockSpec((ATTN_WIDTH, QBLK), lambda m: (0, jnp.maximum(m - flush, 0)))
    dq, dk, dv, drel = pl.pallas_call(
        body, name="attn_backward", grid=(total,),
        in_specs=[qblk, qblk, qblk, pl.BlockSpec(memory_space=pl.ANY),
                  pl.BlockSpec((N_HEADS, 2 * MAX_REL + 1), lambda m: (0, 0))],
        out_specs=[qblk, kblk, kblk, pl.BlockSpec((N_HEADS, 3 * LANES), lambda m: (0, 0))],
        out_shape=[jax.ShapeDtypeStruct((ATTN_WIDTH, t_cols), BF16)] * 3
        + [jax.ShapeDtypeStruct((N_HEADS, 3 * LANES), F32)],
        scratch_shapes=[pltpu.VMEM((ATTN_WIDTH, LEFT + t_cols), BF16), pltpu.VMEM((ATTN_WIDTH, LEFT + t_cols), BF16),
                        pltpu.VMEM((N_HEADS, WIN, QBLK), F32), pltpu.VMEM((N_HEADS, WIN, QBLK), F32),
                        pltpu.VMEM((ATTN_WIDTH, WIN), F32), pltpu.VMEM((ATTN_WIDTH, WIN), F32),
                        pltpu.SemaphoreType.DMA((2,))],
        compiler_params=_params("arbitrary"),
    )(qkv_t, o_t, do_t, qkv_t, rel_bias)
    return dq, dk, dv, drel


FFN_TC = 256


def _ffn_specs(t_rows, tr):
    nj = D_FF // FFN_TC
    hb = tr // FFN_HALO
    last_halo = t_rows // FFN_HALO - 1
    cur = lambda off: pl.BlockSpec((tr, FFN_TC), lambda j, i: (i, j + off))
    prev = lambda off: pl.BlockSpec((FFN_HALO, FFN_TC), lambda j, i: (jnp.maximum(i * hb - 1, 0), j + off))
    nxt = lambda off: pl.BlockSpec((FFN_HALO, FFN_TC), lambda j, i: (jnp.minimum((i + 1) * hb, last_halo), j + off))
    wspec = lambda off: pl.BlockSpec((FFN_K, FFN_TC), lambda j, i: (0, j + off))
    bspec = lambda off: pl.BlockSpec((1, FFN_TC), lambda j, i: (0, j + off))
    return nj, cur, prev, nxt, wspec, bspec


FFN_STRIP = 16


def _ffn_conv(win, w, b, rows):
    out = b + w[2] * win[FFN_HALO:FFN_HALO + rows, :]
    out = out + w[1] * win[FFN_HALO - 1:FFN_HALO - 1 + rows, :]
    return out + w[0] * win[FFN_HALO - 2:FFN_HALO - 2 + rows, :]


def _taps(w_ref):
    return [w_ref[kk:kk + 1, :] for kk in range(FFN_K)]


def _ffn_first_window(prev_ref, cur_ref, tile, rows):
    return jnp.concatenate([jnp.where(tile > 0, prev_ref[...], 0.0), cur_ref[0:rows, :]], axis=0)


def _fold8(v):
    return jnp.sum(v.reshape(v.shape[0] // 8, 8, v.shape[1]), axis=0)


def _ffn_activation(hup, w, b):
    t_rows = hup.shape[0]
    tr = _row_tile(t_rows, 512)
    nj, cur, prev, nxt, wspec, bspec = _ffn_specs(t_rows, tr)
    rs = 2 * FFN_STRIP

    def body(g_ref, gprev_ref, v_ref, vprev_ref, wg_ref, wv_ref, bg_ref, bv_ref, act_ref):
        i = pl.program_id(1)
        wg, wv, bg, bv = _taps(wg_ref), _taps(wv_ref), bg_ref[...], bv_ref[...]

        def emit(base, g_win, v_win):
            act = _gelu(_ffn_conv(g_win, wg, bg, rs)) * _ffn_conv(v_win, wv, bv, rs)
            act_ref[pl.ds(base, rs), :] = act.astype(BF16)

        def strip(s, carry):
            base = pl.multiple_of(s * rs, rs)
            emit(base, g_ref[pl.ds(base - FFN_HALO, rs + FFN_HALO), :], v_ref[pl.ds(base - FFN_HALO, rs + FFN_HALO), :])
            return carry

        emit(0, _ffn_first_window(gprev_ref, g_ref, i, rs), _ffn_first_window(vprev_ref, v_ref, i, rs))
        lax.fori_loop(1, tr // rs, strip, 0)

    return pl.pallas_call(
        body, name="ffn_activation", grid=(nj, t_rows // tr),
        in_specs=[cur(0), prev(0), cur(nj), prev(nj), wspec(0), wspec(nj), bspec(0), bspec(nj)],
        out_specs=cur(0),
        out_shape=jax.ShapeDtypeStruct((t_rows, D_FF), BF16),
        compiler_params=_params("parallel", "parallel"),
    )(hup, hup, hup, hup, w, w, b, b)


def _ffn_backward(dact, hup, w, b):
    t_rows = hup.shape[0]
    tr = _row_tile(t_rows, 512)
    nj, cur, prev, nxt, wspec, bspec = _ffn_specs(t_rows, tr)
    ni = t_rows // tr
    rs = FFN_STRIP
    ns = tr // rs

    def body(da_ref, danext_ref, g_ref, gprev_ref, gnext_ref, v_ref, vprev_ref, vnext_ref,
             wg_ref, wv_ref, bg_ref, bv_ref,
             dhg_ref, dhv_ref, dwg_ref, dwv_ref, dbg_ref, dbv_ref):
        i = pl.program_id(1)

        @pl.when(i == 0)
        def _():
            for ref in (dwg_ref, dwv_ref, dbg_ref, dbv_ref):
                ref[...] = jnp.zeros_like(ref)

        wg, wv, bg, bv = _taps(wg_ref), _taps(wv_ref), bg_ref[...], bv_ref[...]

        def conv_grads(da, g_win, v_win, rows):
            gel, dgel = _gelu_parts(_ffn_conv(g_win, wg, bg, rows))
            return da * _ffn_conv(v_win, wv, bv, rows) * dgel, da * gel

        da_after = jnp.where(i < ni - 1, danext_ref[...], 0.0)
        g_after = jnp.concatenate([g_ref[tr - FFN_HALO:tr, :], gnext_ref[...]], axis=0)
        v_after = jnp.concatenate([v_ref[tr - FFN_HALO:tr, :], vnext_ref[...]], axis=0)
        dcg_after, dcv_after = conv_grads(da_after, g_after, v_after, FFN_HALO)
        zero8 = jnp.zeros((8, FFN_TC), F32)

        def strip_at(base, g_win, v_win, carry):
            dcg_after, dcv_after, dbg, dbv, dwg, dwv = carry
            dcg, dcv = conv_grads(da_ref[pl.ds(base, rs), :], g_win, v_win, rs)
            out = []
            for dc, after, taps, win, dh_ref, db, dw in ((dcg, dcg_after, wg, g_win, dhg_ref, dbg, dwg),
                                                         (dcv, dcv_after, wv, v_win, dhv_ref, dbv, dwv)):
                ext = jnp.concatenate([dc, after], axis=0)
                dh = taps[2] * dc + taps[1] * ext[1:1 + rs, :] + taps[0] * ext[2:2 + rs, :]
                dh_ref[pl.ds(base, rs), :] = dh.astype(BF16)
                db = db + _fold8(dc)
                dw = tuple(dw[kk] + _fold8(dc * win[FFN_HALO - 2 + kk:FFN_HALO - 2 + kk + rs, :]) for kk in range(FFN_K))
                out.append((dc[0:FFN_HALO, :], db, dw))
            return out[0][0], out[1][0], out[0][1], out[1][1], out[0][2], out[1][2]

        def strip(s, carry):
            base = pl.multiple_of((ns - 1 - s) * rs, rs)
            return strip_at(base, g_ref[pl.ds(base - FFN_HALO, rs + FFN_HALO), :],
                            v_ref[pl.ds(base - FFN_HALO, rs + FFN_HALO), :], carry)

        init = (dcg_after, dcv_after, zero8, zero8, (zero8,) * FFN_K, (zero8,) * FFN_K)
        carry = lax.fori_loop(0, ns - 1, strip, init)
        _, _, dbg, dbv, dwg, dwv = strip_at(0, _ffn_first_window(gprev_ref, g_ref, i, rs),
                                            _ffn_first_window(vprev_ref, v_ref, i, rs), carry)
        dbg_ref[...] += jnp.sum(dbg, axis=0, keepdims=True)
        dbv_ref[...] += jnp.sum(dbv, axis=0, keepdims=True)
        for kk in range(FFN_K):
            dwg_ref[kk:kk + 1, :] += jnp.sum(dwg[kk], axis=0, keepdims=True)
            dwv_ref[kk:kk + 1, :] += jnp.sum(dwv[kk], axis=0, keepdims=True)

    half = jax.ShapeDtypeStruct((t_rows, D_FF), BF16)
    return pl.pallas_call(
        body, name="ffn_backward", grid=(nj, ni),
        in_specs=[cur(0), nxt(0), cur(0), prev(0), nxt(0), cur(nj), prev(nj), nxt(nj),
                  wspec(0), wspec(nj), bspec(0), bspec(nj)],
        out_specs=[cur(0), cur(0), wspec(0), wspec(0), bspec(0), bspec(0)],
        out_shape=[half, half, jax.ShapeDtypeStruct((FFN_K, D_FF), F32), jax.ShapeDtypeStruct((FFN_K, D_FF), F32),
                   jax.ShapeDtypeStruct((1, D_FF), F32), jax.ShapeDtypeStruct((1, D_FF), F32)],
        compiler_params=_params("parallel", "arbitrary"),
    )(dact, dact, hup, hup, hup, hup, hup, hup, w, w, b, b)


def _mesh_position():
    return lax.axis_index("x"), lax.axis_index("y"), lax.axis_index("c")


def _hbm_specs(n):
    return [pl.BlockSpec(memory_space=pl.ANY)] * n


def _all_gather(shards, name, place=()):
    n = len(shards)
    total = n + len(place)

    def body(*refs):
        ins, outs = refs[:total], refs[total:2 * total]
        send_sems, recv_sems, local_sems = refs[2 * total:]
        x, y, c = _mesh_position()
        me, sibling = (x, y, c), (x, y, 1 - c)
        chips = [(1 - x, y), (x, 1 - y), (1 - x, 1 - y)]

        def copy(a, slot, block, to, src=None):
            dst = outs[a].at[4 * block[0] + 2 * block[1] + block[2]]
            return pltpu.make_async_remote_copy(
                src_ref=dst if src is None else src, dst_ref=dst,
                send_sem=send_sems.at[a, slot], recv_sem=recv_sems.at[a, slot],
                device_id=to, device_id_type=MESH)

        started = []
        for a in range(total):
            mine = pltpu.make_async_copy(ins[a], outs[a].at[4 * x + 2 * y + c], local_sems.at[a])
            mine.start()
            started.append(mine)
        first = []
        for a in range(n):
            first.append(copy(a, 0, me, sibling, src=ins[a]))
            first += [copy(a, 1 + j, me, (*chip, c), src=ins[a]) for j, chip in enumerate(chips)]
        for cp in first:
            cp.start()
        passed = []
        for j, chip in enumerate(chips):
            for a in range(n):
                copy(a, 1 + j, (*chip, c), me).wait_recv()
                fwd = copy(a, 4 + j, (*chip, c), sibling)
                fwd.start()
                passed.append(fwd)
        for a in range(n):
            copy(a, 0, sibling, me).wait_recv()
            for j, chip in enumerate(chips):
                copy(a, 4 + j, (*chip, 1 - c), me).wait_recv()
        for cp in first + passed:
            cp.wait_send()
        for mine in started:
            mine.wait()

    return pl.pallas_call(
        body, name=name,
        in_specs=_hbm_specs(total), out_specs=_hbm_specs(total),
        out_shape=[jax.ShapeDtypeStruct((N_DEV,) + s.shape, s.dtype) for s in (*shards, *place)],
        scratch_shapes=[pltpu.SemaphoreType.DMA((n, 7)), pltpu.SemaphoreType.DMA((n, 7)),
                        pltpu.SemaphoreType.DMA((total,))],
        compiler_params=pltpu.CompilerParams(has_side_effects=True),
    )(*shards, *place)


_FLIPS = [(dx, dy, dc) for dx in (0, 1) for dy in (0, 1) for dc in (0, 1)][1:]
_HBM = pl.BlockSpec(memory_space=pltpu.HBM)
_SEM = pl.BlockSpec(memory_space=pltpu.SEMAPHORE)
_DATAFLOW = pltpu.SideEffectType.DATAFLOW_SIDE_EFFECTING


def _scatter_copies(src_refs, land_refs, send_sems, recv_sems, gather):
    x, y, c = _mesh_position()
    me = 4 * x + 2 * y + c
    copies = []
    for a, (src, land) in enumerate(zip(src_refs, land_refs)):
        for k, (dx, dy, dc) in enumerate(_FLIPS):
            px, py, pc = (x + dx) % 2, (y + dy) % 2, (c + dc) % 2
            pair = a * len(_FLIPS) + k
            copies.append(pltpu.make_async_remote_copy(
                src_ref=src if gather else src.at[4 * px + 2 * py + pc], dst_ref=land.at[me],
                send_sem=send_sems[pair], recv_sem=recv_sems[pair],
                device_id=(px, py, pc), device_id_type=MESH))
    return copies


def _scatter_start(srcs, lands, gather, name):
    n = len(srcs)
    pairs = n * len(_FLIPS)

    def body(*refs):
        src_refs, land_refs = refs[:n], refs[n:2 * n]
        send_sems, recv_sems = refs[2 * n:2 * n + pairs], refs[2 * n + pairs:2 * n + 2 * pairs]
        token = refs[-1]
        for cp in _scatter_copies(src_refs, land_refs, send_sems, recv_sems, gather):
            cp.start()
        token[...] = jnp.zeros_like(token)

    arrays = [*srcs, *lands]
    sem = pltpu.SemaphoreType.DMA(())
    out = pl.pallas_call(
        body, name=name,
        out_shape=(*[sem] * (2 * pairs), *[pltpu.HBM(v.shape, v.dtype) for v in arrays],
                   jax.ShapeDtypeStruct((8, LANES), F32)),
        in_specs=[_HBM] * (2 * n),
        out_specs=(*[_SEM] * (2 * pairs), *[_HBM] * (2 * n), pl.BlockSpec(memory_space=pltpu.VMEM)),
        input_output_aliases={i: 2 * pairs + i for i in range(2 * n)},
        compiler_params=pltpu.CompilerParams(has_side_effects=_DATAFLOW),
    )(*[pltpu.with_memory_space_constraint(v, pltpu.HBM) for v in arrays])
    sems, rest = out[:2 * pairs], out[2 * pairs:]
    return list(sems[:pairs]), list(sems[pairs:]), list(rest[:n]), list(rest[n:2 * n]), rest[-1]


def _scatter_wait(send_sems, recv_sems, srcs, lands, after, gather, name):
    n = len(srcs)
    pairs = n * len(_FLIPS)

    def body(*refs):
        src_refs, land_refs = refs[:n], refs[n:2 * n]
        send_refs, recv_refs = refs[2 * n:2 * n + pairs], refs[2 * n + pairs:2 * n + 2 * pairs]
        for cp in _scatter_copies(src_refs, land_refs, send_refs, recv_refs, gather):
            cp.wait_send()
            cp.wait_recv()

    arrays = [*srcs, *lands]
    out = pl.pallas_call(
        body, name=name,
        out_shape=tuple(pltpu.HBM(v.shape, v.dtype) for v in arrays),
        in_specs=[*[_HBM] * (2 * n), *[_SEM] * (2 * pairs), pl.BlockSpec(memory_space=pl.ANY)],
        out_specs=tuple([_HBM] * (2 * n)),
        input_output_aliases={i: i for i in range(2 * n)},
        compiler_params=pltpu.CompilerParams(has_side_effects=_DATAFLOW),
    )(*arrays, *send_sems, *recv_sems, after)
    return list(out[:n]), list(out[n:])


def _sum_received(grad, received, me, name):
    _, rows, cols = grad.shape

    def body(me_ref, g_ref, r_ref, o_ref):
        p = pl.program_id(0)
        term = jnp.where(p == me_ref[0], g_ref[0], r_ref[0])

        @pl.when(p == 0)
        def _():
            o_ref[...] = term

        @pl.when(p > 0)
        def _():
            o_ref[...] += term

    blk = (1, rows, cols)
    return pl.pallas_call(
        body, name=name,
        grid_spec=pltpu.PrefetchScalarGridSpec(
            num_scalar_prefetch=1, grid=(N_DEV,),
            in_specs=[pl.BlockSpec(blk, lambda p, me_ref: (me_ref[0], 0, 0)),
                      pl.BlockSpec(blk, lambda p, me_ref: (p, 0, 0))],
            out_specs=pl.BlockSpec((rows, cols), lambda p, me_ref: (0, 0))),
        out_shape=jax.ShapeDtypeStruct((rows, cols), F32),
        compiler_params=_params("arbitrary"),
    )(me, grad, received)


def _exchange_in_chip(grads):
    n = len(grads)

    def body(*refs):
        ins, outs = refs[:n], refs[n:2 * n]
        send_sems, recv_sems = refs[2 * n:]
        x, y, c = _mesh_position()
        copies = []
        for a in range(n):
            for q in range(4):
                copies.append(pltpu.make_async_remote_copy(
                    src_ref=ins[a].at[2 * q + (1 - c)], dst_ref=outs[a].at[q],
                    send_sem=send_sems.at[a, q], recv_sem=recv_sems.at[a, q],
                    device_id=(x, y, 1 - c), device_id_type=MESH))
        for cp in copies:
            cp.start()
        for cp in copies:
            cp.wait_recv()
        for cp in copies:
            cp.wait_send()

    return pl.pallas_call(
        body, name="exchange_in_chip",
        in_specs=_hbm_specs(n), out_specs=_hbm_specs(n),
        out_shape=[jax.ShapeDtypeStruct((4,) + g.shape[1:], g.dtype) for g in grads],
        scratch_shapes=[pltpu.SemaphoreType.DMA((n, 4)), pltpu.SemaphoreType.DMA((n, 4))],
        compiler_params=pltpu.CompilerParams(has_side_effects=True),
    )(*grads)


def _exchange_between_chips(partials):
    n = len(partials)

    def body(*refs):
        ins, outs = refs[:n], refs[n:2 * n]
        send_sems, recv_sems = refs[2 * n:]
        x, y, c = _mesh_position()
        chips = [(1 - x, y), (x, 1 - y), (1 - x, 1 - y)]
        copies = []
        for a in range(n):
            for j, (px, py) in enumerate(chips):
                copies.append(pltpu.make_async_remote_copy(
                    src_ref=ins[a].at[2 * px + py], dst_ref=outs[a].at[j],
                    send_sem=send_sems.at[a, j], recv_sem=recv_sems.at[a, j],
                    device_id=(px, py, c), device_id_type=MESH))
        for cp in copies:
            cp.start()
        for cp in copies:
            cp.wait_recv()
        for cp in copies:
            cp.wait_send()

    return pl.pallas_call(
        body, name="exchange_between_chips",
        in_specs=_hbm_specs(n), out_specs=_hbm_specs(n),
        out_shape=[jax.ShapeDtypeStruct((3,) + p.shape[1:], p.dtype) for p in partials],
        scratch_shapes=[pltpu.SemaphoreType.DMA((n, 3)), pltpu.SemaphoreType.DMA((n, 3))],
        compiler_params=pltpu.CompilerParams(has_side_effects=True),
    )(*partials)


def _add_in_chip(grad, received, core, name):
    _, rows, cols = grad.shape

    def body(core_ref, g_ref, r_ref, o_ref):
        o_ref[...] = g_ref[...] + r_ref[...]

    blk = (1, rows, cols)
    return pl.pallas_call(
        body, name=name,
        grid_spec=pltpu.PrefetchScalarGridSpec(
            num_scalar_prefetch=1, grid=(4,),
            in_specs=[pl.BlockSpec(blk, lambda q, core_ref: (2 * q + core_ref[0], 0, 0)),
                      pl.BlockSpec(blk, lambda q, core_ref: (q, 0, 0))],
            out_specs=pl.BlockSpec(blk, lambda q, core_ref: (q, 0, 0))),
        out_shape=jax.ShapeDtypeStruct((4, rows, cols), F32),
        compiler_params=_params("parallel"),
    )(core, grad, received)


def _add_between_chips(partial, received, chip, name):
    _, rows, cols = partial.shape

    def body(chip_ref, p_ref, r_ref, o_ref):
        o_ref[...] = ((p_ref[0] + r_ref[0]) + r_ref[1]) + r_ref[2]

    return pl.pallas_call(
        body, name=name,
        grid_spec=pltpu.PrefetchScalarGridSpec(
            num_scalar_prefetch=1, grid=(1,),
            in_specs=[pl.BlockSpec((1, rows, cols), lambda i, chip_ref: (chip_ref[0], 0, 0)),
                      pl.BlockSpec((3, rows, cols), lambda i, chip_ref: (0, 0, 0))],
            out_specs=pl.BlockSpec((rows, cols), lambda i, chip_ref: (0, 0))),
        out_shape=jax.ShapeDtypeStruct((rows, cols), F32),
        compiler_params=_params("arbitrary"),
    )(chip, partial, received)


def _sum_devices(gathered):
    _, rows, cols = gathered.shape

    def body(g_ref, o_ref):
        total = g_ref[0]
        for d in range(1, N_DEV):
            total = total + g_ref[d]
        o_ref[...] = total

    return pl.pallas_call(
        body, name="sum_small_grads",
        out_shape=jax.ShapeDtypeStruct((rows, cols), F32),
        compiler_params=_params(),
    )(gathered)


def _adamw(w, g, m, v, name):
    rows, cols = w.shape
    tr = rows
    for cand in (256, 128, 64, 32, 16, 8):
        if rows > cand and rows % cand == 0:
            tr = cand
            break

    def body(w_ref, g_ref, m_ref, v_ref, delta_ref, newm_ref, newv_ref):
        g_v = g_ref[...]
        new_m = ADAM_B1 * m_ref[...] + (1.0 - ADAM_B1) * g_v
        new_v = ADAM_B2 * v_ref[...] + (1.0 - ADAM_B2) * (g_v * g_v)
        m_hat = new_m / (1.0 - ADAM_B1 ** ADAM_STEP)
        v_hat = new_v / (1.0 - ADAM_B2 ** ADAM_STEP)
        delta_ref[...] = -ADAM_LR * (m_hat / (jnp.sqrt(v_hat) + ADAM_EPS) + ADAM_WD * w_ref[...])
        newm_ref[...] = new_m
        newv_ref[...] = new_v

    blk = pl.BlockSpec((tr, cols), lambda i: (i, 0))
    shape = jax.ShapeDtypeStruct((rows, cols), F32)
    return pl.pallas_call(
        body, name=name, grid=(rows // tr,),
        in_specs=[blk] * 4, out_specs=[blk] * 3, out_shape=[shape] * 3,
        compiler_params=_params("parallel"),
    )(w, g, m, v)


def _pack(pieces, rows):
    flat = jnp.concatenate([p.reshape(-1) for p in pieces])
    return jnp.pad(flat, (0, rows * LANES - flat.shape[0])).reshape(rows, LANES)


def _unpack(packed, shapes):
    flat = packed.reshape(-1)
    out, pos = [], 0
    for shape in shapes:
        size = 1
        for s in shape:
            size *= s
        out.append(flat[pos:pos + size].reshape(shape))
        pos += size
    return out


def _rows_for(count):
    return -(-count // (8 * LANES)) * 8


SMALL = ("norm_mix_pre", "conv_dw_b", "conv_ln_g", "conv_ln_b", "rel_bias", "norm_mix_post", "norm_ffn_pre",
         "ffn_dw_b", "norm_ffn_post")
SHARDED_SMALL = ("conv_dw_w", "ffn_dw_w")
LARGE = ("w_in", "w_out", "w_up", "w_down")
WEIGHTS = ("norm_mix_pre", "w_in", "conv_dw_w", "conv_dw_b", "conv_ln_g", "conv_ln_b", "rel_bias", "w_out",
           "norm_mix_post", "norm_ffn_pre", "w_up", "ffn_dw_w", "ffn_dw_b", "w_down", "norm_ffn_post")


def kernel(x, norm_mix_pre, w_in, conv_dw_w, conv_dw_b, conv_ln_g, conv_ln_b, rel_bias, w_out, norm_mix_post, norm_ffn_pre, w_up, ffn_dw_w, ffn_dw_b, w_down, norm_ffn_post, loss_target, m_norm_mix_pre, m_w_in, m_conv_dw_w, m_conv_dw_b, m_conv_ln_g, m_conv_ln_b, m_rel_bias, m_w_out, m_norm_mix_post, m_norm_ffn_pre, m_w_up, m_ffn_dw_w, m_ffn_dw_b, m_w_down, m_norm_ffn_post, v_norm_mix_pre, v_w_in, v_conv_dw_w, v_conv_dw_b, v_conv_ln_g, v_conv_ln_b, v_rel_bias, v_w_out, v_norm_mix_post, v_norm_ffn_pre, v_w_up, v_ffn_dw_w, v_ffn_dw_b, v_w_down, v_norm_ffn_post):
    weights = dict(norm_mix_pre=norm_mix_pre, w_in=w_in, conv_dw_w=conv_dw_w, conv_dw_b=conv_dw_b, conv_ln_g=conv_ln_g,
                   conv_ln_b=conv_ln_b, rel_bias=rel_bias, w_out=w_out, norm_mix_post=norm_mix_post,
                   norm_ffn_pre=norm_ffn_pre, w_up=w_up, ffn_dw_w=ffn_dw_w, ffn_dw_b=ffn_dw_b, w_down=w_down,
                   norm_ffn_post=norm_ffn_post)
    mom1 = dict(norm_mix_pre=m_norm_mix_pre, w_in=m_w_in, conv_dw_w=m_conv_dw_w, conv_dw_b=m_conv_dw_b,
                conv_ln_g=m_conv_ln_g, conv_ln_b=m_conv_ln_b, rel_bias=m_rel_bias, w_out=m_w_out,
                norm_mix_post=m_norm_mix_post, norm_ffn_pre=m_norm_ffn_pre, w_up=m_w_up, ffn_dw_w=m_ffn_dw_w,
                ffn_dw_b=m_ffn_dw_b, w_down=m_w_down, norm_ffn_post=m_norm_ffn_post)
    mom2 = dict(norm_mix_pre=v_norm_mix_pre, w_in=v_w_in, conv_dw_w=v_conv_dw_w, conv_dw_b=v_conv_dw_b,
                conv_ln_g=v_conv_ln_g, conv_ln_b=v_conv_ln_b, rel_bias=v_rel_bias, w_out=v_w_out,
                norm_mix_post=v_norm_mix_post, norm_ffn_pre=v_norm_ffn_pre, w_up=v_w_up, ffn_dw_w=v_ffn_dw_w,
                ffn_dw_b=v_ffn_dw_b, w_down=v_w_down, norm_ffn_post=v_norm_ffn_post)

    x2 = x[0]
    target = loss_target[0]
    t_rows = x2.shape[0]
    d = D_MODEL
    in_cols = 2 * CONV_WIDTH + 3 * ATTN_WIDTH
    my_x, my_y, my_c = _mesh_position()
    my_dev = 4 * my_x + 2 * my_y + my_c

    small_conv = _pack([conv_dw_w[0], ffn_dw_w[0]], 32)
    ffn_shards = [w_up[0].T.astype(BF16), w_down[0].astype(BF16)]
    win_t, wout_g, conv_g, wup_land, wdown_land = _all_gather(
        [w_in[0].T.astype(BF16), w_out[0].astype(BF16), small_conv], "all_gather_weights", place=ffn_shards)
    ffn_send, ffn_recv, ffn_shards, ffn_lands, ffn_token = _scatter_start(
        ffn_shards, [wup_land, wdown_land], True, "gather_ffn_weights_start")
    win_t = win_t.reshape(in_cols, d)
    wout_g = wout_g.reshape(d, d)
    conv_flat = conv_g.reshape(N_DEV, 32 * LANES)
    n_cw = CONV_K * (CONV_WIDTH // N_DEV)
    conv_w_full = conv_flat[:, :n_cw].reshape(N_DEV, CONV_K, CONV_WIDTH // N_DEV).transpose(1, 0, 2).reshape(CONV_K, CONV_WIDTH)
    ffn_w_full = conv_flat[:, n_cw:].reshape(N_DEV, FFN_K, 2 * D_FF // N_DEV).transpose(1, 0, 2).reshape(FFN_K, 2 * D_FF)

    u1 = _pre_norm(x2, norm_mix_pre + ffn_token[0:1, 0:1], "pre_norm_mix")
    proj_a = _matmul(u1, win_t, mode="nt", m=t_rows, n=2 * CONV_WIDTH, k=d, tm=2048, tn=1024, tk=d,
                     out_dtype=F32, name="proj_conv")
    qkv_t = _matmul(win_t, u1, mode="nt", m=3 * ATTN_WIDTH, n=t_rows, k=d, tm=512, tn=2048, tk=d,
                    out_dtype=BF16, name="proj_qkv", a_m0=2 * CONV_WIDTH)
    conv_c, conv_out = _conv_forward(proj_a, conv_w_full, conv_dw_b, conv_ln_g, conv_ln_b)
    o_t = _attn_forward(qkv_t, rel_bias[0])
    mixed = _matmul(conv_out, wout_g, mode="nn", m=t_rows, n=d, k=CONV_WIDTH, tm=2048, tn=1024, tk=CONV_WIDTH,
                    out_dtype=F32, name="out_proj_conv")
    mixed = _matmul(o_t, wout_g, mode="tn", m=t_rows, n=d, k=ATTN_WIDTH, tm=1024, tn=1024, tk=ATTN_WIDTH,
                    out_dtype=F32, name="out_proj_attn", b_k0=CONV_WIDTH, add=mixed)
    h1, u2 = _mid_forward(x2, mixed, norm_mix_post, norm_ffn_pre)
    _, (wup_t, wdown_g) = _scatter_wait(ffn_send, ffn_recv, ffn_shards, ffn_lands, u2, True, "gather_ffn_weights_wait")
    wup_t = wup_t.reshape(2 * D_FF, d)
    wdown_g = wdown_g.reshape(D_FF, d)
    hup =_matmul(u2, wup_t, mode="nt", m=t_rows, n=2 * D_FF, k=d, tm=2048, tn=1408, tk=d,
                  out_dtype=F32, name="ffn_up")
    act = _ffn_activation(hup, ffn_w_full, ffn_dw_b)
    f = _matmul(act, wdown_g, mode="nn", m=t_rows, n=d, k=D_FF, tm=1024, tn=1024, tk=D_FF,
                out_dtype=F32, name="ffn_down")
    loss, dy, df, d_norm_ffn_post = _loss_and_head_backward(h1, f, target, norm_ffn_post)

    dact = _matmul(df, wdown_g, mode="nt", m=t_rows, n=D_FF, k=d, tm=2048, tn=1408, tk=d,
                   out_dtype=F32, name="ffn_down_dx")
    g_wdown = _matmul(act, df, mode="tn", m=D_FF, n=d, k=t_rows, tm=1408, tn=1024, tk=2048,
                      out_dtype=F32, name="ffn_down_dw")
    dhg, dhv, dwg, dwv, dbg, dbv = _ffn_backward(dact, hup, ffn_w_full, ffn_dw_b)
    du2 = _matmul(dhg, wup_t, mode="nn", m=t_rows, n=d, k=D_FF, tm=1024, tn=1024, tk=D_FF,
                  out_dtype=F32, name="ffn_up_dx_gate")
    du2 = _matmul(dhv, wup_t, mode="nn", m=t_rows, n=d, k=D_FF, tm=1024, tn=1024, tk=D_FF,
                  out_dtype=F32, name="ffn_up_dx_value", b_k0=D_FF, add=du2)
    g_wup_t = _matmul(dhg, u2, mode="tn", m=D_FF, n=d, k=t_rows, tm=1408, tn=1024, tk=2048, out_dtype=F32,
                      name="ffn_up_dw_gate", out_rows=2 * D_FF)
    g_wup_t = _matmul(dhv, u2, mode="tn", m=D_FF, n=d, k=t_rows, tm=1408, tn=1024, tk=2048, out_dtype=F32,
                      name="ffn_up_dw_value", out_rows=2 * D_FF, out_m0=D_FF, into=g_wup_t)
    ffn_grads = [g_wup_t.reshape(N_DEV, 2 * D_FF // N_DEV, d), g_wdown.reshape(N_DEV, D_FF // N_DEV, d)]
    red_send, red_recv, ffn_grads, red_lands, red_token = _scatter_start(
        ffn_grads, [lax.empty(g.shape, F32) for g in ffn_grads], False, "reduce_ffn_grads_start")
    dh1, dmixed, d_norm_ffn_pre, d_norm_mix_post = _mid_backward(
        dy, du2, h1, mixed, norm_ffn_pre + red_token[0:1, 0:1], norm_mix_post)
    dconv_out = _matmul(dmixed, wout_g, mode="nt", m=t_rows, n=CONV_WIDTH, k=d, tm=2048, tn=512, tk=d,
                        out_dtype=F32, name="out_proj_dx_conv")
    do_t = _matmul(wout_g, dmixed, mode="nt", m=ATTN_WIDTH, n=t_rows, k=d, tm=512, tn=2048, tk=d,
                   out_dtype=BF16, name="out_proj_dx_attn", a_m0=CONV_WIDTH)
    g_wout = _matmul(conv_out, dmixed, mode="tn", m=CONV_WIDTH, n=d, k=t_rows, tm=512, tn=1024, tk=2048, out_dtype=F32,
                     name="out_proj_dw_conv", out_rows=d)
    g_wout = _matmul(o_t, dmixed, mode="nn", m=ATTN_WIDTH, n=d, k=t_rows, tm=512, tn=1024, tk=2048, out_dtype=F32,
                     name="out_proj_dw_attn", out_rows=d, out_m0=CONV_WIDTH, into=g_wout)
    dproj_a, d_conv_w, d_conv_b, d_ln_g, d_ln_b = _conv_backward(dconv_out, conv_c, proj_a, conv_w_full, conv_ln_g, conv_ln_b)
    dqkv_parts = _attn_backward(qkv_t, o_t, do_t, rel_bias[0])
    drel = dqkv_parts[3]
    du1 = _matmul(dproj_a, win_t, mode="nn", m=t_rows, n=d, k=2 * CONV_WIDTH, tm=2048, tn=1024, tk=1024,
                  out_dtype=F32, name="proj_dx_conv")
    g_win_t = _matmul(dproj_a, u1, mode="tn", m=2 * CONV_WIDTH, n=d, k=t_rows, tm=1024, tn=1024, tk=2048, out_dtype=F32,
                      name="proj_dw_conv", out_rows=in_cols)
    for j, part in enumerate("qkv"):
        row0 = 2 * CONV_WIDTH + j * ATTN_WIDTH
        du1 = _matmul(dqkv_parts[j], win_t, mode="tn", m=t_rows, n=d, k=ATTN_WIDTH, tm=1024, tn=1024, tk=ATTN_WIDTH,
                      out_dtype=F32, name="proj_dx_" + part, b_k0=row0, add=du1)
        g_win_t = _matmul(dqkv_parts[j], u1, mode="nn", m=ATTN_WIDTH, n=d, k=t_rows, tm=512, tn=1024, tk=2048,
                          out_dtype=F32, name="proj_dw_" + part, out_rows=in_cols, out_m0=row0, into=g_win_t)
    dx, d_norm_mix_pre = _input_backward(dh1, du1, x2, norm_mix_pre)

    small_grads = dict(norm_mix_pre=d_norm_mix_pre, conv_dw_b=d_conv_b, conv_ln_g=d_ln_g, conv_ln_b=d_ln_b,
                       rel_bias=drel[:, :2 * MAX_REL + 1], norm_mix_post=d_norm_mix_post, norm_ffn_pre=d_norm_ffn_pre,
                       ffn_dw_b=jnp.concatenate([dbg, dbv], axis=1), norm_ffn_post=d_norm_ffn_post)
    pieces = [small_grads[nm] for nm in SMALL] + [d_conv_w, jnp.concatenate([dwg, dwv], axis=1)]
    count = sum(p.size for p in pieces)
    (gathered_small,) = _all_gather([_pack(pieces, _rows_for(count))], "all_gather_small_grads")
    summed = _sum_devices(gathered_small)
    shapes = [weights[nm].shape for nm in SMALL] + [(CONV_K, CONV_WIDTH), (FFN_K, 2 * D_FF)]
    unpacked = _unpack(summed, shapes)
    grads = dict(zip(SMALL, unpacked[:len(SMALL)]))
    cw_shard, fw_shard = CONV_WIDTH // N_DEV, 2 * D_FF // N_DEV
    grads["conv_dw_w"] = lax.dynamic_slice_in_dim(unpacked[-2], my_dev * cw_shard, cw_shard, axis=1)[None]
    grads["ffn_dw_w"] = lax.dynamic_slice_in_dim(unpacked[-1], my_dev * fw_shard, fw_shard, axis=1)[None]

    late = ("w_in", "w_out")
    full = [g_win_t.reshape(N_DEV, in_cols // N_DEV, d), g_wout.reshape(N_DEV, d // N_DEV, d)]
    from_sibling = _exchange_in_chip(full)
    core = jnp.reshape(my_c, (1,)).astype(jnp.int32)
    chip = jnp.reshape(2 * my_x + my_y, (1,)).astype(jnp.int32)
    partials = [_add_in_chip(g, r, core, "add_in_chip_" + nm) for g, r, nm in zip(full, from_sibling, late)]
    from_chips = _exchange_between_chips(partials)
    reduced = [_add_between_chips(p, r, chip, "add_between_chips_" + nm) for p, r, nm in zip(partials, from_chips, late)]
    grads["w_in"] = reduced[0].T[None]
    grads["w_out"] = reduced[1][None]
    ffn_grads, red_lands = _scatter_wait(red_send, red_recv, ffn_grads, red_lands, dx, False, "reduce_ffn_grads_wait")
    me = jnp.reshape(my_dev, (1,)).astype(jnp.int32)
    grads["w_up"] = _sum_received(ffn_grads[0], red_lands[0], me, "sum_received_w_up").T[None]
    grads["w_down"] = _sum_received(ffn_grads[1], red_lands[1], me, "sum_received_w_down")[None]

    delta, new_m, new_v = {}, {}, {}
    for nm in LARGE:
        dl, nm1, nv1 = _adamw(weights[nm][0], grads[nm][0], mom1[nm][0], mom2[nm][0], "adamw_" + nm)
        delta[nm], new_m[nm], new_v[nm] = dl[None], nm1[None], nv1[None]
    small_names = SMALL + SHARDED_SMALL
    small_count = sum(weights[nm].size for nm in small_names)
    small_rows = _rows_for(small_count)
    packed = [_pack([src[nm] for nm in small_names], small_rows) for src in (weights, grads, mom1, mom2)]
    outs = _adamw(*packed, "adamw_small")
    small_shapes = [weights[nm].shape for nm in small_names]
    for store, arr in zip((delta, new_m, new_v), outs):
        store.update(zip(small_names, _unpack(arr, small_shapes)))

    total_loss = lax.psum(loss[0, 0], AXES)
    return (total_loss, dx[None], *[grads[nm] for nm in WEIGHTS], *[delta[nm] for nm in WEIGHTS],
            *[new_m[nm] for nm in WEIGHTS], *[new_v[nm] for nm in WEIGHTS])
```

```python
import jax
import jax.numpy as jnp
from jax import lax
from jax.experimental import pallas as pl
from jax.experimental.pallas import tpu as pltpu

F32 = jnp.float32
BF16 = jnp.bfloat16
MESH = pl.DeviceIdType.MESH
AXES = ("x", "y", "c")
N_DEV = 8

EPS = 1e-6
NEG_INF = -1e30
D_MODEL = 1024
CONV_WIDTH = 512
ATTN_WIDTH = 512
N_HEADS = 8
HEAD_DIM = 64
CHUNK = 64
LEFT = 8 * CHUNK
QBLK = 2 * CHUNK
WIN = LEFT + QBLK
CONV_K = 31
CONV_HALO = 32
FFN_K = 3
FFN_HALO = 8
D_FF = 2816
MAX_REL = 128
SCALE = HEAD_DIM ** -0.5
ADAM_LR, ADAM_B1, ADAM_B2, ADAM_EPS, ADAM_WD, ADAM_STEP = 0.001, 0.9, 0.999, 1e-08, 0.01, 10

V7X_VMEM_BYTES = 64 * 2**20
VMEM_LIMIT_BYTES = V7X_VMEM_BYTES - 8 * 2**20
LANES = 128


def _params(*sem):
    return pltpu.CompilerParams(dimension_semantics=sem or None, vmem_limit_bytes=VMEM_LIMIT_BYTES)


_DOT_DIMS = {"nn": (((1,), (0,)), ((), ())), "nt": (((1,), (1,)), ((), ())), "tn": (((0,), (0,)), ((), ()))}


def _matmul(a, b, *, mode, m, n, k, tm, tn, tk, out_dtype, name, a_m0=0, b_n0=0, b_k0=0, add=None,
            out_rows=None, out_m0=0, into=None):
    tm, tn, tk = min(tm, m), min(tn, n), min(tk, k)
    out_rows = m if out_rows is None else out_rows
    assert m % tm == 0 and n % tn == 0 and k % tk == 0, (name, m, n, k, tm, tn, tk)
    assert a_m0 % tm == 0 and b_n0 % tn == 0 and b_k0 % tk == 0 and out_m0 % tm == 0, name
    am, bn, bk, om = a_m0 // tm, b_n0 // tn, b_k0 // tk, out_m0 // tm
    gk = k // tk
    dims = _DOT_DIMS[mode]

    if mode == "tn":
        a_spec = pl.BlockSpec((tk, tm), lambda i, j, kk: (kk, i + am))
    else:
        a_spec = pl.BlockSpec((tm, tk), lambda i, j, kk: (i + am, kk))
    if mode == "nt":
        b_spec = pl.BlockSpec((tn, tk), lambda i, j, kk: (j + bn, kk + bk))
    else:
        b_spec = pl.BlockSpec((tk, tn), lambda i, j, kk: (kk + bk, j + bn))
    o_spec = pl.BlockSpec((tm, tn), lambda i, j, kk: (i + om, j))
    in_specs = [a_spec, b_spec]
    operands = [a, b]
    if add is not None:
        assert out_rows == m
        in_specs.append(o_spec)
        operands.append(add)
    aliases = {}
    if into is not None:
        aliases = {len(operands): 0}
        in_specs.append(pl.BlockSpec(memory_space=pl.ANY))
        operands.append(into)

    def body(*refs):
        a_ref, b_ref = refs[0], refs[1]
        add_ref = refs[2] if add is not None else None
        o_ref = refs[len(operands)]
        part = lax.dot_general(a_ref[...].astype(BF16), b_ref[...].astype(BF16), dims,
                               preferred_element_type=F32)

        def finish(total):
            if add_ref is not None:
                total = total + add_ref[...]
            o_ref[...] = total.astype(out_dtype)

        if gk == 1:
            finish(part)
        else:
            acc_ref = refs[-1]
            kk = pl.program_id(2)

            @pl.when(kk == 0)
            def _():
                acc_ref[...] = part

            @pl.when(kk > 0)
            def _():
                acc_ref[...] += part

            @pl.when(kk == gk - 1)
            def _():
                finish(acc_ref[...])

    return pl.pallas_call(
        body, name=name,
        grid=(m // tm, n // tn, gk),
        in_specs=in_specs, out_specs=o_spec,
        out_shape=jax.ShapeDtypeStruct((out_rows, n), out_dtype),
        scratch_shapes=[pltpu.VMEM((tm, tn), F32)] if gk > 1 else [],
        input_output_aliases=aliases,
        compiler_params=_params("parallel", "parallel", "arbitrary"),
    )(*operands)


def _rms_hat(v):
    r = lax.rsqrt(jnp.mean(v * v, axis=-1, keepdims=True) + EPS)
    return v * r, r


def _rms_bwd(dn, hat, r):
    return r * (dn - hat * jnp.mean(dn * hat, axis=-1, keepdims=True))


def _sigmoid(v):
    return 1.0 / (1.0 + jnp.exp(-v))


_GELU_C = 0.7978845608028654


def _gelu(v):
    return 0.5 * v * (1.0 + jnp.tanh(_GELU_C * (v + 0.044715 * v * (v * v))))


def _gelu_parts(v):
    v2 = v * v
    t = jnp.tanh(_GELU_C * (v + 0.044715 * v * v2))
    cdf = 0.5 * (1.0 + t)
    dcdf = 0.5 * (1.0 - t * t) * _GELU_C * (1.0 + 3.0 * 0.044715 * v2)
    return v * cdf, cdf + v * dcdf


def _row_tile(t_rows, want):
    tile = min(want, t_rows)
    assert t_rows % tile == 0
    return tile


def _pre_norm(x, g, name):
    t_rows, d = x.shape
    tm = _row_tile(t_rows, 512)

    def body(x_ref, g_ref, u_ref):
        hat, _ = _rms_hat(x_ref[...])
        u_ref[...] = (hat * g_ref[...]).astype(BF16)

    return pl.pallas_call(
        body, name=name, grid=(t_rows // tm,),
        in_specs=[pl.BlockSpec((tm, d), lambda i: (i, 0)), pl.BlockSpec((1, d), lambda i: (0, 0))],
        out_specs=pl.BlockSpec((tm, d), lambda i: (i, 0)),
        out_shape=jax.ShapeDtypeStruct((t_rows, d), BF16),
        compiler_params=_params("parallel"),
    )(x, g)


def _mid_forward(x, mixed, g_post, g_pre):
    t_rows, d = x.shape
    tm = _row_tile(t_rows, 512)

    def body(x_ref, mixed_ref, gpost_ref, gpre_ref, h1_ref, u2_ref):
        hat, _ = _rms_hat(mixed_ref[...])
        h1 = x_ref[...] + hat * gpost_ref[...]
        h1_ref[...] = h1
        hat1, _ = _rms_hat(h1)
        u2_ref[...] = (hat1 * gpre_ref[...]).astype(BF16)

    row = pl.BlockSpec((tm, d), lambda i: (i, 0))
    vec = pl.BlockSpec((1, d), lambda i: (0, 0))
    return pl.pallas_call(
        body, name="mid_forward", grid=(t_rows // tm,),
        in_specs=[row, row, vec, vec], out_specs=[row, row],
        out_shape=[jax.ShapeDtypeStruct((t_rows, d), F32), jax.ShapeDtypeStruct((t_rows, d), BF16)],
        compiler_params=_params("parallel"),
    )(x, mixed, g_post, g_pre)


def _loss_and_head_backward(h1, f, target, g_post):
    t_rows, d = h1.shape
    tm = _row_tile(t_rows, 512)
    nt = t_rows // tm

    def body(h1_ref, f_ref, tgt_ref, g_ref, loss_ref, dy_ref, df_ref, dg_ref, sq_ref):
        i = pl.program_id(0)

        @pl.when(i == 0)
        def _():
            sq_ref[...] = jnp.zeros_like(sq_ref)
            dg_ref[...] = jnp.zeros_like(dg_ref)

        g = g_ref[...]
        hat, r = _rms_hat(f_ref[...])
        err = h1_ref[...] + hat * g - tgt_ref[...]
        sq_ref[...] += jnp.sum(err * err, axis=0, keepdims=True)
        dy = err * (1.0 / d)
        dy_ref[...] = dy
        dg_ref[...] += jnp.sum(dy * hat, axis=0, keepdims=True)
        df_ref[...] = _rms_bwd(dy * g, hat, r).astype(BF16)

        @pl.when(i == nt - 1)
        def _():
            loss_ref[...] = (0.5 / d) * jnp.sum(sq_ref[...], axis=1, keepdims=True)

    row = pl.BlockSpec((tm, d), lambda i: (i, 0))
    vec = pl.BlockSpec((1, d), lambda i: (0, 0))
    return pl.pallas_call(
        body, name="loss_head_backward", grid=(nt,),
        in_specs=[row, row, row, vec],
        out_specs=[pl.BlockSpec((1, 1), lambda i: (0, 0)), row, row, vec],
        out_shape=[jax.ShapeDtypeStruct((1, 1), F32), jax.ShapeDtypeStruct((t_rows, d), F32),
                   jax.ShapeDtypeStruct((t_rows, d), BF16), jax.ShapeDtypeStruct((1, d), F32)],
        scratch_shapes=[pltpu.VMEM((1, d), F32)],
        compiler_params=_params("arbitrary"),
    )(h1, f, target, g_post)


def _mid_backward(dy, du2, h1, mixed, g_pre, g_post):
    t_rows, d = dy.shape
    tm = _row_tile(t_rows, 512)

    def body(dy_ref, du2_ref, h1_ref, mixed_ref, gpre_ref, gpost_ref, dh1_ref, dmixed_ref, dgpre_ref, dgpost_ref):
        @pl.when(pl.program_id(0) == 0)
        def _():
            dgpre_ref[...] = jnp.zeros_like(dgpre_ref)
            dgpost_ref[...] = jnp.zeros_like(dgpost_ref)

        du2 = du2_ref[...]
        hat1, r1 = _rms_hat(h1_ref[...])
        dgpre_ref[...] += jnp.sum(du2 * hat1, axis=0, keepdims=True)
        dh1 = dy_ref[...] + _rms_bwd(du2 * gpre_ref[...], hat1, r1)
        dh1_ref[...] = dh1
        hatm, rm = _rms_hat(mixed_ref[...])
        dgpost_ref[...] += jnp.sum(dh1 * hatm, axis=0, keepdims=True)
        dmixed_ref[...] = _rms_bwd(dh1 * gpost_ref[...], hatm, rm).astype(BF16)

    row = pl.BlockSpec((tm, d), lambda i: (i, 0))
    vec = pl.BlockSpec((1, d), lambda i: (0, 0))
    return pl.pallas_call(
        body, name="mid_backward", grid=(t_rows // tm,),
        in_specs=[row, row, row, row, vec, vec], out_specs=[row, row, vec, vec],
        out_shape=[jax.ShapeDtypeStruct((t_rows, d), F32), jax.ShapeDtypeStruct((t_rows, d), BF16),
                   jax.ShapeDtypeStruct((1, d), F32), jax.ShapeDtypeStruct((1, d), F32)],
        compiler_params=_params("arbitrary"),
    )(dy, du2, h1, mixed, g_pre, g_post)


def _input_backward(dh1, du1, x, g_pre):
    t_rows, d = x.shape
    tm = _row_tile(t_rows, 512)

    def body(dh1_ref, du1_ref, x_ref, g_ref, dx_ref, dg_ref):
        @pl.when(pl.program_id(0) == 0)
        def _():
            dg_ref[...] = jnp.zeros_like(dg_ref)

        du1 = du1_ref[...]
        hat, r = _rms_hat(x_ref[...])
        dg_ref[...] += jnp.sum(du1 * hat, axis=0, keepdims=True)
        dx_ref[...] = dh1_ref[...] + _rms_bwd(du1 * g_ref[...], hat, r)

    row = pl.BlockSpec((tm, d), lambda i: (i, 0))
    vec = pl.BlockSpec((1, d), lambda i: (0, 0))
    return pl.pallas_call(
        body, name="input_backward", grid=(t_rows // tm,),
        in_specs=[row, row, row, vec], out_specs=[row, vec],
        out_shape=[jax.ShapeDtypeStruct((t_rows, d), F32), jax.ShapeDtypeStruct((1, d), F32)],
        compiler_params=_params("arbitrary"),
    )(dh1, du1, x, g_pre)


CONV_STRIP = 32


def _glu(block):
    return block[:, :CONV_WIDTH] * _sigmoid(block[:, CONV_WIDTH:])


def _layer_norm_parts(c):
    mu = jnp.mean(c, axis=-1, keepdims=True)
    xc = c - mu
    r = lax.rsqrt(jnp.mean(xc * xc, axis=-1, keepdims=True) + EPS)
    return xc * r, r


def _conv_forward(proj_a, w, b, ln_g, ln_b):
    t_rows = proj_a.shape[0]
    tm = _row_tile(t_rows, 512)
    hb = tm // CONV_HALO
    cw = CONV_WIDTH

    def body(cur_ref, prev_ref, w_ref, b_ref, g_ref, beta_ref, c_ref, out_ref, hbuf):
        i = pl.program_id(0)
        hbuf[0:CONV_HALO, :] = jnp.where(i > 0, _glu(prev_ref[...]), 0.0)
        hbuf[CONV_HALO:, :] = _glu(cur_ref[...])

        def strip(s, carry):
            base = pl.multiple_of(s * CONV_STRIP, CONV_STRIP)
            v = hbuf[pl.ds(base, 2 * CONV_STRIP), :]
            acc = jnp.broadcast_to(b_ref[...], (CONV_STRIP, cw))
            off = CONV_HALO - (CONV_K - 1)
            for kk in range(CONV_K):
                acc = acc + w_ref[kk:kk + 1, :] * v[off + kk:off + kk + CONV_STRIP, :]
            c_ref[pl.ds(base, CONV_STRIP), :] = acc
            hat, _ = _layer_norm_parts(acc)
            z = hat * g_ref[...] + beta_ref[...]
            out_ref[pl.ds(base, CONV_STRIP), :] = (z * _sigmoid(z)).astype(BF16)
            return carry

        lax.fori_loop(0, tm // CONV_STRIP, strip, 0)

    vec = pl.BlockSpec((1, cw), lambda i: (0, 0))
    return pl.pallas_call(
        body, name="conv_forward", grid=(t_rows // tm,),
        in_specs=[pl.BlockSpec((tm, 2 * cw), lambda i: (i, 0)),
                  pl.BlockSpec((CONV_HALO, 2 * cw), lambda i: (jnp.maximum(i * hb - 1, 0), 0)),
                  pl.BlockSpec((CONV_K, cw), lambda i: (0, 0)), vec, vec, vec],
        out_specs=[pl.BlockSpec((tm, cw), lambda i: (i, 0)), pl.BlockSpec((tm, cw), lambda i: (i, 0))],
        out_shape=[jax.ShapeDtypeStruct((t_rows, cw), F32), jax.ShapeDtypeStruct((t_rows, cw), BF16)],
        scratch_shapes=[pltpu.VMEM((tm + CONV_HALO, cw), F32)],
        compiler_params=_params("parallel"),
    )(proj_a, proj_a, w, b, ln_g, ln_b)


def _conv_backward(dout, c, proj_a, w, ln_g, ln_b):
    t_rows = c.shape[0]
    tm = _row_tile(t_rows, 512)
    hb = tm // CONV_HALO
    nt = t_rows // tm
    last_halo = t_rows // CONV_HALO - 1
    cw = CONV_WIDTH

    def body(dout_ref, dout_next_ref, c_ref, c_next_ref, cur_ref, prev_ref, w_ref, g_ref, beta_ref,
             dproj_ref, dw_ref, db_ref, dg_ref, dbeta_ref, hbuf, dcbuf, dwacc):
        i = pl.program_id(0)

        @pl.when(i == 0)
        def _():
            dwacc[...] = jnp.zeros_like(dwacc)
            db_ref[...] = jnp.zeros_like(db_ref)
            dg_ref[...] = jnp.zeros_like(dg_ref)
            dbeta_ref[...] = jnp.zeros_like(dbeta_ref)

        def ln_swish_backward(dout_v, c_v):
            hat, r = _layer_norm_parts(c_v)
            g = g_ref[...]
            z = hat * g + beta_ref[...]
            sg = _sigmoid(z)
            dz = dout_v * (sg * (1.0 + z * (1.0 - sg)))
            dhat = dz * g
            dc = r * (dhat - jnp.mean(dhat, axis=-1, keepdims=True)
                      - hat * jnp.mean(dhat * hat, axis=-1, keepdims=True))
            return dc, dz, hat

        dc, dz, hat = ln_swish_backward(dout_ref[...], c_ref[...])
        dg_ref[...] += jnp.sum(dz * hat, axis=0, keepdims=True)
        dbeta_ref[...] += jnp.sum(dz, axis=0, keepdims=True)
        db_ref[...] += jnp.sum(dc, axis=0, keepdims=True)
        dcbuf[0:tm, :] = dc
        dc_next, _, _ = ln_swish_backward(dout_next_ref[...], c_next_ref[...])
        dcbuf[tm:, :] = jnp.where(i < nt - 1, dc_next, 0.0)

        hbuf[0:CONV_HALO, :] = jnp.where(i > 0, _glu(prev_ref[...]), 0.0)
        hbuf[CONV_HALO:, :] = _glu(cur_ref[...])

        def strip(s, carry):
            base = pl.multiple_of(s * CONV_STRIP, CONV_STRIP)
            dv = dcbuf[pl.ds(base, 2 * CONV_STRIP), :]
            hv = hbuf[pl.ds(base, 2 * CONV_STRIP), :]
            dcs = dv[0:CONV_STRIP, :]
            dh = jnp.zeros((CONV_STRIP, cw), F32)
            off = CONV_HALO - (CONV_K - 1)
            for kk in range(CONV_K):
                back = CONV_K - 1 - kk
                dh = dh + w_ref[kk:kk + 1, :] * dv[back:back + CONV_STRIP, :]
                prod = dcs * hv[off + kk:off + kk + CONV_STRIP, :]
                dwacc[kk] += jnp.sum(prod.reshape(CONV_STRIP // 8, 8, cw), axis=0)
            blk = cur_ref[pl.ds(base, CONV_STRIP), :]
            val, sg = blk[:, :cw], _sigmoid(blk[:, cw:])
            dproj_ref[pl.ds(base, CONV_STRIP), 0:cw] = (dh * sg).astype(BF16)
            dproj_ref[pl.ds(base, CONV_STRIP), cw:2 * cw] = (dh * val * sg * (1.0 - sg)).astype(BF16)
            return carry

        lax.fori_loop(0, tm // CONV_STRIP, strip, 0)

        @pl.when(i == nt - 1)
        def _():
            for kk in range(CONV_K):
                dw_ref[kk:kk + 1, :] = jnp.sum(dwacc[kk], axis=0, keepdims=True)

    vec = pl.BlockSpec((1, cw), lambda i: (0, 0))
    cur = lambda width: pl.BlockSpec((tm, width), lambda i: (i, 0))
    nxt = lambda width: pl.BlockSpec((CONV_HALO, width), lambda i: (jnp.minimum((i + 1) * hb, last_halo), 0))
    return pl.pallas_call(
        body, name="conv_backward", grid=(nt,),
        in_specs=[cur(cw), nxt(cw), cur(cw), nxt(cw), cur(2 * cw),
                  pl.BlockSpec((CONV_HALO, 2 * cw), lambda i: (jnp.maximum(i * hb - 1, 0), 0)),
                  pl.BlockSpec((CONV_K, cw), lambda i: (0, 0)), vec, vec],
        out_specs=[cur(2 * cw), pl.BlockSpec((CONV_K, cw), lambda i: (0, 0)), vec, vec, vec],
        out_shape=[jax.ShapeDtypeStruct((t_rows, 2 * cw), BF16), jax.ShapeDtypeStruct((CONV_K, cw), F32),
                   jax.ShapeDtypeStruct((1, cw), F32), jax.ShapeDtypeStruct((1, cw), F32),
                   jax.ShapeDtypeStruct((1, cw), F32)],
        scratch_shapes=[pltpu.VMEM((tm + CONV_HALO, cw), F32), pltpu.VMEM((tm + CONV_HALO, cw), F32),
                        pltpu.VMEM((CONV_K, 8, cw), F32)],
        compiler_params=_params("arbitrary"),
    )(dout, dout, c, c, proj_a, proj_a, w, ln_g, ln_b)


def _attn_load_kv(kv_hbm, k_pad, v_pad, sem, t_cols):
    k_pad[:, 0:LEFT] = jnp.zeros((ATTN_WIDTH, LEFT), BF16)
    v_pad[:, 0:LEFT] = jnp.zeros((ATTN_WIDTH, LEFT), BF16)
    ck = pltpu.make_async_copy(kv_hbm.at[pl.ds(ATTN_WIDTH, ATTN_WIDTH), :], k_pad.at[:, pl.ds(LEFT, t_cols)], sem.at[0])
    cv = pltpu.make_async_copy(kv_hbm.at[pl.ds(2 * ATTN_WIDTH, ATTN_WIDTH), :], v_pad.at[:, pl.ds(LEFT, t_cols)], sem.at[1])
    ck.start()
    cv.start()
    ck.wait()
    cv.wait()


def _attn_build_bias(tab_ref, bias_t):
    row = lax.broadcasted_iota(jnp.int32, (LANES, LANES), 0)
    lane = lax.broadcasted_iota(jnp.int32, (LANES, LANES), 1)
    upper = lane >= row
    lane64 = lax.broadcasted_iota(jnp.int32, (CHUNK, LANES), 1)
    for h in range(N_HEADS):
        far = jnp.broadcast_to(tab_ref[h:h + 1, 2 * MAX_REL:2 * MAX_REL + 1], (LANES, LANES))
        hi = jnp.broadcast_to(tab_ref[h:h + 1, MAX_REL:2 * MAX_REL], (LANES, LANES))
        lo = jnp.broadcast_to(tab_ref[h:h + 1, 0:MAX_REL], (LANES, LANES))
        hi_d = pltpu.roll(hi, 0, 1, stride=1, stride_axis=0)
        lo_d = pltpu.roll(lo, 0, 1, stride=1, stride_axis=0)
        bias_t[h, 0:WIN - 2 * LANES, :] = jnp.broadcast_to(far[0:1, :], (WIN - 2 * LANES, LANES))
        bias_t[h, WIN - 2 * LANES:WIN - LANES, :] = jnp.where(upper, far, hi_d)
        bias_t[h, WIN - LANES:WIN, :] = jnp.where(upper, hi_d, lo_d)
        bias_t[h, 0:CHUNK, :] = jnp.where(lane64 < CHUNK, bias_t[h, 0:CHUNK, :], NEG_INF)
        bias_t[h, WIN - CHUNK:WIN, :] = jnp.where(lane64 >= CHUNK, bias_t[h, WIN - CHUNK:WIN, :], NEG_INF)


def _attn_probs(k_h, q_h, bias_h, first_valid):
    s = lax.dot_general(k_h, q_h, _DOT_DIMS["tn"], preferred_element_type=F32) * SCALE + bias_h
    key = lax.broadcasted_iota(jnp.int32, s.shape, 0)
    s = jnp.where(key >= first_valid, s, NEG_INF)
    e = jnp.exp(s - jnp.max(s, axis=0, keepdims=True))
    return e * (1.0 / jnp.sum(e, axis=0, keepdims=True))


def _attn_forward(qkv_t, rel_bias):
    t_cols = qkv_t.shape[1]
    steps = t_cols // QBLK

    def body(q_ref, kv_hbm, tab_ref, o_ref, k_pad, v_pad, bias_t, sem):
        m = pl.program_id(0)

        @pl.when(m == 0)
        def _():
            _attn_build_bias(tab_ref, bias_t)
            _attn_load_kv(kv_hbm, k_pad, v_pad, sem, t_cols)

        w0 = pl.multiple_of(m * QBLK, QBLK)
        first_valid = LEFT - m * QBLK
        for h in range(N_HEADS):
            rows = slice(h * HEAD_DIM, (h + 1) * HEAD_DIM)
            k_h = k_pad[rows, pl.ds(w0, WIN)]
            v_h = v_pad[rows, pl.ds(w0, WIN)]
            p = _attn_probs(k_h, q_ref[rows, :], bias_t[h], first_valid)
            o_h = lax.dot_general(v_h, p.astype(BF16), _DOT_DIMS["nn"], preferred_element_type=F32)
            o_ref[rows, :] = o_h.astype(BF16)

    return pl.pallas_call(
        body, name="attn_forward", grid=(steps,),
        in_specs=[pl.BlockSpec((ATTN_WIDTH, QBLK), lambda m: (0, m)),
                  pl.BlockSpec(memory_space=pl.ANY),
                  pl.BlockSpec((N_HEADS, 2 * MAX_REL + 1), lambda m: (0, 0))],
        out_specs=pl.BlockSpec((ATTN_WIDTH, QBLK), lambda m: (0, m)),
        out_shape=jax.ShapeDtypeStruct((ATTN_WIDTH, t_cols), BF16),
        scratch_shapes=[pltpu.VMEM((ATTN_WIDTH, LEFT + t_cols), BF16), pltpu.VMEM((ATTN_WIDTH, LEFT + t_cols), BF16),
                        pltpu.VMEM((N_HEADS, WIN, QBLK), F32), pltpu.SemaphoreType.DMA((2,))],
        compiler_params=_params("arbitrary"),
    )(qkv_t, qkv_t, rel_bias)


def _reverse_lanes(v, flip):
    out = jnp.zeros(v.shape, F32)
    rest = v
    for _ in range(3):
        piece = rest.astype(BF16)
        out = out + lax.dot_general(piece, flip, _DOT_DIMS["nn"], preferred_element_type=F32)
        rest = rest - piece.astype(F32)
    return out


def _attn_bias_grad(dbias_t, drel_ref):
    row = lax.broadcasted_iota(jnp.int32, (LANES, LANES), 0)
    lane = lax.broadcasted_iota(jnp.int32, (LANES, LANES), 1)
    flip = (row + lane == LANES - 1).astype(BF16)
    head = lax.broadcasted_iota(jnp.int32, (N_HEADS, LANES), 0)
    lane8 = lax.broadcasted_iota(jnp.int32, (N_HEADS, LANES), 1)
    upper_rev = jnp.zeros((N_HEADS, LANES), F32)
    lower_rev = jnp.zeros((N_HEADS, LANES), F32)
    far = jnp.zeros((N_HEADS, LANES), F32)
    for h in range(N_HEADS):
        def diagonals(block):
            skew = pltpu.roll(_reverse_lanes(block, flip), 0, 1, stride=1, stride_axis=0)
            pos = jnp.sum(jnp.where(lane >= row, skew, 0.0), axis=0, keepdims=True)
            neg = jnp.sum(jnp.where(lane < row, skew, 0.0), axis=0, keepdims=True)
            return pos, neg

        pos4, neg4 = diagonals(dbias_t[h, WIN - LANES:WIN, :])
        pos3, neg3 = diagonals(dbias_t[h, WIN - 2 * LANES:WIN - LANES, :])
        far_h = jnp.sum(jnp.sum(dbias_t[h, 0:WIN - 2 * LANES, :], axis=0, keepdims=True), axis=1, keepdims=True)
        far_h = far_h + jnp.sum(pos3, axis=1, keepdims=True)
        upper_rev = jnp.where(head == h, pos4 + neg3, upper_rev)
        lower_rev = jnp.where(head == h, neg4, lower_rev)
        far = jnp.where((head == h) & (lane8 == 0), far_h, far)
    drel_ref[:, 0:LANES] = _reverse_lanes(lower_rev, flip)
    drel_ref[:, LANES:2 * LANES] = _reverse_lanes(upper_rev, flip)
    drel_ref[:, 2 * LANES:3 * LANES] = far


def _attn_backward(qkv_t, o_t, do_t, rel_bias):
    t_cols = qkv_t.shape[1]
    steps = t_cols // QBLK
    flush = LEFT // QBLK
    total = steps + flush

    def body(q_ref, o_ref, do_ref, kv_hbm, tab_ref, dq_ref, dk_ref, dv_ref, drel_ref,
             k_pad, v_pad, bias_t, dbias_t, dk_acc, dv_acc, sem):
        m = pl.program_id(0)

        @pl.when(m == 0)
        def _():
            _attn_build_bias(tab_ref, bias_t)
            _attn_load_kv(kv_hbm, k_pad, v_pad, sem, t_cols)
            dbias_t[...] = jnp.zeros_like(dbias_t)
            dk_acc[...] = jnp.zeros_like(dk_acc)
            dv_acc[...] = jnp.zeros_like(dv_acc)

        @pl.when(m < steps)
        def _():
            w0 = pl.multiple_of(m * QBLK, QBLK)
            first_valid = LEFT - m * QBLK
            for h in range(N_HEADS):
                rows = slice(h * HEAD_DIM, (h + 1) * HEAD_DIM)
                k_h = k_pad[rows, pl.ds(w0, WIN)]
                v_h = v_pad[rows, pl.ds(w0, WIN)]
                q_h = q_ref[rows, :]
                do_h = do_ref[rows, :]
                p = _attn_probs(k_h, q_h, bias_t[h], first_valid)
                dp = lax.dot_general(v_h, do_h, _DOT_DIMS["tn"], preferred_element_type=F32)
                delta = jnp.sum(do_h.astype(F32) * o_ref[rows, :].astype(F32), axis=0, keepdims=True)
                ds = p * (dp - delta)
                dbias_t[h] += ds
                ds_b = (ds * SCALE).astype(BF16)
                dq_h = lax.dot_general(k_h, ds_b, _DOT_DIMS["nn"], preferred_element_type=F32)
                dq_ref[rows, :] = dq_h.astype(BF16)
                dk_acc[rows, :] += lax.dot_general(q_h, ds_b, _DOT_DIMS["nt"], preferred_element_type=F32)
                dv_acc[rows, :] += lax.dot_general(do_h, p.astype(BF16), _DOT_DIMS["nt"], preferred_element_type=F32)

        dk_ref[...] = dk_acc[:, 0:QBLK].astype(BF16)
        dv_ref[...] = dv_acc[:, 0:QBLK].astype(BF16)
        for acc in (dk_acc, dv_acc):
            rest = acc[:, QBLK:WIN]
            acc[:, 0:LEFT] = rest
            acc[:, LEFT:WIN] = jnp.zeros((ATTN_WIDTH, QBLK), F32)

        @pl.when(m == total - 1)
        def _():
            _attn_bias_grad(dbias_t, drel_ref)

    qblk = pl.BlockSpec((ATTN_WIDTH, QBLK), lambda m: (0, jnp.minimum(m, steps - 1)))
    kblk = pl.BlockSpec((ATTN_WIDTH, QBLK), lambda m: (0, jnp.maximum(m - flush, 0)))
    dq, dk, dv, drel = pl.pallas_call(
        body, name="attn_backward", grid=(total,),
        in_specs=[qblk, qblk, qblk, pl.BlockSpec(memory_space=pl.ANY),
                  pl.BlockSpec((N_HEADS, 2 * MAX_REL + 1), lambda m: (0, 0))],
        out_specs=[qblk, kblk, kblk, pl.BlockSpec((N_HEADS, 3 * LANES), lambda m: (0, 0))],
        out_shape=[jax.ShapeDtypeStruct((ATTN_WIDTH, t_cols), BF16)] * 3
        + [jax.ShapeDtypeStruct((N_HEADS, 3 * LANES), F32)],
        scratch_shapes=[pltpu.VMEM((ATTN_WIDTH, LEFT + t_cols), BF16), pltpu.VMEM((ATTN_WIDTH, LEFT + t_cols), BF16),
                        pltpu.VMEM((N_HEADS, WIN, QBLK), F32), pltpu.VMEM((N_HEADS, WIN, QBLK), F32),
                        pltpu.VMEM((ATTN_WIDTH, WIN), F32), pltpu.VMEM((ATTN_WIDTH, WIN), F32),
                        pltpu.SemaphoreType.DMA((2,))],
        compiler_params=_params("arbitrary"),
    )(qkv_t, o_t, do_t, qkv_t, rel_bias)
    return dq, dk, dv, drel


FFN_TC = 256


def _ffn_specs(t_rows, tr):
    nj = D_FF // FFN_TC
    hb = tr // FFN_HALO
    last_halo = t_rows // FFN_HALO - 1
    cur = lambda off: pl.BlockSpec((tr, FFN_TC), lambda j, i: (i, j + off))
    prev = lambda off: pl.BlockSpec((FFN_HALO, FFN_TC), lambda j, i: (jnp.maximum(i * hb - 1, 0), j + off))
    nxt = lambda off: pl.BlockSpec((FFN_HALO, FFN_TC), lambda j, i: (jnp.minimum((i + 1) * hb, last_halo), j + off))
    wspec = lambda off: pl.BlockSpec((FFN_K, FFN_TC), lambda j, i: (0, j + off))
    bspec = lambda off: pl.BlockSpec((1, FFN_TC), lambda j, i: (0, j + off))
    return nj, cur, prev, nxt, wspec, bspec


FFN_STRIP = 16


def _ffn_conv(win, w, b, rows):
    out = b + w[2] * win[FFN_HALO:FFN_HALO + rows, :]
    out = out + w[1] * win[FFN_HALO - 1:FFN_HALO - 1 + rows, :]
    return out + w[0] * win[FFN_HALO - 2:FFN_HALO - 2 + rows, :]


def _taps(w_ref):
    return [w_ref[kk:kk + 1, :] for kk in range(FFN_K)]


def _ffn_first_window(prev_ref, cur_ref, tile, rows):
    return jnp.concatenate([jnp.where(tile > 0, prev_ref[...], 0.0), cur_ref[0:rows, :]], axis=0)


def _fold8(v):
    return jnp.sum(v.reshape(v.shape[0] // 8, 8, v.shape[1]), axis=0)


def _ffn_activation(hup, w, b):
    t_rows = hup.shape[0]
    tr = _row_tile(t_rows, 512)
    nj, cur, prev, nxt, wspec, bspec = _ffn_specs(t_rows, tr)
    rs = 2 * FFN_STRIP

    def body(g_ref, gprev_ref, v_ref, vprev_ref, wg_ref, wv_ref, bg_ref, bv_ref, act_ref):
        i = pl.program_id(1)
        wg, wv, bg, bv = _taps(wg_ref), _taps(wv_ref), bg_ref[...], bv_ref[...]

        def emit(base, g_win, v_win):
            act = _gelu(_ffn_conv(g_win, wg, bg, rs)) * _ffn_conv(v_win, wv, bv, rs)
            act_ref[pl.ds(base, rs), :] = act.astype(BF16)

        def strip(s, carry):
            base = pl.multiple_of(s * rs, rs)
            emit(base, g_ref[pl.ds(base - FFN_HALO, rs + FFN_HALO), :], v_ref[pl.ds(base - FFN_HALO, rs + FFN_HALO), :])
            return carry

        emit(0, _ffn_first_window(gprev_ref, g_ref, i, rs), _ffn_first_window(vprev_ref, v_ref, i, rs))
        lax.fori_loop(1, tr // rs, strip, 0)

    return pl.pallas_call(
        body, name="ffn_activation", grid=(nj, t_rows // tr),
        in_specs=[cur(0), prev(0), cur(nj), prev(nj), wspec(0), wspec(nj), bspec(0), bspec(nj)],
        out_specs=cur(0),
        out_shape=jax.ShapeDtypeStruct((t_rows, D_FF), BF16),
        compiler_params=_params("parallel", "parallel"),
    )(hup, hup, hup, hup, w, w, b, b)


def _ffn_backward(dact, hup, w, b):
    t_rows = hup.shape[0]
    tr = _row_tile(t_rows, 512)
    nj, cur, prev, nxt, wspec, bspec = _ffn_specs(t_rows, tr)
    ni = t_rows // tr
    rs = FFN_STRIP
    ns = tr // rs

    def body(da_ref, danext_ref, g_ref, gprev_ref, gnext_ref, v_ref, vprev_ref, vnext_ref,
             wg_ref, wv_ref, bg_ref, bv_ref,
             dhg_ref, dhv_ref, dwg_ref, dwv_ref, dbg_ref, dbv_ref):
        i = pl.program_id(1)

        @pl.when(i == 0)
        def _():
            for ref in (dwg_ref, dwv_ref, dbg_ref, dbv_ref):
                ref[...] = jnp.zeros_like(ref)

        wg, wv, bg, bv = _taps(wg_ref), _taps(wv_ref), bg_ref[...], bv_ref[...]

        def conv_grads(da, g_win, v_win, rows):
            gel, dgel = _gelu_parts(_ffn_conv(g_win, wg, bg, rows))
            return da * _ffn_conv(v_win, wv, bv, rows) * dgel, da * gel

        da_after = jnp.where(i < ni - 1, danext_ref[...], 0.0)
        g_after = jnp.concatenate([g_ref[tr - FFN_HALO:tr, :], gnext_ref[...]], axis=0)
        v_after = jnp.concatenate([v_ref[tr - FFN_HALO:tr, :], vnext_ref[...]], axis=0)
        dcg_after, dcv_after = conv_grads(da_after, g_after, v_after, FFN_HALO)
        zero8 = jnp.zeros((8, FFN_TC), F32)

        def strip_at(base, g_win, v_win, carry):
            dcg_after, dcv_after, dbg, dbv, dwg, dwv = carry
            dcg, dcv = conv_grads(da_ref[pl.ds(base, rs), :], g_win, v_win, rs)
            out = []
            for dc, after, taps, win, dh_ref, db, dw in ((dcg, dcg_after, wg, g_win, dhg_ref, dbg, dwg),
                                                         (dcv, dcv_after, wv, v_win, dhv_ref, dbv, dwv)):
                ext = jnp.concatenate([dc, after], axis=0)
                dh = taps[2] * dc + taps[1] * ext[1:1 + rs, :] + taps[0] * ext[2:2 + rs, :]
                dh_ref[pl.ds(base, rs), :] = dh.astype(BF16)
                db = db + _fold8(dc)
                dw = tuple(dw[kk] + _fold8(dc * win[FFN_HALO - 2 + kk:FFN_HALO - 2 + kk + rs, :]) for kk in range(FFN_K))
                out.append((dc[0:FFN_HALO, :], db, dw))
            return out[0][0], out[1][0], out[0][1], out[1][1], out[0][2], out[1][2]

        def strip(s, carry):
            base = pl.multiple_of((ns - 1 - s) * rs, rs)
            return strip_at(base, g_ref[pl.ds(base - FFN_HALO, rs + FFN_HALO), :],
                            v_ref[pl.ds(base - FFN_HALO, rs + FFN_HALO), :], carry)

        init = (dcg_after, dcv_after, zero8, zero8, (zero8,) * FFN_K, (zero8,) * FFN_K)
        carry = lax.fori_loop(0, ns - 1, strip, init)
        _, _, dbg, dbv, dwg, dwv = strip_at(0, _ffn_first_window(gprev_ref, g_ref, i, rs),
                                            _ffn_first_window(vprev_ref, v_ref, i, rs), carry)
        dbg_ref[...] += jnp.sum(dbg, axis=0, keepdims=True)
        dbv_ref[...] += jnp.sum(dbv, axis=0, keepdims=True)
        for kk in range(FFN_K):
            dwg_ref[kk:kk + 1, :] += jnp.sum(dwg[kk], axis=0, keepdims=True)
            dwv_ref[kk:kk + 1, :] += jnp.sum(dwv[kk], axis=0, keepdims=True)

    half = jax.ShapeDtypeStruct((t_rows, D_FF), BF16)
    return pl.pallas_call(
        body, name="ffn_backward", grid=(nj, ni),
        in_specs=[cur(0), nxt(0), cur(0), prev(0), nxt(0), cur(nj), prev(nj), nxt(nj),
                  wspec(0), wspec(nj), bspec(0), bspec(nj)],
        out_specs=[cur(0), cur(0), wspec(0), wspec(0), bspec(0), bspec(0)],
        out_shape=[half, half, jax.ShapeDtypeStruct((FFN_K, D_FF), F32), jax.ShapeDtypeStruct((FFN_K, D_FF), F32),
                   jax.ShapeDtypeStruct((1, D_FF), F32), jax.ShapeDtypeStruct((1, D_FF), F32)],
        compiler_params=_params("parallel", "arbitrary"),
    )(dact, dact, hup, hup, hup, hup, hup, hup, w, w, b, b)


def _mesh_position():
    return lax.axis_index("x"), lax.axis_index("y"), lax.axis_index("c")


def _hbm_specs(n):
    return [pl.BlockSpec(memory_space=pl.ANY)] * n


def _all_gather(shards, name, place=()):
    n = len(shards)
    total = n + len(place)

    def body(*refs):
        ins, outs = refs[:total], refs[total:2 * total]
        send_sems, recv_sems, local_sems = refs[2 * total:]
        x, y, c = _mesh_position()
        me, sibling = (x, y, c), (x, y, 1 - c)
        chips = [(1 - x, y), (x, 1 - y), (1 - x, 1 - y)]

        def copy(a, slot, block, to, src=None):
            dst = outs[a].at[4 * block[0] + 2 * block[1] + block[2]]
            return pltpu.make_async_remote_copy(
                src_ref=dst if src is None else src, dst_ref=dst,
                send_sem=send_sems.at[a, slot], recv_sem=recv_sems.at[a, slot],
                device_id=to, device_id_type=MESH)

        started = []
        for a in range(total):
            mine = pltpu.make_async_copy(ins[a], outs[a].at[4 * x + 2 * y + c], local_sems.at[a])
            mine.start()
            started.append(mine)
        first = []
        for a in range(n):
            first.append(copy(a, 0, me, sibling, src=ins[a]))
            first += [copy(a, 1 + j, me, (*chip, c), src=ins[a]) for j, chip in enumerate(chips)]
        for cp in first:
            cp.start()
        passed = []
        for j, chip in enumerate(chips):
            for a in range(n):
                copy(a, 1 + j, (*chip, c), me).wait_recv()
                fwd = copy(a, 4 + j, (*chip, c), sibling)
                fwd.start()
                passed.append(fwd)
        for a in range(n):
            copy(a, 0, sibling, me).wait_recv()
            for j, chip in enumerate(chips):
                copy(a, 4 + j, (*chip, 1 - c), me).wait_recv()
        for cp in first + passed:
            cp.wait_send()
        for mine in started:
            mine.wait()

    return pl.pallas_call(
        body, name=name,
        in_specs=_hbm_specs(total), out_specs=_hbm_specs(total),
        out_shape=[jax.ShapeDtypeStruct((N_DEV,) + s.shape, s.dtype) for s in (*shards, *place)],
        scratch_shapes=[pltpu.SemaphoreType.DMA((n, 7)), pltpu.SemaphoreType.DMA((n, 7)),
                        pltpu.SemaphoreType.DMA((total,))],
        compiler_params=pltpu.CompilerParams(has_side_effects=True),
    )(*shards, *place)


_FLIPS = [(dx, dy, dc) for dx in (0, 1) for dy in (0, 1) for dc in (0, 1)][1:]
_HBM = pl.BlockSpec(memory_space=pltpu.HBM)
_SEM = pl.BlockSpec(memory_space=pltpu.SEMAPHORE)
_DATAFLOW = pltpu.SideEffectType.DATAFLOW_SIDE_EFFECTING


def _scatter_copies(src_refs, land_refs, send_sems, recv_sems, gather):
    x, y, c = _mesh_position()
    me = 4 * x + 2 * y + c
    copies = []
    for a, (src, land) in enumerate(zip(src_refs, land_refs)):
        for k, (dx, dy, dc) in enumerate(_FLIPS):
            px, py, pc = (x + dx) % 2, (y + dy) % 2, (c + dc) % 2
            pair = a * len(_FLIPS) + k
            copies.append(pltpu.make_async_remote_copy(
                src_ref=src if gather else src.at[4 * px + 2 * py + pc], dst_ref=land.at[me],
                send_sem=send_sems[pair], recv_sem=recv_sems[pair],
                device_id=(px, py, pc), device_id_type=MESH))
    return copies


def _scatter_start(srcs, lands, gather, name):
    n = len(srcs)
    pairs = n * len(_FLIPS)

    def body(*refs):
        src_refs, land_refs = refs[:n], refs[n:2 * n]
        send_sems, recv_sems = refs[2 * n:2 * n + pairs], refs[2 * n + pairs:2 * n + 2 * pairs]
        token = refs[-1]
        for cp in _scatter_copies(src_refs, land_refs, send_sems, recv_sems, gather):
            cp.start()
        token[...] = jnp.zeros_like(token)

    arrays = [*srcs, *lands]
    sem = pltpu.SemaphoreType.DMA(())
    out = pl.pallas_call(
        body, name=name,
        out_shape=(*[sem] * (2 * pairs), *[pltpu.HBM(v.shape, v.dtype) for v in arrays],
                   jax.ShapeDtypeStruct((8, LANES), F32)),
        in_specs=[_HBM] * (2 * n),
        out_specs=(*[_SEM] * (2 * pairs), *[_HBM] * (2 * n), pl.BlockSpec(memory_space=pltpu.VMEM)),
        input_output_aliases={i: 2 * pairs + i for i in range(2 * n)},
        compiler_params=pltpu.CompilerParams(has_side_effects=_DATAFLOW),
    )(*[pltpu.with_memory_space_constraint(v, pltpu.HBM) for v in arrays])
    sems, rest = out[:2 * pairs], out[2 * pairs:]
    return list(sems[:pairs]), list(sems[pairs:]), list(rest[:n]), list(rest[n:2 * n]), rest[-1]


def _scatter_wait(send_sems, recv_sems, srcs, lands, after, gather, name):
    n = len(srcs)
    pairs = n * len(_FLIPS)

    def body(*refs):
        src_refs, land_refs = refs[:n], refs[n:2 * n]
        send_refs, recv_refs = refs[2 * n:2 * n + pairs], refs[2 * n + pairs:2 * n + 2 * pairs]
        for cp in _scatter_copies(src_refs, land_refs, send_refs, recv_refs, gather):
            cp.wait_send()
            cp.wait_recv()

    arrays = [*srcs, *lands]
    out = pl.pallas_call(
        body, name=name,
        out_shape=tuple(pltpu.HBM(v.shape, v.dtype) for v in arrays),
        in_specs=[*[_HBM] * (2 * n), *[_SEM] * (2 * pairs), pl.BlockSpec(memory_space=pl.ANY)],
        out_specs=tuple([_HBM] * (2 * n)),
        input_output_aliases={i: i for i in range(2 * n)},
        compiler_params=pltpu.CompilerParams(has_side_effects=_DATAFLOW),
    )(*arrays, *send_sems, *recv_sems, after)
    return list(out[:n]), list(out[n:])


def _sum_received(grad, received, me, name):
    _, rows, cols = grad.shape

    def body(me_ref, g_ref, r_ref, o_ref):
        p = pl.program_id(0)
        term = jnp.where(p == me_ref[0], g_ref[0], r_ref[0])

        @pl.when(p == 0)
        def _():
            o_ref[...] = term

        @pl.when(p > 0)
        def _():
            o_ref[...] += term

    blk = (1, rows, cols)
    return pl.pallas_call(
        body, name=name,
        grid_spec=pltpu.PrefetchScalarGridSpec(
            num_scalar_prefetch=1, grid=(N_DEV,),
            in_specs=[pl.BlockSpec(blk, lambda p, me_ref: (me_ref[0], 0, 0)),
                      pl.BlockSpec(blk, lambda p, me_ref: (p, 0, 0))],
            out_specs=pl.BlockSpec((rows, cols), lambda p, me_ref: (0, 0))),
        out_shape=jax.ShapeDtypeStruct((rows, cols), F32),
        compiler_params=_params("arbitrary"),
    )(me, grad, received)


def _exchange_in_chip(grads):
    n = len(grads)

    def body(*refs):
        ins, outs = refs[:n], refs[n:2 * n]
        send_sems, recv_sems = refs[2 * n:]
        x, y, c = _mesh_position()
        copies = []
        for a in range(n):
            for q in range(4):
                copies.append(pltpu.make_async_remote_copy(
                    src_ref=ins[a].at[2 * q + (1 - c)], dst_ref=outs[a].at[q],
                    send_sem=send_sems.at[a, q], recv_sem=recv_sems.at[a, q],
                    device_id=(x, y, 1 - c), device_id_type=MESH))
        for cp in copies:
            cp.start()
        for cp in copies:
            cp.wait_recv()
        for cp in copies:
            cp.wait_send()

    return pl.pallas_call(
        body, name="exchange_in_chip",
        in_specs=_hbm_specs(n), out_specs=_hbm_specs(n),
        out_shape=[jax.ShapeDtypeStruct((4,) + g.shape[1:], g.dtype) for g in grads],
        scratch_shapes=[pltpu.SemaphoreType.DMA((n, 4)), pltpu.SemaphoreType.DMA((n, 4))],
        compiler_params=pltpu.CompilerParams(has_side_effects=True),
    )(*grads)


def _exchange_between_chips(partials):
    n = len(partials)

    def body(*refs):
        ins, outs = refs[:n], refs[n:2 * n]
        send_sems, recv_sems = refs[2 * n:]
        x, y, c = _mesh_position()
        chips = [(1 - x, y), (x, 1 - y), (1 - x, 1 - y)]
        copies = []
        for a in range(n):
            for j, (px, py) in enumerate(chips):
                copies.append(pltpu.make_async_remote_copy(
                    src_ref=ins[a].at[2 * px + py], dst_ref=outs[a].at[j],
                    send_sem=send_sems.at[a, j], recv_sem=recv_sems.at[a, j],
                    device_id=(px, py, c), device_id_type=MESH))
        for cp in copies:
            cp.start()
        for cp in copies:
            cp.wait_recv()
        for cp in copies:
            cp.wait_send()

    return pl.pallas_call(
        body, name="exchange_between_chips",
        in_specs=_hbm_specs(n), out_specs=_hbm_specs(n),
        out_shape=[jax.ShapeDtypeStruct((3,) + p.shape[1:], p.dtype) for p in partials],
        scratch_shapes=[pltpu.SemaphoreType.DMA((n, 3)), pltpu.SemaphoreType.DMA((n, 3))],
        compiler_params=pltpu.CompilerParams(has_side_effects=True),
    )(*partials)


def _add_in_chip(grad, received, core, name):
    _, rows, cols = grad.shape

    def body(core_ref, g_ref, r_ref, o_ref):
        o_ref[...] = g_ref[...] + r_ref[...]

    blk = (1, rows, cols)
    return pl.pallas_call(
        body, name=name,
        grid_spec=pltpu.PrefetchScalarGridSpec(
            num_scalar_prefetch=1, grid=(4,),
            in_specs=[pl.BlockSpec(blk, lambda q, core_ref: (2 * q + core_ref[0], 0, 0)),
                      pl.BlockSpec(blk, lambda q, core_ref: (q, 0, 0))],
            out_specs=pl.BlockSpec(blk, lambda q, core_ref: (q, 0, 0))),
        out_shape=jax.ShapeDtypeStruct((4, rows, cols), F32),
        compiler_params=_params("parallel"),
    )(core, grad, received)


def _add_between_chips(partial, received, chip, name):
    _, rows, cols = partial.shape

    def body(chip_ref, p_ref, r_ref, o_ref):
        o_ref[...] = ((p_ref[0] + r_ref[0]) + r_ref[1]) + r_ref[2]

    return pl.pallas_call(
        body, name=name,
        grid_spec=pltpu.PrefetchScalarGridSpec(
            num_scalar_prefetch=1, grid=(1,),
            in_specs=[pl.BlockSpec((1, rows, cols), lambda i, chip_ref: (chip_ref[0], 0, 0)),
                      pl.BlockSpec((3, rows, cols), lambda i, chip_ref: (0, 0, 0))],
            out_specs=pl.BlockSpec((rows, cols), lambda i, chip_ref: (0, 0))),
        out_shape=jax.ShapeDtypeStruct((rows, cols), F32),
        compiler_params=_params("arbitrary"),
    )(chip, partial, received)


def _sum_devices(gathered):
    _, rows, cols = gathered.shape

    def body(g_ref, o_ref):
        total = g_ref[0]
        for d in range(1, N_DEV):
            total = total + g_ref[d]
        o_ref[...] = total

    return pl.pallas_call(
        body, name="sum_small_grads",
        out_shape=jax.ShapeDtypeStruct((rows, cols), F32),
        compiler_params=_params(),
    )(gathered)


def _adamw(w, g, m, v, name):
    rows, cols = w.shape
    tr = rows
    for cand in (256, 128, 64, 32, 16, 8):
        if rows > cand and rows % cand == 0:
            tr = cand
            break

    def body(w_ref, g_ref, m_ref, v_ref, delta_ref, newm_ref, newv_ref):
        g_v = g_ref[...]
        new_m = ADAM_B1 * m_ref[...] + (1.0 - ADAM_B1) * g_v
        new_v = ADAM_B2 * v_ref[...] + (1.0 - ADAM_B2) * (g_v * g_v)
        m_hat = new_m / (1.0 - ADAM_B1 ** ADAM_STEP)
        v_hat = new_v / (1.0 - ADAM_B2 ** ADAM_STEP)
        delta_ref[...] = -ADAM_LR * (m_hat / (jnp.sqrt(v_hat) + ADAM_EPS) + ADAM_WD * w_ref[...])
        newm_ref[...] = new_m
        newv_ref[...] = new_v

    blk = pl.BlockSpec((tr, cols), lambda i: (i, 0))
    shape = jax.ShapeDtypeStruct((rows, cols), F32)
    return pl.pallas_call(
        body, name=name, grid=(rows // tr,),
        in_specs=[blk] * 4, out_specs=[blk] * 3, out_shape=[shape] * 3,
        compiler_params=_params("parallel"),
    )(w, g, m, v)


def _pack(pieces, rows):
    flat = jnp.concatenate([p.reshape(-1) for p in pieces])
    return jnp.pad(flat, (0, rows * LANES - flat.shape[0])).reshape(rows, LANES)


def _unpack(packed, shapes):
    flat = packed.reshape(-1)
    out, pos = [], 0
    for shape in shapes:
        size = 1
        for s in shape:
            size *= s
        out.append(flat[pos:pos + size].reshape(shape))
        pos += size
    return out


def _rows_for(count):
    return -(-count // (8 * LANES)) * 8


SMALL = ("norm_mix_pre", "conv_dw_b", "conv_ln_g", "conv_ln_b", "rel_bias", "norm_mix_post", "norm_ffn_pre",
         "ffn_dw_b", "norm_ffn_post")
SHARDED_SMALL = ("conv_dw_w", "ffn_dw_w")
LARGE = ("w_in", "w_out", "w_up", "w_down")
WEIGHTS = ("norm_mix_pre", "w_in", "conv_dw_w", "conv_dw_b", "conv_ln_g", "conv_ln_b", "rel_bias", "w_out",
           "norm_mix_post", "norm_ffn_pre", "w_up", "ffn_dw_w", "ffn_dw_b", "w_down", "norm_ffn_post")


def kernel(x, norm_mix_pre, w_in, conv_dw_w, conv_dw_b, conv_ln_g, conv_ln_b, rel_bias, w_out, norm_mix_post, norm_ffn_pre, w_up, ffn_dw_w, ffn_dw_b, w_down, norm_ffn_post, loss_target, m_norm_mix_pre, m_w_in, m_conv_dw_w, m_conv_dw_b, m_conv_ln_g, m_conv_ln_b, m_rel_bias, m_w_out, m_norm_mix_post, m_norm_ffn_pre, m_w_up, m_ffn_dw_w, m_ffn_dw_b, m_w_down, m_norm_ffn_post, v_norm_mix_pre, v_w_in, v_conv_dw_w, v_conv_dw_b, v_conv_ln_g, v_conv_ln_b, v_rel_bias, v_w_out, v_norm_mix_post, v_norm_ffn_pre, v_w_up, v_ffn_dw_w, v_ffn_dw_b, v_w_down, v_norm_ffn_post):
    weights = dict(norm_mix_pre=norm_mix_pre, w_in=w_in, conv_dw_w=conv_dw_w, conv_dw_b=conv_dw_b, conv_ln_g=conv_ln_g,
                   conv_ln_b=conv_ln_b, rel_bias=rel_bias, w_out=w_out, norm_mix_post=norm_mix_post,
                   norm_ffn_pre=norm_ffn_pre, w_up=w_up, ffn_dw_w=ffn_dw_w, ffn_dw_b=ffn_dw_b, w_down=w_down,
                   norm_ffn_post=norm_ffn_post)
    mom1 = dict(norm_mix_pre=m_norm_mix_pre, w_in=m_w_in, conv_dw_w=m_conv_dw_w, conv_dw_b=m_conv_dw_b,
                conv_ln_g=m_conv_ln_g, conv_ln_b=m_conv_ln_b, rel_bias=m_rel_bias, w_out=m_w_out,
                norm_mix_post=m_norm_mix_post, norm_ffn_pre=m_norm_ffn_pre, w_up=m_w_up, ffn_dw_w=m_ffn_dw_w,
                ffn_dw_b=m_ffn_dw_b, w_down=m_w_down, norm_ffn_post=m_norm_ffn_post)
    mom2 = dict(norm_mix_pre=v_norm_mix_pre, w_in=v_w_in, conv_dw_w=v_conv_dw_w, conv_dw_b=v_conv_dw_b,
                conv_ln_g=v_conv_ln_g, conv_ln_b=v_conv_ln_b, rel_bias=v_rel_bias, w_out=v_w_out,
                norm_mix_post=v_norm_mix_post, norm_ffn_pre=v_norm_ffn_pre, w_up=v_w_up, ffn_dw_w=v_ffn_dw_w,
                ffn_dw_b=v_ffn_dw_b, w_down=v_w_down, norm_ffn_post=v_norm_ffn_post)

    x2 = x[0]
    target = loss_target[0]
    t_rows = x2.shape[0]
    d = D_MODEL
    in_cols = 2 * CONV_WIDTH + 3 * ATTN_WIDTH
    my_x, my_y, my_c = _mesh_position()
    my_dev = 4 * my_x + 2 * my_y + my_c

    small_conv = _pack([conv_dw_w[0], ffn_dw_w[0]], 32)
    late_shards = [w_out[0].astype(BF16), w_up[0].T.astype(BF16), w_down[0].astype(BF16)]
    win_t, conv_g, *late_lands = _all_gather(
        [w_in[0].T.astype(BF16), small_conv], "all_gather_weights", place=late_shards)
    late_send, late_recv, late_shards, late_lands, late_token = _scatter_start(
        late_shards, late_lands, True, "gather_late_weights_start")
    win_t = win_t.reshape(in_cols, d)
    conv_flat = conv_g.reshape(N_DEV, 32 * LANES)
    n_cw = CONV_K * (CONV_WIDTH // N_DEV)
    conv_w_full = conv_flat[:, :n_cw].reshape(N_DEV, CONV_K, CONV_WIDTH // N_DEV).transpose(1, 0, 2).reshape(CONV_K, CONV_WIDTH)
    ffn_w_full = conv_flat[:, n_cw:].reshape(N_DEV, FFN_K, 2 * D_FF // N_DEV).transpose(1, 0, 2).reshape(FFN_K, 2 * D_FF)

    u1 = _pre_norm(x2, norm_mix_pre + late_token[0:1, 0:1], "pre_norm_mix")
    proj_a = _matmul(u1, win_t, mode="nt", m=t_rows, n=2 * CONV_WIDTH, k=d, tm=2048, tn=1024, tk=d,
                     out_dtype=F32, name="proj_conv")
    qkv_t = _matmul(win_t, u1, mode="nt", m=3 * ATTN_WIDTH, n=t_rows, k=d, tm=512, tn=2048, tk=d,
                    out_dtype=BF16, name="proj_qkv", a_m0=2 * CONV_WIDTH)
    conv_c, conv_out = _conv_forward(proj_a, conv_w_full, conv_dw_b, conv_ln_g, conv_ln_b)
    o_t = _attn_forward(qkv_t, rel_bias[0])
    _, (wout_g, wup_t, wdown_g) = _scatter_wait(late_send, late_recv, late_shards, late_lands, o_t, True,
                                                "gather_late_weights_wait")
    wout_g = wout_g.reshape(d, d)
    wup_t = wup_t.reshape(2 * D_FF, d)
    wdown_g = wdown_g.reshape(D_FF, d)
    mixed = _matmul(conv_out, wout_g, mode="nn", m=t_rows, n=d, k=CONV_WIDTH, tm=2048, tn=1024, tk=CONV_WIDTH,
                    out_dtype=F32, name="out_proj_conv")
    mixed = _matmul(o_t, wout_g, mode="tn", m=t_rows, n=d, k=ATTN_WIDTH, tm=1024, tn=1024, tk=ATTN_WIDTH,
                    out_dtype=F32, name="out_proj_attn", b_k0=CONV_WIDTH, add=mixed)
    h1, u2 = _mid_forward(x2, mixed, norm_mix_post, norm_ffn_pre)
    hup = _matmul(u2, wup_t, mode="nt", m=t_rows, n=2 * D_FF, k=d, tm=2048, tn=1408, tk=d,
                  out_dtype=F32, name="ffn_up")
    act = _ffn_activation(hup, ffn_w_full, ffn_dw_b)
    f = _matmul(act, wdown_g, mode="nn", m=t_rows, n=d, k=D_FF, tm=1024, tn=1024, tk=D_FF,
                out_dtype=F32, name="ffn_down")
    loss, dy, df, d_norm_ffn_post = _loss_and_head_backward(h1, f, target, norm_ffn_post)

    dact = _matmul(df, wdown_g, mode="nt", m=t_rows, n=D_FF, k=d, tm=2048, tn=1408, tk=d,
                   out_dtype=F32, name="ffn_down_dx")
    g_wdown = _matmul(act, df, mode="tn", m=D_FF, n=d, k=t_rows, tm=1408, tn=1024, tk=2048,
                      out_dtype=F32, name="ffn_down_dw")
    dhg, dhv, dwg, dwv, dbg, dbv = _ffn_backward(dact, hup, ffn_w_full, ffn_dw_b)
    du2 = _matmul(dhg, wup_t, mode="nn", m=t_rows, n=d, k=D_FF, tm=1024, tn=1024, tk=D_FF,
                  out_dtype=F32, name="ffn_up_dx_gate")
    du2 = _matmul(dhv, wup_t, mode="nn", m=t_rows, n=d, k=D_FF, tm=1024, tn=1024, tk=D_FF,
                  out_dtype=F32, name="ffn_up_dx_value", b_k0=D_FF, add=du2)
    g_wup_t = _matmul(dhg, u2, mode="tn", m=D_FF, n=d, k=t_rows, tm=1408, tn=1024, tk=2048, out_dtype=F32,
                      name="ffn_up_dw_gate", out_rows=2 * D_FF)
    g_wup_t = _matmul(dhv, u2, mode="tn", m=D_FF, n=d, k=t_rows, tm=1408, tn=1024, tk=2048, out_dtype=F32,
                      name="ffn_up_dw_value", out_rows=2 * D_FF, out_m0=D_FF, into=g_wup_t)
    ffn_grads = [g_wup_t.reshape(N_DEV, 2 * D_FF // N_DEV, d), g_wdown.reshape(N_DEV, D_FF // N_DEV, d)]
    red_send, red_recv, ffn_grads, red_lands, red_token = _scatter_start(
        ffn_grads, [lax.empty(g.shape, F32) for g in ffn_grads], False, "reduce_ffn_grads_start")
    dh1, dmixed, d_norm_ffn_pre, d_norm_mix_post = _mid_backward(
        dy, du2, h1, mixed, norm_ffn_pre + red_token[0:1, 0:1], norm_mix_post)
    dconv_out = _matmul(dmixed, wout_g, mode="nt", m=t_rows, n=CONV_WIDTH, k=d, tm=2048, tn=512, tk=d,
                        out_dtype=F32, name="out_proj_dx_conv")
    do_t = _matmul(wout_g, dmixed, mode="nt", m=ATTN_WIDTH, n=t_rows, k=d, tm=512, tn=2048, tk=d,
                   out_dtype=BF16, name="out_proj_dx_attn", a_m0=CONV_WIDTH)
    g_wout = _matmul(conv_out, dmixed, mode="tn", m=CONV_WIDTH, n=d, k=t_rows, tm=512, tn=1024, tk=2048, out_dtype=F32,
                     name="out_proj_dw_conv", out_rows=d)
    g_wout = _matmul(o_t, dmixed, mode="nn", m=ATTN_WIDTH, n=d, k=t_rows, tm=512, tn=1024, tk=2048, out_dtype=F32,
                     name="out_proj_dw_attn", out_rows=d, out_m0=CONV_WIDTH, into=g_wout)
    wout_handle = _scatter_start([g_wout.reshape(N_DEV, d // N_DEV, d)], [lax.empty((N_DEV, d // N_DEV, d), F32)],
                                 False, "reduce_w_out_grad_start")
    dproj_a, d_conv_w, d_conv_b, d_ln_g, d_ln_b = _conv_backward(
        dconv_out, conv_c, proj_a, conv_w_full, conv_ln_g + wout_handle[4][0:1, 0:1], conv_ln_b)
    dqkv_parts = _attn_backward(qkv_t, o_t, do_t, rel_bias[0])
    drel = dqkv_parts[3]
    du1 = _matmul(dproj_a, win_t, mode="nn", m=t_rows, n=d, k=2 * CONV_WIDTH, tm=2048, tn=1024, tk=1024,
                  out_dtype=F32, name="proj_dx_conv")
    g_win_t = _matmul(dproj_a, u1, mode="tn", m=2 * CONV_WIDTH, n=d, k=t_rows, tm=1024, tn=1024, tk=2048, out_dtype=F32,
                      name="proj_dw_conv", out_rows=in_cols)
    for j, part in enumerate("qkv"):
        row0 = 2 * CONV_WIDTH + j * ATTN_WIDTH
        du1 = _matmul(dqkv_parts[j], win_t, mode="tn", m=t_rows, n=d, k=ATTN_WIDTH, tm=1024, tn=1024, tk=ATTN_WIDTH,
                      out_dtype=F32, name="proj_dx_" + part, b_k0=row0, add=du1)
        g_win_t = _matmul(dqkv_parts[j], u1, mode="nn", m=ATTN_WIDTH, n=d, k=t_rows, tm=512, tn=1024, tk=2048,
                          out_dtype=F32, name="proj_dw_" + part, out_rows=in_cols, out_m0=row0, into=g_win_t)
    win_handle = _scatter_start([g_win_t.reshape(N_DEV, in_cols // N_DEV, d)],
                                [lax.empty((N_DEV, in_cols // N_DEV, d), F32)], False, "reduce_w_in_grad_start")
    dx, d_norm_mix_pre = _input_backward(dh1, du1, x2, norm_mix_pre + win_handle[4][0:1, 0:1])

    small_grads = dict(norm_mix_pre=d_norm_mix_pre, conv_dw_b=d_conv_b, conv_ln_g=d_ln_g, conv_ln_b=d_ln_b,
                       rel_bias=drel[:, :2 * MAX_REL + 1], norm_mix_post=d_norm_mix_post, norm_ffn_pre=d_norm_ffn_pre,
                       ffn_dw_b=jnp.concatenate([dbg, dbv], axis=1), norm_ffn_post=d_norm_ffn_post)
    pieces = [small_grads[nm] for nm in SMALL] + [d_conv_w, jnp.concatenate([dwg, dwv], axis=1), loss]
    count = sum(p.size for p in pieces)
    (gathered_small,) = _all_gather([_pack(pieces, _rows_for(count))], "all_gather_small_grads")
    summed = _sum_devices(gathered_small)
    shapes = [weights[nm].shape for nm in SMALL] + [(CONV_K, CONV_WIDTH), (FFN_K, 2 * D_FF), (1, 1)]
    unpacked = _unpack(summed, shapes)
    grads = dict(zip(SMALL, unpacked[:len(SMALL)]))
    cw_shard, fw_shard = CONV_WIDTH // N_DEV, 2 * D_FF // N_DEV
    grads["conv_dw_w"] = lax.dynamic_slice_in_dim(unpacked[-3], my_dev * cw_shard, cw_shard, axis=1)[None]
    grads["ffn_dw_w"] = lax.dynamic_slice_in_dim(unpacked[-2], my_dev * fw_shard, fw_shard, axis=1)[None]
    total_loss = unpacked[-1].reshape(())

    me = jnp.reshape(my_dev, (1,)).astype(jnp.int32)
    delta, new_m, new_v = {}, {}, {}

    def finish(nm, send, recv, srcs, lands, after, transposed):
        srcs, lands = _scatter_wait(send, recv, srcs, lands, after, False, "reduce_" + nm + "_grad_wait")
        for name_a, src, land in zip(nm.split("_and_"), srcs, lands):
            g = _sum_received(src, land, me, "sum_received_" + name_a)
            grads[name_a] = (g.T if transposed[name_a] else g)[None]
            dl, nm1, nv1 = _adamw(weights[name_a][0], grads[name_a][0], mom1[name_a][0], mom2[name_a][0], "adamw_" + name_a)
            delta[name_a], new_m[name_a], new_v[name_a] = dl[None], nm1[None], nv1[None]

    transposed = dict(w_in=True, w_out=False, w_up=True, w_down=False)
    finish("w_up_and_w_down", red_send, red_recv, ffn_grads, red_lands, dx, transposed)
    finish("w_out", *wout_handle[:4], dx, transposed)
    finish("w_in", *win_handle[:4], delta["w_up"], transposed)
    small_names = SMALL + SHARDED_SMALL
    small_count = sum(weights[nm].size for nm in small_names)
    small_rows = _rows_for(small_count)
    packed = [_pack([src[nm] for nm in small_names], small_rows) for src in (weights, grads, mom1, mom2)]
    outs = _adamw(*packed, "adamw_small")
    small_shapes = [weights[nm].shape for nm in small_names]
    for store, arr in zip((delta, new_m, new_v), outs):
        store.update(zip(small_names, _unpack(arr, small_shapes)))

    return (total_loss, dx[None], *[grads[nm] for nm in WEIGHTS], *[delta[nm] for nm in WEIGHTS],
            *[new_m[nm] for nm in WEIGHTS], *[new_v[nm] for nm in WEIGHTS])
```

```python
import jax
import jax.numpy as jnp
from jax import lax
from jax.experimental import pallas as pl
from jax.experimental.pallas import tpu as pltpu

F32 = jnp.float32
BF16 = jnp.bfloat16
MESH = pl.DeviceIdType.MESH
AXES = ("x", "y", "c")
N_DEV = 8

EPS = 1e-6
NEG_INF = -1e30
D_MODEL = 1024
CONV_WIDTH = 512
ATTN_WIDTH = 512
N_HEADS = 8
HEAD_DIM = 64
CHUNK = 64
LEFT = 8 * CHUNK
QBLK = 2 * CHUNK
WIN = LEFT + QBLK
CONV_K = 31
CONV_HALO = 32
FFN_K = 3
FFN_HALO = 8
D_FF = 2816
MAX_REL = 128
SCALE = HEAD_DIM ** -0.5
ADAM_LR, ADAM_B1, ADAM_B2, ADAM_EPS, ADAM_WD, ADAM_STEP = 0.001, 0.9, 0.999, 1e-08, 0.01, 10

V7X_VMEM_BYTES = 64 * 2**20
VMEM_LIMIT_BYTES = V7X_VMEM_BYTES - 8 * 2**20
LANES = 128


def _params(*sem):
    return pltpu.CompilerParams(dimension_semantics=sem or None, vmem_limit_bytes=VMEM_LIMIT_BYTES)


_DOT_DIMS = {"nn": (((1,), (0,)), ((), ())), "nt": (((1,), (1,)), ((), ())), "tn": (((0,), (0,)), ((), ()))}


def _matmul(a, b, *, mode, m, n, k, tm, tn, tk, out_dtype, name, a_m0=0, b_n0=0, b_k0=0, add=None,
            out_rows=None, out_m0=0, into=None):
    tm, tn, tk = min(tm, m), min(tn, n), min(tk, k)
    out_rows = m if out_rows is None else out_rows
    assert m % tm == 0 and n % tn == 0 and k % tk == 0, (name, m, n, k, tm, tn, tk)
    assert a_m0 % tm == 0 and b_n0 % tn == 0 and b_k0 % tk == 0 and out_m0 % tm == 0, name
    am, bn, bk, om = a_m0 // tm, b_n0 // tn, b_k0 // tk, out_m0 // tm
    gk = k // tk
    dims = _DOT_DIMS[mode]

    if mode == "tn":
        a_spec = pl.BlockSpec((tk, tm), lambda i, j, kk: (kk, i + am))
    else:
        a_spec = pl.BlockSpec((tm, tk), lambda i, j, kk: (i + am, kk))
    if mode == "nt":
        b_spec = pl.BlockSpec((tn, tk), lambda i, j, kk: (j + bn, kk + bk))
    else:
        b_spec = pl.BlockSpec((tk, tn), lambda i, j, kk: (kk + bk, j + bn))
    o_spec = pl.BlockSpec((tm, tn), lambda i, j, kk: (i + om, j))
    in_specs = [a_spec, b_spec]
    operands = [a, b]
    if add is not None:
        assert out_rows == m
        in_specs.append(o_spec)
        operands.append(add)
    aliases = {}
    if into is not None:
        aliases = {len(operands): 0}
        in_specs.append(pl.BlockSpec(memory_space=pl.ANY))
        operands.append(into)

    def body(*refs):
        a_ref, b_ref = refs[0], refs[1]
        add_ref = refs[2] if add is not None else None
        o_ref = refs[len(operands)]
        part = lax.dot_general(a_ref[...].astype(BF16), b_ref[...].astype(BF16), dims,
                               preferred_element_type=F32)

        def finish(total):
            if add_ref is not None:
                total = total + add_ref[...]
            o_ref[...] = total.astype(out_dtype)

        if gk == 1:
            finish(part)
        else:
            acc_ref = refs[-1]
            kk = pl.program_id(2)

            @pl.when(kk == 0)
            def _():
                acc_ref[...] = part

            @pl.when(kk > 0)
            def _():
                acc_ref[...] += part

            @pl.when(kk == gk - 1)
            def _():
                finish(acc_ref[...])

    return pl.pallas_call(
        body, name=name,
        grid=(m // tm, n // tn, gk),
        in_specs=in_specs, out_specs=o_spec,
        out_shape=jax.ShapeDtypeStruct((out_rows, n), out_dtype),
        scratch_shapes=[pltpu.VMEM((tm, tn), F32)] if gk > 1 else [],
        input_output_aliases=aliases,
        compiler_params=_params("parallel", "parallel", "arbitrary"),
    )(*operands)


def _rms_hat(v):
    r = lax.rsqrt(jnp.mean(v * v, axis=-1, keepdims=True) + EPS)
    return v * r, r


def _rms_bwd(dn, hat, r):
    return r * (dn - hat * jnp.mean(dn * hat, axis=-1, keepdims=True))


def _sigmoid(v):
    return 1.0 / (1.0 + jnp.exp(-v))


_GELU_C = 0.7978845608028654


def _gelu(v):
    return 0.5 * v * (1.0 + jnp.tanh(_GELU_C * (v + 0.044715 * v * (v * v))))


def _gelu_parts(v):
    v2 = v * v
    t = jnp.tanh(_GELU_C * (v + 0.044715 * v * v2))
    cdf = 0.5 * (1.0 + t)
    dcdf = 0.5 * (1.0 - t * t) * _GELU_C * (1.0 + 3.0 * 0.044715 * v2)
    return v * cdf, cdf + v * dcdf


def _row_tile(t_rows, want):
    tile = min(want, t_rows)
    assert t_rows % tile == 0
    return tile


def _pre_norm(x, g, name):
    t_rows, d = x.shape
    tm = _row_tile(t_rows, 512)

    def body(x_ref, g_ref, u_ref):
        hat, _ = _rms_hat(x_ref[...])
        u_ref[...] = (hat * g_ref[...]).astype(BF16)

    return pl.pallas_call(
        body, name=name, grid=(t_rows // tm,),
        in_specs=[pl.BlockSpec((tm, d), lambda i: (i, 0)), pl.BlockSpec((1, d), lambda i: (0, 0))],
        out_specs=pl.BlockSpec((tm, d), lambda i: (i, 0)),
        out_shape=jax.ShapeDtypeStruct((t_rows, d), BF16),
        compiler_params=_params("parallel"),
    )(x, g)


def _mid_forward(x, mixed, g_post, g_pre):
    t_rows, d = x.shape
    tm = _row_tile(t_rows, 512)

    def body(x_ref, mixed_ref, gpost_ref, gpre_ref, h1_ref, u2_ref):
        hat, _ = _rms_hat(mixed_ref[...])
        h1 = x_ref[...] + hat * gpost_ref[...]
        h1_ref[...] = h1
        hat1, _ = _rms_hat(h1)
        u2_ref[...] = (hat1 * gpre_ref[...]).astype(BF16)

    row = pl.BlockSpec((tm, d), lambda i: (i, 0))
    vec = pl.BlockSpec((1, d), lambda i: (0, 0))
    return pl.pallas_call(
        body, name="mid_forward", grid=(t_rows // tm,),
        in_specs=[row, row, vec, vec], out_specs=[row, row],
        out_shape=[jax.ShapeDtypeStruct((t_rows, d), F32), jax.ShapeDtypeStruct((t_rows, d), BF16)],
        compiler_params=_params("parallel"),
    )(x, mixed, g_post, g_pre)


def _loss_and_head_backward(h1, f, target, g_post):
    t_rows, d = h1.shape
    tm = _row_tile(t_rows, 512)
    nt = t_rows // tm

    def body(h1_ref, f_ref, tgt_ref, g_ref, loss_ref, dy_ref, df_ref, dg_ref, sq_ref):
        i = pl.program_id(0)

        @pl.when(i == 0)
        def _():
            sq_ref[...] = jnp.zeros_like(sq_ref)
            dg_ref[...] = jnp.zeros_like(dg_ref)

        g = g_ref[...]
        hat, r = _rms_hat(f_ref[...])
        err = h1_ref[...] + hat * g - tgt_ref[...]
        sq_ref[...] += jnp.sum(err * err, axis=0, keepdims=True)
        dy = err * (1.0 / d)
        dy_ref[...] = dy
        dg_ref[...] += jnp.sum(dy * hat, axis=0, keepdims=True)
        df_ref[...] = _rms_bwd(dy * g, hat, r).astype(BF16)

        @pl.when(i == nt - 1)
        def _():
            loss_ref[...] = (0.5 / d) * jnp.sum(sq_ref[...], axis=1, keepdims=True)

    row = pl.BlockSpec((tm, d), lambda i: (i, 0))
    vec = pl.BlockSpec((1, d), lambda i: (0, 0))
    return pl.pallas_call(
        body, name="loss_head_backward", grid=(nt,),
        in_specs=[row, row, row, vec],
        out_specs=[pl.BlockSpec((1, 1), lambda i: (0, 0)), row, row, vec],
        out_shape=[jax.ShapeDtypeStruct((1, 1), F32), jax.ShapeDtypeStruct((t_rows, d), F32),
                   jax.ShapeDtypeStruct((t_rows, d), BF16), jax.ShapeDtypeStruct((1, d), F32)],
        scratch_shapes=[pltpu.VMEM((1, d), F32)],
        compiler_params=_params("arbitrary"),
    )(h1, f, target, g_post)


def _mid_backward(dy, du2, h1, mixed, g_pre, g_post):
    t_rows, d = dy.shape
    tm = _row_tile(t_rows, 512)

    def body(dy_ref, du2_ref, h1_ref, mixed_ref, gpre_ref, gpost_ref, dh1_ref, dmixed_ref, dgpre_ref, dgpost_ref):
        @pl.when(pl.program_id(0) == 0)
        def _():
            dgpre_ref[...] = jnp.zeros_like(dgpre_ref)
            dgpost_ref[...] = jnp.zeros_like(dgpost_ref)

        du2 = du2_ref[...]
        hat1, r1 = _rms_hat(h1_ref[...])
        dgpre_ref[...] += jnp.sum(du2 * hat1, axis=0, keepdims=True)
        dh1 = dy_ref[...] + _rms_bwd(du2 * gpre_ref[...], hat1, r1)
        dh1_ref[...] = dh1
        hatm, rm = _rms_hat(mixed_ref[...])
        dgpost_ref[...] += jnp.sum(dh1 * hatm, axis=0, keepdims=True)
        dmixed_ref[...] = _rms_bwd(dh1 * gpost_ref[...], hatm, rm).astype(BF16)

    row = pl.BlockSpec((tm, d), lambda i: (i, 0))
    vec = pl.BlockSpec((1, d), lambda i: (0, 0))
    return pl.pallas_call(
        body, name="mid_backward", grid=(t_rows // tm,),
        in_specs=[row, row, row, row, vec, vec], out_specs=[row, row, vec, vec],
        out_shape=[jax.ShapeDtypeStruct((t_rows, d), F32), jax.ShapeDtypeStruct((t_rows, d), BF16),
                   jax.ShapeDtypeStruct((1, d), F32), jax.ShapeDtypeStruct((1, d), F32)],
        compiler_params=_params("arbitrary"),
    )(dy, du2, h1, mixed, g_pre, g_post)


def _input_backward(dh1, du1, x, g_pre):
    t_rows, d = x.shape
    tm = _row_tile(t_rows, 512)

    def body(dh1_ref, du1_ref, x_ref, g_ref, dx_ref, dg_ref):
        @pl.when(pl.program_id(0) == 0)
        def _():
            dg_ref[...] = jnp.zeros_like(dg_ref)

        du1 = du1_ref[...]
        hat, r = _rms_hat(x_ref[...])
        dg_ref[...] += jnp.sum(du1 * hat, axis=0, keepdims=True)
        dx_ref[...] = dh1_ref[...] + _rms_bwd(du1 * g_ref[...], hat, r)

    row = pl.BlockSpec((tm, d), lambda i: (i, 0))
    vec = pl.BlockSpec((1, d), lambda i: (0, 0))
    return pl.pallas_call(
        body, name="input_backward", grid=(t_rows // tm,),
        in_specs=[row, row, row, vec], out_specs=[row, vec],
        out_shape=[jax.ShapeDtypeStruct((t_rows, d), F32), jax.ShapeDtypeStruct((1, d), F32)],
        compiler_params=_params("arbitrary"),
    )(dh1, du1, x, g_pre)


CONV_STRIP = 32


def _glu(block):
    return block[:, :CONV_WIDTH] * _sigmoid(block[:, CONV_WIDTH:])


def _layer_norm_parts(c):
    mu = jnp.mean(c, axis=-1, keepdims=True)
    xc = c - mu
    r = lax.rsqrt(jnp.mean(xc * xc, axis=-1, keepdims=True) + EPS)
    return xc * r, r


def _conv_forward(proj_a, w, b, ln_g, ln_b):
    t_rows = proj_a.shape[0]
    tm = _row_tile(t_rows, 512)
    hb = tm // CONV_HALO
    cw = CONV_WIDTH

    def body(cur_ref, prev_ref, w_ref, b_ref, g_ref, beta_ref, c_ref, out_ref, hbuf):
        i = pl.program_id(0)
        hbuf[0:CONV_HALO, :] = jnp.where(i > 0, _glu(prev_ref[...]), 0.0)
        hbuf[CONV_HALO:, :] = _glu(cur_ref[...])

        def strip(s, carry):
            base = pl.multiple_of(s * CONV_STRIP, CONV_STRIP)
            v = hbuf[pl.ds(base, 2 * CONV_STRIP), :]
            acc = jnp.broadcast_to(b_ref[...], (CONV_STRIP, cw))
            off = CONV_HALO - (CONV_K - 1)
            for kk in range(CONV_K):
                acc = acc + w_ref[kk:kk + 1, :] * v[off + kk:off + kk + CONV_STRIP, :]
            c_ref[pl.ds(base, CONV_STRIP), :] = acc
            hat, _ = _layer_norm_parts(acc)
            z = hat * g_ref[...] + beta_ref[...]
            out_ref[pl.ds(base, CONV_STRIP), :] = (z * _sigmoid(z)).astype(BF16)
            return carry

        lax.fori_loop(0, tm // CONV_STRIP, strip, 0)

    vec = pl.BlockSpec((1, cw), lambda i: (0, 0))
    return pl.pallas_call(
        body, name="conv_forward", grid=(t_rows // tm,),
        in_specs=[pl.BlockSpec((tm, 2 * cw), lambda i: (i, 0)),
                  pl.BlockSpec((CONV_HALO, 2 * cw), lambda i: (jnp.maximum(i * hb - 1, 0), 0)),
                  pl.BlockSpec((CONV_K, cw), lambda i: (0, 0)), vec, vec, vec],
        out_specs=[pl.BlockSpec((tm, cw), lambda i: (i, 0)), pl.BlockSpec((tm, cw), lambda i: (i, 0))],
        out_shape=[jax.ShapeDtypeStruct((t_rows, cw), F32), jax.ShapeDtypeStruct((t_rows, cw), BF16)],
        scratch_shapes=[pltpu.VMEM((tm + CONV_HALO, cw), F32)],
        compiler_params=_params("parallel"),
    )(proj_a, proj_a, w, b, ln_g, ln_b)


def _conv_backward(dout, c, proj_a, w, ln_g, ln_b):
    t_rows = c.shape[0]
    tm = _row_tile(t_rows, 512)
    hb = tm // CONV_HALO
    nt = t_rows // tm
    last_halo = t_rows // CONV_HALO - 1
    cw = CONV_WIDTH

    def body(dout_ref, dout_next_ref, c_ref, c_next_ref, cur_ref, prev_ref, w_ref, g_ref, beta_ref,
             dproj_ref, dw_ref, db_ref, dg_ref, dbeta_ref, hbuf, dcbuf, dwacc):
        i = pl.program_id(0)

        @pl.when(i == 0)
        def _():
            dwacc[...] = jnp.zeros_like(dwacc)
            db_ref[...] = jnp.zeros_like(db_ref)
            dg_ref[...] = jnp.zeros_like(dg_ref)
            dbeta_ref[...] = jnp.zeros_like(dbeta_ref)

        def ln_swish_backward(dout_v, c_v):
            hat, r = _layer_norm_parts(c_v)
            g = g_ref[...]
            z = hat * g + beta_ref[...]
            sg = _sigmoid(z)
            dz = dout_v * (sg * (1.0 + z * (1.0 - sg)))
            dhat = dz * g
            dc = r * (dhat - jnp.mean(dhat, axis=-1, keepdims=True)
                      - hat * jnp.mean(dhat * hat, axis=-1, keepdims=True))
            return dc, dz, hat

        dc, dz, hat = ln_swish_backward(dout_ref[...], c_ref[...])
        dg_ref[...] += jnp.sum(dz * hat, axis=0, keepdims=True)
        dbeta_ref[...] += jnp.sum(dz, axis=0, keepdims=True)
        db_ref[...] += jnp.sum(dc, axis=0, keepdims=True)
        dcbuf[0:tm, :] = dc
        dc_next, _, _ = ln_swish_backward(dout_next_ref[...], c_next_ref[...])
        dcbuf[tm:, :] = jnp.where(i < nt - 1, dc_next, 0.0)

        hbuf[0:CONV_HALO, :] = jnp.where(i > 0, _glu(prev_ref[...]), 0.0)
        hbuf[CONV_HALO:, :] = _glu(cur_ref[...])

        def strip(s, carry):
            base = pl.multiple_of(s * CONV_STRIP, CONV_STRIP)
            dv = dcbuf[pl.ds(base, 2 * CONV_STRIP), :]
            hv = hbuf[pl.ds(base, 2 * CONV_STRIP), :]
            dcs = dv[0:CONV_STRIP, :]
            dh = jnp.zeros((CONV_STRIP, cw), F32)
            off = CONV_HALO - (CONV_K - 1)
            for kk in range(CONV_K):
                back = CONV_K - 1 - kk
                dh = dh + w_ref[kk:kk + 1, :] * dv[back:back + CONV_STRIP, :]
                prod = dcs * hv[off + kk:off + kk + CONV_STRIP, :]
                dwacc[kk] += jnp.sum(prod.reshape(CONV_STRIP // 8, 8, cw), axis=0)
            blk = cur_ref[pl.ds(base, CONV_STRIP), :]
            val, sg = blk[:, :cw], _sigmoid(blk[:, cw:])
            dproj_ref[pl.ds(base, CONV_STRIP), 0:cw] = (dh * sg).astype(BF16)
            dproj_ref[pl.ds(base, CONV_STRIP), cw:2 * cw] = (dh * val * sg * (1.0 - sg)).astype(BF16)
            return carry

        lax.fori_loop(0, tm // CONV_STRIP, strip, 0)

        @pl.when(i == nt - 1)
        def _():
            for kk in range(CONV_K):
                dw_ref[kk:kk + 1, :] = jnp.sum(dwacc[kk], axis=0, keepdims=True)

    vec = pl.BlockSpec((1, cw), lambda i: (0, 0))
    cur = lambda width: pl.BlockSpec((tm, width), lambda i: (i, 0))
    nxt = lambda width: pl.BlockSpec((CONV_HALO, width), lambda i: (jnp.minimum((i + 1) * hb, last_halo), 0))
    return pl.pallas_call(
        body, name="conv_backward", grid=(nt,),
        in_specs=[cur(cw), nxt(cw), cur(cw), nxt(cw), cur(2 * cw),
                  pl.BlockSpec((CONV_HALO, 2 * cw), lambda i: (jnp.maximum(i * hb - 1, 0), 0)),
                  pl.BlockSpec((CONV_K, cw), lambda i: (0, 0)), vec, vec],
        out_specs=[cur(2 * cw), pl.BlockSpec((CONV_K, cw), lambda i: (0, 0)), vec, vec, vec],
        out_shape=[jax.ShapeDtypeStruct((t_rows, 2 * cw), BF16), jax.ShapeDtypeStruct((CONV_K, cw), F32),
                   jax.ShapeDtypeStruct((1, cw), F32), jax.ShapeDtypeStruct((1, cw), F32),
                   jax.ShapeDtypeStruct((1, cw), F32)],
        scratch_shapes=[pltpu.VMEM((tm + CONV_HALO, cw), F32), pltpu.VMEM((tm + CONV_HALO, cw), F32),
                        pltpu.VMEM((CONV_K, 8, cw), F32)],
        compiler_params=_params("arbitrary"),
    )(dout, dout, c, c, proj_a, proj_a, w, ln_g, ln_b)


def _attn_load_kv(kv_hbm, k_pad, v_pad, sem, t_cols):
    k_pad[:, 0:LEFT] = jnp.zeros((ATTN_WIDTH, LEFT), BF16)
    v_pad[:, 0:LEFT] = jnp.zeros((ATTN_WIDTH, LEFT), BF16)
    ck = pltpu.make_async_copy(kv_hbm.at[pl.ds(ATTN_WIDTH, ATTN_WIDTH), :], k_pad.at[:, pl.ds(LEFT, t_cols)], sem.at[0])
    cv = pltpu.make_async_copy(kv_hbm.at[pl.ds(2 * ATTN_WIDTH, ATTN_WIDTH), :], v_pad.at[:, pl.ds(LEFT, t_cols)], sem.at[1])
    ck.start()
    cv.start()
    ck.wait()
    cv.wait()


def _attn_build_bias(tab_ref, bias_t):
    row = lax.broadcasted_iota(jnp.int32, (LANES, LANES), 0)
    lane = lax.broadcasted_iota(jnp.int32, (LANES, LANES), 1)
    upper = lane >= row
    lane64 = lax.broadcasted_iota(jnp.int32, (CHUNK, LANES), 1)
    for h in range(N_HEADS):
        far = jnp.broadcast_to(tab_ref[h:h + 1, 2 * MAX_REL:2 * MAX_REL + 1], (LANES, LANES))
        hi = jnp.broadcast_to(tab_ref[h:h + 1, MAX_REL:2 * MAX_REL], (LANES, LANES))
        lo = jnp.broadcast_to(tab_ref[h:h + 1, 0:MAX_REL], (LANES, LANES))
        hi_d = pltpu.roll(hi, 0, 1, stride=1, stride_axis=0)
        lo_d = pltpu.roll(lo, 0, 1, stride=1, stride_axis=0)
        bias_t[h, 0:WIN - 2 * LANES, :] = jnp.broadcast_to(far[0:1, :], (WIN - 2 * LANES, LANES))
        bias_t[h, WIN - 2 * LANES:WIN - LANES, :] = jnp.where(upper, far, hi_d)
        bias_t[h, WIN - LANES:WIN, :] = jnp.where(upper, hi_d, lo_d)
        bias_t[h, 0:CHUNK, :] = jnp.where(lane64 < CHUNK, bias_t[h, 0:CHUNK, :], NEG_INF)
        bias_t[h, WIN - CHUNK:WIN, :] = jnp.where(lane64 >= CHUNK, bias_t[h, WIN - CHUNK:WIN, :], NEG_INF)


def _head_rows(h):
    return slice(h * HEAD_DIM, (h + 1) * HEAD_DIM)


def _attn_scores(k_pad, q_ref, s_buf, w0):
    for h in range(N_HEADS):
        q_h = q_ref[_head_rows(h), :] * jnp.asarray(SCALE, BF16)
        s_buf[h] = lax.dot_general(k_pad[_head_rows(h), pl.ds(w0, WIN)], q_h, _DOT_DIMS["tn"],
                                   preferred_element_type=F32)


def _attn_logits(s, bias, first_valid, key0=0):
    s = s + bias
    if first_valid is not None:
        s = jnp.where(lax.broadcasted_iota(jnp.int32, s.shape, 0) + key0 >= first_valid, s, NEG_INF)
    return s


def _attn_probs(s, bias_h, first_valid):
    s = _attn_logits(s, bias_h, first_valid)
    top = jnp.max(s, axis=0, keepdims=True)
    e = jnp.exp(s - top)
    total = jnp.sum(e, axis=0, keepdims=True)
    return e * (1.0 / total), top + jnp.log(total)


def _attn_by_padding(m, fn):
    @pl.when(m < LEFT // QBLK)
    def _():
        fn(LEFT - m * QBLK)

    @pl.when(m >= LEFT // QBLK)
    def _():
        fn(None)


def _attn_forward(qkv_t, rel_bias):
    t_cols = qkv_t.shape[1]
    steps = t_cols // QBLK

    def body(q_ref, kv_hbm, tab_ref, o_ref, lse_ref, k_pad, v_pad, bias_t, s_buf, p_buf, sem):
        m = pl.program_id(0)

        @pl.when(m == 0)
        def _():
            _attn_build_bias(tab_ref, bias_t)
            _attn_load_kv(kv_hbm, k_pad, v_pad, sem, t_cols)

        w0 = pl.multiple_of(m * QBLK, QBLK)
        _attn_scores(k_pad, q_ref, s_buf, w0)

        def softmax(first_valid):
            for h in range(N_HEADS):
                p, lse = _attn_probs(s_buf[h], bias_t[h], first_valid)
                p_buf[h] = p.astype(BF16)
                lse_ref[h:h + 1, :] = lse

        _attn_by_padding(m, softmax)
        for h in range(N_HEADS):
            o_h = lax.dot_general(v_pad[_head_rows(h), pl.ds(w0, WIN)], p_buf[h], _DOT_DIMS["nn"],
                                  preferred_element_type=F32)
            o_ref[_head_rows(h), :] = o_h.astype(BF16)

    return pl.pallas_call(
        body, name="attn_forward", grid=(steps,),
        in_specs=[pl.BlockSpec((ATTN_WIDTH, QBLK), lambda m: (0, m)),
                  pl.BlockSpec(memory_space=pl.ANY),
                  pl.BlockSpec((N_HEADS, 2 * MAX_REL + 1), lambda m: (0, 0))],
        out_specs=[pl.BlockSpec((ATTN_WIDTH, QBLK), lambda m: (0, m)), pl.BlockSpec((N_HEADS, QBLK), lambda m: (0, m))],
        out_shape=[jax.ShapeDtypeStruct((ATTN_WIDTH, t_cols), BF16), jax.ShapeDtypeStruct((N_HEADS, t_cols), F32)],
        scratch_shapes=[pltpu.VMEM((ATTN_WIDTH, LEFT + t_cols), BF16), pltpu.VMEM((ATTN_WIDTH, LEFT + t_cols), BF16),
                        pltpu.VMEM((N_HEADS, WIN, QBLK), F32), pltpu.VMEM((N_HEADS, WIN, QBLK), F32),
                        pltpu.VMEM((N_HEADS, WIN, QBLK), BF16), pltpu.SemaphoreType.DMA((2,))],
        compiler_params=_params("arbitrary"),
    )(qkv_t, qkv_t, rel_bias)


def _reverse_lanes(v, flip):
    out = jnp.zeros(v.shape, F32)
    rest = v
    for _ in range(3):
        piece = rest.astype(BF16)
        out = out + lax.dot_general(piece, flip, _DOT_DIMS["nn"], preferred_element_type=F32)
        rest = rest - piece.astype(F32)
    return out


def _attn_bias_grad(dbias_t, drel_ref):
    row = lax.broadcasted_iota(jnp.int32, (LANES, LANES), 0)
    lane = lax.broadcasted_iota(jnp.int32, (LANES, LANES), 1)
    flip = (row + lane == LANES - 1).astype(BF16)
    head = lax.broadcasted_iota(jnp.int32, (N_HEADS, LANES), 0)
    lane8 = lax.broadcasted_iota(jnp.int32, (N_HEADS, LANES), 1)
    upper_rev = jnp.zeros((N_HEADS, LANES), F32)
    lower_rev = jnp.zeros((N_HEADS, LANES), F32)
    far = jnp.zeros((N_HEADS, LANES), F32)
    for h in range(N_HEADS):
        def diagonals(block):
            skew = pltpu.roll(_reverse_lanes(block, flip), 0, 1, stride=1, stride_axis=0)
            pos = jnp.sum(jnp.where(lane >= row, skew, 0.0), axis=0, keepdims=True)
            neg = jnp.sum(jnp.where(lane < row, skew, 0.0), axis=0, keepdims=True)
            return pos, neg

        pos4, neg4 = diagonals(dbias_t[h, WIN - LANES:WIN, :])
        pos3, neg3 = diagonals(dbias_t[h, WIN - 2 * LANES:WIN - LANES, :])
        far_h = jnp.sum(jnp.sum(dbias_t[h, 0:WIN - 2 * LANES, :], axis=0, keepdims=True), axis=1, keepdims=True)
        far_h = far_h + jnp.sum(pos3, axis=1, keepdims=True)
        upper_rev = jnp.where(head == h, pos4 + neg3, upper_rev)
        lower_rev = jnp.where(head == h, neg4, lower_rev)
        far = jnp.where((head == h) & (lane8 == 0), far_h, far)
    drel_ref[:, 0:LANES] = _reverse_lanes(lower_rev, flip)
    drel_ref[:, LANES:2 * LANES] = _reverse_lanes(upper_rev, flip)
    drel_ref[:, 2 * LANES:3 * LANES] = far


def _attn_backward(qkv_t, o_t, do_t, lse, rel_bias):
    t_cols = qkv_t.shape[1]
    steps = t_cols // QBLK
    flush = LEFT // QBLK
    total = steps + flush

    def body(q_ref, o_ref, do_ref, lse_ref, kv_hbm, tab_ref, dq_ref, dk_ref, dv_ref, drel_ref,
             k_pad, v_pad, bias_t, dbias_t, dk_acc, dv_acc, s_buf, dp_buf, p_buf, ds_buf, sem):
        m = pl.program_id(0)

        @pl.when(m == 0)
        def _():
            _attn_build_bias(tab_ref, bias_t)
            _attn_load_kv(kv_hbm, k_pad, v_pad, sem, t_cols)
            dbias_t[...] = jnp.zeros_like(dbias_t)
            dk_acc[...] = jnp.zeros_like(dk_acc)
            dv_acc[...] = jnp.zeros_like(dv_acc)

        @pl.when(m < steps)
        def _():
            w0 = pl.multiple_of(m * QBLK, QBLK)
            _attn_scores(k_pad, q_ref, s_buf, w0)
            for h in range(N_HEADS):
                dp_buf[h] = lax.dot_general(v_pad[_head_rows(h), pl.ds(w0, WIN)], do_ref[_head_rows(h), :],
                                            _DOT_DIMS["tn"], preferred_element_type=F32)

            def softmax_backward(first_valid):
                for h in range(N_HEADS):
                    rows = _head_rows(h)
                    delta = jnp.sum(do_ref[rows, :].astype(F32) * o_ref[rows, :].astype(F32), axis=0, keepdims=True)
                    lse_h = lse_ref[h:h + 1, :]
                    for b in range(WIN // LANES):
                        keys = slice(b * LANES, (b + 1) * LANES)
                        s = _attn_logits(s_buf[h, keys, :], bias_t[h, keys, :], first_valid, b * LANES)
                        p = jnp.exp(s - lse_h)
                        ds = p * (dp_buf[h, keys, :] - delta)
                        dbias_t[h, keys, :] += ds
                        p_buf[h, keys, :] = p.astype(BF16)
                        ds_buf[h, keys, :] = (ds * SCALE).astype(BF16)

            _attn_by_padding(m, softmax_backward)
            for h in range(N_HEADS):
                rows = _head_rows(h)
                dq_h = lax.dot_general(k_pad[rows, pl.ds(w0, WIN)], ds_buf[h], _DOT_DIMS["nn"], preferred_element_type=F32)
                dq_ref[rows, :] = dq_h.astype(BF16)
                dk_acc[rows, :] += lax.dot_general(q_ref[rows, :], ds_buf[h], _DOT_DIMS["nt"], preferred_element_type=F32)
                dv_acc[rows, :] += lax.dot_general(do_ref[rows, :], p_buf[h], _DOT_DIMS["nt"], preferred_element_type=F32)

        dk_ref[...] = dk_acc[:, 0:QBLK].astype(BF16)
        dv_ref[...] = dv_acc[:, 0:QBLK].astype(BF16)
        for acc in (dk_acc, dv_acc):
            rest = acc[:, QBLK:WIN]
            acc[:, 0:LEFT] = rest
            acc[:, LEFT:WIN] = jnp.zeros((ATTN_WIDTH, QBLK), F32)

        @pl.when(m == total - 1)
        def _():
            _attn_bias_grad(dbias_t, drel_ref)

    qblk = pl.BlockSpec((ATTN_WIDTH, QBLK), lambda m: (0, jnp.minimum(m, steps - 1)))
    kblk = pl.BlockSpec((ATTN_WIDTH, QBLK), lambda m: (0, jnp.maximum(m - flush, 0)))
    dq, dk, dv, drel = pl.pallas_call(
        body, name="attn_backward", grid=(total,),
        in_specs=[qblk, qblk, qblk, pl.BlockSpec((N_HEADS, QBLK), lambda m: (0, jnp.minimum(m, steps - 1))),
                  pl.BlockSpec(memory_space=pl.ANY), pl.BlockSpec((N_HEADS, 2 * MAX_REL + 1), lambda m: (0, 0))],
        out_specs=[qblk, kblk, kblk, pl.BlockSpec((N_HEADS, 3 * LANES), lambda m: (0, 0))],
        out_shape=[jax.ShapeDtypeStruct((ATTN_WIDTH, t_cols), BF16)] * 3
        + [jax.ShapeDtypeStruct((N_HEADS, 3 * LANES), F32)],
        scratch_shapes=[pltpu.VMEM((ATTN_WIDTH, LEFT + t_cols), BF16), pltpu.VMEM((ATTN_WIDTH, LEFT + t_cols), BF16),
                        pltpu.VMEM((N_HEADS, WIN, QBLK), F32), pltpu.VMEM((N_HEADS, WIN, QBLK), F32),
                        pltpu.VMEM((ATTN_WIDTH, WIN), F32), pltpu.VMEM((ATTN_WIDTH, WIN), F32),
                        pltpu.VMEM((N_HEADS, WIN, QBLK), F32), pltpu.VMEM((N_HEADS, WIN, QBLK), F32),
                        pltpu.VMEM((N_HEADS, WIN, QBLK), BF16), pltpu.VMEM((N_HEADS, WIN, QBLK), BF16),
                        pltpu.SemaphoreType.DMA((2,))],
        compiler_params=_params("arbitrary"),
    )(qkv_t, o_t, do_t, lse, qkv_t, rel_bias)
    return dq, dk, dv, drel


FFN_TC = 256


def _ffn_specs(t_rows, tr):
    nj = D_FF // FFN_TC
    hb = tr // FFN_HALO
    last_halo = t_rows // FFN_HALO - 1
    cur = lambda off: pl.BlockSpec((tr, FFN_TC), lambda j, i: (i, j + off))
    prev = lambda off: pl.BlockSpec((FFN_HALO, FFN_TC), lambda j, i: (jnp.maximum(i * hb - 1, 0), j + off))
    nxt = lambda off: pl.BlockSpec((FFN_HALO, FFN_TC), lambda j, i: (jnp.minimum((i + 1) * hb, last_halo), j + off))
    wspec = lambda off: pl.BlockSpec((FFN_K, FFN_TC), lambda j, i: (0, j + off))
    bspec = lambda off: pl.BlockSpec((1, FFN_TC), lambda j, i: (0, j + off))
    return nj, cur, prev, nxt, wspec, bspec


FFN_STRIP = 16


def _ffn_conv(win, w, b, rows):
    out = b + w[2] * win[FFN_HALO:FFN_HALO + rows, :]
    out = out + w[1] * win[FFN_HALO - 1:FFN_HALO - 1 + rows, :]
    return out + w[0] * win[FFN_HALO - 2:FFN_HALO - 2 + rows, :]


def _taps(w_ref):
    return [w_ref[kk:kk + 1, :] for kk in range(FFN_K)]


def _ffn_first_window(prev_ref, cur_ref, tile, rows):
    return jnp.concatenate([jnp.where(tile > 0, prev_ref[...], 0.0), cur_ref[0:rows, :]], axis=0)


def _fold8(v):
    return jnp.sum(v.reshape(v.shape[0] // 8, 8, v.shape[1]), axis=0)


def _ffn_activation(hup, w, b):
    t_rows = hup.shape[0]
    tr = _row_tile(t_rows, 512)
    nj, cur, prev, nxt, wspec, bspec = _ffn_specs(t_rows, tr)
    rs = 2 * FFN_STRIP

    def body(g_ref, gprev_ref, v_ref, vprev_ref, wg_ref, wv_ref, bg_ref, bv_ref, act_ref):
        i = pl.program_id(1)
        wg, wv, bg, bv = _taps(wg_ref), _taps(wv_ref), bg_ref[...], bv_ref[...]

        def emit(base, g_win, v_win):
            act = _gelu(_ffn_conv(g_win, wg, bg, rs)) * _ffn_conv(v_win, wv, bv, rs)
            act_ref[pl.ds(base, rs), :] = act.astype(BF16)

        def strip(s, carry):
            base = pl.multiple_of(s * rs, rs)
            emit(base, g_ref[pl.ds(base - FFN_HALO, rs + FFN_HALO), :], v_ref[pl.ds(base - FFN_HALO, rs + FFN_HALO), :])
            return carry

        emit(0, _ffn_first_window(gprev_ref, g_ref, i, rs), _ffn_first_window(vprev_ref, v_ref, i, rs))
        lax.fori_loop(1, tr // rs, strip, 0)

    return pl.pallas_call(
        body, name="ffn_activation", grid=(nj, t_rows // tr),
        in_specs=[cur(0), prev(0), cur(nj), prev(nj), wspec(0), wspec(nj), bspec(0), bspec(nj)],
        out_specs=cur(0),
        out_shape=jax.ShapeDtypeStruct((t_rows, D_FF), BF16),
        compiler_params=_params("parallel", "parallel"),
    )(hup, hup, hup, hup, w, w, b, b)


def _ffn_backward(dact, hup, w, b):
    t_rows = hup.shape[0]
    tr = _row_tile(t_rows, 512)
    nj, cur, prev, nxt, wspec, bspec = _ffn_specs(t_rows, tr)
    ni = t_rows // tr
    rs = FFN_STRIP
    ns = tr // rs

    def body(da_ref, danext_ref, g_ref, gprev_ref, gnext_ref, v_ref, vprev_ref, vnext_ref,
             wg_ref, wv_ref, bg_ref, bv_ref,
             dhg_ref, dhv_ref, dwg_ref, dwv_ref, dbg_ref, dbv_ref):
        i = pl.program_id(1)

        @pl.when(i == 0)
        def _():
            for ref in (dwg_ref, dwv_ref, dbg_ref, dbv_ref):
                ref[...] = jnp.zeros_like(ref)

        wg, wv, bg, bv = _taps(wg_ref), _taps(wv_ref), bg_ref[...], bv_ref[...]

        def conv_grads(da, g_win, v_win, rows):
            gel, dgel = _gelu_parts(_ffn_conv(g_win, wg, bg, rows))
            return da * _ffn_conv(v_win, wv, bv, rows) * dgel, da * gel

        da_after = jnp.where(i < ni - 1, danext_ref[...], 0.0)
        g_after = jnp.concatenate([g_ref[tr - FFN_HALO:tr, :], gnext_ref[...]], axis=0)
        v_after = jnp.concatenate([v_ref[tr - FFN_HALO:tr, :], vnext_ref[...]], axis=0)
        dcg_after, dcv_after = conv_grads(da_after, g_after, v_after, FFN_HALO)
        zero8 = jnp.zeros((8, FFN_TC), F32)

        def strip_at(base, g_win, v_win, carry):
            dcg_after, dcv_after, dbg, dbv, dwg, dwv = carry
            dcg, dcv = conv_grads(da_ref[pl.ds(base, rs), :], g_win, v_win, rs)
            out = []
            for dc, after, taps, win, dh_ref, db, dw in ((dcg, dcg_after, wg, g_win, dhg_ref, dbg, dwg),
                                                         (dcv, dcv_after, wv, v_win, dhv_ref, dbv, dwv)):
                ext = jnp.concatenate([dc, after], axis=0)
                dh = taps[2] * dc + taps[1] * ext[1:1 + rs, :] + taps[0] * ext[2:2 + rs, :]
                dh_ref[pl.ds(base, rs), :] = dh.astype(BF16)
                db = db + _fold8(dc)
                dw = tuple(dw[kk] + _fold8(dc * win[FFN_HALO - 2 + kk:FFN_HALO - 2 + kk + rs, :]) for kk in range(FFN_K))
                out.append((dc[0:FFN_HALO, :], db, dw))
            return out[0][0], out[1][0], out[0][1], out[1][1], out[0][2], out[1][2]

        def strip(s, carry):
            base = pl.multiple_of((ns - 1 - s) * rs, rs)
            return strip_at(base, g_ref[pl.ds(base - FFN_HALO, rs + FFN_HALO), :],
                            v_ref[pl.ds(base - FFN_HALO, rs + FFN_HALO), :], carry)

        init = (dcg_after, dcv_after, zero8, zero8, (zero8,) * FFN_K, (zero8,) * FFN_K)
        carry = lax.fori_loop(0, ns - 1, strip, init)
        _, _, dbg, dbv, dwg, dwv = strip_at(0, _ffn_first_window(gprev_ref, g_ref, i, rs),
                                            _ffn_first_window(vprev_ref, v_ref, i, rs), carry)
        dbg_ref[...] += jnp.sum(dbg, axis=0, keepdims=True)
        dbv_ref[...] += jnp.sum(dbv, axis=0, keepdims=True)
        for kk in range(FFN_K):
            dwg_ref[kk:kk + 1, :] += jnp.sum(dwg[kk], axis=0, keepdims=True)
            dwv_ref[kk:kk + 1, :] += jnp.sum(dwv[kk], axis=0, keepdims=True)

    half = jax.ShapeDtypeStruct((t_rows, D_FF), BF16)
    return pl.pallas_call(
        body, name="ffn_backward", grid=(nj, ni),
        in_specs=[cur(0), nxt(0), cur(0), prev(0), nxt(0), cur(nj), prev(nj), nxt(nj),
                  wspec(0), wspec(nj), bspec(0), bspec(nj)],
        out_specs=[cur(0), cur(0), wspec(0), wspec(0), bspec(0), bspec(0)],
        out_shape=[half, half, jax.ShapeDtypeStruct((FFN_K, D_FF), F32), jax.ShapeDtypeStruct((FFN_K, D_FF), F32),
                   jax.ShapeDtypeStruct((1, D_FF), F32), jax.ShapeDtypeStruct((1, D_FF), F32)],
        compiler_params=_params("parallel", "arbitrary"),
    )(dact, dact, hup, hup, hup, hup, hup, hup, w, w, b, b)


def _mesh_position():
    return lax.axis_index("x"), lax.axis_index("y"), lax.axis_index("c")


def _hbm_specs(n):
    return [pl.BlockSpec(memory_space=pl.ANY)] * n


def _all_gather(shards, name, placed=None):
    n = len(shards)

    def body(*refs):
        ins = refs[:n]
        outs = refs[2 * n:3 * n] if placed else refs[n:2 * n]
        send_sems, recv_sems, local_sems = refs[-3:]
        x, y, c = _mesh_position()
        me, sibling = (x, y, c), (x, y, 1 - c)
        chips = [(1 - x, y), (x, 1 - y), (1 - x, 1 - y)]

        def copy(a, slot, block, to, src=None):
            dst = outs[a].at[4 * block[0] + 2 * block[1] + block[2]]
            return pltpu.make_async_remote_copy(
                src_ref=dst if src is None else src, dst_ref=dst,
                send_sem=send_sems.at[a, slot], recv_sem=recv_sems.at[a, slot],
                device_id=to, device_id_type=MESH)

        started = []
        for a in range(0 if placed else n):
            mine = pltpu.make_async_copy(ins[a], outs[a].at[4 * x + 2 * y + c], local_sems.at[a])
            mine.start()
            started.append(mine)
        first = []
        for a in range(n):
            first.append(copy(a, 0, me, sibling, src=ins[a]))
            first += [copy(a, 1 + j, me, (*chip, c), src=ins[a]) for j, chip in enumerate(chips)]
        for cp in first:
            cp.start()
        passed = []
        for j, chip in enumerate(chips):
            for a in range(n):
                copy(a, 1 + j, (*chip, c), me).wait_recv()
                fwd = copy(a, 4 + j, (*chip, c), sibling)
                fwd.start()
                passed.append(fwd)
        for a in range(n):
            copy(a, 0, sibling, me).wait_recv()
            for j, chip in enumerate(chips):
                copy(a, 4 + j, (*chip, 1 - c), me).wait_recv()
        for cp in first + passed:
            cp.wait_send()
        for mine in started:
            mine.wait()

    operands = [*shards, *placed] if placed else list(shards)
    return pl.pallas_call(
        body, name=name,
        in_specs=_hbm_specs(len(operands)), out_specs=_hbm_specs(n),
        out_shape=[jax.ShapeDtypeStruct((N_DEV,) + s.shape, s.dtype) for s in shards],
        scratch_shapes=[pltpu.SemaphoreType.DMA((n, 7)), pltpu.SemaphoreType.DMA((n, 7)),
                        pltpu.SemaphoreType.DMA((n,))],
        input_output_aliases={n + a: a for a in range(n)} if placed else {},
        compiler_params=pltpu.CompilerParams(has_side_effects=True),
    )(*operands)


def _place_own(shards, me):
    n = len(shards)

    def body(me_ref, *refs):
        for src, dst in zip(refs[:n], refs[n:]):
            dst[0] = src[...]

    return pl.pallas_call(
        body, name="place_own_shards",
        grid_spec=pltpu.PrefetchScalarGridSpec(
            num_scalar_prefetch=1, grid=(1,),
            in_specs=[pl.BlockSpec(s.shape, lambda i, me_ref: (0, 0)) for s in shards],
            out_specs=[pl.BlockSpec((1,) + s.shape, lambda i, me_ref: (me_ref[0], 0, 0)) for s in shards]),
        out_shape=[jax.ShapeDtypeStruct((N_DEV,) + s.shape, s.dtype) for s in shards],
        compiler_params=_params("arbitrary"),
    )(me, *shards)


_FLIPS = [(dx, dy, dc) for dx in (0, 1) for dy in (0, 1) for dc in (0, 1)][1:]
_HBM = pl.BlockSpec(memory_space=pltpu.HBM)
_SEM = pl.BlockSpec(memory_space=pltpu.SEMAPHORE)
_DATAFLOW = pltpu.SideEffectType.DATAFLOW_SIDE_EFFECTING


def _scatter_copies(src_refs, land_refs, send_sems, recv_sems, gather):
    x, y, c = _mesh_position()
    me = 4 * x + 2 * y + c
    copies = []
    for a, (src, land) in enumerate(zip(src_refs, land_refs)):
        for k, (dx, dy, dc) in enumerate(_FLIPS):
            px, py, pc = (x + dx) % 2, (y + dy) % 2, (c + dc) % 2
            pair = a * len(_FLIPS) + k
            copies.append(pltpu.make_async_remote_copy(
                src_ref=src if gather else src.at[4 * px + 2 * py + pc], dst_ref=land.at[me],
                send_sem=send_sems[pair], recv_sem=recv_sems[pair],
                device_id=(px, py, pc), device_id_type=MESH))
    return copies


def _scatter_start(srcs, lands, gather, name):
    n = len(srcs)
    pairs = n * len(_FLIPS)

    def body(*refs):
        src_refs, land_refs = refs[:n], refs[n:2 * n]
        send_sems, recv_sems = refs[2 * n:2 * n + pairs], refs[2 * n + pairs:2 * n + 2 * pairs]
        token = refs[-1]
        for cp in _scatter_copies(src_refs, land_refs, send_sems, recv_sems, gather):
            cp.start()
        token[...] = jnp.zeros_like(token)

    arrays = [*srcs, *lands]
    sem = pltpu.SemaphoreType.DMA(())
    out = pl.pallas_call(
        body, name=name,
        out_shape=(*[sem] * (2 * pairs), *[pltpu.HBM(v.shape, v.dtype) for v in arrays],
                   jax.ShapeDtypeStruct((8, LANES), F32)),
        in_specs=[_HBM] * (2 * n),
        out_specs=(*[_SEM] * (2 * pairs), *[_HBM] * (2 * n), pl.BlockSpec(memory_space=pltpu.VMEM)),
        input_output_aliases={i: 2 * pairs + i for i in range(2 * n)},
        compiler_params=pltpu.CompilerParams(has_side_effects=_DATAFLOW),
    )(*[pltpu.with_memory_space_constraint(v, pltpu.HBM) for v in arrays])
    sems, rest = out[:2 * pairs], out[2 * pairs:]
    return list(sems[:pairs]), list(sems[pairs:]), list(rest[:n]), list(rest[n:2 * n]), rest[-1]


def _scatter_wait(send_sems, recv_sems, srcs, lands, after, gather, name):
    n = len(srcs)
    pairs = n * len(_FLIPS)

    def body(*refs):
        src_refs, land_refs = refs[:n], refs[n:2 * n]
        send_refs, recv_refs = refs[2 * n:2 * n + pairs], refs[2 * n + pairs:2 * n + 2 * pairs]
        for cp in _scatter_copies(src_refs, land_refs, send_refs, recv_refs, gather):
            cp.wait_send()
            cp.wait_recv()

    arrays = [*srcs, *lands]
    out = pl.pallas_call(
        body, name=name,
        out_shape=tuple(pltpu.HBM(v.shape, v.dtype) for v in arrays),
        in_specs=[*[_HBM] * (2 * n), *[_SEM] * (2 * pairs), pl.BlockSpec(memory_space=pl.ANY)],
        out_specs=tuple([_HBM] * (2 * n)),
        input_output_aliases={i: i for i in range(2 * n)},
        compiler_params=pltpu.CompilerParams(has_side_effects=_DATAFLOW),
    )(*arrays, *send_sems, *recv_sems, after)
    return list(out[:n]), list(out[n:])


def _sum_received(grad, received, me, name):
    _, rows, cols = grad.shape

    def body(me_ref, g_ref, r_ref, o_ref):
        p = pl.program_id(0)
        term = jnp.where(p == me_ref[0], g_ref[0], r_ref[0])

        @pl.when(p == 0)
        def _():
            o_ref[...] = term

        @pl.when(p > 0)
        def _():
            o_ref[...] += term

    blk = (1, rows, cols)
    return pl.pallas_call(
        body, name=name,
        grid_spec=pltpu.PrefetchScalarGridSpec(
            num_scalar_prefetch=1, grid=(N_DEV,),
            in_specs=[pl.BlockSpec(blk, lambda p, me_ref: (me_ref[0], 0, 0)),
                      pl.BlockSpec(blk, lambda p, me_ref: (p, 0, 0))],
            out_specs=pl.BlockSpec((rows, cols), lambda p, me_ref: (0, 0))),
        out_shape=jax.ShapeDtypeStruct((rows, cols), F32),
        compiler_params=_params("arbitrary"),
    )(me, grad, received)


def _exchange_in_chip(grads):
    n = len(grads)

    def body(*refs):
        ins, outs = refs[:n], refs[n:2 * n]
        send_sems, recv_sems = refs[2 * n:]
        x, y, c = _mesh_position()
        copies = []
        for a in range(n):
            for q in range(4):
                copies.append(pltpu.make_async_remote_copy(
                    src_ref=ins[a].at[2 * q + (1 - c)], dst_ref=outs[a].at[q],
                    send_sem=send_sems.at[a, q], recv_sem=recv_sems.at[a, q],
                    device_id=(x, y, 1 - c), device_id_type=MESH))
        for cp in copies:
            cp.start()
        for cp in copies:
            cp.wait_recv()
        for cp in copies:
            cp.wait_send()

    return pl.pallas_call(
        body, name="exchange_in_chip",
        in_specs=_hbm_specs(n), out_specs=_hbm_specs(n),
        out_shape=[jax.ShapeDtypeStruct((4,) + g.shape[1:], g.dtype) for g in grads],
        scratch_shapes=[pltpu.SemaphoreType.DMA((n, 4)), pltpu.SemaphoreType.DMA((n, 4))],
        compiler_params=pltpu.CompilerParams(has_side_effects=True),
    )(*grads)


def _exchange_between_chips(partials):
    n = len(partials)

    def body(*refs):
        ins, outs = refs[:n], refs[n:2 * n]
        send_sems, recv_sems = refs[2 * n:]
        x, y, c = _mesh_position()
        chips = [(1 - x, y), (x, 1 - y), (1 - x, 1 - y)]
        copies = []
        for a in range(n):
            for j, (px, py) in enumerate(chips):
                copies.append(pltpu.make_async_remote_copy(
                    src_ref=ins[a].at[2 * px + py], dst_ref=outs[a].at[j],
                    send_sem=send_sems.at[a, j], recv_sem=recv_sems.at[a, j],
                    device_id=(px, py, c), device_id_type=MESH))
        for cp in copies:
            cp.start()
        for cp in copies:
            cp.wait_recv()
        for cp in copies:
            cp.wait_send()

    return pl.pallas_call(
        body, name="exchange_between_chips",
        in_specs=_hbm_specs(n), out_specs=_hbm_specs(n),
        out_shape=[jax.ShapeDtypeStruct((3,) + p.shape[1:], p.dtype) for p in partials],
        scratch_shapes=[pltpu.SemaphoreType.DMA((n, 3)), pltpu.SemaphoreType.DMA((n, 3))],
        compiler_params=pltpu.CompilerParams(has_side_effects=True),
    )(*partials)


def _add_in_chip(grad, received, core, name):
    _, rows, cols = grad.shape

    def body(core_ref, g_ref, r_ref, o_ref):
        o_ref[...] = g_ref[...] + r_ref[...]

    blk = (1, rows, cols)
    return pl.pallas_call(
        body, name=name,
        grid_spec=pltpu.PrefetchScalarGridSpec(
            num_scalar_prefetch=1, grid=(4,),
            in_specs=[pl.BlockSpec(blk, lambda q, core_ref: (2 * q + core_ref[0], 0, 0)),
                      pl.BlockSpec(blk, lambda q, core_ref: (q, 0, 0))],
            out_specs=pl.BlockSpec(blk, lambda q, core_ref: (q, 0, 0))),
        out_shape=jax.ShapeDtypeStruct((4, rows, cols), F32),
        compiler_params=_params("parallel"),
    )(core, grad, received)


def _add_between_chips(partial, received, chip, name):
    _, rows, cols = partial.shape

    def body(chip_ref, p_ref, r_ref, o_ref):
        o_ref[...] = ((p_ref[0] + r_ref[0]) + r_ref[1]) + r_ref[2]

    return pl.pallas_call(
        body, name=name,
        grid_spec=pltpu.PrefetchScalarGridSpec(
            num_scalar_prefetch=1, grid=(1,),
            in_specs=[pl.BlockSpec((1, rows, cols), lambda i, chip_ref: (chip_ref[0], 0, 0)),
                      pl.BlockSpec((3, rows, cols), lambda i, chip_ref: (0, 0, 0))],
            out_specs=pl.BlockSpec((rows, cols), lambda i, chip_ref: (0, 0))),
        out_shape=jax.ShapeDtypeStruct((rows, cols), F32),
        compiler_params=_params("arbitrary"),
    )(chip, partial, received)


def _sum_devices(gathered):
    _, rows, cols = gathered.shape

    def body(g_ref, o_ref):
        total = g_ref[0]
        for d in range(1, N_DEV):
            total = total + g_ref[d]
        o_ref[...] = total

    return pl.pallas_call(
        body, name="sum_small_grads",
        out_shape=jax.ShapeDtypeStruct((rows, cols), F32),
        compiler_params=_params(),
    )(gathered)


def _adamw(w, g, m, v, name):
    rows, cols = w.shape
    tr = rows
    for cand in (256, 128, 64, 32, 16, 8):
        if rows > cand and rows % cand == 0:
            tr = cand
            break

    def body(w_ref, g_ref, m_ref, v_ref, delta_ref, newm_ref, newv_ref):
        g_v = g_ref[...]
        new_m = ADAM_B1 * m_ref[...] + (1.0 - ADAM_B1) * g_v
        new_v = ADAM_B2 * v_ref[...] + (1.0 - ADAM_B2) * (g_v * g_v)
        m_hat = new_m / (1.0 - ADAM_B1 ** ADAM_STEP)
        v_hat = new_v / (1.0 - ADAM_B2 ** ADAM_STEP)
        delta_ref[...] = -ADAM_LR * (m_hat / (jnp.sqrt(v_hat) + ADAM_EPS) + ADAM_WD * w_ref[...])
        newm_ref[...] = new_m
        newv_ref[...] = new_v

    blk = pl.BlockSpec((tr, cols), lambda i: (i, 0))
    shape = jax.ShapeDtypeStruct((rows, cols), F32)
    return pl.pallas_call(
        body, name=name, grid=(rows // tr,),
        in_specs=[blk] * 4, out_specs=[blk] * 3, out_shape=[shape] * 3,
        compiler_params=_params("parallel"),
    )(w, g, m, v)


def _pack(pieces, rows):
    flat = jnp.concatenate([p.reshape(-1) for p in pieces])
    return jnp.pad(flat, (0, rows * LANES - flat.shape[0])).reshape(rows, LANES)


def _unpack(packed, shapes):
    flat = packed.reshape(-1)
    out, pos = [], 0
    for shape in shapes:
        size = 1
        for s in shape:
            size *= s
        out.append(flat[pos:pos + size].reshape(shape))
        pos += size
    return out


def _rows_for(count):
    return -(-count // (8 * LANES)) * 8


SMALL = ("norm_mix_pre", "conv_dw_b", "conv_ln_g", "conv_ln_b", "rel_bias", "norm_mix_post", "norm_ffn_pre",
         "ffn_dw_b", "norm_ffn_post")
SHARDED_SMALL = ("conv_dw_w", "ffn_dw_w")
LARGE = ("w_in", "w_out", "w_up", "w_down")
WEIGHTS = ("norm_mix_pre", "w_in", "conv_dw_w", "conv_dw_b", "conv_ln_g", "conv_ln_b", "rel_bias", "w_out",
           "norm_mix_post", "norm_ffn_pre", "w_up", "ffn_dw_w", "ffn_dw_b", "w_down", "norm_ffn_post")


def kernel(x, norm_mix_pre, w_in, conv_dw_w, conv_dw_b, conv_ln_g, conv_ln_b, rel_bias, w_out, norm_mix_post, norm_ffn_pre, w_up, ffn_dw_w, ffn_dw_b, w_down, norm_ffn_post, loss_target, m_norm_mix_pre, m_w_in, m_conv_dw_w, m_conv_dw_b, m_conv_ln_g, m_conv_ln_b, m_rel_bias, m_w_out, m_norm_mix_post, m_norm_ffn_pre, m_w_up, m_ffn_dw_w, m_ffn_dw_b, m_w_down, m_norm_ffn_post, v_norm_mix_pre, v_w_in, v_conv_dw_w, v_conv_dw_b, v_conv_ln_g, v_conv_ln_b, v_rel_bias, v_w_out, v_norm_mix_post, v_norm_ffn_pre, v_w_up, v_ffn_dw_w, v_ffn_dw_b, v_w_down, v_norm_ffn_post):
    weights = dict(norm_mix_pre=norm_mix_pre, w_in=w_in, conv_dw_w=conv_dw_w, conv_dw_b=conv_dw_b, conv_ln_g=conv_ln_g,
                   conv_ln_b=conv_ln_b, rel_bias=rel_bias, w_out=w_out, norm_mix_post=norm_mix_post,
                   norm_ffn_pre=norm_ffn_pre, w_up=w_up, ffn_dw_w=ffn_dw_w, ffn_dw_b=ffn_dw_b, w_down=w_down,
                   norm_ffn_post=norm_ffn_post)
    mom1 = dict(norm_mix_pre=m_norm_mix_pre, w_in=m_w_in, conv_dw_w=m_conv_dw_w, conv_dw_b=m_conv_dw_b,
                conv_ln_g=m_conv_ln_g, conv_ln_b=m_conv_ln_b, rel_bias=m_rel_bias, w_out=m_w_out,
                norm_mix_post=m_norm_mix_post, norm_ffn_pre=m_norm_ffn_pre, w_up=m_w_up, ffn_dw_w=m_ffn_dw_w,
                ffn_dw_b=m_ffn_dw_b, w_down=m_w_down, norm_ffn_post=m_norm_ffn_post)
    mom2 = dict(norm_mix_pre=v_norm_mix_pre, w_in=v_w_in, conv_dw_w=v_conv_dw_w, conv_dw_b=v_conv_dw_b,
                conv_ln_g=v_conv_ln_g, conv_ln_b=v_conv_ln_b, rel_bias=v_rel_bias, w_out=v_w_out,
                norm_mix_post=v_norm_mix_post, norm_ffn_pre=v_norm_ffn_pre, w_up=v_w_up, ffn_dw_w=v_ffn_dw_w,
                ffn_dw_b=v_ffn_dw_b, w_down=v_w_down, norm_ffn_post=v_norm_ffn_post)

    x2 = x[0]
    target = loss_target[0]
    t_rows = x2.shape[0]
    d = D_MODEL
    in_cols = 2 * CONV_WIDTH + 3 * ATTN_WIDTH
    my_x, my_y, my_c = _mesh_position()
    my_dev = 4 * my_x + 2 * my_y + my_c

    small_conv = _pack([conv_dw_w[0], ffn_dw_w[0]], 32)
    me = jnp.reshape(my_dev, (1,)).astype(jnp.int32)
    first_shards = [w_in[0].T.astype(BF16), small_conv]
    late_shards = [w_out[0].astype(BF16), w_up[0].T.astype(BF16), w_down[0].astype(BF16)]
    placed = _place_own(first_shards + late_shards, me)
    win_t, conv_g = _all_gather(first_shards, "all_gather_weights", placed=placed[:2])
    late_lands = placed[2:]
    late_send, late_recv, late_shards, late_lands, late_token = _scatter_start(
        late_shards, late_lands, True, "gather_late_weights_start")
    win_t = win_t.reshape(in_cols, d)
    conv_flat = conv_g.reshape(N_DEV, 32 * LANES)
    n_cw = CONV_K * (CONV_WIDTH // N_DEV)
    conv_w_full = conv_flat[:, :n_cw].reshape(N_DEV, CONV_K, CONV_WIDTH // N_DEV).transpose(1, 0, 2).reshape(CONV_K, CONV_WIDTH)
    ffn_w_full = conv_flat[:, n_cw:].reshape(N_DEV, FFN_K, 2 * D_FF // N_DEV).transpose(1, 0, 2).reshape(FFN_K, 2 * D_FF)

    u1 = _pre_norm(x2, norm_mix_pre + late_token[0:1, 0:1], "pre_norm_mix")
    proj_a = _matmul(u1, win_t, mode="nt", m=t_rows, n=2 * CONV_WIDTH, k=d, tm=2048, tn=1024, tk=d,
                     out_dtype=F32, name="proj_conv")
    qkv_t = _matmul(win_t, u1, mode="nt", m=3 * ATTN_WIDTH, n=t_rows, k=d, tm=512, tn=2048, tk=d,
                    out_dtype=BF16, name="proj_qkv", a_m0=2 * CONV_WIDTH)
    conv_c, conv_out = _conv_forward(proj_a, conv_w_full, conv_dw_b, conv_ln_g, conv_ln_b)
    o_t, attn_lse = _attn_forward(qkv_t, rel_bias[0])
    _, (wout_g, wup_t, wdown_g) = _scatter_wait(late_send, late_recv, late_shards, late_lands, o_t, True,
                                                "gather_late_weights_wait")
    wout_g = wout_g.reshape(d, d)
    wup_t = wup_t.reshape(2 * D_FF, d)
    wdown_g = wdown_g.reshape(D_FF, d)
    mixed = _matmul(conv_out, wout_g, mode="nn", m=t_rows, n=d, k=CONV_WIDTH, tm=2048, tn=1024, tk=CONV_WIDTH,
                    out_dtype=F32, name="out_proj_conv")
    mixed = _matmul(o_t, wout_g, mode="tn", m=t_rows, n=d, k=ATTN_WIDTH, tm=1024, tn=1024, tk=ATTN_WIDTH,
                    out_dtype=F32, name="out_proj_attn", b_k0=CONV_WIDTH, add=mixed)
    h1, u2 = _mid_forward(x2, mixed, norm_mix_post, norm_ffn_pre)
    hup = _matmul(u2, wup_t, mode="nt", m=t_rows, n=2 * D_FF, k=d, tm=2048, tn=1408, tk=d,
                  out_dtype=F32, name="ffn_up")
    act = _ffn_activation(hup, ffn_w_full, ffn_dw_b)
    f = _matmul(act, wdown_g, mode="nn", m=t_rows, n=d, k=D_FF, tm=1024, tn=1024, tk=D_FF,
                out_dtype=F32, name="ffn_down")
    loss, dy, df, d_norm_ffn_post = _loss_and_head_backward(h1, f, target, norm_ffn_post)

    dact = _matmul(df, wdown_g, mode="nt", m=t_rows, n=D_FF, k=d, tm=2048, tn=1408, tk=d,
                   out_dtype=F32, name="ffn_down_dx")
    g_wdown = _matmul(act, df, mode="tn", m=D_FF, n=d, k=t_rows, tm=1408, tn=1024, tk=2048,
                      out_dtype=F32, name="ffn_down_dw")
    dhg, dhv, dwg, dwv, dbg, dbv = _ffn_backward(dact, hup, ffn_w_full, ffn_dw_b)
    du2 = _matmul(dhg, wup_t, mode="nn", m=t_rows, n=d, k=D_FF, tm=1024, tn=1024, tk=D_FF,
                  out_dtype=F32, name="ffn_up_dx_gate")
    du2 = _matmul(dhv, wup_t, mode="nn", m=t_rows, n=d, k=D_FF, tm=1024, tn=1024, tk=D_FF,
                  out_dtype=F32, name="ffn_up_dx_value", b_k0=D_FF, add=du2)
    g_wup_t = _matmul(dhg, u2, mode="tn", m=D_FF, n=d, k=t_rows, tm=1408, tn=1024, tk=2048, out_dtype=F32,
                      name="ffn_up_dw_gate", out_rows=2 * D_FF)
    g_wup_t = _matmul(dhv, u2, mode="tn", m=D_FF, n=d, k=t_rows, tm=1408, tn=1024, tk=2048, out_dtype=F32,
                      name="ffn_up_dw_value", out_rows=2 * D_FF, out_m0=D_FF, into=g_wup_t)
    ffn_grads = [g_wup_t.reshape(N_DEV, 2 * D_FF // N_DEV, d), g_wdown.reshape(N_DEV, D_FF // N_DEV, d)]
    red_send, red_recv, ffn_grads, red_lands, red_token = _scatter_start(
        ffn_grads, [lax.empty(g.shape, F32) for g in ffn_grads], False, "reduce_ffn_grads_start")
    dh1, dmixed, d_norm_ffn_pre, d_norm_mix_post = _mid_backward(
        dy, du2, h1, mixed, norm_ffn_pre + red_token[0:1, 0:1], norm_mix_post)
    dconv_out = _matmul(dmixed, wout_g, mode="nt", m=t_rows, n=CONV_WIDTH, k=d, tm=2048, tn=512, tk=d,
                        out_dtype=F32, name="out_proj_dx_conv")
    do_t = _matmul(wout_g, dmixed, mode="nt", m=ATTN_WIDTH, n=t_rows, k=d, tm=512, tn=2048, tk=d,
                   out_dtype=BF16, name="out_proj_dx_attn", a_m0=CONV_WIDTH)
    g_wout = _matmul(conv_out, dmixed, mode="tn", m=CONV_WIDTH, n=d, k=t_rows, tm=512, tn=1024, tk=2048, out_dtype=F32,
                     name="out_proj_dw_conv", out_rows=d)
    g_wout = _matmul(o_t, dmixed, mode="nn", m=ATTN_WIDTH, n=d, k=t_rows, tm=512, tn=1024, tk=2048, out_dtype=F32,
                     name="out_proj_dw_attn", out_rows=d, out_m0=CONV_WIDTH, into=g_wout)
    wout_handle = _scatter_start([g_wout.reshape(N_DEV, d // N_DEV, d)], [lax.empty((N_DEV, d // N_DEV, d), F32)],
                                 False, "reduce_w_out_grad_start")
    dproj_a, d_conv_w, d_conv_b, d_ln_g, d_ln_b = _conv_backward(
        dconv_out, conv_c, proj_a, conv_w_full, conv_ln_g + wout_handle[4][0:1, 0:1], conv_ln_b)
    dqkv_parts = _attn_backward(qkv_t, o_t, do_t, attn_lse, rel_bias[0])
    drel = dqkv_parts[3]
    du1 = _matmul(dproj_a, win_t, mode="nn", m=t_rows, n=d, k=2 * CONV_WIDTH, tm=2048, tn=1024, tk=1024,
                  out_dtype=F32, name="proj_dx_conv")
    g_win_t = _matmul(dproj_a, u1, mode="tn", m=2 * CONV_WIDTH, n=d, k=t_rows, tm=1024, tn=1024, tk=2048, out_dtype=F32,
                      name="proj_dw_conv", out_rows=in_cols)
    for j, part in enumerate("qkv"):
        row0 = 2 * CONV_WIDTH + j * ATTN_WIDTH
        du1 = _matmul(dqkv_parts[j], win_t, mode="tn", m=t_rows, n=d, k=ATTN_WIDTH, tm=1024, tn=1024, tk=ATTN_WIDTH,
                      out_dtype=F32, name="proj_dx_" + part, b_k0=row0, add=du1)
        g_win_t = _matmul(dqkv_parts[j], u1, mode="nn", m=ATTN_WIDTH, n=d, k=t_rows, tm=512, tn=1024, tk=2048,
                          out_dtype=F32, name="proj_dw_" + part, out_rows=in_cols, out_m0=row0, into=g_win_t)
    win_handle = _scatter_start([g_win_t.reshape(N_DEV, in_cols // N_DEV, d)],
                                [lax.empty((N_DEV, in_cols // N_DEV, d), F32)], False, "reduce_w_in_grad_start")
    dx, d_norm_mix_pre = _input_backward(dh1, du1, x2, norm_mix_pre + win_handle[4][0:1, 0:1])

    small_grads = dict(norm_mix_pre=d_norm_mix_pre, conv_dw_b=d_conv_b, conv_ln_g=d_ln_g, conv_ln_b=d_ln_b,
                       rel_bias=drel[:, :2 * MAX_REL + 1], norm_mix_post=d_norm_mix_post, norm_ffn_pre=d_norm_ffn_pre,
                       ffn_dw_b=jnp.concatenate([dbg, dbv], axis=1), norm_ffn_post=d_norm_ffn_post)
    pieces = [small_grads[nm] for nm in SMALL] + [d_conv_w, jnp.concatenate([dwg, dwv], axis=1), loss]
    count = sum(p.size for p in pieces)
    (gathered_small,) = _all_gather([_pack(pieces, _rows_for(count))], "all_gather_small_grads")
    summed = _sum_devices(gathered_small)
    shapes = [weights[nm].shape for nm in SMALL] + [(CONV_K, CONV_WIDTH), (FFN_K, 2 * D_FF), (1, 1)]
    unpacked = _unpack(summed, shapes)
    grads = dict(zip(SMALL, unpacked[:len(SMALL)]))
    cw_shard, fw_shard = CONV_WIDTH // N_DEV, 2 * D_FF // N_DEV
    grads["conv_dw_w"] = lax.dynamic_slice_in_dim(unpacked[-3], my_dev * cw_shard, cw_shard, axis=1)[None]
    grads["ffn_dw_w"] = lax.dynamic_slice_in_dim(unpacked[-2], my_dev * fw_shard, fw_shard, axis=1)[None]
    total_loss = unpacked[-1].reshape(())

    me = jnp.reshape(my_dev, (1,)).astype(jnp.int32)
    delta, new_m, new_v = {}, {}, {}

    def finish(nm, send, recv, srcs, lands, after, transposed):
        srcs, lands = _scatter_wait(send, recv, srcs, lands, after, False, "reduce_" + nm + "_grad_wait")
        for name_a, src, land in zip(nm.split("_and_"), srcs, lands):
            g = _sum_received(src, land, me, "sum_received_" + name_a)
            grads[name_a] = (g.T if transposed[name_a] else g)[None]
            dl, nm1, nv1 = _adamw(weights[name_a][0], grads[name_a][0], mom1[name_a][0], mom2[name_a][0], "adamw_" + name_a)
            delta[name_a], new_m[name_a], new_v[name_a] = dl[None], nm1[None], nv1[None]

    transposed = dict(w_in=True, w_out=False, w_up=True, w_down=False)
    finish("w_up_and_w_down", red_send, red_recv, ffn_grads, red_lands, dx, transposed)
    finish("w_out", *wout_handle[:4], dx, transposed)
    finish("w_in", *win_handle[:4], delta["w_up"], transposed)
    small_names = SMALL + SHARDED_SMALL
    small_count = sum(weights[nm].size for nm in small_names)
    small_rows = _rows_for(small_count)
    packed = [_pack([src[nm] for nm in small_names], small_rows) for src in (weights, grads, mom1, mom2)]
    outs = _adamw(*packed, "adamw_small")
    small_shapes = [weights[nm].shape for nm in small_names]
    for store, arr in zip((delta, new_m, new_v), outs):
        store.update(zip(small_names, _unpack(arr, small_shapes)))

    return (total_loss, dx[None], *[grads[nm] for nm in WEIGHTS], *[delta[nm] for nm in WEIGHTS],
            *[new_m[nm] for nm in WEIGHTS], *[new_v[nm] for nm in WEIGHTS])
```

```python
import jax
import jax.numpy as jnp
from jax import lax
from jax.experimental import pallas as pl
from jax.experimental.pallas import tpu as pltpu

F32 = jnp.float32
BF16 = jnp.bfloat16
MESH = pl.DeviceIdType.MESH
AXES = ("x", "y", "c")
N_DEV = 8

EPS = 1e-6
NEG_INF = -1e30
D_MODEL = 1024
CONV_WIDTH = 512
ATTN_WIDTH = 512
N_HEADS = 8
HEAD_DIM = 64
CHUNK = 64
LEFT = 8 * CHUNK
QBLK = 2 * CHUNK
WIN = LEFT + QBLK
CONV_K = 31
CONV_HALO = 32
FFN_K = 3
FFN_HALO = 8
D_FF = 2816
MAX_REL = 128
SCALE = HEAD_DIM ** -0.5
ADAM_LR, ADAM_B1, ADAM_B2, ADAM_EPS, ADAM_WD, ADAM_STEP = 0.001, 0.9, 0.999, 1e-08, 0.01, 10

V7X_VMEM_BYTES = 64 * 2**20
VMEM_LIMIT_BYTES = V7X_VMEM_BYTES - 8 * 2**20
LANES = 128


def _params(*sem):
    return pltpu.CompilerParams(dimension_semantics=sem or None, vmem_limit_bytes=VMEM_LIMIT_BYTES)


_DOT_DIMS = {"nn": (((1,), (0,)), ((), ())), "nt": (((1,), (1,)), ((), ())), "tn": (((0,), (0,)), ((), ()))}


def _matmul(a, b, *, mode, m, n, k, tm, tn, tk, out_dtype, name, a_m0=0, b_n0=0, b_k0=0, add=None,
            out_rows=None, out_m0=0, into=None):
    tm, tn, tk = min(tm, m), min(tn, n), min(tk, k)
    out_rows = m if out_rows is None else out_rows
    assert m % tm == 0 and n % tn == 0 and k % tk == 0, (name, m, n, k, tm, tn, tk)
    assert a_m0 % tm == 0 and b_n0 % tn == 0 and b_k0 % tk == 0 and out_m0 % tm == 0, name
    am, bn, bk, om = a_m0 // tm, b_n0 // tn, b_k0 // tk, out_m0 // tm
    gk = k // tk
    dims = _DOT_DIMS[mode]

    if mode == "tn":
        a_spec = pl.BlockSpec((tk, tm), lambda i, j, kk: (kk, i + am))
    else:
        a_spec = pl.BlockSpec((tm, tk), lambda i, j, kk: (i + am, kk))
    if mode == "nt":
        b_spec = pl.BlockSpec((tn, tk), lambda i, j, kk: (j + bn, kk + bk))
    else:
        b_spec = pl.BlockSpec((tk, tn), lambda i, j, kk: (kk + bk, j + bn))
    o_spec = pl.BlockSpec((tm, tn), lambda i, j, kk: (i + om, j))
    in_specs = [a_spec, b_spec]
    operands = [a, b]
    if add is not None:
        assert out_rows == m
        in_specs.append(o_spec)
        operands.append(add)
    aliases = {}
    if into is not None:
        aliases = {len(operands): 0}
        in_specs.append(pl.BlockSpec(memory_space=pl.ANY))
        operands.append(into)

    def body(*refs):
        a_ref, b_ref = refs[0], refs[1]
        add_ref = refs[2] if add is not None else None
        o_ref = refs[len(operands)]
        part = lax.dot_general(a_ref[...].astype(BF16), b_ref[...].astype(BF16), dims,
                               preferred_element_type=F32)

        def finish(total):
            if add_ref is not None:
                total = total + add_ref[...]
            o_ref[...] = total.astype(out_dtype)

        if gk == 1:
            finish(part)
        else:
            acc_ref = refs[-1]
            kk = pl.program_id(2)

            @pl.when(kk == 0)
            def _():
                acc_ref[...] = part

            @pl.when(kk > 0)
            def _():
                acc_ref[...] += part

            @pl.when(kk == gk - 1)
            def _():
                finish(acc_ref[...])

    return pl.pallas_call(
        body, name=name,
        grid=(m // tm, n // tn, gk),
        in_specs=in_specs, out_specs=o_spec,
        out_shape=jax.ShapeDtypeStruct((out_rows, n), out_dtype),
        scratch_shapes=[pltpu.VMEM((tm, tn), F32)] if gk > 1 else [],
        input_output_aliases=aliases,
        compiler_params=_params("parallel", "parallel", "arbitrary"),
    )(*operands)


def _rms_hat(v):
    r = lax.rsqrt(jnp.mean(v * v, axis=-1, keepdims=True) + EPS)
    return v * r, r


def _rms_bwd(dn, hat, r):
    return r * (dn - hat * jnp.mean(dn * hat, axis=-1, keepdims=True))


def _sigmoid(v):
    return 1.0 / (1.0 + jnp.exp(-v))


_GELU_C = 0.7978845608028654


def _gelu(v):
    return 0.5 * v * (1.0 + jnp.tanh(_GELU_C * (v + 0.044715 * v * (v * v))))


def _gelu_parts(v):
    v2 = v * v
    t = jnp.tanh(_GELU_C * (v + 0.044715 * v * v2))
    cdf = 0.5 * (1.0 + t)
    dcdf = 0.5 * (1.0 - t * t) * _GELU_C * (1.0 + 3.0 * 0.044715 * v2)
    return v * cdf, cdf + v * dcdf


def _row_tile(t_rows, want):
    tile = min(want, t_rows)
    assert t_rows % tile == 0
    return tile


def _pre_norm(x, g, name):
    t_rows, d = x.shape
    tm = _row_tile(t_rows, 512)

    def body(x_ref, g_ref, u_ref):
        hat, _ = _rms_hat(x_ref[...])
        u_ref[...] = (hat * g_ref[...]).astype(BF16)

    return pl.pallas_call(
        body, name=name, grid=(t_rows // tm,),
        in_specs=[pl.BlockSpec((tm, d), lambda i: (i, 0)), pl.BlockSpec((1, d), lambda i: (0, 0))],
        out_specs=pl.BlockSpec((tm, d), lambda i: (i, 0)),
        out_shape=jax.ShapeDtypeStruct((t_rows, d), BF16),
        compiler_params=_params("parallel"),
    )(x, g)


def _mid_forward(x, mixed, g_post, g_pre):
    t_rows, d = x.shape
    tm = _row_tile(t_rows, 512)

    def body(x_ref, mixed_ref, gpost_ref, gpre_ref, h1_ref, u2_ref):
        hat, _ = _rms_hat(mixed_ref[...])
        h1 = x_ref[...] + hat * gpost_ref[...]
        h1_ref[...] = h1
        hat1, _ = _rms_hat(h1)
        u2_ref[...] = (hat1 * gpre_ref[...]).astype(BF16)

    row = pl.BlockSpec((tm, d), lambda i: (i, 0))
    vec = pl.BlockSpec((1, d), lambda i: (0, 0))
    return pl.pallas_call(
        body, name="mid_forward", grid=(t_rows // tm,),
        in_specs=[row, row, vec, vec], out_specs=[row, row],
        out_shape=[jax.ShapeDtypeStruct((t_rows, d), F32), jax.ShapeDtypeStruct((t_rows, d), BF16)],
        compiler_params=_params("parallel"),
    )(x, mixed, g_post, g_pre)


def _loss_and_head_backward(h1, f, target, g_post):
    t_rows, d = h1.shape
    tm = _row_tile(t_rows, 512)
    nt = t_rows // tm

    def body(h1_ref, f_ref, tgt_ref, g_ref, loss_ref, dy_ref, df_ref, dg_ref, sq_ref):
        i = pl.program_id(0)

        @pl.when(i == 0)
        def _():
            sq_ref[...] = jnp.zeros_like(sq_ref)
            dg_ref[...] = jnp.zeros_like(dg_ref)

        g = g_ref[...]
        hat, r = _rms_hat(f_ref[...])
        err = h1_ref[...] + hat * g - tgt_ref[...]
        sq_ref[...] += jnp.sum(err * err, axis=0, keepdims=True)
        dy = err * (1.0 / d)
        dy_ref[...] = dy
        dg_ref[...] += jnp.sum(dy * hat, axis=0, keepdims=True)
        df_ref[...] = _rms_bwd(dy * g, hat, r).astype(BF16)

        @pl.when(i == nt - 1)
        def _():
            loss_ref[...] = (0.5 / d) * jnp.sum(sq_ref[...], axis=1, keepdims=True)

    row = pl.BlockSpec((tm, d), lambda i: (i, 0))
    vec = pl.BlockSpec((1, d), lambda i: (0, 0))
    return pl.pallas_call(
        body, name="loss_head_backward", grid=(nt,),
        in_specs=[row, row, row, vec],
        out_specs=[pl.BlockSpec((1, 1), lambda i: (0, 0)), row, row, vec],
        out_shape=[jax.ShapeDtypeStruct((1, 1), F32), jax.ShapeDtypeStruct((t_rows, d), F32),
                   jax.ShapeDtypeStruct((t_rows, d), BF16), jax.ShapeDtypeStruct((1, d), F32)],
        scratch_shapes=[pltpu.VMEM((1, d), F32)],
        compiler_params=_params("arbitrary"),
    )(h1, f, target, g_post)


def _mid_backward(dy, du2, h1, mixed, g_pre, g_post):
    t_rows, d = dy.shape
    tm = _row_tile(t_rows, 512)

    def body(dy_ref, du2_ref, h1_ref, mixed_ref, gpre_ref, gpost_ref, dh1_ref, dmixed_ref, dgpre_ref, dgpost_ref):
        @pl.when(pl.program_id(0) == 0)
        def _():
            dgpre_ref[...] = jnp.zeros_like(dgpre_ref)
            dgpost_ref[...] = jnp.zeros_like(dgpost_ref)

        du2 = du2_ref[...]
        hat1, r1 = _rms_hat(h1_ref[...])
        dgpre_ref[...] += jnp.sum(du2 * hat1, axis=0, keepdims=True)
        dh1 = dy_ref[...] + _rms_bwd(du2 * gpre_ref[...], hat1, r1)
        dh1_ref[...] = dh1
        hatm, rm = _rms_hat(mixed_ref[...])
        dgpost_ref[...] += jnp.sum(dh1 * hatm, axis=0, keepdims=True)
        dmixed_ref[...] = _rms_bwd(dh1 * gpost_ref[...], hatm, rm).astype(BF16)

    row = pl.BlockSpec((tm, d), lambda i: (i, 0))
    vec = pl.BlockSpec((1, d), lambda i: (0, 0))
    return pl.pallas_call(
        body, name="mid_backward", grid=(t_rows // tm,),
        in_specs=[row, row, row, row, vec, vec], out_specs=[row, row, vec, vec],
        out_shape=[jax.ShapeDtypeStruct((t_rows, d), F32), jax.ShapeDtypeStruct((t_rows, d), BF16),
                   jax.ShapeDtypeStruct((1, d), F32), jax.ShapeDtypeStruct((1, d), F32)],
        compiler_params=_params("arbitrary"),
    )(dy, du2, h1, mixed, g_pre, g_post)


def _input_backward(dh1, du1, x, g_pre):
    t_rows, d = x.shape
    tm = _row_tile(t_rows, 512)

    def body(dh1_ref, du1_ref, x_ref, g_ref, dx_ref, dg_ref):
        @pl.when(pl.program_id(0) == 0)
        def _():
            dg_ref[...] = jnp.zeros_like(dg_ref)

        du1 = du1_ref[...]
        hat, r = _rms_hat(x_ref[...])
        dg_ref[...] += jnp.sum(du1 * hat, axis=0, keepdims=True)
        dx_ref[...] = dh1_ref[...] + _rms_bwd(du1 * g_ref[...], hat, r)

    row = pl.BlockSpec((tm, d), lambda i: (i, 0))
    vec = pl.BlockSpec((1, d), lambda i: (0, 0))
    return pl.pallas_call(
        body, name="input_backward", grid=(t_rows // tm,),
        in_specs=[row, row, row, vec], out_specs=[row, vec],
        out_shape=[jax.ShapeDtypeStruct((t_rows, d), F32), jax.ShapeDtypeStruct((1, d), F32)],
        compiler_params=_params("arbitrary"),
    )(dh1, du1, x, g_pre)


CONV_STRIP = 32


def _glu(block):
    return block[:, :CONV_WIDTH] * _sigmoid(block[:, CONV_WIDTH:])


def _layer_norm_parts(c):
    mu = jnp.mean(c, axis=-1, keepdims=True)
    xc = c - mu
    r = lax.rsqrt(jnp.mean(xc * xc, axis=-1, keepdims=True) + EPS)
    return xc * r, r


CONV_WINDOW = 2 * CONV_STRIP
SHIFTED_ROWS = CONV_WINDOW - 8


def _shifted_copies(v, shifted):
    for s in range(1, 8):
        shifted[s] = v[s:s + SHIFTED_ROWS, :]


def _window_rows(v, shifted, start):
    s, a = start % 8, start - start % 8
    return v[a:a + CONV_STRIP, :] if s == 0 else shifted[s, a:a + CONV_STRIP, :]


def _conv_forward(proj_a, w, b, ln_g, ln_b):
    t_rows = proj_a.shape[0]
    tm = _row_tile(t_rows, 512)
    hb = tm // CONV_HALO
    cw = CONV_WIDTH

    def body(cur_ref, prev_ref, w_ref, b_ref, g_ref, beta_ref, c_ref, out_ref, hbuf, shifted):
        i = pl.program_id(0)
        hbuf[0:CONV_HALO, :] = jnp.where(i > 0, _glu(prev_ref[...]), 0.0)
        hbuf[CONV_HALO:, :] = _glu(cur_ref[...])

        def strip(s, carry):
            base = pl.multiple_of(s * CONV_STRIP, CONV_STRIP)
            v = hbuf[pl.ds(base, CONV_WINDOW), :]
            _shifted_copies(v, shifted)
            acc = jnp.broadcast_to(b_ref[...], (CONV_STRIP, cw))
            off = CONV_HALO - (CONV_K - 1)
            for kk in range(CONV_K):
                acc = acc + w_ref[kk:kk + 1, :] * _window_rows(v, shifted, off + kk)
            c_ref[pl.ds(base, CONV_STRIP), :] = acc
            hat, _ = _layer_norm_parts(acc)
            z = hat * g_ref[...] + beta_ref[...]
            out_ref[pl.ds(base, CONV_STRIP), :] = (z * _sigmoid(z)).astype(BF16)
            return carry

        lax.fori_loop(0, tm // CONV_STRIP, strip, 0)

    vec = pl.BlockSpec((1, cw), lambda i: (0, 0))
    return pl.pallas_call(
        body, name="conv_forward", grid=(t_rows // tm,),
        in_specs=[pl.BlockSpec((tm, 2 * cw), lambda i: (i, 0)),
                  pl.BlockSpec((CONV_HALO, 2 * cw), lambda i: (jnp.maximum(i * hb - 1, 0), 0)),
                  pl.BlockSpec((CONV_K, cw), lambda i: (0, 0)), vec, vec, vec],
        out_specs=[pl.BlockSpec((tm, cw), lambda i: (i, 0)), pl.BlockSpec((tm, cw), lambda i: (i, 0))],
        out_shape=[jax.ShapeDtypeStruct((t_rows, cw), F32), jax.ShapeDtypeStruct((t_rows, cw), BF16)],
        scratch_shapes=[pltpu.VMEM((tm + CONV_HALO, cw), F32), pltpu.VMEM((8, SHIFTED_ROWS, cw), F32)],
        compiler_params=_params("parallel"),
    )(proj_a, proj_a, w, b, ln_g, ln_b)


def _conv_backward(dout, c, proj_a, w, ln_g, ln_b):
    t_rows = c.shape[0]
    tm = _row_tile(t_rows, 512)
    hb = tm // CONV_HALO
    nt = t_rows // tm
    last_halo = t_rows // CONV_HALO - 1
    cw = CONV_WIDTH

    def body(dout_ref, dout_next_ref, c_ref, c_next_ref, cur_ref, prev_ref, w_ref, g_ref, beta_ref,
             dproj_ref, dw_ref, db_ref, dg_ref, dbeta_ref, hbuf, dcbuf, dwacc, h_shifted, d_shifted):
        i = pl.program_id(0)

        @pl.when(i == 0)
        def _():
            dwacc[...] = jnp.zeros_like(dwacc)
            db_ref[...] = jnp.zeros_like(db_ref)
            dg_ref[...] = jnp.zeros_like(dg_ref)
            dbeta_ref[...] = jnp.zeros_like(dbeta_ref)

        def ln_swish_backward(dout_v, c_v):
            hat, r = _layer_norm_parts(c_v)
            g = g_ref[...]
            z = hat * g + beta_ref[...]
            sg = _sigmoid(z)
            dz = dout_v * (sg * (1.0 + z * (1.0 - sg)))
            dhat = dz * g
            dc = r * (dhat - jnp.mean(dhat, axis=-1, keepdims=True)
                      - hat * jnp.mean(dhat * hat, axis=-1, keepdims=True))
            return dc, dz, hat

        dc, dz, hat = ln_swish_backward(dout_ref[...], c_ref[...])
        dg_ref[...] += jnp.sum(dz * hat, axis=0, keepdims=True)
        dbeta_ref[...] += jnp.sum(dz, axis=0, keepdims=True)
        db_ref[...] += jnp.sum(dc, axis=0, keepdims=True)
        dcbuf[0:tm, :] = dc
        dc_next, _, _ = ln_swish_backward(dout_next_ref[...], c_next_ref[...])
        dcbuf[tm:, :] = jnp.where(i < nt - 1, dc_next, 0.0)

        hbuf[0:CONV_HALO, :] = jnp.where(i > 0, _glu(prev_ref[...]), 0.0)
        hbuf[CONV_HALO:, :] = _glu(cur_ref[...])

        def strip(s, carry):
            base = pl.multiple_of(s * CONV_STRIP, CONV_STRIP)
            dv = dcbuf[pl.ds(base, CONV_WINDOW), :]
            hv = hbuf[pl.ds(base, CONV_WINDOW), :]
            _shifted_copies(dv, d_shifted)
            _shifted_copies(hv, h_shifted)
            dcs = dv[0:CONV_STRIP, :]
            dh = jnp.zeros((CONV_STRIP, cw), F32)
            off = CONV_HALO - (CONV_K - 1)
            for kk in range(CONV_K):
                back = CONV_K - 1 - kk
                dh = dh + w_ref[kk:kk + 1, :] * _window_rows(dv, d_shifted, back)
                prod = dcs * _window_rows(hv, h_shifted, off + kk)
                dwacc[kk] += jnp.sum(prod.reshape(CONV_STRIP // 8, 8, cw), axis=0)
            blk = cur_ref[pl.ds(base, CONV_STRIP), :]
            val, sg = blk[:, :cw], _sigmoid(blk[:, cw:])
            dproj_ref[pl.ds(base, CONV_STRIP), 0:cw] = (dh * sg).astype(BF16)
            dproj_ref[pl.ds(base, CONV_STRIP), cw:2 * cw] = (dh * val * sg * (1.0 - sg)).astype(BF16)
            return carry

        lax.fori_loop(0, tm // CONV_STRIP, strip, 0)

        @pl.when(i == nt - 1)
        def _():
            for kk in range(CONV_K):
                dw_ref[kk:kk + 1, :] = jnp.sum(dwacc[kk], axis=0, keepdims=True)

    vec = pl.BlockSpec((1, cw), lambda i: (0, 0))
    cur = lambda width: pl.BlockSpec((tm, width), lambda i: (i, 0))
    nxt = lambda width: pl.BlockSpec((CONV_HALO, width), lambda i: (jnp.minimum((i + 1) * hb, last_halo), 0))
    return pl.pallas_call(
        body, name="conv_backward", grid=(nt,),
        in_specs=[cur(cw), nxt(cw), cur(cw), nxt(cw), cur(2 * cw),
                  pl.BlockSpec((CONV_HALO, 2 * cw), lambda i: (jnp.maximum(i * hb - 1, 0), 0)),
                  pl.BlockSpec((CONV_K, cw), lambda i: (0, 0)), vec, vec],
        out_specs=[cur(2 * cw), pl.BlockSpec((CONV_K, cw), lambda i: (0, 0)), vec, vec, vec],
        out_shape=[jax.ShapeDtypeStruct((t_rows, 2 * cw), BF16), jax.ShapeDtypeStruct((CONV_K, cw), F32),
                   jax.ShapeDtypeStruct((1, cw), F32), jax.ShapeDtypeStruct((1, cw), F32),
                   jax.ShapeDtypeStruct((1, cw), F32)],
        scratch_shapes=[pltpu.VMEM((tm + CONV_HALO, cw), F32), pltpu.VMEM((tm + CONV_HALO, cw), F32),
                        pltpu.VMEM((CONV_K, 8, cw), F32), pltpu.VMEM((8, SHIFTED_ROWS, cw), F32),
                        pltpu.VMEM((8, SHIFTED_ROWS, cw), F32)],
        compiler_params=_params("arbitrary"),
    )(dout, dout, c, c, proj_a, proj_a, w, ln_g, ln_b)


def _attn_load_kv(kv_hbm, k_pad, v_pad, sem, t_cols):
    k_pad[:, 0:LEFT] = jnp.zeros((ATTN_WIDTH, LEFT), BF16)
    v_pad[:, 0:LEFT] = jnp.zeros((ATTN_WIDTH, LEFT), BF16)
    ck = pltpu.make_async_copy(kv_hbm.at[pl.ds(ATTN_WIDTH, ATTN_WIDTH), :], k_pad.at[:, pl.ds(LEFT, t_cols)], sem.at[0])
    cv = pltpu.make_async_copy(kv_hbm.at[pl.ds(2 * ATTN_WIDTH, ATTN_WIDTH), :], v_pad.at[:, pl.ds(LEFT, t_cols)], sem.at[1])
    ck.start()
    cv.start()
    ck.wait()
    cv.wait()


def _attn_build_bias(tab_ref, bias_t):
    row = lax.broadcasted_iota(jnp.int32, (LANES, LANES), 0)
    lane = lax.broadcasted_iota(jnp.int32, (LANES, LANES), 1)
    upper = lane >= row
    lane64 = lax.broadcasted_iota(jnp.int32, (CHUNK, LANES), 1)
    for h in range(N_HEADS):
        far = jnp.broadcast_to(tab_ref[h:h + 1, 2 * MAX_REL:2 * MAX_REL + 1], (LANES, LANES))
        hi = jnp.broadcast_to(tab_ref[h:h + 1, MAX_REL:2 * MAX_REL], (LANES, LANES))
        lo = jnp.broadcast_to(tab_ref[h:h + 1, 0:MAX_REL], (LANES, LANES))
        hi_d = pltpu.roll(hi, 0, 1, stride=1, stride_axis=0)
        lo_d = pltpu.roll(lo, 0, 1, stride=1, stride_axis=0)
        bias_t[h, 0:WIN - 2 * LANES, :] = jnp.broadcast_to(far[0:1, :], (WIN - 2 * LANES, LANES))
        bias_t[h, WIN - 2 * LANES:WIN - LANES, :] = jnp.where(upper, far, hi_d)
        bias_t[h, WIN - LANES:WIN, :] = jnp.where(upper, hi_d, lo_d)
        bias_t[h, 0:CHUNK, :] = jnp.where(lane64 < CHUNK, bias_t[h, 0:CHUNK, :], NEG_INF)
        bias_t[h, WIN - CHUNK:WIN, :] = jnp.where(lane64 >= CHUNK, bias_t[h, WIN - CHUNK:WIN, :], NEG_INF)


def _head_rows(h):
    return slice(h * HEAD_DIM, (h + 1) * HEAD_DIM)


def _attn_scores(k_pad, q_ref, s_buf, w0):
    for h in range(N_HEADS):
        q_h = q_ref[_head_rows(h), :] * jnp.asarray(SCALE, BF16)
        s_buf[h] = lax.dot_general(k_pad[_head_rows(h), pl.ds(w0, WIN)], q_h, _DOT_DIMS["tn"],
                                   preferred_element_type=F32)


def _attn_logits(s, bias, first_valid, key0=0):
    s = s + bias
    if first_valid is not None:
        s = jnp.where(lax.broadcasted_iota(jnp.int32, s.shape, 0) + key0 >= first_valid, s, NEG_INF)
    return s


def _attn_probs(s, bias_h, first_valid):
    s = _attn_logits(s, bias_h, first_valid)
    top = jnp.max(s, axis=0, keepdims=True)
    e = jnp.exp(s - top)
    total = jnp.sum(e, axis=0, keepdims=True)
    return e * (1.0 / total), top + jnp.log(total)


def _attn_by_padding(m, fn):
    @pl.when(m < LEFT // QBLK)
    def _():
        fn(LEFT - m * QBLK)

    @pl.when(m >= LEFT // QBLK)
    def _():
        fn(None)


def _attn_forward(qkv_t, rel_bias):
    t_cols = qkv_t.shape[1]
    steps = t_cols // QBLK

    def body(q_ref, kv_hbm, tab_ref, o_ref, lse_ref, k_pad, v_pad, bias_t, s_buf, p_buf, sem):
        m = pl.program_id(0)

        @pl.when(m == 0)
        def _():
            _attn_build_bias(tab_ref, bias_t)
            _attn_load_kv(kv_hbm, k_pad, v_pad, sem, t_cols)

        w0 = pl.multiple_of(m * QBLK, QBLK)
        _attn_scores(k_pad, q_ref, s_buf, w0)

        def softmax(first_valid):
            for h in range(N_HEADS):
                p, lse = _attn_probs(s_buf[h], bias_t[h], first_valid)
                p_buf[h] = p.astype(BF16)
                lse_ref[h:h + 1, :] = lse

        _attn_by_padding(m, softmax)
        for h in range(N_HEADS):
            o_h = lax.dot_general(v_pad[_head_rows(h), pl.ds(w0, WIN)], p_buf[h], _DOT_DIMS["nn"],
                                  preferred_element_type=F32)
            o_ref[_head_rows(h), :] = o_h.astype(BF16)

    return pl.pallas_call(
        body, name="attn_forward", grid=(steps,),
        in_specs=[pl.BlockSpec((ATTN_WIDTH, QBLK), lambda m: (0, m)),
                  pl.BlockSpec(memory_space=pl.ANY),
                  pl.BlockSpec((N_HEADS, 2 * MAX_REL + 1), lambda m: (0, 0))],
        out_specs=[pl.BlockSpec((ATTN_WIDTH, QBLK), lambda m: (0, m)), pl.BlockSpec((N_HEADS, QBLK), lambda m: (0, m))],
        out_shape=[jax.ShapeDtypeStruct((ATTN_WIDTH, t_cols), BF16), jax.ShapeDtypeStruct((N_HEADS, t_cols), F32)],
        scratch_shapes=[pltpu.VMEM((ATTN_WIDTH, LEFT + t_cols), BF16), pltpu.VMEM((ATTN_WIDTH, LEFT + t_cols), BF16),
                        pltpu.VMEM((N_HEADS, WIN, QBLK), F32), pltpu.VMEM((N_HEADS, WIN, QBLK), F32),
                        pltpu.VMEM((N_HEADS, WIN, QBLK), BF16), pltpu.SemaphoreType.DMA((2,))],
        compiler_params=_params("arbitrary"),
    )(qkv_t, qkv_t, rel_bias)


def _reverse_lanes(v, flip):
    out = jnp.zeros(v.shape, F32)
    rest = v
    for _ in range(3):
        piece = rest.astype(BF16)
        out = out + lax.dot_general(piece, flip, _DOT_DIMS["nn"], preferred_element_type=F32)
        rest = rest - piece.astype(F32)
    return out


def _attn_bias_grad(dbias_t, drel_ref):
    row = lax.broadcasted_iota(jnp.int32, (LANES, LANES), 0)
    lane = lax.broadcasted_iota(jnp.int32, (LANES, LANES), 1)
    flip = (row + lane == LANES - 1).astype(BF16)
    head = lax.broadcasted_iota(jnp.int32, (N_HEADS, LANES), 0)
    lane8 = lax.broadcasted_iota(jnp.int32, (N_HEADS, LANES), 1)
    upper_rev = jnp.zeros((N_HEADS, LANES), F32)
    lower_rev = jnp.zeros((N_HEADS, LANES), F32)
    far = jnp.zeros((N_HEADS, LANES), F32)
    for h in range(N_HEADS):
        def diagonals(block):
            skew = pltpu.roll(_reverse_lanes(block, flip), 0, 1, stride=1, stride_axis=0)
            pos = jnp.sum(jnp.where(lane >= row, skew, 0.0), axis=0, keepdims=True)
            neg = jnp.sum(jnp.where(lane < row, skew, 0.0), axis=0, keepdims=True)
            return pos, neg

        pos4, neg4 = diagonals(dbias_t[h, WIN - LANES:WIN, :])
        pos3, neg3 = diagonals(dbias_t[h, WIN - 2 * LANES:WIN - LANES, :])
        far_h = jnp.sum(jnp.sum(dbias_t[h, 0:WIN - 2 * LANES, :], axis=0, keepdims=True), axis=1, keepdims=True)
        far_h = far_h + jnp.sum(pos3, axis=1, keepdims=True)
        upper_rev = jnp.where(head == h, pos4 + neg3, upper_rev)
        lower_rev = jnp.where(head == h, neg4, lower_rev)
        far = jnp.where((head == h) & (lane8 == 0), far_h, far)
    drel_ref[:, 0:LANES] = _reverse_lanes(lower_rev, flip)
    drel_ref[:, LANES:2 * LANES] = _reverse_lanes(upper_rev, flip)
    drel_ref[:, 2 * LANES:3 * LANES] = far


def _attn_backward(qkv_t, o_t, do_t, lse, rel_bias):
    t_cols = qkv_t.shape[1]
    steps = t_cols // QBLK
    flush = LEFT // QBLK
    total = steps + flush

    def body(q_ref, o_ref, do_ref, lse_ref, kv_hbm, tab_ref, dq_ref, dk_ref, dv_ref, drel_ref,
             k_pad, v_pad, bias_t, dbias_t, dk_acc, dv_acc, s_buf, dp_buf, p_buf, ds_buf, sem):
        m = pl.program_id(0)

        @pl.when(m == 0)
        def _():
            _attn_build_bias(tab_ref, bias_t)
            _attn_load_kv(kv_hbm, k_pad, v_pad, sem, t_cols)
            dbias_t[...] = jnp.zeros_like(dbias_t)
            dk_acc[...] = jnp.zeros_like(dk_acc)
            dv_acc[...] = jnp.zeros_like(dv_acc)

        @pl.when(m < steps)
        def _():
            w0 = pl.multiple_of(m * QBLK, QBLK)
            _attn_scores(k_pad, q_ref, s_buf, w0)
            for h in range(N_HEADS):
                dp_buf[h] = lax.dot_general(v_pad[_head_rows(h), pl.ds(w0, WIN)], do_ref[_head_rows(h), :],
                                            _DOT_DIMS["tn"], preferred_element_type=F32)

            def softmax_backward(first_valid):
                for h in range(N_HEADS):
                    rows = _head_rows(h)
                    delta = jnp.sum(do_ref[rows, :].astype(F32) * o_ref[rows, :].astype(F32), axis=0, keepdims=True)
                    lse_h = lse_ref[h:h + 1, :]
                    for b in range(WIN // LANES):
                        keys = slice(b * LANES, (b + 1) * LANES)
                        s = _attn_logits(s_buf[h, keys, :], bias_t[h, keys, :], first_valid, b * LANES)
                        p = jnp.exp(s - lse_h)
                        ds = p * (dp_buf[h, keys, :] - delta)
                        dbias_t[h, keys, :] += ds
                        p_buf[h, keys, :] = p.astype(BF16)
                        ds_buf[h, keys, :] = (ds * SCALE).astype(BF16)

            _attn_by_padding(m, softmax_backward)
            for h in range(N_HEADS):
                rows = _head_rows(h)
                dq_h = lax.dot_general(k_pad[rows, pl.ds(w0, WIN)], ds_buf[h], _DOT_DIMS["nn"], preferred_element_type=F32)
                dq_ref[rows, :] = dq_h.astype(BF16)
                dk_acc[rows, :] += lax.dot_general(q_ref[rows, :], ds_buf[h], _DOT_DIMS["nt"], preferred_element_type=F32)
                dv_acc[rows, :] += lax.dot_general(do_ref[rows, :], p_buf[h], _DOT_DIMS["nt"], preferred_element_type=F32)

        dk_ref[...] = dk_acc[:, 0:QBLK].astype(BF16)
        dv_ref[...] = dv_acc[:, 0:QBLK].astype(BF16)
        for acc in (dk_acc, dv_acc):
            rest = acc[:, QBLK:WIN]
            acc[:, 0:LEFT] = rest
            acc[:, LEFT:WIN] = jnp.zeros((ATTN_WIDTH, QBLK), F32)

        @pl.when(m == total - 1)
        def _():
            _attn_bias_grad(dbias_t, drel_ref)

    qblk = pl.BlockSpec((ATTN_WIDTH, QBLK), lambda m: (0, jnp.minimum(m, steps - 1)))
    kblk = pl.BlockSpec((ATTN_WIDTH, QBLK), lambda m: (0, jnp.maximum(m - flush, 0)))
    dq, dk, dv, drel = pl.pallas_call(
        body, name="attn_backward", grid=(total,),
        in_specs=[qblk, qblk, qblk, pl.BlockSpec((N_HEADS, QBLK), lambda m: (0, jnp.minimum(m, steps - 1))),
                  pl.BlockSpec(memory_space=pl.ANY), pl.BlockSpec((N_HEADS, 2 * MAX_REL + 1), lambda m: (0, 0))],
        out_specs=[qblk, kblk, kblk, pl.BlockSpec((N_HEADS, 3 * LANES), lambda m: (0, 0))],
        out_shape=[jax.ShapeDtypeStruct((ATTN_WIDTH, t_cols), BF16)] * 3
        + [jax.ShapeDtypeStruct((N_HEADS, 3 * LANES), F32)],
        scratch_shapes=[pltpu.VMEM((ATTN_WIDTH, LEFT + t_cols), BF16), pltpu.VMEM((ATTN_WIDTH, LEFT + t_cols), BF16),
                        pltpu.VMEM((N_HEADS, WIN, QBLK), F32), pltpu.VMEM((N_HEADS, WIN, QBLK), F32),
                        pltpu.VMEM((ATTN_WIDTH, WIN), F32), pltpu.VMEM((ATTN_WIDTH, WIN), F32),
                        pltpu.VMEM((N_HEADS, WIN, QBLK), F32), pltpu.VMEM((N_HEADS, WIN, QBLK), F32),
                        pltpu.VMEM((N_HEADS, WIN, QBLK), BF16), pltpu.VMEM((N_HEADS, WIN, QBLK), BF16),
                        pltpu.SemaphoreType.DMA((2,))],
        compiler_params=_params("arbitrary"),
    )(qkv_t, o_t, do_t, lse, qkv_t, rel_bias)
    return dq, dk, dv, drel


FFN_TC = 256


def _ffn_specs(t_rows, tr):
    nj = D_FF // FFN_TC
    hb = tr // FFN_HALO
    last_halo = t_rows // FFN_HALO - 1
    cur = lambda off: pl.BlockSpec((tr, FFN_TC), lambda j, i: (i, j + off))
    prev = lambda off: pl.BlockSpec((FFN_HALO, FFN_TC), lambda j, i: (jnp.maximum(i * hb - 1, 0), j + off))
    nxt = lambda off: pl.BlockSpec((FFN_HALO, FFN_TC), lambda j, i: (jnp.minimum((i + 1) * hb, last_halo), j + off))
    wspec = lambda off: pl.BlockSpec((FFN_K, FFN_TC), lambda j, i: (0, j + off))
    bspec = lambda off: pl.BlockSpec((1, FFN_TC), lambda j, i: (0, j + off))
    return nj, cur, prev, nxt, wspec, bspec


FFN_STRIP = 16


def _ffn_conv(win, w, b, rows):
    out = b + w[2] * win[FFN_HALO:FFN_HALO + rows, :]
    out = out + w[1] * win[FFN_HALO - 1:FFN_HALO - 1 + rows, :]
    return out + w[0] * win[FFN_HALO - 2:FFN_HALO - 2 + rows, :]


def _taps(w_ref):
    return [w_ref[kk:kk + 1, :] for kk in range(FFN_K)]


def _ffn_first_window(prev_ref, cur_ref, tile, rows):
    return jnp.concatenate([jnp.where(tile > 0, prev_ref[...], 0.0), cur_ref[0:rows, :]], axis=0)


def _fold8(v):
    return jnp.sum(v.reshape(v.shape[0] // 8, 8, v.shape[1]), axis=0)


def _ffn_activation(hup, w, b):
    t_rows = hup.shape[0]
    tr = _row_tile(t_rows, 512)
    nj, cur, prev, nxt, wspec, bspec = _ffn_specs(t_rows, tr)
    rs = 2 * FFN_STRIP

    def body(g_ref, gprev_ref, v_ref, vprev_ref, wg_ref, wv_ref, bg_ref, bv_ref, act_ref, gel_ref, slope_ref):
        i = pl.program_id(1)
        wg, wv, bg, bv = _taps(wg_ref), _taps(wv_ref), bg_ref[...], bv_ref[...]

        def emit(base, g_win, v_win):
            gel, dgel = _gelu_parts(_ffn_conv(g_win, wg, bg, rs))
            cv = _ffn_conv(v_win, wv, bv, rs)
            act_ref[pl.ds(base, rs), :] = (gel * cv).astype(BF16)
            gel_ref[pl.ds(base, rs), :] = gel
            slope_ref[pl.ds(base, rs), :] = cv * dgel

        def strip(s, carry):
            base = pl.multiple_of(s * rs, rs)
            emit(base, g_ref[pl.ds(base - FFN_HALO, rs + FFN_HALO), :], v_ref[pl.ds(base - FFN_HALO, rs + FFN_HALO), :])
            return carry

        emit(0, _ffn_first_window(gprev_ref, g_ref, i, rs), _ffn_first_window(vprev_ref, v_ref, i, rs))
        lax.fori_loop(1, tr // rs, strip, 0)

    return pl.pallas_call(
        body, name="ffn_activation", grid=(nj, t_rows // tr),
        in_specs=[cur(0), prev(0), cur(nj), prev(nj), wspec(0), wspec(nj), bspec(0), bspec(nj)],
        out_specs=[cur(0), cur(0), cur(0)],
        out_shape=[jax.ShapeDtypeStruct((t_rows, D_FF), BF16), jax.ShapeDtypeStruct((t_rows, D_FF), F32),
                   jax.ShapeDtypeStruct((t_rows, D_FF), F32)],
        compiler_params=_params("parallel", "parallel"),
    )(hup, hup, hup, hup, w, w, b, b)


def _ffn_backward(dact, gel, slope, hup, w):
    t_rows = hup.shape[0]
    tr = _row_tile(t_rows, 512)
    nj, cur, prev, nxt, wspec, bspec = _ffn_specs(t_rows, tr)
    ni = t_rows // tr
    rs = FFN_STRIP
    ns = tr // rs

    def body(da_ref, danext_ref, gel_ref, gelnext_ref, slope_ref, slopenext_ref, g_ref, gprev_ref, v_ref, vprev_ref,
             wg_ref, wv_ref, dhg_ref, dhv_ref, dwg_ref, dwv_ref, dbg_ref, dbv_ref):
        i = pl.program_id(1)

        @pl.when(i == 0)
        def _():
            for ref in (dwg_ref, dwv_ref, dbg_ref, dbv_ref):
                ref[...] = jnp.zeros_like(ref)

        wg, wv = _taps(wg_ref), _taps(wv_ref)
        da_after = jnp.where(i < ni - 1, danext_ref[...], 0.0)
        dcg_after, dcv_after = da_after * slopenext_ref[...], da_after * gelnext_ref[...]
        zero8 = jnp.zeros((8, FFN_TC), F32)

        def strip_at(base, g_win, v_win, carry):
            dcg_after, dcv_after, dbg, dbv, dwg, dwv = carry
            da = da_ref[pl.ds(base, rs), :]
            dcg, dcv = da * slope_ref[pl.ds(base, rs), :], da * gel_ref[pl.ds(base, rs), :]
            out = []
            for dc, after, taps, win, dh_ref, db, dw in ((dcg, dcg_after, wg, g_win, dhg_ref, dbg, dwg),
                                                         (dcv, dcv_after, wv, v_win, dhv_ref, dbv, dwv)):
                ext = jnp.concatenate([dc, after], axis=0)
                dh = taps[2] * dc + taps[1] * ext[1:1 + rs, :] + taps[0] * ext[2:2 + rs, :]
                dh_ref[pl.ds(base, rs), :] = dh.astype(BF16)
                db = db + _fold8(dc)
                dw = tuple(dw[kk] + _fold8(dc * win[FFN_HALO - 2 + kk:FFN_HALO - 2 + kk + rs, :]) for kk in range(FFN_K))
                out.append((dc[0:FFN_HALO, :], db, dw))
            return out[0][0], out[1][0], out[0][1], out[1][1], out[0][2], out[1][2]

        def strip(s, carry):
            base = pl.multiple_of((ns - 1 - s) * rs, rs)
            return strip_at(base, g_ref[pl.ds(base - FFN_HALO, rs + FFN_HALO), :],
                            v_ref[pl.ds(base - FFN_HALO, rs + FFN_HALO), :], carry)

        init = (dcg_after, dcv_after, zero8, zero8, (zero8,) * FFN_K, (zero8,) * FFN_K)
        carry = lax.fori_loop(0, ns - 1, strip, init)
        _, _, dbg, dbv, dwg, dwv = strip_at(0, _ffn_first_window(gprev_ref, g_ref, i, rs),
                                            _ffn_first_window(vprev_ref, v_ref, i, rs), carry)
        dbg_ref[...] += jnp.sum(dbg, axis=0, keepdims=True)
        dbv_ref[...] += jnp.sum(dbv, axis=0, keepdims=True)
        for kk in range(FFN_K):
            dwg_ref[kk:kk + 1, :] += jnp.sum(dwg[kk], axis=0, keepdims=True)
            dwv_ref[kk:kk + 1, :] += jnp.sum(dwv[kk], axis=0, keepdims=True)

    half = jax.ShapeDtypeStruct((t_rows, D_FF), BF16)
    return pl.pallas_call(
        body, name="ffn_backward", grid=(nj, ni),
        in_specs=[cur(0), nxt(0), cur(0), nxt(0), cur(0), nxt(0), cur(0), prev(0), cur(nj), prev(nj),
                  wspec(0), wspec(nj)],
        out_specs=[cur(0), cur(0), wspec(0), wspec(0), bspec(0), bspec(0)],
        out_shape=[half, half, jax.ShapeDtypeStruct((FFN_K, D_FF), F32), jax.ShapeDtypeStruct((FFN_K, D_FF), F32),
                   jax.ShapeDtypeStruct((1, D_FF), F32), jax.ShapeDtypeStruct((1, D_FF), F32)],
        compiler_params=_params("parallel", "arbitrary"),
    )(dact, dact, gel, gel, slope, slope, hup, hup, hup, hup, w, w)


def _mesh_position():
    return lax.axis_index("x"), lax.axis_index("y"), lax.axis_index("c")


def _hbm_specs(n):
    return [pl.BlockSpec(memory_space=pl.ANY)] * n


def _all_gather(shards, name, placed=None):
    n = len(shards)

    def body(*refs):
        ins = refs[:n]
        outs = refs[2 * n:3 * n] if placed else refs[n:2 * n]
        send_sems, recv_sems, local_sems = refs[-3:]
        x, y, c = _mesh_position()
        me, sibling = (x, y, c), (x, y, 1 - c)
        chips = [(1 - x, y), (x, 1 - y), (1 - x, 1 - y)]

        def copy(a, slot, block, to, src=None):
            dst = outs[a].at[4 * block[0] + 2 * block[1] + block[2]]
            return pltpu.make_async_remote_copy(
                src_ref=dst if src is None else src, dst_ref=dst,
                send_sem=send_sems.at[a, slot], recv_sem=recv_sems.at[a, slot],
                device_id=to, device_id_type=MESH)

        started = []
        for a in range(0 if placed else n):
            mine = pltpu.make_async_copy(ins[a], outs[a].at[4 * x + 2 * y + c], local_sems.at[a])
            mine.start()
            started.append(mine)
        first = []
        for a in range(n):
            first.append(copy(a, 0, me, sibling, src=ins[a]))
            first += [copy(a, 1 + j, me, (*chip, c), src=ins[a]) for j, chip in enumerate(chips)]
        for cp in first:
            cp.start()
        passed = []
        for j, chip in enumerate(chips):
            for a in range(n):
                copy(a, 1 + j, (*chip, c), me).wait_recv()
                fwd = copy(a, 4 + j, (*chip, c), sibling)
                fwd.start()
                passed.append(fwd)
        for a in range(n):
            copy(a, 0, sibling, me).wait_recv()
            for j, chip in enumerate(chips):
                copy(a, 4 + j, (*chip, 1 - c), me).wait_recv()
        for cp in first + passed:
            cp.wait_send()
        for mine in started:
            mine.wait()

    operands = [*shards, *placed] if placed else list(shards)
    return pl.pallas_call(
        body, name=name,
        in_specs=_hbm_specs(len(operands)), out_specs=_hbm_specs(n),
        out_shape=[jax.ShapeDtypeStruct((N_DEV,) + s.shape, s.dtype) for s in shards],
        scratch_shapes=[pltpu.SemaphoreType.DMA((n, 7)), pltpu.SemaphoreType.DMA((n, 7)),
                        pltpu.SemaphoreType.DMA((n,))],
        input_output_aliases={n + a: a for a in range(n)} if placed else {},
        compiler_params=pltpu.CompilerParams(has_side_effects=True),
    )(*operands)


def _place_own(shards, me):
    n = len(shards)

    def body(me_ref, *refs):
        for src, dst in zip(refs[:n], refs[n:]):
            dst[0] = src[...]

    return pl.pallas_call(
        body, name="place_own_shards",
        grid_spec=pltpu.PrefetchScalarGridSpec(
            num_scalar_prefetch=1, grid=(1,),
            in_specs=[pl.BlockSpec(s.shape, lambda i, me_ref: (0, 0)) for s in shards],
            out_specs=[pl.BlockSpec((1,) + s.shape, lambda i, me_ref: (me_ref[0], 0, 0)) for s in shards]),
        out_shape=[jax.ShapeDtypeStruct((N_DEV,) + s.shape, s.dtype) for s in shards],
        compiler_params=_params("arbitrary"),
    )(me, *shards)


_FLIPS = [(dx, dy, dc) for dx in (0, 1) for dy in (0, 1) for dc in (0, 1)][1:]
_HBM = pl.BlockSpec(memory_space=pltpu.HBM)
_SEM = pl.BlockSpec(memory_space=pltpu.SEMAPHORE)
_DATAFLOW = pltpu.SideEffectType.DATAFLOW_SIDE_EFFECTING


def _scatter_copies(src_refs, land_refs, send_sems, recv_sems, gather):
    x, y, c = _mesh_position()
    me = 4 * x + 2 * y + c
    copies = []
    for a, (src, land) in enumerate(zip(src_refs, land_refs)):
        for k, (dx, dy, dc) in enumerate(_FLIPS):
            px, py, pc = (x + dx) % 2, (y + dy) % 2, (c + dc) % 2
            pair = a * len(_FLIPS) + k
            copies.append(pltpu.make_async_remote_copy(
                src_ref=src if gather else src.at[4 * px + 2 * py + pc], dst_ref=land.at[me],
                send_sem=send_sems[pair], recv_sem=recv_sems[pair],
                device_id=(px, py, pc), device_id_type=MESH))
    return copies


def _scatter_start(srcs, lands, gather, name, after=None):
    n = len(srcs)
    pairs = n * len(_FLIPS)
    extra = [] if after is None else [after]

    def body(*refs):
        src_refs, land_refs = refs[:n], refs[n:2 * n]
        first = 2 * n + len(extra)
        send_sems, recv_sems = refs[first:first + pairs], refs[first + pairs:first + 2 * pairs]
        token = refs[-1]
        for cp in _scatter_copies(src_refs, land_refs, send_sems, recv_sems, gather):
            cp.start()
        token[...] = jnp.zeros_like(token)

    arrays = [*srcs, *lands]
    sem = pltpu.SemaphoreType.DMA(())
    out = pl.pallas_call(
        body, name=name,
        out_shape=(*[sem] * (2 * pairs), *[pltpu.HBM(v.shape, v.dtype) for v in arrays],
                   jax.ShapeDtypeStruct((8, LANES), F32)),
        in_specs=[*[_HBM] * (2 * n), *[pl.BlockSpec(memory_space=pl.ANY)] * len(extra)],
        out_specs=(*[_SEM] * (2 * pairs), *[_HBM] * (2 * n), pl.BlockSpec(memory_space=pltpu.VMEM)),
        input_output_aliases={i: 2 * pairs + i for i in range(2 * n)},
        compiler_params=pltpu.CompilerParams(has_side_effects=_DATAFLOW),
    )(*[pltpu.with_memory_space_constraint(v, pltpu.HBM) for v in arrays], *extra)
    sems, rest = out[:2 * pairs], out[2 * pairs:]
    return list(sems[:pairs]), list(sems[pairs:]), list(rest[:n]), list(rest[n:2 * n]), rest[-1]


def _scatter_wait(send_sems, recv_sems, srcs, lands, after, gather, name):
    n = len(srcs)
    pairs = n * len(_FLIPS)

    def body(*refs):
        src_refs, land_refs = refs[:n], refs[n:2 * n]
        send_refs, recv_refs = refs[2 * n:2 * n + pairs], refs[2 * n + pairs:2 * n + 2 * pairs]
        for cp in _scatter_copies(src_refs, land_refs, send_refs, recv_refs, gather):
            cp.wait_send()
            cp.wait_recv()

    arrays = [*srcs, *lands]
    out = pl.pallas_call(
        body, name=name,
        out_shape=tuple(pltpu.HBM(v.shape, v.dtype) for v in arrays),
        in_specs=[*[_HBM] * (2 * n), *[_SEM] * (2 * pairs), pl.BlockSpec(memory_space=pl.ANY)],
        out_specs=tuple([_HBM] * (2 * n)),
        input_output_aliases={i: i for i in range(2 * n)},
        compiler_params=pltpu.CompilerParams(has_side_effects=_DATAFLOW),
    )(*arrays, *send_sems, *recv_sems, after)
    return list(out[:n]), list(out[n:])


def _sum_received(grad, received, me, name):
    _, rows, cols = grad.shape

    def body(me_ref, g_ref, r_ref, o_ref):
        p = pl.program_id(0)
        term = jnp.where(p == me_ref[0], g_ref[0], r_ref[0])

        @pl.when(p == 0)
        def _():
            o_ref[...] = term

        @pl.when(p > 0)
        def _():
            o_ref[...] += term

    blk = (1, rows, cols)
    return pl.pallas_call(
        body, name=name,
        grid_spec=pltpu.PrefetchScalarGridSpec(
            num_scalar_prefetch=1, grid=(N_DEV,),
            in_specs=[pl.BlockSpec(blk, lambda p, me_ref: (me_ref[0], 0, 0)),
                      pl.BlockSpec(blk, lambda p, me_ref: (p, 0, 0))],
            out_specs=pl.BlockSpec((rows, cols), lambda p, me_ref: (0, 0))),
        out_shape=jax.ShapeDtypeStruct((rows, cols), F32),
        compiler_params=_params("arbitrary"),
    )(me, grad, received)


def _exchange_in_chip(grads):
    n = len(grads)

    def body(*refs):
        ins, outs = refs[:n], refs[n:2 * n]
        send_sems, recv_sems = refs[2 * n:]
        x, y, c = _mesh_position()
        copies = []
        for a in range(n):
            for q in range(4):
                copies.append(pltpu.make_async_remote_copy(
                    src_ref=ins[a].at[2 * q + (1 - c)], dst_ref=outs[a].at[q],
                    send_sem=send_sems.at[a, q], recv_sem=recv_sems.at[a, q],
                    device_id=(x, y, 1 - c), device_id_type=MESH))
        for cp in copies:
            cp.start()
        for cp in copies:
            cp.wait_recv()
        for cp in copies:
            cp.wait_send()

    return pl.pallas_call(
        body, name="exchange_in_chip",
        in_specs=_hbm_specs(n), out_specs=_hbm_specs(n),
        out_shape=[jax.ShapeDtypeStruct((4,) + g.shape[1:], g.dtype) for g in grads],
        scratch_shapes=[pltpu.SemaphoreType.DMA((n, 4)), pltpu.SemaphoreType.DMA((n, 4))],
        compiler_params=pltpu.CompilerParams(has_side_effects=True),
    )(*grads)


def _exchange_between_chips(partials):
    n = len(partials)

    def body(*refs):
        ins, outs = refs[:n], refs[n:2 * n]
        send_sems, recv_sems = refs[2 * n:]
        x, y, c = _mesh_position()
        chips = [(1 - x, y), (x, 1 - y), (1 - x, 1 - y)]
        copies = []
        for a in range(n):
            for j, (px, py) in enumerate(chips):
                copies.append(pltpu.make_async_remote_copy(
                    src_ref=ins[a].at[2 * px + py], dst_ref=outs[a].at[j],
                    send_sem=send_sems.at[a, j], recv_sem=recv_sems.at[a, j],
                    device_id=(px, py, c), device_id_type=MESH))
        for cp in copies:
            cp.start()
        for cp in copies:
            cp.wait_recv()
        for cp in copies:
            cp.wait_send()

    return pl.pallas_call(
        body, name="exchange_between_chips",
        in_specs=_hbm_specs(n), out_specs=_hbm_specs(n),
        out_shape=[jax.ShapeDtypeStruct((3,) + p.shape[1:], p.dtype) for p in partials],
        scratch_shapes=[pltpu.SemaphoreType.DMA((n, 3)), pltpu.SemaphoreType.DMA((n, 3))],
        compiler_params=pltpu.CompilerParams(has_side_effects=True),
    )(*partials)


def _add_in_chip(grad, received, core, name):
    _, rows, cols = grad.shape

    def body(core_ref, g_ref, r_ref, o_ref):
        o_ref[...] = g_ref[...] + r_ref[...]

    blk = (1, rows, cols)
    return pl.pallas_call(
        body, name=name,
        grid_spec=pltpu.PrefetchScalarGridSpec(
            num_scalar_prefetch=1, grid=(4,),
            in_specs=[pl.BlockSpec(blk, lambda q, core_ref: (2 * q + core_ref[0], 0, 0)),
                      pl.BlockSpec(blk, lambda q, core_ref: (q, 0, 0))],
            out_specs=pl.BlockSpec(blk, lambda q, core_ref: (q, 0, 0))),
        out_shape=jax.ShapeDtypeStruct((4, rows, cols), F32),
        compiler_params=_params("parallel"),
    )(core, grad, received)


def _add_between_chips(partial, received, chip, name):
    _, rows, cols = partial.shape

    def body(chip_ref, p_ref, r_ref, o_ref):
        o_ref[...] = ((p_ref[0] + r_ref[0]) + r_ref[1]) + r_ref[2]

    return pl.pallas_call(
        body, name=name,
        grid_spec=pltpu.PrefetchScalarGridSpec(
            num_scalar_prefetch=1, grid=(1,),
            in_specs=[pl.BlockSpec((1, rows, cols), lambda i, chip_ref: (chip_ref[0], 0, 0)),
                      pl.BlockSpec((3, rows, cols), lambda i, chip_ref: (0, 0, 0))],
            out_specs=pl.BlockSpec((rows, cols), lambda i, chip_ref: (0, 0))),
        out_shape=jax.ShapeDtypeStruct((rows, cols), F32),
        compiler_params=_params("arbitrary"),
    )(chip, partial, received)


def _sum_devices(gathered):
    _, rows, cols = gathered.shape

    def body(g_ref, o_ref):
        total = g_ref[0]
        for d in range(1, N_DEV):
            total = total + g_ref[d]
        o_ref[...] = total

    return pl.pallas_call(
        body, name="sum_small_grads",
        out_shape=jax.ShapeDtypeStruct((rows, cols), F32),
        compiler_params=_params(),
    )(gathered)


def _adamw(w, g, m, v, name):
    rows, cols = w.shape
    tr = rows
    for cand in (256, 128, 64, 32, 16, 8):
        if rows > cand and rows % cand == 0:
            tr = cand
            break

    def body(w_ref, g_ref, m_ref, v_ref, delta_ref, newm_ref, newv_ref):
        g_v = g_ref[...]
        new_m = ADAM_B1 * m_ref[...] + (1.0 - ADAM_B1) * g_v
        new_v = ADAM_B2 * v_ref[...] + (1.0 - ADAM_B2) * (g_v * g_v)
        m_hat = new_m / (1.0 - ADAM_B1 ** ADAM_STEP)
        v_hat = new_v / (1.0 - ADAM_B2 ** ADAM_STEP)
        delta_ref[...] = -ADAM_LR * (m_hat / (jnp.sqrt(v_hat) + ADAM_EPS) + ADAM_WD * w_ref[...])
        newm_ref[...] = new_m
        newv_ref[...] = new_v

    blk = pl.BlockSpec((tr, cols), lambda i: (i, 0))
    shape = jax.ShapeDtypeStruct((rows, cols), F32)
    return pl.pallas_call(
        body, name=name, grid=(rows // tr,),
        in_specs=[blk] * 4, out_specs=[blk] * 3, out_shape=[shape] * 3,
        compiler_params=_params("parallel"),
    )(w, g, m, v)


def _pack(pieces, rows):
    flat = jnp.concatenate([p.reshape(-1) for p in pieces])
    return jnp.pad(flat, (0, rows * LANES - flat.shape[0])).reshape(rows, LANES)


def _unpack(packed, shapes):
    flat = packed.reshape(-1)
    out, pos = [], 0
    for shape in shapes:
        size = 1
        for s in shape:
            size *= s
        out.append(flat[pos:pos + size].reshape(shape))
        pos += size
    return out


def _rows_for(count):
    return -(-count // (8 * LANES)) * 8


SMALL = ("norm_mix_pre", "conv_dw_b", "conv_ln_g", "conv_ln_b", "rel_bias", "norm_mix_post", "norm_ffn_pre",
         "ffn_dw_b", "norm_ffn_post")
SHARDED_SMALL = ("conv_dw_w", "ffn_dw_w")
LARGE = ("w_in", "w_out", "w_up", "w_down")
WEIGHTS = ("norm_mix_pre", "w_in", "conv_dw_w", "conv_dw_b", "conv_ln_g", "conv_ln_b", "rel_bias", "w_out",
           "norm_mix_post", "norm_ffn_pre", "w_up", "ffn_dw_w", "ffn_dw_b", "w_down", "norm_ffn_post")


def kernel(x, norm_mix_pre, w_in, conv_dw_w, conv_dw_b, conv_ln_g, conv_ln_b, rel_bias, w_out, norm_mix_post, norm_ffn_pre, w_up, ffn_dw_w, ffn_dw_b, w_down, norm_ffn_post, loss_target, m_norm_mix_pre, m_w_in, m_conv_dw_w, m_conv_dw_b, m_conv_ln_g, m_conv_ln_b, m_rel_bias, m_w_out, m_norm_mix_post, m_norm_ffn_pre, m_w_up, m_ffn_dw_w, m_ffn_dw_b, m_w_down, m_norm_ffn_post, v_norm_mix_pre, v_w_in, v_conv_dw_w, v_conv_dw_b, v_conv_ln_g, v_conv_ln_b, v_rel_bias, v_w_out, v_norm_mix_post, v_norm_ffn_pre, v_w_up, v_ffn_dw_w, v_ffn_dw_b, v_w_down, v_norm_ffn_post):
    weights = dict(norm_mix_pre=norm_mix_pre, w_in=w_in, conv_dw_w=conv_dw_w, conv_dw_b=conv_dw_b, conv_ln_g=conv_ln_g,
                   conv_ln_b=conv_ln_b, rel_bias=rel_bias, w_out=w_out, norm_mix_post=norm_mix_post,
                   norm_ffn_pre=norm_ffn_pre, w_up=w_up, ffn_dw_w=ffn_dw_w, ffn_dw_b=ffn_dw_b, w_down=w_down,
                   norm_ffn_post=norm_ffn_post)
    mom1 = dict(norm_mix_pre=m_norm_mix_pre, w_in=m_w_in, conv_dw_w=m_conv_dw_w, conv_dw_b=m_conv_dw_b,
                conv_ln_g=m_conv_ln_g, conv_ln_b=m_conv_ln_b, rel_bias=m_rel_bias, w_out=m_w_out,
                norm_mix_post=m_norm_mix_post, norm_ffn_pre=m_norm_ffn_pre, w_up=m_w_up, ffn_dw_w=m_ffn_dw_w,
                ffn_dw_b=m_ffn_dw_b, w_down=m_w_down, norm_ffn_post=m_norm_ffn_post)
    mom2 = dict(norm_mix_pre=v_norm_mix_pre, w_in=v_w_in, conv_dw_w=v_conv_dw_w, conv_dw_b=v_conv_dw_b,
                conv_ln_g=v_conv_ln_g, conv_ln_b=v_conv_ln_b, rel_bias=v_rel_bias, w_out=v_w_out,
                norm_mix_post=v_norm_mix_post, norm_ffn_pre=v_norm_ffn_pre, w_up=v_w_up, ffn_dw_w=v_ffn_dw_w,
                ffn_dw_b=v_ffn_dw_b, w_down=v_w_down, norm_ffn_post=v_norm_ffn_post)

    x2 = x[0]
    target = loss_target[0]
    t_rows = x2.shape[0]
    d = D_MODEL
    in_cols = 2 * CONV_WIDTH + 3 * ATTN_WIDTH
    my_x, my_y, my_c = _mesh_position()
    my_dev = 4 * my_x + 2 * my_y + my_c

    small_conv = _pack([conv_dw_w[0], ffn_dw_w[0]], 32)
    me = jnp.reshape(my_dev, (1,)).astype(jnp.int32)
    first_shards = [w_in[0].T.astype(BF16), small_conv]
    late_shards = [w_out[0].astype(BF16), w_up[0].T.astype(BF16), w_down[0].astype(BF16)]
    placed = _place_own(first_shards + late_shards, me)
    win_t, conv_g = _all_gather(first_shards, "all_gather_weights", placed=placed[:2])
    late_lands = placed[2:]
    late_send, late_recv, late_shards, late_lands, late_token = _scatter_start(
        late_shards, late_lands, True, "gather_late_weights_start", after=win_t)
    win_t = win_t.reshape(in_cols, d)
    conv_flat = conv_g.reshape(N_DEV, 32 * LANES)
    n_cw = CONV_K * (CONV_WIDTH // N_DEV)
    conv_w_full = conv_flat[:, :n_cw].reshape(N_DEV, CONV_K, CONV_WIDTH // N_DEV).transpose(1, 0, 2).reshape(CONV_K, CONV_WIDTH)
    ffn_w_full = conv_flat[:, n_cw:].reshape(N_DEV, FFN_K, 2 * D_FF // N_DEV).transpose(1, 0, 2).reshape(FFN_K, 2 * D_FF)

    u1 = _pre_norm(x2, norm_mix_pre + late_token[0:1, 0:1], "pre_norm_mix")
    proj_a = _matmul(u1, win_t, mode="nt", m=t_rows, n=2 * CONV_WIDTH, k=d, tm=2048, tn=1024, tk=d,
                     out_dtype=F32, name="proj_conv")
    qkv_t = _matmul(win_t, u1, mode="nt", m=3 * ATTN_WIDTH, n=t_rows, k=d, tm=512, tn=2048, tk=d,
                    out_dtype=BF16, name="proj_qkv", a_m0=2 * CONV_WIDTH)
    conv_c, conv_out = _conv_forward(proj_a, conv_w_full, conv_dw_b, conv_ln_g, conv_ln_b)
    o_t, attn_lse = _attn_forward(qkv_t, rel_bias[0])
    _, (wout_g, wup_t, wdown_g) = _scatter_wait(late_send, late_recv, late_shards, late_lands, o_t, True,
                                                "gather_late_weights_wait")
    wout_g = wout_g.reshape(d, d)
    wup_t = wup_t.reshape(2 * D_FF, d)
    wdown_g = wdown_g.reshape(D_FF, d)
    mixed = _matmul(conv_out, wout_g, mode="nn", m=t_rows, n=d, k=CONV_WIDTH, tm=2048, tn=1024, tk=CONV_WIDTH,
                    out_dtype=F32, name="out_proj_conv")
    mixed = _matmul(o_t, wout_g, mode="tn", m=t_rows, n=d, k=ATTN_WIDTH, tm=1024, tn=1024, tk=ATTN_WIDTH,
                    out_dtype=F32, name="out_proj_attn", b_k0=CONV_WIDTH, add=mixed)
    h1, u2 = _mid_forward(x2, mixed, norm_mix_post, norm_ffn_pre)
    hup = _matmul(u2, wup_t, mode="nt", m=t_rows, n=2 * D_FF, k=d, tm=2048, tn=1408, tk=d,
                  out_dtype=F32, name="ffn_up")
    act, ffn_gel, ffn_slope = _ffn_activation(hup, ffn_w_full, ffn_dw_b)
    f = _matmul(act, wdown_g, mode="nn", m=t_rows, n=d, k=D_FF, tm=1024, tn=1024, tk=D_FF,
                out_dtype=F32, name="ffn_down")
    loss, dy, df, d_norm_ffn_post = _loss_and_head_backward(h1, f, target, norm_ffn_post)

    dact = _matmul(df, wdown_g, mode="nt", m=t_rows, n=D_FF, k=d, tm=2048, tn=1408, tk=d,
                   out_dtype=F32, name="ffn_down_dx")
    g_wdown = _matmul(act, df, mode="tn", m=D_FF, n=d, k=t_rows, tm=1408, tn=1024, tk=2048,
                      out_dtype=F32, name="ffn_down_dw")
    dhg, dhv, dwg, dwv, dbg, dbv = _ffn_backward(dact, ffn_gel, ffn_slope, hup, ffn_w_full)
    du2 = _matmul(dhg, wup_t, mode="nn", m=t_rows, n=d, k=D_FF, tm=1024, tn=1024, tk=D_FF,
                  out_dtype=F32, name="ffn_up_dx_gate")
    du2 = _matmul(dhv, wup_t, mode="nn", m=t_rows, n=d, k=D_FF, tm=1024, tn=1024, tk=D_FF,
                  out_dtype=F32, name="ffn_up_dx_value", b_k0=D_FF, add=du2)
    g_wup_t = _matmul(dhg, u2, mode="tn", m=D_FF, n=d, k=t_rows, tm=1408, tn=1024, tk=2048, out_dtype=F32,
                      name="ffn_up_dw_gate", out_rows=2 * D_FF)
    g_wup_t = _matmul(dhv, u2, mode="tn", m=D_FF, n=d, k=t_rows, tm=1408, tn=1024, tk=2048, out_dtype=F32,
                      name="ffn_up_dw_value", out_rows=2 * D_FF, out_m0=D_FF, into=g_wup_t)
    ffn_grads = [g_wup_t.reshape(N_DEV, 2 * D_FF // N_DEV, d), g_wdown.reshape(N_DEV, D_FF // N_DEV, d)]
    red_send, red_recv, ffn_grads, red_lands, red_token = _scatter_start(
        ffn_grads, [lax.empty(g.shape, F32) for g in ffn_grads], False, "reduce_ffn_grads_start")
    dh1, dmixed, d_norm_ffn_pre, d_norm_mix_post = _mid_backward(
        dy, du2, h1, mixed, norm_ffn_pre + red_token[0:1, 0:1], norm_mix_post)
    dconv_out = _matmul(dmixed, wout_g, mode="nt", m=t_rows, n=CONV_WIDTH, k=d, tm=2048, tn=512, tk=d,
                        out_dtype=F32, name="out_proj_dx_conv")
    do_t = _matmul(wout_g, dmixed, mode="nt", m=ATTN_WIDTH, n=t_rows, k=d, tm=512, tn=2048, tk=d,
                   out_dtype=BF16, name="out_proj_dx_attn", a_m0=CONV_WIDTH)
    g_wout = _matmul(conv_out, dmixed, mode="tn", m=CONV_WIDTH, n=d, k=t_rows, tm=512, tn=1024, tk=2048, out_dtype=F32,
                     name="out_proj_dw_conv", out_rows=d)
    g_wout = _matmul(o_t, dmixed, mode="nn", m=ATTN_WIDTH, n=d, k=t_rows, tm=512, tn=1024, tk=2048, out_dtype=F32,
                     name="out_proj_dw_attn", out_rows=d, out_m0=CONV_WIDTH, into=g_wout)
    wout_handle = _scatter_start([g_wout.reshape(N_DEV, d // N_DEV, d)], [lax.empty((N_DEV, d // N_DEV, d), F32)],
                                 False, "reduce_w_out_grad_start")
    dproj_a, d_conv_w, d_conv_b, d_ln_g, d_ln_b = _conv_backward(
        dconv_out, conv_c, proj_a, conv_w_full, conv_ln_g + wout_handle[4][0:1, 0:1], conv_ln_b)
    dqkv_parts = _attn_backward(qkv_t, o_t, do_t, attn_lse, rel_bias[0])
    drel = dqkv_parts[3]
    du1 = _matmul(dproj_a, win_t, mode="nn", m=t_rows, n=d, k=2 * CONV_WIDTH, tm=2048, tn=1024, tk=1024,
                  out_dtype=F32, name="proj_dx_conv")
    g_win_t = _matmul(dproj_a, u1, mode="tn", m=2 * CONV_WIDTH, n=d, k=t_rows, tm=1024, tn=1024, tk=2048, out_dtype=F32,
                      name="proj_dw_conv", out_rows=in_cols)
    for j, part in enumerate("qkv"):
        row0 = 2 * CONV_WIDTH + j * ATTN_WIDTH
        du1 = _matmul(dqkv_parts[j], win_t, mode="tn", m=t_rows, n=d, k=ATTN_WIDTH, tm=1024, tn=1024, tk=ATTN_WIDTH,
                      out_dtype=F32, name="proj_dx_" + part, b_k0=row0, add=du1)
        g_win_t = _matmul(dqkv_parts[j], u1, mode="nn", m=ATTN_WIDTH, n=d, k=t_rows, tm=512, tn=1024, tk=2048,
                          out_dtype=F32, name="proj_dw_" + part, out_rows=in_cols, out_m0=row0, into=g_win_t)
    win_handle = _scatter_start([g_win_t.reshape(N_DEV, in_cols // N_DEV, d)],
                                [lax.empty((N_DEV, in_cols // N_DEV, d), F32)], False, "reduce_w_in_grad_start")
    dx, d_norm_mix_pre = _input_backward(dh1, du1, x2, norm_mix_pre + win_handle[4][0:1, 0:1])

    small_grads = dict(norm_mix_pre=d_norm_mix_pre, conv_dw_b=d_conv_b, conv_ln_g=d_ln_g, conv_ln_b=d_ln_b,
                       rel_bias=drel[:, :2 * MAX_REL + 1], norm_mix_post=d_norm_mix_post, norm_ffn_pre=d_norm_ffn_pre,
                       ffn_dw_b=jnp.concatenate([dbg, dbv], axis=1), norm_ffn_post=d_norm_ffn_post)
    pieces = [small_grads[nm] for nm in SMALL] + [d_conv_w, jnp.concatenate([dwg, dwv], axis=1), loss]
    count = sum(p.size for p in pieces)
    (gathered_small,) = _all_gather([_pack(pieces, _rows_for(count))], "all_gather_small_grads")
    summed = _sum_devices(gathered_small)
    shapes = [weights[nm].shape for nm in SMALL] + [(CONV_K, CONV_WIDTH), (FFN_K, 2 * D_FF), (1, 1)]
    unpacked = _unpack(summed, shapes)
    grads = dict(zip(SMALL, unpacked[:len(SMALL)]))
    cw_shard, fw_shard = CONV_WIDTH // N_DEV, 2 * D_FF // N_DEV
    grads["conv_dw_w"] = lax.dynamic_slice_in_dim(unpacked[-3], my_dev * cw_shard, cw_shard, axis=1)[None]
    grads["ffn_dw_w"] = lax.dynamic_slice_in_dim(unpacked[-2], my_dev * fw_shard, fw_shard, axis=1)[None]
    total_loss = unpacked[-1].reshape(())

    me = jnp.reshape(my_dev, (1,)).astype(jnp.int32)
    delta, new_m, new_v = {}, {}, {}

    def finish(nm, send, recv, srcs, lands, after, transposed):
        srcs, lands = _scatter_wait(send, recv, srcs, lands, after, False, "reduce_" + nm + "_grad_wait")
        for name_a, src, land in zip(nm.split("_and_"), srcs, lands):
            g = _sum_received(src, land, me, "sum_received_" + name_a)
            grads[name_a] = (g.T if transposed[name_a] else g)[None]
            dl, nm1, nv1 = _adamw(weights[name_a][0], grads[name_a][0], mom1[name_a][0], mom2[name_a][0], "adamw_" + name_a)
            delta[name_a], new_m[name_a], new_v[name_a] = dl[None], nm1[None], nv1[None]

    transposed = dict(w_in=True, w_out=False, w_up=True, w_down=False)
    finish("w_up_and_w_down", red_send, red_recv, ffn_grads, red_lands, dx, transposed)
    finish("w_out", *wout_handle[:4], dx, transposed)
    finish("w_in", *win_handle[:4], delta["w_up"], transposed)
    small_names = SMALL + SHARDED_SMALL
    small_count = sum(weights[nm].size for nm in small_names)
    small_rows = _rows_for(small_count)
    packed = [_pack([src[nm] for nm in small_names], small_rows) for src in (weights, grads, mom1, mom2)]
    outs = _adamw(*packed, "adamw_small")
    small_shapes = [weights[nm].shape for nm in small_names]
    for store, arr in zip((delta, new_m, new_v), outs):
        store.update(zip(small_names, _unpack(arr, small_shapes)))

    return (total_loss, dx[None], *[grads[nm] for nm in WEIGHTS], *[delta[nm] for nm in WEIGHTS],
            *[new_m[nm] for nm in WEIGHTS], *[new_v[nm] for nm in WEIGHTS])
```

```python
import jax
import jax.numpy as jnp
from jax import lax
from jax.experimental import pallas as pl
from jax.experimental.pallas import tpu as pltpu

F32 = jnp.float32
BF16 = jnp.bfloat16
MESH = pl.DeviceIdType.MESH
AXES = ("x", "y", "c")
N_DEV = 8

EPS = 1e-6
NEG_INF = -1e30
D_MODEL = 1024
CONV_WIDTH = 512
ATTN_WIDTH = 512
N_HEADS = 8
HEAD_DIM = 64
CHUNK = 64
LEFT = 8 * CHUNK
QBLK = 2 * CHUNK
WIN = LEFT + QBLK
CONV_K = 31
CONV_HALO = 32
FFN_K = 3
FFN_HALO = 8
D_FF = 2816
MAX_REL = 128
SCALE = HEAD_DIM ** -0.5
ADAM_LR, ADAM_B1, ADAM_B2, ADAM_EPS, ADAM_WD, ADAM_STEP = 0.001, 0.9, 0.999, 1e-08, 0.01, 10

V7X_VMEM_BYTES = 64 * 2**20
VMEM_LIMIT_BYTES = V7X_VMEM_BYTES - 8 * 2**20
LANES = 128


def _params(*sem):
    return pltpu.CompilerParams(dimension_semantics=sem or None, vmem_limit_bytes=VMEM_LIMIT_BYTES)


_DOT_DIMS = {"nn": (((1,), (0,)), ((), ())), "nt": (((1,), (1,)), ((), ())), "tn": (((0,), (0,)), ((), ()))}


def _matmul(a, b, *, mode, m, n, k, tm, tn, tk, out_dtype, name, a_m0=0, b_n0=0, b_k0=0, add=None,
            out_rows=None, out_m0=0, into=None):
    tm, tn, tk = min(tm, m), min(tn, n), min(tk, k)
    out_rows = m if out_rows is None else out_rows
    assert m % tm == 0 and n % tn == 0 and k % tk == 0, (name, m, n, k, tm, tn, tk)
    assert a_m0 % tm == 0 and b_n0 % tn == 0 and b_k0 % tk == 0 and out_m0 % tm == 0, name
    am, bn, bk, om = a_m0 // tm, b_n0 // tn, b_k0 // tk, out_m0 // tm
    gk = k // tk
    dims = _DOT_DIMS[mode]

    if mode == "tn":
        a_spec = pl.BlockSpec((tk, tm), lambda i, j, kk: (kk, i + am))
    else:
        a_spec = pl.BlockSpec((tm, tk), lambda i, j, kk: (i + am, kk))
    if mode == "nt":
        b_spec = pl.BlockSpec((tn, tk), lambda i, j, kk: (j + bn, kk + bk))
    else:
        b_spec = pl.BlockSpec((tk, tn), lambda i, j, kk: (kk + bk, j + bn))
    o_spec = pl.BlockSpec((tm, tn), lambda i, j, kk: (i + om, j))
    in_specs = [a_spec, b_spec]
    operands = [a, b]
    if add is not None:
        assert out_rows == m
        in_specs.append(o_spec)
        operands.append(add)
    aliases = {}
    if into is not None:
        aliases = {len(operands): 0}
        in_specs.append(pl.BlockSpec(memory_space=pl.ANY))
        operands.append(into)

    def body(*refs):
        a_ref, b_ref = refs[0], refs[1]
        add_ref = refs[2] if add is not None else None
        o_ref = refs[len(operands)]
        part = lax.dot_general(a_ref[...].astype(BF16), b_ref[...].astype(BF16), dims,
                               preferred_element_type=F32)

        def finish(total):
            if add_ref is not None:
                total = total + add_ref[...]
            o_ref[...] = total.astype(out_dtype)

        if gk == 1:
            finish(part)
        else:
            acc_ref = refs[-1]
            kk = pl.program_id(2)

            @pl.when(kk == 0)
            def _():
                acc_ref[...] = part

            @pl.when(kk > 0)
            def _():
                acc_ref[...] += part

            @pl.when(kk == gk - 1)
            def _():
                finish(acc_ref[...])

    return pl.pallas_call(
        body, name=name,
        grid=(m // tm, n // tn, gk),
        in_specs=in_specs, out_specs=o_spec,
        out_shape=jax.ShapeDtypeStruct((out_rows, n), out_dtype),
        scratch_shapes=[pltpu.VMEM((tm, tn), F32)] if gk > 1 else [],
        input_output_aliases=aliases,
        compiler_params=_params("parallel", "parallel", "arbitrary"),
    )(*operands)


def _rms_hat(v):
    r = lax.rsqrt(jnp.mean(v * v, axis=-1, keepdims=True) + EPS)
    return v * r, r


def _rms_bwd(dn, hat, r):
    return r * (dn - hat * jnp.mean(dn * hat, axis=-1, keepdims=True))


def _sigmoid(v):
    return 1.0 / (1.0 + jnp.exp(-v))


_GELU_C = 0.7978845608028654


def _gelu(v):
    return 0.5 * v * (1.0 + jnp.tanh(_GELU_C * (v + 0.044715 * v * (v * v))))


def _gelu_parts(v):
    v2 = v * v
    t = jnp.tanh(_GELU_C * (v + 0.044715 * v * v2))
    cdf = 0.5 * (1.0 + t)
    dcdf = 0.5 * (1.0 - t * t) * _GELU_C * (1.0 + 3.0 * 0.044715 * v2)
    return v * cdf, cdf + v * dcdf


def _row_tile(t_rows, want):
    tile = min(want, t_rows)
    assert t_rows % tile == 0
    return tile


def _pre_norm(x, g, name):
    t_rows, d = x.shape
    tm = _row_tile(t_rows, 512)

    def body(x_ref, g_ref, u_ref):
        hat, _ = _rms_hat(x_ref[...])
        u_ref[...] = (hat * g_ref[...]).astype(BF16)

    return pl.pallas_call(
        body, name=name, grid=(t_rows // tm,),
        in_specs=[pl.BlockSpec((tm, d), lambda i: (i, 0)), pl.BlockSpec((1, d), lambda i: (0, 0))],
        out_specs=pl.BlockSpec((tm, d), lambda i: (i, 0)),
        out_shape=jax.ShapeDtypeStruct((t_rows, d), BF16),
        compiler_params=_params("parallel"),
    )(x, g)


def _mid_forward(x, mixed, g_post, g_pre):
    t_rows, d = x.shape
    tm = _row_tile(t_rows, 512)

    def body(x_ref, mixed_ref, gpost_ref, gpre_ref, h1_ref, u2_ref):
        hat, _ = _rms_hat(mixed_ref[...])
        h1 = x_ref[...] + hat * gpost_ref[...]
        h1_ref[...] = h1
        hat1, _ = _rms_hat(h1)
        u2_ref[...] = (hat1 * gpre_ref[...]).astype(BF16)

    row = pl.BlockSpec((tm, d), lambda i: (i, 0))
    vec = pl.BlockSpec((1, d), lambda i: (0, 0))
    return pl.pallas_call(
        body, name="mid_forward", grid=(t_rows // tm,),
        in_specs=[row, row, vec, vec], out_specs=[row, row],
        out_shape=[jax.ShapeDtypeStruct((t_rows, d), F32), jax.ShapeDtypeStruct((t_rows, d), BF16)],
        compiler_params=_params("parallel"),
    )(x, mixed, g_post, g_pre)


def _loss_and_head_backward(h1, f, target, g_post):
    t_rows, d = h1.shape
    tm = _row_tile(t_rows, 512)
    nt = t_rows // tm

    def body(h1_ref, f_ref, tgt_ref, g_ref, loss_ref, dy_ref, df_ref, dg_ref, sq_ref):
        i = pl.program_id(0)

        @pl.when(i == 0)
        def _():
            sq_ref[...] = jnp.zeros_like(sq_ref)
            dg_ref[...] = jnp.zeros_like(dg_ref)

        g = g_ref[...]
        hat, r = _rms_hat(f_ref[...])
        err = h1_ref[...] + hat * g - tgt_ref[...]
        sq_ref[...] += jnp.sum(err * err, axis=0, keepdims=True)
        dy = err * (1.0 / d)
        dy_ref[...] = dy
        dg_ref[...] += jnp.sum(dy * hat, axis=0, keepdims=True)
        df_ref[...] = _rms_bwd(dy * g, hat, r).astype(BF16)

        @pl.when(i == nt - 1)
        def _():
            loss_ref[...] = (0.5 / d) * jnp.sum(sq_ref[...], axis=1, keepdims=True)

    row = pl.BlockSpec((tm, d), lambda i: (i, 0))
    vec = pl.BlockSpec((1, d), lambda i: (0, 0))
    return pl.pallas_call(
        body, name="loss_head_backward", grid=(nt,),
        in_specs=[row, row, row, vec],
        out_specs=[pl.BlockSpec((1, 1), lambda i: (0, 0)), row, row, vec],
        out_shape=[jax.ShapeDtypeStruct((1, 1), F32), jax.ShapeDtypeStruct((t_rows, d), F32),
                   jax.ShapeDtypeStruct((t_rows, d), BF16), jax.ShapeDtypeStruct((1, d), F32)],
        scratch_shapes=[pltpu.VMEM((1, d), F32)],
        compiler_params=_params("arbitrary"),
    )(h1, f, target, g_post)


def _mid_backward(dy, du2, h1, mixed, g_pre, g_post):
    t_rows, d = dy.shape
    tm = _row_tile(t_rows, 512)

    def body(dy_ref, du2_ref, h1_ref, mixed_ref, gpre_ref, gpost_ref, dh1_ref, dmixed_ref, dgpre_ref, dgpost_ref):
        @pl.when(pl.program_id(0) == 0)
        def _():
            dgpre_ref[...] = jnp.zeros_like(dgpre_ref)
            dgpost_ref[...] = jnp.zeros_like(dgpost_ref)

        du2 = du2_ref[...]
        hat1, r1 = _rms_hat(h1_ref[...])
        dgpre_ref[...] += jnp.sum(du2 * hat1, axis=0, keepdims=True)
        dh1 = dy_ref[...] + _rms_bwd(du2 * gpre_ref[...], hat1, r1)
        dh1_ref[...] = dh1
        hatm, rm = _rms_hat(mixed_ref[...])
        dgpost_ref[...] += jnp.sum(dh1 * hatm, axis=0, keepdims=True)
        dmixed_ref[...] = _rms_bwd(dh1 * gpost_ref[...], hatm, rm).astype(BF16)

    row = pl.BlockSpec((tm, d), lambda i: (i, 0))
    vec = pl.BlockSpec((1, d), lambda i: (0, 0))
    return pl.pallas_call(
        body, name="mid_backward", grid=(t_rows // tm,),
        in_specs=[row, row, row, row, vec, vec], out_specs=[row, row, vec, vec],
        out_shape=[jax.ShapeDtypeStruct((t_rows, d), F32), jax.ShapeDtypeStruct((t_rows, d), BF16),
                   jax.ShapeDtypeStruct((1, d), F32), jax.ShapeDtypeStruct((1, d), F32)],
        compiler_params=_params("arbitrary"),
    )(dy, du2, h1, mixed, g_pre, g_post)


def _input_backward(dh1, du1, x, g_pre):
    t_rows, d = x.shape
    tm = _row_tile(t_rows, 512)

    def body(dh1_ref, du1_ref, x_ref, g_ref, dx_ref, dg_ref):
        @pl.when(pl.program_id(0) == 0)
        def _():
            dg_ref[...] = jnp.zeros_like(dg_ref)

        du1 = du1_ref[...]
        hat, r = _rms_hat(x_ref[...])
        dg_ref[...] += jnp.sum(du1 * hat, axis=0, keepdims=True)
        dx_ref[...] = dh1_ref[...] + _rms_bwd(du1 * g_ref[...], hat, r)

    row = pl.BlockSpec((tm, d), lambda i: (i, 0))
    vec = pl.BlockSpec((1, d), lambda i: (0, 0))
    return pl.pallas_call(
        body, name="input_backward", grid=(t_rows // tm,),
        in_specs=[row, row, row, vec], out_specs=[row, vec],
        out_shape=[jax.ShapeDtypeStruct((t_rows, d), F32), jax.ShapeDtypeStruct((1, d), F32)],
        compiler_params=_params("arbitrary"),
    )(dh1, du1, x, g_pre)


CONV_STRIP = 32


def _glu(block):
    return block[:, :CONV_WIDTH] * _sigmoid(block[:, CONV_WIDTH:])


def _layer_norm_parts(c):
    mu = jnp.mean(c, axis=-1, keepdims=True)
    xc = c - mu
    r = lax.rsqrt(jnp.mean(xc * xc, axis=-1, keepdims=True) + EPS)
    return xc * r, r


CONV_WINDOW = 2 * CONV_STRIP
SHIFTED_ROWS = CONV_WINDOW - 8


def _shifted_copies(v, shifted):
    for s in range(1, 8):
        shifted[s] = v[s:s + SHIFTED_ROWS, :]


def _window_rows(v, shifted, start):
    s, a = start % 8, start - start % 8
    return v[a:a + CONV_STRIP, :] if s == 0 else shifted[s, a:a + CONV_STRIP, :]


def _conv_forward(proj_a, w, b, ln_g, ln_b):
    t_rows = proj_a.shape[0]
    tm = _row_tile(t_rows, 512)
    hb = tm // CONV_HALO
    cw = CONV_WIDTH

    def body(cur_ref, prev_ref, w_ref, b_ref, g_ref, beta_ref, c_ref, out_ref, hbuf, shifted):
        i = pl.program_id(0)
        hbuf[0:CONV_HALO, :] = jnp.where(i > 0, _glu(prev_ref[...]), 0.0)
        hbuf[CONV_HALO:, :] = _glu(cur_ref[...])

        def strip(s, carry):
            base = pl.multiple_of(s * CONV_STRIP, CONV_STRIP)
            v = hbuf[pl.ds(base, CONV_WINDOW), :]
            _shifted_copies(v, shifted)
            acc = jnp.broadcast_to(b_ref[...], (CONV_STRIP, cw))
            off = CONV_HALO - (CONV_K - 1)
            for kk in range(CONV_K):
                acc = acc + w_ref[kk:kk + 1, :] * _window_rows(v, shifted, off + kk)
            c_ref[pl.ds(base, CONV_STRIP), :] = acc
            hat, _ = _layer_norm_parts(acc)
            z = hat * g_ref[...] + beta_ref[...]
            out_ref[pl.ds(base, CONV_STRIP), :] = (z * _sigmoid(z)).astype(BF16)
            return carry

        lax.fori_loop(0, tm // CONV_STRIP, strip, 0)

    vec = pl.BlockSpec((1, cw), lambda i: (0, 0))
    return pl.pallas_call(
        body, name="conv_forward", grid=(t_rows // tm,),
        in_specs=[pl.BlockSpec((tm, 2 * cw), lambda i: (i, 0)),
                  pl.BlockSpec((CONV_HALO, 2 * cw), lambda i: (jnp.maximum(i * hb - 1, 0), 0)),
                  pl.BlockSpec((CONV_K, cw), lambda i: (0, 0)), vec, vec, vec],
        out_specs=[pl.BlockSpec((tm, cw), lambda i: (i, 0)), pl.BlockSpec((tm, cw), lambda i: (i, 0))],
        out_shape=[jax.ShapeDtypeStruct((t_rows, cw), F32), jax.ShapeDtypeStruct((t_rows, cw), BF16)],
        scratch_shapes=[pltpu.VMEM((tm + CONV_HALO, cw), F32), pltpu.VMEM((8, SHIFTED_ROWS, cw), F32)],
        compiler_params=_params("parallel"),
    )(proj_a, proj_a, w, b, ln_g, ln_b)


def _conv_backward(dout, c, proj_a, w, ln_g, ln_b):
    t_rows = c.shape[0]
    tm = _row_tile(t_rows, 512)
    hb = tm // CONV_HALO
    nt = t_rows // tm
    last_halo = t_rows // CONV_HALO - 1
    cw = CONV_WIDTH

    def body(dout_ref, dout_next_ref, c_ref, c_next_ref, cur_ref, prev_ref, w_ref, g_ref, beta_ref,
             dproj_ref, dw_ref, db_ref, dg_ref, dbeta_ref, hbuf, dcbuf, dwacc, h_shifted, d_shifted):
        i = pl.program_id(0)

        @pl.when(i == 0)
        def _():
            dwacc[...] = jnp.zeros_like(dwacc)
            db_ref[...] = jnp.zeros_like(db_ref)
            dg_ref[...] = jnp.zeros_like(dg_ref)
            dbeta_ref[...] = jnp.zeros_like(dbeta_ref)

        def ln_swish_backward(dout_v, c_v):
            hat, r = _layer_norm_parts(c_v)
            g = g_ref[...]
            z = hat * g + beta_ref[...]
            sg = _sigmoid(z)
            dz = dout_v * (sg * (1.0 + z * (1.0 - sg)))
            dhat = dz * g
            dc = r * (dhat - jnp.mean(dhat, axis=-1, keepdims=True)
                      - hat * jnp.mean(dhat * hat, axis=-1, keepdims=True))
            return dc, dz, hat

        dc, dz, hat = ln_swish_backward(dout_ref[...], c_ref[...])
        dg_ref[...] += jnp.sum(dz * hat, axis=0, keepdims=True)
        dbeta_ref[...] += jnp.sum(dz, axis=0, keepdims=True)
        db_ref[...] += jnp.sum(dc, axis=0, keepdims=True)
        dcbuf[0:tm, :] = dc
        dc_next, _, _ = ln_swish_backward(dout_next_ref[...], c_next_ref[...])
        dcbuf[tm:, :] = jnp.where(i < nt - 1, dc_next, 0.0)

        hbuf[0:CONV_HALO, :] = jnp.where(i > 0, _glu(prev_ref[...]), 0.0)
        hbuf[CONV_HALO:, :] = _glu(cur_ref[...])

        def strip(s, carry):
            base = pl.multiple_of(s * CONV_STRIP, CONV_STRIP)
            dv = dcbuf[pl.ds(base, CONV_WINDOW), :]
            hv = hbuf[pl.ds(base, CONV_WINDOW), :]
            _shifted_copies(dv, d_shifted)
            _shifted_copies(hv, h_shifted)
            dcs = dv[0:CONV_STRIP, :]
            dh = jnp.zeros((CONV_STRIP, cw), F32)
            off = CONV_HALO - (CONV_K - 1)
            for kk in range(CONV_K):
                back = CONV_K - 1 - kk
                dh = dh + w_ref[kk:kk + 1, :] * _window_rows(dv, d_shifted, back)
                prod = dcs * _window_rows(hv, h_shifted, off + kk)
                dwacc[kk] += jnp.sum(prod.reshape(CONV_STRIP // 8, 8, cw), axis=0)
            blk = cur_ref[pl.ds(base, CONV_STRIP), :]
            val, sg = blk[:, :cw], _sigmoid(blk[:, cw:])
            dproj_ref[pl.ds(base, CONV_STRIP), 0:cw] = (dh * sg).astype(BF16)
            dproj_ref[pl.ds(base, CONV_STRIP), cw:2 * cw] = (dh * val * sg * (1.0 - sg)).astype(BF16)
            return carry

        lax.fori_loop(0, tm // CONV_STRIP, strip, 0)

        @pl.when(i == nt - 1)
        def _():
            for kk in range(CONV_K):
                dw_ref[kk:kk + 1, :] = jnp.sum(dwacc[kk], axis=0, keepdims=True)

    vec = pl.BlockSpec((1, cw), lambda i: (0, 0))
    cur = lambda width: pl.BlockSpec((tm, width), lambda i: (i, 0))
    nxt = lambda width: pl.BlockSpec((CONV_HALO, width), lambda i: (jnp.minimum((i + 1) * hb, last_halo), 0))
    return pl.pallas_call(
        body, name="conv_backward", grid=(nt,),
        in_specs=[cur(cw), nxt(cw), cur(cw), nxt(cw), cur(2 * cw),
                  pl.BlockSpec((CONV_HALO, 2 * cw), lambda i: (jnp.maximum(i * hb - 1, 0), 0)),
                  pl.BlockSpec((CONV_K, cw), lambda i: (0, 0)), vec, vec],
        out_specs=[cur(2 * cw), pl.BlockSpec((CONV_K, cw), lambda i: (0, 0)), vec, vec, vec],
        out_shape=[jax.ShapeDtypeStruct((t_rows, 2 * cw), BF16), jax.ShapeDtypeStruct((CONV_K, cw), F32),
                   jax.ShapeDtypeStruct((1, cw), F32), jax.ShapeDtypeStruct((1, cw), F32),
                   jax.ShapeDtypeStruct((1, cw), F32)],
        scratch_shapes=[pltpu.VMEM((tm + CONV_HALO, cw), F32), pltpu.VMEM((tm + CONV_HALO, cw), F32),
                        pltpu.VMEM((CONV_K, 8, cw), F32), pltpu.VMEM((8, SHIFTED_ROWS, cw), F32),
                        pltpu.VMEM((8, SHIFTED_ROWS, cw), F32)],
        compiler_params=_params("arbitrary"),
    )(dout, dout, c, c, proj_a, proj_a, w, ln_g, ln_b)


def _attn_load_kv(kv_hbm, k_pad, v_pad, sem, t_cols):
    k_pad[:, 0:LEFT] = jnp.zeros((ATTN_WIDTH, LEFT), BF16)
    v_pad[:, 0:LEFT] = jnp.zeros((ATTN_WIDTH, LEFT), BF16)
    ck = pltpu.make_async_copy(kv_hbm.at[pl.ds(ATTN_WIDTH, ATTN_WIDTH), :], k_pad.at[:, pl.ds(LEFT, t_cols)], sem.at[0])
    cv = pltpu.make_async_copy(kv_hbm.at[pl.ds(2 * ATTN_WIDTH, ATTN_WIDTH), :], v_pad.at[:, pl.ds(LEFT, t_cols)], sem.at[1])
    ck.start()
    cv.start()
    ck.wait()
    cv.wait()


def _attn_build_bias(tab_ref, bias_t):
    row = lax.broadcasted_iota(jnp.int32, (LANES, LANES), 0)
    lane = lax.broadcasted_iota(jnp.int32, (LANES, LANES), 1)
    upper = lane >= row
    lane64 = lax.broadcasted_iota(jnp.int32, (CHUNK, LANES), 1)
    for h in range(N_HEADS):
        far = jnp.broadcast_to(tab_ref[h:h + 1, 2 * MAX_REL:2 * MAX_REL + 1], (LANES, LANES))
        hi = jnp.broadcast_to(tab_ref[h:h + 1, MAX_REL:2 * MAX_REL], (LANES, LANES))
        lo = jnp.broadcast_to(tab_ref[h:h + 1, 0:MAX_REL], (LANES, LANES))
        hi_d = pltpu.roll(hi, 0, 1, stride=1, stride_axis=0)
        lo_d = pltpu.roll(lo, 0, 1, stride=1, stride_axis=0)
        bias_t[h, 0:WIN - 2 * LANES, :] = jnp.broadcast_to(far[0:1, :], (WIN - 2 * LANES, LANES))
        bias_t[h, WIN - 2 * LANES:WIN - LANES, :] = jnp.where(upper, far, hi_d)
        bias_t[h, WIN - LANES:WIN, :] = jnp.where(upper, hi_d, lo_d)
        bias_t[h, 0:CHUNK, :] = jnp.where(lane64 < CHUNK, bias_t[h, 0:CHUNK, :], NEG_INF)
        bias_t[h, WIN - CHUNK:WIN, :] = jnp.where(lane64 >= CHUNK, bias_t[h, WIN - CHUNK:WIN, :], NEG_INF)


def _head_rows(h):
    return slice(h * HEAD_DIM, (h + 1) * HEAD_DIM)


def _attn_scores(k_pad, q_ref, s_buf, w0):
    for h in range(N_HEADS):
        q_h = q_ref[_head_rows(h), :] * jnp.asarray(SCALE, BF16)
        s_buf[h] = lax.dot_general(k_pad[_head_rows(h), pl.ds(w0, WIN)], q_h, _DOT_DIMS["tn"],
                                   preferred_element_type=F32)


def _attn_logits(s, bias, first_valid, key0=0):
    s = s + bias
    if first_valid is not None:
        s = jnp.where(lax.broadcasted_iota(jnp.int32, s.shape, 0) + key0 >= first_valid, s, NEG_INF)
    return s


def _attn_probs(s, bias_h, first_valid):
    s = _attn_logits(s, bias_h, first_valid)
    top = jnp.max(s, axis=0, keepdims=True)
    e = jnp.exp(s - top)
    total = jnp.sum(e, axis=0, keepdims=True)
    return e * (1.0 / total), top + jnp.log(total)


def _attn_by_padding(m, fn):
    @pl.when(m < LEFT // QBLK)
    def _():
        fn(LEFT - m * QBLK)

    @pl.when(m >= LEFT // QBLK)
    def _():
        fn(None)


def _attn_forward(qkv_t, rel_bias):
    t_cols = qkv_t.shape[1]
    steps = t_cols // QBLK

    def body(q_ref, kv_hbm, tab_ref, o_ref, lse_ref, k_pad, v_pad, bias_t, s_buf, p_buf, sem):
        m = pl.program_id(0)

        @pl.when(m == 0)
        def _():
            _attn_build_bias(tab_ref, bias_t)
            _attn_load_kv(kv_hbm, k_pad, v_pad, sem, t_cols)

        w0 = pl.multiple_of(m * QBLK, QBLK)
        _attn_scores(k_pad, q_ref, s_buf, w0)

        def softmax(first_valid):
            for h in range(N_HEADS):
                p, lse = _attn_probs(s_buf[h], bias_t[h], first_valid)
                p_buf[h] = p.astype(BF16)
                lse_ref[h:h + 1, :] = lse

        _attn_by_padding(m, softmax)
        for h in range(N_HEADS):
            o_h = lax.dot_general(v_pad[_head_rows(h), pl.ds(w0, WIN)], p_buf[h], _DOT_DIMS["nn"],
                                  preferred_element_type=F32)
            o_ref[_head_rows(h), :] = o_h.astype(BF16)

    return pl.pallas_call(
        body, name="attn_forward", grid=(steps,),
        in_specs=[pl.BlockSpec((ATTN_WIDTH, QBLK), lambda m: (0, m)),
                  pl.BlockSpec(memory_space=pl.ANY),
                  pl.BlockSpec((N_HEADS, 2 * MAX_REL + 1), lambda m: (0, 0))],
        out_specs=[pl.BlockSpec((ATTN_WIDTH, QBLK), lambda m: (0, m)), pl.BlockSpec((N_HEADS, QBLK), lambda m: (0, m))],
        out_shape=[jax.ShapeDtypeStruct((ATTN_WIDTH, t_cols), BF16), jax.ShapeDtypeStruct((N_HEADS, t_cols), F32)],
        scratch_shapes=[pltpu.VMEM((ATTN_WIDTH, LEFT + t_cols), BF16), pltpu.VMEM((ATTN_WIDTH, LEFT + t_cols), BF16),
                        pltpu.VMEM((N_HEADS, WIN, QBLK), F32), pltpu.VMEM((N_HEADS, WIN, QBLK), F32),
                        pltpu.VMEM((N_HEADS, WIN, QBLK), BF16), pltpu.SemaphoreType.DMA((2,))],
        compiler_params=_params("arbitrary"),
    )(qkv_t, qkv_t, rel_bias)


def _reverse_lanes(v, flip):
    out = jnp.zeros(v.shape, F32)
    rest = v
    for _ in range(3):
        piece = rest.astype(BF16)
        out = out + lax.dot_general(piece, flip, _DOT_DIMS["nn"], preferred_element_type=F32)
        rest = rest - piece.astype(F32)
    return out


def _attn_bias_grad(dbias_t, drel_ref):
    row = lax.broadcasted_iota(jnp.int32, (LANES, LANES), 0)
    lane = lax.broadcasted_iota(jnp.int32, (LANES, LANES), 1)
    flip = (row + lane == LANES - 1).astype(BF16)
    head = lax.broadcasted_iota(jnp.int32, (N_HEADS, LANES), 0)
    lane8 = lax.broadcasted_iota(jnp.int32, (N_HEADS, LANES), 1)
    upper_rev = jnp.zeros((N_HEADS, LANES), F32)
    lower_rev = jnp.zeros((N_HEADS, LANES), F32)
    far = jnp.zeros((N_HEADS, LANES), F32)
    for h in range(N_HEADS):
        def diagonals(block):
            skew = pltpu.roll(_reverse_lanes(block, flip), 0, 1, stride=1, stride_axis=0)
            pos = jnp.sum(jnp.where(lane >= row, skew, 0.0), axis=0, keepdims=True)
            neg = jnp.sum(jnp.where(lane < row, skew, 0.0), axis=0, keepdims=True)
            return pos, neg

        pos4, neg4 = diagonals(dbias_t[h, WIN - LANES:WIN, :])
        pos3, neg3 = diagonals(dbias_t[h, WIN - 2 * LANES:WIN - LANES, :])
        far_h = jnp.sum(jnp.sum(dbias_t[h, 0:WIN - 2 * LANES, :], axis=0, keepdims=True), axis=1, keepdims=True)
        far_h = far_h + jnp.sum(pos3, axis=1, keepdims=True)
        upper_rev = jnp.where(head == h, pos4 + neg3, upper_rev)
        lower_rev = jnp.where(head == h, neg4, lower_rev)
        far = jnp.where((head == h) & (lane8 == 0), far_h, far)
    drel_ref[:, 0:LANES] = _reverse_lanes(lower_rev, flip)
    drel_ref[:, LANES:2 * LANES] = _reverse_lanes(upper_rev, flip)
    drel_ref[:, 2 * LANES:3 * LANES] = far


def _attn_backward(qkv_t, o_t, do_t, lse, rel_bias):
    t_cols = qkv_t.shape[1]
    steps = t_cols // QBLK
    flush = LEFT // QBLK
    total = steps + flush

    def body(q_ref, o_ref, do_ref, lse_ref, kv_hbm, tab_ref, dq_ref, dk_ref, dv_ref, drel_ref,
             k_pad, v_pad, bias_t, dbias_t, dk_acc, dv_acc, s_buf, dp_buf, p_buf, ds_buf, sem):
        m = pl.program_id(0)

        @pl.when(m == 0)
        def _():
            _attn_build_bias(tab_ref, bias_t)
            _attn_load_kv(kv_hbm, k_pad, v_pad, sem, t_cols)
            dbias_t[...] = jnp.zeros_like(dbias_t)
            dk_acc[...] = jnp.zeros_like(dk_acc)
            dv_acc[...] = jnp.zeros_like(dv_acc)

        @pl.when(m < steps)
        def _():
            w0 = pl.multiple_of(m * QBLK, QBLK)
            _attn_scores(k_pad, q_ref, s_buf, w0)
            for h in range(N_HEADS):
                dp_buf[h] = lax.dot_general(v_pad[_head_rows(h), pl.ds(w0, WIN)], do_ref[_head_rows(h), :],
                                            _DOT_DIMS["tn"], preferred_element_type=F32)

            def softmax_backward(first_valid):
                for h in range(N_HEADS):
                    rows = _head_rows(h)
                    delta = jnp.sum(do_ref[rows, :].astype(F32) * o_ref[rows, :].astype(F32), axis=0, keepdims=True)
                    lse_h = lse_ref[h:h + 1, :]
                    for b in range(WIN // LANES):
                        keys = slice(b * LANES, (b + 1) * LANES)
                        s = _attn_logits(s_buf[h, keys, :], bias_t[h, keys, :], first_valid, b * LANES)
                        p = jnp.exp(s - lse_h)
                        ds = p * (dp_buf[h, keys, :] - delta)
                        dbias_t[h, keys, :] += ds
                        p_buf[h, keys, :] = p.astype(BF16)
                        ds_buf[h, keys, :] = (ds * SCALE).astype(BF16)

            _attn_by_padding(m, softmax_backward)
            for h in range(N_HEADS):
                rows = _head_rows(h)
                dq_h = lax.dot_general(k_pad[rows, pl.ds(w0, WIN)], ds_buf[h], _DOT_DIMS["nn"], preferred_element_type=F32)
                dq_ref[rows, :] = dq_h.astype(BF16)
                dk_acc[rows, :] += lax.dot_general(q_ref[rows, :], ds_buf[h], _DOT_DIMS["nt"], preferred_element_type=F32)
                dv_acc[rows, :] += lax.dot_general(do_ref[rows, :], p_buf[h], _DOT_DIMS["nt"], preferred_element_type=F32)

        dk_ref[...] = dk_acc[:, 0:QBLK].astype(BF16)
        dv_ref[...] = dv_acc[:, 0:QBLK].astype(BF16)
        for acc in (dk_acc, dv_acc):
            rest = acc[:, QBLK:WIN]
            acc[:, 0:LEFT] = rest
            acc[:, LEFT:WIN] = jnp.zeros((ATTN_WIDTH, QBLK), F32)

        @pl.when(m == total - 1)
        def _():
            _attn_bias_grad(dbias_t, drel_ref)

    qblk = pl.BlockSpec((ATTN_WIDTH, QBLK), lambda m: (0, jnp.minimum(m, steps - 1)))
    kblk = pl.BlockSpec((ATTN_WIDTH, QBLK), lambda m: (0, jnp.maximum(m - flush, 0)))
    dq, dk, dv, drel = pl.pallas_call(
        body, name="attn_backward", grid=(total,),
        in_specs=[qblk, qblk, qblk, pl.BlockSpec((N_HEADS, QBLK), lambda m: (0, jnp.minimum(m, steps - 1))),
                  pl.BlockSpec(memory_space=pl.ANY), pl.BlockSpec((N_HEADS, 2 * MAX_REL + 1), lambda m: (0, 0))],
        out_specs=[qblk, kblk, kblk, pl.BlockSpec((N_HEADS, 3 * LANES), lambda m: (0, 0))],
        out_shape=[jax.ShapeDtypeStruct((ATTN_WIDTH, t_cols), BF16)] * 3
        + [jax.ShapeDtypeStruct((N_HEADS, 3 * LANES), F32)],
        scratch_shapes=[pltpu.VMEM((ATTN_WIDTH, LEFT + t_cols), BF16), pltpu.VMEM((ATTN_WIDTH, LEFT + t_cols), BF16),
                        pltpu.VMEM((N_HEADS, WIN, QBLK), F32), pltpu.VMEM((N_HEADS, WIN, QBLK), F32),
                        pltpu.VMEM((ATTN_WIDTH, WIN), F32), pltpu.VMEM((ATTN_WIDTH, WIN), F32),
                        pltpu.VMEM((N_HEADS, WIN, QBLK), F32), pltpu.VMEM((N_HEADS, WIN, QBLK), F32),
                        pltpu.VMEM((N_HEADS, WIN, QBLK), BF16), pltpu.VMEM((N_HEADS, WIN, QBLK), BF16),
                        pltpu.SemaphoreType.DMA((2,))],
        compiler_params=_params("arbitrary"),
    )(qkv_t, o_t, do_t, lse, qkv_t, rel_bias)
    return dq, dk, dv, drel


FFN_TC = D_FF // 2
FFN_TR = 256


def _ffn_specs(t_rows, tr):
    nj = D_FF // FFN_TC
    hb = tr // FFN_HALO
    last_halo = t_rows // FFN_HALO - 1
    cur = lambda off: pl.BlockSpec((tr, FFN_TC), lambda j, i: (i, j + off))
    prev = lambda off: pl.BlockSpec((FFN_HALO, FFN_TC), lambda j, i: (jnp.maximum(i * hb - 1, 0), j + off))
    nxt = lambda off: pl.BlockSpec((FFN_HALO, FFN_TC), lambda j, i: (jnp.minimum((i + 1) * hb, last_halo), j + off))
    wspec = lambda off: pl.BlockSpec((FFN_K, FFN_TC), lambda j, i: (0, j + off))
    bspec = lambda off: pl.BlockSpec((1, FFN_TC), lambda j, i: (0, j + off))
    return nj, cur, prev, nxt, wspec, bspec


FFN_STRIP = 16


def _ffn_conv(win, w, b, rows):
    out = b + w[2] * win[FFN_HALO:FFN_HALO + rows, :]
    out = out + w[1] * win[FFN_HALO - 1:FFN_HALO - 1 + rows, :]
    return out + w[0] * win[FFN_HALO - 2:FFN_HALO - 2 + rows, :]


def _taps(w_ref):
    return [w_ref[kk:kk + 1, :] for kk in range(FFN_K)]


def _ffn_first_window(prev_ref, cur_ref, tile, rows):
    return jnp.concatenate([jnp.where(tile > 0, prev_ref[...], 0.0), cur_ref[0:rows, :]], axis=0)


def _fold8(v):
    return jnp.sum(v.reshape(v.shape[0] // 8, 8, v.shape[1]), axis=0)


def _ffn_activation(hup, w, b):
    t_rows = hup.shape[0]
    tr = _row_tile(t_rows, FFN_TR)
    nj, cur, prev, nxt, wspec, bspec = _ffn_specs(t_rows, tr)
    rs = FFN_STRIP

    def body(g_ref, gprev_ref, v_ref, vprev_ref, wg_ref, wv_ref, bg_ref, bv_ref, act_ref, gel_ref, slope_ref):
        i = pl.program_id(1)
        wg, wv, bg, bv = _taps(wg_ref), _taps(wv_ref), bg_ref[...], bv_ref[...]

        def emit(base, g_win, v_win):
            gel, dgel = _gelu_parts(_ffn_conv(g_win, wg, bg, rs))
            cv = _ffn_conv(v_win, wv, bv, rs)
            act_ref[pl.ds(base, rs), :] = (gel * cv).astype(BF16)
            gel_ref[pl.ds(base, rs), :] = gel
            slope_ref[pl.ds(base, rs), :] = cv * dgel

        def strip(s, carry):
            base = pl.multiple_of(s * rs, rs)
            emit(base, g_ref[pl.ds(base - FFN_HALO, rs + FFN_HALO), :], v_ref[pl.ds(base - FFN_HALO, rs + FFN_HALO), :])
            return carry

        emit(0, _ffn_first_window(gprev_ref, g_ref, i, rs), _ffn_first_window(vprev_ref, v_ref, i, rs))
        lax.fori_loop(1, tr // rs, strip, 0)

    return pl.pallas_call(
        body, name="ffn_activation", grid=(nj, t_rows // tr),
        in_specs=[cur(0), prev(0), cur(nj), prev(nj), wspec(0), wspec(nj), bspec(0), bspec(nj)],
        out_specs=[cur(0), cur(0), cur(0)],
        out_shape=[jax.ShapeDtypeStruct((t_rows, D_FF), BF16), jax.ShapeDtypeStruct((t_rows, D_FF), F32),
                   jax.ShapeDtypeStruct((t_rows, D_FF), F32)],
        compiler_params=_params("parallel", "parallel"),
    )(hup, hup, hup, hup, w, w, b, b)


def _ffn_backward(dact, gel, slope, hup, w):
    t_rows = hup.shape[0]
    tr = _row_tile(t_rows, FFN_TR)
    nj, cur, prev, nxt, wspec, bspec = _ffn_specs(t_rows, tr)
    ni = t_rows // tr
    rs = FFN_STRIP
    ns = tr // rs

    def body(da_ref, danext_ref, gel_ref, gelnext_ref, slope_ref, slopenext_ref, g_ref, gprev_ref, v_ref, vprev_ref,
             wg_ref, wv_ref, dhg_ref, dhv_ref, dwg_ref, dwv_ref, dbg_ref, dbv_ref, sums):
        i = pl.program_id(1)

        @pl.when(i == 0)
        def _():
            sums[...] = jnp.zeros_like(sums)

        wg, wv = _taps(wg_ref), _taps(wv_ref)
        da_after = jnp.where(i < ni - 1, danext_ref[...], 0.0)

        def strip_at(base, g_win, v_win, carry):
            da = da_ref[pl.ds(base, rs), :]
            dcg, dcv = da * slope_ref[pl.ds(base, rs), :], da * gel_ref[pl.ds(base, rs), :]
            out = []
            for half_i, (dc, after, taps, win, dh_ref) in enumerate(((dcg, carry[0], wg, g_win, dhg_ref),
                                                                   (dcv, carry[1], wv, v_win, dhv_ref))):
                ext = jnp.concatenate([dc, after], axis=0)
                dh = taps[2] * dc + taps[1] * ext[1:1 + rs, :] + taps[0] * ext[2:2 + rs, :]
                dh_ref[pl.ds(base, rs), :] = dh.astype(BF16)
                sums[4 * half_i] += _fold8(dc)
                for kk in range(FFN_K):
                    sums[4 * half_i + 1 + kk] += _fold8(dc * win[FFN_HALO - 2 + kk:FFN_HALO - 2 + kk + rs, :])
                out.append(dc[0:FFN_HALO, :])
            return tuple(out)

        def strip(s, carry):
            base = pl.multiple_of((ns - 1 - s) * rs, rs)
            return strip_at(base, g_ref[pl.ds(base - FFN_HALO, rs + FFN_HALO), :],
                            v_ref[pl.ds(base - FFN_HALO, rs + FFN_HALO), :], carry)

        carry = lax.fori_loop(0, ns - 1, strip, (da_after * slopenext_ref[...], da_after * gelnext_ref[...]))
        strip_at(0, _ffn_first_window(gprev_ref, g_ref, i, rs), _ffn_first_window(vprev_ref, v_ref, i, rs), carry)

        @pl.when(i == ni - 1)
        def _():
            for half_i, (db_ref, dw_ref) in enumerate(((dbg_ref, dwg_ref), (dbv_ref, dwv_ref))):
                db_ref[...] = jnp.sum(sums[4 * half_i], axis=0, keepdims=True)
                for kk in range(FFN_K):
                    dw_ref[kk:kk + 1, :] = jnp.sum(sums[4 * half_i + 1 + kk], axis=0, keepdims=True)

    half = jax.ShapeDtypeStruct((t_rows, D_FF), BF16)
    return pl.pallas_call(
        body, name="ffn_backward", grid=(nj, ni),
        in_specs=[cur(0), nxt(0), cur(0), nxt(0), cur(0), nxt(0), cur(0), prev(0), cur(nj), prev(nj),
                  wspec(0), wspec(nj)],
        out_specs=[cur(0), cur(0), wspec(0), wspec(0), bspec(0), bspec(0)],
        out_shape=[half, half, jax.ShapeDtypeStruct((FFN_K, D_FF), F32), jax.ShapeDtypeStruct((FFN_K, D_FF), F32),
                   jax.ShapeDtypeStruct((1, D_FF), F32), jax.ShapeDtypeStruct((1, D_FF), F32)],
        scratch_shapes=[pltpu.VMEM((2 * (1 + FFN_K), 8, FFN_TC), F32)],
        compiler_params=_params("parallel", "arbitrary"),
    )(dact, dact, gel, gel, slope, slope, hup, hup, hup, hup, w, w)


def _mesh_position():
    return lax.axis_index("x"), lax.axis_index("y"), lax.axis_index("c")


def _hbm_specs(n):
    return [pl.BlockSpec(memory_space=pl.ANY)] * n


def _all_gather(shards, name, placed=None):
    n = len(shards)

    def body(*refs):
        ins = refs[:n]
        outs = refs[2 * n:3 * n] if placed else refs[n:2 * n]
        send_sems, recv_sems, local_sems = refs[-3:]
        x, y, c = _mesh_position()
        me, sibling = (x, y, c), (x, y, 1 - c)
        chips = [(1 - x, y), (x, 1 - y), (1 - x, 1 - y)]

        def copy(a, slot, block, to, src=None):
            dst = outs[a].at[4 * block[0] + 2 * block[1] + block[2]]
            return pltpu.make_async_remote_copy(
                src_ref=dst if src is None else src, dst_ref=dst,
                send_sem=send_sems.at[a, slot], recv_sem=recv_sems.at[a, slot],
                device_id=to, device_id_type=MESH)

        started = []
        for a in range(0 if placed else n):
            mine = pltpu.make_async_copy(ins[a], outs[a].at[4 * x + 2 * y + c], local_sems.at[a])
            mine.start()
            started.append(mine)
        first = []
        for a in range(n):
            first.append(copy(a, 0, me, sibling, src=ins[a]))
            first += [copy(a, 1 + j, me, (*chip, c), src=ins[a]) for j, chip in enumerate(chips)]
        for cp in first:
            cp.start()
        passed = []
        for j, chip in enumerate(chips):
            for a in range(n):
                copy(a, 1 + j, (*chip, c), me).wait_recv()
                fwd = copy(a, 4 + j, (*chip, c), sibling)
                fwd.start()
                passed.append(fwd)
        for a in range(n):
            copy(a, 0, sibling, me).wait_recv()
            for j, chip in enumerate(chips):
                copy(a, 4 + j, (*chip, 1 - c), me).wait_recv()
        for cp in first + passed:
            cp.wait_send()
        for mine in started:
            mine.wait()

    operands = [*shards, *placed] if placed else list(shards)
    return pl.pallas_call(
        body, name=name,
        in_specs=_hbm_specs(len(operands)), out_specs=_hbm_specs(n),
        out_shape=[jax.ShapeDtypeStruct((N_DEV,) + s.shape, s.dtype) for s in shards],
        scratch_shapes=[pltpu.SemaphoreType.DMA((n, 7)), pltpu.SemaphoreType.DMA((n, 7)),
                        pltpu.SemaphoreType.DMA((n,))],
        input_output_aliases={n + a: a for a in range(n)} if placed else {},
        compiler_params=pltpu.CompilerParams(has_side_effects=True),
    )(*operands)


def _place_own(shards, me):
    n = len(shards)

    def body(me_ref, *refs):
        for src, dst in zip(refs[:n], refs[n:]):
            dst[0] = src[...]

    return pl.pallas_call(
        body, name="place_own_shards",
        grid_spec=pltpu.PrefetchScalarGridSpec(
            num_scalar_prefetch=1, grid=(1,),
            in_specs=[pl.BlockSpec(s.shape, lambda i, me_ref: (0, 0)) for s in shards],
            out_specs=[pl.BlockSpec((1,) + s.shape, lambda i, me_ref: (me_ref[0], 0, 0)) for s in shards]),
        out_shape=[jax.ShapeDtypeStruct((N_DEV,) + s.shape, s.dtype) for s in shards],
        compiler_params=_params("arbitrary"),
    )(me, *shards)


_FLIPS = [(dx, dy, dc) for dx in (0, 1) for dy in (0, 1) for dc in (0, 1)][1:]
_HBM = pl.BlockSpec(memory_space=pltpu.HBM)
_SEM = pl.BlockSpec(memory_space=pltpu.SEMAPHORE)
_DATAFLOW = pltpu.SideEffectType.DATAFLOW_SIDE_EFFECTING


def _scatter_copies(src_refs, land_refs, send_sems, recv_sems, gather):
    x, y, c = _mesh_position()
    me = 4 * x + 2 * y + c
    copies = []
    for a, (src, land) in enumerate(zip(src_refs, land_refs)):
        for k, (dx, dy, dc) in enumerate(_FLIPS):
            px, py, pc = (x + dx) % 2, (y + dy) % 2, (c + dc) % 2
            pair = a * len(_FLIPS) + k
            copies.append(pltpu.make_async_remote_copy(
                src_ref=src if gather else src.at[4 * px + 2 * py + pc], dst_ref=land.at[me],
                send_sem=send_sems[pair], recv_sem=recv_sems[pair],
                device_id=(px, py, pc), device_id_type=MESH))
    return copies


def _scatter_start(srcs, lands, gather, name, after=None):
    n = len(srcs)
    pairs = n * len(_FLIPS)
    extra = [] if after is None else [after]

    def body(*refs):
        src_refs, land_refs = refs[:n], refs[n:2 * n]
        first = 2 * n + len(extra)
        send_sems, recv_sems = refs[first:first + pairs], refs[first + pairs:first + 2 * pairs]
        token = refs[-1]
        for cp in _scatter_copies(src_refs, land_refs, send_sems, recv_sems, gather):
            cp.start()
        token[...] = jnp.zeros_like(token)

    arrays = [*srcs, *lands]
    sem = pltpu.SemaphoreType.DMA(())
    out = pl.pallas_call(
        body, name=name,
        out_shape=(*[sem] * (2 * pairs), *[pltpu.HBM(v.shape, v.dtype) for v in arrays],
                   jax.ShapeDtypeStruct((8, LANES), F32)),
        in_specs=[*[_HBM] * (2 * n), *[pl.BlockSpec(memory_space=pl.ANY)] * len(extra)],
        out_specs=(*[_SEM] * (2 * pairs), *[_HBM] * (2 * n), pl.BlockSpec(memory_space=pltpu.VMEM)),
        input_output_aliases={i: 2 * pairs + i for i in range(2 * n)},
        compiler_params=pltpu.CompilerParams(has_side_effects=_DATAFLOW),
    )(*[pltpu.with_memory_space_constraint(v, pltpu.HBM) for v in arrays], *extra)
    sems, rest = out[:2 * pairs], out[2 * pairs:]
    return list(sems[:pairs]), list(sems[pairs:]), list(rest[:n]), list(rest[n:2 * n]), rest[-1]


def _scatter_wait(send_sems, recv_sems, srcs, lands, after, gather, name):
    n = len(srcs)
    pairs = n * len(_FLIPS)

    def body(*refs):
        src_refs, land_refs = refs[:n], refs[n:2 * n]
        send_refs, recv_refs = refs[2 * n:2 * n + pairs], refs[2 * n + pairs:2 * n + 2 * pairs]
        for cp in _scatter_copies(src_refs, land_refs, send_refs, recv_refs, gather):
            cp.wait_send()
            cp.wait_recv()

    arrays = [*srcs, *lands]
    out = pl.pallas_call(
        body, name=name,
        out_shape=tuple(pltpu.HBM(v.shape, v.dtype) for v in arrays),
        in_specs=[*[_HBM] * (2 * n), *[_SEM] * (2 * pairs), pl.BlockSpec(memory_space=pl.ANY)],
        out_specs=tuple([_HBM] * (2 * n)),
        input_output_aliases={i: i for i in range(2 * n)},
        compiler_params=pltpu.CompilerParams(has_side_effects=_DATAFLOW),
    )(*arrays, *send_sems, *recv_sems, after)
    return list(out[:n]), list(out[n:])


def _sum_received(grad, received, me, name):
    _, rows, cols = grad.shape

    def body(me_ref, g_ref, r_ref, o_ref):
        p = pl.program_id(0)
        term = jnp.where(p == me_ref[0], g_ref[0], r_ref[0])

        @pl.when(p == 0)
        def _():
            o_ref[...] = term

        @pl.when(p > 0)
        def _():
            o_ref[...] += term

    blk = (1, rows, cols)
    return pl.pallas_call(
        body, name=name,
        grid_spec=pltpu.PrefetchScalarGridSpec(
            num_scalar_prefetch=1, grid=(N_DEV,),
            in_specs=[pl.BlockSpec(blk, lambda p, me_ref: (me_ref[0], 0, 0)),
                      pl.BlockSpec(blk, lambda p, me_ref: (p, 0, 0))],
            out_specs=pl.BlockSpec((rows, cols), lambda p, me_ref: (0, 0))),
        out_shape=jax.ShapeDtypeStruct((rows, cols), F32),
        compiler_params=_params("arbitrary"),
    )(me, grad, received)


def _exchange_in_chip(grads):
    n = len(grads)

    def body(*refs):
        ins, outs = refs[:n], refs[n:2 * n]
        send_sems, recv_sems = refs[2 * n:]
        x, y, c = _mesh_position()
        copies = []
        for a in range(n):
            for q in range(4):
                copies.append(pltpu.make_async_remote_copy(
                    src_ref=ins[a].at[2 * q + (1 - c)], dst_ref=outs[a].at[q],
                    send_sem=send_sems.at[a, q], recv_sem=recv_sems.at[a, q],
                    device_id=(x, y, 1 - c), device_id_type=MESH))
        for cp in copies:
            cp.start()
        for cp in copies:
            cp.wait_recv()
        for cp in copies:
            cp.wait_send()

    return pl.pallas_call(
        body, name="exchange_in_chip",
        in_specs=_hbm_specs(n), out_specs=_hbm_specs(n),
        out_shape=[jax.ShapeDtypeStruct((4,) + g.shape[1:], g.dtype) for g in grads],
        scratch_shapes=[pltpu.SemaphoreType.DMA((n, 4)), pltpu.SemaphoreType.DMA((n, 4))],
        compiler_params=pltpu.CompilerParams(has_side_effects=True),
    )(*grads)


def _exchange_between_chips(partials):
    n = len(partials)

    def body(*refs):
        ins, outs = refs[:n], refs[n:2 * n]
        send_sems, recv_sems = refs[2 * n:]
        x, y, c = _mesh_position()
        chips = [(1 - x, y), (x, 1 - y), (1 - x, 1 - y)]
        copies = []
        for a in range(n):
            for j, (px, py) in enumerate(chips):
                copies.append(pltpu.make_async_remote_copy(
                    src_ref=ins[a].at[2 * px + py], dst_ref=outs[a].at[j],
                    send_sem=send_sems.at[a, j], recv_sem=recv_sems.at[a, j],
                    device_id=(px, py, c), device_id_type=MESH))
        for cp in copies:
            cp.start()
        for cp in copies:
            cp.wait_recv()
        for cp in copies:
            cp.wait_send()

    return pl.pallas_call(
        body, name="exchange_between_chips",
        in_specs=_hbm_specs(n), out_specs=_hbm_specs(n),
        out_shape=[jax.ShapeDtypeStruct((3,) + p.shape[1:], p.dtype) for p in partials],
        scratch_shapes=[pltpu.SemaphoreType.DMA((n, 3)), pltpu.SemaphoreType.DMA((n, 3))],
        compiler_params=pltpu.CompilerParams(has_side_effects=True),
    )(*partials)


def _add_in_chip(grad, received, core, name):
    _, rows, cols = grad.shape

    def body(core_ref, g_ref, r_ref, o_ref):
        o_ref[...] = g_ref[...] + r_ref[...]

    blk = (1, rows, cols)
    return pl.pallas_call(
        body, name=name,
        grid_spec=pltpu.PrefetchScalarGridSpec(
            num_scalar_prefetch=1, grid=(4,),
            in_specs=[pl.BlockSpec(blk, lambda q, core_ref: (2 * q + core_ref[0], 0, 0)),
                      pl.BlockSpec(blk, lambda q, core_ref: (q, 0, 0))],
            out_specs=pl.BlockSpec(blk, lambda q, core_ref: (q, 0, 0))),
        out_shape=jax.ShapeDtypeStruct((4, rows, cols), F32),
        compiler_params=_params("parallel"),
    )(core, grad, received)


def _add_between_chips(partial, received, chip, name):
    _, rows, cols = partial.shape

    def body(chip_ref, p_ref, r_ref, o_ref):
        o_ref[...] = ((p_ref[0] + r_ref[0]) + r_ref[1]) + r_ref[2]

    return pl.pallas_call(
        body, name=name,
        grid_spec=pltpu.PrefetchScalarGridSpec(
            num_scalar_prefetch=1, grid=(1,),
            in_specs=[pl.BlockSpec((1, rows, cols), lambda i, chip_ref: (chip_ref[0], 0, 0)),
                      pl.BlockSpec((3, rows, cols), lambda i, chip_ref: (0, 0, 0))],
            out_specs=pl.BlockSpec((rows, cols), lambda i, chip_ref: (0, 0))),
        out_shape=jax.ShapeDtypeStruct((rows, cols), F32),
        compiler_params=_params("arbitrary"),
    )(chip, partial, received)


def _sum_devices(gathered):
    _, rows, cols = gathered.shape

    def body(g_ref, o_ref):
        total = g_ref[0]
        for d in range(1, N_DEV):
            total = total + g_ref[d]
        o_ref[...] = total

    return pl.pallas_call(
        body, name="sum_small_grads",
        out_shape=jax.ShapeDtypeStruct((rows, cols), F32),
        compiler_params=_params(),
    )(gathered)


def _adamw(w, g, m, v, name):
    rows, cols = w.shape
    tr = rows
    for cand in (256, 128, 64, 32, 16, 8):
        if rows > cand and rows % cand == 0:
            tr = cand
            break

    def body(w_ref, g_ref, m_ref, v_ref, delta_ref, newm_ref, newv_ref):
        g_v = g_ref[...]
        new_m = ADAM_B1 * m_ref[...] + (1.0 - ADAM_B1) * g_v
        new_v = ADAM_B2 * v_ref[...] + (1.0 - ADAM_B2) * (g_v * g_v)
        m_hat = new_m / (1.0 - ADAM_B1 ** ADAM_STEP)
        v_hat = new_v / (1.0 - ADAM_B2 ** ADAM_STEP)
        delta_ref[...] = -ADAM_LR * (m_hat / (jnp.sqrt(v_hat) + ADAM_EPS) + ADAM_WD * w_ref[...])
        newm_ref[...] = new_m
        newv_ref[...] = new_v

    blk = pl.BlockSpec((tr, cols), lambda i: (i, 0))
    shape = jax.ShapeDtypeStruct((rows, cols), F32)
    return pl.pallas_call(
        body, name=name, grid=(rows // tr,),
        in_specs=[blk] * 4, out_specs=[blk] * 3, out_shape=[shape] * 3,
        compiler_params=_params("parallel"),
    )(w, g, m, v)


def _pack(pieces, rows):
    flat = jnp.concatenate([p.reshape(-1) for p in pieces])
    return jnp.pad(flat, (0, rows * LANES - flat.shape[0])).reshape(rows, LANES)


def _unpack(packed, shapes):
    flat = packed.reshape(-1)
    out, pos = [], 0
    for shape in shapes:
        size = 1
        for s in shape:
            size *= s
        out.append(flat[pos:pos + size].reshape(shape))
        pos += size
    return out


def _rows_for(count):
    return -(-count // (8 * LANES)) * 8


SMALL = ("norm_mix_pre", "conv_dw_b", "conv_ln_g", "conv_ln_b", "rel_bias", "norm_mix_post", "norm_ffn_pre",
         "ffn_dw_b", "norm_ffn_post")
SHARDED_SMALL = ("conv_dw_w", "ffn_dw_w")
LARGE = ("w_in", "w_out", "w_up", "w_down")
WEIGHTS = ("norm_mix_pre", "w_in", "conv_dw_w", "conv_dw_b", "conv_ln_g", "conv_ln_b", "rel_bias", "w_out",
           "norm_mix_post", "norm_ffn_pre", "w_up", "ffn_dw_w", "ffn_dw_b", "w_down", "norm_ffn_post")


def kernel(x, norm_mix_pre, w_in, conv_dw_w, conv_dw_b, conv_ln_g, conv_ln_b, rel_bias, w_out, norm_mix_post, norm_ffn_pre, w_up, ffn_dw_w, ffn_dw_b, w_down, norm_ffn_post, loss_target, m_norm_mix_pre, m_w_in, m_conv_dw_w, m_conv_dw_b, m_conv_ln_g, m_conv_ln_b, m_rel_bias, m_w_out, m_norm_mix_post, m_norm_ffn_pre, m_w_up, m_ffn_dw_w, m_ffn_dw_b, m_w_down, m_norm_ffn_post, v_norm_mix_pre, v_w_in, v_conv_dw_w, v_conv_dw_b, v_conv_ln_g, v_conv_ln_b, v_rel_bias, v_w_out, v_norm_mix_post, v_norm_ffn_pre, v_w_up, v_ffn_dw_w, v_ffn_dw_b, v_w_down, v_norm_ffn_post):
    weights = dict(norm_mix_pre=norm_mix_pre, w_in=w_in, conv_dw_w=conv_dw_w, conv_dw_b=conv_dw_b, conv_ln_g=conv_ln_g,
                   conv_ln_b=conv_ln_b, rel_bias=rel_bias, w_out=w_out, norm_mix_post=norm_mix_post,
                   norm_ffn_pre=norm_ffn_pre, w_up=w_up, ffn_dw_w=ffn_dw_w, ffn_dw_b=ffn_dw_b, w_down=w_down,
                   norm_ffn_post=norm_ffn_post)
    mom1 = dict(norm_mix_pre=m_norm_mix_pre, w_in=m_w_in, conv_dw_w=m_conv_dw_w, conv_dw_b=m_conv_dw_b,
                conv_ln_g=m_conv_ln_g, conv_ln_b=m_conv_ln_b, rel_bias=m_rel_bias, w_out=m_w_out,
                norm_mix_post=m_norm_mix_post, norm_ffn_pre=m_norm_ffn_pre, w_up=m_w_up, ffn_dw_w=m_ffn_dw_w,
                ffn_dw_b=m_ffn_dw_b, w_down=m_w_down, norm_ffn_post=m_norm_ffn_post)
    mom2 = dict(norm_mix_pre=v_norm_mix_pre, w_in=v_w_in, conv_dw_w=v_conv_dw_w, conv_dw_b=v_conv_dw_b,
                conv_ln_g=v_conv_ln_g, conv_ln_b=v_conv_ln_b, rel_bias=v_rel_bias, w_out=v_w_out,
                norm_mix_post=v_norm_mix_post, norm_ffn_pre=v_norm_ffn_pre, w_up=v_w_up, ffn_dw_w=v_ffn_dw_w,
                ffn_dw_b=v_ffn_dw_b, w_down=v_w_down, norm_ffn_post=v_norm_ffn_post)

    x2 = x[0]
    target = loss_target[0]
    t_rows = x2.shape[0]
    d = D_MODEL
    in_cols = 2 * CONV_WIDTH + 3 * ATTN_WIDTH
    my_x, my_y, my_c = _mesh_position()
    my_dev = 4 * my_x + 2 * my_y + my_c

    small_conv = _pack([conv_dw_w[0], ffn_dw_w[0]], 32)
    me = jnp.reshape(my_dev, (1,)).astype(jnp.int32)
    first_shards = [w_in[0].T.astype(BF16), small_conv]
    late_shards = [w_out[0].astype(BF16), w_up[0].T.astype(BF16), w_down[0].astype(BF16)]
    placed = _place_own(first_shards + late_shards, me)
    win_t, conv_g = _all_gather(first_shards, "all_gather_weights", placed=placed[:2])
    late_lands = placed[2:]
    late_send, late_recv, late_shards, late_lands, late_token = _scatter_start(
        late_shards, late_lands, True, "gather_late_weights_start", after=win_t)
    win_t = win_t.reshape(in_cols, d)
    conv_flat = conv_g.reshape(N_DEV, 32 * LANES)
    n_cw = CONV_K * (CONV_WIDTH // N_DEV)
    conv_w_full = conv_flat[:, :n_cw].reshape(N_DEV, CONV_K, CONV_WIDTH // N_DEV).transpose(1, 0, 2).reshape(CONV_K, CONV_WIDTH)
    ffn_w_full = conv_flat[:, n_cw:].reshape(N_DEV, FFN_K, 2 * D_FF // N_DEV).transpose(1, 0, 2).reshape(FFN_K, 2 * D_FF)

    u1 = _pre_norm(x2, norm_mix_pre + late_token[0:1, 0:1], "pre_norm_mix")
    proj_a = _matmul(u1, win_t, mode="nt", m=t_rows, n=2 * CONV_WIDTH, k=d, tm=2048, tn=1024, tk=d,
                     out_dtype=F32, name="proj_conv")
    qkv_t = _matmul(win_t, u1, mode="nt", m=3 * ATTN_WIDTH, n=t_rows, k=d, tm=512, tn=2048, tk=d,
                    out_dtype=BF16, name="proj_qkv", a_m0=2 * CONV_WIDTH)
    conv_c, conv_out = _conv_forward(proj_a, conv_w_full, conv_dw_b, conv_ln_g, conv_ln_b)
    o_t, attn_lse = _attn_forward(qkv_t, rel_bias[0])
    _, (wout_g, wup_t, wdown_g) = _scatter_wait(late_send, late_recv, late_shards, late_lands, o_t, True,
                                                "gather_late_weights_wait")
    wout_g = wout_g.reshape(d, d)
    wup_t = wup_t.reshape(2 * D_FF, d)
    wdown_g = wdown_g.reshape(D_FF, d)
    mixed = _matmul(conv_out, wout_g, mode="nn", m=t_rows, n=d, k=CONV_WIDTH, tm=2048, tn=1024, tk=CONV_WIDTH,
                    out_dtype=F32, name="out_proj_conv")
    mixed = _matmul(o_t, wout_g, mode="tn", m=t_rows, n=d, k=ATTN_WIDTH, tm=1024, tn=1024, tk=ATTN_WIDTH,
                    out_dtype=F32, name="out_proj_attn", b_k0=CONV_WIDTH, add=mixed)
    h1, u2 = _mid_forward(x2, mixed, norm_mix_post, norm_ffn_pre)
    hup = _matmul(u2, wup_t, mode="nt", m=t_rows, n=2 * D_FF, k=d, tm=2048, tn=1408, tk=d,
                  out_dtype=F32, name="ffn_up")
    act, ffn_gel, ffn_slope = _ffn_activation(hup, ffn_w_full, ffn_dw_b)
    f = _matmul(act, wdown_g, mode="nn", m=t_rows, n=d, k=D_FF, tm=1024, tn=1024, tk=D_FF,
                out_dtype=F32, name="ffn_down")
    loss, dy, df, d_norm_ffn_post = _loss_and_head_backward(h1, f, target, norm_ffn_post)

    dact = _matmul(df, wdown_g, mode="nt", m=t_rows, n=D_FF, k=d, tm=2048, tn=1408, tk=d,
                   out_dtype=F32, name="ffn_down_dx")
    g_wdown = _matmul(act, df, mode="tn", m=D_FF, n=d, k=t_rows, tm=1408, tn=1024, tk=2048,
                      out_dtype=F32, name="ffn_down_dw")
    dhg, dhv, dwg, dwv, dbg, dbv = _ffn_backward(dact, ffn_gel, ffn_slope, hup, ffn_w_full)
    du2 = _matmul(dhg, wup_t, mode="nn", m=t_rows, n=d, k=D_FF, tm=1024, tn=1024, tk=D_FF,
                  out_dtype=F32, name="ffn_up_dx_gate")
    du2 = _matmul(dhv, wup_t, mode="nn", m=t_rows, n=d, k=D_FF, tm=1024, tn=1024, tk=D_FF,
                  out_dtype=F32, name="ffn_up_dx_value", b_k0=D_FF, add=du2)
    g_wup_t = _matmul(dhg, u2, mode="tn", m=D_FF, n=d, k=t_rows, tm=1408, tn=1024, tk=2048, out_dtype=F32,
                      name="ffn_up_dw_gate", out_rows=2 * D_FF)
    g_wup_t = _matmul(dhv, u2, mode="tn", m=D_FF, n=d, k=t_rows, tm=1408, tn=1024, tk=2048, out_dtype=F32,
                      name="ffn_up_dw_value", out_rows=2 * D_FF, out_m0=D_FF, into=g_wup_t)
    ffn_grads = [g_wup_t.reshape(N_DEV, 2 * D_FF // N_DEV, d), g_wdown.reshape(N_DEV, D_FF // N_DEV, d)]
    red_send, red_recv, ffn_grads, red_lands, red_token = _scatter_start(
        ffn_grads, [lax.empty(g.shape, F32) for g in ffn_grads], False, "reduce_ffn_grads_start")
    dh1, dmixed, d_norm_ffn_pre, d_norm_mix_post = _mid_backward(
        dy, du2, h1, mixed, norm_ffn_pre + red_token[0:1, 0:1], norm_mix_post)
    dconv_out = _matmul(dmixed, wout_g, mode="nt", m=t_rows, n=CONV_WIDTH, k=d, tm=2048, tn=512, tk=d,
                        out_dtype=F32, name="out_proj_dx_conv")
    do_t = _matmul(wout_g, dmixed, mode="nt", m=ATTN_WIDTH, n=t_rows, k=d, tm=512, tn=2048, tk=d,
                   out_dtype=BF16, name="out_proj_dx_attn", a_m0=CONV_WIDTH)
    g_wout = _matmul(conv_out, dmixed, mode="tn", m=CONV_WIDTH, n=d, k=t_rows, tm=512, tn=1024, tk=2048, out_dtype=F32,
                     name="out_proj_dw_conv", out_rows=d)
    g_wout = _matmul(o_t, dmixed, mode="nn", m=ATTN_WIDTH, n=d, k=t_rows, tm=512, tn=1024, tk=2048, out_dtype=F32,
                     name="out_proj_dw_attn", out_rows=d, out_m0=CONV_WIDTH, into=g_wout)
    wout_handle = _scatter_start([g_wout.reshape(N_DEV, d // N_DEV, d)], [lax.empty((N_DEV, d // N_DEV, d), F32)],
                                 False, "reduce_w_out_grad_start")
    dproj_a, d_conv_w, d_conv_b, d_ln_g, d_ln_b = _conv_backward(
        dconv_out, conv_c, proj_a, conv_w_full, conv_ln_g + wout_handle[4][0:1, 0:1], conv_ln_b)
    dqkv_parts = _attn_backward(qkv_t, o_t, do_t, attn_lse, rel_bias[0])
    drel = dqkv_parts[3]
    du1 = _matmul(dproj_a, win_t, mode="nn", m=t_rows, n=d, k=2 * CONV_WIDTH, tm=2048, tn=1024, tk=1024,
                  out_dtype=F32, name="proj_dx_conv")
    g_win_t = _matmul(dproj_a, u1, mode="tn", m=2 * CONV_WIDTH, n=d, k=t_rows, tm=1024, tn=1024, tk=2048, out_dtype=F32,
                      name="proj_dw_conv", out_rows=in_cols)
    for j, part in enumerate("qkv"):
        row0 = 2 * CONV_WIDTH + j * ATTN_WIDTH
        du1 = _matmul(dqkv_parts[j], win_t, mode="tn", m=t_rows, n=d, k=ATTN_WIDTH, tm=1024, tn=1024, tk=ATTN_WIDTH,
                      out_dtype=F32, name="proj_dx_" + part, b_k0=row0, add=du1)
        g_win_t = _matmul(dqkv_parts[j], u1, mode="nn", m=ATTN_WIDTH, n=d, k=t_rows, tm=512, tn=1024, tk=2048,
                          out_dtype=F32, name="proj_dw_" + part, out_rows=in_cols, out_m0=row0, into=g_win_t)
    win_handle = _scatter_start([g_win_t.reshape(N_DEV, in_cols // N_DEV, d)],
                                [lax.empty((N_DEV, in_cols // N_DEV, d), F32)], False, "reduce_w_in_grad_start")
    dx, d_norm_mix_pre = _input_backward(dh1, du1, x2, norm_mix_pre + win_handle[4][0:1, 0:1])

    small_grads = dict(norm_mix_pre=d_norm_mix_pre, conv_dw_b=d_conv_b, conv_ln_g=d_ln_g, conv_ln_b=d_ln_b,
                       rel_bias=drel[:, :2 * MAX_REL + 1], norm_mix_post=d_norm_mix_post, norm_ffn_pre=d_norm_ffn_pre,
                       ffn_dw_b=jnp.concatenate([dbg, dbv], axis=1), norm_ffn_post=d_norm_ffn_post)
    pieces = [small_grads[nm] for nm in SMALL] + [d_conv_w, jnp.concatenate([dwg, dwv], axis=1), loss]
    count = sum(p.size for p in pieces)
    (gathered_small,) = _all_gather([_pack(pieces, _rows_for(count))], "all_gather_small_grads")
    summed = _sum_devices(gathered_small)
    shapes = [weights[nm].shape for nm in SMALL] + [(CONV_K, CONV_WIDTH), (FFN_K, 2 * D_FF), (1, 1)]
    unpacked = _unpack(summed, shapes)
    grads = dict(zip(SMALL, unpacked[:len(SMALL)]))
    cw_shard, fw_shard = CONV_WIDTH // N_DEV, 2 * D_FF // N_DEV
    grads["conv_dw_w"] = lax.dynamic_slice_in_dim(unpacked[-3], my_dev * cw_shard, cw_shard, axis=1)[None]
    grads["ffn_dw_w"] = lax.dynamic_slice_in_dim(unpacked[-2], my_dev * fw_shard, fw_shard, axis=1)[None]
    total_loss = unpacked[-1].reshape(())

    me = jnp.reshape(my_dev, (1,)).astype(jnp.int32)
    delta, new_m, new_v = {}, {}, {}

    def finish(nm, send, recv, srcs, lands, after, transposed):
        srcs, lands = _scatter_wait(send, recv, srcs, lands, after, False, "reduce_" + nm + "_grad_wait")
        for name_a, src, land in zip(nm.split("_and_"), srcs, lands):
            g = _sum_received(src, land, me, "sum_received_" + name_a)
            grads[name_a] = (g.T if transposed[name_a] else g)[None]
            dl, nm1, nv1 = _adamw(weights[name_a][0], grads[name_a][0], mom1[name_a][0], mom2[name_a][0], "adamw_" + name_a)
            delta[name_a], new_m[name_a], new_v[name_a] = dl[None], nm1[None], nv1[None]

    transposed = dict(w_in=True, w_out=False, w_up=True, w_down=False)
    finish("w_up_and_w_down", red_send, red_recv, ffn_grads, red_lands, dx, transposed)
    finish("w_out", *wout_handle[:4], dx, transposed)
    finish("w_in", *win_handle[:4], delta["w_up"], transposed)
    small_names = SMALL + SHARDED_SMALL
    small_count = sum(weights[nm].size for nm in small_names)
    small_rows = _rows_for(small_count)
    packed = [_pack([src[nm] for nm in small_names], small_rows) for src in (weights, grads, mom1, mom2)]
    outs = _adamw(*packed, "adamw_small")
    small_shapes = [weights[nm].shape for nm in small_names]
    for store, arr in zip((delta, new_m, new_v), outs):
        store.update(zip(small_names, _unpack(arr, small_shapes)))

    return (total_loss, dx[None], *[grads[nm] for nm in WEIGHTS], *[delta[nm] for nm in WEIGHTS],
            *[new_m[nm] for nm in WEIGHTS], *[new_v[nm] for nm in WEIGHTS])
```

```python
import jax
import jax.numpy as jnp
from jax import lax
from jax.experimental import pallas as pl
from jax.experimental.pallas import tpu as pltpu

F32 = jnp.float32
BF16 = jnp.bfloat16
MESH = pl.DeviceIdType.MESH
AXES = ("x", "y", "c")
N_DEV = 8

EPS = 1e-6
NEG_INF = -1e30
D_MODEL = 1024
CONV_WIDTH = 512
ATTN_WIDTH = 512
N_HEADS = 8
HEAD_DIM = 64
CHUNK = 64
LEFT = 8 * CHUNK
QBLK = 2 * CHUNK
WIN = LEFT + QBLK
CONV_K = 31
CONV_HALO = 32
FFN_K = 3
FFN_HALO = 8
D_FF = 2816
MAX_REL = 128
SCALE = HEAD_DIM ** -0.5
ADAM_LR, ADAM_B1, ADAM_B2, ADAM_EPS, ADAM_WD, ADAM_STEP = 0.001, 0.9, 0.999, 1e-08, 0.01, 10

V7X_VMEM_BYTES = 64 * 2**20
VMEM_LIMIT_BYTES = V7X_VMEM_BYTES - 8 * 2**20
LANES = 128


def _params(*sem):
    return pltpu.CompilerParams(dimension_semantics=sem or None, vmem_limit_bytes=VMEM_LIMIT_BYTES)


_DOT_DIMS = {"nn": (((1,), (0,)), ((), ())), "nt": (((1,), (1,)), ((), ())), "tn": (((0,), (0,)), ((), ()))}


def _matmul(a, b, *, mode, m, n, k, tm, tn, tk, out_dtype, name, a_m0=0, b_n0=0, b_k0=0, add=None,
            out_rows=None, out_m0=0, into=None):
    tm, tn, tk = min(tm, m), min(tn, n), min(tk, k)
    out_rows = m if out_rows is None else out_rows
    assert m % tm == 0 and n % tn == 0 and k % tk == 0, (name, m, n, k, tm, tn, tk)
    assert a_m0 % tm == 0 and b_n0 % tn == 0 and b_k0 % tk == 0 and out_m0 % tm == 0, name
    am, bn, bk, om = a_m0 // tm, b_n0 // tn, b_k0 // tk, out_m0 // tm
    gk = k // tk
    dims = _DOT_DIMS[mode]

    if mode == "tn":
        a_spec = pl.BlockSpec((tk, tm), lambda i, j, kk: (kk, i + am))
    else:
        a_spec = pl.BlockSpec((tm, tk), lambda i, j, kk: (i + am, kk))
    if mode == "nt":
        b_spec = pl.BlockSpec((tn, tk), lambda i, j, kk: (j + bn, kk + bk))
    else:
        b_spec = pl.BlockSpec((tk, tn), lambda i, j, kk: (kk + bk, j + bn))
    o_spec = pl.BlockSpec((tm, tn), lambda i, j, kk: (i + om, j))
    in_specs = [a_spec, b_spec]
    operands = [a, b]
    if add is not None:
        assert out_rows == m
        in_specs.append(o_spec)
        operands.append(add)
    aliases = {}
    if into is not None:
        aliases = {len(operands): 0}
        in_specs.append(pl.BlockSpec(memory_space=pl.ANY))
        operands.append(into)

    def body(*refs):
        a_ref, b_ref = refs[0], refs[1]
        add_ref = refs[2] if add is not None else None
        o_ref = refs[len(operands)]
        part = lax.dot_general(a_ref[...].astype(BF16), b_ref[...].astype(BF16), dims,
                               preferred_element_type=F32)

        def finish(total):
            if add_ref is not None:
                total = total + add_ref[...]
            o_ref[...] = total.astype(out_dtype)

        if gk == 1:
            finish(part)
        else:
            acc_ref = refs[-1]
            kk = pl.program_id(2)

            @pl.when(kk == 0)
            def _():
                acc_ref[...] = part

            @pl.when(kk > 0)
            def _():
                acc_ref[...] += part

            @pl.when(kk == gk - 1)
            def _():
                finish(acc_ref[...])

    return pl.pallas_call(
        body, name=name,
        grid=(m // tm, n // tn, gk),
        in_specs=in_specs, out_specs=o_spec,
        out_shape=jax.ShapeDtypeStruct((out_rows, n), out_dtype),
        scratch_shapes=[pltpu.VMEM((tm, tn), F32)] if gk > 1 else [],
        input_output_aliases=aliases,
        compiler_params=_params("parallel", "parallel", "arbitrary"),
    )(*operands)


def _matmul_sum(pieces, b, *, m, n, tm, name):
    tm = min(tm, m)
    assert m % tm == 0
    in_specs, operands = [], []
    for a, mode, k, k0 in pieces:
        assert k0 % k == 0
        if mode == "tn":
            in_specs.append(pl.BlockSpec((k, tm), lambda i: (0, i)))
        else:
            in_specs.append(pl.BlockSpec((tm, k), lambda i: (i, 0)))
        in_specs.append(pl.BlockSpec((k, n), lambda i, blk=k0 // k: (blk, 0)))
        operands += [a, b]

    def body(*refs):
        total = None
        for p, (_, mode, _, _) in enumerate(pieces):
            part = lax.dot_general(refs[2 * p][...], refs[2 * p + 1][...], _DOT_DIMS[mode], preferred_element_type=F32)
            total = part if total is None else total + part
        refs[-1][...] = total

    return pl.pallas_call(
        body, name=name, grid=(m // tm,),
        in_specs=in_specs, out_specs=pl.BlockSpec((tm, n), lambda i: (i, 0)),
        out_shape=jax.ShapeDtypeStruct((m, n), F32),
        compiler_params=_params("parallel"),
    )(*operands)


def _rms_hat(v):
    r = lax.rsqrt(jnp.mean(v * v, axis=-1, keepdims=True) + EPS)
    return v * r, r


def _rms_bwd(dn, hat, r):
    return r * (dn - hat * jnp.mean(dn * hat, axis=-1, keepdims=True))


def _sigmoid(v):
    return 1.0 / (1.0 + jnp.exp(-v))


_GELU_C = 0.7978845608028654


def _gelu(v):
    return 0.5 * v * (1.0 + jnp.tanh(_GELU_C * (v + 0.044715 * v * (v * v))))


def _gelu_parts(v):
    v2 = v * v
    t = jnp.tanh(_GELU_C * (v + 0.044715 * v * v2))
    cdf = 0.5 * (1.0 + t)
    dcdf = 0.5 * (1.0 - t * t) * _GELU_C * (1.0 + 3.0 * 0.044715 * v2)
    return v * cdf, cdf + v * dcdf


def _row_tile(t_rows, want):
    tile = min(want, t_rows)
    assert t_rows % tile == 0
    return tile


def _pre_norm(x, g, name):
    t_rows, d = x.shape
    tm = _row_tile(t_rows, 512)

    def body(x_ref, g_ref, u_ref):
        hat, _ = _rms_hat(x_ref[...])
        u_ref[...] = (hat * g_ref[...]).astype(BF16)

    return pl.pallas_call(
        body, name=name, grid=(t_rows // tm,),
        in_specs=[pl.BlockSpec((tm, d), lambda i: (i, 0)), pl.BlockSpec((1, d), lambda i: (0, 0))],
        out_specs=pl.BlockSpec((tm, d), lambda i: (i, 0)),
        out_shape=jax.ShapeDtypeStruct((t_rows, d), BF16),
        compiler_params=_params("parallel"),
    )(x, g)


def _mid_forward(x, mixed, g_post, g_pre):
    t_rows, d = x.shape
    tm = _row_tile(t_rows, 512)

    def body(x_ref, mixed_ref, gpost_ref, gpre_ref, h1_ref, u2_ref):
        hat, _ = _rms_hat(mixed_ref[...])
        h1 = x_ref[...] + hat * gpost_ref[...]
        h1_ref[...] = h1
        hat1, _ = _rms_hat(h1)
        u2_ref[...] = (hat1 * gpre_ref[...]).astype(BF16)

    row = pl.BlockSpec((tm, d), lambda i: (i, 0))
    vec = pl.BlockSpec((1, d), lambda i: (0, 0))
    return pl.pallas_call(
        body, name="mid_forward", grid=(t_rows // tm,),
        in_specs=[row, row, vec, vec], out_specs=[row, row],
        out_shape=[jax.ShapeDtypeStruct((t_rows, d), F32), jax.ShapeDtypeStruct((t_rows, d), BF16)],
        compiler_params=_params("parallel"),
    )(x, mixed, g_post, g_pre)


def _loss_and_head_backward(h1, f, target, g_post):
    t_rows, d = h1.shape
    tm = _row_tile(t_rows, 512)
    nt = t_rows // tm

    def body(h1_ref, f_ref, tgt_ref, g_ref, loss_ref, dy_ref, df_ref, dg_ref, sq_ref):
        i = pl.program_id(0)

        @pl.when(i == 0)
        def _():
            sq_ref[...] = jnp.zeros_like(sq_ref)
            dg_ref[...] = jnp.zeros_like(dg_ref)

        g = g_ref[...]
        hat, r = _rms_hat(f_ref[...])
        err = h1_ref[...] + hat * g - tgt_ref[...]
        sq_ref[...] += jnp.sum(err * err, axis=0, keepdims=True)
        dy = err * (1.0 / d)
        dy_ref[...] = dy
        dg_ref[...] += jnp.sum(dy * hat, axis=0, keepdims=True)
        df_ref[...] = _rms_bwd(dy * g, hat, r).astype(BF16)

        @pl.when(i == nt - 1)
        def _():
            loss_ref[...] = (0.5 / d) * jnp.sum(sq_ref[...], axis=1, keepdims=True)

    row = pl.BlockSpec((tm, d), lambda i: (i, 0))
    vec = pl.BlockSpec((1, d), lambda i: (0, 0))
    return pl.pallas_call(
        body, name="loss_head_backward", grid=(nt,),
        in_specs=[row, row, row, vec],
        out_specs=[pl.BlockSpec((1, 1), lambda i: (0, 0)), row, row, vec],
        out_shape=[jax.ShapeDtypeStruct((1, 1), F32), jax.ShapeDtypeStruct((t_rows, d), F32),
                   jax.ShapeDtypeStruct((t_rows, d), BF16), jax.ShapeDtypeStruct((1, d), F32)],
        scratch_shapes=[pltpu.VMEM((1, d), F32)],
        compiler_params=_params("arbitrary"),
    )(h1, f, target, g_post)


def _mid_backward(dy, du2, h1, mixed, g_pre, g_post):
    t_rows, d = dy.shape
    tm = _row_tile(t_rows, 512)

    def body(dy_ref, du2_ref, h1_ref, mixed_ref, gpre_ref, gpost_ref, dh1_ref, dmixed_ref, dgpre_ref, dgpost_ref):
        @pl.when(pl.program_id(0) == 0)
        def _():
            dgpre_ref[...] = jnp.zeros_like(dgpre_ref)
            dgpost_ref[...] = jnp.zeros_like(dgpost_ref)

        du2 = du2_ref[...]
        hat1, r1 = _rms_hat(h1_ref[...])
        dgpre_ref[...] += jnp.sum(du2 * hat1, axis=0, keepdims=True)
        dh1 = dy_ref[...] + _rms_bwd(du2 * gpre_ref[...], hat1, r1)
        dh1_ref[...] = dh1
        hatm, rm = _rms_hat(mixed_ref[...])
        dgpost_ref[...] += jnp.sum(dh1 * hatm, axis=0, keepdims=True)
        dmixed_ref[...] = _rms_bwd(dh1 * gpost_ref[...], hatm, rm).astype(BF16)

    row = pl.BlockSpec((tm, d), lambda i: (i, 0))
    vec = pl.BlockSpec((1, d), lambda i: (0, 0))
    return pl.pallas_call(
        body, name="mid_backward", grid=(t_rows // tm,),
        in_specs=[row, row, row, row, vec, vec], out_specs=[row, row, vec, vec],
        out_shape=[jax.ShapeDtypeStruct((t_rows, d), F32), jax.ShapeDtypeStruct((t_rows, d), BF16),
                   jax.ShapeDtypeStruct((1, d), F32), jax.ShapeDtypeStruct((1, d), F32)],
        compiler_params=_params("arbitrary"),
    )(dy, du2, h1, mixed, g_pre, g_post)


def _input_backward(dh1, du1, x, g_pre):
    t_rows, d = x.shape
    tm = _row_tile(t_rows, 512)

    def body(dh1_ref, du1_ref, x_ref, g_ref, dx_ref, dg_ref):
        @pl.when(pl.program_id(0) == 0)
        def _():
            dg_ref[...] = jnp.zeros_like(dg_ref)

        du1 = du1_ref[...]
        hat, r = _rms_hat(x_ref[...])
        dg_ref[...] += jnp.sum(du1 * hat, axis=0, keepdims=True)
        dx_ref[...] = dh1_ref[...] + _rms_bwd(du1 * g_ref[...], hat, r)

    row = pl.BlockSpec((tm, d), lambda i: (i, 0))
    vec = pl.BlockSpec((1, d), lambda i: (0, 0))
    return pl.pallas_call(
        body, name="input_backward", grid=(t_rows // tm,),
        in_specs=[row, row, row, vec], out_specs=[row, vec],
        out_shape=[jax.ShapeDtypeStruct((t_rows, d), F32), jax.ShapeDtypeStruct((1, d), F32)],
        compiler_params=_params("arbitrary"),
    )(dh1, du1, x, g_pre)


CONV_STRIP = 32


def _glu(block):
    return block[:, :CONV_WIDTH] * _sigmoid(block[:, CONV_WIDTH:])


def _layer_norm_parts(c):
    mu = jnp.mean(c, axis=-1, keepdims=True)
    xc = c - mu
    r = lax.rsqrt(jnp.mean(xc * xc, axis=-1, keepdims=True) + EPS)
    return xc * r, r


CONV_WINDOW = 2 * CONV_STRIP
SHIFTED_ROWS = CONV_WINDOW - 8


def _shifted_copies(v, shifted):
    for s in range(1, 8):
        shifted[s] = v[s:s + SHIFTED_ROWS, :]


def _window_rows(v, shifted, start):
    s, a = start % 8, start - start % 8
    return v[a:a + CONV_STRIP, :] if s == 0 else shifted[s, a:a + CONV_STRIP, :]


def _conv_forward(proj_a, w, b, ln_g, ln_b):
    t_rows = proj_a.shape[0]
    tm = _row_tile(t_rows, 512)
    hb = tm // CONV_HALO
    cw = CONV_WIDTH

    def body(cur_ref, prev_ref, w_ref, b_ref, g_ref, beta_ref, c_ref, out_ref, hbuf, shifted):
        i = pl.program_id(0)
        hbuf[0:CONV_HALO, :] = jnp.where(i > 0, _glu(prev_ref[...]), 0.0)
        hbuf[CONV_HALO:, :] = _glu(cur_ref[...])

        def strip(s, carry):
            base = pl.multiple_of(s * CONV_STRIP, CONV_STRIP)
            v = hbuf[pl.ds(base, CONV_WINDOW), :]
            _shifted_copies(v, shifted)
            acc = jnp.broadcast_to(b_ref[...], (CONV_STRIP, cw))
            off = CONV_HALO - (CONV_K - 1)
            for kk in range(CONV_K):
                acc = acc + w_ref[kk:kk + 1, :] * _window_rows(v, shifted, off + kk)
            c_ref[pl.ds(base, CONV_STRIP), :] = acc
            hat, _ = _layer_norm_parts(acc)
            z = hat * g_ref[...] + beta_ref[...]
            out_ref[pl.ds(base, CONV_STRIP), :] = (z * _sigmoid(z)).astype(BF16)
            return carry

        lax.fori_loop(0, tm // CONV_STRIP, strip, 0)

    vec = pl.BlockSpec((1, cw), lambda i: (0, 0))
    return pl.pallas_call(
        body, name="conv_forward", grid=(t_rows // tm,),
        in_specs=[pl.BlockSpec((tm, 2 * cw), lambda i: (i, 0)),
                  pl.BlockSpec((CONV_HALO, 2 * cw), lambda i: (jnp.maximum(i * hb - 1, 0), 0)),
                  pl.BlockSpec((CONV_K, cw), lambda i: (0, 0)), vec, vec, vec],
        out_specs=[pl.BlockSpec((tm, cw), lambda i: (i, 0)), pl.BlockSpec((tm, cw), lambda i: (i, 0))],
        out_shape=[jax.ShapeDtypeStruct((t_rows, cw), F32), jax.ShapeDtypeStruct((t_rows, cw), BF16)],
        scratch_shapes=[pltpu.VMEM((tm + CONV_HALO, cw), F32), pltpu.VMEM((8, SHIFTED_ROWS, cw), F32)],
        compiler_params=_params("parallel"),
    )(proj_a, proj_a, w, b, ln_g, ln_b)


def _conv_backward(dout, c, proj_a, w, ln_g, ln_b):
    t_rows = c.shape[0]
    tm = _row_tile(t_rows, 512)
    hb = tm // CONV_HALO
    nt = t_rows // tm
    last_halo = t_rows // CONV_HALO - 1
    cw = CONV_WIDTH

    def body(dout_ref, dout_next_ref, c_ref, c_next_ref, cur_ref, prev_ref, w_ref, g_ref, beta_ref,
             dproj_ref, dw_ref, db_ref, dg_ref, dbeta_ref, hbuf, dcbuf, dwacc, h_shifted, d_shifted):
        i = pl.program_id(0)

        @pl.when(i == 0)
        def _():
            dwacc[...] = jnp.zeros_like(dwacc)
            db_ref[...] = jnp.zeros_like(db_ref)
            dg_ref[...] = jnp.zeros_like(dg_ref)
            dbeta_ref[...] = jnp.zeros_like(dbeta_ref)

        def ln_swish_backward(dout_v, c_v):
            hat, r = _layer_norm_parts(c_v)
            g = g_ref[...]
            z = hat * g + beta_ref[...]
            sg = _sigmoid(z)
            dz = dout_v * (sg * (1.0 + z * (1.0 - sg)))
            dhat = dz * g
            dc = r * (dhat - jnp.mean(dhat, axis=-1, keepdims=True)
                      - hat * jnp.mean(dhat * hat, axis=-1, keepdims=True))
            return dc, dz, hat

        dc, dz, hat = ln_swish_backward(dout_ref[...], c_ref[...])
        dg_ref[...] += jnp.sum(dz * hat, axis=0, keepdims=True)
        dbeta_ref[...] += jnp.sum(dz, axis=0, keepdims=True)
        db_ref[...] += jnp.sum(dc, axis=0, keepdims=True)
        dcbuf[0:tm, :] = dc
        dc_next, _, _ = ln_swish_backward(dout_next_ref[...], c_next_ref[...])
        dcbuf[tm:, :] = jnp.where(i < nt - 1, dc_next, 0.0)

        hbuf[0:CONV_HALO, :] = jnp.where(i > 0, _glu(prev_ref[...]), 0.0)
        hbuf[CONV_HALO:, :] = _glu(cur_ref[...])

        def strip(s, carry):
            base = pl.multiple_of(s * CONV_STRIP, CONV_STRIP)
            dv = dcbuf[pl.ds(base, CONV_WINDOW), :]
            hv = hbuf[pl.ds(base, CONV_WINDOW), :]
            _shifted_copies(dv, d_shifted)
            _shifted_copies(hv, h_shifted)
            dcs = dv[0:CONV_STRIP, :]
            dh = jnp.zeros((CONV_STRIP, cw), F32)
            off = CONV_HALO - (CONV_K - 1)
            for kk in range(CONV_K):
                back = CONV_K - 1 - kk
                dh = dh + w_ref[kk:kk + 1, :] * _window_rows(dv, d_shifted, back)
                prod = dcs * _window_rows(hv, h_shifted, off + kk)
                dwacc[kk] += jnp.sum(prod.reshape(CONV_STRIP // 8, 8, cw), axis=0)
            blk = cur_ref[pl.ds(base, CONV_STRIP), :]
            val, sg = blk[:, :cw], _sigmoid(blk[:, cw:])
            dproj_ref[pl.ds(base, CONV_STRIP), 0:cw] = (dh * sg).astype(BF16)
            dproj_ref[pl.ds(base, CONV_STRIP), cw:2 * cw] = (dh * val * sg * (1.0 - sg)).astype(BF16)
            return carry

        lax.fori_loop(0, tm // CONV_STRIP, strip, 0)

        @pl.when(i == nt - 1)
        def _():
            for kk in range(CONV_K):
                dw_ref[kk:kk + 1, :] = jnp.sum(dwacc[kk], axis=0, keepdims=True)

    vec = pl.BlockSpec((1, cw), lambda i: (0, 0))
    cur = lambda width: pl.BlockSpec((tm, width), lambda i: (i, 0))
    nxt = lambda width: pl.BlockSpec((CONV_HALO, width), lambda i: (jnp.minimum((i + 1) * hb, last_halo), 0))
    return pl.pallas_call(
        body, name="conv_backward", grid=(nt,),
        in_specs=[cur(cw), nxt(cw), cur(cw), nxt(cw), cur(2 * cw),
                  pl.BlockSpec((CONV_HALO, 2 * cw), lambda i: (jnp.maximum(i * hb - 1, 0), 0)),
                  pl.BlockSpec((CONV_K, cw), lambda i: (0, 0)), vec, vec],
        out_specs=[cur(2 * cw), pl.BlockSpec((CONV_K, cw), lambda i: (0, 0)), vec, vec, vec],
        out_shape=[jax.ShapeDtypeStruct((t_rows, 2 * cw), BF16), jax.ShapeDtypeStruct((CONV_K, cw), F32),
                   jax.ShapeDtypeStruct((1, cw), F32), jax.ShapeDtypeStruct((1, cw), F32),
                   jax.ShapeDtypeStruct((1, cw), F32)],
        scratch_shapes=[pltpu.VMEM((tm + CONV_HALO, cw), F32), pltpu.VMEM((tm + CONV_HALO, cw), F32),
                        pltpu.VMEM((CONV_K, 8, cw), F32), pltpu.VMEM((8, SHIFTED_ROWS, cw), F32),
                        pltpu.VMEM((8, SHIFTED_ROWS, cw), F32)],
        compiler_params=_params("arbitrary"),
    )(dout, dout, c, c, proj_a, proj_a, w, ln_g, ln_b)


def _attn_load_kv(kv_hbm, k_pad, v_pad, sem, t_cols):
    k_pad[:, 0:LEFT] = jnp.zeros((ATTN_WIDTH, LEFT), BF16)
    v_pad[:, 0:LEFT] = jnp.zeros((ATTN_WIDTH, LEFT), BF16)
    ck = pltpu.make_async_copy(kv_hbm.at[pl.ds(ATTN_WIDTH, ATTN_WIDTH), :], k_pad.at[:, pl.ds(LEFT, t_cols)], sem.at[0])
    cv = pltpu.make_async_copy(kv_hbm.at[pl.ds(2 * ATTN_WIDTH, ATTN_WIDTH), :], v_pad.at[:, pl.ds(LEFT, t_cols)], sem.at[1])
    ck.start()
    cv.start()
    ck.wait()
    cv.wait()


def _attn_build_bias(tab_ref, bias_t):
    row = lax.broadcasted_iota(jnp.int32, (LANES, LANES), 0)
    lane = lax.broadcasted_iota(jnp.int32, (LANES, LANES), 1)
    upper = lane >= row
    lane64 = lax.broadcasted_iota(jnp.int32, (CHUNK, LANES), 1)
    for h in range(N_HEADS):
        far = jnp.broadcast_to(tab_ref[h:h + 1, 2 * MAX_REL:2 * MAX_REL + 1], (LANES, LANES))
        hi = jnp.broadcast_to(tab_ref[h:h + 1, MAX_REL:2 * MAX_REL], (LANES, LANES))
        lo = jnp.broadcast_to(tab_ref[h:h + 1, 0:MAX_REL], (LANES, LANES))
        hi_d = pltpu.roll(hi, 0, 1, stride=1, stride_axis=0)
        lo_d = pltpu.roll(lo, 0, 1, stride=1, stride_axis=0)
        bias_t[h, 0:WIN - 2 * LANES, :] = jnp.broadcast_to(far[0:1, :], (WIN - 2 * LANES, LANES))
        bias_t[h, WIN - 2 * LANES:WIN - LANES, :] = jnp.where(upper, far, hi_d)
        bias_t[h, WIN - LANES:WIN, :] = jnp.where(upper, hi_d, lo_d)
        bias_t[h, 0:CHUNK, :] = jnp.where(lane64 < CHUNK, bias_t[h, 0:CHUNK, :], NEG_INF)
        bias_t[h, WIN - CHUNK:WIN, :] = jnp.where(lane64 >= CHUNK, bias_t[h, WIN - CHUNK:WIN, :], NEG_INF)


def _head_rows(h):
    return slice(h * HEAD_DIM, (h + 1) * HEAD_DIM)


def _attn_scores(k_pad, q_ref, s_buf, w0):
    for h in range(N_HEADS):
        q_h = q_ref[_head_rows(h), :] * jnp.asarray(SCALE, BF16)
        s_buf[h] = lax.dot_general(k_pad[_head_rows(h), pl.ds(w0, WIN)], q_h, _DOT_DIMS["tn"],
                                   preferred_element_type=F32)


def _attn_logits(s, bias, first_valid, key0=0):
    s = s + bias
    if first_valid is not None:
        s = jnp.where(lax.broadcasted_iota(jnp.int32, s.shape, 0) + key0 >= first_valid, s, NEG_INF)
    return s


def _attn_probs(s, bias_h, first_valid):
    s = _attn_logits(s, bias_h, first_valid)
    top = jnp.max(s, axis=0, keepdims=True)
    e = jnp.exp(s - top)
    total = jnp.sum(e, axis=0, keepdims=True)
    return e * (1.0 / total), top + jnp.log(total)


def _attn_by_padding(m, fn):
    @pl.when(m < LEFT // QBLK)
    def _():
        fn(LEFT - m * QBLK)

    @pl.when(m >= LEFT // QBLK)
    def _():
        fn(None)


def _attn_forward(qkv_t, rel_bias):
    t_cols = qkv_t.shape[1]
    steps = t_cols // QBLK

    def body(q_ref, kv_hbm, tab_ref, o_ref, lse_ref, k_pad, v_pad, bias_t, s_buf, p_buf, sem):
        m = pl.program_id(0)

        @pl.when(m == 0)
        def _():
            _attn_build_bias(tab_ref, bias_t)
            _attn_load_kv(kv_hbm, k_pad, v_pad, sem, t_cols)

        w0 = pl.multiple_of(m * QBLK, QBLK)
        _attn_scores(k_pad, q_ref, s_buf, w0)

        def softmax(first_valid):
            for h in range(N_HEADS):
                p, lse = _attn_probs(s_buf[h], bias_t[h], first_valid)
                p_buf[h] = p.astype(BF16)
                lse_ref[h:h + 1, :] = lse

        _attn_by_padding(m, softmax)
        for h in range(N_HEADS):
            o_h = lax.dot_general(v_pad[_head_rows(h), pl.ds(w0, WIN)], p_buf[h], _DOT_DIMS["nn"],
                                  preferred_element_type=F32)
            o_ref[_head_rows(h), :] = o_h.astype(BF16)

    return pl.pallas_call(
        body, name="attn_forward", grid=(steps,),
        in_specs=[pl.BlockSpec((ATTN_WIDTH, QBLK), lambda m: (0, m)),
                  pl.BlockSpec(memory_space=pl.ANY),
                  pl.BlockSpec((N_HEADS, 2 * MAX_REL + 1), lambda m: (0, 0))],
        out_specs=[pl.BlockSpec((ATTN_WIDTH, QBLK), lambda m: (0, m)), pl.BlockSpec((N_HEADS, QBLK), lambda m: (0, m))],
        out_shape=[jax.ShapeDtypeStruct((ATTN_WIDTH, t_cols), BF16), jax.ShapeDtypeStruct((N_HEADS, t_cols), F32)],
        scratch_shapes=[pltpu.VMEM((ATTN_WIDTH, LEFT + t_cols), BF16), pltpu.VMEM((ATTN_WIDTH, LEFT + t_cols), BF16),
                        pltpu.VMEM((N_HEADS, WIN, QBLK), F32), pltpu.VMEM((N_HEADS, WIN, QBLK), F32),
                        pltpu.VMEM((N_HEADS, WIN, QBLK), BF16), pltpu.SemaphoreType.DMA((2,))],
        compiler_params=_params("arbitrary"),
    )(qkv_t, qkv_t, rel_bias)


def _reverse_lanes(v, flip):
    out = jnp.zeros(v.shape, F32)
    rest = v
    for _ in range(3):
        piece = rest.astype(BF16)
        out = out + lax.dot_general(piece, flip, _DOT_DIMS["nn"], preferred_element_type=F32)
        rest = rest - piece.astype(F32)
    return out


def _attn_bias_grad(dbias_t, drel_ref):
    row = lax.broadcasted_iota(jnp.int32, (LANES, LANES), 0)
    lane = lax.broadcasted_iota(jnp.int32, (LANES, LANES), 1)
    flip = (row + lane == LANES - 1).astype(BF16)
    head = lax.broadcasted_iota(jnp.int32, (N_HEADS, LANES), 0)
    lane8 = lax.broadcasted_iota(jnp.int32, (N_HEADS, LANES), 1)
    upper_rev = jnp.zeros((N_HEADS, LANES), F32)
    lower_rev = jnp.zeros((N_HEADS, LANES), F32)
    far = jnp.zeros((N_HEADS, LANES), F32)
    for h in range(N_HEADS):
        def diagonals(block):
            skew = pltpu.roll(_reverse_lanes(block, flip), 0, 1, stride=1, stride_axis=0)
            pos = jnp.sum(jnp.where(lane >= row, skew, 0.0), axis=0, keepdims=True)
            neg = jnp.sum(jnp.where(lane < row, skew, 0.0), axis=0, keepdims=True)
            return pos, neg

        pos4, neg4 = diagonals(dbias_t[h, WIN - LANES:WIN, :])
        pos3, neg3 = diagonals(dbias_t[h, WIN - 2 * LANES:WIN - LANES, :])
        far_h = jnp.sum(jnp.sum(dbias_t[h, 0:WIN - 2 * LANES, :], axis=0, keepdims=True), axis=1, keepdims=True)
        far_h = far_h + jnp.sum(pos3, axis=1, keepdims=True)
        upper_rev = jnp.where(head == h, pos4 + neg3, upper_rev)
        lower_rev = jnp.where(head == h, neg4, lower_rev)
        far = jnp.where((head == h) & (lane8 == 0), far_h, far)
    drel_ref[:, 0:LANES] = _reverse_lanes(lower_rev, flip)
    drel_ref[:, LANES:2 * LANES] = _reverse_lanes(upper_rev, flip)
    drel_ref[:, 2 * LANES:3 * LANES] = far


def _attn_backward(qkv_t, o_t, do_t, lse, rel_bias):
    t_cols = qkv_t.shape[1]
    steps = t_cols // QBLK
    flush = LEFT // QBLK
    total = steps + flush

    def body(q_ref, o_ref, do_ref, lse_ref, kv_hbm, tab_ref, dq_ref, dk_ref, dv_ref, drel_ref,
             k_pad, v_pad, bias_t, dbias_t, dk_acc, dv_acc, s_buf, dp_buf, p_buf, ds_buf, sem):
        m = pl.program_id(0)

        @pl.when(m == 0)
        def _():
            _attn_build_bias(tab_ref, bias_t)
            _attn_load_kv(kv_hbm, k_pad, v_pad, sem, t_cols)
            dbias_t[...] = jnp.zeros_like(dbias_t)
            dk_acc[...] = jnp.zeros_like(dk_acc)
            dv_acc[...] = jnp.zeros_like(dv_acc)

        @pl.when(m < steps)
        def _():
            w0 = pl.multiple_of(m * QBLK, QBLK)
            _attn_scores(k_pad, q_ref, s_buf, w0)
            for h in range(N_HEADS):
                dp_buf[h] = lax.dot_general(v_pad[_head_rows(h), pl.ds(w0, WIN)], do_ref[_head_rows(h), :],
                                            _DOT_DIMS["tn"], preferred_element_type=F32)

            def softmax_backward(first_valid):
                for h in range(N_HEADS):
                    rows = _head_rows(h)
                    delta = jnp.sum(do_ref[rows, :].astype(F32) * o_ref[rows, :].astype(F32), axis=0, keepdims=True)
                    lse_h = lse_ref[h:h + 1, :]
                    for b in range(WIN // LANES):
                        keys = slice(b * LANES, (b + 1) * LANES)
                        s = _attn_logits(s_buf[h, keys, :], bias_t[h, keys, :], first_valid, b * LANES)
                        p = jnp.exp(s - lse_h)
                        ds = p * (dp_buf[h, keys, :] - delta)
                        dbias_t[h, keys, :] += ds
                        p_buf[h, keys, :] = p.astype(BF16)
                        ds_buf[h, keys, :] = (ds * SCALE).astype(BF16)

            _attn_by_padding(m, softmax_backward)
            for h in range(N_HEADS):
                rows = _head_rows(h)
                dq_h = lax.dot_general(k_pad[rows, pl.ds(w0, WIN)], ds_buf[h], _DOT_DIMS["nn"], preferred_element_type=F32)
                dq_ref[rows, :] = dq_h.astype(BF16)
                dk_acc[rows, :] += lax.dot_general(q_ref[rows, :], ds_buf[h], _DOT_DIMS["nt"], preferred_element_type=F32)
                dv_acc[rows, :] += lax.dot_general(do_ref[rows, :], p_buf[h], _DOT_DIMS["nt"], preferred_element_type=F32)

        dk_ref[...] = dk_acc[:, 0:QBLK].astype(BF16)
        dv_ref[...] = dv_acc[:, 0:QBLK].astype(BF16)
        for acc in (dk_acc, dv_acc):
            rest = acc[:, QBLK:WIN]
            acc[:, 0:LEFT] = rest
            acc[:, LEFT:WIN] = jnp.zeros((ATTN_WIDTH, QBLK), F32)

        @pl.when(m == total - 1)
        def _():
            _attn_bias_grad(dbias_t, drel_ref)

    qblk = pl.BlockSpec((ATTN_WIDTH, QBLK), lambda m: (0, jnp.minimum(m, steps - 1)))
    kblk = pl.BlockSpec((ATTN_WIDTH, QBLK), lambda m: (0, jnp.maximum(m - flush, 0)))
    dq, dk, dv, drel = pl.pallas_call(
        body, name="attn_backward", grid=(total,),
        in_specs=[qblk, qblk, qblk, pl.BlockSpec((N_HEADS, QBLK), lambda m: (0, jnp.minimum(m, steps - 1))),
                  pl.BlockSpec(memory_space=pl.ANY), pl.BlockSpec((N_HEADS, 2 * MAX_REL + 1), lambda m: (0, 0))],
        out_specs=[qblk, kblk, kblk, pl.BlockSpec((N_HEADS, 3 * LANES), lambda m: (0, 0))],
        out_shape=[jax.ShapeDtypeStruct((ATTN_WIDTH, t_cols), BF16)] * 3
        + [jax.ShapeDtypeStruct((N_HEADS, 3 * LANES), F32)],
        scratch_shapes=[pltpu.VMEM((ATTN_WIDTH, LEFT + t_cols), BF16), pltpu.VMEM((ATTN_WIDTH, LEFT + t_cols), BF16),
                        pltpu.VMEM((N_HEADS, WIN, QBLK), F32), pltpu.VMEM((N_HEADS, WIN, QBLK), F32),
                        pltpu.VMEM((ATTN_WIDTH, WIN), F32), pltpu.VMEM((ATTN_WIDTH, WIN), F32),
                        pltpu.VMEM((N_HEADS, WIN, QBLK), F32), pltpu.VMEM((N_HEADS, WIN, QBLK), F32),
                        pltpu.VMEM((N_HEADS, WIN, QBLK), BF16), pltpu.VMEM((N_HEADS, WIN, QBLK), BF16),
                        pltpu.SemaphoreType.DMA((2,))],
        compiler_params=_params("arbitrary"),
    )(qkv_t, o_t, do_t, lse, qkv_t, rel_bias)
    return dq, dk, dv, drel


FFN_TC = D_FF // 2
FFN_TR = 256


def _ffn_specs(t_rows, tr):
    nj = D_FF // FFN_TC
    hb = tr // FFN_HALO
    last_halo = t_rows // FFN_HALO - 1
    cur = lambda off: pl.BlockSpec((tr, FFN_TC), lambda j, i: (i, j + off))
    prev = lambda off: pl.BlockSpec((FFN_HALO, FFN_TC), lambda j, i: (jnp.maximum(i * hb - 1, 0), j + off))
    nxt = lambda off: pl.BlockSpec((FFN_HALO, FFN_TC), lambda j, i: (jnp.minimum((i + 1) * hb, last_halo), j + off))
    wspec = lambda off: pl.BlockSpec((FFN_K, FFN_TC), lambda j, i: (0, j + off))
    bspec = lambda off: pl.BlockSpec((1, FFN_TC), lambda j, i: (0, j + off))
    return nj, cur, prev, nxt, wspec, bspec


FFN_STRIP = 16


def _ffn_conv(win, w, b, rows):
    out = b + w[2] * win[FFN_HALO:FFN_HALO + rows, :]
    out = out + w[1] * win[FFN_HALO - 1:FFN_HALO - 1 + rows, :]
    return out + w[0] * win[FFN_HALO - 2:FFN_HALO - 2 + rows, :]


def _taps(w_ref):
    return [w_ref[kk:kk + 1, :] for kk in range(FFN_K)]


def _ffn_first_window(prev_ref, cur_ref, tile, rows):
    return jnp.concatenate([jnp.where(tile > 0, prev_ref[...], 0.0), cur_ref[0:rows, :]], axis=0)


def _fold8(v):
    return jnp.sum(v.reshape(v.shape[0] // 8, 8, v.shape[1]), axis=0)


def _ffn_activation(hup, w, b):
    t_rows = hup.shape[0]
    tr = _row_tile(t_rows, FFN_TR)
    nj, cur, prev, nxt, wspec, bspec = _ffn_specs(t_rows, tr)
    rs = FFN_STRIP

    def body(g_ref, gprev_ref, v_ref, vprev_ref, wg_ref, wv_ref, bg_ref, bv_ref, act_ref, gel_ref, slope_ref):
        i = pl.program_id(1)
        wg, wv, bg, bv = _taps(wg_ref), _taps(wv_ref), bg_ref[...], bv_ref[...]

        def emit(base, g_win, v_win):
            gel, dgel = _gelu_parts(_ffn_conv(g_win, wg, bg, rs))
            cv = _ffn_conv(v_win, wv, bv, rs)
            act_ref[pl.ds(base, rs), :] = (gel * cv).astype(BF16)
            gel_ref[pl.ds(base, rs), :] = gel
            slope_ref[pl.ds(base, rs), :] = cv * dgel

        def strip(s, carry):
            base = pl.multiple_of(s * rs, rs)
            emit(base, g_ref[pl.ds(base - FFN_HALO, rs + FFN_HALO), :], v_ref[pl.ds(base - FFN_HALO, rs + FFN_HALO), :])
            return carry

        emit(0, _ffn_first_window(gprev_ref, g_ref, i, rs), _ffn_first_window(vprev_ref, v_ref, i, rs))
        lax.fori_loop(1, tr // rs, strip, 0)

    return pl.pallas_call(
        body, name="ffn_activation", grid=(nj, t_rows // tr),
        in_specs=[cur(0), prev(0), cur(nj), prev(nj), wspec(0), wspec(nj), bspec(0), bspec(nj)],
        out_specs=[cur(0), cur(0), cur(0)],
        out_shape=[jax.ShapeDtypeStruct((t_rows, D_FF), BF16), jax.ShapeDtypeStruct((t_rows, D_FF), F32),
                   jax.ShapeDtypeStruct((t_rows, D_FF), F32)],
        compiler_params=_params("parallel", "parallel"),
    )(hup, hup, hup, hup, w, w, b, b)


def _ffn_backward(dact, gel, slope, hup, w):
    t_rows = hup.shape[0]
    tr = _row_tile(t_rows, FFN_TR)
    nj, cur, prev, nxt, wspec, bspec = _ffn_specs(t_rows, tr)
    ni = t_rows // tr
    rs = FFN_STRIP
    ns = tr // rs

    def body(da_ref, danext_ref, gel_ref, gelnext_ref, slope_ref, slopenext_ref, g_ref, gprev_ref, v_ref, vprev_ref,
             wg_ref, wv_ref, dhg_ref, dhv_ref, dwg_ref, dwv_ref, dbg_ref, dbv_ref, sums):
        i = pl.program_id(1)

        @pl.when(i == 0)
        def _():
            sums[...] = jnp.zeros_like(sums)

        wg, wv = _taps(wg_ref), _taps(wv_ref)
        da_after = jnp.where(i < ni - 1, danext_ref[...], 0.0)

        def strip_at(base, g_win, v_win, carry):
            da = da_ref[pl.ds(base, rs), :]
            dcg, dcv = da * slope_ref[pl.ds(base, rs), :], da * gel_ref[pl.ds(base, rs), :]
            out = []
            for half_i, (dc, after, taps, win, dh_ref) in enumerate(((dcg, carry[0], wg, g_win, dhg_ref),
                                                                   (dcv, carry[1], wv, v_win, dhv_ref))):
                ext = jnp.concatenate([dc, after], axis=0)
                dh = taps[2] * dc + taps[1] * ext[1:1 + rs, :] + taps[0] * ext[2:2 + rs, :]
                dh_ref[pl.ds(base, rs), :] = dh.astype(BF16)
                sums[4 * half_i] += _fold8(dc)
                for kk in range(FFN_K):
                    sums[4 * half_i + 1 + kk] += _fold8(dc * win[FFN_HALO - 2 + kk:FFN_HALO - 2 + kk + rs, :])
                out.append(dc[0:FFN_HALO, :])
            return tuple(out)

        def strip(s, carry):
            base = pl.multiple_of((ns - 1 - s) * rs, rs)
            return strip_at(base, g_ref[pl.ds(base - FFN_HALO, rs + FFN_HALO), :],
                            v_ref[pl.ds(base - FFN_HALO, rs + FFN_HALO), :], carry)

        carry = lax.fori_loop(0, ns - 1, strip, (da_after * slopenext_ref[...], da_after * gelnext_ref[...]))
        strip_at(0, _ffn_first_window(gprev_ref, g_ref, i, rs), _ffn_first_window(vprev_ref, v_ref, i, rs), carry)

        @pl.when(i == ni - 1)
        def _():
            for half_i, (db_ref, dw_ref) in enumerate(((dbg_ref, dwg_ref), (dbv_ref, dwv_ref))):
                db_ref[...] = jnp.sum(sums[4 * half_i], axis=0, keepdims=True)
                for kk in range(FFN_K):
                    dw_ref[kk:kk + 1, :] = jnp.sum(sums[4 * half_i + 1 + kk], axis=0, keepdims=True)

    half = jax.ShapeDtypeStruct((t_rows, D_FF), BF16)
    return pl.pallas_call(
        body, name="ffn_backward", grid=(nj, ni),
        in_specs=[cur(0), nxt(0), cur(0), nxt(0), cur(0), nxt(0), cur(0), prev(0), cur(nj), prev(nj),
                  wspec(0), wspec(nj)],
        out_specs=[cur(0), cur(0), wspec(0), wspec(0), bspec(0), bspec(0)],
        out_shape=[half, half, jax.ShapeDtypeStruct((FFN_K, D_FF), F32), jax.ShapeDtypeStruct((FFN_K, D_FF), F32),
                   jax.ShapeDtypeStruct((1, D_FF), F32), jax.ShapeDtypeStruct((1, D_FF), F32)],
        scratch_shapes=[pltpu.VMEM((2 * (1 + FFN_K), 8, FFN_TC), F32)],
        compiler_params=_params("parallel", "arbitrary"),
    )(dact, dact, gel, gel, slope, slope, hup, hup, hup, hup, w, w)


def _mesh_position():
    return lax.axis_index("x"), lax.axis_index("y"), lax.axis_index("c")


def _hbm_specs(n):
    return [pl.BlockSpec(memory_space=pl.ANY)] * n


def _all_gather(shards, name, placed=None):
    n = len(shards)

    def body(*refs):
        ins = refs[:n]
        outs = refs[2 * n:3 * n] if placed else refs[n:2 * n]
        send_sems, recv_sems, local_sems = refs[-3:]
        x, y, c = _mesh_position()
        me, sibling = (x, y, c), (x, y, 1 - c)
        chips = [(1 - x, y), (x, 1 - y), (1 - x, 1 - y)]

        def copy(a, slot, block, to, src=None):
            dst = outs[a].at[4 * block[0] + 2 * block[1] + block[2]]
            return pltpu.make_async_remote_copy(
                src_ref=dst if src is None else src, dst_ref=dst,
                send_sem=send_sems.at[a, slot], recv_sem=recv_sems.at[a, slot],
                device_id=to, device_id_type=MESH)

        started = []
        for a in range(0 if placed else n):
            mine = pltpu.make_async_copy(ins[a], outs[a].at[4 * x + 2 * y + c], local_sems.at[a])
            mine.start()
            started.append(mine)
        first = []
        for a in range(n):
            first.append(copy(a, 0, me, sibling, src=ins[a]))
            first += [copy(a, 1 + j, me, (*chip, c), src=ins[a]) for j, chip in enumerate(chips)]
        for cp in first:
            cp.start()
        passed = []
        for j, chip in enumerate(chips):
            for a in range(n):
                copy(a, 1 + j, (*chip, c), me).wait_recv()
                fwd = copy(a, 4 + j, (*chip, c), sibling)
                fwd.start()
                passed.append(fwd)
        for a in range(n):
            copy(a, 0, sibling, me).wait_recv()
            for j, chip in enumerate(chips):
                copy(a, 4 + j, (*chip, 1 - c), me).wait_recv()
        for cp in first + passed:
            cp.wait_send()
        for mine in started:
            mine.wait()

    operands = [*shards, *placed] if placed else list(shards)
    return pl.pallas_call(
        body, name=name,
        in_specs=_hbm_specs(len(operands)), out_specs=_hbm_specs(n),
        out_shape=[jax.ShapeDtypeStruct((N_DEV,) + s.shape, s.dtype) for s in shards],
        scratch_shapes=[pltpu.SemaphoreType.DMA((n, 7)), pltpu.SemaphoreType.DMA((n, 7)),
                        pltpu.SemaphoreType.DMA((n,))],
        input_output_aliases={n + a: a for a in range(n)} if placed else {},
        compiler_params=pltpu.CompilerParams(has_side_effects=True),
    )(*operands)


def _place_own(shards, me):
    n = len(shards)

    def body(me_ref, *refs):
        for src, dst in zip(refs[:n], refs[n:]):
            dst[0] = src[...]

    return pl.pallas_call(
        body, name="place_own_shards",
        grid_spec=pltpu.PrefetchScalarGridSpec(
            num_scalar_prefetch=1, grid=(1,),
            in_specs=[pl.BlockSpec(s.shape, lambda i, me_ref: (0, 0)) for s in shards],
            out_specs=[pl.BlockSpec((1,) + s.shape, lambda i, me_ref: (me_ref[0], 0, 0)) for s in shards]),
        out_shape=[jax.ShapeDtypeStruct((N_DEV,) + s.shape, s.dtype) for s in shards],
        compiler_params=_params("arbitrary"),
    )(me, *shards)


_FLIPS = [(dx, dy, dc) for dx in (0, 1) for dy in (0, 1) for dc in (0, 1)][1:]
_HBM = pl.BlockSpec(memory_space=pltpu.HBM)
_SEM = pl.BlockSpec(memory_space=pltpu.SEMAPHORE)
_DATAFLOW = pltpu.SideEffectType.DATAFLOW_SIDE_EFFECTING


def _scatter_copies(src_refs, land_refs, send_sems, recv_sems, gather):
    x, y, c = _mesh_position()
    me = 4 * x + 2 * y + c
    copies = []
    for a, (src, land) in enumerate(zip(src_refs, land_refs)):
        for k, (dx, dy, dc) in enumerate(_FLIPS):
            px, py, pc = (x + dx) % 2, (y + dy) % 2, (c + dc) % 2
            pair = a * len(_FLIPS) + k
            copies.append(pltpu.make_async_remote_copy(
                src_ref=src if gather else src.at[4 * px + 2 * py + pc], dst_ref=land.at[me],
                send_sem=send_sems[pair], recv_sem=recv_sems[pair],
                device_id=(px, py, pc), device_id_type=MESH))
    return copies


def _scatter_start(srcs, lands, gather, name, after=None):
    n = len(srcs)
    pairs = n * len(_FLIPS)
    extra = [] if after is None else [after]

    def body(*refs):
        src_refs, land_refs = refs[:n], refs[n:2 * n]
        first = 2 * n + len(extra)
        send_sems, recv_sems = refs[first:first + pairs], refs[first + pairs:first + 2 * pairs]
        token = refs[-1]
        for cp in _scatter_copies(src_refs, land_refs, send_sems, recv_sems, gather):
            cp.start()
        token[...] = jnp.zeros_like(token)

    arrays = [*srcs, *lands]
    sem = pltpu.SemaphoreType.DMA(())
    out = pl.pallas_call(
        body, name=name,
        out_shape=(*[sem] * (2 * pairs), *[pltpu.HBM(v.shape, v.dtype) for v in arrays],
                   jax.ShapeDtypeStruct((8, LANES), F32)),
        in_specs=[*[_HBM] * (2 * n), *[pl.BlockSpec(memory_space=pl.ANY)] * len(extra)],
        out_specs=(*[_SEM] * (2 * pairs), *[_HBM] * (2 * n), pl.BlockSpec(memory_space=pltpu.VMEM)),
        input_output_aliases={i: 2 * pairs + i for i in range(2 * n)},
        compiler_params=pltpu.CompilerParams(has_side_effects=_DATAFLOW),
    )(*[pltpu.with_memory_space_constraint(v, pltpu.HBM) for v in arrays], *extra)
    sems, rest = out[:2 * pairs], out[2 * pairs:]
    return list(sems[:pairs]), list(sems[pairs:]), list(rest[:n]), list(rest[n:2 * n]), rest[-1]


def _scatter_wait(send_sems, recv_sems, srcs, lands, after, gather, name):
    n = len(srcs)
    pairs = n * len(_FLIPS)

    def body(*refs):
        src_refs, land_refs = refs[:n], refs[n:2 * n]
        send_refs, recv_refs = refs[2 * n:2 * n + pairs], refs[2 * n + pairs:2 * n + 2 * pairs]
        for cp in _scatter_copies(src_refs, land_refs, send_refs, recv_refs, gather):
            cp.wait_send()
            cp.wait_recv()

    arrays = [*srcs, *lands]
    out = pl.pallas_call(
        body, name=name,
        out_shape=tuple(pltpu.HBM(v.shape, v.dtype) for v in arrays),
        in_specs=[*[_HBM] * (2 * n), *[_SEM] * (2 * pairs), pl.BlockSpec(memory_space=pl.ANY)],
        out_specs=tuple([_HBM] * (2 * n)),
        input_output_aliases={i: i for i in range(2 * n)},
        compiler_params=pltpu.CompilerParams(has_side_effects=_DATAFLOW),
    )(*arrays, *send_sems, *recv_sems, after)
    return list(out[:n]), list(out[n:])


def _sum_received(grad, received, me, name):
    _, rows, cols = grad.shape

    def body(me_ref, g_ref, r_ref, o_ref):
        p = pl.program_id(0)
        term = jnp.where(p == me_ref[0], g_ref[0], r_ref[0])

        @pl.when(p == 0)
        def _():
            o_ref[...] = term

        @pl.when(p > 0)
        def _():
            o_ref[...] += term

    blk = (1, rows, cols)
    return pl.pallas_call(
        body, name=name,
        grid_spec=pltpu.PrefetchScalarGridSpec(
            num_scalar_prefetch=1, grid=(N_DEV,),
            in_specs=[pl.BlockSpec(blk, lambda p, me_ref: (me_ref[0], 0, 0)),
                      pl.BlockSpec(blk, lambda p, me_ref: (p, 0, 0))],
            out_specs=pl.BlockSpec((rows, cols), lambda p, me_ref: (0, 0))),
        out_shape=jax.ShapeDtypeStruct((rows, cols), F32),
        compiler_params=_params("arbitrary"),
    )(me, grad, received)


def _exchange_in_chip(grads):
    n = len(grads)

    def body(*refs):
        ins, outs = refs[:n], refs[n:2 * n]
        send_sems, recv_sems = refs[2 * n:]
        x, y, c = _mesh_position()
        copies = []
        for a in range(n):
            for q in range(4):
                copies.append(pltpu.make_async_remote_copy(
                    src_ref=ins[a].at[2 * q + (1 - c)], dst_ref=outs[a].at[q],
                    send_sem=send_sems.at[a, q], recv_sem=recv_sems.at[a, q],
                    device_id=(x, y, 1 - c), device_id_type=MESH))
        for cp in copies:
            cp.start()
        for cp in copies:
            cp.wait_recv()
        for cp in copies:
            cp.wait_send()

    return pl.pallas_call(
        body, name="exchange_in_chip",
        in_specs=_hbm_specs(n), out_specs=_hbm_specs(n),
        out_shape=[jax.ShapeDtypeStruct((4,) + g.shape[1:], g.dtype) for g in grads],
        scratch_shapes=[pltpu.SemaphoreType.DMA((n, 4)), pltpu.SemaphoreType.DMA((n, 4))],
        compiler_params=pltpu.CompilerParams(has_side_effects=True),
    )(*grads)


def _exchange_between_chips(partials):
    n = len(partials)

    def body(*refs):
        ins, outs = refs[:n], refs[n:2 * n]
        send_sems, recv_sems = refs[2 * n:]
        x, y, c = _mesh_position()
        chips = [(1 - x, y), (x, 1 - y), (1 - x, 1 - y)]
        copies = []
        for a in range(n):
            for j, (px, py) in enumerate(chips):
                copies.append(pltpu.make_async_remote_copy(
                    src_ref=ins[a].at[2 * px + py], dst_ref=outs[a].at[j],
                    send_sem=send_sems.at[a, j], recv_sem=recv_sems.at[a, j],
                    device_id=(px, py, c), device_id_type=MESH))
        for cp in copies:
            cp.start()
        for cp in copies:
            cp.wait_recv()
        for cp in copies:
            cp.wait_send()

    return pl.pallas_call(
        body, name="exchange_between_chips",
        in_specs=_hbm_specs(n), out_specs=_hbm_specs(n),
        out_shape=[jax.ShapeDtypeStruct((3,) + p.shape[1:], p.dtype) for p in partials],
        scratch_shapes=[pltpu.SemaphoreType.DMA((n, 3)), pltpu.SemaphoreType.DMA((n, 3))],
        compiler_params=pltpu.CompilerParams(has_side_effects=True),
    )(*partials)


def _add_in_chip(grad, received, core, name):
    _, rows, cols = grad.shape

    def body(core_ref, g_ref, r_ref, o_ref):
        o_ref[...] = g_ref[...] + r_ref[...]

    blk = (1, rows, cols)
    return pl.pallas_call(
        body, name=name,
        grid_spec=pltpu.PrefetchScalarGridSpec(
            num_scalar_prefetch=1, grid=(4,),
            in_specs=[pl.BlockSpec(blk, lambda q, core_ref: (2 * q + core_ref[0], 0, 0)),
                      pl.BlockSpec(blk, lambda q, core_ref: (q, 0, 0))],
            out_specs=pl.BlockSpec(blk, lambda q, core_ref: (q, 0, 0))),
        out_shape=jax.ShapeDtypeStruct((4, rows, cols), F32),
        compiler_params=_params("parallel"),
    )(core, grad, received)


def _add_between_chips(partial, received, chip, name):
    _, rows, cols = partial.shape

    def body(chip_ref, p_ref, r_ref, o_ref):
        o_ref[...] = ((p_ref[0] + r_ref[0]) + r_ref[1]) + r_ref[2]

    return pl.pallas_call(
        body, name=name,
        grid_spec=pltpu.PrefetchScalarGridSpec(
            num_scalar_prefetch=1, grid=(1,),
            in_specs=[pl.BlockSpec((1, rows, cols), lambda i, chip_ref: (chip_ref[0], 0, 0)),
                      pl.BlockSpec((3, rows, cols), lambda i, chip_ref: (0, 0, 0))],
            out_specs=pl.BlockSpec((rows, cols), lambda i, chip_ref: (0, 0))),
        out_shape=jax.ShapeDtypeStruct((rows, cols), F32),
        compiler_params=_params("arbitrary"),
    )(chip, partial, received)


def _sum_devices(gathered):
    _, rows, cols = gathered.shape

    def body(g_ref, o_ref):
        total = g_ref[0]
        for d in range(1, N_DEV):
            total = total + g_ref[d]
        o_ref[...] = total

    return pl.pallas_call(
        body, name="sum_small_grads",
        out_shape=jax.ShapeDtypeStruct((rows, cols), F32),
        compiler_params=_params(),
    )(gathered)


def _adamw(w, g, m, v, name):
    rows, cols = w.shape
    tr = rows
    for cand in (256, 128, 64, 32, 16, 8):
        if rows > cand and rows % cand == 0:
            tr = cand
            break

    def body(w_ref, g_ref, m_ref, v_ref, delta_ref, newm_ref, newv_ref):
        g_v = g_ref[...]
        new_m = ADAM_B1 * m_ref[...] + (1.0 - ADAM_B1) * g_v
        new_v = ADAM_B2 * v_ref[...] + (1.0 - ADAM_B2) * (g_v * g_v)
        m_hat = new_m / (1.0 - ADAM_B1 ** ADAM_STEP)
        v_hat = new_v / (1.0 - ADAM_B2 ** ADAM_STEP)
        delta_ref[...] = -ADAM_LR * (m_hat / (jnp.sqrt(v_hat) + ADAM_EPS) + ADAM_WD * w_ref[...])
        newm_ref[...] = new_m
        newv_ref[...] = new_v

    blk = pl.BlockSpec((tr, cols), lambda i: (i, 0))
    shape = jax.ShapeDtypeStruct((rows, cols), F32)
    return pl.pallas_call(
        body, name=name, grid=(rows // tr,),
        in_specs=[blk] * 4, out_specs=[blk] * 3, out_shape=[shape] * 3,
        compiler_params=_params("parallel"),
    )(w, g, m, v)


def _adamw_update(w, g, m, v):
    new_m = ADAM_B1 * m + (1.0 - ADAM_B1) * g
    new_v = ADAM_B2 * v + (1.0 - ADAM_B2) * (g * g)
    m_hat = new_m / (1.0 - ADAM_B1 ** ADAM_STEP)
    v_hat = new_v / (1.0 - ADAM_B2 ** ADAM_STEP)
    return -ADAM_LR * (m_hat / (jnp.sqrt(v_hat) + ADAM_EPS) + ADAM_WD * w), new_m, new_v


def _transpose_f32(v):
    r = v.shape[0]
    eye = (lax.broadcasted_iota(jnp.int32, (r, r), 0) == lax.broadcasted_iota(jnp.int32, (r, r), 1)).astype(BF16)
    out, rest = None, v
    for _ in range(3):
        piece = rest.astype(BF16)
        part = lax.dot_general(piece, eye, _DOT_DIMS["tn"], preferred_element_type=F32)
        out = part if out is None else out + part
        rest = rest - piece.astype(F32)
    return out


def _reduce_and_adamw(grad, received, me, w, m, v, transposed, name):
    _, rows, cols = grad.shape
    tc = 256
    assert cols % tc == 0

    def body(me_ref, g_ref, r_ref, w_ref, m_ref, v_ref, grad_ref, delta_ref, newm_ref, newv_ref, acc):
        p = pl.program_id(1)
        term = jnp.where(p == me_ref[0], g_ref[0], r_ref[0])

        @pl.when(p == 0)
        def _():
            acc[...] = term

        @pl.when(p > 0)
        def _():
            acc[...] += term

        @pl.when(p == N_DEV - 1)
        def _():
            g = _transpose_f32(acc[...]) if transposed else acc[...]
            grad_ref[...] = g
            delta_ref[...], newm_ref[...], newv_ref[...] = _adamw_update(w_ref[...], g, m_ref[...], v_ref[...])

    blk = (1, rows, tc)
    if transposed:
        tile = pl.BlockSpec((tc, rows), lambda j, p, me_ref: (j, 0))
    else:
        tile = pl.BlockSpec((rows, tc), lambda j, p, me_ref: (0, j))
    shape = jax.ShapeDtypeStruct(w.shape, F32)
    return pl.pallas_call(
        body, name=name,
        grid_spec=pltpu.PrefetchScalarGridSpec(
            num_scalar_prefetch=1, grid=(cols // tc, N_DEV),
            in_specs=[pl.BlockSpec(blk, lambda j, p, me_ref: (me_ref[0], 0, j)),
                      pl.BlockSpec(blk, lambda j, p, me_ref: (p, 0, j)), tile, tile, tile],
            out_specs=[tile] * 4,
            scratch_shapes=[pltpu.VMEM((rows, tc), F32)]),
        out_shape=[shape] * 4,
        compiler_params=_params("parallel", "arbitrary"),
    )(me, grad, received, w, m, v)


def _pack(pieces, rows):
    flat = jnp.concatenate([p.reshape(-1) for p in pieces])
    return jnp.pad(flat, (0, rows * LANES - flat.shape[0])).reshape(rows, LANES)


def _unpack(packed, shapes):
    flat = packed.reshape(-1)
    out, pos = [], 0
    for shape in shapes:
        size = 1
        for s in shape:
            size *= s
        out.append(flat[pos:pos + size].reshape(shape))
        pos += size
    return out


def _rows_for(count):
    return -(-count // (8 * LANES)) * 8


SMALL = ("norm_mix_pre", "conv_dw_b", "conv_ln_g", "conv_ln_b", "rel_bias", "norm_mix_post", "norm_ffn_pre",
         "ffn_dw_b", "norm_ffn_post")
SHARDED_SMALL = ("conv_dw_w", "ffn_dw_w")
LARGE = ("w_in", "w_out", "w_up", "w_down")
WEIGHTS = ("norm_mix_pre", "w_in", "conv_dw_w", "conv_dw_b", "conv_ln_g", "conv_ln_b", "rel_bias", "w_out",
           "norm_mix_post", "norm_ffn_pre", "w_up", "ffn_dw_w", "ffn_dw_b", "w_down", "norm_ffn_post")


def kernel(x, norm_mix_pre, w_in, conv_dw_w, conv_dw_b, conv_ln_g, conv_ln_b, rel_bias, w_out, norm_mix_post, norm_ffn_pre, w_up, ffn_dw_w, ffn_dw_b, w_down, norm_ffn_post, loss_target, m_norm_mix_pre, m_w_in, m_conv_dw_w, m_conv_dw_b, m_conv_ln_g, m_conv_ln_b, m_rel_bias, m_w_out, m_norm_mix_post, m_norm_ffn_pre, m_w_up, m_ffn_dw_w, m_ffn_dw_b, m_w_down, m_norm_ffn_post, v_norm_mix_pre, v_w_in, v_conv_dw_w, v_conv_dw_b, v_conv_ln_g, v_conv_ln_b, v_rel_bias, v_w_out, v_norm_mix_post, v_norm_ffn_pre, v_w_up, v_ffn_dw_w, v_ffn_dw_b, v_w_down, v_norm_ffn_post):
    weights = dict(norm_mix_pre=norm_mix_pre, w_in=w_in, conv_dw_w=conv_dw_w, conv_dw_b=conv_dw_b, conv_ln_g=conv_ln_g,
                   conv_ln_b=conv_ln_b, rel_bias=rel_bias, w_out=w_out, norm_mix_post=norm_mix_post,
                   norm_ffn_pre=norm_ffn_pre, w_up=w_up, ffn_dw_w=ffn_dw_w, ffn_dw_b=ffn_dw_b, w_down=w_down,
                   norm_ffn_post=norm_ffn_post)
    mom1 = dict(norm_mix_pre=m_norm_mix_pre, w_in=m_w_in, conv_dw_w=m_conv_dw_w, conv_dw_b=m_conv_dw_b,
                conv_ln_g=m_conv_ln_g, conv_ln_b=m_conv_ln_b, rel_bias=m_rel_bias, w_out=m_w_out,
                norm_mix_post=m_norm_mix_post, norm_ffn_pre=m_norm_ffn_pre, w_up=m_w_up, ffn_dw_w=m_ffn_dw_w,
                ffn_dw_b=m_ffn_dw_b, w_down=m_w_down, norm_ffn_post=m_norm_ffn_post)
    mom2 = dict(norm_mix_pre=v_norm_mix_pre, w_in=v_w_in, conv_dw_w=v_conv_dw_w, conv_dw_b=v_conv_dw_b,
                conv_ln_g=v_conv_ln_g, conv_ln_b=v_conv_ln_b, rel_bias=v_rel_bias, w_out=v_w_out,
                norm_mix_post=v_norm_mix_post, norm_ffn_pre=v_norm_ffn_pre, w_up=v_w_up, ffn_dw_w=v_ffn_dw_w,
                ffn_dw_b=v_ffn_dw_b, w_down=v_w_down, norm_ffn_post=v_norm_ffn_post)

    x2 = x[0]
    target = loss_target[0]
    t_rows = x2.shape[0]
    d = D_MODEL
    in_cols = 2 * CONV_WIDTH + 3 * ATTN_WIDTH
    my_x, my_y, my_c = _mesh_position()
    my_dev = 4 * my_x + 2 * my_y + my_c

    small_conv = _pack([conv_dw_w[0], ffn_dw_w[0]], 32)
    me = jnp.reshape(my_dev, (1,)).astype(jnp.int32)
    first_shards = [w_in[0].T.astype(BF16), small_conv]
    late_shards = [w_out[0].astype(BF16), w_up[0].T.astype(BF16), w_down[0].astype(BF16)]
    placed = _place_own(first_shards + late_shards, me)
    win_t, conv_g = _all_gather(first_shards, "all_gather_weights", placed=placed[:2])
    late_lands = placed[2:]
    late_send, late_recv, late_shards, late_lands, late_token = _scatter_start(
        late_shards, late_lands, True, "gather_late_weights_start", after=win_t)
    win_t = win_t.reshape(in_cols, d)
    conv_flat = conv_g.reshape(N_DEV, 32 * LANES)
    n_cw = CONV_K * (CONV_WIDTH // N_DEV)
    conv_w_full = conv_flat[:, :n_cw].reshape(N_DEV, CONV_K, CONV_WIDTH // N_DEV).transpose(1, 0, 2).reshape(CONV_K, CONV_WIDTH)
    ffn_w_full = conv_flat[:, n_cw:].reshape(N_DEV, FFN_K, 2 * D_FF // N_DEV).transpose(1, 0, 2).reshape(FFN_K, 2 * D_FF)

    u1 = _pre_norm(x2, norm_mix_pre + late_token[0:1, 0:1], "pre_norm_mix")
    proj_a = _matmul(u1, win_t, mode="nt", m=t_rows, n=2 * CONV_WIDTH, k=d, tm=2048, tn=1024, tk=d,
                     out_dtype=F32, name="proj_conv")
    qkv_t = _matmul(win_t, u1, mode="nt", m=3 * ATTN_WIDTH, n=t_rows, k=d, tm=512, tn=2048, tk=d,
                    out_dtype=BF16, name="proj_qkv", a_m0=2 * CONV_WIDTH)
    conv_c, conv_out = _conv_forward(proj_a, conv_w_full, conv_dw_b, conv_ln_g, conv_ln_b)
    o_t, attn_lse = _attn_forward(qkv_t, rel_bias[0])
    _, (wout_g, wup_t, wdown_g) = _scatter_wait(late_send, late_recv, late_shards, late_lands, o_t, True,
                                                "gather_late_weights_wait")
    wout_g = wout_g.reshape(d, d)
    wup_t = wup_t.reshape(2 * D_FF, d)
    wdown_g = wdown_g.reshape(D_FF, d)
    mixed = _matmul_sum([(conv_out, "nn", CONV_WIDTH, 0), (o_t, "tn", ATTN_WIDTH, CONV_WIDTH)], wout_g,
                        m=t_rows, n=d, tm=1024, name="out_proj")
    h1, u2 = _mid_forward(x2, mixed, norm_mix_post, norm_ffn_pre)
    hup = _matmul(u2, wup_t, mode="nt", m=t_rows, n=2 * D_FF, k=d, tm=2048, tn=1408, tk=d,
                  out_dtype=F32, name="ffn_up")
    act, ffn_gel, ffn_slope = _ffn_activation(hup, ffn_w_full, ffn_dw_b)
    f = _matmul(act, wdown_g, mode="nn", m=t_rows, n=d, k=D_FF, tm=1024, tn=1024, tk=D_FF,
                out_dtype=F32, name="ffn_down")
    loss, dy, df, d_norm_ffn_post = _loss_and_head_backward(h1, f, target, norm_ffn_post)

    dact = _matmul(df, wdown_g, mode="nt", m=t_rows, n=D_FF, k=d, tm=2048, tn=1408, tk=d,
                   out_dtype=F32, name="ffn_down_dx")
    g_wdown = _matmul(act, df, mode="tn", m=D_FF, n=d, k=t_rows, tm=1408, tn=1024, tk=2048,
                      out_dtype=F32, name="ffn_down_dw")
    dhg, dhv, dwg, dwv, dbg, dbv = _ffn_backward(dact, ffn_gel, ffn_slope, hup, ffn_w_full)
    du2 = _matmul_sum([(dhg, "nn", D_FF, 0), (dhv, "nn", D_FF, D_FF)], wup_t, m=t_rows, n=d, tm=512, name="ffn_up_dx")
    g_wup_t = _matmul(dhg, u2, mode="tn", m=D_FF, n=d, k=t_rows, tm=1408, tn=1024, tk=2048, out_dtype=F32,
                      name="ffn_up_dw_gate", out_rows=2 * D_FF)
    g_wup_t = _matmul(dhv, u2, mode="tn", m=D_FF, n=d, k=t_rows, tm=1408, tn=1024, tk=2048, out_dtype=F32,
                      name="ffn_up_dw_value", out_rows=2 * D_FF, out_m0=D_FF, into=g_wup_t)
    ffn_grads = [g_wup_t.reshape(N_DEV, 2 * D_FF // N_DEV, d), g_wdown.reshape(N_DEV, D_FF // N_DEV, d)]
    red_send, red_recv, ffn_grads, red_lands, red_token = _scatter_start(
        ffn_grads, [lax.empty(g.shape, F32) for g in ffn_grads], False, "reduce_ffn_grads_start")
    dh1, dmixed, d_norm_ffn_pre, d_norm_mix_post = _mid_backward(
        dy, du2, h1, mixed, norm_ffn_pre + red_token[0:1, 0:1], norm_mix_post)
    dconv_out = _matmul(dmixed, wout_g, mode="nt", m=t_rows, n=CONV_WIDTH, k=d, tm=2048, tn=512, tk=d,
                        out_dtype=F32, name="out_proj_dx_conv")
    do_t = _matmul(wout_g, dmixed, mode="nt", m=ATTN_WIDTH, n=t_rows, k=d, tm=512, tn=2048, tk=d,
                   out_dtype=BF16, name="out_proj_dx_attn", a_m0=CONV_WIDTH)
    g_wout = _matmul(conv_out, dmixed, mode="tn", m=CONV_WIDTH, n=d, k=t_rows, tm=512, tn=1024, tk=2048, out_dtype=F32,
                     name="out_proj_dw_conv", out_rows=d)
    g_wout = _matmul(o_t, dmixed, mode="nn", m=ATTN_WIDTH, n=d, k=t_rows, tm=512, tn=1024, tk=2048, out_dtype=F32,
                     name="out_proj_dw_attn", out_rows=d, out_m0=CONV_WIDTH, into=g_wout)
    wout_handle = _scatter_start([g_wout.reshape(N_DEV, d // N_DEV, d)], [lax.empty((N_DEV, d // N_DEV, d), F32)],
                                 False, "reduce_w_out_grad_start")
    dproj_a, d_conv_w, d_conv_b, d_ln_g, d_ln_b = _conv_backward(
        dconv_out, conv_c, proj_a, conv_w_full, conv_ln_g + wout_handle[4][0:1, 0:1], conv_ln_b)
    dqkv_parts = _attn_backward(qkv_t, o_t, do_t, attn_lse, rel_bias[0])
    drel = dqkv_parts[3]
    du1 = _matmul_sum([(dproj_a, "nn", 2 * CONV_WIDTH, 0)]
                      + [(dqkv_parts[j], "tn", ATTN_WIDTH, 2 * CONV_WIDTH + j * ATTN_WIDTH) for j in range(3)],
                      win_t, m=t_rows, n=d, tm=1024, name="proj_dx")
    g_win_t =_matmul(dproj_a, u1, mode="tn", m=2 * CONV_WIDTH, n=d, k=t_rows, tm=1024, tn=1024, tk=2048, out_dtype=F32,
                      name="proj_dw_conv", out_rows=in_cols)
    for j, part in enumerate("qkv"):
        row0 = 2 * CONV_WIDTH + j * ATTN_WIDTH
        g_win_t = _matmul(dqkv_parts[j], u1, mode="nn", m=ATTN_WIDTH, n=d, k=t_rows, tm=512, tn=1024, tk=2048,
                          out_dtype=F32, name="proj_dw_" + part, out_rows=in_cols, out_m0=row0, into=g_win_t)
    win_handle = _scatter_start([g_win_t.reshape(N_DEV, in_cols // N_DEV, d)],
                                [lax.empty((N_DEV, in_cols // N_DEV, d), F32)], False, "reduce_w_in_grad_start")
    dx, d_norm_mix_pre = _input_backward(dh1, du1, x2, norm_mix_pre + win_handle[4][0:1, 0:1])

    small_grads = dict(norm_mix_pre=d_norm_mix_pre, conv_dw_b=d_conv_b, conv_ln_g=d_ln_g, conv_ln_b=d_ln_b,
                       rel_bias=drel[:, :2 * MAX_REL + 1], norm_mix_post=d_norm_mix_post, norm_ffn_pre=d_norm_ffn_pre,
                       ffn_dw_b=jnp.concatenate([dbg, dbv], axis=1), norm_ffn_post=d_norm_ffn_post)
    pieces = [small_grads[nm] for nm in SMALL] + [d_conv_w, jnp.concatenate([dwg, dwv], axis=1), loss]
    count = sum(p.size for p in pieces)
    (gathered_small,) = _all_gather([_pack(pieces, _rows_for(count))], "all_gather_small_grads")
    summed = _sum_devices(gathered_small)
    shapes = [weights[nm].shape for nm in SMALL] + [(CONV_K, CONV_WIDTH), (FFN_K, 2 * D_FF), (1, 1)]
    unpacked = _unpack(summed, shapes)
    grads = dict(zip(SMALL, unpacked[:len(SMALL)]))
    cw_shard, fw_shard = CONV_WIDTH // N_DEV, 2 * D_FF // N_DEV
    grads["conv_dw_w"] = lax.dynamic_slice_in_dim(unpacked[-3], my_dev * cw_shard, cw_shard, axis=1)[None]
    grads["ffn_dw_w"] = lax.dynamic_slice_in_dim(unpacked[-2], my_dev * fw_shard, fw_shard, axis=1)[None]
    total_loss = unpacked[-1].reshape(())

    me = jnp.reshape(my_dev, (1,)).astype(jnp.int32)
    delta, new_m, new_v = {}, {}, {}

    def finish(nm, send, recv, srcs, lands, after, transposed):
        srcs, lands = _scatter_wait(send, recv, srcs, lands, after, False, "reduce_" + nm + "_grad_wait")
        for name_a, src, land in zip(nm.split("_and_"), srcs, lands):
            g, dl, nm1, nv1 = _reduce_and_adamw(src, land, me, weights[name_a][0], mom1[name_a][0], mom2[name_a][0],
                                                transposed[name_a], "reduce_adamw_" + name_a)
            grads[name_a], delta[name_a], new_m[name_a], new_v[name_a] = g[None], dl[None], nm1[None], nv1[None]

    transposed = dict(w_in=True, w_out=False, w_up=True, w_down=False)
    finish("w_up_and_w_down", red_send, red_recv, ffn_grads, red_lands, dx, transposed)
    finish("w_out", *wout_handle[:4], dx, transposed)
    finish("w_in", *win_handle[:4], delta["w_up"], transposed)
    small_names = SMALL + SHARDED_SMALL
    small_count = sum(weights[nm].size for nm in small_names)
    small_rows = _rows_for(small_count)
    packed = [_pack([src[nm] for nm in small_names], small_rows) for src in (weights, grads, mom1, mom2)]
    outs = _adamw(*packed, "adamw_small")
    small_shapes = [weights[nm].shape for nm in small_names]
    for store, arr in zip((delta, new_m, new_v), outs):
        store.update(zip(small_names, _unpack(arr, small_shapes)))

    return (total_loss, dx[None], *[grads[nm] for nm in WEIGHTS], *[delta[nm] for nm in WEIGHTS],
            *[new_m[nm] for nm in WEIGHTS], *[new_v[nm] for nm in WEIGHTS])
```

```python
import jax
import jax.numpy as jnp
from jax import lax
from jax.experimental import pallas as pl
from jax.experimental.pallas import tpu as pltpu

F32 = jnp.float32
BF16 = jnp.bfloat16
MESH = pl.DeviceIdType.MESH
AXES = ("x", "y", "c")
N_DEV = 8

EPS = 1e-6
NEG_INF = -1e30
D_MODEL = 1024
CONV_WIDTH = 512
ATTN_WIDTH = 512
N_HEADS = 8
HEAD_DIM = 64
CHUNK = 64
LEFT = 8 * CHUNK
QBLK = 2 * CHUNK
WIN = LEFT + QBLK
CONV_K = 31
CONV_HALO = 32
FFN_K = 3
FFN_HALO = 8
D_FF = 2816
MAX_REL = 128
SCALE = HEAD_DIM ** -0.5
ADAM_LR, ADAM_B1, ADAM_B2, ADAM_EPS, ADAM_WD, ADAM_STEP = 0.001, 0.9, 0.999, 1e-08, 0.01, 10

V7X_VMEM_BYTES = 64 * 2**20
VMEM_LIMIT_BYTES = V7X_VMEM_BYTES - 8 * 2**20
LANES = 128


def _params(*sem):
    return pltpu.CompilerParams(dimension_semantics=sem or None, vmem_limit_bytes=VMEM_LIMIT_BYTES)


_DOT_DIMS = {"nn": (((1,), (0,)), ((), ())), "nt": (((1,), (1,)), ((), ())), "tn": (((0,), (0,)), ((), ()))}


def _matmul(a, b, *, mode, m, n, k, tm, tn, tk, out_dtype, name, a_m0=0, b_n0=0, b_k0=0, add=None,
            out_rows=None, out_m0=0, into=None):
    tm, tn, tk = min(tm, m), min(tn, n), min(tk, k)
    out_rows = m if out_rows is None else out_rows
    assert m % tm == 0 and n % tn == 0 and k % tk == 0, (name, m, n, k, tm, tn, tk)
    assert a_m0 % tm == 0 and b_n0 % tn == 0 and b_k0 % tk == 0 and out_m0 % tm == 0, name
    am, bn, bk, om = a_m0 // tm, b_n0 // tn, b_k0 // tk, out_m0 // tm
    gk = k // tk
    dims = _DOT_DIMS[mode]

    if mode == "tn":
        a_spec = pl.BlockSpec((tk, tm), lambda i, j, kk: (kk, i + am))
    else:
        a_spec = pl.BlockSpec((tm, tk), lambda i, j, kk: (i + am, kk))
    if mode == "nt":
        b_spec = pl.BlockSpec((tn, tk), lambda i, j, kk: (j + bn, kk + bk))
    else:
        b_spec = pl.BlockSpec((tk, tn), lambda i, j, kk: (kk + bk, j + bn))
    o_spec = pl.BlockSpec((tm, tn), lambda i, j, kk: (i + om, j))
    in_specs = [a_spec, b_spec]
    operands = [a, b]
    if add is not None:
        assert out_rows == m
        in_specs.append(o_spec)
        operands.append(add)
    aliases = {}
    if into is not None:
        aliases = {len(operands): 0}
        in_specs.append(pl.BlockSpec(memory_space=pl.ANY))
        operands.append(into)

    def body(*refs):
        a_ref, b_ref = refs[0], refs[1]
        add_ref = refs[2] if add is not None else None
        o_ref = refs[len(operands)]
        part = lax.dot_general(a_ref[...].astype(BF16), b_ref[...].astype(BF16), dims,
                               preferred_element_type=F32)

        def finish(total):
            if add_ref is not None:
                total = total + add_ref[...]
            o_ref[...] = total.astype(out_dtype)

        if gk == 1:
            finish(part)
        else:
            acc_ref = refs[-1]
            kk = pl.program_id(2)

            @pl.when(kk == 0)
            def _():
                acc_ref[...] = part

            @pl.when(kk > 0)
            def _():
                acc_ref[...] += part

            @pl.when(kk == gk - 1)
            def _():
                finish(acc_ref[...])

    return pl.pallas_call(
        body, name=name,
        grid=(m // tm, n // tn, gk),
        in_specs=in_specs, out_specs=o_spec,
        out_shape=jax.ShapeDtypeStruct((out_rows, n), out_dtype),
        scratch_shapes=[pltpu.VMEM((tm, tn), F32)] if gk > 1 else [],
        input_output_aliases=aliases,
        compiler_params=_params("parallel", "parallel", "arbitrary"),
    )(*operands)


def _matmul_sum(pieces, b, *, m, n, tm, name):
    tm = min(tm, m)
    assert m % tm == 0
    in_specs, operands = [], []
    for a, mode, k, k0 in pieces:
        assert k0 % k == 0
        if mode == "tn":
            in_specs.append(pl.BlockSpec((k, tm), lambda i: (0, i)))
        else:
            in_specs.append(pl.BlockSpec((tm, k), lambda i: (i, 0)))
        in_specs.append(pl.BlockSpec((k, n), lambda i, blk=k0 // k: (blk, 0)))
        operands += [a, b]

    def body(*refs):
        total = None
        for p, (_, mode, _, _) in enumerate(pieces):
            part = lax.dot_general(refs[2 * p][...], refs[2 * p + 1][...], _DOT_DIMS[mode], preferred_element_type=F32)
            total = part if total is None else total + part
        refs[-1][...] = total

    return pl.pallas_call(
        body, name=name, grid=(m // tm,),
        in_specs=in_specs, out_specs=pl.BlockSpec((tm, n), lambda i: (i, 0)),
        out_shape=jax.ShapeDtypeStruct((m, n), F32),
        compiler_params=_params("parallel"),
    )(*operands)


def _rms_hat(v):
    r = lax.rsqrt(jnp.mean(v * v, axis=-1, keepdims=True) + EPS)
    return v * r, r


def _rms_bwd(dn, hat, r):
    return r * (dn - hat * jnp.mean(dn * hat, axis=-1, keepdims=True))


def _sigmoid(v):
    return 1.0 / (1.0 + jnp.exp(-v))


_GELU_C = 0.7978845608028654


def _gelu(v):
    return 0.5 * v * (1.0 + jnp.tanh(_GELU_C * (v + 0.044715 * v * (v * v))))


def _gelu_parts(v):
    v2 = v * v
    t = jnp.tanh(_GELU_C * (v + 0.044715 * v * v2))
    cdf = 0.5 * (1.0 + t)
    dcdf = 0.5 * (1.0 - t * t) * _GELU_C * (1.0 + 3.0 * 0.044715 * v2)
    return v * cdf, cdf + v * dcdf


def _row_tile(t_rows, want):
    tile = min(want, t_rows)
    assert t_rows % tile == 0
    return tile


def _pre_norm(x, g, name):
    t_rows, d = x.shape
    tm = _row_tile(t_rows, 512)

    def body(x_ref, g_ref, u_ref):
        hat, _ = _rms_hat(x_ref[...])
        u_ref[...] = (hat * g_ref[...]).astype(BF16)

    return pl.pallas_call(
        body, name=name, grid=(t_rows // tm,),
        in_specs=[pl.BlockSpec((tm, d), lambda i: (i, 0)), pl.BlockSpec((1, d), lambda i: (0, 0))],
        out_specs=pl.BlockSpec((tm, d), lambda i: (i, 0)),
        out_shape=jax.ShapeDtypeStruct((t_rows, d), BF16),
        compiler_params=_params("parallel"),
    )(x, g)


def _mid_forward(x, mixed, g_post, g_pre):
    t_rows, d = x.shape
    tm = _row_tile(t_rows, 512)

    def body(x_ref, mixed_ref, gpost_ref, gpre_ref, h1_ref, u2_ref):
        hat, _ = _rms_hat(mixed_ref[...])
        h1 = x_ref[...] + hat * gpost_ref[...]
        h1_ref[...] = h1
        hat1, _ = _rms_hat(h1)
        u2_ref[...] = (hat1 * gpre_ref[...]).astype(BF16)

    row = pl.BlockSpec((tm, d), lambda i: (i, 0))
    vec = pl.BlockSpec((1, d), lambda i: (0, 0))
    return pl.pallas_call(
        body, name="mid_forward", grid=(t_rows // tm,),
        in_specs=[row, row, vec, vec], out_specs=[row, row],
        out_shape=[jax.ShapeDtypeStruct((t_rows, d), F32), jax.ShapeDtypeStruct((t_rows, d), BF16)],
        compiler_params=_params("parallel"),
    )(x, mixed, g_post, g_pre)


def _loss_and_head_backward(h1, f, target, g_post):
    t_rows, d = h1.shape
    tm = _row_tile(t_rows, 512)
    nt = t_rows // tm

    def body(h1_ref, f_ref, tgt_ref, g_ref, loss_ref, dy_ref, df_ref, dg_ref, sq_ref):
        i = pl.program_id(0)

        @pl.when(i == 0)
        def _():
            sq_ref[...] = jnp.zeros_like(sq_ref)
            dg_ref[...] = jnp.zeros_like(dg_ref)

        g = g_ref[...]
        hat, r = _rms_hat(f_ref[...])
        err = h1_ref[...] + hat * g - tgt_ref[...]
        sq_ref[...] += jnp.sum(err * err, axis=0, keepdims=True)
        dy = err * (1.0 / d)
        dy_ref[...] = dy
        dg_ref[...] += jnp.sum(dy * hat, axis=0, keepdims=True)
        df_ref[...] = _rms_bwd(dy * g, hat, r).astype(BF16)

        @pl.when(i == nt - 1)
        def _():
            loss_ref[...] = (0.5 / d) * jnp.sum(sq_ref[...], axis=1, keepdims=True)

    row = pl.BlockSpec((tm, d), lambda i: (i, 0))
    vec = pl.BlockSpec((1, d), lambda i: (0, 0))
    return pl.pallas_call(
        body, name="loss_head_backward", grid=(nt,),
        in_specs=[row, row, row, vec],
        out_specs=[pl.BlockSpec((1, 1), lambda i: (0, 0)), row, row, vec],
        out_shape=[jax.ShapeDtypeStruct((1, 1), F32), jax.ShapeDtypeStruct((t_rows, d), F32),
                   jax.ShapeDtypeStruct((t_rows, d), BF16), jax.ShapeDtypeStruct((1, d), F32)],
        scratch_shapes=[pltpu.VMEM((1, d), F32)],
        compiler_params=_params("arbitrary"),
    )(h1, f, target, g_post)


def _mid_backward(dy, du2, h1, mixed, g_pre, g_post):
    t_rows, d = dy.shape
    tm = _row_tile(t_rows, 512)

    def body(dy_ref, du2_ref, h1_ref, mixed_ref, gpre_ref, gpost_ref, dh1_ref, dmixed_ref, dgpre_ref, dgpost_ref):
        @pl.when(pl.program_id(0) == 0)
        def _():
            dgpre_ref[...] = jnp.zeros_like(dgpre_ref)
            dgpost_ref[...] = jnp.zeros_like(dgpost_ref)

        du2 = du2_ref[...]
        hat1, r1 = _rms_hat(h1_ref[...])
        dgpre_ref[...] += jnp.sum(du2 * hat1, axis=0, keepdims=True)
        dh1 = dy_ref[...] + _rms_bwd(du2 * gpre_ref[...], hat1, r1)
        dh1_ref[...] = dh1
        hatm, rm = _rms_hat(mixed_ref[...])
        dgpost_ref[...] += jnp.sum(dh1 * hatm, axis=0, keepdims=True)
        dmixed_ref[...] = _rms_bwd(dh1 * gpost_ref[...], hatm, rm).astype(BF16)

    row = pl.BlockSpec((tm, d), lambda i: (i, 0))
    vec = pl.BlockSpec((1, d), lambda i: (0, 0))
    return pl.pallas_call(
        body, name="mid_backward", grid=(t_rows // tm,),
        in_specs=[row, row, row, row, vec, vec], out_specs=[row, row, vec, vec],
        out_shape=[jax.ShapeDtypeStruct((t_rows, d), F32), jax.ShapeDtypeStruct((t_rows, d), BF16),
                   jax.ShapeDtypeStruct((1, d), F32), jax.ShapeDtypeStruct((1, d), F32)],
        compiler_params=_params("arbitrary"),
    )(dy, du2, h1, mixed, g_pre, g_post)


def _input_backward(dh1, du1, x, g_pre):
    t_rows, d = x.shape
    tm = _row_tile(t_rows, 512)

    def body(dh1_ref, du1_ref, x_ref, g_ref, dx_ref, dg_ref):
        @pl.when(pl.program_id(0) == 0)
        def _():
            dg_ref[...] = jnp.zeros_like(dg_ref)

        du1 = du1_ref[...]
        hat, r = _rms_hat(x_ref[...])
        dg_ref[...] += jnp.sum(du1 * hat, axis=0, keepdims=True)
        dx_ref[...] = dh1_ref[...] + _rms_bwd(du1 * g_ref[...], hat, r)

    row = pl.BlockSpec((tm, d), lambda i: (i, 0))
    vec = pl.BlockSpec((1, d), lambda i: (0, 0))
    return pl.pallas_call(
        body, name="input_backward", grid=(t_rows // tm,),
        in_specs=[row, row, row, vec], out_specs=[row, vec],
        out_shape=[jax.ShapeDtypeStruct((t_rows, d), F32), jax.ShapeDtypeStruct((1, d), F32)],
        compiler_params=_params("arbitrary"),
    )(dh1, du1, x, g_pre)


CONV_STRIP = 32


def _glu(block):
    return block[:, :CONV_WIDTH] * _sigmoid(block[:, CONV_WIDTH:])


def _layer_norm_parts(c):
    mu = jnp.mean(c, axis=-1, keepdims=True)
    xc = c - mu
    r = lax.rsqrt(jnp.mean(xc * xc, axis=-1, keepdims=True) + EPS)
    return xc * r, r


CONV_WINDOW = 2 * CONV_STRIP
SHIFTED_ROWS = CONV_WINDOW - 8


def _shifted_copies(v, shifted):
    for s in range(1, 8):
        shifted[s] = v[s:s + SHIFTED_ROWS, :]


def _window_rows(v, shifted, start):
    s, a = start % 8, start - start % 8
    return v[a:a + CONV_STRIP, :] if s == 0 else shifted[s, a:a + CONV_STRIP, :]


def _conv_forward(proj_a, w, b, ln_g, ln_b):
    t_rows = proj_a.shape[0]
    tm = _row_tile(t_rows, 512)
    hb = tm // CONV_HALO
    cw = CONV_WIDTH

    def body(cur_ref, prev_ref, w_ref, b_ref, g_ref, beta_ref, c_ref, out_ref, hbuf, shifted):
        i = pl.program_id(0)
        hbuf[0:CONV_HALO, :] = jnp.where(i > 0, _glu(prev_ref[...]), 0.0)
        hbuf[CONV_HALO:, :] = _glu(cur_ref[...])

        def strip(s, carry):
            base = pl.multiple_of(s * CONV_STRIP, CONV_STRIP)
            v = hbuf[pl.ds(base, CONV_WINDOW), :]
            _shifted_copies(v, shifted)
            acc = jnp.broadcast_to(b_ref[...], (CONV_STRIP, cw))
            off = CONV_HALO - (CONV_K - 1)
            for kk in range(CONV_K):
                acc = acc + w_ref[kk:kk + 1, :] * _window_rows(v, shifted, off + kk)
            c_ref[pl.ds(base, CONV_STRIP), :] = acc
            hat, _ = _layer_norm_parts(acc)
            z = hat * g_ref[...] + beta_ref[...]
            out_ref[pl.ds(base, CONV_STRIP), :] = (z * _sigmoid(z)).astype(BF16)
            return carry

        lax.fori_loop(0, tm // CONV_STRIP, strip, 0)

    vec = pl.BlockSpec((1, cw), lambda i: (0, 0))
    return pl.pallas_call(
        body, name="conv_forward", grid=(t_rows // tm,),
        in_specs=[pl.BlockSpec((tm, 2 * cw), lambda i: (i, 0)),
                  pl.BlockSpec((CONV_HALO, 2 * cw), lambda i: (jnp.maximum(i * hb - 1, 0), 0)),
                  pl.BlockSpec((CONV_K, cw), lambda i: (0, 0)), vec, vec, vec],
        out_specs=[pl.BlockSpec((tm, cw), lambda i: (i, 0)), pl.BlockSpec((tm, cw), lambda i: (i, 0))],
        out_shape=[jax.ShapeDtypeStruct((t_rows, cw), F32), jax.ShapeDtypeStruct((t_rows, cw), BF16)],
        scratch_shapes=[pltpu.VMEM((tm + CONV_HALO, cw), F32), pltpu.VMEM((8, SHIFTED_ROWS, cw), F32)],
        compiler_params=_params("parallel"),
    )(proj_a, proj_a, w, b, ln_g, ln_b)


def _conv_backward(dout, c, proj_a, w, ln_g, ln_b):
    t_rows = c.shape[0]
    tm = _row_tile(t_rows, 512)
    hb = tm // CONV_HALO
    nt = t_rows // tm
    last_halo = t_rows // CONV_HALO - 1
    cw = CONV_WIDTH

    def body(dout_ref, dout_next_ref, c_ref, c_next_ref, cur_ref, prev_ref, w_ref, g_ref, beta_ref,
             dproj_ref, dw_ref, db_ref, dg_ref, dbeta_ref, hbuf, dcbuf, dwacc, h_shifted, d_shifted):
        i = pl.program_id(0)

        @pl.when(i == 0)
        def _():
            dwacc[...] = jnp.zeros_like(dwacc)
            db_ref[...] = jnp.zeros_like(db_ref)
            dg_ref[...] = jnp.zeros_like(dg_ref)
            dbeta_ref[...] = jnp.zeros_like(dbeta_ref)

        def ln_swish_backward(dout_v, c_v):
            hat, r = _layer_norm_parts(c_v)
            g = g_ref[...]
            z = hat * g + beta_ref[...]
            sg = _sigmoid(z)
            dz = dout_v * (sg * (1.0 + z * (1.0 - sg)))
            dhat = dz * g
            dc = r * (dhat - jnp.mean(dhat, axis=-1, keepdims=True)
                      - hat * jnp.mean(dhat * hat, axis=-1, keepdims=True))
            return dc, dz, hat

        dc, dz, hat = ln_swish_backward(dout_ref[...], c_ref[...])
        dg_ref[...] += jnp.sum(dz * hat, axis=0, keepdims=True)
        dbeta_ref[...] += jnp.sum(dz, axis=0, keepdims=True)
        db_ref[...] += jnp.sum(dc, axis=0, keepdims=True)
        dcbuf[0:tm, :] = dc
        dc_next, _, _ = ln_swish_backward(dout_next_ref[...], c_next_ref[...])
        dcbuf[tm:, :] = jnp.where(i < nt - 1, dc_next, 0.0)

        hbuf[0:CONV_HALO, :] = jnp.where(i > 0, _glu(prev_ref[...]), 0.0)
        hbuf[CONV_HALO:, :] = _glu(cur_ref[...])

        def strip(s, carry):
            base = pl.multiple_of(s * CONV_STRIP, CONV_STRIP)
            dv = dcbuf[pl.ds(base, CONV_WINDOW), :]
            hv = hbuf[pl.ds(base, CONV_WINDOW), :]
            _shifted_copies(dv, d_shifted)
            _shifted_copies(hv, h_shifted)
            dcs = dv[0:CONV_STRIP, :]
            dh = jnp.zeros((CONV_STRIP, cw), F32)
            off = CONV_HALO - (CONV_K - 1)
            for kk in range(CONV_K):
                back = CONV_K - 1 - kk
                dh = dh + w_ref[kk:kk + 1, :] * _window_rows(dv, d_shifted, back)
                prod = dcs * _window_rows(hv, h_shifted, off + kk)
                dwacc[kk] += jnp.sum(prod.reshape(CONV_STRIP // 8, 8, cw), axis=0)
            blk = cur_ref[pl.ds(base, CONV_STRIP), :]
            val, sg = blk[:, :cw], _sigmoid(blk[:, cw:])
            dproj_ref[pl.ds(base, CONV_STRIP), 0:cw] = (dh * sg).astype(BF16)
            dproj_ref[pl.ds(base, CONV_STRIP), cw:2 * cw] = (dh * val * sg * (1.0 - sg)).astype(BF16)
            return carry

        lax.fori_loop(0, tm // CONV_STRIP, strip, 0)

        @pl.when(i == nt - 1)
        def _():
            for kk in range(CONV_K):
                dw_ref[kk:kk + 1, :] = jnp.sum(dwacc[kk], axis=0, keepdims=True)

    vec = pl.BlockSpec((1, cw), lambda i: (0, 0))
    cur = lambda width: pl.BlockSpec((tm, width), lambda i: (i, 0))
    nxt = lambda width: pl.BlockSpec((CONV_HALO, width), lambda i: (jnp.minimum((i + 1) * hb, last_halo), 0))
    return pl.pallas_call(
        body, name="conv_backward", grid=(nt,),
        in_specs=[cur(cw), nxt(cw), cur(cw), nxt(cw), cur(2 * cw),
                  pl.BlockSpec((CONV_HALO, 2 * cw), lambda i: (jnp.maximum(i * hb - 1, 0), 0)),
                  pl.BlockSpec((CONV_K, cw), lambda i: (0, 0)), vec, vec],
        out_specs=[cur(2 * cw), pl.BlockSpec((CONV_K, cw), lambda i: (0, 0)), vec, vec, vec],
        out_shape=[jax.ShapeDtypeStruct((t_rows, 2 * cw), BF16), jax.ShapeDtypeStruct((CONV_K, cw), F32),
                   jax.ShapeDtypeStruct((1, cw), F32), jax.ShapeDtypeStruct((1, cw), F32),
                   jax.ShapeDtypeStruct((1, cw), F32)],
        scratch_shapes=[pltpu.VMEM((tm + CONV_HALO, cw), F32), pltpu.VMEM((tm + CONV_HALO, cw), F32),
                        pltpu.VMEM((CONV_K, 8, cw), F32), pltpu.VMEM((8, SHIFTED_ROWS, cw), F32),
                        pltpu.VMEM((8, SHIFTED_ROWS, cw), F32)],
        compiler_params=_params("arbitrary"),
    )(dout, dout, c, c, proj_a, proj_a, w, ln_g, ln_b)


def _attn_load_kv(kv_hbm, k_pad, v_pad, sem, t_cols):
    k_pad[:, 0:LEFT] = jnp.zeros((ATTN_WIDTH, LEFT), BF16)
    v_pad[:, 0:LEFT] = jnp.zeros((ATTN_WIDTH, LEFT), BF16)
    ck = pltpu.make_async_copy(kv_hbm.at[pl.ds(ATTN_WIDTH, ATTN_WIDTH), :], k_pad.at[:, pl.ds(LEFT, t_cols)], sem.at[0])
    cv = pltpu.make_async_copy(kv_hbm.at[pl.ds(2 * ATTN_WIDTH, ATTN_WIDTH), :], v_pad.at[:, pl.ds(LEFT, t_cols)], sem.at[1])
    ck.start()
    cv.start()
    ck.wait()
    cv.wait()


def _attn_build_bias(tab_ref, bias_t):
    row = lax.broadcasted_iota(jnp.int32, (LANES, LANES), 0)
    lane = lax.broadcasted_iota(jnp.int32, (LANES, LANES), 1)
    upper = lane >= row
    lane64 = lax.broadcasted_iota(jnp.int32, (CHUNK, LANES), 1)
    for h in range(N_HEADS):
        far = jnp.broadcast_to(tab_ref[h:h + 1, 2 * MAX_REL:2 * MAX_REL + 1], (LANES, LANES))
        hi = jnp.broadcast_to(tab_ref[h:h + 1, MAX_REL:2 * MAX_REL], (LANES, LANES))
        lo = jnp.broadcast_to(tab_ref[h:h + 1, 0:MAX_REL], (LANES, LANES))
        hi_d = pltpu.roll(hi, 0, 1, stride=1, stride_axis=0)
        lo_d = pltpu.roll(lo, 0, 1, stride=1, stride_axis=0)
        bias_t[h, 0:WIN - 2 * LANES, :] = jnp.broadcast_to(far[0:1, :], (WIN - 2 * LANES, LANES))
        bias_t[h, WIN - 2 * LANES:WIN - LANES, :] = jnp.where(upper, far, hi_d)
        bias_t[h, WIN - LANES:WIN, :] = jnp.where(upper, hi_d, lo_d)
        bias_t[h, 0:CHUNK, :] = jnp.where(lane64 < CHUNK, bias_t[h, 0:CHUNK, :], NEG_INF)
        bias_t[h, WIN - CHUNK:WIN, :] = jnp.where(lane64 >= CHUNK, bias_t[h, WIN - CHUNK:WIN, :], NEG_INF)


def _head_rows(h):
    return slice(h * HEAD_DIM, (h + 1) * HEAD_DIM)


def _attn_scores(k_pad, q_ref, s_buf, w0):
    for h in range(N_HEADS):
        q_h = q_ref[_head_rows(h), :] * jnp.asarray(SCALE, BF16)
        s_buf[h] = lax.dot_general(k_pad[_head_rows(h), pl.ds(w0, WIN)], q_h, _DOT_DIMS["tn"],
                                   preferred_element_type=F32)


def _attn_logits(s, bias, first_valid, key0=0):
    s = s + bias
    if first_valid is not None:
        s = jnp.where(lax.broadcasted_iota(jnp.int32, s.shape, 0) + key0 >= first_valid, s, NEG_INF)
    return s


def _attn_probs(s, bias_h, first_valid):
    s = _attn_logits(s, bias_h, first_valid)
    top = jnp.max(s, axis=0, keepdims=True)
    e = jnp.exp(s - top)
    total = jnp.sum(e, axis=0, keepdims=True)
    return e * (1.0 / total), top + jnp.log(total)


def _attn_by_padding(m, fn):
    @pl.when(m < LEFT // QBLK)
    def _():
        fn(LEFT - m * QBLK)

    @pl.when(m >= LEFT // QBLK)
    def _():
        fn(None)


def _attn_forward(qkv_t, rel_bias):
    t_cols = qkv_t.shape[1]
    steps = t_cols // QBLK

    def body(q_ref, kv_hbm, tab_ref, o_ref, lse_ref, k_pad, v_pad, bias_t, s_buf, p_buf, sem):
        m = pl.program_id(0)

        @pl.when(m == 0)
        def _():
            _attn_build_bias(tab_ref, bias_t)
            _attn_load_kv(kv_hbm, k_pad, v_pad, sem, t_cols)

        w0 = pl.multiple_of(m * QBLK, QBLK)
        _attn_scores(k_pad, q_ref, s_buf, w0)

        def softmax(first_valid):
            for h in range(N_HEADS):
                p, lse = _attn_probs(s_buf[h], bias_t[h], first_valid)
                p_buf[h] = p.astype(BF16)
                lse_ref[h:h + 1, :] = lse

        _attn_by_padding(m, softmax)
        for h in range(N_HEADS):
            o_h = lax.dot_general(v_pad[_head_rows(h), pl.ds(w0, WIN)], p_buf[h], _DOT_DIMS["nn"],
                                  preferred_element_type=F32)
            o_ref[_head_rows(h), :] = o_h.astype(BF16)

    return pl.pallas_call(
        body, name="attn_forward", grid=(steps,),
        in_specs=[pl.BlockSpec((ATTN_WIDTH, QBLK), lambda m: (0, m)),
                  pl.BlockSpec(memory_space=pl.ANY),
                  pl.BlockSpec((N_HEADS, 2 * MAX_REL + 1), lambda m: (0, 0))],
        out_specs=[pl.BlockSpec((ATTN_WIDTH, QBLK), lambda m: (0, m)), pl.BlockSpec((N_HEADS, QBLK), lambda m: (0, m))],
        out_shape=[jax.ShapeDtypeStruct((ATTN_WIDTH, t_cols), BF16), jax.ShapeDtypeStruct((N_HEADS, t_cols), F32)],
        scratch_shapes=[pltpu.VMEM((ATTN_WIDTH, LEFT + t_cols), BF16), pltpu.VMEM((ATTN_WIDTH, LEFT + t_cols), BF16),
                        pltpu.VMEM((N_HEADS, WIN, QBLK), F32), pltpu.VMEM((N_HEADS, WIN, QBLK), F32),
                        pltpu.VMEM((N_HEADS, WIN, QBLK), BF16), pltpu.SemaphoreType.DMA((2,))],
        compiler_params=_params("arbitrary"),
    )(qkv_t, qkv_t, rel_bias)


def _reverse_lanes(v, flip):
    out = jnp.zeros(v.shape, F32)
    rest = v
    for _ in range(3):
        piece = rest.astype(BF16)
        out = out + lax.dot_general(piece, flip, _DOT_DIMS["nn"], preferred_element_type=F32)
        rest = rest - piece.astype(F32)
    return out


def _attn_bias_grad(dbias_t, drel_ref):
    row = lax.broadcasted_iota(jnp.int32, (LANES, LANES), 0)
    lane = lax.broadcasted_iota(jnp.int32, (LANES, LANES), 1)
    flip = (row + lane == LANES - 1).astype(BF16)
    head = lax.broadcasted_iota(jnp.int32, (N_HEADS, LANES), 0)
    lane8 = lax.broadcasted_iota(jnp.int32, (N_HEADS, LANES), 1)
    upper_rev = jnp.zeros((N_HEADS, LANES), F32)
    lower_rev = jnp.zeros((N_HEADS, LANES), F32)
    far = jnp.zeros((N_HEADS, LANES), F32)
    for h in range(N_HEADS):
        def diagonals(block):
            skew = pltpu.roll(_reverse_lanes(block, flip), 0, 1, stride=1, stride_axis=0)
            pos = jnp.sum(jnp.where(lane >= row, skew, 0.0), axis=0, keepdims=True)
            neg = jnp.sum(jnp.where(lane < row, skew, 0.0), axis=0, keepdims=True)
            return pos, neg

        pos4, neg4 = diagonals(dbias_t[h, WIN - LANES:WIN, :])
        pos3, neg3 = diagonals(dbias_t[h, WIN - 2 * LANES:WIN - LANES, :])
        far_h = jnp.sum(jnp.sum(dbias_t[h, 0:WIN - 2 * LANES, :], axis=0, keepdims=True), axis=1, keepdims=True)
        far_h = far_h + jnp.sum(pos3, axis=1, keepdims=True)
        upper_rev = jnp.where(head == h, pos4 + neg3, upper_rev)
        lower_rev = jnp.where(head == h, neg4, lower_rev)
        far = jnp.where((head == h) & (lane8 == 0), far_h, far)
    drel_ref[:, 0:LANES] = _reverse_lanes(lower_rev, flip)
    drel_ref[:, LANES:2 * LANES] = _reverse_lanes(upper_rev, flip)
    drel_ref[:, 2 * LANES:3 * LANES] = far


def _attn_backward(qkv_t, o_t, do_t, lse, rel_bias):
    t_cols = qkv_t.shape[1]
    steps = t_cols // QBLK
    flush = LEFT // QBLK
    total = steps + flush

    def body(q_ref, o_ref, do_ref, lse_ref, kv_hbm, tab_ref, dq_ref, dk_ref, dv_ref, drel_ref,
             k_pad, v_pad, bias_t, dbias_t, dk_acc, dv_acc, s_buf, dp_buf, p_buf, ds_buf, sem):
        m = pl.program_id(0)

        @pl.when(m == 0)
        def _():
            _attn_build_bias(tab_ref, bias_t)
            _attn_load_kv(kv_hbm, k_pad, v_pad, sem, t_cols)
            dbias_t[...] = jnp.zeros_like(dbias_t)
            dk_acc[...] = jnp.zeros_like(dk_acc)
            dv_acc[...] = jnp.zeros_like(dv_acc)

        @pl.when(m < steps)
        def _():
            w0 = pl.multiple_of(m * QBLK, QBLK)
            _attn_scores(k_pad, q_ref, s_buf, w0)
            for h in range(N_HEADS):
                dp_buf[h] = lax.dot_general(v_pad[_head_rows(h), pl.ds(w0, WIN)], do_ref[_head_rows(h), :],
                                            _DOT_DIMS["tn"], preferred_element_type=F32)

            def softmax_backward(first_valid):
                for h in range(N_HEADS):
                    rows = _head_rows(h)
                    delta = jnp.sum(do_ref[rows, :].astype(F32) * o_ref[rows, :].astype(F32), axis=0, keepdims=True)
                    lse_h = lse_ref[h:h + 1, :]
                    for b in range(WIN // LANES):
                        keys = slice(b * LANES, (b + 1) * LANES)
                        s = _attn_logits(s_buf[h, keys, :], bias_t[h, keys, :], first_valid, b * LANES)
                        p = jnp.exp(s - lse_h)
                        ds = p * (dp_buf[h, keys, :] - delta)
                        dbias_t[h, keys, :] += ds
                        p_buf[h, keys, :] = p.astype(BF16)
                        ds_buf[h, keys, :] = (ds * SCALE).astype(BF16)

            _attn_by_padding(m, softmax_backward)
            for h in range(N_HEADS):
                rows = _head_rows(h)
                dq_h = lax.dot_general(k_pad[rows, pl.ds(w0, WIN)], ds_buf[h], _DOT_DIMS["nn"], preferred_element_type=F32)
                dq_ref[rows, :] = dq_h.astype(BF16)
                dk_h = lax.dot_general(q_ref[rows, :], ds_buf[h], _DOT_DIMS["nt"], preferred_element_type=F32)
                dv_h = lax.dot_general(do_ref[rows, :], p_buf[h], _DOT_DIMS["nt"], preferred_element_type=F32)
                for b in range(WIN // QBLK):
                    slot = pl.multiple_of(lax.rem(m + b, WIN // QBLK) * QBLK, QBLK)
                    dk_acc[rows, pl.ds(slot, QBLK)] += dk_h[:, b * QBLK:(b + 1) * QBLK]
                    dv_acc[rows, pl.ds(slot, QBLK)] += dv_h[:, b * QBLK:(b + 1) * QBLK]

        oldest = pl.multiple_of(lax.rem(m, WIN // QBLK) * QBLK, QBLK)
        dk_ref[...] = dk_acc[:, pl.ds(oldest, QBLK)].astype(BF16)
        dv_ref[...] = dv_acc[:, pl.ds(oldest, QBLK)].astype(BF16)
        dk_acc[:, pl.ds(oldest, QBLK)] = jnp.zeros((ATTN_WIDTH, QBLK), F32)
        dv_acc[:, pl.ds(oldest, QBLK)] = jnp.zeros((ATTN_WIDTH, QBLK), F32)

        @pl.when(m == total - 1)
        def _():
            _attn_bias_grad(dbias_t, drel_ref)

    qblk = pl.BlockSpec((ATTN_WIDTH, QBLK), lambda m: (0, jnp.minimum(m, steps - 1)))
    kblk = pl.BlockSpec((ATTN_WIDTH, QBLK), lambda m: (0, jnp.maximum(m - flush, 0)))
    dq, dk, dv, drel = pl.pallas_call(
        body, name="attn_backward", grid=(total,),
        in_specs=[qblk, qblk, qblk, pl.BlockSpec((N_HEADS, QBLK), lambda m: (0, jnp.minimum(m, steps - 1))),
                  pl.BlockSpec(memory_space=pl.ANY), pl.BlockSpec((N_HEADS, 2 * MAX_REL + 1), lambda m: (0, 0))],
        out_specs=[qblk, kblk, kblk, pl.BlockSpec((N_HEADS, 3 * LANES), lambda m: (0, 0))],
        out_shape=[jax.ShapeDtypeStruct((ATTN_WIDTH, t_cols), BF16)] * 3
        + [jax.ShapeDtypeStruct((N_HEADS, 3 * LANES), F32)],
        scratch_shapes=[pltpu.VMEM((ATTN_WIDTH, LEFT + t_cols), BF16), pltpu.VMEM((ATTN_WIDTH, LEFT + t_cols), BF16),
                        pltpu.VMEM((N_HEADS, WIN, QBLK), F32), pltpu.VMEM((N_HEADS, WIN, QBLK), F32),
                        pltpu.VMEM((ATTN_WIDTH, WIN), F32), pltpu.VMEM((ATTN_WIDTH, WIN), F32),
                        pltpu.VMEM((N_HEADS, WIN, QBLK), F32), pltpu.VMEM((N_HEADS, WIN, QBLK), F32),
                        pltpu.VMEM((N_HEADS, WIN, QBLK), BF16), pltpu.VMEM((N_HEADS, WIN, QBLK), BF16),
                        pltpu.SemaphoreType.DMA((2,))],
        compiler_params=_params("arbitrary"),
    )(qkv_t, o_t, do_t, lse, qkv_t, rel_bias)
    return dq, dk, dv, drel


FFN_TC = D_FF // 2
FFN_TR = 256


def _ffn_specs(t_rows, tr):
    nj = D_FF // FFN_TC
    hb = tr // FFN_HALO
    last_halo = t_rows // FFN_HALO - 1
    cur = lambda off: pl.BlockSpec((tr, FFN_TC), lambda j, i: (i, j + off))
    prev = lambda off: pl.BlockSpec((FFN_HALO, FFN_TC), lambda j, i: (jnp.maximum(i * hb - 1, 0), j + off))
    nxt = lambda off: pl.BlockSpec((FFN_HALO, FFN_TC), lambda j, i: (jnp.minimum((i + 1) * hb, last_halo), j + off))
    wspec = lambda off: pl.BlockSpec((FFN_K, FFN_TC), lambda j, i: (0, j + off))
    bspec = lambda off: pl.BlockSpec((1, FFN_TC), lambda j, i: (0, j + off))
    return nj, cur, prev, nxt, wspec, bspec


FFN_STRIP = 16


def _ffn_conv(win, w, b, rows):
    out = b + w[2] * win[FFN_HALO:FFN_HALO + rows, :]
    out = out + w[1] * win[FFN_HALO - 1:FFN_HALO - 1 + rows, :]
    return out + w[0] * win[FFN_HALO - 2:FFN_HALO - 2 + rows, :]


def _taps(w_ref):
    return [w_ref[kk:kk + 1, :] for kk in range(FFN_K)]


def _ffn_first_window(prev_ref, cur_ref, tile, rows):
    return jnp.concatenate([jnp.where(tile > 0, prev_ref[...], 0.0), cur_ref[0:rows, :]], axis=0)


def _fold8(v):
    return jnp.sum(v.reshape(v.shape[0] // 8, 8, v.shape[1]), axis=0)


def _ffn_activation(hup, w, b):
    t_rows = hup.shape[0]
    tr = _row_tile(t_rows, FFN_TR)
    nj, cur, prev, nxt, wspec, bspec = _ffn_specs(t_rows, tr)
    rs = FFN_STRIP

    def body(g_ref, gprev_ref, v_ref, vprev_ref, wg_ref, wv_ref, bg_ref, bv_ref, act_ref, gel_ref, slope_ref):
        i = pl.program_id(1)
        wg, wv, bg, bv = _taps(wg_ref), _taps(wv_ref), bg_ref[...], bv_ref[...]

        def emit(base, g_win, v_win):
            gel, dgel = _gelu_parts(_ffn_conv(g_win, wg, bg, rs))
            cv = _ffn_conv(v_win, wv, bv, rs)
            act_ref[pl.ds(base, rs), :] = (gel * cv).astype(BF16)
            gel_ref[pl.ds(base, rs), :] = gel
            slope_ref[pl.ds(base, rs), :] = cv * dgel

        def strip(s, carry):
            base = pl.multiple_of(s * rs, rs)
            emit(base, g_ref[pl.ds(base - FFN_HALO, rs + FFN_HALO), :], v_ref[pl.ds(base - FFN_HALO, rs + FFN_HALO), :])
            return carry

        emit(0, _ffn_first_window(gprev_ref, g_ref, i, rs), _ffn_first_window(vprev_ref, v_ref, i, rs))
        lax.fori_loop(1, tr // rs, strip, 0)

    return pl.pallas_call(
        body, name="ffn_activation", grid=(nj, t_rows // tr),
        in_specs=[cur(0), prev(0), cur(nj), prev(nj), wspec(0), wspec(nj), bspec(0), bspec(nj)],
        out_specs=[cur(0), cur(0), cur(0)],
        out_shape=[jax.ShapeDtypeStruct((t_rows, D_FF), BF16), jax.ShapeDtypeStruct((t_rows, D_FF), F32),
                   jax.ShapeDtypeStruct((t_rows, D_FF), F32)],
        compiler_params=_params("parallel", "parallel"),
    )(hup, hup, hup, hup, w, w, b, b)


def _ffn_backward(dact, gel, slope, hup, w):
    t_rows = hup.shape[0]
    tr = _row_tile(t_rows, FFN_TR)
    nj, cur, prev, nxt, wspec, bspec = _ffn_specs(t_rows, tr)
    ni = t_rows // tr
    rs = FFN_STRIP
    ns = tr // rs

    def body(da_ref, danext_ref, gel_ref, gelnext_ref, slope_ref, slopenext_ref, g_ref, gprev_ref, v_ref, vprev_ref,
             wg_ref, wv_ref, dhg_ref, dhv_ref, dwg_ref, dwv_ref, dbg_ref, dbv_ref, sums):
        i = pl.program_id(1)

        @pl.when(i == 0)
        def _():
            sums[...] = jnp.zeros_like(sums)

        wg, wv = _taps(wg_ref), _taps(wv_ref)
        da_after = jnp.where(i < ni - 1, danext_ref[...], 0.0)

        def strip_at(base, g_win, v_win, carry):
            da = da_ref[pl.ds(base, rs), :]
            dcg, dcv = da * slope_ref[pl.ds(base, rs), :], da * gel_ref[pl.ds(base, rs), :]
            out = []
            for half_i, (dc, after, taps, win, dh_ref) in enumerate(((dcg, carry[0], wg, g_win, dhg_ref),
                                                                   (dcv, carry[1], wv, v_win, dhv_ref))):
                ext = jnp.concatenate([dc, after], axis=0)
                dh = taps[2] * dc + taps[1] * ext[1:1 + rs, :] + taps[0] * ext[2:2 + rs, :]
                dh_ref[pl.ds(base, rs), :] = dh.astype(BF16)
                sums[4 * half_i] += _fold8(dc)
                for kk in range(FFN_K):
                    sums[4 * half_i + 1 + kk] += _fold8(dc * win[FFN_HALO - 2 + kk:FFN_HALO - 2 + kk + rs, :])
                out.append(dc[0:FFN_HALO, :])
            return tuple(out)

        def strip(s, carry):
            base = pl.multiple_of((ns - 1 - s) * rs, rs)
            return strip_at(base, g_ref[pl.ds(base - FFN_HALO, rs + FFN_HALO), :],
                            v_ref[pl.ds(base - FFN_HALO, rs + FFN_HALO), :], carry)

        carry = lax.fori_loop(0, ns - 1, strip, (da_after * slopenext_ref[...], da_after * gelnext_ref[...]))
        strip_at(0, _ffn_first_window(gprev_ref, g_ref, i, rs), _ffn_first_window(vprev_ref, v_ref, i, rs), carry)

        @pl.when(i == ni - 1)
        def _():
            for half_i, (db_ref, dw_ref) in enumerate(((dbg_ref, dwg_ref), (dbv_ref, dwv_ref))):
                db_ref[...] = jnp.sum(sums[4 * half_i], axis=0, keepdims=True)
                for kk in range(FFN_K):
                    dw_ref[kk:kk + 1, :] = jnp.sum(sums[4 * half_i + 1 + kk], axis=0, keepdims=True)

    half = jax.ShapeDtypeStruct((t_rows, D_FF), BF16)
    return pl.pallas_call(
        body, name="ffn_backward", grid=(nj, ni),
        in_specs=[cur(0), nxt(0), cur(0), nxt(0), cur(0), nxt(0), cur(0), prev(0), cur(nj), prev(nj),
                  wspec(0), wspec(nj)],
        out_specs=[cur(0), cur(0), wspec(0), wspec(0), bspec(0), bspec(0)],
        out_shape=[half, half, jax.ShapeDtypeStruct((FFN_K, D_FF), F32), jax.ShapeDtypeStruct((FFN_K, D_FF), F32),
                   jax.ShapeDtypeStruct((1, D_FF), F32), jax.ShapeDtypeStruct((1, D_FF), F32)],
        scratch_shapes=[pltpu.VMEM((2 * (1 + FFN_K), 8, FFN_TC), F32)],
        compiler_params=_params("parallel", "arbitrary"),
    )(dact, dact, gel, gel, slope, slope, hup, hup, hup, hup, w, w)


def _mesh_position():
    return lax.axis_index("x"), lax.axis_index("y"), lax.axis_index("c")


def _hbm_specs(n):
    return [pl.BlockSpec(memory_space=pl.ANY)] * n


def _all_gather(shards, name, placed=None):
    n = len(shards)

    def body(*refs):
        ins = refs[:n]
        outs = refs[2 * n:3 * n] if placed else refs[n:2 * n]
        send_sems, recv_sems, local_sems = refs[-3:]
        x, y, c = _mesh_position()
        me, sibling = (x, y, c), (x, y, 1 - c)
        chips = [(1 - x, y), (x, 1 - y), (1 - x, 1 - y)]

        def copy(a, slot, block, to, src=None):
            dst = outs[a].at[4 * block[0] + 2 * block[1] + block[2]]
            return pltpu.make_async_remote_copy(
                src_ref=dst if src is None else src, dst_ref=dst,
                send_sem=send_sems.at[a, slot], recv_sem=recv_sems.at[a, slot],
                device_id=to, device_id_type=MESH)

        started = []
        for a in range(0 if placed else n):
            mine = pltpu.make_async_copy(ins[a], outs[a].at[4 * x + 2 * y + c], local_sems.at[a])
            mine.start()
            started.append(mine)
        first = []
        for a in range(n):
            first.append(copy(a, 0, me, sibling, src=ins[a]))
            first += [copy(a, 1 + j, me, (*chip, c), src=ins[a]) for j, chip in enumerate(chips)]
        for cp in first:
            cp.start()
        passed = []
        for j, chip in enumerate(chips):
            for a in range(n):
                copy(a, 1 + j, (*chip, c), me).wait_recv()
                fwd = copy(a, 4 + j, (*chip, c), sibling)
                fwd.start()
                passed.append(fwd)
        for a in range(n):
            copy(a, 0, sibling, me).wait_recv()
            for j, chip in enumerate(chips):
                copy(a, 4 + j, (*chip, 1 - c), me).wait_recv()
        for cp in first + passed:
            cp.wait_send()
        for mine in started:
            mine.wait()

    operands = [*shards, *placed] if placed else list(shards)
    return pl.pallas_call(
        body, name=name,
        in_specs=_hbm_specs(len(operands)), out_specs=_hbm_specs(n),
        out_shape=[jax.ShapeDtypeStruct((N_DEV,) + s.shape, s.dtype) for s in shards],
        scratch_shapes=[pltpu.SemaphoreType.DMA((n, 7)), pltpu.SemaphoreType.DMA((n, 7)),
                        pltpu.SemaphoreType.DMA((n,))],
        input_output_aliases={n + a: a for a in range(n)} if placed else {},
        compiler_params=pltpu.CompilerParams(has_side_effects=True),
    )(*operands)


def _place_own(shards, me):
    n = len(shards)

    def body(me_ref, *refs):
        for src, dst in zip(refs[:n], refs[n:]):
            dst[0] = src[...]

    return pl.pallas_call(
        body, name="place_own_shards",
        grid_spec=pltpu.PrefetchScalarGridSpec(
            num_scalar_prefetch=1, grid=(1,),
            in_specs=[pl.BlockSpec(s.shape, lambda i, me_ref: (0, 0)) for s in shards],
            out_specs=[pl.BlockSpec((1,) + s.shape, lambda i, me_ref: (me_ref[0], 0, 0)) for s in shards]),
        out_shape=[jax.ShapeDtypeStruct((N_DEV,) + s.shape, s.dtype) for s in shards],
        compiler_params=_params("arbitrary"),
    )(me, *shards)


_FLIPS = [(dx, dy, dc) for dx in (0, 1) for dy in (0, 1) for dc in (0, 1)][1:]
_HBM = pl.BlockSpec(memory_space=pltpu.HBM)
_SEM = pl.BlockSpec(memory_space=pltpu.SEMAPHORE)
_DATAFLOW = pltpu.SideEffectType.DATAFLOW_SIDE_EFFECTING


def _scatter_copies(src_refs, land_refs, send_sems, recv_sems, gather):
    x, y, c = _mesh_position()
    me = 4 * x + 2 * y + c
    copies = []
    for a, (src, land) in enumerate(zip(src_refs, land_refs)):
        for k, (dx, dy, dc) in enumerate(_FLIPS):
            px, py, pc = (x + dx) % 2, (y + dy) % 2, (c + dc) % 2
            pair = a * len(_FLIPS) + k
            copies.append(pltpu.make_async_remote_copy(
                src_ref=src if gather else src.at[4 * px + 2 * py + pc], dst_ref=land.at[me],
                send_sem=send_sems[pair], recv_sem=recv_sems[pair],
                device_id=(px, py, pc), device_id_type=MESH))
    return copies


def _scatter_start(srcs, lands, gather, name, after=None):
    n = len(srcs)
    pairs = n * len(_FLIPS)
    extra = [] if after is None else [after]

    def body(*refs):
        src_refs, land_refs = refs[:n], refs[n:2 * n]
        first = 2 * n + len(extra)
        send_sems, recv_sems = refs[first:first + pairs], refs[first + pairs:first + 2 * pairs]
        token = refs[-1]
        for cp in _scatter_copies(src_refs, land_refs, send_sems, recv_sems, gather):
            cp.start()
        token[...] = jnp.zeros_like(token)

    arrays = [*srcs, *lands]
    sem = pltpu.SemaphoreType.DMA(())
    out = pl.pallas_call(
        body, name=name,
        out_shape=(*[sem] * (2 * pairs), *[pltpu.HBM(v.shape, v.dtype) for v in arrays],
                   jax.ShapeDtypeStruct((8, LANES), F32)),
        in_specs=[*[_HBM] * (2 * n), *[pl.BlockSpec(memory_space=pl.ANY)] * len(extra)],
        out_specs=(*[_SEM] * (2 * pairs), *[_HBM] * (2 * n), pl.BlockSpec(memory_space=pltpu.VMEM)),
        input_output_aliases={i: 2 * pairs + i for i in range(2 * n)},
        compiler_params=pltpu.CompilerParams(has_side_effects=_DATAFLOW),
    )(*[pltpu.with_memory_space_constraint(v, pltpu.HBM) for v in arrays], *extra)
    sems, rest = out[:2 * pairs], out[2 * pairs:]
    return list(sems[:pairs]), list(sems[pairs:]), list(rest[:n]), list(rest[n:2 * n]), rest[-1]


def _scatter_wait(send_sems, recv_sems, srcs, lands, after, gather, name):
    n = len(srcs)
    pairs = n * len(_FLIPS)

    def body(*refs):
        src_refs, land_refs = refs[:n], refs[n:2 * n]
        send_refs, recv_refs = refs[2 * n:2 * n + pairs], refs[2 * n + pairs:2 * n + 2 * pairs]
        for cp in _scatter_copies(src_refs, land_refs, send_refs, recv_refs, gather):
            cp.wait_send()
            cp.wait_recv()

    arrays = [*srcs, *lands]
    out = pl.pallas_call(
        body, name=name,
        out_shape=tuple(pltpu.HBM(v.shape, v.dtype) for v in arrays),
        in_specs=[*[_HBM] * (2 * n), *[_SEM] * (2 * pairs), pl.BlockSpec(memory_space=pl.ANY)],
        out_specs=tuple([_HBM] * (2 * n)),
        input_output_aliases={i: i for i in range(2 * n)},
        compiler_params=pltpu.CompilerParams(has_side_effects=_DATAFLOW),
    )(*arrays, *send_sems, *recv_sems, after)
    return list(out[:n]), list(out[n:])


def _sum_received(grad, received, me, name):
    _, rows, cols = grad.shape

    def body(me_ref, g_ref, r_ref, o_ref):
        p = pl.program_id(0)
        term = jnp.where(p == me_ref[0], g_ref[0], r_ref[0])

        @pl.when(p == 0)
        def _():
            o_ref[...] = term

        @pl.when(p > 0)
        def _():
            o_ref[...] += term

    blk = (1, rows, cols)
    return pl.pallas_call(
        body, name=name,
        grid_spec=pltpu.PrefetchScalarGridSpec(
            num_scalar_prefetch=1, grid=(N_DEV,),
            in_specs=[pl.BlockSpec(blk, lambda p, me_ref: (me_ref[0], 0, 0)),
                      pl.BlockSpec(blk, lambda p, me_ref: (p, 0, 0))],
            out_specs=pl.BlockSpec((rows, cols), lambda p, me_ref: (0, 0))),
        out_shape=jax.ShapeDtypeStruct((rows, cols), F32),
        compiler_params=_params("arbitrary"),
    )(me, grad, received)


def _exchange_in_chip(grads):
    n = len(grads)

    def body(*refs):
        ins, outs = refs[:n], refs[n:2 * n]
        send_sems, recv_sems = refs[2 * n:]
        x, y, c = _mesh_position()
        copies = []
        for a in range(n):
            for q in range(4):
                copies.append(pltpu.make_async_remote_copy(
                    src_ref=ins[a].at[2 * q + (1 - c)], dst_ref=outs[a].at[q],
                    send_sem=send_sems.at[a, q], recv_sem=recv_sems.at[a, q],
                    device_id=(x, y, 1 - c), device_id_type=MESH))
        for cp in copies:
            cp.start()
        for cp in copies:
            cp.wait_recv()
        for cp in copies:
            cp.wait_send()

    return pl.pallas_call(
        body, name="exchange_in_chip",
        in_specs=_hbm_specs(n), out_specs=_hbm_specs(n),
        out_shape=[jax.ShapeDtypeStruct((4,) + g.shape[1:], g.dtype) for g in grads],
        scratch_shapes=[pltpu.SemaphoreType.DMA((n, 4)), pltpu.SemaphoreType.DMA((n, 4))],
        compiler_params=pltpu.CompilerParams(has_side_effects=True),
    )(*grads)


def _exchange_between_chips(partials):
    n = len(partials)

    def body(*refs):
        ins, outs = refs[:n], refs[n:2 * n]
        send_sems, recv_sems = refs[2 * n:]
        x, y, c = _mesh_position()
        chips = [(1 - x, y), (x, 1 - y), (1 - x, 1 - y)]
        copies = []
        for a in range(n):
            for j, (px, py) in enumerate(chips):
                copies.append(pltpu.make_async_remote_copy(
                    src_ref=ins[a].at[2 * px + py], dst_ref=outs[a].at[j],
                    send_sem=send_sems.at[a, j], recv_sem=recv_sems.at[a, j],
                    device_id=(px, py, c), device_id_type=MESH))
        for cp in copies:
            cp.start()
        for cp in copies:
            cp.wait_recv()
        for cp in copies:
            cp.wait_send()

    return pl.pallas_call(
        body, name="exchange_between_chips",
        in_specs=_hbm_specs(n), out_specs=_hbm_specs(n),
        out_shape=[jax.ShapeDtypeStruct((3,) + p.shape[1:], p.dtype) for p in partials],
        scratch_shapes=[pltpu.SemaphoreType.DMA((n, 3)), pltpu.SemaphoreType.DMA((n, 3))],
        compiler_params=pltpu.CompilerParams(has_side_effects=True),
    )(*partials)


def _add_in_chip(grad, received, core, name):
    _, rows, cols = grad.shape

    def body(core_ref, g_ref, r_ref, o_ref):
        o_ref[...] = g_ref[...] + r_ref[...]

    blk = (1, rows, cols)
    return pl.pallas_call(
        body, name=name,
        grid_spec=pltpu.PrefetchScalarGridSpec(
            num_scalar_prefetch=1, grid=(4,),
            in_specs=[pl.BlockSpec(blk, lambda q, core_ref: (2 * q + core_ref[0], 0, 0)),
                      pl.BlockSpec(blk, lambda q, core_ref: (q, 0, 0))],
            out_specs=pl.BlockSpec(blk, lambda q, core_ref: (q, 0, 0))),
        out_shape=jax.ShapeDtypeStruct((4, rows, cols), F32),
        compiler_params=_params("parallel"),
    )(core, grad, received)


def _add_between_chips(partial, received, chip, name):
    _, rows, cols = partial.shape

    def body(chip_ref, p_ref, r_ref, o_ref):
        o_ref[...] = ((p_ref[0] + r_ref[0]) + r_ref[1]) + r_ref[2]

    return pl.pallas_call(
        body, name=name,
        grid_spec=pltpu.PrefetchScalarGridSpec(
            num_scalar_prefetch=1, grid=(1,),
            in_specs=[pl.BlockSpec((1, rows, cols), lambda i, chip_ref: (chip_ref[0], 0, 0)),
                      pl.BlockSpec((3, rows, cols), lambda i, chip_ref: (0, 0, 0))],
            out_specs=pl.BlockSpec((rows, cols), lambda i, chip_ref: (0, 0))),
        out_shape=jax.ShapeDtypeStruct((rows, cols), F32),
        compiler_params=_params("arbitrary"),
    )(chip, partial, received)


def _sum_devices(gathered):
    _, rows, cols = gathered.shape

    def body(g_ref, o_ref):
        total = g_ref[0]
        for d in range(1, N_DEV):
            total = total + g_ref[d]
        o_ref[...] = total

    return pl.pallas_call(
        body, name="sum_small_grads",
        out_shape=jax.ShapeDtypeStruct((rows, cols), F32),
        compiler_params=_params(),
    )(gathered)


def _adamw(w, g, m, v, name):
    rows, cols = w.shape
    tr = rows
    for cand in (256, 128, 64, 32, 16, 8):
        if rows > cand and rows % cand == 0:
            tr = cand
            break

    def body(w_ref, g_ref, m_ref, v_ref, delta_ref, newm_ref, newv_ref):
        g_v = g_ref[...]
        new_m = ADAM_B1 * m_ref[...] + (1.0 - ADAM_B1) * g_v
        new_v = ADAM_B2 * v_ref[...] + (1.0 - ADAM_B2) * (g_v * g_v)
        m_hat = new_m / (1.0 - ADAM_B1 ** ADAM_STEP)
        v_hat = new_v / (1.0 - ADAM_B2 ** ADAM_STEP)
        delta_ref[...] = -ADAM_LR * (m_hat / (jnp.sqrt(v_hat) + ADAM_EPS) + ADAM_WD * w_ref[...])
        newm_ref[...] = new_m
        newv_ref[...] = new_v

    blk = pl.BlockSpec((tr, cols), lambda i: (i, 0))
    shape = jax.ShapeDtypeStruct((rows, cols), F32)
    return pl.pallas_call(
        body, name=name, grid=(rows // tr,),
        in_specs=[blk] * 4, out_specs=[blk] * 3, out_shape=[shape] * 3,
        compiler_params=_params("parallel"),
    )(w, g, m, v)


def _adamw_update(w, g, m, v):
    new_m = ADAM_B1 * m + (1.0 - ADAM_B1) * g
    new_v = ADAM_B2 * v + (1.0 - ADAM_B2) * (g * g)
    m_hat = new_m / (1.0 - ADAM_B1 ** ADAM_STEP)
    v_hat = new_v / (1.0 - ADAM_B2 ** ADAM_STEP)
    return -ADAM_LR * (m_hat / (jnp.sqrt(v_hat) + ADAM_EPS) + ADAM_WD * w), new_m, new_v


def _reduce_and_adamw(grad, received, me, w, m, v, name):
    _, rows, cols = grad.shape
    tr = rows // 2
    assert rows % 16 == 0

    def body(me_ref, g_ref, r_ref, w_ref, m_ref, v_ref, grad_ref, delta_ref, newm_ref, newv_ref, acc):
        p = pl.program_id(1)
        term = jnp.where(p == me_ref[0], g_ref[0], r_ref[0])

        @pl.when(p == 0)
        def _():
            acc[...] = term

        @pl.when(p > 0)
        def _():
            acc[...] += term

        @pl.when(p == N_DEV - 1)
        def _():
            g = acc[...]
            grad_ref[...] = g
            delta_ref[...], newm_ref[...], newv_ref[...] = _adamw_update(w_ref[...], g, m_ref[...], v_ref[...])

    blk = (1, tr, cols)
    tile = pl.BlockSpec((tr, cols), lambda j, p, me_ref: (j, 0))
    shape = jax.ShapeDtypeStruct((rows, cols), F32)
    return pl.pallas_call(
        body, name=name,
        grid_spec=pltpu.PrefetchScalarGridSpec(
            num_scalar_prefetch=1, grid=(rows // tr, N_DEV),
            in_specs=[pl.BlockSpec(blk, lambda j, p, me_ref: (me_ref[0], j, 0)),
                      pl.BlockSpec(blk, lambda j, p, me_ref: (p, j, 0)), tile, tile, tile],
            out_specs=[tile] * 4,
            scratch_shapes=[pltpu.VMEM((tr, cols), F32)]),
        out_shape=[shape] * 4,
        compiler_params=_params("parallel", "arbitrary"),
    )(me, grad, received, w, m, v)


def _pack(pieces, rows):
    flat = jnp.concatenate([p.reshape(-1) for p in pieces])
    return jnp.pad(flat, (0, rows * LANES - flat.shape[0])).reshape(rows, LANES)


def _unpack(packed, shapes):
    flat = packed.reshape(-1)
    out, pos = [], 0
    for shape in shapes:
        size = 1
        for s in shape:
            size *= s
        out.append(flat[pos:pos + size].reshape(shape))
        pos += size
    return out


def _rows_for(count):
    return -(-count // (8 * LANES)) * 8


SMALL = ("norm_mix_pre", "conv_dw_b", "conv_ln_g", "conv_ln_b", "rel_bias", "norm_mix_post", "norm_ffn_pre",
         "ffn_dw_b", "norm_ffn_post")
SHARDED_SMALL = ("conv_dw_w", "ffn_dw_w")
LARGE = ("w_in", "w_out", "w_up", "w_down")
WEIGHTS = ("norm_mix_pre", "w_in", "conv_dw_w", "conv_dw_b", "conv_ln_g", "conv_ln_b", "rel_bias", "w_out",
           "norm_mix_post", "norm_ffn_pre", "w_up", "ffn_dw_w", "ffn_dw_b", "w_down", "norm_ffn_post")


def kernel(x, norm_mix_pre, w_in, conv_dw_w, conv_dw_b, conv_ln_g, conv_ln_b, rel_bias, w_out, norm_mix_post, norm_ffn_pre, w_up, ffn_dw_w, ffn_dw_b, w_down, norm_ffn_post, loss_target, m_norm_mix_pre, m_w_in, m_conv_dw_w, m_conv_dw_b, m_conv_ln_g, m_conv_ln_b, m_rel_bias, m_w_out, m_norm_mix_post, m_norm_ffn_pre, m_w_up, m_ffn_dw_w, m_ffn_dw_b, m_w_down, m_norm_ffn_post, v_norm_mix_pre, v_w_in, v_conv_dw_w, v_conv_dw_b, v_conv_ln_g, v_conv_ln_b, v_rel_bias, v_w_out, v_norm_mix_post, v_norm_ffn_pre, v_w_up, v_ffn_dw_w, v_ffn_dw_b, v_w_down, v_norm_ffn_post):
    weights = dict(norm_mix_pre=norm_mix_pre, w_in=w_in, conv_dw_w=conv_dw_w, conv_dw_b=conv_dw_b, conv_ln_g=conv_ln_g,
                   conv_ln_b=conv_ln_b, rel_bias=rel_bias, w_out=w_out, norm_mix_post=norm_mix_post,
                   norm_ffn_pre=norm_ffn_pre, w_up=w_up, ffn_dw_w=ffn_dw_w, ffn_dw_b=ffn_dw_b, w_down=w_down,
                   norm_ffn_post=norm_ffn_post)
    mom1 = dict(norm_mix_pre=m_norm_mix_pre, w_in=m_w_in, conv_dw_w=m_conv_dw_w, conv_dw_b=m_conv_dw_b,
                conv_ln_g=m_conv_ln_g, conv_ln_b=m_conv_ln_b, rel_bias=m_rel_bias, w_out=m_w_out,
                norm_mix_post=m_norm_mix_post, norm_ffn_pre=m_norm_ffn_pre, w_up=m_w_up, ffn_dw_w=m_ffn_dw_w,
                ffn_dw_b=m_ffn_dw_b, w_down=m_w_down, norm_ffn_post=m_norm_ffn_post)
    mom2 = dict(norm_mix_pre=v_norm_mix_pre, w_in=v_w_in, conv_dw_w=v_conv_dw_w, conv_dw_b=v_conv_dw_b,
                conv_ln_g=v_conv_ln_g, conv_ln_b=v_conv_ln_b, rel_bias=v_rel_bias, w_out=v_w_out,
                norm_mix_post=v_norm_mix_post, norm_ffn_pre=v_norm_ffn_pre, w_up=v_w_up, ffn_dw_w=v_ffn_dw_w,
                ffn_dw_b=v_ffn_dw_b, w_down=v_w_down, norm_ffn_post=v_norm_ffn_post)

    x2 = x[0]
    target = loss_target[0]
    t_rows = x2.shape[0]
    d = D_MODEL
    in_cols = 2 * CONV_WIDTH + 3 * ATTN_WIDTH
    my_x, my_y, my_c = _mesh_position()
    my_dev = 4 * my_x + 2 * my_y + my_c

    small_conv = _pack([conv_dw_w[0], ffn_dw_w[0]], 32)
    me = jnp.reshape(my_dev, (1,)).astype(jnp.int32)
    first_shards = [w_in[0].T.astype(BF16), small_conv]
    late_shards = [w_out[0].astype(BF16), w_up[0].T.astype(BF16), w_down[0].astype(BF16)]
    placed = _place_own(first_shards + late_shards, me)
    win_t, conv_g = _all_gather(first_shards, "all_gather_weights", placed=placed[:2])
    wout_gather = _scatter_start(late_shards[:1], placed[2:3], True, "gather_w_out_start", after=win_t)
    ffn_gather = _scatter_start(late_shards[1:], placed[3:], True, "gather_ffn_weights_start", after=wout_gather[4])
    late_token = ffn_gather[4]
    win_t = win_t.reshape(in_cols, d)
    conv_flat = conv_g.reshape(N_DEV, 32 * LANES)
    n_cw = CONV_K * (CONV_WIDTH // N_DEV)
    conv_w_full = conv_flat[:, :n_cw].reshape(N_DEV, CONV_K, CONV_WIDTH // N_DEV).transpose(1, 0, 2).reshape(CONV_K, CONV_WIDTH)
    ffn_w_full = conv_flat[:, n_cw:].reshape(N_DEV, FFN_K, 2 * D_FF // N_DEV).transpose(1, 0, 2).reshape(FFN_K, 2 * D_FF)

    u1 = _pre_norm(x2, norm_mix_pre + late_token[0:1, 0:1], "pre_norm_mix")
    proj_a = _matmul(u1, win_t, mode="nt", m=t_rows, n=2 * CONV_WIDTH, k=d, tm=2048, tn=1024, tk=d,
                     out_dtype=F32, name="proj_conv")
    qkv_t = _matmul(win_t, u1, mode="nt", m=3 * ATTN_WIDTH, n=t_rows, k=d, tm=512, tn=2048, tk=d,
                    out_dtype=BF16, name="proj_qkv", a_m0=2 * CONV_WIDTH)
    conv_c, conv_out = _conv_forward(proj_a, conv_w_full, conv_dw_b, conv_ln_g, conv_ln_b)
    o_t, attn_lse = _attn_forward(qkv_t, rel_bias[0])
    _, (wout_g,) = _scatter_wait(*wout_gather[:4], o_t, True, "gather_w_out_wait")
    wout_g = wout_g.reshape(d, d)
    mixed = _matmul_sum([(conv_out, "nn", CONV_WIDTH, 0), (o_t, "tn", ATTN_WIDTH, CONV_WIDTH)], wout_g,
                        m=t_rows, n=d, tm=1024, name="out_proj")
    h1, u2 = _mid_forward(x2, mixed, norm_mix_post, norm_ffn_pre)
    _, (wup_t, wdown_g) = _scatter_wait(*ffn_gather[:4], u2, True, "gather_ffn_weights_wait")
    wup_t = wup_t.reshape(2 * D_FF, d)
    wdown_g = wdown_g.reshape(D_FF, d)
    hup = _matmul(u2, wup_t, mode="nt", m=t_rows, n=2 * D_FF, k=d, tm=2048, tn=1408, tk=d,
                  out_dtype=F32, name="ffn_up")
    act, ffn_gel, ffn_slope = _ffn_activation(hup, ffn_w_full, ffn_dw_b)
    f = _matmul(act, wdown_g, mode="nn", m=t_rows, n=d, k=D_FF, tm=1024, tn=1024, tk=D_FF,
                out_dtype=F32, name="ffn_down")
    loss, dy, df, d_norm_ffn_post = _loss_and_head_backward(h1, f, target, norm_ffn_post)

    dact = _matmul(df, wdown_g, mode="nt", m=t_rows, n=D_FF, k=d, tm=2048, tn=1408, tk=d,
                   out_dtype=F32, name="ffn_down_dx")
    g_wdown = _matmul(act, df, mode="tn", m=D_FF, n=d, k=t_rows, tm=1408, tn=1024, tk=2048,
                      out_dtype=F32, name="ffn_down_dw")
    dhg, dhv, dwg, dwv, dbg, dbv = _ffn_backward(dact, ffn_gel, ffn_slope, hup, ffn_w_full)
    du2 = _matmul_sum([(dhg, "nn", D_FF, 0), (dhv, "nn", D_FF, D_FF)], wup_t, m=t_rows, n=d, tm=512, name="ffn_up_dx")
    g_wup_t = _matmul(dhg, u2, mode="tn", m=D_FF, n=d, k=t_rows, tm=1408, tn=1024, tk=2048, out_dtype=F32,
                      name="ffn_up_dw_gate", out_rows=2 * D_FF)
    g_wup_t = _matmul(dhv, u2, mode="tn", m=D_FF, n=d, k=t_rows, tm=1408, tn=1024, tk=2048, out_dtype=F32,
                      name="ffn_up_dw_value", out_rows=2 * D_FF, out_m0=D_FF, into=g_wup_t)
    ffn_grads = [g_wup_t.reshape(N_DEV, 2 * D_FF // N_DEV, d), g_wdown.reshape(N_DEV, D_FF // N_DEV, d)]
    red_send, red_recv, ffn_grads, red_lands, red_token = _scatter_start(
        ffn_grads, [lax.empty(g.shape, F32) for g in ffn_grads], False, "reduce_ffn_grads_start")
    dh1, dmixed, d_norm_ffn_pre, d_norm_mix_post = _mid_backward(
        dy, du2, h1, mixed, norm_ffn_pre + red_token[0:1, 0:1], norm_mix_post)
    dconv_out = _matmul(dmixed, wout_g, mode="nt", m=t_rows, n=CONV_WIDTH, k=d, tm=2048, tn=512, tk=d,
                        out_dtype=F32, name="out_proj_dx_conv")
    do_t = _matmul(wout_g, dmixed, mode="nt", m=ATTN_WIDTH, n=t_rows, k=d, tm=512, tn=2048, tk=d,
                   out_dtype=BF16, name="out_proj_dx_attn", a_m0=CONV_WIDTH)
    g_wout = _matmul(conv_out, dmixed, mode="tn", m=CONV_WIDTH, n=d, k=t_rows, tm=512, tn=1024, tk=2048, out_dtype=F32,
                     name="out_proj_dw_conv", out_rows=d)
    g_wout = _matmul(o_t, dmixed, mode="nn", m=ATTN_WIDTH, n=d, k=t_rows, tm=512, tn=1024, tk=2048, out_dtype=F32,
                     name="out_proj_dw_attn", out_rows=d, out_m0=CONV_WIDTH, into=g_wout)
    wout_handle = _scatter_start([g_wout.reshape(N_DEV, d // N_DEV, d)], [lax.empty((N_DEV, d // N_DEV, d), F32)],
                                 False, "reduce_w_out_grad_start")
    dproj_a, d_conv_w, d_conv_b, d_ln_g, d_ln_b = _conv_backward(
        dconv_out, conv_c, proj_a, conv_w_full, conv_ln_g + wout_handle[4][0:1, 0:1], conv_ln_b)
    dqkv_parts = _attn_backward(qkv_t, o_t, do_t, attn_lse, rel_bias[0])
    drel = dqkv_parts[3]
    du1 = _matmul_sum([(dproj_a, "nn", 2 * CONV_WIDTH, 0)]
                      + [(dqkv_parts[j], "tn", ATTN_WIDTH, 2 * CONV_WIDTH + j * ATTN_WIDTH) for j in range(3)],
                      win_t, m=t_rows, n=d, tm=1024, name="proj_dx")
    g_win_t =_matmul(dproj_a, u1, mode="tn", m=2 * CONV_WIDTH, n=d, k=t_rows, tm=1024, tn=1024, tk=2048, out_dtype=F32,
                      name="proj_dw_conv", out_rows=in_cols)
    for j, part in enumerate("qkv"):
        row0 = 2 * CONV_WIDTH + j * ATTN_WIDTH
        g_win_t = _matmul(dqkv_parts[j], u1, mode="nn", m=ATTN_WIDTH, n=d, k=t_rows, tm=512, tn=1024, tk=2048,
                          out_dtype=F32, name="proj_dw_" + part, out_rows=in_cols, out_m0=row0, into=g_win_t)
    win_handle = _scatter_start([g_win_t.reshape(N_DEV, in_cols // N_DEV, d)],
                                [lax.empty((N_DEV, in_cols // N_DEV, d), F32)], False, "reduce_w_in_grad_start")
    dx, d_norm_mix_pre = _input_backward(dh1, du1, x2, norm_mix_pre + win_handle[4][0:1, 0:1])

    small_grads = dict(norm_mix_pre=d_norm_mix_pre, conv_dw_b=d_conv_b, conv_ln_g=d_ln_g, conv_ln_b=d_ln_b,
                       rel_bias=drel[:, :2 * MAX_REL + 1], norm_mix_post=d_norm_mix_post, norm_ffn_pre=d_norm_ffn_pre,
                       ffn_dw_b=jnp.concatenate([dbg, dbv], axis=1), norm_ffn_post=d_norm_ffn_post)
    pieces = [small_grads[nm] for nm in SMALL] + [d_conv_w, jnp.concatenate([dwg, dwv], axis=1), loss]
    count = sum(p.size for p in pieces)
    (gathered_small,) = _all_gather([_pack(pieces, _rows_for(count))], "all_gather_small_grads")
    summed = _sum_devices(gathered_small)
    shapes = [weights[nm].shape for nm in SMALL] + [(CONV_K, CONV_WIDTH), (FFN_K, 2 * D_FF), (1, 1)]
    unpacked = _unpack(summed, shapes)
    grads = dict(zip(SMALL, unpacked[:len(SMALL)]))
    cw_shard, fw_shard = CONV_WIDTH // N_DEV, 2 * D_FF // N_DEV
    grads["conv_dw_w"] = lax.dynamic_slice_in_dim(unpacked[-3], my_dev * cw_shard, cw_shard, axis=1)[None]
    grads["ffn_dw_w"] = lax.dynamic_slice_in_dim(unpacked[-2], my_dev * fw_shard, fw_shard, axis=1)[None]
    total_loss = unpacked[-1].reshape(())

    me = jnp.reshape(my_dev, (1,)).astype(jnp.int32)
    delta, new_m, new_v = {}, {}, {}

    def finish(nm, send, recv, srcs, lands, after, transposed):
        srcs, lands = _scatter_wait(send, recv, srcs, lands, after, False, "reduce_" + nm + "_grad_wait")
        for name_a, src, land in zip(nm.split("_and_"), srcs, lands):
            flip = (lambda t: t.T) if transposed[name_a] else (lambda t: t)
            outs = _reduce_and_adamw(src, land, me, flip(weights[name_a][0]), flip(mom1[name_a][0]),
                                     flip(mom2[name_a][0]), "reduce_adamw_" + name_a)
            for store, arr in zip((grads, delta, new_m, new_v), outs):
                store[name_a] = flip(arr)[None]

    transposed = dict(w_in=True, w_out=False, w_up=True, w_down=False)
    finish("w_up_and_w_down", red_send, red_recv, ffn_grads, red_lands, dx, transposed)
    finish("w_out", *wout_handle[:4], dx, transposed)
    finish("w_in", *win_handle[:4], delta["w_up"], transposed)
    small_names = SMALL + SHARDED_SMALL
    small_count = sum(weights[nm].size for nm in small_names)
    small_rows = _rows_for(small_count)
    packed = [_pack([src[nm] for nm in small_names], small_rows) for src in (weights, grads, mom1, mom2)]
    outs = _adamw(*packed, "adamw_small")
    small_shapes = [weights[nm].shape for nm in small_names]
    for store, arr in zip((delta, new_m, new_v), outs):
        store.update(zip(small_names, _unpack(arr, small_shapes)))

    return (total_loss, dx[None], *[grads[nm] for nm in WEIGHTS], *[delta[nm] for nm in WEIGHTS],
            *[new_m[nm] for nm in WEIGHTS], *[new_v[nm] for nm in WEIGHTS])
```

```python
import jax
import jax.numpy as jnp
from jax import lax
from jax.experimental import pallas as pl
from jax.experimental.pallas import tpu as pltpu

F32 = jnp.float32
BF16 = jnp.bfloat16
MESH = pl.DeviceIdType.MESH
AXES = ("x", "y", "c")
N_DEV = 8

EPS = 1e-6
NEG_INF = -1e30
D_MODEL = 1024
CONV_WIDTH = 512
ATTN_WIDTH = 512
N_HEADS = 8
HEAD_DIM = 64
CHUNK = 64
LEFT = 8 * CHUNK
QBLK = 2 * CHUNK
WIN = LEFT + QBLK
CONV_K = 31
CONV_HALO = 32
FFN_K = 3
FFN_HALO = 8
D_FF = 2816
MAX_REL = 128
SCALE = HEAD_DIM ** -0.5
ADAM_LR, ADAM_B1, ADAM_B2, ADAM_EPS, ADAM_WD, ADAM_STEP = 0.001, 0.9, 0.999, 1e-08, 0.01, 10

V7X_VMEM_BYTES = 64 * 2**20
VMEM_LIMIT_BYTES = V7X_VMEM_BYTES - 8 * 2**20
LANES = 128


def _params(*sem):
    return pltpu.CompilerParams(dimension_semantics=sem or None, vmem_limit_bytes=VMEM_LIMIT_BYTES)


_DOT_DIMS = {"nn": (((1,), (0,)), ((), ())), "nt": (((1,), (1,)), ((), ())), "tn": (((0,), (0,)), ((), ()))}


def _matmul(a, b, *, mode, m, n, k, tm, tn, tk, out_dtype, name, a_m0=0, b_n0=0, b_k0=0, add=None,
            out_rows=None, out_m0=0, into=None):
    tm, tn, tk = min(tm, m), min(tn, n), min(tk, k)
    out_rows = m if out_rows is None else out_rows
    assert m % tm == 0 and n % tn == 0 and k % tk == 0, (name, m, n, k, tm, tn, tk)
    assert a_m0 % tm == 0 and b_n0 % tn == 0 and b_k0 % tk == 0 and out_m0 % tm == 0, name
    am, bn, bk, om = a_m0 // tm, b_n0 // tn, b_k0 // tk, out_m0 // tm
    gk = k // tk
    dims = _DOT_DIMS[mode]

    if mode == "tn":
        a_spec = pl.BlockSpec((tk, tm), lambda i, j, kk: (kk, i + am))
    else:
        a_spec = pl.BlockSpec((tm, tk), lambda i, j, kk: (i + am, kk))
    if mode == "nt":
        b_spec = pl.BlockSpec((tn, tk), lambda i, j, kk: (j + bn, kk + bk))
    else:
        b_spec = pl.BlockSpec((tk, tn), lambda i, j, kk: (kk + bk, j + bn))
    o_spec = pl.BlockSpec((tm, tn), lambda i, j, kk: (i + om, j))
    in_specs = [a_spec, b_spec]
    operands = [a, b]
    if add is not None:
        assert out_rows == m
        in_specs.append(o_spec)
        operands.append(add)
    aliases = {}
    if into is not None:
        aliases = {len(operands): 0}
        in_specs.append(pl.BlockSpec(memory_space=pl.ANY))
        operands.append(into)

    def body(*refs):
        a_ref, b_ref = refs[0], refs[1]
        add_ref = refs[2] if add is not None else None
        o_ref = refs[len(operands)]
        part = lax.dot_general(a_ref[...].astype(BF16), b_ref[...].astype(BF16), dims,
                               preferred_element_type=F32)

        def finish(total):
            if add_ref is not None:
                total = total + add_ref[...]
            o_ref[...] = total.astype(out_dtype)

        if gk == 1:
            finish(part)
        else:
            acc_ref = refs[-1]
            kk = pl.program_id(2)

            @pl.when(kk == 0)
            def _():
                acc_ref[...] = part

            @pl.when(kk > 0)
            def _():
                acc_ref[...] += part

            @pl.when(kk == gk - 1)
            def _():
                finish(acc_ref[...])

    return pl.pallas_call(
        body, name=name,
        grid=(m // tm, n // tn, gk),
        in_specs=in_specs, out_specs=o_spec,
        out_shape=jax.ShapeDtypeStruct((out_rows, n), out_dtype),
        scratch_shapes=[pltpu.VMEM((tm, tn), F32)] if gk > 1 else [],
        input_output_aliases=aliases,
        compiler_params=_params("parallel", "parallel", "arbitrary"),
    )(*operands)


def _matmul_sum(pieces, b, *, m, n, tm, name):
    tm = min(tm, m)
    assert m % tm == 0
    in_specs, operands = [], []
    for a, mode, k, k0 in pieces:
        assert k0 % k == 0
        if mode == "tn":
            in_specs.append(pl.BlockSpec((k, tm), lambda i: (0, i)))
        else:
            in_specs.append(pl.BlockSpec((tm, k), lambda i: (i, 0)))
        in_specs.append(pl.BlockSpec((k, n), lambda i, blk=k0 // k: (blk, 0)))
        operands += [a, b]

    def body(*refs):
        total = None
        for p, (_, mode, _, _) in enumerate(pieces):
            part = lax.dot_general(refs[2 * p][...], refs[2 * p + 1][...], _DOT_DIMS[mode], preferred_element_type=F32)
            total = part if total is None else total + part
        refs[-1][...] = total

    return pl.pallas_call(
        body, name=name, grid=(m // tm,),
        in_specs=in_specs, out_specs=pl.BlockSpec((tm, n), lambda i: (i, 0)),
        out_shape=jax.ShapeDtypeStruct((m, n), F32),
        compiler_params=_params("parallel"),
    )(*operands)


def _matmul_rows(pieces, b, *, m, n, tm, name, row_ins, vec_ins, row_outs, acc_outs, epilogue):
    tm = min(tm, m)
    assert m % tm == 0
    steps = m // tm
    in_specs, operands = [], []
    for a, mode, k, k0 in pieces:
        assert k0 % k == 0
        if mode == "tn":
            in_specs.append(pl.BlockSpec((k, tm), lambda i: (0, i)))
        else:
            in_specs.append(pl.BlockSpec((tm, k), lambda i: (i, 0)))
        in_specs.append(pl.BlockSpec((k, n), lambda i, blk=k0 // k: (blk, 0)))
        operands += [a, b]
    row = pl.BlockSpec((tm, n), lambda i: (i, 0))
    in_specs += [row] * len(row_ins) + [pl.BlockSpec((1, n), lambda i: (0, 0))] * len(vec_ins)
    operands += [*row_ins, *vec_ins]
    n_in = len(operands)

    def body(*refs):
        total = None
        for p, (_, mode, _, _) in enumerate(pieces):
            part = lax.dot_general(refs[2 * p][...], refs[2 * p + 1][...], _DOT_DIMS[mode], preferred_element_type=F32)
            total = part if total is None else total + part
        first = 2 * len(pieces)
        rows = refs[first:first + len(row_ins)]
        vecs = refs[first + len(row_ins):n_in]
        outs = refs[n_in:n_in + len(row_outs)]
        accs = refs[n_in + len(row_outs):]
        epilogue(total, rows, vecs, outs, accs, pl.program_id(0), steps)

    return pl.pallas_call(
        body, name=name, grid=(steps,),
        in_specs=in_specs,
        out_specs=[row] * len(row_outs) + [pl.BlockSpec(s, lambda i: (0, 0)) for s in acc_outs],
        out_shape=[jax.ShapeDtypeStruct((m, n), dt) for dt in row_outs]
        + [jax.ShapeDtypeStruct(s, F32) for s in acc_outs],
        compiler_params=_params("arbitrary" if acc_outs else "parallel"),
    )(*operands)


def _rms_hat(v):
    r = lax.rsqrt(jnp.mean(v * v, axis=-1, keepdims=True) + EPS)
    return v * r, r


def _rms_bwd(dn, hat, r):
    return r * (dn - hat * jnp.mean(dn * hat, axis=-1, keepdims=True))


def _sigmoid(v):
    return 1.0 / (1.0 + jnp.exp(-v))


_GELU_C = 0.7978845608028654


def _gelu(v):
    return 0.5 * v * (1.0 + jnp.tanh(_GELU_C * (v + 0.044715 * v * (v * v))))


def _gelu_parts(v):
    v2 = v * v
    t = jnp.tanh(_GELU_C * (v + 0.044715 * v * v2))
    cdf = 0.5 * (1.0 + t)
    dcdf = 0.5 * (1.0 - t * t) * _GELU_C * (1.0 + 3.0 * 0.044715 * v2)
    return v * cdf, cdf + v * dcdf


def _row_tile(t_rows, want):
    tile = min(want, t_rows)
    assert t_rows % tile == 0
    return tile


def _pre_norm(x, g, name):
    t_rows, d = x.shape
    tm = _row_tile(t_rows, 512)

    def body(x_ref, g_ref, u_ref):
        hat, _ = _rms_hat(x_ref[...])
        u_ref[...] = (hat * g_ref[...]).astype(BF16)

    return pl.pallas_call(
        body, name=name, grid=(t_rows // tm,),
        in_specs=[pl.BlockSpec((tm, d), lambda i: (i, 0)), pl.BlockSpec((1, d), lambda i: (0, 0))],
        out_specs=pl.BlockSpec((tm, d), lambda i: (i, 0)),
        out_shape=jax.ShapeDtypeStruct((t_rows, d), BF16),
        compiler_params=_params("parallel"),
    )(x, g)


def _mid_forward(x, mixed, g_post, g_pre):
    t_rows, d = x.shape
    tm = _row_tile(t_rows, 512)

    def body(x_ref, mixed_ref, gpost_ref, gpre_ref, h1_ref, u2_ref):
        hat, _ = _rms_hat(mixed_ref[...])
        h1 = x_ref[...] + hat * gpost_ref[...]
        h1_ref[...] = h1
        hat1, _ = _rms_hat(h1)
        u2_ref[...] = (hat1 * gpre_ref[...]).astype(BF16)

    row = pl.BlockSpec((tm, d), lambda i: (i, 0))
    vec = pl.BlockSpec((1, d), lambda i: (0, 0))
    return pl.pallas_call(
        body, name="mid_forward", grid=(t_rows // tm,),
        in_specs=[row, row, vec, vec], out_specs=[row, row],
        out_shape=[jax.ShapeDtypeStruct((t_rows, d), F32), jax.ShapeDtypeStruct((t_rows, d), BF16)],
        compiler_params=_params("parallel"),
    )(x, mixed, g_post, g_pre)


def _loss_and_head_backward(h1, f, target, g_post):
    t_rows, d = h1.shape
    tm = _row_tile(t_rows, 512)
    nt = t_rows // tm

    def body(h1_ref, f_ref, tgt_ref, g_ref, loss_ref, dy_ref, df_ref, dg_ref, sq_ref):
        i = pl.program_id(0)

        @pl.when(i == 0)
        def _():
            sq_ref[...] = jnp.zeros_like(sq_ref)
            dg_ref[...] = jnp.zeros_like(dg_ref)

        g = g_ref[...]
        hat, r = _rms_hat(f_ref[...])
        err = h1_ref[...] + hat * g - tgt_ref[...]
        sq_ref[...] += jnp.sum(err * err, axis=0, keepdims=True)
        dy = err * (1.0 / d)
        dy_ref[...] = dy
        dg_ref[...] += jnp.sum(dy * hat, axis=0, keepdims=True)
        df_ref[...] = _rms_bwd(dy * g, hat, r).astype(BF16)

        @pl.when(i == nt - 1)
        def _():
            loss_ref[...] = (0.5 / d) * jnp.sum(sq_ref[...], axis=1, keepdims=True)

    row = pl.BlockSpec((tm, d), lambda i: (i, 0))
    vec = pl.BlockSpec((1, d), lambda i: (0, 0))
    return pl.pallas_call(
        body, name="loss_head_backward", grid=(nt,),
        in_specs=[row, row, row, vec],
        out_specs=[pl.BlockSpec((1, 1), lambda i: (0, 0)), row, row, vec],
        out_shape=[jax.ShapeDtypeStruct((1, 1), F32), jax.ShapeDtypeStruct((t_rows, d), F32),
                   jax.ShapeDtypeStruct((t_rows, d), BF16), jax.ShapeDtypeStruct((1, d), F32)],
        scratch_shapes=[pltpu.VMEM((1, d), F32)],
        compiler_params=_params("arbitrary"),
    )(h1, f, target, g_post)


def _mid_backward(dy, du2, h1, mixed, g_pre, g_post):
    t_rows, d = dy.shape
    tm = _row_tile(t_rows, 512)

    def body(dy_ref, du2_ref, h1_ref, mixed_ref, gpre_ref, gpost_ref, dh1_ref, dmixed_ref, dgpre_ref, dgpost_ref):
        @pl.when(pl.program_id(0) == 0)
        def _():
            dgpre_ref[...] = jnp.zeros_like(dgpre_ref)
            dgpost_ref[...] = jnp.zeros_like(dgpost_ref)

        du2 = du2_ref[...]
        hat1, r1 = _rms_hat(h1_ref[...])
        dgpre_ref[...] += jnp.sum(du2 * hat1, axis=0, keepdims=True)
        dh1 = dy_ref[...] + _rms_bwd(du2 * gpre_ref[...], hat1, r1)
        dh1_ref[...] = dh1
        hatm, rm = _rms_hat(mixed_ref[...])
        dgpost_ref[...] += jnp.sum(dh1 * hatm, axis=0, keepdims=True)
        dmixed_ref[...] = _rms_bwd(dh1 * gpost_ref[...], hatm, rm).astype(BF16)

    row = pl.BlockSpec((tm, d), lambda i: (i, 0))
    vec = pl.BlockSpec((1, d), lambda i: (0, 0))
    return pl.pallas_call(
        body, name="mid_backward", grid=(t_rows // tm,),
        in_specs=[row, row, row, row, vec, vec], out_specs=[row, row, vec, vec],
        out_shape=[jax.ShapeDtypeStruct((t_rows, d), F32), jax.ShapeDtypeStruct((t_rows, d), BF16),
                   jax.ShapeDtypeStruct((1, d), F32), jax.ShapeDtypeStruct((1, d), F32)],
        compiler_params=_params("arbitrary"),
    )(dy, du2, h1, mixed, g_pre, g_post)


def _input_backward(dh1, du1, x, g_pre):
    t_rows, d = x.shape
    tm = _row_tile(t_rows, 512)

    def body(dh1_ref, du1_ref, x_ref, g_ref, dx_ref, dg_ref):
        @pl.when(pl.program_id(0) == 0)
        def _():
            dg_ref[...] = jnp.zeros_like(dg_ref)

        du1 = du1_ref[...]
        hat, r = _rms_hat(x_ref[...])
        dg_ref[...] += jnp.sum(du1 * hat, axis=0, keepdims=True)
        dx_ref[...] = dh1_ref[...] + _rms_bwd(du1 * g_ref[...], hat, r)

    row = pl.BlockSpec((tm, d), lambda i: (i, 0))
    vec = pl.BlockSpec((1, d), lambda i: (0, 0))
    return pl.pallas_call(
        body, name="input_backward", grid=(t_rows // tm,),
        in_specs=[row, row, row, vec], out_specs=[row, vec],
        out_shape=[jax.ShapeDtypeStruct((t_rows, d), F32), jax.ShapeDtypeStruct((1, d), F32)],
        compiler_params=_params("arbitrary"),
    )(dh1, du1, x, g_pre)


def _zero_at_start(accs, step):
    @pl.when(step == 0)
    def _():
        for acc in accs:
            acc[...] = jnp.zeros_like(acc)


def _mid_forward_epilogue(mixed, rows, vecs, outs, accs, step, steps):
    (x_ref,), (gpost_ref, gpre_ref), (mixed_ref, h1_ref, u2_ref) = rows, vecs, outs
    mixed_ref[...] = mixed
    hat, _ = _rms_hat(mixed)
    h1 = x_ref[...] + hat * gpost_ref[...]
    h1_ref[...] = h1
    hat1, _ = _rms_hat(h1)
    u2_ref[...] = (hat1 * gpre_ref[...]).astype(BF16)


def _loss_epilogue(f, rows, vecs, outs, accs, step, steps):
    (h1_ref, tgt_ref), (g_ref,), (dy_ref, df_ref), (sq_ref, dg_ref, loss_ref) = rows, vecs, outs, accs
    _zero_at_start(accs, step)
    g = g_ref[...]
    d = f.shape[-1]
    hat, r = _rms_hat(f)
    err = h1_ref[...] + hat * g - tgt_ref[...]
    sq_ref[...] += jnp.sum(err * err, axis=0, keepdims=True)
    dy = err * (1.0 / d)
    dy_ref[...] = dy
    dg_ref[...] += jnp.sum(dy * hat, axis=0, keepdims=True)
    df_ref[...] = _rms_bwd(dy * g, hat, r).astype(BF16)

    @pl.when(step == steps - 1)
    def _():
        loss_ref[...] = (0.5 / d) * jnp.sum(sq_ref[...], axis=1, keepdims=True)


def _mid_backward_epilogue(du2, rows, vecs, outs, accs, step, steps):
    (dy_ref, h1_ref, mixed_ref), (gpre_ref, gpost_ref), (dh1_ref, dmixed_ref), (dgpre_ref, dgpost_ref) = rows, vecs, outs, accs
    _zero_at_start(accs, step)
    hat1, r1 = _rms_hat(h1_ref[...])
    dgpre_ref[...] += jnp.sum(du2 * hat1, axis=0, keepdims=True)
    dh1 = dy_ref[...] + _rms_bwd(du2 * gpre_ref[...], hat1, r1)
    dh1_ref[...] = dh1
    hatm, rm = _rms_hat(mixed_ref[...])
    dgpost_ref[...] += jnp.sum(dh1 * hatm, axis=0, keepdims=True)
    dmixed_ref[...] = _rms_bwd(dh1 * gpost_ref[...], hatm, rm).astype(BF16)


def _input_backward_epilogue(du1, rows, vecs, outs, accs, step, steps):
    (dh1_ref, x_ref), (g_ref,), (dx_ref,), (dg_ref,) = rows, vecs, outs, accs
    _zero_at_start(accs, step)
    hat, r = _rms_hat(x_ref[...])
    dg_ref[...] += jnp.sum(du1 * hat, axis=0, keepdims=True)
    dx_ref[...] = dh1_ref[...] + _rms_bwd(du1 * g_ref[...], hat, r)


CONV_STRIP = 32


def _glu(block):
    return block[:, :CONV_WIDTH] * _sigmoid(block[:, CONV_WIDTH:])


def _layer_norm_parts(c):
    mu = jnp.mean(c, axis=-1, keepdims=True)
    xc = c - mu
    r = lax.rsqrt(jnp.mean(xc * xc, axis=-1, keepdims=True) + EPS)
    return xc * r, r


CONV_WINDOW = 2 * CONV_STRIP
SHIFTED_ROWS = CONV_WINDOW - 8


def _shifted_copies(v, shifted):
    for s in range(1, 8):
        shifted[s] = v[s:s + SHIFTED_ROWS, :]


def _window_rows(v, shifted, start):
    s, a = start % 8, start - start % 8
    return v[a:a + CONV_STRIP, :] if s == 0 else shifted[s, a:a + CONV_STRIP, :]


def _conv_forward(proj_a, w, b, ln_g, ln_b):
    t_rows = proj_a.shape[0]
    tm = _row_tile(t_rows, 512)
    hb = tm // CONV_HALO
    cw = CONV_WIDTH

    def body(cur_ref, prev_ref, w_ref, b_ref, g_ref, beta_ref, c_ref, out_ref, hbuf, shifted):
        i = pl.program_id(0)
        hbuf[0:CONV_HALO, :] = jnp.where(i > 0, _glu(prev_ref[...]), 0.0)
        hbuf[CONV_HALO:, :] = _glu(cur_ref[...])

        def strip(s, carry):
            base = pl.multiple_of(s * CONV_STRIP, CONV_STRIP)
            v = hbuf[pl.ds(base, CONV_WINDOW), :]
            _shifted_copies(v, shifted)
            acc = jnp.broadcast_to(b_ref[...], (CONV_STRIP, cw))
            off = CONV_HALO - (CONV_K - 1)
            for kk in range(CONV_K):
                acc = acc + w_ref[kk:kk + 1, :] * _window_rows(v, shifted, off + kk)
            c_ref[pl.ds(base, CONV_STRIP), :] = acc
            hat, _ = _layer_norm_parts(acc)
            z = hat * g_ref[...] + beta_ref[...]
            out_ref[pl.ds(base, CONV_STRIP), :] = (z * _sigmoid(z)).astype(BF16)
            return carry

        lax.fori_loop(0, tm // CONV_STRIP, strip, 0)

    vec = pl.BlockSpec((1, cw), lambda i: (0, 0))
    return pl.pallas_call(
        body, name="conv_forward", grid=(t_rows // tm,),
        in_specs=[pl.BlockSpec((tm, 2 * cw), lambda i: (i, 0)),
                  pl.BlockSpec((CONV_HALO, 2 * cw), lambda i: (jnp.maximum(i * hb - 1, 0), 0)),
                  pl.BlockSpec((CONV_K, cw), lambda i: (0, 0)), vec, vec, vec],
        out_specs=[pl.BlockSpec((tm, cw), lambda i: (i, 0)), pl.BlockSpec((tm, cw), lambda i: (i, 0))],
        out_shape=[jax.ShapeDtypeStruct((t_rows, cw), F32), jax.ShapeDtypeStruct((t_rows, cw), BF16)],
        scratch_shapes=[pltpu.VMEM((tm + CONV_HALO, cw), F32), pltpu.VMEM((8, SHIFTED_ROWS, cw), F32)],
        compiler_params=_params("parallel"),
    )(proj_a, proj_a, w, b, ln_g, ln_b)


def _conv_backward(dout, c, proj_a, w, ln_g, ln_b):
    t_rows = c.shape[0]
    tm = _row_tile(t_rows, 512)
    hb = tm // CONV_HALO
    nt = t_rows // tm
    last_halo = t_rows // CONV_HALO - 1
    cw = CONV_WIDTH

    def body(dout_ref, dout_next_ref, c_ref, c_next_ref, cur_ref, prev_ref, w_ref, g_ref, beta_ref,
             dproj_ref, dw_ref, db_ref, dg_ref, dbeta_ref, hbuf, dcbuf, dwacc, h_shifted, d_shifted):
        i = pl.program_id(0)

        @pl.when(i == 0)
        def _():
            dwacc[...] = jnp.zeros_like(dwacc)
            db_ref[...] = jnp.zeros_like(db_ref)
            dg_ref[...] = jnp.zeros_like(dg_ref)
            dbeta_ref[...] = jnp.zeros_like(dbeta_ref)

        def ln_swish_backward(dout_v, c_v):
            hat, r = _layer_norm_parts(c_v)
            g = g_ref[...]
            z = hat * g + beta_ref[...]
            sg = _sigmoid(z)
            dz = dout_v * (sg * (1.0 + z * (1.0 - sg)))
            dhat = dz * g
            dc = r * (dhat - jnp.mean(dhat, axis=-1, keepdims=True)
                      - hat * jnp.mean(dhat * hat, axis=-1, keepdims=True))
            return dc, dz, hat

        dc, dz, hat = ln_swish_backward(dout_ref[...], c_ref[...])
        dg_ref[...] += jnp.sum(dz * hat, axis=0, keepdims=True)
        dbeta_ref[...] += jnp.sum(dz, axis=0, keepdims=True)
        db_ref[...] += jnp.sum(dc, axis=0, keepdims=True)
        dcbuf[0:tm, :] = dc
        dc_next, _, _ = ln_swish_backward(dout_next_ref[...], c_next_ref[...])
        dcbuf[tm:, :] = jnp.where(i < nt - 1, dc_next, 0.0)

        hbuf[0:CONV_HALO, :] = jnp.where(i > 0, _glu(prev_ref[...]), 0.0)
        hbuf[CONV_HALO:, :] = _glu(cur_ref[...])

        def strip(s, carry):
            base = pl.multiple_of(s * CONV_STRIP, CONV_STRIP)
            dv = dcbuf[pl.ds(base, CONV_WINDOW), :]
            hv = hbuf[pl.ds(base, CONV_WINDOW), :]
            _shifted_copies(dv, d_shifted)
            _shifted_copies(hv, h_shifted)
            dcs = dv[0:CONV_STRIP, :]
            dh = jnp.zeros((CONV_STRIP, cw), F32)
            off = CONV_HALO - (CONV_K - 1)
            for kk in range(CONV_K):
                back = CONV_K - 1 - kk
                dh = dh + w_ref[kk:kk + 1, :] * _window_rows(dv, d_shifted, back)
                prod = dcs * _window_rows(hv, h_shifted, off + kk)
                dwacc[kk] += jnp.sum(prod.reshape(CONV_STRIP // 8, 8, cw), axis=0)
            blk = cur_ref[pl.ds(base, CONV_STRIP), :]
            val, sg = blk[:, :cw], _sigmoid(blk[:, cw:])
            dproj_ref[pl.ds(base, CONV_STRIP), 0:cw] = (dh * sg).astype(BF16)
            dproj_ref[pl.ds(base, CONV_STRIP), cw:2 * cw] = (dh * val * sg * (1.0 - sg)).astype(BF16)
            return carry

        lax.fori_loop(0, tm // CONV_STRIP, strip, 0)

        @pl.when(i == nt - 1)
        def _():
            for kk in range(CONV_K):
                dw_ref[kk:kk + 1, :] = jnp.sum(dwacc[kk], axis=0, keepdims=True)

    vec = pl.BlockSpec((1, cw), lambda i: (0, 0))
    cur = lambda width: pl.BlockSpec((tm, width), lambda i: (i, 0))
    nxt = lambda width: pl.BlockSpec((CONV_HALO, width), lambda i: (jnp.minimum((i + 1) * hb, last_halo), 0))
    return pl.pallas_call(
        body, name="conv_backward", grid=(nt,),
        in_specs=[cur(cw), nxt(cw), cur(cw), nxt(cw), cur(2 * cw),
                  pl.BlockSpec((CONV_HALO, 2 * cw), lambda i: (jnp.maximum(i * hb - 1, 0), 0)),
                  pl.BlockSpec((CONV_K, cw), lambda i: (0, 0)), vec, vec],
        out_specs=[cur(2 * cw), pl.BlockSpec((CONV_K, cw), lambda i: (0, 0)), vec, vec, vec],
        out_shape=[jax.ShapeDtypeStruct((t_rows, 2 * cw), BF16), jax.ShapeDtypeStruct((CONV_K, cw), F32),
                   jax.ShapeDtypeStruct((1, cw), F32), jax.ShapeDtypeStruct((1, cw), F32),
                   jax.ShapeDtypeStruct((1, cw), F32)],
        scratch_shapes=[pltpu.VMEM((tm + CONV_HALO, cw), F32), pltpu.VMEM((tm + CONV_HALO, cw), F32),
                        pltpu.VMEM((CONV_K, 8, cw), F32), pltpu.VMEM((8, SHIFTED_ROWS, cw), F32),
                        pltpu.VMEM((8, SHIFTED_ROWS, cw), F32)],
        compiler_params=_params("arbitrary"),
    )(dout, dout, c, c, proj_a, proj_a, w, ln_g, ln_b)


def _attn_load_kv(kv_hbm, k_pad, v_pad, sem, t_cols):
    k_pad[:, 0:LEFT] = jnp.zeros((ATTN_WIDTH, LEFT), BF16)
    v_pad[:, 0:LEFT] = jnp.zeros((ATTN_WIDTH, LEFT), BF16)
    ck = pltpu.make_async_copy(kv_hbm.at[pl.ds(ATTN_WIDTH, ATTN_WIDTH), :], k_pad.at[:, pl.ds(LEFT, t_cols)], sem.at[0])
    cv = pltpu.make_async_copy(kv_hbm.at[pl.ds(2 * ATTN_WIDTH, ATTN_WIDTH), :], v_pad.at[:, pl.ds(LEFT, t_cols)], sem.at[1])
    ck.start()
    cv.start()
    ck.wait()
    cv.wait()


def _attn_build_bias(tab_ref, bias_t):
    row = lax.broadcasted_iota(jnp.int32, (LANES, LANES), 0)
    lane = lax.broadcasted_iota(jnp.int32, (LANES, LANES), 1)
    upper = lane >= row
    lane64 = lax.broadcasted_iota(jnp.int32, (CHUNK, LANES), 1)
    for h in range(N_HEADS):
        far = jnp.broadcast_to(tab_ref[h:h + 1, 2 * MAX_REL:2 * MAX_REL + 1], (LANES, LANES))
        hi = jnp.broadcast_to(tab_ref[h:h + 1, MAX_REL:2 * MAX_REL], (LANES, LANES))
        lo = jnp.broadcast_to(tab_ref[h:h + 1, 0:MAX_REL], (LANES, LANES))
        hi_d = pltpu.roll(hi, 0, 1, stride=1, stride_axis=0)
        lo_d = pltpu.roll(lo, 0, 1, stride=1, stride_axis=0)
        bias_t[h, 0:WIN - 2 * LANES, :] = jnp.broadcast_to(far[0:1, :], (WIN - 2 * LANES, LANES))
        bias_t[h, WIN - 2 * LANES:WIN - LANES, :] = jnp.where(upper, far, hi_d)
        bias_t[h, WIN - LANES:WIN, :] = jnp.where(upper, hi_d, lo_d)
        bias_t[h, 0:CHUNK, :] = jnp.where(lane64 < CHUNK, bias_t[h, 0:CHUNK, :], NEG_INF)
        bias_t[h, WIN - CHUNK:WIN, :] = jnp.where(lane64 >= CHUNK, bias_t[h, WIN - CHUNK:WIN, :], NEG_INF)


def _head_rows(h):
    return slice(h * HEAD_DIM, (h + 1) * HEAD_DIM)


def _attn_scores(k_pad, q_ref, s_buf, w0):
    for h in range(N_HEADS):
        q_h = q_ref[_head_rows(h), :] * jnp.asarray(SCALE, BF16)
        s_buf[h] = lax.dot_general(k_pad[_head_rows(h), pl.ds(w0, WIN)], q_h, _DOT_DIMS["tn"],
                                   preferred_element_type=F32)


def _attn_logits(s, bias, first_valid, key0=0):
    s = s + bias
    if first_valid is not None:
        s = jnp.where(lax.broadcasted_iota(jnp.int32, s.shape, 0) + key0 >= first_valid, s, NEG_INF)
    return s


def _attn_probs(s, bias_h, first_valid):
    s = _attn_logits(s, bias_h, first_valid)
    top = jnp.max(s, axis=0, keepdims=True)
    e = jnp.exp(s - top)
    total = jnp.sum(e, axis=0, keepdims=True)
    return e * (1.0 / total), top + jnp.log(total)


def _attn_by_padding(m, fn):
    @pl.when(m < LEFT // QBLK)
    def _():
        fn(LEFT - m * QBLK)

    @pl.when(m >= LEFT // QBLK)
    def _():
        fn(None)


def _attn_forward(qkv_t, rel_bias):
    t_cols = qkv_t.shape[1]
    steps = t_cols // QBLK

    def body(q_ref, kv_hbm, tab_ref, o_ref, lse_ref, k_pad, v_pad, bias_t, s_buf, p_buf, sem):
        m = pl.program_id(0)

        @pl.when(m == 0)
        def _():
            _attn_build_bias(tab_ref, bias_t)
            _attn_load_kv(kv_hbm, k_pad, v_pad, sem, t_cols)

        w0 = pl.multiple_of(m * QBLK, QBLK)
        _attn_scores(k_pad, q_ref, s_buf, w0)

        def softmax(first_valid):
            for h in range(N_HEADS):
                p, lse = _attn_probs(s_buf[h], bias_t[h], first_valid)
                p_buf[h] = p.astype(BF16)
                lse_ref[h:h + 1, :] = lse

        _attn_by_padding(m, softmax)
        for h in range(N_HEADS):
            o_h = lax.dot_general(v_pad[_head_rows(h), pl.ds(w0, WIN)], p_buf[h], _DOT_DIMS["nn"],
                                  preferred_element_type=F32)
            o_ref[_head_rows(h), :] = o_h.astype(BF16)

    return pl.pallas_call(
        body, name="attn_forward", grid=(steps,),
        in_specs=[pl.BlockSpec((ATTN_WIDTH, QBLK), lambda m: (0, m)),
                  pl.BlockSpec(memory_space=pl.ANY),
                  pl.BlockSpec((N_HEADS, 2 * MAX_REL + 1), lambda m: (0, 0))],
        out_specs=[pl.BlockSpec((ATTN_WIDTH, QBLK), lambda m: (0, m)), pl.BlockSpec((N_HEADS, QBLK), lambda m: (0, m))],
        out_shape=[jax.ShapeDtypeStruct((ATTN_WIDTH, t_cols), BF16), jax.ShapeDtypeStruct((N_HEADS, t_cols), F32)],
        scratch_shapes=[pltpu.VMEM((ATTN_WIDTH, LEFT + t_cols), BF16), pltpu.VMEM((ATTN_WIDTH, LEFT + t_cols), BF16),
                        pltpu.VMEM((N_HEADS, WIN, QBLK), F32), pltpu.VMEM((N_HEADS, WIN, QBLK), F32),
                        pltpu.VMEM((N_HEADS, WIN, QBLK), BF16), pltpu.SemaphoreType.DMA((2,))],
        compiler_params=_params("arbitrary"),
    )(qkv_t, qkv_t, rel_bias)


def _reverse_lanes(v, flip):
    out = jnp.zeros(v.shape, F32)
    rest = v
    for _ in range(3):
        piece = rest.astype(BF16)
        out = out + lax.dot_general(piece, flip, _DOT_DIMS["nn"], preferred_element_type=F32)
        rest = rest - piece.astype(F32)
    return out


def _attn_bias_grad(dbias_t, drel_ref):
    row = lax.broadcasted_iota(jnp.int32, (LANES, LANES), 0)
    lane = lax.broadcasted_iota(jnp.int32, (LANES, LANES), 1)
    flip = (row + lane == LANES - 1).astype(BF16)
    head = lax.broadcasted_iota(jnp.int32, (N_HEADS, LANES), 0)
    lane8 = lax.broadcasted_iota(jnp.int32, (N_HEADS, LANES), 1)
    upper_rev = jnp.zeros((N_HEADS, LANES), F32)
    lower_rev = jnp.zeros((N_HEADS, LANES), F32)
    far = jnp.zeros((N_HEADS, LANES), F32)
    for h in range(N_HEADS):
        def diagonals(block):
            skew = pltpu.roll(_reverse_lanes(block, flip), 0, 1, stride=1, stride_axis=0)
            pos = jnp.sum(jnp.where(lane >= row, skew, 0.0), axis=0, keepdims=True)
            neg = jnp.sum(jnp.where(lane < row, skew, 0.0), axis=0, keepdims=True)
            return pos, neg

        pos4, neg4 = diagonals(dbias_t[h, WIN - LANES:WIN, :])
        pos3, neg3 = diagonals(dbias_t[h, WIN - 2 * LANES:WIN - LANES, :])
        far_h = jnp.sum(jnp.sum(dbias_t[h, 0:WIN - 2 * LANES, :], axis=0, keepdims=True), axis=1, keepdims=True)
        far_h = far_h + jnp.sum(pos3, axis=1, keepdims=True)
        upper_rev = jnp.where(head == h, pos4 + neg3, upper_rev)
        lower_rev = jnp.where(head == h, neg4, lower_rev)
        far = jnp.where((head == h) & (lane8 == 0), far_h, far)
    drel_ref[:, 0:LANES] = _reverse_lanes(lower_rev, flip)
    drel_ref[:, LANES:2 * LANES] = _reverse_lanes(upper_rev, flip)
    drel_ref[:, 2 * LANES:3 * LANES] = far


def _attn_backward(qkv_t, o_t, do_t, lse, rel_bias):
    t_cols = qkv_t.shape[1]
    steps = t_cols // QBLK
    flush = LEFT // QBLK
    total = steps + flush

    def body(q_ref, o_ref, do_ref, lse_ref, kv_hbm, tab_ref, dq_ref, dk_ref, dv_ref, drel_ref,
             k_pad, v_pad, bias_t, dbias_t, dk_acc, dv_acc, s_buf, dp_buf, p_buf, ds_buf, sem):
        m = pl.program_id(0)

        @pl.when(m == 0)
        def _():
            _attn_build_bias(tab_ref, bias_t)
            _attn_load_kv(kv_hbm, k_pad, v_pad, sem, t_cols)
            dbias_t[...] = jnp.zeros_like(dbias_t)
            dk_acc[...] = jnp.zeros_like(dk_acc)
            dv_acc[...] = jnp.zeros_like(dv_acc)

        @pl.when(m < steps)
        def _():
            w0 = pl.multiple_of(m * QBLK, QBLK)
            _attn_scores(k_pad, q_ref, s_buf, w0)
            for h in range(N_HEADS):
                dp_buf[h] = lax.dot_general(v_pad[_head_rows(h), pl.ds(w0, WIN)], do_ref[_head_rows(h), :],
                                            _DOT_DIMS["tn"], preferred_element_type=F32)

            def softmax_backward(first_valid):
                for h in range(N_HEADS):
                    rows = _head_rows(h)
                    delta = jnp.sum(do_ref[rows, :].astype(F32) * o_ref[rows, :].astype(F32), axis=0, keepdims=True)
                    lse_h = lse_ref[h:h + 1, :]
                    for b in range(WIN // LANES):
                        keys = slice(b * LANES, (b + 1) * LANES)
                        s = _attn_logits(s_buf[h, keys, :], bias_t[h, keys, :], first_valid, b * LANES)
                        p = jnp.exp(s - lse_h)
                        ds = p * (dp_buf[h, keys, :] - delta)
                        dbias_t[h, keys, :] += ds
                        p_buf[h, keys, :] = p.astype(BF16)
                        ds_buf[h, keys, :] = (ds * SCALE).astype(BF16)

            _attn_by_padding(m, softmax_backward)
            for h in range(N_HEADS):
                rows = _head_rows(h)
                dq_h = lax.dot_general(k_pad[rows, pl.ds(w0, WIN)], ds_buf[h], _DOT_DIMS["nn"], preferred_element_type=F32)
                dq_ref[rows, :] = dq_h.astype(BF16)
                dk_h = lax.dot_general(q_ref[rows, :], ds_buf[h], _DOT_DIMS["nt"], preferred_element_type=F32)
                dv_h = lax.dot_general(do_ref[rows, :], p_buf[h], _DOT_DIMS["nt"], preferred_element_type=F32)
                for b in range(WIN // QBLK):
                    slot = pl.multiple_of(lax.rem(m + b, WIN // QBLK) * QBLK, QBLK)
                    dk_acc[rows, pl.ds(slot, QBLK)] += dk_h[:, b * QBLK:(b + 1) * QBLK]
                    dv_acc[rows, pl.ds(slot, QBLK)] += dv_h[:, b * QBLK:(b + 1) * QBLK]

        oldest = pl.multiple_of(lax.rem(m, WIN // QBLK) * QBLK, QBLK)
        dk_ref[...] = dk_acc[:, pl.ds(oldest, QBLK)].astype(BF16)
        dv_ref[...] = dv_acc[:, pl.ds(oldest, QBLK)].astype(BF16)
        dk_acc[:, pl.ds(oldest, QBLK)] = jnp.zeros((ATTN_WIDTH, QBLK), F32)
        dv_acc[:, pl.ds(oldest, QBLK)] = jnp.zeros((ATTN_WIDTH, QBLK), F32)

        @pl.when(m == total - 1)
        def _():
            _attn_bias_grad(dbias_t, drel_ref)

    qblk = pl.BlockSpec((ATTN_WIDTH, QBLK), lambda m: (0, jnp.minimum(m, steps - 1)))
    kblk = pl.BlockSpec((ATTN_WIDTH, QBLK), lambda m: (0, jnp.maximum(m - flush, 0)))
    dq, dk, dv, drel = pl.pallas_call(
        body, name="attn_backward", grid=(total,),
        in_specs=[qblk, qblk, qblk, pl.BlockSpec((N_HEADS, QBLK), lambda m: (0, jnp.minimum(m, steps - 1))),
                  pl.BlockSpec(memory_space=pl.ANY), pl.BlockSpec((N_HEADS, 2 * MAX_REL + 1), lambda m: (0, 0))],
        out_specs=[qblk, kblk, kblk, pl.BlockSpec((N_HEADS, 3 * LANES), lambda m: (0, 0))],
        out_shape=[jax.ShapeDtypeStruct((ATTN_WIDTH, t_cols), BF16)] * 3
        + [jax.ShapeDtypeStruct((N_HEADS, 3 * LANES), F32)],
        scratch_shapes=[pltpu.VMEM((ATTN_WIDTH, LEFT + t_cols), BF16), pltpu.VMEM((ATTN_WIDTH, LEFT + t_cols), BF16),
                        pltpu.VMEM((N_HEADS, WIN, QBLK), F32), pltpu.VMEM((N_HEADS, WIN, QBLK), F32),
                        pltpu.VMEM((ATTN_WIDTH, WIN), F32), pltpu.VMEM((ATTN_WIDTH, WIN), F32),
                        pltpu.VMEM((N_HEADS, WIN, QBLK), F32), pltpu.VMEM((N_HEADS, WIN, QBLK), F32),
                        pltpu.VMEM((N_HEADS, WIN, QBLK), BF16), pltpu.VMEM((N_HEADS, WIN, QBLK), BF16),
                        pltpu.SemaphoreType.DMA((2,))],
        compiler_params=_params("arbitrary"),
    )(qkv_t, o_t, do_t, lse, qkv_t, rel_bias)
    return dq, dk, dv, drel


FFN_TC = D_FF // 2
FFN_TR = 256


def _ffn_specs(t_rows, tr):
    nj = D_FF // FFN_TC
    hb = tr // FFN_HALO
    last_halo = t_rows // FFN_HALO - 1
    cur = lambda off: pl.BlockSpec((tr, FFN_TC), lambda j, i: (i, j + off))
    prev = lambda off: pl.BlockSpec((FFN_HALO, FFN_TC), lambda j, i: (jnp.maximum(i * hb - 1, 0), j + off))
    nxt = lambda off: pl.BlockSpec((FFN_HALO, FFN_TC), lambda j, i: (jnp.minimum((i + 1) * hb, last_halo), j + off))
    wspec = lambda off: pl.BlockSpec((FFN_K, FFN_TC), lambda j, i: (0, j + off))
    bspec = lambda off: pl.BlockSpec((1, FFN_TC), lambda j, i: (0, j + off))
    return nj, cur, prev, nxt, wspec, bspec


FFN_STRIP = 16


def _ffn_conv(win, w, b, rows):
    out = b + w[2] * win[FFN_HALO:FFN_HALO + rows, :]
    out = out + w[1] * win[FFN_HALO - 1:FFN_HALO - 1 + rows, :]
    return out + w[0] * win[FFN_HALO - 2:FFN_HALO - 2 + rows, :]


def _taps(w_ref):
    return [w_ref[kk:kk + 1, :] for kk in range(FFN_K)]


def _ffn_first_window(prev_ref, cur_ref, tile, rows):
    return jnp.concatenate([jnp.where(tile > 0, prev_ref[...], 0.0), cur_ref[0:rows, :]], axis=0)


def _fold8(v):
    return jnp.sum(v.reshape(v.shape[0] // 8, 8, v.shape[1]), axis=0)


def _ffn_activation(hup, w, b):
    t_rows = hup.shape[0]
    tr = _row_tile(t_rows, FFN_TR)
    nj, cur, prev, nxt, wspec, bspec = _ffn_specs(t_rows, tr)
    rs = FFN_STRIP

    def body(g_ref, gprev_ref, v_ref, vprev_ref, wg_ref, wv_ref, bg_ref, bv_ref, act_ref, gel_ref, slope_ref):
        i = pl.program_id(1)
        wg, wv, bg, bv = _taps(wg_ref), _taps(wv_ref), bg_ref[...], bv_ref[...]

        def emit(base, g_win, v_win):
            gel, dgel = _gelu_parts(_ffn_conv(g_win, wg, bg, rs))
            cv = _ffn_conv(v_win, wv, bv, rs)
            act_ref[pl.ds(base, rs), :] = (gel * cv).astype(BF16)
            gel_ref[pl.ds(base, rs), :] = gel
            slope_ref[pl.ds(base, rs), :] = cv * dgel

        def strip(s, carry):
            base = pl.multiple_of(s * rs, rs)
            emit(base, g_ref[pl.ds(base - FFN_HALO, rs + FFN_HALO), :], v_ref[pl.ds(base - FFN_HALO, rs + FFN_HALO), :])
            return carry

        emit(0, _ffn_first_window(gprev_ref, g_ref, i, rs), _ffn_first_window(vprev_ref, v_ref, i, rs))
        lax.fori_loop(1, tr // rs, strip, 0)

    return pl.pallas_call(
        body, name="ffn_activation", grid=(nj, t_rows // tr),
        in_specs=[cur(0), prev(0), cur(nj), prev(nj), wspec(0), wspec(nj), bspec(0), bspec(nj)],
        out_specs=[cur(0), cur(0), cur(0)],
        out_shape=[jax.ShapeDtypeStruct((t_rows, D_FF), BF16), jax.ShapeDtypeStruct((t_rows, D_FF), F32),
                   jax.ShapeDtypeStruct((t_rows, D_FF), F32)],
        compiler_params=_params("parallel", "parallel"),
    )(hup, hup, hup, hup, w, w, b, b)


def _ffn_backward(dact, gel, slope, hup, w):
    t_rows = hup.shape[0]
    tr = _row_tile(t_rows, FFN_TR)
    nj, cur, prev, nxt, wspec, bspec = _ffn_specs(t_rows, tr)
    ni = t_rows // tr
    rs = FFN_STRIP
    ns = tr // rs

    def body(da_ref, danext_ref, gel_ref, gelnext_ref, slope_ref, slopenext_ref, g_ref, gprev_ref, v_ref, vprev_ref,
             wg_ref, wv_ref, dhg_ref, dhv_ref, dwg_ref, dwv_ref, dbg_ref, dbv_ref, sums):
        i = pl.program_id(1)

        @pl.when(i == 0)
        def _():
            sums[...] = jnp.zeros_like(sums)

        wg, wv = _taps(wg_ref), _taps(wv_ref)
        da_after = jnp.where(i < ni - 1, danext_ref[...], 0.0)

        def strip_at(base, g_win, v_win, carry):
            da = da_ref[pl.ds(base, rs), :]
            dcg, dcv = da * slope_ref[pl.ds(base, rs), :], da * gel_ref[pl.ds(base, rs), :]
            out = []
            for half_i, (dc, after, taps, win, dh_ref) in enumerate(((dcg, carry[0], wg, g_win, dhg_ref),
                                                                   (dcv, carry[1], wv, v_win, dhv_ref))):
                ext = jnp.concatenate([dc, after], axis=0)
                dh = taps[2] * dc + taps[1] * ext[1:1 + rs, :] + taps[0] * ext[2:2 + rs, :]
                dh_ref[pl.ds(base, rs), :] = dh.astype(BF16)
                sums[4 * half_i] += _fold8(dc)
                for kk in range(FFN_K):
                    sums[4 * half_i + 1 + kk] += _fold8(dc * win[FFN_HALO - 2 + kk:FFN_HALO - 2 + kk + rs, :])
                out.append(dc[0:FFN_HALO, :])
            return tuple(out)

        def strip(s, carry):
            base = pl.multiple_of((ns - 1 - s) * rs, rs)
            return strip_at(base, g_ref[pl.ds(base - FFN_HALO, rs + FFN_HALO), :],
                            v_ref[pl.ds(base - FFN_HALO, rs + FFN_HALO), :], carry)

        carry = lax.fori_loop(0, ns - 1, strip, (da_after * slopenext_ref[...], da_after * gelnext_ref[...]))
        strip_at(0, _ffn_first_window(gprev_ref, g_ref, i, rs), _ffn_first_window(vprev_ref, v_ref, i, rs), carry)

        @pl.when(i == ni - 1)
        def _():
            for half_i, (db_ref, dw_ref) in enumerate(((dbg_ref, dwg_ref), (dbv_ref, dwv_ref))):
                db_ref[...] = jnp.sum(sums[4 * half_i], axis=0, keepdims=True)
                for kk in range(FFN_K):
                    dw_ref[kk:kk + 1, :] = jnp.sum(sums[4 * half_i + 1 + kk], axis=0, keepdims=True)

    half = jax.ShapeDtypeStruct((t_rows, D_FF), BF16)
    return pl.pallas_call(
        body, name="ffn_backward", grid=(nj, ni),
        in_specs=[cur(0), nxt(0), cur(0), nxt(0), cur(0), nxt(0), cur(0), prev(0), cur(nj), prev(nj),
                  wspec(0), wspec(nj)],
        out_specs=[cur(0), cur(0), wspec(0), wspec(0), bspec(0), bspec(0)],
        out_shape=[half, half, jax.ShapeDtypeStruct((FFN_K, D_FF), F32), jax.ShapeDtypeStruct((FFN_K, D_FF), F32),
                   jax.ShapeDtypeStruct((1, D_FF), F32), jax.ShapeDtypeStruct((1, D_FF), F32)],
        scratch_shapes=[pltpu.VMEM((2 * (1 + FFN_K), 8, FFN_TC), F32)],
        compiler_params=_params("parallel", "arbitrary"),
    )(dact, dact, gel, gel, slope, slope, hup, hup, hup, hup, w, w)


def _mesh_position():
    return lax.axis_index("x"), lax.axis_index("y"), lax.axis_index("c")


def _hbm_specs(n):
    return [pl.BlockSpec(memory_space=pl.ANY)] * n


def _all_gather(shards, name, placed=None):
    n = len(shards)

    def body(*refs):
        ins = refs[:n]
        outs = refs[2 * n:3 * n] if placed else refs[n:2 * n]
        send_sems, recv_sems, local_sems = refs[-3:]
        x, y, c = _mesh_position()
        me, sibling = (x, y, c), (x, y, 1 - c)
        chips = [(1 - x, y), (x, 1 - y), (1 - x, 1 - y)]

        def copy(a, slot, block, to, src=None):
            dst = outs[a].at[4 * block[0] + 2 * block[1] + block[2]]
            return pltpu.make_async_remote_copy(
                src_ref=dst if src is None else src, dst_ref=dst,
                send_sem=send_sems.at[a, slot], recv_sem=recv_sems.at[a, slot],
                device_id=to, device_id_type=MESH)

        started = []
        for a in range(0 if placed else n):
            mine = pltpu.make_async_copy(ins[a], outs[a].at[4 * x + 2 * y + c], local_sems.at[a])
            mine.start()
            started.append(mine)
        first = []
        for a in range(n):
            first.append(copy(a, 0, me, sibling, src=ins[a]))
            first += [copy(a, 1 + j, me, (*chip, c), src=ins[a]) for j, chip in enumerate(chips)]
        for cp in first:
            cp.start()
        passed = []
        for j, chip in enumerate(chips):
            for a in range(n):
                copy(a, 1 + j, (*chip, c), me).wait_recv()
                fwd = copy(a, 4 + j, (*chip, c), sibling)
                fwd.start()
                passed.append(fwd)
        for a in range(n):
            copy(a, 0, sibling, me).wait_recv()
            for j, chip in enumerate(chips):
                copy(a, 4 + j, (*chip, 1 - c), me).wait_recv()
        for cp in first + passed:
            cp.wait_send()
        for mine in started:
            mine.wait()

    operands = [*shards, *placed] if placed else list(shards)
    return pl.pallas_call(
        body, name=name,
        in_specs=_hbm_specs(len(operands)), out_specs=_hbm_specs(n),
        out_shape=[jax.ShapeDtypeStruct((N_DEV,) + s.shape, s.dtype) for s in shards],
        scratch_shapes=[pltpu.SemaphoreType.DMA((n, 7)), pltpu.SemaphoreType.DMA((n, 7)),
                        pltpu.SemaphoreType.DMA((n,))],
        input_output_aliases={n + a: a for a in range(n)} if placed else {},
        compiler_params=pltpu.CompilerParams(has_side_effects=True),
    )(*operands)


def _place_own(shards, me):
    n = len(shards)

    def body(me_ref, *refs):
        for src, dst in zip(refs[:n], refs[n:]):
            dst[0] = src[...]

    return pl.pallas_call(
        body, name="place_own_shards",
        grid_spec=pltpu.PrefetchScalarGridSpec(
            num_scalar_prefetch=1, grid=(1,),
            in_specs=[pl.BlockSpec(s.shape, lambda i, me_ref: (0, 0)) for s in shards],
            out_specs=[pl.BlockSpec((1,) + s.shape, lambda i, me_ref: (me_ref[0], 0, 0)) for s in shards]),
        out_shape=[jax.ShapeDtypeStruct((N_DEV,) + s.shape, s.dtype) for s in shards],
        compiler_params=_params("arbitrary"),
    )(me, *shards)


_FLIPS = [(dx, dy, dc) for dx in (0, 1) for dy in (0, 1) for dc in (0, 1)][1:]
_HBM = pl.BlockSpec(memory_space=pltpu.HBM)
_SEM = pl.BlockSpec(memory_space=pltpu.SEMAPHORE)
_DATAFLOW = pltpu.SideEffectType.DATAFLOW_SIDE_EFFECTING


def _scatter_copies(src_refs, land_refs, send_sems, recv_sems, gather):
    x, y, c = _mesh_position()
    me = 4 * x + 2 * y + c
    copies = []
    for a, (src, land) in enumerate(zip(src_refs, land_refs)):
        for k, (dx, dy, dc) in enumerate(_FLIPS):
            px, py, pc = (x + dx) % 2, (y + dy) % 2, (c + dc) % 2
            pair = a * len(_FLIPS) + k
            copies.append(pltpu.make_async_remote_copy(
                src_ref=src if gather else src.at[4 * px + 2 * py + pc], dst_ref=land.at[me],
                send_sem=send_sems[pair], recv_sem=recv_sems[pair],
                device_id=(px, py, pc), device_id_type=MESH))
    return copies


def _scatter_start(srcs, lands, gather, name, after=None):
    n = len(srcs)
    pairs = n * len(_FLIPS)
    extra = [] if after is None else [after]

    def body(*refs):
        src_refs, land_refs = refs[:n], refs[n:2 * n]
        first = 2 * n + len(extra)
        send_sems, recv_sems = refs[first:first + pairs], refs[first + pairs:first + 2 * pairs]
        token = refs[-1]
        for cp in _scatter_copies(src_refs, land_refs, send_sems, recv_sems, gather):
            cp.start()
        token[...] = jnp.zeros_like(token)

    arrays = [*srcs, *lands]
    sem = pltpu.SemaphoreType.DMA(())
    out = pl.pallas_call(
        body, name=name,
        out_shape=(*[sem] * (2 * pairs), *[pltpu.HBM(v.shape, v.dtype) for v in arrays],
                   jax.ShapeDtypeStruct((8, LANES), F32)),
        in_specs=[*[_HBM] * (2 * n), *[pl.BlockSpec(memory_space=pl.ANY)] * len(extra)],
        out_specs=(*[_SEM] * (2 * pairs), *[_HBM] * (2 * n), pl.BlockSpec(memory_space=pltpu.VMEM)),
        input_output_aliases={i: 2 * pairs + i for i in range(2 * n)},
        compiler_params=pltpu.CompilerParams(has_side_effects=_DATAFLOW),
    )(*[pltpu.with_memory_space_constraint(v, pltpu.HBM) for v in arrays], *extra)
    sems, rest = out[:2 * pairs], out[2 * pairs:]
    return list(sems[:pairs]), list(sems[pairs:]), list(rest[:n]), list(rest[n:2 * n]), rest[-1]


def _scatter_wait(send_sems, recv_sems, srcs, lands, after, gather, name):
    n = len(srcs)
    pairs = n * len(_FLIPS)

    def body(*refs):
        src_refs, land_refs = refs[:n], refs[n:2 * n]
        send_refs, recv_refs = refs[2 * n:2 * n + pairs], refs[2 * n + pairs:2 * n + 2 * pairs]
        for cp in _scatter_copies(src_refs, land_refs, send_refs, recv_refs, gather):
            cp.wait_send()
            cp.wait_recv()

    arrays = [*srcs, *lands]
    out = pl.pallas_call(
        body, name=name,
        out_shape=tuple(pltpu.HBM(v.shape, v.dtype) for v in arrays),
        in_specs=[*[_HBM] * (2 * n), *[_SEM] * (2 * pairs), pl.BlockSpec(memory_space=pl.ANY)],
        out_specs=tuple([_HBM] * (2 * n)),
        input_output_aliases={i: i for i in range(2 * n)},
        compiler_params=pltpu.CompilerParams(has_side_effects=_DATAFLOW),
    )(*arrays, *send_sems, *recv_sems, after)
    return list(out[:n]), list(out[n:])


def _sum_received(grad, received, me, name):
    _, rows, cols = grad.shape

    def body(me_ref, g_ref, r_ref, o_ref):
        p = pl.program_id(0)
        term = jnp.where(p == me_ref[0], g_ref[0], r_ref[0])

        @pl.when(p == 0)
        def _():
            o_ref[...] = term

        @pl.when(p > 0)
        def _():
            o_ref[...] += term

    blk = (1, rows, cols)
    return pl.pallas_call(
        body, name=name,
        grid_spec=pltpu.PrefetchScalarGridSpec(
            num_scalar_prefetch=1, grid=(N_DEV,),
            in_specs=[pl.BlockSpec(blk, lambda p, me_ref: (me_ref[0], 0, 0)),
                      pl.BlockSpec(blk, lambda p, me_ref: (p, 0, 0))],
            out_specs=pl.BlockSpec((rows, cols), lambda p, me_ref: (0, 0))),
        out_shape=jax.ShapeDtypeStruct((rows, cols), F32),
        compiler_params=_params("arbitrary"),
    )(me, grad, received)


def _exchange_in_chip(grads):
    n = len(grads)

    def body(*refs):
        ins, outs = refs[:n], refs[n:2 * n]
        send_sems, recv_sems = refs[2 * n:]
        x, y, c = _mesh_position()
        copies = []
        for a in range(n):
            for q in range(4):
                copies.append(pltpu.make_async_remote_copy(
                    src_ref=ins[a].at[2 * q + (1 - c)], dst_ref=outs[a].at[q],
                    send_sem=send_sems.at[a, q], recv_sem=recv_sems.at[a, q],
                    device_id=(x, y, 1 - c), device_id_type=MESH))
        for cp in copies:
            cp.start()
        for cp in copies:
            cp.wait_recv()
        for cp in copies:
            cp.wait_send()

    return pl.pallas_call(
        body, name="exchange_in_chip",
        in_specs=_hbm_specs(n), out_specs=_hbm_specs(n),
        out_shape=[jax.ShapeDtypeStruct((4,) + g.shape[1:], g.dtype) for g in grads],
        scratch_shapes=[pltpu.SemaphoreType.DMA((n, 4)), pltpu.SemaphoreType.DMA((n, 4))],
        compiler_params=pltpu.CompilerParams(has_side_effects=True),
    )(*grads)


def _exchange_between_chips(partials):
    n = len(partials)

    def body(*refs):
        ins, outs = refs[:n], refs[n:2 * n]
        send_sems, recv_sems = refs[2 * n:]
        x, y, c = _mesh_position()
        chips = [(1 - x, y), (x, 1 - y), (1 - x, 1 - y)]
        copies = []
        for a in range(n):
            for j, (px, py) in enumerate(chips):
                copies.append(pltpu.make_async_remote_copy(
                    src_ref=ins[a].at[2 * px + py], dst_ref=outs[a].at[j],
                    send_sem=send_sems.at[a, j], recv_sem=recv_sems.at[a, j],
                    device_id=(px, py, c), device_id_type=MESH))
        for cp in copies:
            cp.start()
        for cp in copies:
            cp.wait_recv()
        for cp in copies:
            cp.wait_send()

    return pl.pallas_call(
        body, name="exchange_between_chips",
        in_specs=_hbm_specs(n), out_specs=_hbm_specs(n),
        out_shape=[jax.ShapeDtypeStruct((3,) + p.shape[1:], p.dtype) for p in partials],
        scratch_shapes=[pltpu.SemaphoreType.DMA((n, 3)), pltpu.SemaphoreType.DMA((n, 3))],
        compiler_params=pltpu.CompilerParams(has_side_effects=True),
    )(*partials)


def _add_in_chip(grad, received, core, name):
    _, rows, cols = grad.shape

    def body(core_ref, g_ref, r_ref, o_ref):
        o_ref[...] = g_ref[...] + r_ref[...]

    blk = (1, rows, cols)
    return pl.pallas_call(
        body, name=name,
        grid_spec=pltpu.PrefetchScalarGridSpec(
            num_scalar_prefetch=1, grid=(4,),
            in_specs=[pl.BlockSpec(blk, lambda q, core_ref: (2 * q + core_ref[0], 0, 0)),
                      pl.BlockSpec(blk, lambda q, core_ref: (q, 0, 0))],
            out_specs=pl.BlockSpec(blk, lambda q, core_ref: (q, 0, 0))),
        out_shape=jax.ShapeDtypeStruct((4, rows, cols), F32),
        compiler_params=_params("parallel"),
    )(core, grad, received)


def _add_between_chips(partial, received, chip, name):
    _, rows, cols = partial.shape

    def body(chip_ref, p_ref, r_ref, o_ref):
        o_ref[...] = ((p_ref[0] + r_ref[0]) + r_ref[1]) + r_ref[2]

    return pl.pallas_call(
        body, name=name,
        grid_spec=pltpu.PrefetchScalarGridSpec(
            num_scalar_prefetch=1, grid=(1,),
            in_specs=[pl.BlockSpec((1, rows, cols), lambda i, chip_ref: (chip_ref[0], 0, 0)),
                      pl.BlockSpec((3, rows, cols), lambda i, chip_ref: (0, 0, 0))],
            out_specs=pl.BlockSpec((rows, cols), lambda i, chip_ref: (0, 0))),
        out_shape=jax.ShapeDtypeStruct((rows, cols), F32),
        compiler_params=_params("arbitrary"),
    )(chip, partial, received)


def _sum_devices(gathered):
    _, rows, cols = gathered.shape

    def body(g_ref, o_ref):
        total = g_ref[0]
        for d in range(1, N_DEV):
            total = total + g_ref[d]
        o_ref[...] = total

    return pl.pallas_call(
        body, name="sum_small_grads",
        out_shape=jax.ShapeDtypeStruct((rows, cols), F32),
        compiler_params=_params(),
    )(gathered)


def _adamw(w, g, m, v, name):
    rows, cols = w.shape
    tr = rows
    for cand in (256, 128, 64, 32, 16, 8):
        if rows > cand and rows % cand == 0:
            tr = cand
            break

    def body(w_ref, g_ref, m_ref, v_ref, delta_ref, newm_ref, newv_ref):
        g_v = g_ref[...]
        new_m = ADAM_B1 * m_ref[...] + (1.0 - ADAM_B1) * g_v
        new_v = ADAM_B2 * v_ref[...] + (1.0 - ADAM_B2) * (g_v * g_v)
        m_hat = new_m / (1.0 - ADAM_B1 ** ADAM_STEP)
        v_hat = new_v / (1.0 - ADAM_B2 ** ADAM_STEP)
        delta_ref[...] = -ADAM_LR * (m_hat / (jnp.sqrt(v_hat) + ADAM_EPS) + ADAM_WD * w_ref[...])
        newm_ref[...] = new_m
        newv_ref[...] = new_v

    blk = pl.BlockSpec((tr, cols), lambda i: (i, 0))
    shape = jax.ShapeDtypeStruct((rows, cols), F32)
    return pl.pallas_call(
        body, name=name, grid=(rows // tr,),
        in_specs=[blk] * 4, out_specs=[blk] * 3, out_shape=[shape] * 3,
        compiler_params=_params("parallel"),
    )(w, g, m, v)


def _adamw_update(w, g, m, v):
    new_m = ADAM_B1 * m + (1.0 - ADAM_B1) * g
    new_v = ADAM_B2 * v + (1.0 - ADAM_B2) * (g * g)
    m_hat = new_m / (1.0 - ADAM_B1 ** ADAM_STEP)
    v_hat = new_v / (1.0 - ADAM_B2 ** ADAM_STEP)
    return -ADAM_LR * (m_hat / (jnp.sqrt(v_hat) + ADAM_EPS) + ADAM_WD * w), new_m, new_v


def _reduce_and_adamw(grad, received, me, w, m, v, name):
    _, rows, cols = grad.shape
    tr = rows // 2
    assert rows % 16 == 0

    def body(me_ref, g_ref, r_ref, w_ref, m_ref, v_ref, grad_ref, delta_ref, newm_ref, newv_ref, acc):
        p = pl.program_id(1)
        term = jnp.where(p == me_ref[0], g_ref[0], r_ref[0]).astype(F32)

        @pl.when(p == 0)
        def _():
            acc[...] = term

        @pl.when(p > 0)
        def _():
            acc[...] += term

        @pl.when(p == N_DEV - 1)
        def _():
            g = acc[...]
            grad_ref[...] = g
            delta_ref[...], newm_ref[...], newv_ref[...] = _adamw_update(w_ref[...], g, m_ref[...], v_ref[...])

    blk = (1, tr, cols)
    tile = pl.BlockSpec((tr, cols), lambda j, p, me_ref: (j, 0))
    shape = jax.ShapeDtypeStruct((rows, cols), F32)
    return pl.pallas_call(
        body, name=name,
        grid_spec=pltpu.PrefetchScalarGridSpec(
            num_scalar_prefetch=1, grid=(rows // tr, N_DEV),
            in_specs=[pl.BlockSpec(blk, lambda j, p, me_ref: (me_ref[0], j, 0)),
                      pl.BlockSpec(blk, lambda j, p, me_ref: (p, j, 0)), tile, tile, tile],
            out_specs=[tile] * 4,
            scratch_shapes=[pltpu.VMEM((tr, cols), F32)]),
        out_shape=[shape] * 4,
        compiler_params=_params("parallel", "arbitrary"),
    )(me, grad, received, w, m, v)


def _pack(pieces, rows):
    flat = jnp.concatenate([p.reshape(-1) for p in pieces])
    return jnp.pad(flat, (0, rows * LANES - flat.shape[0])).reshape(rows, LANES)


def _unpack(packed, shapes):
    flat = packed.reshape(-1)
    out, pos = [], 0
    for shape in shapes:
        size = 1
        for s in shape:
            size *= s
        out.append(flat[pos:pos + size].reshape(shape))
        pos += size
    return out


def _rows_for(count):
    return -(-count // (8 * LANES)) * 8


SMALL = ("norm_mix_pre", "conv_dw_b", "conv_ln_g", "conv_ln_b", "rel_bias", "norm_mix_post", "norm_ffn_pre",
         "ffn_dw_b", "norm_ffn_post")
SHARDED_SMALL = ("conv_dw_w", "ffn_dw_w")
LARGE = ("w_in", "w_out", "w_up", "w_down")
WEIGHTS = ("norm_mix_pre", "w_in", "conv_dw_w", "conv_dw_b", "conv_ln_g", "conv_ln_b", "rel_bias", "w_out",
           "norm_mix_post", "norm_ffn_pre", "w_up", "ffn_dw_w", "ffn_dw_b", "w_down", "norm_ffn_post")


def kernel(x, norm_mix_pre, w_in, conv_dw_w, conv_dw_b, conv_ln_g, conv_ln_b, rel_bias, w_out, norm_mix_post, norm_ffn_pre, w_up, ffn_dw_w, ffn_dw_b, w_down, norm_ffn_post, loss_target, m_norm_mix_pre, m_w_in, m_conv_dw_w, m_conv_dw_b, m_conv_ln_g, m_conv_ln_b, m_rel_bias, m_w_out, m_norm_mix_post, m_norm_ffn_pre, m_w_up, m_ffn_dw_w, m_ffn_dw_b, m_w_down, m_norm_ffn_post, v_norm_mix_pre, v_w_in, v_conv_dw_w, v_conv_dw_b, v_conv_ln_g, v_conv_ln_b, v_rel_bias, v_w_out, v_norm_mix_post, v_norm_ffn_pre, v_w_up, v_ffn_dw_w, v_ffn_dw_b, v_w_down, v_norm_ffn_post):
    weights = dict(norm_mix_pre=norm_mix_pre, w_in=w_in, conv_dw_w=conv_dw_w, conv_dw_b=conv_dw_b, conv_ln_g=conv_ln_g,
                   conv_ln_b=conv_ln_b, rel_bias=rel_bias, w_out=w_out, norm_mix_post=norm_mix_post,
                   norm_ffn_pre=norm_ffn_pre, w_up=w_up, ffn_dw_w=ffn_dw_w, ffn_dw_b=ffn_dw_b, w_down=w_down,
                   norm_ffn_post=norm_ffn_post)
    mom1 = dict(norm_mix_pre=m_norm_mix_pre, w_in=m_w_in, conv_dw_w=m_conv_dw_w, conv_dw_b=m_conv_dw_b,
                conv_ln_g=m_conv_ln_g, conv_ln_b=m_conv_ln_b, rel_bias=m_rel_bias, w_out=m_w_out,
                norm_mix_post=m_norm_mix_post, norm_ffn_pre=m_norm_ffn_pre, w_up=m_w_up, ffn_dw_w=m_ffn_dw_w,
                ffn_dw_b=m_ffn_dw_b, w_down=m_w_down, norm_ffn_post=m_norm_ffn_post)
    mom2 = dict(norm_mix_pre=v_norm_mix_pre, w_in=v_w_in, conv_dw_w=v_conv_dw_w, conv_dw_b=v_conv_dw_b,
                conv_ln_g=v_conv_ln_g, conv_ln_b=v_conv_ln_b, rel_bias=v_rel_bias, w_out=v_w_out,
                norm_mix_post=v_norm_mix_post, norm_ffn_pre=v_norm_ffn_pre, w_up=v_w_up, ffn_dw_w=v_ffn_dw_w,
                ffn_dw_b=v_ffn_dw_b, w_down=v_w_down, norm_ffn_post=v_norm_ffn_post)

    x2 = x[0]
    target = loss_target[0]
    t_rows = x2.shape[0]
    d = D_MODEL
    in_cols = 2 * CONV_WIDTH + 3 * ATTN_WIDTH
    my_x, my_y, my_c = _mesh_position()
    my_dev = 4 * my_x + 2 * my_y + my_c

    small_conv = _pack([conv_dw_w[0], ffn_dw_w[0]], 32)
    me = jnp.reshape(my_dev, (1,)).astype(jnp.int32)
    first_shards = [w_in[0].T.astype(BF16), small_conv]
    late_shards = [w_out[0].astype(BF16), w_up[0].T.astype(BF16), w_down[0].astype(BF16)]
    placed = _place_own(first_shards + late_shards, me)
    win_t, conv_g = _all_gather(first_shards, "all_gather_weights", placed=placed[:2])
    wout_gather = _scatter_start(late_shards[:1], placed[2:3], True, "gather_w_out_start", after=win_t)
    ffn_gather = _scatter_start(late_shards[1:], placed[3:], True, "gather_ffn_weights_start", after=wout_gather[4])
    late_token = ffn_gather[4]
    win_t = win_t.reshape(in_cols, d)
    conv_flat = conv_g.reshape(N_DEV, 32 * LANES)
    n_cw = CONV_K * (CONV_WIDTH // N_DEV)
    conv_w_full = conv_flat[:, :n_cw].reshape(N_DEV, CONV_K, CONV_WIDTH // N_DEV).transpose(1, 0, 2).reshape(CONV_K, CONV_WIDTH)
    ffn_w_full = conv_flat[:, n_cw:].reshape(N_DEV, FFN_K, 2 * D_FF // N_DEV).transpose(1, 0, 2).reshape(FFN_K, 2 * D_FF)

    u1 = _pre_norm(x2, norm_mix_pre + late_token[0:1, 0:1], "pre_norm_mix")
    proj_a = _matmul(u1, win_t, mode="nt", m=t_rows, n=2 * CONV_WIDTH, k=d, tm=2048, tn=1024, tk=d,
                     out_dtype=F32, name="proj_conv")
    qkv_t = _matmul(win_t, u1, mode="nt", m=3 * ATTN_WIDTH, n=t_rows, k=d, tm=512, tn=2048, tk=d,
                    out_dtype=BF16, name="proj_qkv", a_m0=2 * CONV_WIDTH)
    conv_c, conv_out = _conv_forward(proj_a, conv_w_full, conv_dw_b, conv_ln_g, conv_ln_b)
    o_t, attn_lse = _attn_forward(qkv_t, rel_bias[0])
    _, (wout_g,) = _scatter_wait(*wout_gather[:4], o_t, True, "gather_w_out_wait")
    wout_g = wout_g.reshape(d, d)
    mixed, h1, u2 = _matmul_rows(
        [(conv_out, "nn", CONV_WIDTH, 0), (o_t, "tn", ATTN_WIDTH, CONV_WIDTH)], wout_g, m=t_rows, n=d, tm=1024,
        name="out_proj_mid_forward", row_ins=[x2], vec_ins=[norm_mix_post, norm_ffn_pre], row_outs=[F32, F32, BF16],
        acc_outs=[], epilogue=_mid_forward_epilogue)
    _, (wup_t, wdown_g) = _scatter_wait(*ffn_gather[:4], u2, True, "gather_ffn_weights_wait")
    wup_t = wup_t.reshape(2 * D_FF, d)
    wdown_g = wdown_g.reshape(D_FF, d)
    hup = _matmul(u2, wup_t, mode="nt", m=t_rows, n=2 * D_FF, k=d, tm=2048, tn=1408, tk=d,
                  out_dtype=F32, name="ffn_up")
    act, ffn_gel, ffn_slope = _ffn_activation(hup, ffn_w_full, ffn_dw_b)
    dy, df, _, d_norm_ffn_post, loss = _matmul_rows(
        [(act, "nn", D_FF, 0)], wdown_g, m=t_rows, n=d, tm=512, name="ffn_down_loss_backward",
        row_ins=[h1, target], vec_ins=[norm_ffn_post], row_outs=[F32, BF16], acc_outs=[(1, d), (1, d), (1, 1)],
        epilogue=_loss_epilogue)

    dact = _matmul(df, wdown_g, mode="nt", m=t_rows, n=D_FF, k=d, tm=2048, tn=1408, tk=d,
                   out_dtype=F32, name="ffn_down_dx")
    g_wdown = _matmul(act, df, mode="tn", m=D_FF, n=d, k=t_rows, tm=1408, tn=1024, tk=2048,
                      out_dtype=F32, name="ffn_down_dw")
    dhg, dhv, dwg, dwv, dbg, dbv = _ffn_backward(dact, ffn_gel, ffn_slope, hup, ffn_w_full)
    g_wup_t = _matmul(dhg, u2, mode="tn", m=D_FF, n=d, k=t_rows, tm=1408, tn=1024, tk=2048, out_dtype=F32,
                      name="ffn_up_dw_gate", out_rows=2 * D_FF)
    g_wup_t = _matmul(dhv, u2, mode="tn", m=D_FF, n=d, k=t_rows, tm=1408, tn=1024, tk=2048, out_dtype=F32,
                      name="ffn_up_dw_value", out_rows=2 * D_FF, out_m0=D_FF, into=g_wup_t)
    ffn_grads = [g_wup_t.reshape(N_DEV, 2 * D_FF // N_DEV, d), g_wdown.reshape(N_DEV, D_FF // N_DEV, d)]
    red_send, red_recv, ffn_grads, red_lands, red_token = _scatter_start(
        ffn_grads, [lax.empty(g.shape, F32) for g in ffn_grads], False, "reduce_ffn_grads_start")
    dh1, dmixed, d_norm_ffn_pre, d_norm_mix_post = _matmul_rows(
        [(dhg, "nn", D_FF, 0), (dhv, "nn", D_FF, D_FF)], wup_t, m=t_rows, n=d, tm=256, name="ffn_up_dx_mid_backward",
        row_ins=[dy, h1, mixed], vec_ins=[norm_ffn_pre + red_token[0:1, 0:1], norm_mix_post], row_outs=[F32, BF16],
        acc_outs=[(1, d), (1, d)], epilogue=_mid_backward_epilogue)
    dconv_out = _matmul(dmixed, wout_g, mode="nt", m=t_rows, n=CONV_WIDTH, k=d, tm=2048, tn=512, tk=d,
                        out_dtype=F32, name="out_proj_dx_conv")
    do_t = _matmul(wout_g, dmixed, mode="nt", m=ATTN_WIDTH, n=t_rows, k=d, tm=512, tn=2048, tk=d,
                   out_dtype=BF16, name="out_proj_dx_attn", a_m0=CONV_WIDTH)
    g_wout = _matmul(conv_out, dmixed, mode="tn", m=CONV_WIDTH, n=d, k=t_rows, tm=512, tn=1024, tk=2048, out_dtype=F32,
                     name="out_proj_dw_conv", out_rows=d)
    g_wout = _matmul(o_t, dmixed, mode="nn", m=ATTN_WIDTH, n=d, k=t_rows, tm=512, tn=1024, tk=2048, out_dtype=F32,
                     name="out_proj_dw_attn", out_rows=d, out_m0=CONV_WIDTH, into=g_wout)
    wout_handle = _scatter_start([g_wout.reshape(N_DEV, d // N_DEV, d)], [lax.empty((N_DEV, d // N_DEV, d), F32)],
                                 False, "reduce_w_out_grad_start")
    dproj_a, d_conv_w, d_conv_b, d_ln_g, d_ln_b = _conv_backward(
        dconv_out, conv_c, proj_a, conv_w_full, conv_ln_g + wout_handle[4][0:1, 0:1], conv_ln_b)
    dqkv_parts = _attn_backward(qkv_t, o_t, do_t, attn_lse, rel_bias[0])
    drel = dqkv_parts[3]
    g_win_t = _matmul(dproj_a, u1, mode="tn", m=2 * CONV_WIDTH, n=d, k=t_rows, tm=1024, tn=1024, tk=2048, out_dtype=BF16,
                      name="proj_dw_conv", out_rows=in_cols)
    for j, part in enumerate("qkv"):
        row0 = 2 * CONV_WIDTH + j * ATTN_WIDTH
        g_win_t = _matmul(dqkv_parts[j], u1, mode="nn", m=ATTN_WIDTH, n=d, k=t_rows, tm=512, tn=1024, tk=2048,
                          out_dtype=BF16, name="proj_dw_" + part, out_rows=in_cols, out_m0=row0, into=g_win_t)
    win_handle = _scatter_start([g_win_t.reshape(N_DEV, in_cols // N_DEV, d)],
                                [lax.empty((N_DEV, in_cols // N_DEV, d), BF16)], False, "reduce_w_in_grad_start")
    dx, d_norm_mix_pre = _matmul_rows(
        [(dproj_a, "nn", 2 * CONV_WIDTH, 0)]
        + [(dqkv_parts[j], "tn", ATTN_WIDTH, 2 * CONV_WIDTH + j * ATTN_WIDTH) for j in range(3)],
        win_t, m=t_rows, n=d, tm=1024, name="proj_dx_input_backward", row_ins=[dh1, x2],
        vec_ins=[norm_mix_pre + win_handle[4][0:1, 0:1]], row_outs=[F32], acc_outs=[(1, d)],
        epilogue=_input_backward_epilogue)

    small_grads = dict(norm_mix_pre=d_norm_mix_pre, conv_dw_b=d_conv_b, conv_ln_g=d_ln_g, conv_ln_b=d_ln_b,
                       rel_bias=drel[:, :2 * MAX_REL + 1], norm_mix_post=d_norm_mix_post, norm_ffn_pre=d_norm_ffn_pre,
                       ffn_dw_b=jnp.concatenate([dbg, dbv], axis=1), norm_ffn_post=d_norm_ffn_post)
    pieces = [small_grads[nm] for nm in SMALL] + [d_conv_w, jnp.concatenate([dwg, dwv], axis=1), loss]
    count = sum(p.size for p in pieces)
    (gathered_small,) = _all_gather([_pack(pieces, _rows_for(count))], "all_gather_small_grads")
    summed = _sum_devices(gathered_small)
    shapes = [weights[nm].shape for nm in SMALL] + [(CONV_K, CONV_WIDTH), (FFN_K, 2 * D_FF), (1, 1)]
    unpacked = _unpack(summed, shapes)
    grads = dict(zip(SMALL, unpacked[:len(SMALL)]))
    cw_shard, fw_shard = CONV_WIDTH // N_DEV, 2 * D_FF // N_DEV
    grads["conv_dw_w"] = lax.dynamic_slice_in_dim(unpacked[-3], my_dev * cw_shard, cw_shard, axis=1)[None]
    grads["ffn_dw_w"] = lax.dynamic_slice_in_dim(unpacked[-2], my_dev * fw_shard, fw_shard, axis=1)[None]
    total_loss = unpacked[-1].reshape(())

    me = jnp.reshape(my_dev, (1,)).astype(jnp.int32)
    delta, new_m, new_v = {}, {}, {}

    def finish(nm, send, recv, srcs, lands, after, transposed):
        srcs, lands = _scatter_wait(send, recv, srcs, lands, after, False, "reduce_" + nm + "_grad_wait")
        for name_a, src, land in zip(nm.split("_and_"), srcs, lands):
            flip = (lambda t: t.T) if transposed[name_a] else (lambda t: t)
            outs = _reduce_and_adamw(src, land, me, flip(weights[name_a][0]), flip(mom1[name_a][0]),
                                     flip(mom2[name_a][0]), "reduce_adamw_" + name_a)
            for store, arr in zip((grads, delta, new_m, new_v), outs):
                store[name_a] = flip(arr)[None]

    transposed = dict(w_in=True, w_out=False, w_up=True, w_down=False)
    finish("w_up_and_w_down", red_send, red_recv, ffn_grads, red_lands, dx, transposed)
    finish("w_out", *wout_handle[:4], dx, transposed)
    finish("w_in", *win_handle[:4], delta["w_up"], transposed)
    small_names = SMALL + SHARDED_SMALL
    small_count = sum(weights[nm].size for nm in small_names)
    small_rows = _rows_for(small_count)
    packed = [_pack([src[nm] for nm in small_names], small_rows) for src in (weights, grads, mom1, mom2)]
    outs = _adamw(*packed, "adamw_small")
    small_shapes = [weights[nm].shape for nm in small_names]
    for store, arr in zip((delta, new_m, new_v), outs):
        store.update(zip(small_names, _unpack(arr, small_shapes)))

    return (total_loss, dx[None], *[grads[nm] for nm in WEIGHTS], *[delta[nm] for nm in WEIGHTS],
            *[new_m[nm] for nm in WEIGHTS], *[new_v[nm] for nm in WEIGHTS])
```

```python
import jax
import jax.numpy as jnp
from jax import lax
from jax.experimental import pallas as pl
from jax.experimental.pallas import tpu as pltpu

F32 = jnp.float32
BF16 = jnp.bfloat16
MESH = pl.DeviceIdType.MESH
AXES = ("x", "y", "c")
N_DEV = 8

EPS = 1e-6
NEG_INF = -1e30
D_MODEL = 1024
CONV_WIDTH = 512
ATTN_WIDTH = 512
N_HEADS = 8
HEAD_DIM = 64
CHUNK = 64
LEFT = 8 * CHUNK
QBLK = 2 * CHUNK
WIN = LEFT + QBLK
CONV_K = 31
CONV_HALO = 32
FFN_K = 3
FFN_HALO = 8
D_FF = 2816
MAX_REL = 128
SCALE = HEAD_DIM ** -0.5
ADAM_LR, ADAM_B1, ADAM_B2, ADAM_EPS, ADAM_WD, ADAM_STEP = 0.001, 0.9, 0.999, 1e-08, 0.01, 10

V7X_VMEM_BYTES = 64 * 2**20
VMEM_LIMIT_BYTES = V7X_VMEM_BYTES - 8 * 2**20
LANES = 128


def _params(*sem):
    return pltpu.CompilerParams(dimension_semantics=sem or None, vmem_limit_bytes=VMEM_LIMIT_BYTES)


_DOT_DIMS = {"nn": (((1,), (0,)), ((), ())), "nt": (((1,), (1,)), ((), ())), "tn": (((0,), (0,)), ((), ()))}


def _matmul(a, b, *, mode, m, n, k, tm, tn, tk, out_dtype, name, a_m0=0, b_n0=0, b_k0=0, add=None,
            out_rows=None, out_m0=0, into=None):
    tm, tn, tk = min(tm, m), min(tn, n), min(tk, k)
    out_rows = m if out_rows is None else out_rows
    assert m % tm == 0 and n % tn == 0 and k % tk == 0, (name, m, n, k, tm, tn, tk)
    assert a_m0 % tm == 0 and b_n0 % tn == 0 and b_k0 % tk == 0 and out_m0 % tm == 0, name
    am, bn, bk, om = a_m0 // tm, b_n0 // tn, b_k0 // tk, out_m0 // tm
    gk = k // tk
    dims = _DOT_DIMS[mode]

    if mode == "tn":
        a_spec = pl.BlockSpec((tk, tm), lambda i, j, kk: (kk, i + am))
    else:
        a_spec = pl.BlockSpec((tm, tk), lambda i, j, kk: (i + am, kk))
    if mode == "nt":
        b_spec = pl.BlockSpec((tn, tk), lambda i, j, kk: (j + bn, kk + bk))
    else:
        b_spec = pl.BlockSpec((tk, tn), lambda i, j, kk: (kk + bk, j + bn))
    o_spec = pl.BlockSpec((tm, tn), lambda i, j, kk: (i + om, j))
    in_specs = [a_spec, b_spec]
    operands = [a, b]
    if add is not None:
        assert out_rows == m
        in_specs.append(o_spec)
        operands.append(add)
    aliases = {}
    if into is not None:
        aliases = {len(operands): 0}
        in_specs.append(pl.BlockSpec(memory_space=pl.ANY))
        operands.append(into)

    def body(*refs):
        a_ref, b_ref = refs[0], refs[1]
        add_ref = refs[2] if add is not None else None
        o_ref = refs[len(operands)]
        part = lax.dot_general(a_ref[...].astype(BF16), b_ref[...].astype(BF16), dims,
                               preferred_element_type=F32)

        def finish(total):
            if add_ref is not None:
                total = total + add_ref[...]
            o_ref[...] = total.astype(out_dtype)

        if gk == 1:
            finish(part)
        else:
            acc_ref = refs[-1]
            kk = pl.program_id(2)

            @pl.when(kk == 0)
            def _():
                acc_ref[...] = part

            @pl.when(kk > 0)
            def _():
                acc_ref[...] += part

            @pl.when(kk == gk - 1)
            def _():
                finish(acc_ref[...])

    return pl.pallas_call(
        body, name=name,
        grid=(m // tm, n // tn, gk),
        in_specs=in_specs, out_specs=o_spec,
        out_shape=jax.ShapeDtypeStruct((out_rows, n), out_dtype),
        scratch_shapes=[pltpu.VMEM((tm, tn), F32)] if gk > 1 else [],
        input_output_aliases=aliases,
        compiler_params=_params("parallel", "parallel", "arbitrary"),
    )(*operands)


def _matmul_rows(pieces, b, *, m, n, tm, name, row_ins, vec_ins, row_outs, acc_outs, epilogue):
    tm = min(tm, m)
    assert m % tm == 0
    steps = m // tm
    in_specs, operands = [], []
    for a, mode, k, k0 in pieces:
        assert k0 % k == 0
        if mode == "tn":
            in_specs.append(pl.BlockSpec((k, tm), lambda i: (0, i)))
        else:
            in_specs.append(pl.BlockSpec((tm, k), lambda i: (i, 0)))
        in_specs.append(pl.BlockSpec((k, n), lambda i, blk=k0 // k: (blk, 0)))
        operands += [a, b]
    row = pl.BlockSpec((tm, n), lambda i: (i, 0))
    in_specs += [row] * len(row_ins) + [pl.BlockSpec((1, n), lambda i: (0, 0))] * len(vec_ins)
    operands += [*row_ins, *vec_ins]
    n_in = len(operands)

    def body(*refs):
        total = None
        for p, (_, mode, _, _) in enumerate(pieces):
            part = lax.dot_general(refs[2 * p][...], refs[2 * p + 1][...], _DOT_DIMS[mode], preferred_element_type=F32)
            total = part if total is None else total + part
        first = 2 * len(pieces)
        rows = refs[first:first + len(row_ins)]
        vecs = refs[first + len(row_ins):n_in]
        outs = refs[n_in:n_in + len(row_outs)]
        accs = refs[n_in + len(row_outs):]
        epilogue(total, rows, vecs, outs, accs, pl.program_id(0), steps)

    return pl.pallas_call(
        body, name=name, grid=(steps,),
        in_specs=in_specs,
        out_specs=[row] * len(row_outs) + [pl.BlockSpec(s, lambda i: (0, 0)) for s in acc_outs],
        out_shape=[jax.ShapeDtypeStruct((m, n), dt) for dt in row_outs]
        + [jax.ShapeDtypeStruct(s, F32) for s in acc_outs],
        compiler_params=_params("arbitrary" if acc_outs else "parallel"),
    )(*operands)


def _rms_hat(v):
    r = lax.rsqrt(jnp.mean(v * v, axis=-1, keepdims=True) + EPS)
    return v * r, r


def _rms_bwd(dn, hat, r):
    return r * (dn - hat * jnp.mean(dn * hat, axis=-1, keepdims=True))


def _sigmoid(v):
    return 1.0 / (1.0 + jnp.exp(-v))


_GELU_C = 0.7978845608028654


def _gelu(v):
    return 0.5 * v * (1.0 + jnp.tanh(_GELU_C * (v + 0.044715 * v * (v * v))))


def _gelu_parts(v):
    v2 = v * v
    t = jnp.tanh(_GELU_C * (v + 0.044715 * v * v2))
    cdf = 0.5 * (1.0 + t)
    dcdf = 0.5 * (1.0 - t * t) * _GELU_C * (1.0 + 3.0 * 0.044715 * v2)
    return v * cdf, cdf + v * dcdf


def _row_tile(t_rows, want):
    tile = min(want, t_rows)
    assert t_rows % tile == 0
    return tile


def _pre_norm(x, g, name):
    t_rows, d = x.shape
    tm = _row_tile(t_rows, 512)

    def body(x_ref, g_ref, u_ref):
        hat, _ = _rms_hat(x_ref[...])
        u_ref[...] = (hat * g_ref[...]).astype(BF16)

    return pl.pallas_call(
        body, name=name, grid=(t_rows // tm,),
        in_specs=[pl.BlockSpec((tm, d), lambda i: (i, 0)), pl.BlockSpec((1, d), lambda i: (0, 0))],
        out_specs=pl.BlockSpec((tm, d), lambda i: (i, 0)),
        out_shape=jax.ShapeDtypeStruct((t_rows, d), BF16),
        compiler_params=_params("parallel"),
    )(x, g)


def _zero_at_start(accs, step):
    @pl.when(step == 0)
    def _():
        for acc in accs:
            acc[...] = jnp.zeros_like(acc)


def _mid_forward_epilogue(mixed, rows, vecs, outs, accs, step, steps):
    (x_ref,), (gpost_ref, gpre_ref), (mixed_ref, h1_ref, u2_ref) = rows, vecs, outs
    mixed_ref[...] = mixed
    hat, _ = _rms_hat(mixed)
    h1 = x_ref[...] + hat * gpost_ref[...]
    h1_ref[...] = h1
    hat1, _ = _rms_hat(h1)
    u2_ref[...] = (hat1 * gpre_ref[...]).astype(BF16)


def _loss_epilogue(f, rows, vecs, outs, accs, step, steps):
    (h1_ref, tgt_ref), (g_ref,), (dy_ref, df_ref), (sq_ref, dg_ref, loss_ref) = rows, vecs, outs, accs
    _zero_at_start(accs, step)
    g = g_ref[...]
    d = f.shape[-1]
    hat, r = _rms_hat(f)
    err = h1_ref[...] + hat * g - tgt_ref[...]
    sq_ref[...] += jnp.sum(err * err, axis=0, keepdims=True)
    dy = err * (1.0 / d)
    dy_ref[...] = dy
    dg_ref[...] += jnp.sum(dy * hat, axis=0, keepdims=True)
    df_ref[...] = _rms_bwd(dy * g, hat, r).astype(BF16)

    @pl.when(step == steps - 1)
    def _():
        loss_ref[...] = (0.5 / d) * jnp.sum(sq_ref[...], axis=1, keepdims=True)


def _mid_backward_epilogue(du2, rows, vecs, outs, accs, step, steps):
    (dy_ref, h1_ref, mixed_ref), (gpre_ref, gpost_ref), (dh1_ref, dmixed_ref), (dgpre_ref, dgpost_ref) = rows, vecs, outs, accs
    _zero_at_start(accs, step)
    hat1, r1 = _rms_hat(h1_ref[...])
    dgpre_ref[...] += jnp.sum(du2 * hat1, axis=0, keepdims=True)
    dh1 = dy_ref[...] + _rms_bwd(du2 * gpre_ref[...], hat1, r1)
    dh1_ref[...] = dh1
    hatm, rm = _rms_hat(mixed_ref[...])
    dgpost_ref[...] += jnp.sum(dh1 * hatm, axis=0, keepdims=True)
    dmixed_ref[...] = _rms_bwd(dh1 * gpost_ref[...], hatm, rm).astype(BF16)


def _input_backward_epilogue(du1, rows, vecs, outs, accs, step, steps):
    (dh1_ref, x_ref), (g_ref,), (dx_ref,), (dg_ref,) = rows, vecs, outs, accs
    _zero_at_start(accs, step)
    hat, r = _rms_hat(x_ref[...])
    dg_ref[...] += jnp.sum(du1 * hat, axis=0, keepdims=True)
    dx_ref[...] = dh1_ref[...] + _rms_bwd(du1 * g_ref[...], hat, r)


CONV_STRIP = 32


def _glu(block):
    return block[:, :CONV_WIDTH] * _sigmoid(block[:, CONV_WIDTH:])


def _layer_norm_parts(c):
    mu = jnp.mean(c, axis=-1, keepdims=True)
    xc = c - mu
    r = lax.rsqrt(jnp.mean(xc * xc, axis=-1, keepdims=True) + EPS)
    return xc * r, r


CONV_WINDOW = 2 * CONV_STRIP
SHIFTED_ROWS = CONV_WINDOW - 8


def _shifted_copies(v, shifted):
    for s in range(1, 8):
        shifted[s] = v[s:s + SHIFTED_ROWS, :]


def _window_rows(v, shifted, start):
    s, a = start % 8, start - start % 8
    return v[a:a + CONV_STRIP, :] if s == 0 else shifted[s, a:a + CONV_STRIP, :]


def _conv_forward(proj_a, w, b, ln_g, ln_b):
    t_rows = proj_a.shape[0]
    tm = _row_tile(t_rows, 512)
    hb = tm // CONV_HALO
    cw = CONV_WIDTH

    def body(cur_ref, prev_ref, w_ref, b_ref, g_ref, beta_ref, c_ref, out_ref, hbuf, shifted):
        i = pl.program_id(0)
        hbuf[0:CONV_HALO, :] = jnp.where(i > 0, _glu(prev_ref[...]), 0.0)
        hbuf[CONV_HALO:, :] = _glu(cur_ref[...])

        def strip(s, carry):
            base = pl.multiple_of(s * CONV_STRIP, CONV_STRIP)
            v = hbuf[pl.ds(base, CONV_WINDOW), :]
            _shifted_copies(v, shifted)
            acc = jnp.broadcast_to(b_ref[...], (CONV_STRIP, cw))
            off = CONV_HALO - (CONV_K - 1)
            for kk in range(CONV_K):
                acc = acc + w_ref[kk:kk + 1, :] * _window_rows(v, shifted, off + kk)
            c_ref[pl.ds(base, CONV_STRIP), :] = acc
            hat, _ = _layer_norm_parts(acc)
            z = hat * g_ref[...] + beta_ref[...]
            out_ref[pl.ds(base, CONV_STRIP), :] = (z * _sigmoid(z)).astype(BF16)
            return carry

        lax.fori_loop(0, tm // CONV_STRIP, strip, 0)

    vec = pl.BlockSpec((1, cw), lambda i: (0, 0))
    return pl.pallas_call(
        body, name="conv_forward", grid=(t_rows // tm,),
        in_specs=[pl.BlockSpec((tm, 2 * cw), lambda i: (i, 0)),
                  pl.BlockSpec((CONV_HALO, 2 * cw), lambda i: (jnp.maximum(i * hb - 1, 0), 0)),
                  pl.BlockSpec((CONV_K, cw), lambda i: (0, 0)), vec, vec, vec],
        out_specs=[pl.BlockSpec((tm, cw), lambda i: (i, 0)), pl.BlockSpec((tm, cw), lambda i: (i, 0))],
        out_shape=[jax.ShapeDtypeStruct((t_rows, cw), F32), jax.ShapeDtypeStruct((t_rows, cw), BF16)],
        scratch_shapes=[pltpu.VMEM((tm + CONV_HALO, cw), F32), pltpu.VMEM((8, SHIFTED_ROWS, cw), F32)],
        compiler_params=_params("parallel"),
    )(proj_a, proj_a, w, b, ln_g, ln_b)


def _conv_backward(dout, c, proj_a, w, ln_g, ln_b):
    t_rows = c.shape[0]
    tm = _row_tile(t_rows, 512)
    hb = tm // CONV_HALO
    nt = t_rows // tm
    last_halo = t_rows // CONV_HALO - 1
    cw = CONV_WIDTH

    def body(dout_ref, dout_next_ref, c_ref, c_next_ref, cur_ref, prev_ref, w_ref, g_ref, beta_ref,
             dproj_ref, dw_ref, db_ref, dg_ref, dbeta_ref, hbuf, dcbuf, dwacc, h_shifted, d_shifted):
        i = pl.program_id(0)

        @pl.when(i == 0)
        def _():
            dwacc[...] = jnp.zeros_like(dwacc)
            db_ref[...] = jnp.zeros_like(db_ref)
            dg_ref[...] = jnp.zeros_like(dg_ref)
            dbeta_ref[...] = jnp.zeros_like(dbeta_ref)

        def ln_swish_backward(dout_v, c_v):
            hat, r = _layer_norm_parts(c_v)
            g = g_ref[...]
            z = hat * g + beta_ref[...]
            sg = _sigmoid(z)
            dz = dout_v * (sg * (1.0 + z * (1.0 - sg)))
            dhat = dz * g
            dc = r * (dhat - jnp.mean(dhat, axis=-1, keepdims=True)
                      - hat * jnp.mean(dhat * hat, axis=-1, keepdims=True))
            return dc, dz, hat

        dc, dz, hat = ln_swish_backward(dout_ref[...], c_ref[...])
        dg_ref[...] += jnp.sum(dz * hat, axis=0, keepdims=True)
        dbeta_ref[...] += jnp.sum(dz, axis=0, keepdims=True)
        db_ref[...] += jnp.sum(dc, axis=0, keepdims=True)
        dcbuf[0:tm, :] = dc
        dc_next, _, _ = ln_swish_backward(dout_next_ref[...], c_next_ref[...])
        dcbuf[tm:, :] = jnp.where(i < nt - 1, dc_next, 0.0)

        hbuf[0:CONV_HALO, :] = jnp.where(i > 0, _glu(prev_ref[...]), 0.0)
        hbuf[CONV_HALO:, :] = _glu(cur_ref[...])

        def strip(s, carry):
            base = pl.multiple_of(s * CONV_STRIP, CONV_STRIP)
            dv = dcbuf[pl.ds(base, CONV_WINDOW), :]
            hv = hbuf[pl.ds(base, CONV_WINDOW), :]
            _shifted_copies(dv, d_shifted)
            _shifted_copies(hv, h_shifted)
            dcs = dv[0:CONV_STRIP, :]
            dh = jnp.zeros((CONV_STRIP, cw), F32)
            off = CONV_HALO - (CONV_K - 1)
            for kk in range(CONV_K):
                back = CONV_K - 1 - kk
                dh = dh + w_ref[kk:kk + 1, :] * _window_rows(dv, d_shifted, back)
                prod = dcs * _window_rows(hv, h_shifted, off + kk)
                dwacc[kk] += jnp.sum(prod.reshape(CONV_STRIP // 8, 8, cw), axis=0)
            blk = cur_ref[pl.ds(base, CONV_STRIP), :]
            val, sg = blk[:, :cw], _sigmoid(blk[:, cw:])
            dproj_ref[pl.ds(base, CONV_STRIP), 0:cw] = (dh * sg).astype(BF16)
            dproj_ref[pl.ds(base, CONV_STRIP), cw:2 * cw] = (dh * val * sg * (1.0 - sg)).astype(BF16)
            return carry

        lax.fori_loop(0, tm // CONV_STRIP, strip, 0)

        @pl.when(i == nt - 1)
        def _():
            for kk in range(CONV_K):
                dw_ref[kk:kk + 1, :] = jnp.sum(dwacc[kk], axis=0, keepdims=True)

    vec = pl.BlockSpec((1, cw), lambda i: (0, 0))
    cur = lambda width: pl.BlockSpec((tm, width), lambda i: (i, 0))
    nxt = lambda width: pl.BlockSpec((CONV_HALO, width), lambda i: (jnp.minimum((i + 1) * hb, last_halo), 0))
    return pl.pallas_call(
        body, name="conv_backward", grid=(nt,),
        in_specs=[cur(cw), nxt(cw), cur(cw), nxt(cw), cur(2 * cw),
                  pl.BlockSpec((CONV_HALO, 2 * cw), lambda i: (jnp.maximum(i * hb - 1, 0), 0)),
                  pl.BlockSpec((CONV_K, cw), lambda i: (0, 0)), vec, vec],
        out_specs=[cur(2 * cw), pl.BlockSpec((CONV_K, cw), lambda i: (0, 0)), vec, vec, vec],
        out_shape=[jax.ShapeDtypeStruct((t_rows, 2 * cw), BF16), jax.ShapeDtypeStruct((CONV_K, cw), F32),
                   jax.ShapeDtypeStruct((1, cw), F32), jax.ShapeDtypeStruct((1, cw), F32),
                   jax.ShapeDtypeStruct((1, cw), F32)],
        scratch_shapes=[pltpu.VMEM((tm + CONV_HALO, cw), F32), pltpu.VMEM((tm + CONV_HALO, cw), F32),
                        pltpu.VMEM((CONV_K, 8, cw), F32), pltpu.VMEM((8, SHIFTED_ROWS, cw), F32),
                        pltpu.VMEM((8, SHIFTED_ROWS, cw), F32)],
        compiler_params=_params("arbitrary"),
    )(dout, dout, c, c, proj_a, proj_a, w, ln_g, ln_b)


def _attn_load_kv(kv_hbm, k_pad, v_pad, sem, t_cols):
    k_pad[:, 0:LEFT] = jnp.zeros((ATTN_WIDTH, LEFT), BF16)
    v_pad[:, 0:LEFT] = jnp.zeros((ATTN_WIDTH, LEFT), BF16)
    ck = pltpu.make_async_copy(kv_hbm.at[pl.ds(ATTN_WIDTH, ATTN_WIDTH), :], k_pad.at[:, pl.ds(LEFT, t_cols)], sem.at[0])
    cv = pltpu.make_async_copy(kv_hbm.at[pl.ds(2 * ATTN_WIDTH, ATTN_WIDTH), :], v_pad.at[:, pl.ds(LEFT, t_cols)], sem.at[1])
    ck.start()
    cv.start()
    ck.wait()
    cv.wait()


def _attn_build_bias(tab_ref, bias_t):
    row = lax.broadcasted_iota(jnp.int32, (LANES, LANES), 0)
    lane = lax.broadcasted_iota(jnp.int32, (LANES, LANES), 1)
    upper = lane >= row
    lane64 = lax.broadcasted_iota(jnp.int32, (CHUNK, LANES), 1)
    for h in range(N_HEADS):
        far = jnp.broadcast_to(tab_ref[h:h + 1, 2 * MAX_REL:2 * MAX_REL + 1], (LANES, LANES))
        hi = jnp.broadcast_to(tab_ref[h:h + 1, MAX_REL:2 * MAX_REL], (LANES, LANES))
        lo = jnp.broadcast_to(tab_ref[h:h + 1, 0:MAX_REL], (LANES, LANES))
        hi_d = pltpu.roll(hi, 0, 1, stride=1, stride_axis=0)
        lo_d = pltpu.roll(lo, 0, 1, stride=1, stride_axis=0)
        bias_t[h, 0:WIN - 2 * LANES, :] = jnp.broadcast_to(far[0:1, :], (WIN - 2 * LANES, LANES))
        bias_t[h, WIN - 2 * LANES:WIN - LANES, :] = jnp.where(upper, far, hi_d)
        bias_t[h, WIN - LANES:WIN, :] = jnp.where(upper, hi_d, lo_d)
        bias_t[h, 0:CHUNK, :] = jnp.where(lane64 < CHUNK, bias_t[h, 0:CHUNK, :], NEG_INF)
        bias_t[h, WIN - CHUNK:WIN, :] = jnp.where(lane64 >= CHUNK, bias_t[h, WIN - CHUNK:WIN, :], NEG_INF)


def _head_rows(h):
    return slice(h * HEAD_DIM, (h + 1) * HEAD_DIM)


def _attn_scores(k_pad, q_ref, s_buf, w0):
    for h in range(N_HEADS):
        q_h = q_ref[_head_rows(h), :] * jnp.asarray(SCALE, BF16)
        s_buf[h] = lax.dot_general(k_pad[_head_rows(h), pl.ds(w0, WIN)], q_h, _DOT_DIMS["tn"],
                                   preferred_element_type=F32)


def _attn_logits(s, bias, first_valid, key0=0):
    s = s + bias
    if first_valid is not None:
        s = jnp.where(lax.broadcasted_iota(jnp.int32, s.shape, 0) + key0 >= first_valid, s, NEG_INF)
    return s


def _attn_probs(s, bias_h, first_valid):
    s = _attn_logits(s, bias_h, first_valid)
    top = jnp.max(s, axis=0, keepdims=True)
    e = jnp.exp(s - top)
    total = jnp.sum(e, axis=0, keepdims=True)
    return e * (1.0 / total), top + jnp.log(total)


def _attn_by_padding(m, fn):
    @pl.when(m < LEFT // QBLK)
    def _():
        fn(LEFT - m * QBLK)

    @pl.when(m >= LEFT // QBLK)
    def _():
        fn(None)


def _attn_forward(qkv_t, rel_bias):
    t_cols = qkv_t.shape[1]
    steps = t_cols // QBLK

    def body(q_ref, kv_hbm, tab_ref, o_ref, lse_ref, k_pad, v_pad, bias_t, s_buf, p_buf, sem):
        m = pl.program_id(0)

        @pl.when(m == 0)
        def _():
            _attn_build_bias(tab_ref, bias_t)
            _attn_load_kv(kv_hbm, k_pad, v_pad, sem, t_cols)

        w0 = pl.multiple_of(m * QBLK, QBLK)
        _attn_scores(k_pad, q_ref, s_buf, w0)

        def softmax(first_valid):
            for h in range(N_HEADS):
                p, lse = _attn_probs(s_buf[h], bias_t[h], first_valid)
                p_buf[h] = p.astype(BF16)
                lse_ref[h:h + 1, :] = lse

        _attn_by_padding(m, softmax)
        for h in range(N_HEADS):
            o_h = lax.dot_general(v_pad[_head_rows(h), pl.ds(w0, WIN)], p_buf[h], _DOT_DIMS["nn"],
                                  preferred_element_type=F32)
            o_ref[_head_rows(h), :] = o_h.astype(BF16)

    return pl.pallas_call(
        body, name="attn_forward", grid=(steps,),
        in_specs=[pl.BlockSpec((ATTN_WIDTH, QBLK), lambda m: (0, m)),
                  pl.BlockSpec(memory_space=pl.ANY),
                  pl.BlockSpec((N_HEADS, 2 * MAX_REL + 1), lambda m: (0, 0))],
        out_specs=[pl.BlockSpec((ATTN_WIDTH, QBLK), lambda m: (0, m)), pl.BlockSpec((N_HEADS, QBLK), lambda m: (0, m))],
        out_shape=[jax.ShapeDtypeStruct((ATTN_WIDTH, t_cols), BF16), jax.ShapeDtypeStruct((N_HEADS, t_cols), F32)],
        scratch_shapes=[pltpu.VMEM((ATTN_WIDTH, LEFT + t_cols), BF16), pltpu.VMEM((ATTN_WIDTH, LEFT + t_cols), BF16),
                        pltpu.VMEM((N_HEADS, WIN, QBLK), F32), pltpu.VMEM((N_HEADS, WIN, QBLK), F32),
                        pltpu.VMEM((N_HEADS, WIN, QBLK), BF16), pltpu.SemaphoreType.DMA((2,))],
        compiler_params=_params("arbitrary"),
    )(qkv_t, qkv_t, rel_bias)


def _reverse_lanes(v, flip):
    out = jnp.zeros(v.shape, F32)
    rest = v
    for _ in range(3):
        piece = rest.astype(BF16)
        out = out + lax.dot_general(piece, flip, _DOT_DIMS["nn"], preferred_element_type=F32)
        rest = rest - piece.astype(F32)
    return out


def _attn_bias_grad(dbias_t, drel_ref):
    row = lax.broadcasted_iota(jnp.int32, (LANES, LANES), 0)
    lane = lax.broadcasted_iota(jnp.int32, (LANES, LANES), 1)
    flip = (row + lane == LANES - 1).astype(BF16)
    head = lax.broadcasted_iota(jnp.int32, (N_HEADS, LANES), 0)
    lane8 = lax.broadcasted_iota(jnp.int32, (N_HEADS, LANES), 1)
    upper_rev = jnp.zeros((N_HEADS, LANES), F32)
    lower_rev = jnp.zeros((N_HEADS, LANES), F32)
    far = jnp.zeros((N_HEADS, LANES), F32)
    for h in range(N_HEADS):
        def diagonals(block):
            skew = pltpu.roll(_reverse_lanes(block, flip), 0, 1, stride=1, stride_axis=0)
            pos = jnp.sum(jnp.where(lane >= row, skew, 0.0), axis=0, keepdims=True)
            neg = jnp.sum(jnp.where(lane < row, skew, 0.0), axis=0, keepdims=True)
            return pos, neg

        pos4, neg4 = diagonals(dbias_t[h, WIN - LANES:WIN, :])
        pos3, neg3 = diagonals(dbias_t[h, WIN - 2 * LANES:WIN - LANES, :])
        far_h = jnp.sum(jnp.sum(dbias_t[h, 0:WIN - 2 * LANES, :], axis=0, keepdims=True), axis=1, keepdims=True)
        far_h = far_h + jnp.sum(pos3, axis=1, keepdims=True)
        upper_rev = jnp.where(head == h, pos4 + neg3, upper_rev)
        lower_rev = jnp.where(head == h, neg4, lower_rev)
        far = jnp.where((head == h) & (lane8 == 0), far_h, far)
    drel_ref[:, 0:LANES] = _reverse_lanes(lower_rev, flip)
    drel_ref[:, LANES:2 * LANES] = _reverse_lanes(upper_rev, flip)
    drel_ref[:, 2 * LANES:3 * LANES] = far


def _attn_backward(qkv_t, o_t, do_t, lse, rel_bias):
    t_cols = qkv_t.shape[1]
    steps = t_cols // QBLK
    flush = LEFT // QBLK
    total = steps + flush

    def body(q_ref, o_ref, do_ref, lse_ref, kv_hbm, tab_ref, dq_ref, dk_ref, dv_ref, drel_ref,
             k_pad, v_pad, bias_t, dbias_t, dk_acc, dv_acc, s_buf, dp_buf, p_buf, ds_buf, sem):
        m = pl.program_id(0)

        @pl.when(m == 0)
        def _():
            _attn_build_bias(tab_ref, bias_t)
            _attn_load_kv(kv_hbm, k_pad, v_pad, sem, t_cols)
            dbias_t[...] = jnp.zeros_like(dbias_t)
            dk_acc[...] = jnp.zeros_like(dk_acc)
            dv_acc[...] = jnp.zeros_like(dv_acc)

        @pl.when(m < steps)
        def _():
            w0 = pl.multiple_of(m * QBLK, QBLK)
            _attn_scores(k_pad, q_ref, s_buf, w0)
            for h in range(N_HEADS):
                dp_buf[h] = lax.dot_general(v_pad[_head_rows(h), pl.ds(w0, WIN)], do_ref[_head_rows(h), :],
                                            _DOT_DIMS["tn"], preferred_element_type=F32)

            def softmax_backward(first_valid):
                for h in range(N_HEADS):
                    rows = _head_rows(h)
                    delta = jnp.sum(do_ref[rows, :].astype(F32) * o_ref[rows, :].astype(F32), axis=0, keepdims=True)
                    lse_h = lse_ref[h:h + 1, :]
                    for b in range(WIN // LANES):
                        keys = slice(b * LANES, (b + 1) * LANES)
                        s = _attn_logits(s_buf[h, keys, :], bias_t[h, keys, :], first_valid, b * LANES)
                        p = jnp.exp(s - lse_h)
                        ds = p * (dp_buf[h, keys, :] - delta)
                        dbias_t[h, keys, :] += ds
                        p_buf[h, keys, :] = p.astype(BF16)
                        ds_buf[h, keys, :] = (ds * SCALE).astype(BF16)

            _attn_by_padding(m, softmax_backward)
            for h in range(N_HEADS):
                rows = _head_rows(h)
                dq_h = lax.dot_general(k_pad[rows, pl.ds(w0, WIN)], ds_buf[h], _DOT_DIMS["nn"], preferred_element_type=F32)
                dq_ref[rows, :] = dq_h.astype(BF16)
                dk_h = lax.dot_general(q_ref[rows, :], ds_buf[h], _DOT_DIMS["nt"], preferred_element_type=F32)
                dv_h = lax.dot_general(do_ref[rows, :], p_buf[h], _DOT_DIMS["nt"], preferred_element_type=F32)
                for b in range(WIN // QBLK):
                    slot = pl.multiple_of(lax.rem(m + b, WIN // QBLK) * QBLK, QBLK)
                    dk_acc[rows, pl.ds(slot, QBLK)] += dk_h[:, b * QBLK:(b + 1) * QBLK]
                    dv_acc[rows, pl.ds(slot, QBLK)] += dv_h[:, b * QBLK:(b + 1) * QBLK]

        oldest = pl.multiple_of(lax.rem(m, WIN // QBLK) * QBLK, QBLK)
        dk_ref[...] = dk_acc[:, pl.ds(oldest, QBLK)].astype(BF16)
        dv_ref[...] = dv_acc[:, pl.ds(oldest, QBLK)].astype(BF16)
        dk_acc[:, pl.ds(oldest, QBLK)] = jnp.zeros((ATTN_WIDTH, QBLK), F32)
        dv_acc[:, pl.ds(oldest, QBLK)] = jnp.zeros((ATTN_WIDTH, QBLK), F32)

        @pl.when(m == total - 1)
        def _():
            _attn_bias_grad(dbias_t, drel_ref)

    qblk = pl.BlockSpec((ATTN_WIDTH, QBLK), lambda m: (0, jnp.minimum(m, steps - 1)))
    kblk = pl.BlockSpec((ATTN_WIDTH, QBLK), lambda m: (0, jnp.maximum(m - flush, 0)))
    dq, dk, dv, drel = pl.pallas_call(
        body, name="attn_backward", grid=(total,),
        in_specs=[qblk, qblk, qblk, pl.BlockSpec((N_HEADS, QBLK), lambda m: (0, jnp.minimum(m, steps - 1))),
                  pl.BlockSpec(memory_space=pl.ANY), pl.BlockSpec((N_HEADS, 2 * MAX_REL + 1), lambda m: (0, 0))],
        out_specs=[qblk, kblk, kblk, pl.BlockSpec((N_HEADS, 3 * LANES), lambda m: (0, 0))],
        out_shape=[jax.ShapeDtypeStruct((ATTN_WIDTH, t_cols), BF16)] * 3
        + [jax.ShapeDtypeStruct((N_HEADS, 3 * LANES), F32)],
        scratch_shapes=[pltpu.VMEM((ATTN_WIDTH, LEFT + t_cols), BF16), pltpu.VMEM((ATTN_WIDTH, LEFT + t_cols), BF16),
                        pltpu.VMEM((N_HEADS, WIN, QBLK), F32), pltpu.VMEM((N_HEADS, WIN, QBLK), F32),
                        pltpu.VMEM((ATTN_WIDTH, WIN), F32), pltpu.VMEM((ATTN_WIDTH, WIN), F32),
                        pltpu.VMEM((N_HEADS, WIN, QBLK), F32), pltpu.VMEM((N_HEADS, WIN, QBLK), F32),
                        pltpu.VMEM((N_HEADS, WIN, QBLK), BF16), pltpu.VMEM((N_HEADS, WIN, QBLK), BF16),
                        pltpu.SemaphoreType.DMA((2,))],
        compiler_params=_params("arbitrary"),
    )(qkv_t, o_t, do_t, lse, qkv_t, rel_bias)
    return dq, dk, dv, drel


FFN_TC = D_FF // 2
FFN_TR = 512


def _ffn_specs(t_rows, tr):
    nj = D_FF // FFN_TC
    hb = tr // FFN_HALO
    last_halo = t_rows // FFN_HALO - 1
    cur = lambda off: pl.BlockSpec((tr, FFN_TC), lambda j, i: (i, j + off))
    prev = lambda off: pl.BlockSpec((FFN_HALO, FFN_TC), lambda j, i: (jnp.maximum(i * hb - 1, 0), j + off))
    nxt = lambda off: pl.BlockSpec((FFN_HALO, FFN_TC), lambda j, i: (jnp.minimum((i + 1) * hb, last_halo), j + off))
    wspec = lambda off: pl.BlockSpec((FFN_K, FFN_TC), lambda j, i: (0, j + off))
    bspec = lambda off: pl.BlockSpec((1, FFN_TC), lambda j, i: (0, j + off))
    return nj, cur, prev, nxt, wspec, bspec


FFN_STRIP = 16


def _ffn_conv(win, w, b, rows):
    out = b + w[2] * win[FFN_HALO:FFN_HALO + rows, :]
    out = out + w[1] * win[FFN_HALO - 1:FFN_HALO - 1 + rows, :]
    return out + w[0] * win[FFN_HALO - 2:FFN_HALO - 2 + rows, :]


def _taps(w_ref):
    return [w_ref[kk:kk + 1, :] for kk in range(FFN_K)]


def _ffn_first_window(prev_ref, cur_ref, tile, rows):
    return jnp.concatenate([jnp.where(tile > 0, prev_ref[...], 0.0), cur_ref[0:rows, :]], axis=0)


def _fold8(v):
    return jnp.sum(v.reshape(v.shape[0] // 8, 8, v.shape[1]), axis=0)


def _ffn_activation(hup, w, b):
    t_rows = hup.shape[0]
    tr = _row_tile(t_rows, FFN_TR)
    nj, cur, prev, nxt, wspec, bspec = _ffn_specs(t_rows, tr)
    rs = FFN_STRIP

    def body(g_ref, gprev_ref, v_ref, vprev_ref, wg_ref, wv_ref, bg_ref, bv_ref, act_ref, gel_ref, slope_ref):
        i = pl.program_id(1)
        wg, wv, bg, bv = _taps(wg_ref), _taps(wv_ref), bg_ref[...], bv_ref[...]

        def emit(base, g_win, v_win):
            gel, dgel = _gelu_parts(_ffn_conv(g_win, wg, bg, rs))
            cv = _ffn_conv(v_win, wv, bv, rs)
            act_ref[pl.ds(base, rs), :] = (gel * cv).astype(BF16)
            gel_ref[pl.ds(base, rs), :] = gel
            slope_ref[pl.ds(base, rs), :] = cv * dgel

        def strip(s, carry):
            base = pl.multiple_of(s * rs, rs)
            emit(base, g_ref[pl.ds(base - FFN_HALO, rs + FFN_HALO), :], v_ref[pl.ds(base - FFN_HALO, rs + FFN_HALO), :])
            return carry

        emit(0, _ffn_first_window(gprev_ref, g_ref, i, rs), _ffn_first_window(vprev_ref, v_ref, i, rs))
        lax.fori_loop(1, tr // rs, strip, 0)

    return pl.pallas_call(
        body, name="ffn_activation", grid=(nj, t_rows // tr),
        in_specs=[cur(0), prev(0), cur(nj), prev(nj), wspec(0), wspec(nj), bspec(0), bspec(nj)],
        out_specs=[cur(0), cur(0), cur(0)],
        out_shape=[jax.ShapeDtypeStruct((t_rows, D_FF), BF16), jax.ShapeDtypeStruct((t_rows, D_FF), F32),
                   jax.ShapeDtypeStruct((t_rows, D_FF), F32)],
        compiler_params=_params("parallel", "parallel"),
    )(hup, hup, hup, hup, w, w, b, b)


def _ffn_backward(dact, gel, slope, hup, w):
    t_rows = hup.shape[0]
    tr = _row_tile(t_rows, FFN_TR)
    nj, cur, prev, nxt, wspec, bspec = _ffn_specs(t_rows, tr)
    ni = t_rows // tr
    rs = FFN_STRIP
    ns = tr // rs

    def body(da_ref, danext_ref, gel_ref, gelnext_ref, slope_ref, slopenext_ref, g_ref, gprev_ref, v_ref, vprev_ref,
             wg_ref, wv_ref, dhg_ref, dhv_ref, dwg_ref, dwv_ref, dbg_ref, dbv_ref, sums):
        i = pl.program_id(1)

        @pl.when(i == 0)
        def _():
            sums[...] = jnp.zeros_like(sums)

        wg, wv = _taps(wg_ref), _taps(wv_ref)
        da_after = jnp.where(i < ni - 1, danext_ref[...], 0.0)

        def strip_at(base, g_win, v_win, carry):
            da = da_ref[pl.ds(base, rs), :]
            dcg, dcv = da * slope_ref[pl.ds(base, rs), :], da * gel_ref[pl.ds(base, rs), :]
            out = []
            for half_i, (dc, after, taps, win, dh_ref) in enumerate(((dcg, carry[0], wg, g_win, dhg_ref),
                                                                   (dcv, carry[1], wv, v_win, dhv_ref))):
                ext = jnp.concatenate([dc, after], axis=0)
                dh = taps[2] * dc + taps[1] * ext[1:1 + rs, :] + taps[0] * ext[2:2 + rs, :]
                dh_ref[pl.ds(base, rs), :] = dh.astype(BF16)
                sums[4 * half_i] += _fold8(dc)
                for kk in range(FFN_K):
                    sums[4 * half_i + 1 + kk] += _fold8(dc * win[FFN_HALO - 2 + kk:FFN_HALO - 2 + kk + rs, :])
                out.append(dc[0:FFN_HALO, :])
            return tuple(out)

        def strip(s, carry):
            base = pl.multiple_of((ns - 1 - s) * rs, rs)
            return strip_at(base, g_ref[pl.ds(base - FFN_HALO, rs + FFN_HALO), :],
                            v_ref[pl.ds(base - FFN_HALO, rs + FFN_HALO), :], carry)

        carry = lax.fori_loop(0, ns - 1, strip, (da_after * slopenext_ref[...], da_after * gelnext_ref[...]))
        strip_at(0, _ffn_first_window(gprev_ref, g_ref, i, rs), _ffn_first_window(vprev_ref, v_ref, i, rs), carry)

        @pl.when(i == ni - 1)
        def _():
            for half_i, (db_ref, dw_ref) in enumerate(((dbg_ref, dwg_ref), (dbv_ref, dwv_ref))):
                db_ref[...] = jnp.sum(sums[4 * half_i], axis=0, keepdims=True)
                for kk in range(FFN_K):
                    dw_ref[kk:kk + 1, :] = jnp.sum(sums[4 * half_i + 1 + kk], axis=0, keepdims=True)

    half = jax.ShapeDtypeStruct((t_rows, D_FF), BF16)
    return pl.pallas_call(
        body, name="ffn_backward", grid=(nj, ni),
        in_specs=[cur(0), nxt(0), cur(0), nxt(0), cur(0), nxt(0), cur(0), prev(0), cur(nj), prev(nj),
                  wspec(0), wspec(nj)],
        out_specs=[cur(0), cur(0), wspec(0), wspec(0), bspec(0), bspec(0)],
        out_shape=[half, half, jax.ShapeDtypeStruct((FFN_K, D_FF), F32), jax.ShapeDtypeStruct((FFN_K, D_FF), F32),
                   jax.ShapeDtypeStruct((1, D_FF), F32), jax.ShapeDtypeStruct((1, D_FF), F32)],
        scratch_shapes=[pltpu.VMEM((2 * (1 + FFN_K), 8, FFN_TC), F32)],
        compiler_params=_params("parallel", "arbitrary"),
    )(dact, dact, gel, gel, slope, slope, hup, hup, hup, hup, w, w)


def _mesh_position():
    return lax.axis_index("x"), lax.axis_index("y"), lax.axis_index("c")


def _hbm_specs(n):
    return [pl.BlockSpec(memory_space=pl.ANY)] * n


def _all_gather(shards, name, placed=None):
    n = len(shards)

    def body(*refs):
        ins = refs[:n]
        outs = refs[2 * n:3 * n] if placed else refs[n:2 * n]
        send_sems, recv_sems, local_sems = refs[-3:]
        x, y, c = _mesh_position()
        me, sibling = (x, y, c), (x, y, 1 - c)
        chips = [(1 - x, y), (x, 1 - y), (1 - x, 1 - y)]

        def copy(a, slot, block, to, src=None):
            dst = outs[a].at[4 * block[0] + 2 * block[1] + block[2]]
            return pltpu.make_async_remote_copy(
                src_ref=dst if src is None else src, dst_ref=dst,
                send_sem=send_sems.at[a, slot], recv_sem=recv_sems.at[a, slot],
                device_id=to, device_id_type=MESH)

        started = []
        for a in range(0 if placed else n):
            mine = pltpu.make_async_copy(ins[a], outs[a].at[4 * x + 2 * y + c], local_sems.at[a])
            mine.start()
            started.append(mine)
        first = []
        for a in range(n):
            first.append(copy(a, 0, me, sibling, src=ins[a]))
            first += [copy(a, 1 + j, me, (*chip, c), src=ins[a]) for j, chip in enumerate(chips)]
        for cp in first:
            cp.start()
        passed = []
        for j, chip in enumerate(chips):
            for a in range(n):
                copy(a, 1 + j, (*chip, c), me).wait_recv()
                fwd = copy(a, 4 + j, (*chip, c), sibling)
                fwd.start()
                passed.append(fwd)
        for a in range(n):
            copy(a, 0, sibling, me).wait_recv()
            for j, chip in enumerate(chips):
                copy(a, 4 + j, (*chip, 1 - c), me).wait_recv()
        for cp in first + passed:
            cp.wait_send()
        for mine in started:
            mine.wait()

    operands = [*shards, *placed] if placed else list(shards)
    return pl.pallas_call(
        body, name=name,
        in_specs=_hbm_specs(len(operands)), out_specs=_hbm_specs(n),
        out_shape=[jax.ShapeDtypeStruct((N_DEV,) + s.shape, s.dtype) for s in shards],
        scratch_shapes=[pltpu.SemaphoreType.DMA((n, 7)), pltpu.SemaphoreType.DMA((n, 7)),
                        pltpu.SemaphoreType.DMA((n,))],
        input_output_aliases={n + a: a for a in range(n)} if placed else {},
        compiler_params=pltpu.CompilerParams(has_side_effects=True),
    )(*operands)


def _place_own(shards, me):
    n = len(shards)

    def body(me_ref, *refs):
        for src, dst in zip(refs[:n], refs[n:]):
            dst[0] = src[...]

    return pl.pallas_call(
        body, name="place_own_shards",
        grid_spec=pltpu.PrefetchScalarGridSpec(
            num_scalar_prefetch=1, grid=(1,),
            in_specs=[pl.BlockSpec(s.shape, lambda i, me_ref: (0, 0)) for s in shards],
            out_specs=[pl.BlockSpec((1,) + s.shape, lambda i, me_ref: (me_ref[0], 0, 0)) for s in shards]),
        out_shape=[jax.ShapeDtypeStruct((N_DEV,) + s.shape, s.dtype) for s in shards],
        compiler_params=_params("arbitrary"),
    )(me, *shards)


_FLIPS = [(dx, dy, dc) for dx in (0, 1) for dy in (0, 1) for dc in (0, 1)][1:]
_HBM = pl.BlockSpec(memory_space=pltpu.HBM)
_SEM = pl.BlockSpec(memory_space=pltpu.SEMAPHORE)
_DATAFLOW = pltpu.SideEffectType.DATAFLOW_SIDE_EFFECTING


def _scatter_copies(src_refs, land_refs, send_sems, recv_sems, gather):
    x, y, c = _mesh_position()
    me = 4 * x + 2 * y + c
    copies = []
    for a, (src, land) in enumerate(zip(src_refs, land_refs)):
        for k, (dx, dy, dc) in enumerate(_FLIPS):
            px, py, pc = (x + dx) % 2, (y + dy) % 2, (c + dc) % 2
            pair = a * len(_FLIPS) + k
            copies.append(pltpu.make_async_remote_copy(
                src_ref=src if gather else src.at[4 * px + 2 * py + pc], dst_ref=land.at[me],
                send_sem=send_sems[pair], recv_sem=recv_sems[pair],
                device_id=(px, py, pc), device_id_type=MESH))
    return copies


def _scatter_start(srcs, lands, gather, name, after=None):
    n = len(srcs)
    pairs = n * len(_FLIPS)
    extra = [] if after is None else [after]

    def body(*refs):
        src_refs, land_refs = refs[:n], refs[n:2 * n]
        first = 2 * n + len(extra)
        send_sems, recv_sems = refs[first:first + pairs], refs[first + pairs:first + 2 * pairs]
        token = refs[-1]
        for cp in _scatter_copies(src_refs, land_refs, send_sems, recv_sems, gather):
            cp.start()
        token[...] = jnp.zeros_like(token)

    arrays = [*srcs, *lands]
    sem = pltpu.SemaphoreType.DMA(())
    out = pl.pallas_call(
        body, name=name,
        out_shape=(*[sem] * (2 * pairs), *[pltpu.HBM(v.shape, v.dtype) for v in arrays],
                   jax.ShapeDtypeStruct((8, LANES), F32)),
        in_specs=[*[_HBM] * (2 * n), *[pl.BlockSpec(memory_space=pl.ANY)] * len(extra)],
        out_specs=(*[_SEM] * (2 * pairs), *[_HBM] * (2 * n), pl.BlockSpec(memory_space=pltpu.VMEM)),
        input_output_aliases={i: 2 * pairs + i for i in range(2 * n)},
        compiler_params=pltpu.CompilerParams(has_side_effects=_DATAFLOW),
    )(*[pltpu.with_memory_space_constraint(v, pltpu.HBM) for v in arrays], *extra)
    sems, rest = out[:2 * pairs], out[2 * pairs:]
    return list(sems[:pairs]), list(sems[pairs:]), list(rest[:n]), list(rest[n:2 * n]), rest[-1]


def _scatter_wait(send_sems, recv_sems, srcs, lands, after, gather, name):
    n = len(srcs)
    pairs = n * len(_FLIPS)

    def body(*refs):
        src_refs, land_refs = refs[:n], refs[n:2 * n]
        send_refs, recv_refs = refs[2 * n:2 * n + pairs], refs[2 * n + pairs:2 * n + 2 * pairs]
        for cp in _scatter_copies(src_refs, land_refs, send_refs, recv_refs, gather):
            cp.wait_send()
            cp.wait_recv()

    arrays = [*srcs, *lands]
    out = pl.pallas_call(
        body, name=name,
        out_shape=tuple(pltpu.HBM(v.shape, v.dtype) for v in arrays),
        in_specs=[*[_HBM] * (2 * n), *[_SEM] * (2 * pairs), pl.BlockSpec(memory_space=pl.ANY)],
        out_specs=tuple([_HBM] * (2 * n)),
        input_output_aliases={i: i for i in range(2 * n)},
        compiler_params=pltpu.CompilerParams(has_side_effects=_DATAFLOW),
    )(*arrays, *send_sems, *recv_sems, after)
    return list(out[:n]), list(out[n:])


def _sum_devices(gathered):
    _, rows, cols = gathered.shape

    def body(g_ref, o_ref):
        total = g_ref[0]
        for d in range(1, N_DEV):
            total = total + g_ref[d]
        o_ref[...] = total

    return pl.pallas_call(
        body, name="sum_small_grads",
        out_shape=jax.ShapeDtypeStruct((rows, cols), F32),
        compiler_params=_params(),
    )(gathered)


def _adamw(w, g, m, v, name):
    rows, cols = w.shape
    tr = rows
    for cand in (256, 128, 64, 32, 16, 8):
        if rows > cand and rows % cand == 0:
            tr = cand
            break

    def body(w_ref, g_ref, m_ref, v_ref, delta_ref, newm_ref, newv_ref):
        g_v = g_ref[...]
        new_m = ADAM_B1 * m_ref[...] + (1.0 - ADAM_B1) * g_v
        new_v = ADAM_B2 * v_ref[...] + (1.0 - ADAM_B2) * (g_v * g_v)
        m_hat = new_m / (1.0 - ADAM_B1 ** ADAM_STEP)
        v_hat = new_v / (1.0 - ADAM_B2 ** ADAM_STEP)
        delta_ref[...] = -ADAM_LR * (m_hat / (jnp.sqrt(v_hat) + ADAM_EPS) + ADAM_WD * w_ref[...])
        newm_ref[...] = new_m
        newv_ref[...] = new_v

    blk = pl.BlockSpec((tr, cols), lambda i: (i, 0))
    shape = jax.ShapeDtypeStruct((rows, cols), F32)
    return pl.pallas_call(
        body, name=name, grid=(rows // tr,),
        in_specs=[blk] * 4, out_specs=[blk] * 3, out_shape=[shape] * 3,
        compiler_params=_params("parallel"),
    )(w, g, m, v)


def _adamw_update(w, g, m, v):
    new_m = ADAM_B1 * m + (1.0 - ADAM_B1) * g
    new_v = ADAM_B2 * v + (1.0 - ADAM_B2) * (g * g)
    m_hat = new_m / (1.0 - ADAM_B1 ** ADAM_STEP)
    v_hat = new_v / (1.0 - ADAM_B2 ** ADAM_STEP)
    return -ADAM_LR * (m_hat / (jnp.sqrt(v_hat) + ADAM_EPS) + ADAM_WD * w), new_m, new_v


def _reduce_and_adamw(grad, received, me, w, m, v, name):
    _, rows, cols = grad.shape
    tr = rows // 2
    assert rows % 16 == 0

    def body(me_ref, g_ref, r_ref, w_ref, m_ref, v_ref, grad_ref, delta_ref, newm_ref, newv_ref, acc):
        p = pl.program_id(1)
        term = jnp.where(p == me_ref[0], g_ref[0], r_ref[0]).astype(F32)

        @pl.when(p == 0)
        def _():
            acc[...] = term

        @pl.when(p > 0)
        def _():
            acc[...] += term

        @pl.when(p == N_DEV - 1)
        def _():
            g = acc[...]
            grad_ref[...] = g
            delta_ref[...], newm_ref[...], newv_ref[...] = _adamw_update(w_ref[...], g, m_ref[...], v_ref[...])

    blk = (1, tr, cols)
    tile = pl.BlockSpec((tr, cols), lambda j, p, me_ref: (j, 0))
    shape = jax.ShapeDtypeStruct((rows, cols), F32)
    return pl.pallas_call(
        body, name=name,
        grid_spec=pltpu.PrefetchScalarGridSpec(
            num_scalar_prefetch=1, grid=(rows // tr, N_DEV),
            in_specs=[pl.BlockSpec(blk, lambda j, p, me_ref: (me_ref[0], j, 0)),
                      pl.BlockSpec(blk, lambda j, p, me_ref: (p, j, 0)), tile, tile, tile],
            out_specs=[tile] * 4,
            scratch_shapes=[pltpu.VMEM((tr, cols), F32)]),
        out_shape=[shape] * 4,
        compiler_params=_params("parallel", "arbitrary"),
    )(me, grad, received, w, m, v)


def _pack(pieces, rows):
    flat = jnp.concatenate([p.reshape(-1) for p in pieces])
    return jnp.pad(flat, (0, rows * LANES - flat.shape[0])).reshape(rows, LANES)


def _unpack(packed, shapes):
    flat = packed.reshape(-1)
    out, pos = [], 0
    for shape in shapes:
        size = 1
        for s in shape:
            size *= s
        out.append(flat[pos:pos + size].reshape(shape))
        pos += size
    return out


def _rows_for(count):
    return -(-count // (8 * LANES)) * 8


SMALL = ("norm_mix_pre", "conv_dw_b", "conv_ln_g", "conv_ln_b", "rel_bias", "norm_mix_post", "norm_ffn_pre",
         "ffn_dw_b", "norm_ffn_post")
SHARDED_SMALL = ("conv_dw_w", "ffn_dw_w")
LARGE = ("w_in", "w_out", "w_up", "w_down")
WEIGHTS = ("norm_mix_pre", "w_in", "conv_dw_w", "conv_dw_b", "conv_ln_g", "conv_ln_b", "rel_bias", "w_out",
           "norm_mix_post", "norm_ffn_pre", "w_up", "ffn_dw_w", "ffn_dw_b", "w_down", "norm_ffn_post")


def kernel(x, norm_mix_pre, w_in, conv_dw_w, conv_dw_b, conv_ln_g, conv_ln_b, rel_bias, w_out, norm_mix_post, norm_ffn_pre, w_up, ffn_dw_w, ffn_dw_b, w_down, norm_ffn_post, loss_target, m_norm_mix_pre, m_w_in, m_conv_dw_w, m_conv_dw_b, m_conv_ln_g, m_conv_ln_b, m_rel_bias, m_w_out, m_norm_mix_post, m_norm_ffn_pre, m_w_up, m_ffn_dw_w, m_ffn_dw_b, m_w_down, m_norm_ffn_post, v_norm_mix_pre, v_w_in, v_conv_dw_w, v_conv_dw_b, v_conv_ln_g, v_conv_ln_b, v_rel_bias, v_w_out, v_norm_mix_post, v_norm_ffn_pre, v_w_up, v_ffn_dw_w, v_ffn_dw_b, v_w_down, v_norm_ffn_post):
    weights = dict(norm_mix_pre=norm_mix_pre, w_in=w_in, conv_dw_w=conv_dw_w, conv_dw_b=conv_dw_b, conv_ln_g=conv_ln_g,
                   conv_ln_b=conv_ln_b, rel_bias=rel_bias, w_out=w_out, norm_mix_post=norm_mix_post,
                   norm_ffn_pre=norm_ffn_pre, w_up=w_up, ffn_dw_w=ffn_dw_w, ffn_dw_b=ffn_dw_b, w_down=w_down,
                   norm_ffn_post=norm_ffn_post)
    mom1 = dict(norm_mix_pre=m_norm_mix_pre, w_in=m_w_in, conv_dw_w=m_conv_dw_w, conv_dw_b=m_conv_dw_b,
                conv_ln_g=m_conv_ln_g, conv_ln_b=m_conv_ln_b, rel_bias=m_rel_bias, w_out=m_w_out,
                norm_mix_post=m_norm_mix_post, norm_ffn_pre=m_norm_ffn_pre, w_up=m_w_up, ffn_dw_w=m_ffn_dw_w,
                ffn_dw_b=m_ffn_dw_b, w_down=m_w_down, norm_ffn_post=m_norm_ffn_post)
    mom2 = dict(norm_mix_pre=v_norm_mix_pre, w_in=v_w_in, conv_dw_w=v_conv_dw_w, conv_dw_b=v_conv_dw_b,
                conv_ln_g=v_conv_ln_g, conv_ln_b=v_conv_ln_b, rel_bias=v_rel_bias, w_out=v_w_out,
                norm_mix_post=v_norm_mix_post, norm_ffn_pre=v_norm_ffn_pre, w_up=v_w_up, ffn_dw_w=v_ffn_dw_w,
                ffn_dw_b=v_ffn_dw_b, w_down=v_w_down, norm_ffn_post=v_norm_ffn_post)

    x2 = x[0]
    target = loss_target[0]
    t_rows = x2.shape[0]
    d = D_MODEL
    in_cols = 2 * CONV_WIDTH + 3 * ATTN_WIDTH
    my_x, my_y, my_c = _mesh_position()
    my_dev = 4 * my_x + 2 * my_y + my_c

    small_conv = _pack([conv_dw_w[0], ffn_dw_w[0]], 32)
    me = jnp.reshape(my_dev, (1,)).astype(jnp.int32)
    first_shards = [w_in[0].T.astype(BF16), small_conv]
    late_shards = [w_out[0].astype(BF16), w_up[0].T.astype(BF16), w_down[0].astype(BF16)]
    placed = _place_own(first_shards + late_shards, me)
    win_t, conv_g = _all_gather(first_shards, "all_gather_weights", placed=placed[:2])
    wout_gather = _scatter_start(late_shards[:1], placed[2:3], True, "gather_w_out_start", after=win_t)
    ffn_gather = _scatter_start(late_shards[1:], placed[3:], True, "gather_ffn_weights_start", after=wout_gather[4])
    late_token = ffn_gather[4]
    win_t = win_t.reshape(in_cols, d)
    conv_flat = conv_g.reshape(N_DEV, 32 * LANES)
    n_cw = CONV_K * (CONV_WIDTH // N_DEV)
    conv_w_full = conv_flat[:, :n_cw].reshape(N_DEV, CONV_K, CONV_WIDTH // N_DEV).transpose(1, 0, 2).reshape(CONV_K, CONV_WIDTH)
    ffn_w_full = conv_flat[:, n_cw:].reshape(N_DEV, FFN_K, 2 * D_FF // N_DEV).transpose(1, 0, 2).reshape(FFN_K, 2 * D_FF)

    u1 = _pre_norm(x2, norm_mix_pre + late_token[0:1, 0:1], "pre_norm_mix")
    proj_a = _matmul(u1, win_t, mode="nt", m=t_rows, n=2 * CONV_WIDTH, k=d, tm=2048, tn=1024, tk=d,
                     out_dtype=F32, name="proj_conv")
    qkv_t = _matmul(win_t, u1, mode="nt", m=3 * ATTN_WIDTH, n=t_rows, k=d, tm=512, tn=2048, tk=d,
                    out_dtype=BF16, name="proj_qkv", a_m0=2 * CONV_WIDTH)
    conv_c, conv_out = _conv_forward(proj_a, conv_w_full, conv_dw_b, conv_ln_g, conv_ln_b)
    o_t, attn_lse = _attn_forward(qkv_t, rel_bias[0])
    _, (wout_g,) = _scatter_wait(*wout_gather[:4], o_t, True, "gather_w_out_wait")
    wout_g = wout_g.reshape(d, d)
    mixed, h1, u2 = _matmul_rows(
        [(conv_out, "nn", CONV_WIDTH, 0), (o_t, "tn", ATTN_WIDTH, CONV_WIDTH)], wout_g, m=t_rows, n=d, tm=1024,
        name="out_proj_mid_forward", row_ins=[x2], vec_ins=[norm_mix_post, norm_ffn_pre], row_outs=[F32, F32, BF16],
        acc_outs=[], epilogue=_mid_forward_epilogue)
    _, (wup_t, wdown_g) = _scatter_wait(*ffn_gather[:4], u2, True, "gather_ffn_weights_wait")
    wup_t = wup_t.reshape(2 * D_FF, d)
    wdown_g = wdown_g.reshape(D_FF, d)
    hup = _matmul(u2, wup_t, mode="nt", m=t_rows, n=2 * D_FF, k=d, tm=2048, tn=1408, tk=d,
                  out_dtype=F32, name="ffn_up")
    act, ffn_gel, ffn_slope = _ffn_activation(hup, ffn_w_full, ffn_dw_b)
    dy, df, _, d_norm_ffn_post, loss = _matmul_rows(
        [(act, "nn", D_FF, 0)], wdown_g, m=t_rows, n=d, tm=512, name="ffn_down_loss_backward",
        row_ins=[h1, target], vec_ins=[norm_ffn_post], row_outs=[F32, BF16], acc_outs=[(1, d), (1, d), (1, 1)],
        epilogue=_loss_epilogue)

    dact = _matmul(df, wdown_g, mode="nt", m=t_rows, n=D_FF, k=d, tm=2048, tn=1408, tk=d,
                   out_dtype=F32, name="ffn_down_dx")
    g_wdown = _matmul(act, df, mode="tn", m=D_FF, n=d, k=t_rows, tm=1408, tn=1024, tk=2048,
                      out_dtype=F32, name="ffn_down_dw")
    dhg, dhv, dwg, dwv, dbg, dbv = _ffn_backward(dact, ffn_gel, ffn_slope, hup, ffn_w_full)
    g_wup_t = _matmul(dhg, u2, mode="tn", m=D_FF, n=d, k=t_rows, tm=1408, tn=1024, tk=2048, out_dtype=F32,
                      name="ffn_up_dw_gate", out_rows=2 * D_FF)
    g_wup_t = _matmul(dhv, u2, mode="tn", m=D_FF, n=d, k=t_rows, tm=1408, tn=1024, tk=2048, out_dtype=F32,
                      name="ffn_up_dw_value", out_rows=2 * D_FF, out_m0=D_FF, into=g_wup_t)
    ffn_grads = [g_wup_t.reshape(N_DEV, 2 * D_FF // N_DEV, d), g_wdown.reshape(N_DEV, D_FF // N_DEV, d)]
    red_send, red_recv, ffn_grads, red_lands, red_token = _scatter_start(
        ffn_grads, [lax.empty(g.shape, F32) for g in ffn_grads], False, "reduce_ffn_grads_start")
    dh1, dmixed, d_norm_ffn_pre, d_norm_mix_post = _matmul_rows(
        [(dhg, "nn", D_FF, 0), (dhv, "nn", D_FF, D_FF)], wup_t, m=t_rows, n=d, tm=256, name="ffn_up_dx_mid_backward",
        row_ins=[dy, h1, mixed], vec_ins=[norm_ffn_pre + red_token[0:1, 0:1], norm_mix_post], row_outs=[F32, BF16],
        acc_outs=[(1, d), (1, d)], epilogue=_mid_backward_epilogue)
    dconv_out = _matmul(dmixed, wout_g, mode="nt", m=t_rows, n=CONV_WIDTH, k=d, tm=2048, tn=512, tk=d,
                        out_dtype=F32, name="out_proj_dx_conv")
    do_t = _matmul(wout_g, dmixed, mode="nt", m=ATTN_WIDTH, n=t_rows, k=d, tm=512, tn=2048, tk=d,
                   out_dtype=BF16, name="out_proj_dx_attn", a_m0=CONV_WIDTH)
    g_wout = _matmul(conv_out, dmixed, mode="tn", m=CONV_WIDTH, n=d, k=t_rows, tm=512, tn=1024, tk=2048, out_dtype=F32,
                     name="out_proj_dw_conv", out_rows=d)
    g_wout = _matmul(o_t, dmixed, mode="nn", m=ATTN_WIDTH, n=d, k=t_rows, tm=512, tn=1024, tk=2048, out_dtype=F32,
                     name="out_proj_dw_attn", out_rows=d, out_m0=CONV_WIDTH, into=g_wout)
    wout_handle = _scatter_start([g_wout.reshape(N_DEV, d // N_DEV, d)], [lax.empty((N_DEV, d // N_DEV, d), F32)],
                                 False, "reduce_w_out_grad_start")
    dproj_a, d_conv_w, d_conv_b, d_ln_g, d_ln_b = _conv_backward(
        dconv_out, conv_c, proj_a, conv_w_full, conv_ln_g + wout_handle[4][0:1, 0:1], conv_ln_b)
    dqkv_parts = _attn_backward(qkv_t, o_t, do_t, attn_lse, rel_bias[0])
    drel = dqkv_parts[3]
    g_win_t = _matmul(dproj_a, u1, mode="tn", m=2 * CONV_WIDTH, n=d, k=t_rows, tm=1024, tn=1024, tk=2048, out_dtype=BF16,
                      name="proj_dw_conv", out_rows=in_cols)
    for j, part in enumerate("qkv"):
        row0 = 2 * CONV_WIDTH + j * ATTN_WIDTH
        g_win_t = _matmul(dqkv_parts[j], u1, mode="nn", m=ATTN_WIDTH, n=d, k=t_rows, tm=512, tn=1024, tk=2048,
                          out_dtype=BF16, name="proj_dw_" + part, out_rows=in_cols, out_m0=row0, into=g_win_t)
    win_handle = _scatter_start([g_win_t.reshape(N_DEV, in_cols // N_DEV, d)],
                                [lax.empty((N_DEV, in_cols // N_DEV, d), BF16)], False, "reduce_w_in_grad_start")
    dx, d_norm_mix_pre = _matmul_rows(
        [(dproj_a, "nn", 2 * CONV_WIDTH, 0)]
        + [(dqkv_parts[j], "tn", ATTN_WIDTH, 2 * CONV_WIDTH + j * ATTN_WIDTH) for j in range(3)],
        win_t, m=t_rows, n=d, tm=1024, name="proj_dx_input_backward", row_ins=[dh1, x2],
        vec_ins=[norm_mix_pre + win_handle[4][0:1, 0:1]], row_outs=[F32], acc_outs=[(1, d)],
        epilogue=_input_backward_epilogue)

    small_grads = dict(norm_mix_pre=d_norm_mix_pre, conv_dw_b=d_conv_b, conv_ln_g=d_ln_g, conv_ln_b=d_ln_b,
                       rel_bias=drel[:, :2 * MAX_REL + 1], norm_mix_post=d_norm_mix_post, norm_ffn_pre=d_norm_ffn_pre,
                       ffn_dw_b=jnp.concatenate([dbg, dbv], axis=1), norm_ffn_post=d_norm_ffn_post)
    pieces = [small_grads[nm] for nm in SMALL] + [d_conv_w, jnp.concatenate([dwg, dwv], axis=1), loss]
    count = sum(p.size for p in pieces)
    (gathered_small,) = _all_gather([_pack(pieces, _rows_for(count))], "all_gather_small_grads")
    summed = _sum_devices(gathered_small)
    shapes = [weights[nm].shape for nm in SMALL] + [(CONV_K, CONV_WIDTH), (FFN_K, 2 * D_FF), (1, 1)]
    unpacked = _unpack(summed, shapes)
    grads = dict(zip(SMALL, unpacked[:len(SMALL)]))
    cw_shard, fw_shard = CONV_WIDTH // N_DEV, 2 * D_FF // N_DEV
    grads["conv_dw_w"] = lax.dynamic_slice_in_dim(unpacked[-3], my_dev * cw_shard, cw_shard, axis=1)[None]
    grads["ffn_dw_w"] = lax.dynamic_slice_in_dim(unpacked[-2], my_dev * fw_shard, fw_shard, axis=1)[None]
    total_loss = unpacked[-1].reshape(())

    me = jnp.reshape(my_dev, (1,)).astype(jnp.int32)
    delta, new_m, new_v = {}, {}, {}

    def finish(nm, send, recv, srcs, lands, after, transposed):
        srcs, lands = _scatter_wait(send, recv, srcs, lands, after, False, "reduce_" + nm + "_grad_wait")
        for name_a, src, land in zip(nm.split("_and_"), srcs, lands):
            flip = (lambda t: t.T) if transposed[name_a] else (lambda t: t)
            outs = _reduce_and_adamw(src, land, me, flip(weights[name_a][0]), flip(mom1[name_a][0]),
                                     flip(mom2[name_a][0]), "reduce_adamw_" + name_a)
            for store, arr in zip((grads, delta, new_m, new_v), outs):
                store[name_a] = flip(arr)[None]

    transposed = dict(w_in=True, w_out=False, w_up=True, w_down=False)
    finish("w_up_and_w_down", red_send, red_recv, ffn_grads, red_lands, dx, transposed)
    finish("w_out", *wout_handle[:4], dx, transposed)
    finish("w_in", *win_handle[:4], delta["w_up"], transposed)
    small_names = SMALL + SHARDED_SMALL
    small_count = sum(weights[nm].size for nm in small_names)
    small_rows = _rows_for(small_count)
    packed = [_pack([src[nm] for nm in small_names], small_rows) for src in (weights, grads, mom1, mom2)]
    outs = _adamw(*packed, "adamw_small")
    small_shapes = [weights[nm].shape for nm in small_names]
    for store, arr in zip((delta, new_m, new_v), outs):
        store.update(zip(small_names, _unpack(arr, small_shapes)))

    return (total_loss, dx[None], *[grads[nm] for nm in WEIGHTS], *[delta[nm] for nm in WEIGHTS],
            *[new_m[nm] for nm in WEIGHTS], *[new_v[nm] for nm in WEIGHTS])
```

```python
import jax
import jax.numpy as jnp
from jax import lax
from jax.experimental import pallas as pl
from jax.experimental.pallas import tpu as pltpu

F32 = jnp.float32
BF16 = jnp.bfloat16
MESH = pl.DeviceIdType.MESH
AXES = ("x", "y", "c")
N_DEV = 8

EPS = 1e-6
NEG_INF = -1e30
D_MODEL = 1024
CONV_WIDTH = 512
ATTN_WIDTH = 512
N_HEADS = 8
HEAD_DIM = 64
CHUNK = 64
LEFT = 8 * CHUNK
QBLK = 2 * CHUNK
WIN = LEFT + QBLK
CONV_K = 31
CONV_HALO = 32
FFN_K = 3
FFN_HALO = 8
D_FF = 2816
MAX_REL = 128
SCALE = HEAD_DIM ** -0.5
ADAM_LR, ADAM_B1, ADAM_B2, ADAM_EPS, ADAM_WD, ADAM_STEP = 0.001, 0.9, 0.999, 1e-08, 0.01, 10

V7X_VMEM_BYTES = 64 * 2**20
VMEM_LIMIT_BYTES = V7X_VMEM_BYTES - 8 * 2**20
LANES = 128


def _params(*sem):
    return pltpu.CompilerParams(dimension_semantics=sem or None, vmem_limit_bytes=VMEM_LIMIT_BYTES)


_DOT_DIMS = {"nn": (((1,), (0,)), ((), ())), "nt": (((1,), (1,)), ((), ())), "tn": (((0,), (0,)), ((), ()))}


def _matmul(a, b, *, mode, m, n, k, tm, tn, tk, out_dtype, name, a_m0=0, b_n0=0, b_k0=0, add=None,
            out_rows=None, out_m0=0, into=None):
    tm, tn, tk = min(tm, m), min(tn, n), min(tk, k)
    out_rows = m if out_rows is None else out_rows
    assert m % tm == 0 and n % tn == 0 and k % tk == 0, (name, m, n, k, tm, tn, tk)
    assert a_m0 % tm == 0 and b_n0 % tn == 0 and b_k0 % tk == 0 and out_m0 % tm == 0, name
    am, bn, bk, om = a_m0 // tm, b_n0 // tn, b_k0 // tk, out_m0 // tm
    gk = k // tk
    dims = _DOT_DIMS[mode]

    if mode == "tn":
        a_spec = pl.BlockSpec((tk, tm), lambda i, j, kk: (kk, i + am))
    else:
        a_spec = pl.BlockSpec((tm, tk), lambda i, j, kk: (i + am, kk))
    if mode == "nt":
        b_spec = pl.BlockSpec((tn, tk), lambda i, j, kk: (j + bn, kk + bk))
    else:
        b_spec = pl.BlockSpec((tk, tn), lambda i, j, kk: (kk + bk, j + bn))
    o_spec = pl.BlockSpec((tm, tn), lambda i, j, kk: (i + om, j))
    in_specs = [a_spec, b_spec]
    operands = [a, b]
    if add is not None:
        assert out_rows == m
        in_specs.append(o_spec)
        operands.append(add)
    aliases = {}
    if into is not None:
        aliases = {len(operands): 0}
        in_specs.append(pl.BlockSpec(memory_space=pl.ANY))
        operands.append(into)

    def body(*refs):
        a_ref, b_ref = refs[0], refs[1]
        add_ref = refs[2] if add is not None else None
        o_ref = refs[len(operands)]
        part = lax.dot_general(a_ref[...].astype(BF16), b_ref[...].astype(BF16), dims,
                               preferred_element_type=F32)

        def finish(total):
            if add_ref is not None:
                total = total + add_ref[...]
            o_ref[...] = total.astype(out_dtype)

        if gk == 1:
            finish(part)
        else:
            acc_ref = refs[-1]
            kk = pl.program_id(2)

            @pl.when(kk == 0)
            def _():
                acc_ref[...] = part

            @pl.when(kk > 0)
            def _():
                acc_ref[...] += part

            @pl.when(kk == gk - 1)
            def _():
                finish(acc_ref[...])

    return pl.pallas_call(
        body, name=name,
        grid=(m // tm, n // tn, gk),
        in_specs=in_specs, out_specs=o_spec,
        out_shape=jax.ShapeDtypeStruct((out_rows, n), out_dtype),
        scratch_shapes=[pltpu.VMEM((tm, tn), F32)] if gk > 1 else [],
        input_output_aliases=aliases,
        compiler_params=_params("parallel", "parallel", "arbitrary"),
    )(*operands)


def _matmul_rows(pieces, b, *, m, n, tm, name, row_ins, vec_ins, row_outs, acc_outs, epilogue):
    tm = min(tm, m)
    assert m % tm == 0
    steps = m // tm
    in_specs, operands = [], []
    for a, mode, k, k0 in pieces:
        assert k0 % k == 0
        if mode == "tn":
            in_specs.append(pl.BlockSpec((k, tm), lambda i: (0, i)))
        else:
            in_specs.append(pl.BlockSpec((tm, k), lambda i: (i, 0)))
        in_specs.append(pl.BlockSpec((k, n), lambda i, blk=k0 // k: (blk, 0)))
        operands += [a, b]
    row = pl.BlockSpec((tm, n), lambda i: (i, 0))
    in_specs += [row] * len(row_ins) + [pl.BlockSpec((1, n), lambda i: (0, 0))] * len(vec_ins)
    operands += [*row_ins, *vec_ins]
    n_in = len(operands)

    def body(*refs):
        total = None
        for p, (_, mode, _, _) in enumerate(pieces):
            part = lax.dot_general(refs[2 * p][...], refs[2 * p + 1][...], _DOT_DIMS[mode], preferred_element_type=F32)
            total = part if total is None else total + part
        first = 2 * len(pieces)
        rows = refs[first:first + len(row_ins)]
        vecs = refs[first + len(row_ins):n_in]
        outs = refs[n_in:n_in + len(row_outs)]
        accs = refs[n_in + len(row_outs):]
        epilogue(total, rows, vecs, outs, accs, pl.program_id(0), steps)

    return pl.pallas_call(
        body, name=name, grid=(steps,),
        in_specs=in_specs,
        out_specs=[row] * len(row_outs) + [pl.BlockSpec(s, lambda i: (0, 0)) for s in acc_outs],
        out_shape=[jax.ShapeDtypeStruct((m, n), dt) for dt in row_outs]
        + [jax.ShapeDtypeStruct(s, F32) for s in acc_outs],
        compiler_params=_params("arbitrary" if acc_outs else "parallel"),
    )(*operands)


def _rms_hat(v):
    r = lax.rsqrt(jnp.mean(v * v, axis=-1, keepdims=True) + EPS)
    return v * r, r


def _rms_bwd(dn, hat, r):
    return r * (dn - hat * jnp.mean(dn * hat, axis=-1, keepdims=True))


def _sigmoid(v):
    return 1.0 / (1.0 + jnp.exp(-v))


_GELU_C = 0.7978845608028654


def _gelu(v):
    return 0.5 * v * (1.0 + jnp.tanh(_GELU_C * (v + 0.044715 * v * (v * v))))


def _gelu_parts(v):
    v2 = v * v
    t = jnp.tanh(_GELU_C * (v + 0.044715 * v * v2))
    cdf = 0.5 * (1.0 + t)
    dcdf = 0.5 * (1.0 - t * t) * _GELU_C * (1.0 + 3.0 * 0.044715 * v2)
    return v * cdf, cdf + v * dcdf


def _row_tile(t_rows, want):
    tile = min(want, t_rows)
    assert t_rows % tile == 0
    return tile


def _pre_norm(x, g, name):
    t_rows, d = x.shape
    tm = _row_tile(t_rows, 512)

    def body(x_ref, g_ref, u_ref):
        hat, _ = _rms_hat(x_ref[...])
        u_ref[...] = (hat * g_ref[...]).astype(BF16)

    return pl.pallas_call(
        body, name=name, grid=(t_rows // tm,),
        in_specs=[pl.BlockSpec((tm, d), lambda i: (i, 0)), pl.BlockSpec((1, d), lambda i: (0, 0))],
        out_specs=pl.BlockSpec((tm, d), lambda i: (i, 0)),
        out_shape=jax.ShapeDtypeStruct((t_rows, d), BF16),
        compiler_params=_params("parallel"),
    )(x, g)


def _zero_at_start(accs, step):
    @pl.when(step == 0)
    def _():
        for acc in accs:
            acc[...] = jnp.zeros_like(acc)


def _mid_forward_epilogue(mixed, rows, vecs, outs, accs, step, steps):
    (x_ref,), (gpost_ref, gpre_ref), (mixed_ref, h1_ref, u2_ref) = rows, vecs, outs
    mixed_ref[...] = mixed
    hat, _ = _rms_hat(mixed)
    h1 = x_ref[...] + hat * gpost_ref[...]
    h1_ref[...] = h1
    hat1, _ = _rms_hat(h1)
    u2_ref[...] = (hat1 * gpre_ref[...]).astype(BF16)


def _loss_epilogue(f, rows, vecs, outs, accs, step, steps):
    (h1_ref, tgt_ref), (g_ref,), (dy_ref, df_ref), (sq_ref, dg_ref, loss_ref) = rows, vecs, outs, accs
    _zero_at_start(accs, step)
    g = g_ref[...]
    d = f.shape[-1]
    hat, r = _rms_hat(f)
    err = h1_ref[...] + hat * g - tgt_ref[...]
    sq_ref[...] += jnp.sum(err * err, axis=0, keepdims=True)
    dy = err * (1.0 / d)
    dy_ref[...] = dy
    dg_ref[...] += jnp.sum(dy * hat, axis=0, keepdims=True)
    df_ref[...] = _rms_bwd(dy * g, hat, r).astype(BF16)

    @pl.when(step == steps - 1)
    def _():
        loss_ref[...] = (0.5 / d) * jnp.sum(sq_ref[...], axis=1, keepdims=True)


def _mid_backward_epilogue(du2, rows, vecs, outs, accs, step, steps):
    (dy_ref, h1_ref, mixed_ref), (gpre_ref, gpost_ref), (dh1_ref, dmixed_ref), (dgpre_ref, dgpost_ref) = rows, vecs, outs, accs
    _zero_at_start(accs, step)
    hat1, r1 = _rms_hat(h1_ref[...])
    dgpre_ref[...] += jnp.sum(du2 * hat1, axis=0, keepdims=True)
    dh1 = dy_ref[...] + _rms_bwd(du2 * gpre_ref[...], hat1, r1)
    dh1_ref[...] = dh1
    hatm, rm = _rms_hat(mixed_ref[...])
    dgpost_ref[...] += jnp.sum(dh1 * hatm, axis=0, keepdims=True)
    dmixed_ref[...] = _rms_bwd(dh1 * gpost_ref[...], hatm, rm).astype(BF16)


def _input_backward_epilogue(du1, rows, vecs, outs, accs, step, steps):
    (dh1_ref, x_ref), (g_ref,), (dx_ref,), (dg_ref,) = rows, vecs, outs, accs
    _zero_at_start(accs, step)
    hat, r = _rms_hat(x_ref[...])
    dg_ref[...] += jnp.sum(du1 * hat, axis=0, keepdims=True)
    dx_ref[...] = dh1_ref[...] + _rms_bwd(du1 * g_ref[...], hat, r)


CONV_STRIP = 32


def _glu(block):
    return block[:, :CONV_WIDTH] * _sigmoid(block[:, CONV_WIDTH:])


def _layer_norm_parts(c):
    mu = jnp.mean(c, axis=-1, keepdims=True)
    xc = c - mu
    r = lax.rsqrt(jnp.mean(xc * xc, axis=-1, keepdims=True) + EPS)
    return xc * r, r


CONV_WINDOW = 2 * CONV_STRIP
SHIFTED_ROWS = CONV_WINDOW - 8


def _shifted_copies(v, shifted):
    for s in range(1, 8):
        shifted[s] = v[s:s + SHIFTED_ROWS, :]


def _window_rows(v, shifted, start):
    s, a = start % 8, start - start % 8
    return v[a:a + CONV_STRIP, :] if s == 0 else shifted[s, a:a + CONV_STRIP, :]


def _conv_forward(proj_a, w, b, ln_g, ln_b):
    t_rows = proj_a.shape[0]
    tm = _row_tile(t_rows, 512)
    hb = tm // CONV_HALO
    cw = CONV_WIDTH

    def body(cur_ref, prev_ref, w_ref, b_ref, g_ref, beta_ref, c_ref, out_ref, hbuf, shifted):
        i = pl.program_id(0)
        hbuf[0:CONV_HALO, :] = jnp.where(i > 0, _glu(prev_ref[...]), 0.0)
        hbuf[CONV_HALO:, :] = _glu(cur_ref[...])

        def strip(s, carry):
            base = pl.multiple_of(s * CONV_STRIP, CONV_STRIP)
            v = hbuf[pl.ds(base, CONV_WINDOW), :]
            _shifted_copies(v, shifted)
            acc = jnp.broadcast_to(b_ref[...], (CONV_STRIP, cw))
            off = CONV_HALO - (CONV_K - 1)
            for kk in range(CONV_K):
                acc = acc + w_ref[kk:kk + 1, :] * _window_rows(v, shifted, off + kk)
            c_ref[pl.ds(base, CONV_STRIP), :] = acc
            hat, _ = _layer_norm_parts(acc)
            z = hat * g_ref[...] + beta_ref[...]
            out_ref[pl.ds(base, CONV_STRIP), :] = (z * _sigmoid(z)).astype(BF16)
            return carry

        lax.fori_loop(0, tm // CONV_STRIP, strip, 0)

    vec = pl.BlockSpec((1, cw), lambda i: (0, 0))
    return pl.pallas_call(
        body, name="conv_forward", grid=(t_rows // tm,),
        in_specs=[pl.BlockSpec((tm, 2 * cw), lambda i: (i, 0)),
                  pl.BlockSpec((CONV_HALO, 2 * cw), lambda i: (jnp.maximum(i * hb - 1, 0), 0)),
                  pl.BlockSpec((CONV_K, cw), lambda i: (0, 0)), vec, vec, vec],
        out_specs=[pl.BlockSpec((tm, cw), lambda i: (i, 0)), pl.BlockSpec((tm, cw), lambda i: (i, 0))],
        out_shape=[jax.ShapeDtypeStruct((t_rows, cw), F32), jax.ShapeDtypeStruct((t_rows, cw), BF16)],
        scratch_shapes=[pltpu.VMEM((tm + CONV_HALO, cw), F32), pltpu.VMEM((8, SHIFTED_ROWS, cw), F32)],
        compiler_params=_params("parallel"),
    )(proj_a, proj_a, w, b, ln_g, ln_b)


def _conv_backward(dout, c, proj_a, w, ln_g, ln_b):
    t_rows = c.shape[0]
    tm = _row_tile(t_rows, 512)
    hb = tm // CONV_HALO
    nt = t_rows // tm
    last_halo = t_rows // CONV_HALO - 1
    cw = CONV_WIDTH

    def body(dout_ref, dout_next_ref, c_ref, c_next_ref, cur_ref, prev_ref, w_ref, g_ref, beta_ref,
             dproj_ref, dw_ref, db_ref, dg_ref, dbeta_ref, hbuf, dcbuf, dwacc, h_shifted, d_shifted):
        i = pl.program_id(0)

        @pl.when(i == 0)
        def _():
            dwacc[...] = jnp.zeros_like(dwacc)
            db_ref[...] = jnp.zeros_like(db_ref)
            dg_ref[...] = jnp.zeros_like(dg_ref)
            dbeta_ref[...] = jnp.zeros_like(dbeta_ref)

        def ln_swish_backward(dout_v, c_v):
            hat, r = _layer_norm_parts(c_v)
            g = g_ref[...]
            z = hat * g + beta_ref[...]
            sg = _sigmoid(z)
            dz = dout_v * (sg * (1.0 + z * (1.0 - sg)))
            dhat = dz * g
            dc = r * (dhat - jnp.mean(dhat, axis=-1, keepdims=True)
                      - hat * jnp.mean(dhat * hat, axis=-1, keepdims=True))
            return dc, dz, hat

        dc, dz, hat = ln_swish_backward(dout_ref[...], c_ref[...])
        dg_ref[...] += jnp.sum(dz * hat, axis=0, keepdims=True)
        dbeta_ref[...] += jnp.sum(dz, axis=0, keepdims=True)
        db_ref[...] += jnp.sum(dc, axis=0, keepdims=True)
        dcbuf[0:tm, :] = dc
        dc_next, _, _ = ln_swish_backward(dout_next_ref[...], c_next_ref[...])
        dcbuf[tm:, :] = jnp.where(i < nt - 1, dc_next, 0.0)

        hbuf[0:CONV_HALO, :] = jnp.where(i > 0, _glu(prev_ref[...]), 0.0)
        hbuf[CONV_HALO:, :] = _glu(cur_ref[...])

        def strip(s, carry):
            base = pl.multiple_of(s * CONV_STRIP, CONV_STRIP)
            dv = dcbuf[pl.ds(base, CONV_WINDOW), :]
            hv = hbuf[pl.ds(base, CONV_WINDOW), :]
            _shifted_copies(dv, d_shifted)
            _shifted_copies(hv, h_shifted)
            dcs = dv[0:CONV_STRIP, :]
            dh = jnp.zeros((CONV_STRIP, cw), F32)
            off = CONV_HALO - (CONV_K - 1)
            for kk in range(CONV_K):
                back = CONV_K - 1 - kk
                dh = dh + w_ref[kk:kk + 1, :] * _window_rows(dv, d_shifted, back)
                prod = dcs * _window_rows(hv, h_shifted, off + kk)
                dwacc[kk] += jnp.sum(prod.reshape(CONV_STRIP // 8, 8, cw), axis=0)
            blk = cur_ref[pl.ds(base, CONV_STRIP), :]
            val, sg = blk[:, :cw], _sigmoid(blk[:, cw:])
            dproj_ref[pl.ds(base, CONV_STRIP), 0:cw] = (dh * sg).astype(BF16)
            dproj_ref[pl.ds(base, CONV_STRIP), cw:2 * cw] = (dh * val * sg * (1.0 - sg)).astype(BF16)
            return carry

        lax.fori_loop(0, tm // CONV_STRIP, strip, 0)

        @pl.when(i == nt - 1)
        def _():
            for kk in range(CONV_K):
                dw_ref[kk:kk + 1, :] = jnp.sum(dwacc[kk], axis=0, keepdims=True)

    vec = pl.BlockSpec((1, cw), lambda i: (0, 0))
    cur = lambda width: pl.BlockSpec((tm, width), lambda i: (i, 0))
    nxt = lambda width: pl.BlockSpec((CONV_HALO, width), lambda i: (jnp.minimum((i + 1) * hb, last_halo), 0))
    return pl.pallas_call(
        body, name="conv_backward", grid=(nt,),
        in_specs=[cur(cw), nxt(cw), cur(cw), nxt(cw), cur(2 * cw),
                  pl.BlockSpec((CONV_HALO, 2 * cw), lambda i: (jnp.maximum(i * hb - 1, 0), 0)),
                  pl.BlockSpec((CONV_K, cw), lambda i: (0, 0)), vec, vec],
        out_specs=[cur(2 * cw), pl.BlockSpec((CONV_K, cw), lambda i: (0, 0)), vec, vec, vec],
        out_shape=[jax.ShapeDtypeStruct((t_rows, 2 * cw), BF16), jax.ShapeDtypeStruct((CONV_K, cw), F32),
                   jax.ShapeDtypeStruct((1, cw), F32), jax.ShapeDtypeStruct((1, cw), F32),
                   jax.ShapeDtypeStruct((1, cw), F32)],
        scratch_shapes=[pltpu.VMEM((tm + CONV_HALO, cw), F32), pltpu.VMEM((tm + CONV_HALO, cw), F32),
                        pltpu.VMEM((CONV_K, 8, cw), F32), pltpu.VMEM((8, SHIFTED_ROWS, cw), F32),
                        pltpu.VMEM((8, SHIFTED_ROWS, cw), F32)],
        compiler_params=_params("arbitrary"),
    )(dout, dout, c, c, proj_a, proj_a, w, ln_g, ln_b)


def _attn_load_kv(kv_hbm, k_pad, v_pad, sem, t_cols):
    k_pad[:, 0:LEFT] = jnp.zeros((ATTN_WIDTH, LEFT), BF16)
    v_pad[:, 0:LEFT] = jnp.zeros((ATTN_WIDTH, LEFT), BF16)
    ck = pltpu.make_async_copy(kv_hbm.at[pl.ds(ATTN_WIDTH, ATTN_WIDTH), :], k_pad.at[:, pl.ds(LEFT, t_cols)], sem.at[0])
    cv = pltpu.make_async_copy(kv_hbm.at[pl.ds(2 * ATTN_WIDTH, ATTN_WIDTH), :], v_pad.at[:, pl.ds(LEFT, t_cols)], sem.at[1])
    ck.start()
    cv.start()
    ck.wait()
    cv.wait()


def _attn_build_bias(tab_ref, bias_t):
    row = lax.broadcasted_iota(jnp.int32, (LANES, LANES), 0)
    lane = lax.broadcasted_iota(jnp.int32, (LANES, LANES), 1)
    upper = lane >= row
    lane64 = lax.broadcasted_iota(jnp.int32, (CHUNK, LANES), 1)
    for h in range(N_HEADS):
        far = jnp.broadcast_to(tab_ref[h:h + 1, 2 * MAX_REL:2 * MAX_REL + 1], (LANES, LANES))
        hi = jnp.broadcast_to(tab_ref[h:h + 1, MAX_REL:2 * MAX_REL], (LANES, LANES))
        lo = jnp.broadcast_to(tab_ref[h:h + 1, 0:MAX_REL], (LANES, LANES))
        hi_d = pltpu.roll(hi, 0, 1, stride=1, stride_axis=0)
        lo_d = pltpu.roll(lo, 0, 1, stride=1, stride_axis=0)
        bias_t[h, 0:WIN - 2 * LANES, :] = jnp.broadcast_to(far[0:1, :], (WIN - 2 * LANES, LANES))
        bias_t[h, WIN - 2 * LANES:WIN - LANES, :] = jnp.where(upper, far, hi_d)
        bias_t[h, WIN - LANES:WIN, :] = jnp.where(upper, hi_d, lo_d)
        bias_t[h, 0:CHUNK, :] = jnp.where(lane64 < CHUNK, bias_t[h, 0:CHUNK, :], NEG_INF)
        bias_t[h, WIN - CHUNK:WIN, :] = jnp.where(lane64 >= CHUNK, bias_t[h, WIN - CHUNK:WIN, :], NEG_INF)


def _head_rows(h):
    return slice(h * HEAD_DIM, (h + 1) * HEAD_DIM)


def _attn_scores(k_pad, q_ref, s_buf, w0):
    for h in range(N_HEADS):
        q_h = q_ref[_head_rows(h), :] * jnp.asarray(SCALE, BF16)
        s_buf[h] = lax.dot_general(k_pad[_head_rows(h), pl.ds(w0, WIN)], q_h, _DOT_DIMS["tn"],
                                   preferred_element_type=F32)


def _attn_logits(s, bias, first_valid, key0=0):
    s = s + bias
    if first_valid is not None:
        s = jnp.where(lax.broadcasted_iota(jnp.int32, s.shape, 0) + key0 >= first_valid, s, NEG_INF)
    return s


def _attn_probs(s, bias_h, first_valid):
    s = _attn_logits(s, bias_h, first_valid)
    top = jnp.max(s, axis=0, keepdims=True)
    e = jnp.exp(s - top)
    total = jnp.sum(e, axis=0, keepdims=True)
    return e * (1.0 / total), top + jnp.log(total)


def _attn_by_padding(m, fn):
    @pl.when(m < LEFT // QBLK)
    def _():
        fn(LEFT - m * QBLK)

    @pl.when(m >= LEFT // QBLK)
    def _():
        fn(None)


def _attn_forward(qkv_t, rel_bias):
    t_cols = qkv_t.shape[1]
    steps = t_cols // QBLK

    def body(q_ref, kv_hbm, tab_ref, o_ref, lse_ref, k_pad, v_pad, bias_t, s_buf, p_buf, sem):
        m = pl.program_id(0)

        @pl.when(m == 0)
        def _():
            _attn_build_bias(tab_ref, bias_t)
            _attn_load_kv(kv_hbm, k_pad, v_pad, sem, t_cols)

        w0 = pl.multiple_of(m * QBLK, QBLK)
        _attn_scores(k_pad, q_ref, s_buf, w0)

        def softmax(first_valid):
            for h in range(N_HEADS):
                p, lse = _attn_probs(s_buf[h], bias_t[h], first_valid)
                p_buf[h] = p.astype(BF16)
                lse_ref[h:h + 1, :] = lse

        _attn_by_padding(m, softmax)
        for h in range(N_HEADS):
            o_h = lax.dot_general(v_pad[_head_rows(h), pl.ds(w0, WIN)], p_buf[h], _DOT_DIMS["nn"],
                                  preferred_element_type=F32)
            o_ref[_head_rows(h), :] = o_h.astype(BF16)

    return pl.pallas_call(
        body, name="attn_forward", grid=(steps,),
        in_specs=[pl.BlockSpec((ATTN_WIDTH, QBLK), lambda m: (0, m)),
                  pl.BlockSpec(memory_space=pl.ANY),
                  pl.BlockSpec((N_HEADS, 2 * MAX_REL + 1), lambda m: (0, 0))],
        out_specs=[pl.BlockSpec((ATTN_WIDTH, QBLK), lambda m: (0, m)), pl.BlockSpec((N_HEADS, QBLK), lambda m: (0, m))],
        out_shape=[jax.ShapeDtypeStruct((ATTN_WIDTH, t_cols), BF16), jax.ShapeDtypeStruct((N_HEADS, t_cols), F32)],
        scratch_shapes=[pltpu.VMEM((ATTN_WIDTH, LEFT + t_cols), BF16), pltpu.VMEM((ATTN_WIDTH, LEFT + t_cols), BF16),
                        pltpu.VMEM((N_HEADS, WIN, QBLK), F32), pltpu.VMEM((N_HEADS, WIN, QBLK), F32),
                        pltpu.VMEM((N_HEADS, WIN, QBLK), BF16), pltpu.SemaphoreType.DMA((2,))],
        compiler_params=_params("arbitrary"),
    )(qkv_t, qkv_t, rel_bias)


def _reverse_lanes(v, flip):
    out = jnp.zeros(v.shape, F32)
    rest = v
    for _ in range(3):
        piece = rest.astype(BF16)
        out = out + lax.dot_general(piece, flip, _DOT_DIMS["nn"], preferred_element_type=F32)
        rest = rest - piece.astype(F32)
    return out


def _attn_bias_grad(dbias_t, drel_ref):
    row = lax.broadcasted_iota(jnp.int32, (LANES, LANES), 0)
    lane = lax.broadcasted_iota(jnp.int32, (LANES, LANES), 1)
    flip = (row + lane == LANES - 1).astype(BF16)
    head = lax.broadcasted_iota(jnp.int32, (N_HEADS, LANES), 0)
    lane8 = lax.broadcasted_iota(jnp.int32, (N_HEADS, LANES), 1)
    upper_rev = jnp.zeros((N_HEADS, LANES), F32)
    lower_rev = jnp.zeros((N_HEADS, LANES), F32)
    far = jnp.zeros((N_HEADS, LANES), F32)
    for h in range(N_HEADS):
        def diagonals(block):
            skew = pltpu.roll(_reverse_lanes(block, flip), 0, 1, stride=1, stride_axis=0)
            pos = jnp.sum(jnp.where(lane >= row, skew, 0.0), axis=0, keepdims=True)
            neg = jnp.sum(jnp.where(lane < row, skew, 0.0), axis=0, keepdims=True)
            return pos, neg

        pos4, neg4 = diagonals(dbias_t[h, WIN - LANES:WIN, :])
        pos3, neg3 = diagonals(dbias_t[h, WIN - 2 * LANES:WIN - LANES, :])
        far_h = jnp.sum(jnp.sum(dbias_t[h, 0:WIN - 2 * LANES, :], axis=0, keepdims=True), axis=1, keepdims=True)
        far_h = far_h + jnp.sum(pos3, axis=1, keepdims=True)
        upper_rev = jnp.where(head == h, pos4 + neg3, upper_rev)
        lower_rev = jnp.where(head == h, neg4, lower_rev)
        far = jnp.where((head == h) & (lane8 == 0), far_h, far)
    drel_ref[:, 0:LANES] = _reverse_lanes(lower_rev, flip)
    drel_ref[:, LANES:2 * LANES] = _reverse_lanes(upper_rev, flip)
    drel_ref[:, 2 * LANES:3 * LANES] = far


def _attn_backward(qkv_t, o_t, do_t, lse, rel_bias):
    t_cols = qkv_t.shape[1]
    steps = t_cols // QBLK
    flush = LEFT // QBLK
    total = steps + flush

    def body(q_ref, o_ref, do_ref, lse_ref, kv_hbm, tab_ref, dq_ref, dk_ref, dv_ref, drel_ref,
             k_pad, v_pad, bias_t, dbias_t, dk_acc, dv_acc, s_buf, dp_buf, p_buf, ds_buf, sem):
        m = pl.program_id(0)

        @pl.when(m == 0)
        def _():
            _attn_build_bias(tab_ref, bias_t)
            _attn_load_kv(kv_hbm, k_pad, v_pad, sem, t_cols)
            dbias_t[...] = jnp.zeros_like(dbias_t)
            dk_acc[...] = jnp.zeros_like(dk_acc)
            dv_acc[...] = jnp.zeros_like(dv_acc)

        @pl.when(m < steps)
        def _():
            w0 = pl.multiple_of(m * QBLK, QBLK)
            _attn_scores(k_pad, q_ref, s_buf, w0)
            for h in range(N_HEADS):
                dp_buf[h] = lax.dot_general(v_pad[_head_rows(h), pl.ds(w0, WIN)], do_ref[_head_rows(h), :],
                                            _DOT_DIMS["tn"], preferred_element_type=F32)

            def softmax_backward(first_valid):
                for h in range(N_HEADS):
                    rows = _head_rows(h)
                    delta = jnp.sum(do_ref[rows, :].astype(F32) * o_ref[rows, :].astype(F32), axis=0, keepdims=True)
                    lse_h = lse_ref[h:h + 1, :]
                    for b in range(WIN // LANES):
                        keys = slice(b * LANES, (b + 1) * LANES)
                        s = _attn_logits(s_buf[h, keys, :], bias_t[h, keys, :], first_valid, b * LANES)
                        p = jnp.exp(s - lse_h)
                        ds = p * (dp_buf[h, keys, :] - delta)
                        dbias_t[h, keys, :] += ds
                        p_buf[h, keys, :] = p.astype(BF16)
                        ds_buf[h, keys, :] = (ds * SCALE).astype(BF16)

            _attn_by_padding(m, softmax_backward)
            for h in range(N_HEADS):
                rows = _head_rows(h)
                dq_h = lax.dot_general(k_pad[rows, pl.ds(w0, WIN)], ds_buf[h], _DOT_DIMS["nn"], preferred_element_type=F32)
                dq_ref[rows, :] = dq_h.astype(BF16)
                dk_h = lax.dot_general(q_ref[rows, :], ds_buf[h], _DOT_DIMS["nt"], preferred_element_type=F32)
                dv_h = lax.dot_general(do_ref[rows, :], p_buf[h], _DOT_DIMS["nt"], preferred_element_type=F32)
                for b in range(WIN // QBLK):
                    slot = pl.multiple_of(lax.rem(m + b, WIN // QBLK) * QBLK, QBLK)
                    dk_acc[rows, pl.ds(slot, QBLK)] += dk_h[:, b * QBLK:(b + 1) * QBLK]
                    dv_acc[rows, pl.ds(slot, QBLK)] += dv_h[:, b * QBLK:(b + 1) * QBLK]

        oldest = pl.multiple_of(lax.rem(m, WIN // QBLK) * QBLK, QBLK)
        dk_ref[...] = dk_acc[:, pl.ds(oldest, QBLK)].astype(BF16)
        dv_ref[...] = dv_acc[:, pl.ds(oldest, QBLK)].astype(BF16)
        dk_acc[:, pl.ds(oldest, QBLK)] = jnp.zeros((ATTN_WIDTH, QBLK), F32)
        dv_acc[:, pl.ds(oldest, QBLK)] = jnp.zeros((ATTN_WIDTH, QBLK), F32)

        @pl.when(m == total - 1)
        def _():
            _attn_bias_grad(dbias_t, drel_ref)

    qblk = pl.BlockSpec((ATTN_WIDTH, QBLK), lambda m: (0, jnp.minimum(m, steps - 1)))
    kblk = pl.BlockSpec((ATTN_WIDTH, QBLK), lambda m: (0, jnp.maximum(m - flush, 0)))
    dq, dk, dv, drel = pl.pallas_call(
        body, name="attn_backward", grid=(total,),
        in_specs=[qblk, qblk, qblk, pl.BlockSpec((N_HEADS, QBLK), lambda m: (0, jnp.minimum(m, steps - 1))),
                  pl.BlockSpec(memory_space=pl.ANY), pl.BlockSpec((N_HEADS, 2 * MAX_REL + 1), lambda m: (0, 0))],
        out_specs=[qblk, kblk, kblk, pl.BlockSpec((N_HEADS, 3 * LANES), lambda m: (0, 0))],
        out_shape=[jax.ShapeDtypeStruct((ATTN_WIDTH, t_cols), BF16)] * 3
        + [jax.ShapeDtypeStruct((N_HEADS, 3 * LANES), F32)],
        scratch_shapes=[pltpu.VMEM((ATTN_WIDTH, LEFT + t_cols), BF16), pltpu.VMEM((ATTN_WIDTH, LEFT + t_cols), BF16),
                        pltpu.VMEM((N_HEADS, WIN, QBLK), F32), pltpu.VMEM((N_HEADS, WIN, QBLK), F32),
                        pltpu.VMEM((ATTN_WIDTH, WIN), F32), pltpu.VMEM((ATTN_WIDTH, WIN), F32),
                        pltpu.VMEM((N_HEADS, WIN, QBLK), F32), pltpu.VMEM((N_HEADS, WIN, QBLK), F32),
                        pltpu.VMEM((N_HEADS, WIN, QBLK), BF16), pltpu.VMEM((N_HEADS, WIN, QBLK), BF16),
                        pltpu.SemaphoreType.DMA((2,))],
        compiler_params=_params("arbitrary"),
    )(qkv_t, o_t, do_t, lse, qkv_t, rel_bias)
    return dq, dk, dv, drel


FFN_TC = D_FF // 2
FFN_TR = 512


def _ffn_specs(t_rows, tr):
    nj = D_FF // FFN_TC
    hb = tr // FFN_HALO
    last_halo = t_rows // FFN_HALO - 1
    cur = lambda off: pl.BlockSpec((tr, FFN_TC), lambda j, i: (i, j + off))
    prev = lambda off: pl.BlockSpec((FFN_HALO, FFN_TC), lambda j, i: (jnp.maximum(i * hb - 1, 0), j + off))
    nxt = lambda off: pl.BlockSpec((FFN_HALO, FFN_TC), lambda j, i: (jnp.minimum((i + 1) * hb, last_halo), j + off))
    wspec = lambda off: pl.BlockSpec((FFN_K, FFN_TC), lambda j, i: (0, j + off))
    bspec = lambda off: pl.BlockSpec((1, FFN_TC), lambda j, i: (0, j + off))
    return nj, cur, prev, nxt, wspec, bspec


FFN_STRIP = 16


def _ffn_conv(win, w, b, rows):
    out = b + w[2] * win[FFN_HALO:FFN_HALO + rows, :]
    out = out + w[1] * win[FFN_HALO - 1:FFN_HALO - 1 + rows, :]
    return out + w[0] * win[FFN_HALO - 2:FFN_HALO - 2 + rows, :]


def _taps(w_ref):
    return [w_ref[kk:kk + 1, :] for kk in range(FFN_K)]


def _ffn_first_window(prev_ref, cur_ref, tile, rows):
    return jnp.concatenate([jnp.where(tile > 0, prev_ref[...], 0.0), cur_ref[0:rows, :]], axis=0)


def _fold8(v):
    return jnp.sum(v.reshape(v.shape[0] // 8, 8, v.shape[1]), axis=0)


def _ffn_activation(hup, w, b):
    t_rows = hup.shape[0]
    tr = _row_tile(t_rows, FFN_TR)
    nj, cur, prev, nxt, wspec, bspec = _ffn_specs(t_rows, tr)
    rs = FFN_STRIP

    def body(g_ref, gprev_ref, v_ref, vprev_ref, wg_ref, wv_ref, bg_ref, bv_ref, act_ref, gel_ref, slope_ref):
        i = pl.program_id(1)
        wg, wv, bg, bv = _taps(wg_ref), _taps(wv_ref), bg_ref[...], bv_ref[...]

        def emit(base, g_win, v_win):
            gel, dgel = _gelu_parts(_ffn_conv(g_win, wg, bg, rs))
            cv = _ffn_conv(v_win, wv, bv, rs)
            act_ref[pl.ds(base, rs), :] = (gel * cv).astype(BF16)
            gel_ref[pl.ds(base, rs), :] = gel
            slope_ref[pl.ds(base, rs), :] = cv * dgel

        def strip(s, carry):
            base = pl.multiple_of(s * rs, rs)
            emit(base, g_ref[pl.ds(base - FFN_HALO, rs + FFN_HALO), :], v_ref[pl.ds(base - FFN_HALO, rs + FFN_HALO), :])
            return carry

        emit(0, _ffn_first_window(gprev_ref, g_ref, i, rs), _ffn_first_window(vprev_ref, v_ref, i, rs))
        lax.fori_loop(1, tr // rs, strip, 0)

    return pl.pallas_call(
        body, name="ffn_activation", grid=(nj, t_rows // tr),
        in_specs=[cur(0), prev(0), cur(nj), prev(nj), wspec(0), wspec(nj), bspec(0), bspec(nj)],
        out_specs=[cur(0), cur(0), cur(0)],
        out_shape=[jax.ShapeDtypeStruct((t_rows, D_FF), BF16), jax.ShapeDtypeStruct((t_rows, D_FF), F32),
                   jax.ShapeDtypeStruct((t_rows, D_FF), F32)],
        compiler_params=_params("parallel", "parallel"),
    )(hup, hup, hup, hup, w, w, b, b)


def _ffn_backward(dact, gel, slope, hup, w):
    t_rows = hup.shape[0]
    tr = _row_tile(t_rows, FFN_TR)
    nj, cur, prev, nxt, wspec, bspec = _ffn_specs(t_rows, tr)
    ni = t_rows // tr
    rs = FFN_STRIP
    ns = tr // rs

    def body(da_ref, danext_ref, gel_ref, gelnext_ref, slope_ref, slopenext_ref, g_ref, gprev_ref, v_ref, vprev_ref,
             wg_ref, wv_ref, dhg_ref, dhv_ref, dwg_ref, dwv_ref, dbg_ref, dbv_ref, sums):
        i = pl.program_id(1)

        @pl.when(i == 0)
        def _():
            sums[...] = jnp.zeros_like(sums)

        wg, wv = _taps(wg_ref), _taps(wv_ref)
        da_after = jnp.where(i < ni - 1, danext_ref[...], 0.0)

        def strip_at(base, g_win, v_win, carry):
            da = da_ref[pl.ds(base, rs), :]
            dcg, dcv = da * slope_ref[pl.ds(base, rs), :], da * gel_ref[pl.ds(base, rs), :]
            out = []
            for half_i, (dc, after, taps, win, dh_ref) in enumerate(((dcg, carry[0], wg, g_win, dhg_ref),
                                                                   (dcv, carry[1], wv, v_win, dhv_ref))):
                ext = jnp.concatenate([dc, after], axis=0)
                dh = taps[2] * dc + taps[1] * ext[1:1 + rs, :] + taps[0] * ext[2:2 + rs, :]
                dh_ref[pl.ds(base, rs), :] = dh.astype(BF16)
                sums[4 * half_i] += _fold8(dc)
                for kk in range(FFN_K):
                    sums[4 * half_i + 1 + kk] += _fold8(dc * win[FFN_HALO - 2 + kk:FFN_HALO - 2 + kk + rs, :])
                out.append(dc[0:FFN_HALO, :])
            return tuple(out)

        def strip(s, carry):
            base = pl.multiple_of((ns - 1 - s) * rs, rs)
            return strip_at(base, g_ref[pl.ds(base - FFN_HALO, rs + FFN_HALO), :],
                            v_ref[pl.ds(base - FFN_HALO, rs + FFN_HALO), :], carry)

        carry = lax.fori_loop(0, ns - 1, strip, (da_after * slopenext_ref[...], da_after * gelnext_ref[...]))
        strip_at(0, _ffn_first_window(gprev_ref, g_ref, i, rs), _ffn_first_window(vprev_ref, v_ref, i, rs), carry)

        @pl.when(i == ni - 1)
        def _():
            for half_i, (db_ref, dw_ref) in enumerate(((dbg_ref, dwg_ref), (dbv_ref, dwv_ref))):
                db_ref[...] = jnp.sum(sums[4 * half_i], axis=0, keepdims=True)
                for kk in range(FFN_K):
                    dw_ref[kk:kk + 1, :] = jnp.sum(sums[4 * half_i + 1 + kk], axis=0, keepdims=True)

    half = jax.ShapeDtypeStruct((t_rows, D_FF), BF16)
    return pl.pallas_call(
        body, name="ffn_backward", grid=(nj, ni),
        in_specs=[cur(0), nxt(0), cur(0), nxt(0), cur(0), nxt(0), cur(0), prev(0), cur(nj), prev(nj),
                  wspec(0), wspec(nj)],
        out_specs=[cur(0), cur(0), wspec(0), wspec(0), bspec(0), bspec(0)],
        out_shape=[half, half, jax.ShapeDtypeStruct((FFN_K, D_FF), F32), jax.ShapeDtypeStruct((FFN_K, D_FF), F32),
                   jax.ShapeDtypeStruct((1, D_FF), F32), jax.ShapeDtypeStruct((1, D_FF), F32)],
        scratch_shapes=[pltpu.VMEM((2 * (1 + FFN_K), 8, FFN_TC), F32)],
        compiler_params=_params("parallel", "arbitrary"),
    )(dact, dact, gel, gel, slope, slope, hup, hup, hup, hup, w, w)


def _mesh_position():
    return lax.axis_index("x"), lax.axis_index("y"), lax.axis_index("c")


def _hbm_specs(n):
    return [pl.BlockSpec(memory_space=pl.ANY)] * n


def _all_gather(shards, name, placed=None):
    n = len(shards)

    def body(*refs):
        ins = refs[:n]
        outs = refs[2 * n:3 * n] if placed else refs[n:2 * n]
        send_sems, recv_sems, local_sems = refs[-3:]
        x, y, c = _mesh_position()
        me, sibling = (x, y, c), (x, y, 1 - c)
        chips = [(1 - x, y), (x, 1 - y), (1 - x, 1 - y)]

        def copy(a, slot, block, to, src=None):
            dst = outs[a].at[4 * block[0] + 2 * block[1] + block[2]]
            return pltpu.make_async_remote_copy(
                src_ref=dst if src is None else src, dst_ref=dst,
                send_sem=send_sems.at[a, slot], recv_sem=recv_sems.at[a, slot],
                device_id=to, device_id_type=MESH)

        started = []
        for a in range(0 if placed else n):
            mine = pltpu.make_async_copy(ins[a], outs[a].at[4 * x + 2 * y + c], local_sems.at[a])
            mine.start()
            started.append(mine)
        first = []
        for a in range(n):
            first.append(copy(a, 0, me, sibling, src=ins[a]))
            first += [copy(a, 1 + j, me, (*chip, c), src=ins[a]) for j, chip in enumerate(chips)]
        for cp in first:
            cp.start()
        passed = []
        for j, chip in enumerate(chips):
            for a in range(n):
                copy(a, 1 + j, (*chip, c), me).wait_recv()
                fwd = copy(a, 4 + j, (*chip, c), sibling)
                fwd.start()
                passed.append(fwd)
        for a in range(n):
            copy(a, 0, sibling, me).wait_recv()
            for j, chip in enumerate(chips):
                copy(a, 4 + j, (*chip, 1 - c), me).wait_recv()
        for cp in first + passed:
            cp.wait_send()
        for mine in started:
            mine.wait()

    operands = [*shards, *placed] if placed else list(shards)
    return pl.pallas_call(
        body, name=name,
        in_specs=_hbm_specs(len(operands)), out_specs=_hbm_specs(n),
        out_shape=[jax.ShapeDtypeStruct((N_DEV,) + s.shape, s.dtype) for s in shards],
        scratch_shapes=[pltpu.SemaphoreType.DMA((n, 7)), pltpu.SemaphoreType.DMA((n, 7)),
                        pltpu.SemaphoreType.DMA((n,))],
        input_output_aliases={n + a: a for a in range(n)} if placed else {},
        compiler_params=pltpu.CompilerParams(has_side_effects=True),
    )(*operands)


def _place_own(shards, me):
    n = len(shards)

    def body(me_ref, *refs):
        for src, dst in zip(refs[:n], refs[n:]):
            dst[0] = src[...]

    return pl.pallas_call(
        body, name="place_own_shards",
        grid_spec=pltpu.PrefetchScalarGridSpec(
            num_scalar_prefetch=1, grid=(1,),
            in_specs=[pl.BlockSpec(s.shape, lambda i, me_ref: (0, 0)) for s in shards],
            out_specs=[pl.BlockSpec((1,) + s.shape, lambda i, me_ref: (me_ref[0], 0, 0)) for s in shards]),
        out_shape=[jax.ShapeDtypeStruct((N_DEV,) + s.shape, s.dtype) for s in shards],
        compiler_params=_params("arbitrary"),
    )(me, *shards)


_FLIPS = [(dx, dy, dc) for dx in (0, 1) for dy in (0, 1) for dc in (0, 1)][1:]
_HBM = pl.BlockSpec(memory_space=pltpu.HBM)
_SEM = pl.BlockSpec(memory_space=pltpu.SEMAPHORE)
_DATAFLOW = pltpu.SideEffectType.DATAFLOW_SIDE_EFFECTING


def _scatter_copies(src_refs, land_refs, send_sems, recv_sems, gather):
    x, y, c = _mesh_position()
    me = 4 * x + 2 * y + c
    copies = []
    for a, (src, land) in enumerate(zip(src_refs, land_refs)):
        for k, (dx, dy, dc) in enumerate(_FLIPS):
            px, py, pc = (x + dx) % 2, (y + dy) % 2, (c + dc) % 2
            pair = a * len(_FLIPS) + k
            copies.append(pltpu.make_async_remote_copy(
                src_ref=src if gather else src.at[4 * px + 2 * py + pc], dst_ref=land.at[me],
                send_sem=send_sems[pair], recv_sem=recv_sems[pair],
                device_id=(px, py, pc), device_id_type=MESH))
    return copies


def _scatter_start(srcs, lands, gather, name, after=None):
    n = len(srcs)
    pairs = n * len(_FLIPS)
    extra = [] if after is None else [after]

    def body(*refs):
        src_refs, land_refs = refs[:n], refs[n:2 * n]
        first = 2 * n + len(extra)
        send_sems, recv_sems = refs[first:first + pairs], refs[first + pairs:first + 2 * pairs]
        token = refs[-1]
        for cp in _scatter_copies(src_refs, land_refs, send_sems, recv_sems, gather):
            cp.start()
        token[...] = jnp.zeros_like(token)

    arrays = [*srcs, *lands]
    sem = pltpu.SemaphoreType.DMA(())
    out = pl.pallas_call(
        body, name=name,
        out_shape=(*[sem] * (2 * pairs), *[pltpu.HBM(v.shape, v.dtype) for v in arrays],
                   jax.ShapeDtypeStruct((8, LANES), F32)),
        in_specs=[*[_HBM] * (2 * n), *[pl.BlockSpec(memory_space=pl.ANY)] * len(extra)],
        out_specs=(*[_SEM] * (2 * pairs), *[_HBM] * (2 * n), pl.BlockSpec(memory_space=pltpu.VMEM)),
        input_output_aliases={i: 2 * pairs + i for i in range(2 * n)},
        compiler_params=pltpu.CompilerParams(has_side_effects=_DATAFLOW),
    )(*[pltpu.with_memory_space_constraint(v, pltpu.HBM) for v in arrays], *extra)
    sems, rest = out[:2 * pairs], out[2 * pairs:]
    return list(sems[:pairs]), list(sems[pairs:]), list(rest[:n]), list(rest[n:2 * n]), rest[-1]


def _scatter_wait(send_sems, recv_sems, srcs, lands, after, gather, name):
    n = len(srcs)
    pairs = n * len(_FLIPS)

    def body(*refs):
        src_refs, land_refs = refs[:n], refs[n:2 * n]
        send_refs, recv_refs = refs[2 * n:2 * n + pairs], refs[2 * n + pairs:2 * n + 2 * pairs]
        for cp in _scatter_copies(src_refs, land_refs, send_refs, recv_refs, gather):
            cp.wait_send()
            cp.wait_recv()

    arrays = [*srcs, *lands]
    out = pl.pallas_call(
        body, name=name,
        out_shape=tuple(pltpu.HBM(v.shape, v.dtype) for v in arrays),
        in_specs=[*[_HBM] * (2 * n), *[_SEM] * (2 * pairs), pl.BlockSpec(memory_space=pl.ANY)],
        out_specs=tuple([_HBM] * (2 * n)),
        input_output_aliases={i: i for i in range(2 * n)},
        compiler_params=pltpu.CompilerParams(has_side_effects=_DATAFLOW),
    )(*arrays, *send_sems, *recv_sems, after)
    return list(out[:n]), list(out[n:])


def _sum_devices(gathered):
    _, rows, cols = gathered.shape

    def body(g_ref, o_ref):
        total = g_ref[0]
        for d in range(1, N_DEV):
            total = total + g_ref[d]
        o_ref[...] = total

    return pl.pallas_call(
        body, name="sum_small_grads",
        out_shape=jax.ShapeDtypeStruct((rows, cols), F32),
        compiler_params=_params(),
    )(gathered)


def _adamw(w, g, m, v, name):
    rows, cols = w.shape
    tr = rows
    for cand in (256, 128, 64, 32, 16, 8):
        if rows > cand and rows % cand == 0:
            tr = cand
            break

    def body(w_ref, g_ref, m_ref, v_ref, delta_ref, newm_ref, newv_ref):
        g_v = g_ref[...]
        new_m = ADAM_B1 * m_ref[...] + (1.0 - ADAM_B1) * g_v
        new_v = ADAM_B2 * v_ref[...] + (1.0 - ADAM_B2) * (g_v * g_v)
        m_hat = new_m / (1.0 - ADAM_B1 ** ADAM_STEP)
        v_hat = new_v / (1.0 - ADAM_B2 ** ADAM_STEP)
        delta_ref[...] = -ADAM_LR * (m_hat / (jnp.sqrt(v_hat) + ADAM_EPS) + ADAM_WD * w_ref[...])
        newm_ref[...] = new_m
        newv_ref[...] = new_v

    blk = pl.BlockSpec((tr, cols), lambda i: (i, 0))
    shape = jax.ShapeDtypeStruct((rows, cols), F32)
    return pl.pallas_call(
        body, name=name, grid=(rows // tr,),
        in_specs=[blk] * 4, out_specs=[blk] * 3, out_shape=[shape] * 3,
        compiler_params=_params("parallel"),
    )(w, g, m, v)


def _adamw_update(w, g, m, v):
    new_m = ADAM_B1 * m + (1.0 - ADAM_B1) * g
    new_v = ADAM_B2 * v + (1.0 - ADAM_B2) * (g * g)
    m_hat = new_m / (1.0 - ADAM_B1 ** ADAM_STEP)
    v_hat = new_v / (1.0 - ADAM_B2 ** ADAM_STEP)
    return -ADAM_LR * (m_hat / (jnp.sqrt(v_hat) + ADAM_EPS) + ADAM_WD * w), new_m, new_v


def _reduce_and_adamw(grad, received, me, w, m, v, name):
    _, rows, cols = grad.shape
    tr = rows // 2
    assert rows % 16 == 0

    def body(me_ref, g_ref, r_ref, w_ref, m_ref, v_ref, grad_ref, delta_ref, newm_ref, newv_ref, acc):
        p = pl.program_id(1)
        term = jnp.where(p == me_ref[0], g_ref[0], r_ref[0]).astype(F32)

        @pl.when(p == 0)
        def _():
            acc[...] = term

        @pl.when(p > 0)
        def _():
            acc[...] += term

        @pl.when(p == N_DEV - 1)
        def _():
            g = acc[...]
            grad_ref[...] = g
            delta_ref[...], newm_ref[...], newv_ref[...] = _adamw_update(w_ref[...], g, m_ref[...], v_ref[...])

    blk = (1, tr, cols)
    tile = pl.BlockSpec((tr, cols), lambda j, p, me_ref: (j, 0))
    shape = jax.ShapeDtypeStruct((rows, cols), F32)
    return pl.pallas_call(
        body, name=name,
        grid_spec=pltpu.PrefetchScalarGridSpec(
            num_scalar_prefetch=1, grid=(rows // tr, N_DEV),
            in_specs=[pl.BlockSpec(blk, lambda j, p, me_ref: (me_ref[0], j, 0)),
                      pl.BlockSpec(blk, lambda j, p, me_ref: (p, j, 0)), tile, tile, tile],
            out_specs=[tile] * 4,
            scratch_shapes=[pltpu.VMEM((tr, cols), F32)]),
        out_shape=[shape] * 4,
        compiler_params=_params("parallel", "arbitrary"),
    )(me, grad, received, w, m, v)


def _pack(pieces, rows):
    flat = jnp.concatenate([p.reshape(-1) for p in pieces])
    return jnp.pad(flat, (0, rows * LANES - flat.shape[0])).reshape(rows, LANES)


def _unpack(packed, shapes):
    flat = packed.reshape(-1)
    out, pos = [], 0
    for shape in shapes:
        size = 1
        for s in shape:
            size *= s
        out.append(flat[pos:pos + size].reshape(shape))
        pos += size
    return out


def _rows_for(count):
    return -(-count // (8 * LANES)) * 8


SMALL = ("norm_mix_pre", "conv_dw_b", "conv_ln_g", "conv_ln_b", "rel_bias", "norm_mix_post", "norm_ffn_pre",
         "ffn_dw_b", "norm_ffn_post")
SHARDED_SMALL = ("conv_dw_w", "ffn_dw_w")
LARGE = ("w_in", "w_out", "w_up", "w_down")
WEIGHTS = ("norm_mix_pre", "w_in", "conv_dw_w", "conv_dw_b", "conv_ln_g", "conv_ln_b", "rel_bias", "w_out",
           "norm_mix_post", "norm_ffn_pre", "w_up", "ffn_dw_w", "ffn_dw_b", "w_down", "norm_ffn_post")


def kernel(x, norm_mix_pre, w_in, conv_dw_w, conv_dw_b, conv_ln_g, conv_ln_b, rel_bias, w_out, norm_mix_post, norm_ffn_pre, w_up, ffn_dw_w, ffn_dw_b, w_down, norm_ffn_post, loss_target, m_norm_mix_pre, m_w_in, m_conv_dw_w, m_conv_dw_b, m_conv_ln_g, m_conv_ln_b, m_rel_bias, m_w_out, m_norm_mix_post, m_norm_ffn_pre, m_w_up, m_ffn_dw_w, m_ffn_dw_b, m_w_down, m_norm_ffn_post, v_norm_mix_pre, v_w_in, v_conv_dw_w, v_conv_dw_b, v_conv_ln_g, v_conv_ln_b, v_rel_bias, v_w_out, v_norm_mix_post, v_norm_ffn_pre, v_w_up, v_ffn_dw_w, v_ffn_dw_b, v_w_down, v_norm_ffn_post):
    weights = dict(norm_mix_pre=norm_mix_pre, w_in=w_in, conv_dw_w=conv_dw_w, conv_dw_b=conv_dw_b, conv_ln_g=conv_ln_g,
                   conv_ln_b=conv_ln_b, rel_bias=rel_bias, w_out=w_out, norm_mix_post=norm_mix_post,
                   norm_ffn_pre=norm_ffn_pre, w_up=w_up, ffn_dw_w=ffn_dw_w, ffn_dw_b=ffn_dw_b, w_down=w_down,
                   norm_ffn_post=norm_ffn_post)
    mom1 = dict(norm_mix_pre=m_norm_mix_pre, w_in=m_w_in, conv_dw_w=m_conv_dw_w, conv_dw_b=m_conv_dw_b,
                conv_ln_g=m_conv_ln_g, conv_ln_b=m_conv_ln_b, rel_bias=m_rel_bias, w_out=m_w_out,
                norm_mix_post=m_norm_mix_post, norm_ffn_pre=m_norm_ffn_pre, w_up=m_w_up, ffn_dw_w=m_ffn_dw_w,
                ffn_dw_b=m_ffn_dw_b, w_down=m_w_down, norm_ffn_post=m_norm_ffn_post)
    mom2 = dict(norm_mix_pre=v_norm_mix_pre, w_in=v_w_in, conv_dw_w=v_conv_dw_w, conv_dw_b=v_conv_dw_b,
                conv_ln_g=v_conv_ln_g, conv_ln_b=v_conv_ln_b, rel_bias=v_rel_bias, w_out=v_w_out,
                norm_mix_post=v_norm_mix_post, norm_ffn_pre=v_norm_ffn_pre, w_up=v_w_up, ffn_dw_w=v_ffn_dw_w,
                ffn_dw_b=v_ffn_dw_b, w_down=v_w_down, norm_ffn_post=v_norm_ffn_post)

    x2 = x[0]
    target = loss_target[0]
    t_rows = x2.shape[0]
    d = D_MODEL
    in_cols = 2 * CONV_WIDTH + 3 * ATTN_WIDTH
    my_x, my_y, my_c = _mesh_position()
    my_dev = 4 * my_x + 2 * my_y + my_c

    small_conv = _pack([conv_dw_w[0], ffn_dw_w[0]], 32)
    me = jnp.reshape(my_dev, (1,)).astype(jnp.int32)
    first_shards = [w_in[0].T.astype(BF16), small_conv]
    late_shards = [w_out[0].astype(BF16), w_up[0].T.astype(BF16), w_down[0].astype(BF16)]
    placed = _place_own(first_shards + late_shards, me)
    win_t, conv_g = _all_gather(first_shards, "all_gather_weights", placed=placed[:2])
    wout_gather = _scatter_start(late_shards[:1], placed[2:3], True, "gather_w_out_start", after=win_t)
    ffn_gather = _scatter_start(late_shards[1:], placed[3:], True, "gather_ffn_weights_start", after=wout_gather[4])
    late_token = ffn_gather[4]
    win_t = win_t.reshape(in_cols, d)
    conv_flat = conv_g.reshape(N_DEV, 32 * LANES)
    n_cw = CONV_K * (CONV_WIDTH // N_DEV)
    conv_w_full = conv_flat[:, :n_cw].reshape(N_DEV, CONV_K, CONV_WIDTH // N_DEV).transpose(1, 0, 2).reshape(CONV_K, CONV_WIDTH)
    ffn_w_full = conv_flat[:, n_cw:].reshape(N_DEV, FFN_K, 2 * D_FF // N_DEV).transpose(1, 0, 2).reshape(FFN_K, 2 * D_FF)

    u1 = _pre_norm(x2, norm_mix_pre + late_token[0:1, 0:1], "pre_norm_mix")
    proj_a = _matmul(u1, win_t, mode="nt", m=t_rows, n=2 * CONV_WIDTH, k=d, tm=2048, tn=1024, tk=d,
                     out_dtype=F32, name="proj_conv")
    qkv_t = _matmul(win_t, u1, mode="nt", m=3 * ATTN_WIDTH, n=t_rows, k=d, tm=512, tn=2048, tk=d,
                    out_dtype=BF16, name="proj_qkv", a_m0=2 * CONV_WIDTH)
    conv_c, conv_out = _conv_forward(proj_a, conv_w_full, conv_dw_b, conv_ln_g, conv_ln_b)
    o_t, attn_lse = _attn_forward(qkv_t, rel_bias[0])
    _, (wout_g,) = _scatter_wait(*wout_gather[:4], o_t, True, "gather_w_out_wait")
    wout_g = wout_g.reshape(d, d)
    mixed, h1, u2 = _matmul_rows(
        [(conv_out, "nn", CONV_WIDTH, 0), (o_t, "tn", ATTN_WIDTH, CONV_WIDTH)], wout_g, m=t_rows, n=d, tm=1024,
        name="out_proj_mid_forward", row_ins=[x2], vec_ins=[norm_mix_post, norm_ffn_pre], row_outs=[F32, F32, BF16],
        acc_outs=[], epilogue=_mid_forward_epilogue)
    _, (wup_t, wdown_g) = _scatter_wait(*ffn_gather[:4], u2, True, "gather_ffn_weights_wait")
    wup_t = wup_t.reshape(2 * D_FF, d)
    wdown_g = wdown_g.reshape(D_FF, d)
    hup = _matmul(u2, wup_t, mode="nt", m=t_rows, n=2 * D_FF, k=d, tm=2048, tn=1408, tk=d,
                  out_dtype=F32, name="ffn_up")
    act, ffn_gel, ffn_slope = _ffn_activation(hup, ffn_w_full, ffn_dw_b)
    dy, df, _, d_norm_ffn_post, loss = _matmul_rows(
        [(act, "nn", D_FF, 0)], wdown_g, m=t_rows, n=d, tm=1024, name="ffn_down_loss_backward",
        row_ins=[h1, target], vec_ins=[norm_ffn_post], row_outs=[F32, BF16], acc_outs=[(1, d), (1, d), (1, 1)],
        epilogue=_loss_epilogue)

    dact = _matmul(df, wdown_g, mode="nt", m=t_rows, n=D_FF, k=d, tm=2048, tn=1408, tk=d,
                   out_dtype=F32, name="ffn_down_dx")
    g_wdown = _matmul(act, df, mode="tn", m=D_FF, n=d, k=t_rows, tm=1408, tn=1024, tk=2048,
                      out_dtype=F32, name="ffn_down_dw")
    dhg, dhv, dwg, dwv, dbg, dbv = _ffn_backward(dact, ffn_gel, ffn_slope, hup, ffn_w_full)
    g_wup_t = _matmul(dhg, u2, mode="tn", m=D_FF, n=d, k=t_rows, tm=1408, tn=1024, tk=2048, out_dtype=F32,
                      name="ffn_up_dw_gate", out_rows=2 * D_FF)
    g_wup_t = _matmul(dhv, u2, mode="tn", m=D_FF, n=d, k=t_rows, tm=1408, tn=1024, tk=2048, out_dtype=F32,
                      name="ffn_up_dw_value", out_rows=2 * D_FF, out_m0=D_FF, into=g_wup_t)
    ffn_grads = [g_wup_t.reshape(N_DEV, 2 * D_FF // N_DEV, d), g_wdown.reshape(N_DEV, D_FF // N_DEV, d)]
    red_send, red_recv, ffn_grads, red_lands, red_token = _scatter_start(
        ffn_grads, [lax.empty(g.shape, F32) for g in ffn_grads], False, "reduce_ffn_grads_start")
    dh1, dmixed, d_norm_ffn_pre, d_norm_mix_post = _matmul_rows(
        [(dhg, "nn", D_FF, 0), (dhv, "nn", D_FF, D_FF)], wup_t, m=t_rows, n=d, tm=512, name="ffn_up_dx_mid_backward",
        row_ins=[dy, h1, mixed], vec_ins=[norm_ffn_pre + red_token[0:1, 0:1], norm_mix_post], row_outs=[F32, BF16],
        acc_outs=[(1, d), (1, d)], epilogue=_mid_backward_epilogue)
    dconv_out = _matmul(dmixed, wout_g, mode="nt", m=t_rows, n=CONV_WIDTH, k=d, tm=2048, tn=512, tk=d,
                        out_dtype=F32, name="out_proj_dx_conv")
    do_t = _matmul(wout_g, dmixed, mode="nt", m=ATTN_WIDTH, n=t_rows, k=d, tm=512, tn=2048, tk=d,
                   out_dtype=BF16, name="out_proj_dx_attn", a_m0=CONV_WIDTH)
    g_wout = _matmul(conv_out, dmixed, mode="tn", m=CONV_WIDTH, n=d, k=t_rows, tm=512, tn=1024, tk=2048, out_dtype=F32,
                     name="out_proj_dw_conv", out_rows=d)
    g_wout = _matmul(o_t, dmixed, mode="nn", m=ATTN_WIDTH, n=d, k=t_rows, tm=512, tn=1024, tk=2048, out_dtype=F32,
                     name="out_proj_dw_attn", out_rows=d, out_m0=CONV_WIDTH, into=g_wout)
    wout_handle = _scatter_start([g_wout.reshape(N_DEV, d // N_DEV, d)], [lax.empty((N_DEV, d // N_DEV, d), F32)],
                                 False, "reduce_w_out_grad_start")
    dproj_a, d_conv_w, d_conv_b, d_ln_g, d_ln_b = _conv_backward(
        dconv_out, conv_c, proj_a, conv_w_full, conv_ln_g + wout_handle[4][0:1, 0:1], conv_ln_b)
    dqkv_parts = _attn_backward(qkv_t, o_t, do_t, attn_lse, rel_bias[0])
    drel = dqkv_parts[3]
    g_win_t = _matmul(dproj_a, u1, mode="tn", m=2 * CONV_WIDTH, n=d, k=t_rows, tm=1024, tn=1024, tk=2048, out_dtype=BF16,
                      name="proj_dw_conv", out_rows=in_cols)
    for j, part in enumerate("qkv"):
        row0 = 2 * CONV_WIDTH + j * ATTN_WIDTH
        g_win_t = _matmul(dqkv_parts[j], u1, mode="nn", m=ATTN_WIDTH, n=d, k=t_rows, tm=512, tn=1024, tk=2048,
                          out_dtype=BF16, name="proj_dw_" + part, out_rows=in_cols, out_m0=row0, into=g_win_t)
    win_handle = _scatter_start([g_win_t.reshape(N_DEV, in_cols // N_DEV, d)],
                                [lax.empty((N_DEV, in_cols // N_DEV, d), BF16)], False, "reduce_w_in_grad_start")
    dx, d_norm_mix_pre = _matmul_rows(
        [(dproj_a, "nn", 2 * CONV_WIDTH, 0)]
        + [(dqkv_parts[j], "tn", ATTN_WIDTH, 2 * CONV_WIDTH + j * ATTN_WIDTH) for j in range(3)],
        win_t, m=t_rows, n=d, tm=1024, name="proj_dx_input_backward", row_ins=[dh1, x2],
        vec_ins=[norm_mix_pre + win_handle[4][0:1, 0:1]], row_outs=[F32], acc_outs=[(1, d)],
        epilogue=_input_backward_epilogue)

    small_grads = dict(norm_mix_pre=d_norm_mix_pre, conv_dw_b=d_conv_b, conv_ln_g=d_ln_g, conv_ln_b=d_ln_b,
                       rel_bias=drel[:, :2 * MAX_REL + 1], norm_mix_post=d_norm_mix_post, norm_ffn_pre=d_norm_ffn_pre,
                       ffn_dw_b=jnp.concatenate([dbg, dbv], axis=1), norm_ffn_post=d_norm_ffn_post)
    pieces = [small_grads[nm] for nm in SMALL] + [d_conv_w, jnp.concatenate([dwg, dwv], axis=1), loss]
    count = sum(p.size for p in pieces)
    (gathered_small,) = _all_gather([_pack(pieces, _rows_for(count))], "all_gather_small_grads")
    summed = _sum_devices(gathered_small)
    shapes = [weights[nm].shape for nm in SMALL] + [(CONV_K, CONV_WIDTH), (FFN_K, 2 * D_FF), (1, 1)]
    unpacked = _unpack(summed, shapes)
    grads = dict(zip(SMALL, unpacked[:len(SMALL)]))
    cw_shard, fw_shard = CONV_WIDTH // N_DEV, 2 * D_FF // N_DEV
    grads["conv_dw_w"] = lax.dynamic_slice_in_dim(unpacked[-3], my_dev * cw_shard, cw_shard, axis=1)[None]
    grads["ffn_dw_w"] = lax.dynamic_slice_in_dim(unpacked[-2], my_dev * fw_shard, fw_shard, axis=1)[None]
    total_loss = unpacked[-1].reshape(())

    me = jnp.reshape(my_dev, (1,)).astype(jnp.int32)
    delta, new_m, new_v = {}, {}, {}

    def finish(nm, send, recv, srcs, lands, after, transposed):
        srcs, lands = _scatter_wait(send, recv, srcs, lands, after, False, "reduce_" + nm + "_grad_wait")
        for name_a, src, land in zip(nm.split("_and_"), srcs, lands):
            flip = (lambda t: t.T) if transposed[name_a] else (lambda t: t)
            outs = _reduce_and_adamw(src, land, me, flip(weights[name_a][0]), flip(mom1[name_a][0]),
                                     flip(mom2[name_a][0]), "reduce_adamw_" + name_a)
            for store, arr in zip((grads, delta, new_m, new_v), outs):
                store[name_a] = flip(arr)[None]

    transposed = dict(w_in=True, w_out=False, w_up=True, w_down=False)
    finish("w_up_and_w_down", red_send, red_recv, ffn_grads, red_lands, dx, transposed)
    finish("w_out", *wout_handle[:4], dx, transposed)
    finish("w_in", *win_handle[:4], delta["w_up"], transposed)
    small_names = SMALL + SHARDED_SMALL
    small_count = sum(weights[nm].size for nm in small_names)
    small_rows = _rows_for(small_count)
    packed = [_pack([src[nm] for nm in small_names], small_rows) for src in (weights, grads, mom1, mom2)]
    outs = _adamw(*packed, "adamw_small")
    small_shapes = [weights[nm].shape for nm in small_names]
    for store, arr in zip((delta, new_m, new_v), outs):
        store.update(zip(small_names, _unpack(arr, small_shapes)))

    return (total_loss, dx[None], *[grads[nm] for nm in WEIGHTS], *[delta[nm] for nm in WEIGHTS],
            *[new_m[nm] for nm in WEIGHTS], *[new_v[nm] for nm in WEIGHTS])
```

```python
import jax
import jax.numpy as jnp
from jax import lax
from jax.experimental import pallas as pl
from jax.experimental.pallas import tpu as pltpu

F32 = jnp.float32
BF16 = jnp.bfloat16
MESH = pl.DeviceIdType.MESH
N_DEV = 8

EPS = 1e-6
NEG_INF = -1e30
D_MODEL = 1024
CONV_WIDTH = 512
ATTN_WIDTH = 512
N_HEADS = 8
HEAD_DIM = 64
CHUNK = 64
LEFT = 8 * CHUNK
QBLK = 2 * CHUNK
WIN = LEFT + QBLK
CONV_K = 31
CONV_HALO = 32
FFN_K = 3
FFN_HALO = 8
D_FF = 2816
MAX_REL = 128
SCALE = HEAD_DIM ** -0.5
ADAM_LR, ADAM_B1, ADAM_B2, ADAM_EPS, ADAM_WD, ADAM_STEP = 0.001, 0.9, 0.999, 1e-08, 0.01, 10

V7X_VMEM_BYTES = 64 * 2**20
VMEM_LIMIT_BYTES = V7X_VMEM_BYTES - 8 * 2**20
LANES = 128


def _params(*sem):
    return pltpu.CompilerParams(dimension_semantics=sem or None, vmem_limit_bytes=VMEM_LIMIT_BYTES)


_DOT_DIMS = {"nn": (((1,), (0,)), ((), ())), "nt": (((1,), (1,)), ((), ())), "tn": (((0,), (0,)), ((), ()))}


def _matmul(a, b, *, mode, m, n, k, tm, tn, tk, out_dtype, name, a_m0=0, b_n0=0, b_k0=0, add=None,
            out_rows=None, out_m0=0, into=None):
    tm, tn, tk = min(tm, m), min(tn, n), min(tk, k)
    out_rows = m if out_rows is None else out_rows
    assert m % tm == 0 and n % tn == 0 and k % tk == 0, (name, m, n, k, tm, tn, tk)
    assert a_m0 % tm == 0 and b_n0 % tn == 0 and b_k0 % tk == 0 and out_m0 % tm == 0, name
    am, bn, bk, om = a_m0 // tm, b_n0 // tn, b_k0 // tk, out_m0 // tm
    gk = k // tk
    dims = _DOT_DIMS[mode]

    if mode == "tn":
        a_spec = pl.BlockSpec((tk, tm), lambda i, j, kk: (kk, i + am))
    else:
        a_spec = pl.BlockSpec((tm, tk), lambda i, j, kk: (i + am, kk))
    if mode == "nt":
        b_spec = pl.BlockSpec((tn, tk), lambda i, j, kk: (j + bn, kk + bk))
    else:
        b_spec = pl.BlockSpec((tk, tn), lambda i, j, kk: (kk + bk, j + bn))
    o_spec = pl.BlockSpec((tm, tn), lambda i, j, kk: (i + om, j))
    in_specs = [a_spec, b_spec]
    operands = [a, b]
    if add is not None:
        assert out_rows == m
        in_specs.append(o_spec)
        operands.append(add)
    aliases = {}
    if into is not None:
        aliases = {len(operands): 0}
        in_specs.append(pl.BlockSpec(memory_space=pl.ANY))
        operands.append(into)

    def body(*refs):
        a_ref, b_ref = refs[0], refs[1]
        add_ref = refs[2] if add is not None else None
        o_ref = refs[len(operands)]
        part = lax.dot_general(a_ref[...].astype(BF16), b_ref[...].astype(BF16), dims,
                               preferred_element_type=F32)

        def finish(total):
            if add_ref is not None:
                total = total + add_ref[...]
            o_ref[...] = total.astype(out_dtype)

        if gk == 1:
            finish(part)
        else:
            acc_ref = refs[-1]
            kk = pl.program_id(2)

            @pl.when(kk == 0)
            def _():
                acc_ref[...] = part

            @pl.when(kk > 0)
            def _():
                acc_ref[...] += part

            @pl.when(kk == gk - 1)
            def _():
                finish(acc_ref[...])

    return pl.pallas_call(
        body, name=name,
        grid=(m // tm, n // tn, gk),
        in_specs=in_specs, out_specs=o_spec,
        out_shape=jax.ShapeDtypeStruct((out_rows, n), out_dtype),
        scratch_shapes=[pltpu.VMEM((tm, tn), F32)] if gk > 1 else [],
        input_output_aliases=aliases,
        compiler_params=_params("parallel", "parallel", "arbitrary"),
    )(*operands)


def _matmul_rows(pieces, b, *, m, n, tm, name, row_ins, vec_ins, row_outs, acc_outs, epilogue):
    tm = min(tm, m)
    assert m % tm == 0
    steps = m // tm
    in_specs, operands = [], []
    for a, mode, k, k0 in pieces:
        assert k0 % k == 0
        if mode == "tn":
            in_specs.append(pl.BlockSpec((k, tm), lambda i: (0, i)))
        else:
            in_specs.append(pl.BlockSpec((tm, k), lambda i: (i, 0)))
        in_specs.append(pl.BlockSpec((k, n), lambda i, blk=k0 // k: (blk, 0)))
        operands += [a, b]
    row = pl.BlockSpec((tm, n), lambda i: (i, 0))
    in_specs += [row] * len(row_ins) + [pl.BlockSpec((1, n), lambda i: (0, 0))] * len(vec_ins)
    operands += [*row_ins, *vec_ins]
    n_in = len(operands)

    def body(*refs):
        total = None
        for p, (_, mode, _, _) in enumerate(pieces):
            part = lax.dot_general(refs[2 * p][...], refs[2 * p + 1][...], _DOT_DIMS[mode], preferred_element_type=F32)
            total = part if total is None else total + part
        first = 2 * len(pieces)
        rows = refs[first:first + len(row_ins)]
        vecs = refs[first + len(row_ins):n_in]
        outs = refs[n_in:n_in + len(row_outs)]
        accs = refs[n_in + len(row_outs):]
        epilogue(total, rows, vecs, outs, accs, pl.program_id(0), steps)

    return pl.pallas_call(
        body, name=name, grid=(steps,),
        in_specs=in_specs,
        out_specs=[row] * len(row_outs) + [pl.BlockSpec(s, lambda i: (0, 0)) for s in acc_outs],
        out_shape=[jax.ShapeDtypeStruct((m, n), dt) for dt in row_outs]
        + [jax.ShapeDtypeStruct(s, F32) for s in acc_outs],
        compiler_params=_params("arbitrary" if acc_outs else "parallel"),
    )(*operands)


def _rms_hat(v):
    r = lax.rsqrt(jnp.mean(v * v, axis=-1, keepdims=True) + EPS)
    return v * r, r


def _rms_bwd(dn, hat, r):
    return r * (dn - hat * jnp.mean(dn * hat, axis=-1, keepdims=True))


def _sigmoid(v):
    return 1.0 / (1.0 + jnp.exp(-v))


_GELU_C = 0.7978845608028654


def _gelu(v):
    return 0.5 * v * (1.0 + jnp.tanh(_GELU_C * (v + 0.044715 * v * (v * v))))


def _gelu_parts(v):
    v2 = v * v
    t = jnp.tanh(_GELU_C * (v + 0.044715 * v * v2))
    cdf = 0.5 * (1.0 + t)
    dcdf = 0.5 * (1.0 - t * t) * _GELU_C * (1.0 + 3.0 * 0.044715 * v2)
    return v * cdf, cdf + v * dcdf


def _row_tile(t_rows, want):
    tile = min(want, t_rows)
    assert t_rows % tile == 0
    return tile


def _pre_norm_proj(x, g, w_t, n, name):
    t_rows, d = x.shape
    tm = _row_tile(t_rows, 1024)

    def body(x_ref, g_ref, w_ref, u_ref, o_ref):
        hat, _ = _rms_hat(x_ref[...])
        u = (hat * g_ref[...]).astype(BF16)
        u_ref[...] = u
        o_ref[...] = lax.dot_general(u, w_ref[...], _DOT_DIMS["nt"], preferred_element_type=F32)

    return pl.pallas_call(
        body, name=name, grid=(t_rows // tm,),
        in_specs=[pl.BlockSpec((tm, d), lambda i: (i, 0)), pl.BlockSpec((1, d), lambda i: (0, 0)),
                  pl.BlockSpec((n, d), lambda i: (0, 0))],
        out_specs=[pl.BlockSpec((tm, d), lambda i: (i, 0)), pl.BlockSpec((tm, n), lambda i: (i, 0))],
        out_shape=[jax.ShapeDtypeStruct((t_rows, d), BF16), jax.ShapeDtypeStruct((t_rows, n), F32)],
        compiler_params=_params("parallel"),
    )(x, g, w_t)


def _zero_at_start(accs, step):
    @pl.when(step == 0)
    def _():
        for acc in accs:
            acc[...] = jnp.zeros_like(acc)


def _mid_forward_epilogue(mixed, rows, vecs, outs, accs, step, steps):
    (x_ref,), (gpost_ref, gpre_ref), (mixed_ref, h1_ref, u2_ref) = rows, vecs, outs
    mixed_ref[...] = mixed
    hat, _ = _rms_hat(mixed)
    h1 = x_ref[...] + hat * gpost_ref[...]
    h1_ref[...] = h1
    hat1, _ = _rms_hat(h1)
    u2_ref[...] = (hat1 * gpre_ref[...]).astype(BF16)


def _loss_epilogue(f, rows, vecs, outs, accs, step, steps):
    (h1_ref, tgt_ref), (g_ref,), (dy_ref, df_ref), (sq_ref, dg_ref, loss_ref) = rows, vecs, outs, accs
    _zero_at_start(accs, step)
    g = g_ref[...]
    d = f.shape[-1]
    hat, r = _rms_hat(f)
    err = h1_ref[...] + hat * g - tgt_ref[...]
    sq_ref[...] += jnp.sum(err * err, axis=0, keepdims=True)
    dy = err * (1.0 / d)
    dy_ref[...] = dy
    dg_ref[...] += jnp.sum(dy * hat, axis=0, keepdims=True)
    df_ref[...] = _rms_bwd(dy * g, hat, r).astype(BF16)

    @pl.when(step == steps - 1)
    def _():
        loss_ref[...] = (0.5 / d) * jnp.sum(sq_ref[...], axis=1, keepdims=True)


def _mid_backward_epilogue(du2, rows, vecs, outs, accs, step, steps):
    (dy_ref, h1_ref, mixed_ref), (gpre_ref, gpost_ref), (dh1_ref, dmixed_ref), (dgpre_ref, dgpost_ref) = rows, vecs, outs, accs
    _zero_at_start(accs, step)
    hat1, r1 = _rms_hat(h1_ref[...])
    dgpre_ref[...] += jnp.sum(du2 * hat1, axis=0, keepdims=True)
    dh1 = dy_ref[...] + _rms_bwd(du2 * gpre_ref[...], hat1, r1)
    dh1_ref[...] = dh1
    hatm, rm = _rms_hat(mixed_ref[...])
    dgpost_ref[...] += jnp.sum(dh1 * hatm, axis=0, keepdims=True)
    dmixed_ref[...] = _rms_bwd(dh1 * gpost_ref[...], hatm, rm).astype(BF16)


def _input_backward_epilogue(du1, rows, vecs, outs, accs, step, steps):
    (dh1_ref, x_ref), (g_ref,), (dx_ref,), (dg_ref,) = rows, vecs, outs, accs
    _zero_at_start(accs, step)
    hat, r = _rms_hat(x_ref[...])
    dg_ref[...] += jnp.sum(du1 * hat, axis=0, keepdims=True)
    dx_ref[...] = dh1_ref[...] + _rms_bwd(du1 * g_ref[...], hat, r)


CONV_STRIP = 32


def _glu(block):
    return block[:, :CONV_WIDTH] * _sigmoid(block[:, CONV_WIDTH:])


def _layer_norm_parts(c):
    mu = jnp.mean(c, axis=-1, keepdims=True)
    xc = c - mu
    r = lax.rsqrt(jnp.mean(xc * xc, axis=-1, keepdims=True) + EPS)
    return xc * r, r


CONV_WINDOW = 2 * CONV_STRIP
SHIFTED_ROWS = CONV_WINDOW - 8


def _shifted_copies(v, shifted):
    for s in range(1, 8):
        shifted[s] = v[s:s + SHIFTED_ROWS, :]


def _window_rows(v, shifted, start):
    s, a = start % 8, start - start % 8
    return v[a:a + CONV_STRIP, :] if s == 0 else shifted[s, a:a + CONV_STRIP, :]


def _conv_forward(proj_a, w, b, ln_g, ln_b):
    t_rows = proj_a.shape[0]
    tm = _row_tile(t_rows, 512)
    hb = tm // CONV_HALO
    cw = CONV_WIDTH

    def body(cur_ref, prev_ref, w_ref, b_ref, g_ref, beta_ref, c_ref, out_ref, hbuf, shifted):
        i = pl.program_id(0)
        hbuf[0:CONV_HALO, :] = jnp.where(i > 0, _glu(prev_ref[...]), 0.0)
        hbuf[CONV_HALO:, :] = _glu(cur_ref[...])

        def strip(s, carry):
            base = pl.multiple_of(s * CONV_STRIP, CONV_STRIP)
            v = hbuf[pl.ds(base, CONV_WINDOW), :]
            _shifted_copies(v, shifted)
            acc = jnp.broadcast_to(b_ref[...], (CONV_STRIP, cw))
            off = CONV_HALO - (CONV_K - 1)
            for kk in range(CONV_K):
                acc = acc + w_ref[kk:kk + 1, :] * _window_rows(v, shifted, off + kk)
            c_ref[pl.ds(base, CONV_STRIP), :] = acc
            hat, _ = _layer_norm_parts(acc)
            z = hat * g_ref[...] + beta_ref[...]
            out_ref[pl.ds(base, CONV_STRIP), :] = (z * _sigmoid(z)).astype(BF16)
            return carry

        lax.fori_loop(0, tm // CONV_STRIP, strip, 0)

    vec = pl.BlockSpec((1, cw), lambda i: (0, 0))
    return pl.pallas_call(
        body, name="conv_forward", grid=(t_rows // tm,),
        in_specs=[pl.BlockSpec((tm, 2 * cw), lambda i: (i, 0)),
                  pl.BlockSpec((CONV_HALO, 2 * cw), lambda i: (jnp.maximum(i * hb - 1, 0), 0)),
                  pl.BlockSpec((CONV_K, cw), lambda i: (0, 0)), vec, vec, vec],
        out_specs=[pl.BlockSpec((tm, cw), lambda i: (i, 0)), pl.BlockSpec((tm, cw), lambda i: (i, 0))],
        out_shape=[jax.ShapeDtypeStruct((t_rows, cw), F32), jax.ShapeDtypeStruct((t_rows, cw), BF16)],
        scratch_shapes=[pltpu.VMEM((tm + CONV_HALO, cw), F32), pltpu.VMEM((8, SHIFTED_ROWS, cw), F32)],
        compiler_params=_params("parallel"),
    )(proj_a, proj_a, w, b, ln_g, ln_b)


def _conv_backward(dout, c, proj_a, w, ln_g, ln_b):
    t_rows = c.shape[0]
    tm = _row_tile(t_rows, 512)
    hb = tm // CONV_HALO
    nt = t_rows // tm
    last_halo = t_rows // CONV_HALO - 1
    cw = CONV_WIDTH

    def body(dout_ref, dout_next_ref, c_ref, c_next_ref, cur_ref, prev_ref, w_ref, g_ref, beta_ref,
             dproj_ref, dw_ref, db_ref, dg_ref, dbeta_ref, hbuf, dcbuf, dwacc, h_shifted, d_shifted):
        i = pl.program_id(0)

        @pl.when(i == 0)
        def _():
            dwacc[...] = jnp.zeros_like(dwacc)
            db_ref[...] = jnp.zeros_like(db_ref)
            dg_ref[...] = jnp.zeros_like(dg_ref)
            dbeta_ref[...] = jnp.zeros_like(dbeta_ref)

        def ln_swish_backward(dout_v, c_v):
            hat, r = _layer_norm_parts(c_v)
            g = g_ref[...]
            z = hat * g + beta_ref[...]
            sg = _sigmoid(z)
            dz = dout_v * (sg * (1.0 + z * (1.0 - sg)))
            dhat = dz * g
            dc = r * (dhat - jnp.mean(dhat, axis=-1, keepdims=True)
                      - hat * jnp.mean(dhat * hat, axis=-1, keepdims=True))
            return dc, dz, hat

        dc, dz, hat = ln_swish_backward(dout_ref[...], c_ref[...])
        dg_ref[...] += jnp.sum(dz * hat, axis=0, keepdims=True)
        dbeta_ref[...] += jnp.sum(dz, axis=0, keepdims=True)
        db_ref[...] += jnp.sum(dc, axis=0, keepdims=True)
        dcbuf[0:tm, :] = dc
        dc_next, _, _ = ln_swish_backward(dout_next_ref[...], c_next_ref[...])
        dcbuf[tm:, :] = jnp.where(i < nt - 1, dc_next, 0.0)

        hbuf[0:CONV_HALO, :] = jnp.where(i > 0, _glu(prev_ref[...]), 0.0)
        hbuf[CONV_HALO:, :] = _glu(cur_ref[...])

        def strip(s, carry):
            base = pl.multiple_of(s * CONV_STRIP, CONV_STRIP)
            dv = dcbuf[pl.ds(base, CONV_WINDOW), :]
            hv = hbuf[pl.ds(base, CONV_WINDOW), :]
            _shifted_copies(dv, d_shifted)
            _shifted_copies(hv, h_shifted)
            dcs = dv[0:CONV_STRIP, :]
            dh = jnp.zeros((CONV_STRIP, cw), F32)
            off = CONV_HALO - (CONV_K - 1)
            for kk in range(CONV_K):
                back = CONV_K - 1 - kk
                dh = dh + w_ref[kk:kk + 1, :] * _window_rows(dv, d_shifted, back)
                prod = dcs * _window_rows(hv, h_shifted, off + kk)
                dwacc[kk] += jnp.sum(prod.reshape(CONV_STRIP // 8, 8, cw), axis=0)
            blk = cur_ref[pl.ds(base, CONV_STRIP), :]
            val, sg = blk[:, :cw], _sigmoid(blk[:, cw:])
            dproj_ref[pl.ds(base, CONV_STRIP), 0:cw] = (dh * sg).astype(BF16)
            dproj_ref[pl.ds(base, CONV_STRIP), cw:2 * cw] = (dh * val * sg * (1.0 - sg)).astype(BF16)
            return carry

        lax.fori_loop(0, tm // CONV_STRIP, strip, 0)

        @pl.when(i == nt - 1)
        def _():
            for kk in range(CONV_K):
                dw_ref[kk:kk + 1, :] = jnp.sum(dwacc[kk], axis=0, keepdims=True)

    vec = pl.BlockSpec((1, cw), lambda i: (0, 0))
    cur = lambda width: pl.BlockSpec((tm, width), lambda i: (i, 0))
    nxt = lambda width: pl.BlockSpec((CONV_HALO, width), lambda i: (jnp.minimum((i + 1) * hb, last_halo), 0))
    return pl.pallas_call(
        body, name="conv_backward", grid=(nt,),
        in_specs=[cur(cw), nxt(cw), cur(cw), nxt(cw), cur(2 * cw),
                  pl.BlockSpec((CONV_HALO, 2 * cw), lambda i: (jnp.maximum(i * hb - 1, 0), 0)),
                  pl.BlockSpec((CONV_K, cw), lambda i: (0, 0)), vec, vec],
        out_specs=[cur(2 * cw), pl.BlockSpec((CONV_K, cw), lambda i: (0, 0)), vec, vec, vec],
        out_shape=[jax.ShapeDtypeStruct((t_rows, 2 * cw), BF16), jax.ShapeDtypeStruct((CONV_K, cw), F32),
                   jax.ShapeDtypeStruct((1, cw), F32), jax.ShapeDtypeStruct((1, cw), F32),
                   jax.ShapeDtypeStruct((1, cw), F32)],
        scratch_shapes=[pltpu.VMEM((tm + CONV_HALO, cw), F32), pltpu.VMEM((tm + CONV_HALO, cw), F32),
                        pltpu.VMEM((CONV_K, 8, cw), F32), pltpu.VMEM((8, SHIFTED_ROWS, cw), F32),
                        pltpu.VMEM((8, SHIFTED_ROWS, cw), F32)],
        compiler_params=_params("arbitrary"),
    )(dout, dout, c, c, proj_a, proj_a, w, ln_g, ln_b)


def _attn_load_kv(kv_hbm, k_pad, v_pad, sem, t_cols):
    k_pad[:, 0:LEFT] = jnp.zeros((ATTN_WIDTH, LEFT), BF16)
    v_pad[:, 0:LEFT] = jnp.zeros((ATTN_WIDTH, LEFT), BF16)
    ck = pltpu.make_async_copy(kv_hbm.at[pl.ds(ATTN_WIDTH, ATTN_WIDTH), :], k_pad.at[:, pl.ds(LEFT, t_cols)], sem.at[0])
    cv = pltpu.make_async_copy(kv_hbm.at[pl.ds(2 * ATTN_WIDTH, ATTN_WIDTH), :], v_pad.at[:, pl.ds(LEFT, t_cols)], sem.at[1])
    ck.start()
    cv.start()
    ck.wait()
    cv.wait()


def _attn_build_bias(tab_ref, bias_t):
    row = lax.broadcasted_iota(jnp.int32, (LANES, LANES), 0)
    lane = lax.broadcasted_iota(jnp.int32, (LANES, LANES), 1)
    upper = lane >= row
    lane64 = lax.broadcasted_iota(jnp.int32, (CHUNK, LANES), 1)
    for h in range(N_HEADS):
        far = jnp.broadcast_to(tab_ref[h:h + 1, 2 * MAX_REL:2 * MAX_REL + 1], (LANES, LANES))
        hi = jnp.broadcast_to(tab_ref[h:h + 1, MAX_REL:2 * MAX_REL], (LANES, LANES))
        lo = jnp.broadcast_to(tab_ref[h:h + 1, 0:MAX_REL], (LANES, LANES))
        hi_d = pltpu.roll(hi, 0, 1, stride=1, stride_axis=0)
        lo_d = pltpu.roll(lo, 0, 1, stride=1, stride_axis=0)
        bias_t[h, 0:WIN - 2 * LANES, :] = jnp.broadcast_to(far[0:1, :], (WIN - 2 * LANES, LANES))
        bias_t[h, WIN - 2 * LANES:WIN - LANES, :] = jnp.where(upper, far, hi_d)
        bias_t[h, WIN - LANES:WIN, :] = jnp.where(upper, hi_d, lo_d)
        bias_t[h, 0:CHUNK, :] = jnp.where(lane64 < CHUNK, bias_t[h, 0:CHUNK, :], NEG_INF)
        bias_t[h, WIN - CHUNK:WIN, :] = jnp.where(lane64 >= CHUNK, bias_t[h, WIN - CHUNK:WIN, :], NEG_INF)


def _head_rows(h):
    return slice(h * HEAD_DIM, (h + 1) * HEAD_DIM)


def _attn_scores(k_pad, q_ref, s_buf, w0):
    for h in range(N_HEADS):
        q_h = q_ref[_head_rows(h), :] * jnp.asarray(SCALE, BF16)
        s_buf[h] = lax.dot_general(k_pad[_head_rows(h), pl.ds(w0, WIN)], q_h, _DOT_DIMS["tn"],
                                   preferred_element_type=F32)


def _attn_logits(s, bias, first_valid, key0=0):
    s = s + bias
    if first_valid is not None:
        s = jnp.where(lax.broadcasted_iota(jnp.int32, s.shape, 0) + key0 >= first_valid, s, NEG_INF)
    return s


def _attn_probs(s, bias_h, first_valid):
    s = _attn_logits(s, bias_h, first_valid)
    top = jnp.max(s, axis=0, keepdims=True)
    e = jnp.exp(s - top)
    total = jnp.sum(e, axis=0, keepdims=True)
    return e * (1.0 / total), top + jnp.log(total)


def _attn_by_padding(m, fn):
    @pl.when(m < LEFT // QBLK)
    def _():
        fn(LEFT - m * QBLK)

    @pl.when(m >= LEFT // QBLK)
    def _():
        fn(None)


def _attn_forward(qkv_t, rel_bias):
    t_cols = qkv_t.shape[1]
    steps = t_cols // QBLK

    def body(q_ref, kv_hbm, tab_ref, o_ref, lse_ref, k_pad, v_pad, bias_t, s_buf, p_buf, sem):
        m = pl.program_id(0)

        @pl.when(m == 0)
        def _():
            _attn_build_bias(tab_ref, bias_t)
            _attn_load_kv(kv_hbm, k_pad, v_pad, sem, t_cols)

        w0 = pl.multiple_of(m * QBLK, QBLK)
        _attn_scores(k_pad, q_ref, s_buf, w0)

        def softmax(first_valid):
            for h in range(N_HEADS):
                p, lse = _attn_probs(s_buf[h], bias_t[h], first_valid)
                p_buf[h] = p.astype(BF16)
                lse_ref[h:h + 1, :] = lse

        _attn_by_padding(m, softmax)
        for h in range(N_HEADS):
            o_h = lax.dot_general(v_pad[_head_rows(h), pl.ds(w0, WIN)], p_buf[h], _DOT_DIMS["nn"],
                                  preferred_element_type=F32)
            o_ref[_head_rows(h), :] = o_h.astype(BF16)

    return pl.pallas_call(
        body, name="attn_forward", grid=(steps,),
        in_specs=[pl.BlockSpec((ATTN_WIDTH, QBLK), lambda m: (0, m)),
                  pl.BlockSpec(memory_space=pl.ANY),
                  pl.BlockSpec((N_HEADS, 2 * MAX_REL + 1), lambda m: (0, 0))],
        out_specs=[pl.BlockSpec((ATTN_WIDTH, QBLK), lambda m: (0, m)), pl.BlockSpec((N_HEADS, QBLK), lambda m: (0, m))],
        out_shape=[jax.ShapeDtypeStruct((ATTN_WIDTH, t_cols), BF16), jax.ShapeDtypeStruct((N_HEADS, t_cols), F32)],
        scratch_shapes=[pltpu.VMEM((ATTN_WIDTH, LEFT + t_cols), BF16), pltpu.VMEM((ATTN_WIDTH, LEFT + t_cols), BF16),
                        pltpu.VMEM((N_HEADS, WIN, QBLK), F32), pltpu.VMEM((N_HEADS, WIN, QBLK), F32),
                        pltpu.VMEM((N_HEADS, WIN, QBLK), BF16), pltpu.SemaphoreType.DMA((2,))],
        compiler_params=_params("arbitrary"),
    )(qkv_t, qkv_t, rel_bias)


def _reverse_lanes(v, flip):
    out = jnp.zeros(v.shape, F32)
    rest = v
    for _ in range(3):
        piece = rest.astype(BF16)
        out = out + lax.dot_general(piece, flip, _DOT_DIMS["nn"], preferred_element_type=F32)
        rest = rest - piece.astype(F32)
    return out


def _attn_bias_grad(dbias_t, drel_ref):
    row = lax.broadcasted_iota(jnp.int32, (LANES, LANES), 0)
    lane = lax.broadcasted_iota(jnp.int32, (LANES, LANES), 1)
    flip = (row + lane == LANES - 1).astype(BF16)
    head = lax.broadcasted_iota(jnp.int32, (N_HEADS, LANES), 0)
    lane8 = lax.broadcasted_iota(jnp.int32, (N_HEADS, LANES), 1)
    upper_rev = jnp.zeros((N_HEADS, LANES), F32)
    lower_rev = jnp.zeros((N_HEADS, LANES), F32)
    far = jnp.zeros((N_HEADS, LANES), F32)
    for h in range(N_HEADS):
        def diagonals(block):
            skew = pltpu.roll(_reverse_lanes(block, flip), 0, 1, stride=1, stride_axis=0)
            pos = jnp.sum(jnp.where(lane >= row, skew, 0.0), axis=0, keepdims=True)
            neg = jnp.sum(jnp.where(lane < row, skew, 0.0), axis=0, keepdims=True)
            return pos, neg

        pos4, neg4 = diagonals(dbias_t[h, WIN - LANES:WIN, :])
        pos3, neg3 = diagonals(dbias_t[h, WIN - 2 * LANES:WIN - LANES, :])
        far_h = jnp.sum(jnp.sum(dbias_t[h, 0:WIN - 2 * LANES, :], axis=0, keepdims=True), axis=1, keepdims=True)
        far_h = far_h + jnp.sum(pos3, axis=1, keepdims=True)
        upper_rev = jnp.where(head == h, pos4 + neg3, upper_rev)
        lower_rev = jnp.where(head == h, neg4, lower_rev)
        far = jnp.where((head == h) & (lane8 == 0), far_h, far)
    drel_ref[:, 0:LANES] = _reverse_lanes(lower_rev, flip)
    drel_ref[:, LANES:2 * LANES] = _reverse_lanes(upper_rev, flip)
    drel_ref[:, 2 * LANES:3 * LANES] = far


def _attn_backward(qkv_t, o_t, do_t, lse, rel_bias):
    t_cols = qkv_t.shape[1]
    steps = t_cols // QBLK
    flush = LEFT // QBLK
    total = steps + flush

    def body(q_ref, o_ref, do_ref, lse_ref, kv_hbm, tab_ref, dq_ref, dk_ref, dv_ref, drel_ref,
             k_pad, v_pad, bias_t, dbias_t, dk_acc, dv_acc, s_buf, dp_buf, p_buf, ds_buf, sem):
        m = pl.program_id(0)

        @pl.when(m == 0)
        def _():
            _attn_build_bias(tab_ref, bias_t)
            _attn_load_kv(kv_hbm, k_pad, v_pad, sem, t_cols)
            dbias_t[...] = jnp.zeros_like(dbias_t)
            dk_acc[...] = jnp.zeros_like(dk_acc)
            dv_acc[...] = jnp.zeros_like(dv_acc)

        @pl.when(m < steps)
        def _():
            w0 = pl.multiple_of(m * QBLK, QBLK)
            _attn_scores(k_pad, q_ref, s_buf, w0)
            for h in range(N_HEADS):
                dp_buf[h] = lax.dot_general(v_pad[_head_rows(h), pl.ds(w0, WIN)], do_ref[_head_rows(h), :],
                                            _DOT_DIMS["tn"], preferred_element_type=F32)

            def softmax_backward(first_valid):
                for h in range(N_HEADS):
                    rows = _head_rows(h)
                    delta = jnp.sum(do_ref[rows, :].astype(F32) * o_ref[rows, :].astype(F32), axis=0, keepdims=True)
                    lse_h = lse_ref[h:h + 1, :]
                    for b in range(WIN // LANES):
                        keys = slice(b * LANES, (b + 1) * LANES)
                        s = _attn_logits(s_buf[h, keys, :], bias_t[h, keys, :], first_valid, b * LANES)
                        p = jnp.exp(s - lse_h)
                        ds = p * (dp_buf[h, keys, :] - delta)
                        dbias_t[h, keys, :] += ds
                        p_buf[h, keys, :] = p.astype(BF16)
                        ds_buf[h, keys, :] = (ds * SCALE).astype(BF16)

            _attn_by_padding(m, softmax_backward)
            for h in range(N_HEADS):
                rows = _head_rows(h)
                dq_h = lax.dot_general(k_pad[rows, pl.ds(w0, WIN)], ds_buf[h], _DOT_DIMS["nn"], preferred_element_type=F32)
                dq_ref[rows, :] = dq_h.astype(BF16)
                dk_h = lax.dot_general(q_ref[rows, :], ds_buf[h], _DOT_DIMS["nt"], preferred_element_type=F32)
                dv_h = lax.dot_general(do_ref[rows, :], p_buf[h], _DOT_DIMS["nt"], preferred_element_type=F32)
                for b in range(WIN // QBLK):
                    slot = pl.multiple_of(lax.rem(m + b, WIN // QBLK) * QBLK, QBLK)
                    dk_acc[rows, pl.ds(slot, QBLK)] += dk_h[:, b * QBLK:(b + 1) * QBLK]
                    dv_acc[rows, pl.ds(slot, QBLK)] += dv_h[:, b * QBLK:(b + 1) * QBLK]

        oldest = pl.multiple_of(lax.rem(m, WIN // QBLK) * QBLK, QBLK)
        dk_ref[...] = dk_acc[:, pl.ds(oldest, QBLK)].astype(BF16)
        dv_ref[...] = dv_acc[:, pl.ds(oldest, QBLK)].astype(BF16)
        dk_acc[:, pl.ds(oldest, QBLK)] = jnp.zeros((ATTN_WIDTH, QBLK), F32)
        dv_acc[:, pl.ds(oldest, QBLK)] = jnp.zeros((ATTN_WIDTH, QBLK), F32)

        @pl.when(m == total - 1)
        def _():
            _attn_bias_grad(dbias_t, drel_ref)

    qblk = pl.BlockSpec((ATTN_WIDTH, QBLK), lambda m: (0, jnp.minimum(m, steps - 1)))
    kblk = pl.BlockSpec((ATTN_WIDTH, QBLK), lambda m: (0, jnp.maximum(m - flush, 0)))
    dq, dk, dv, drel = pl.pallas_call(
        body, name="attn_backward", grid=(total,),
        in_specs=[qblk, qblk, qblk, pl.BlockSpec((N_HEADS, QBLK), lambda m: (0, jnp.minimum(m, steps - 1))),
                  pl.BlockSpec(memory_space=pl.ANY), pl.BlockSpec((N_HEADS, 2 * MAX_REL + 1), lambda m: (0, 0))],
        out_specs=[qblk, kblk, kblk, pl.BlockSpec((N_HEADS, 3 * LANES), lambda m: (0, 0))],
        out_shape=[jax.ShapeDtypeStruct((ATTN_WIDTH, t_cols), BF16)] * 3
        + [jax.ShapeDtypeStruct((N_HEADS, 3 * LANES), F32)],
        scratch_shapes=[pltpu.VMEM((ATTN_WIDTH, LEFT + t_cols), BF16), pltpu.VMEM((ATTN_WIDTH, LEFT + t_cols), BF16),
                        pltpu.VMEM((N_HEADS, WIN, QBLK), F32), pltpu.VMEM((N_HEADS, WIN, QBLK), F32),
                        pltpu.VMEM((ATTN_WIDTH, WIN), F32), pltpu.VMEM((ATTN_WIDTH, WIN), F32),
                        pltpu.VMEM((N_HEADS, WIN, QBLK), F32), pltpu.VMEM((N_HEADS, WIN, QBLK), F32),
                        pltpu.VMEM((N_HEADS, WIN, QBLK), BF16), pltpu.VMEM((N_HEADS, WIN, QBLK), BF16),
                        pltpu.SemaphoreType.DMA((2,))],
        compiler_params=_params("arbitrary"),
    )(qkv_t, o_t, do_t, lse, qkv_t, rel_bias)
    return dq, dk, dv, drel


FFN_TC = D_FF // 2
FFN_TR = 512


def _ffn_specs(t_rows, tr):
    nj = D_FF // FFN_TC
    hb = tr // FFN_HALO
    last_halo = t_rows // FFN_HALO - 1
    cur = lambda off: pl.BlockSpec((tr, FFN_TC), lambda j, i: (i, j + off))
    prev = lambda off: pl.BlockSpec((FFN_HALO, FFN_TC), lambda j, i: (jnp.maximum(i * hb - 1, 0), j + off))
    nxt = lambda off: pl.BlockSpec((FFN_HALO, FFN_TC), lambda j, i: (jnp.minimum((i + 1) * hb, last_halo), j + off))
    wspec = lambda off: pl.BlockSpec((FFN_K, FFN_TC), lambda j, i: (0, j + off))
    bspec = lambda off: pl.BlockSpec((1, FFN_TC), lambda j, i: (0, j + off))
    return nj, cur, prev, nxt, wspec, bspec


FFN_STRIP = 16


def _ffn_conv(win, w, b, rows):
    out = b + w[2] * win[FFN_HALO:FFN_HALO + rows, :]
    out = out + w[1] * win[FFN_HALO - 1:FFN_HALO - 1 + rows, :]
    return out + w[0] * win[FFN_HALO - 2:FFN_HALO - 2 + rows, :]


def _taps(w_ref):
    return [w_ref[kk:kk + 1, :] for kk in range(FFN_K)]


def _ffn_first_window(prev_ref, cur_ref, tile, rows):
    return jnp.concatenate([jnp.where(tile > 0, prev_ref[...], 0.0), cur_ref[0:rows, :]], axis=0)


def _fold8(v):
    return jnp.sum(v.reshape(v.shape[0] // 8, 8, v.shape[1]), axis=0)


def _ffn_activation(hup, w, b):
    t_rows = hup.shape[0]
    tr = _row_tile(t_rows, FFN_TR)
    nj, cur, prev, nxt, wspec, bspec = _ffn_specs(t_rows, tr)
    rs = FFN_STRIP

    def body(g_ref, gprev_ref, v_ref, vprev_ref, wg_ref, wv_ref, bg_ref, bv_ref, act_ref, gel_ref, slope_ref):
        i = pl.program_id(1)
        wg, wv, bg, bv = _taps(wg_ref), _taps(wv_ref), bg_ref[...], bv_ref[...]

        def emit(base, g_win, v_win):
            gel, dgel = _gelu_parts(_ffn_conv(g_win, wg, bg, rs))
            cv = _ffn_conv(v_win, wv, bv, rs)
            act_ref[pl.ds(base, rs), :] = (gel * cv).astype(BF16)
            gel_ref[pl.ds(base, rs), :] = gel
            slope_ref[pl.ds(base, rs), :] = cv * dgel

        def strip(s, carry):
            base = pl.multiple_of(s * rs, rs)
            emit(base, g_ref[pl.ds(base - FFN_HALO, rs + FFN_HALO), :], v_ref[pl.ds(base - FFN_HALO, rs + FFN_HALO), :])
            return carry

        emit(0, _ffn_first_window(gprev_ref, g_ref, i, rs), _ffn_first_window(vprev_ref, v_ref, i, rs))
        lax.fori_loop(1, tr // rs, strip, 0)

    return pl.pallas_call(
        body, name="ffn_activation", grid=(nj, t_rows // tr),
        in_specs=[cur(0), prev(0), cur(nj), prev(nj), wspec(0), wspec(nj), bspec(0), bspec(nj)],
        out_specs=[cur(0), cur(0), cur(0)],
        out_shape=[jax.ShapeDtypeStruct((t_rows, D_FF), BF16), jax.ShapeDtypeStruct((t_rows, D_FF), F32),
                   jax.ShapeDtypeStruct((t_rows, D_FF), F32)],
        compiler_params=_params("parallel", "parallel"),
    )(hup, hup, hup, hup, w, w, b, b)


def _ffn_backward(dact, gel, slope, hup, w):
    t_rows = hup.shape[0]
    tr = _row_tile(t_rows, FFN_TR)
    nj, cur, prev, nxt, wspec, bspec = _ffn_specs(t_rows, tr)
    ni = t_rows // tr
    rs = FFN_STRIP
    ns = tr // rs

    def body(da_ref, danext_ref, gel_ref, gelnext_ref, slope_ref, slopenext_ref, g_ref, gprev_ref, v_ref, vprev_ref,
             wg_ref, wv_ref, dhg_ref, dhv_ref, dwg_ref, dwv_ref, dbg_ref, dbv_ref, sums):
        i = pl.program_id(1)

        @pl.when(i == 0)
        def _():
            sums[...] = jnp.zeros_like(sums)

        wg, wv = _taps(wg_ref), _taps(wv_ref)
        da_after = jnp.where(i < ni - 1, danext_ref[...], 0.0)

        def strip_at(base, g_win, v_win, carry):
            da = da_ref[pl.ds(base, rs), :]
            dcg, dcv = da * slope_ref[pl.ds(base, rs), :], da * gel_ref[pl.ds(base, rs), :]
            out = []
            for half_i, (dc, after, taps, win, dh_ref) in enumerate(((dcg, carry[0], wg, g_win, dhg_ref),
                                                                   (dcv, carry[1], wv, v_win, dhv_ref))):
                ext = jnp.concatenate([dc, after], axis=0)
                dh = taps[2] * dc + taps[1] * ext[1:1 + rs, :] + taps[0] * ext[2:2 + rs, :]
                dh_ref[pl.ds(base, rs), :] = dh.astype(BF16)
                sums[4 * half_i] += _fold8(dc)
                for kk in range(FFN_K):
                    sums[4 * half_i + 1 + kk] += _fold8(dc * win[FFN_HALO - 2 + kk:FFN_HALO - 2 + kk + rs, :])
                out.append(dc[0:FFN_HALO, :])
            return tuple(out)

        def strip(s, carry):
            base = pl.multiple_of((ns - 1 - s) * rs, rs)
            return strip_at(base, g_ref[pl.ds(base - FFN_HALO, rs + FFN_HALO), :],
                            v_ref[pl.ds(base - FFN_HALO, rs + FFN_HALO), :], carry)

        carry = lax.fori_loop(0, ns - 1, strip, (da_after * slopenext_ref[...], da_after * gelnext_ref[...]))
        strip_at(0, _ffn_first_window(gprev_ref, g_ref, i, rs), _ffn_first_window(vprev_ref, v_ref, i, rs), carry)

        @pl.when(i == ni - 1)
        def _():
            for half_i, (db_ref, dw_ref) in enumerate(((dbg_ref, dwg_ref), (dbv_ref, dwv_ref))):
                db_ref[...] = jnp.sum(sums[4 * half_i], axis=0, keepdims=True)
                for kk in range(FFN_K):
                    dw_ref[kk:kk + 1, :] = jnp.sum(sums[4 * half_i + 1 + kk], axis=0, keepdims=True)

    half = jax.ShapeDtypeStruct((t_rows, D_FF), BF16)
    return pl.pallas_call(
        body, name="ffn_backward", grid=(nj, ni),
        in_specs=[cur(0), nxt(0), cur(0), nxt(0), cur(0), nxt(0), cur(0), prev(0), cur(nj), prev(nj),
                  wspec(0), wspec(nj)],
        out_specs=[cur(0), cur(0), wspec(0), wspec(0), bspec(0), bspec(0)],
        out_shape=[half, half, jax.ShapeDtypeStruct((FFN_K, D_FF), F32), jax.ShapeDtypeStruct((FFN_K, D_FF), F32),
                   jax.ShapeDtypeStruct((1, D_FF), F32), jax.ShapeDtypeStruct((1, D_FF), F32)],
        scratch_shapes=[pltpu.VMEM((2 * (1 + FFN_K), 8, FFN_TC), F32)],
        compiler_params=_params("parallel", "arbitrary"),
    )(dact, dact, gel, gel, slope, slope, hup, hup, hup, hup, w, w)


def _mesh_position():
    return lax.axis_index("x"), lax.axis_index("y"), lax.axis_index("c")


def _hbm_specs(n):
    return [pl.BlockSpec(memory_space=pl.ANY)] * n


def _all_gather(shards, name, placed=None):
    n = len(shards)

    def body(*refs):
        ins = refs[:n]
        outs = refs[2 * n:3 * n] if placed else refs[n:2 * n]
        send_sems, recv_sems, local_sems = refs[-3:]
        x, y, c = _mesh_position()
        me, sibling = (x, y, c), (x, y, 1 - c)
        chips = [(1 - x, y), (x, 1 - y), (1 - x, 1 - y)]

        def copy(a, slot, block, to, src=None):
            dst = outs[a].at[4 * block[0] + 2 * block[1] + block[2]]
            return pltpu.make_async_remote_copy(
                src_ref=dst if src is None else src, dst_ref=dst,
                send_sem=send_sems.at[a, slot], recv_sem=recv_sems.at[a, slot],
                device_id=to, device_id_type=MESH)

        started = []
        for a in range(0 if placed else n):
            mine = pltpu.make_async_copy(ins[a], outs[a].at[4 * x + 2 * y + c], local_sems.at[a])
            mine.start()
            started.append(mine)
        first = []
        for a in range(n):
            first.append(copy(a, 0, me, sibling, src=ins[a]))
            first += [copy(a, 1 + j, me, (*chip, c), src=ins[a]) for j, chip in enumerate(chips)]
        for cp in first:
            cp.start()
        passed = []
        for j, chip in enumerate(chips):
            for a in range(n):
                copy(a, 1 + j, (*chip, c), me).wait_recv()
                fwd = copy(a, 4 + j, (*chip, c), sibling)
                fwd.start()
                passed.append(fwd)
        for a in range(n):
            copy(a, 0, sibling, me).wait_recv()
            for j, chip in enumerate(chips):
                copy(a, 4 + j, (*chip, 1 - c), me).wait_recv()
        for cp in first + passed:
            cp.wait_send()
        for mine in started:
            mine.wait()

    operands = [*shards, *placed] if placed else list(shards)
    return pl.pallas_call(
        body, name=name,
        in_specs=_hbm_specs(len(operands)), out_specs=_hbm_specs(n),
        out_shape=[jax.ShapeDtypeStruct((N_DEV,) + s.shape, s.dtype) for s in shards],
        scratch_shapes=[pltpu.SemaphoreType.DMA((n, 7)), pltpu.SemaphoreType.DMA((n, 7)),
                        pltpu.SemaphoreType.DMA((n,))],
        input_output_aliases={n + a: a for a in range(n)} if placed else {},
        compiler_params=pltpu.CompilerParams(has_side_effects=True),
    )(*operands)


def _place_own(shards, me):
    n = len(shards)

    def body(me_ref, *refs):
        for src, dst in zip(refs[:n], refs[n:]):
            dst[0] = src[...]

    return pl.pallas_call(
        body, name="place_own_shards",
        grid_spec=pltpu.PrefetchScalarGridSpec(
            num_scalar_prefetch=1, grid=(1,),
            in_specs=[pl.BlockSpec(s.shape, lambda i, me_ref: (0, 0)) for s in shards],
            out_specs=[pl.BlockSpec((1,) + s.shape, lambda i, me_ref: (me_ref[0], 0, 0)) for s in shards]),
        out_shape=[jax.ShapeDtypeStruct((N_DEV,) + s.shape, s.dtype) for s in shards],
        compiler_params=_params("arbitrary"),
    )(me, *shards)


_FLIPS = [(dx, dy, dc) for dx in (0, 1) for dy in (0, 1) for dc in (0, 1)][1:]
_HBM = pl.BlockSpec(memory_space=pltpu.HBM)
_SEM = pl.BlockSpec(memory_space=pltpu.SEMAPHORE)
_DATAFLOW = pltpu.SideEffectType.DATAFLOW_SIDE_EFFECTING


def _scatter_copies(src_refs, land_refs, send_sems, recv_sems, gather):
    x, y, c = _mesh_position()
    me = 4 * x + 2 * y + c
    copies = []
    for a, (src, land) in enumerate(zip(src_refs, land_refs)):
        for k, (dx, dy, dc) in enumerate(_FLIPS):
            px, py, pc = (x + dx) % 2, (y + dy) % 2, (c + dc) % 2
            pair = a * len(_FLIPS) + k
            copies.append(pltpu.make_async_remote_copy(
                src_ref=src if gather else src.at[4 * px + 2 * py + pc], dst_ref=land.at[me],
                send_sem=send_sems[pair], recv_sem=recv_sems[pair],
                device_id=(px, py, pc), device_id_type=MESH))
    return copies


def _scatter_start(srcs, lands, gather, name, after=None):
    n = len(srcs)
    pairs = n * len(_FLIPS)
    extra = [] if after is None else [after]

    def body(*refs):
        src_refs, land_refs = refs[:n], refs[n:2 * n]
        first = 2 * n + len(extra)
        send_sems, recv_sems = refs[first:first + pairs], refs[first + pairs:first + 2 * pairs]
        token = refs[-1]
        for cp in _scatter_copies(src_refs, land_refs, send_sems, recv_sems, gather):
            cp.start()
        token[...] = jnp.zeros_like(token)

    arrays = [*srcs, *lands]
    sem = pltpu.SemaphoreType.DMA(())
    out = pl.pallas_call(
        body, name=name,
        out_shape=(*[sem] * (2 * pairs), *[pltpu.HBM(v.shape, v.dtype) for v in arrays],
                   jax.ShapeDtypeStruct((8, LANES), F32)),
        in_specs=[*[_HBM] * (2 * n), *[pl.BlockSpec(memory_space=pl.ANY)] * len(extra)],
        out_specs=(*[_SEM] * (2 * pairs), *[_HBM] * (2 * n), pl.BlockSpec(memory_space=pltpu.VMEM)),
        input_output_aliases={i: 2 * pairs + i for i in range(2 * n)},
        compiler_params=pltpu.CompilerParams(has_side_effects=_DATAFLOW),
    )(*[pltpu.with_memory_space_constraint(v, pltpu.HBM) for v in arrays], *extra)
    sems, rest = out[:2 * pairs], out[2 * pairs:]
    return list(sems[:pairs]), list(sems[pairs:]), list(rest[:n]), list(rest[n:2 * n]), rest[-1]


def _scatter_wait(send_sems, recv_sems, srcs, lands, after, gather, name):
    n = len(srcs)
    pairs = n * len(_FLIPS)

    def body(*refs):
        src_refs, land_refs = refs[:n], refs[n:2 * n]
        send_refs, recv_refs = refs[2 * n:2 * n + pairs], refs[2 * n + pairs:2 * n + 2 * pairs]
        for cp in _scatter_copies(src_refs, land_refs, send_refs, recv_refs, gather):
            cp.wait_send()
            cp.wait_recv()

    arrays = [*srcs, *lands]
    out = pl.pallas_call(
        body, name=name,
        out_shape=tuple(pltpu.HBM(v.shape, v.dtype) for v in arrays),
        in_specs=[*[_HBM] * (2 * n), *[_SEM] * (2 * pairs), pl.BlockSpec(memory_space=pl.ANY)],
        out_specs=tuple([_HBM] * (2 * n)),
        input_output_aliases={i: i for i in range(2 * n)},
        compiler_params=pltpu.CompilerParams(has_side_effects=_DATAFLOW),
    )(*arrays, *send_sems, *recv_sems, after)
    return list(out[:n]), list(out[n:])


def _sum_devices(gathered):
    _, rows, cols = gathered.shape

    def body(g_ref, o_ref):
        total = g_ref[0]
        for d in range(1, N_DEV):
            total = total + g_ref[d]
        o_ref[...] = total

    return pl.pallas_call(
        body, name="sum_small_grads",
        out_shape=jax.ShapeDtypeStruct((rows, cols), F32),
        compiler_params=_params(),
    )(gathered)


def _adamw(w, g, m, v, name):
    rows, cols = w.shape
    tr = rows
    for cand in (256, 128, 64, 32, 16, 8):
        if rows > cand and rows % cand == 0:
            tr = cand
            break

    def body(w_ref, g_ref, m_ref, v_ref, delta_ref, newm_ref, newv_ref):
        g_v = g_ref[...]
        new_m = ADAM_B1 * m_ref[...] + (1.0 - ADAM_B1) * g_v
        new_v = ADAM_B2 * v_ref[...] + (1.0 - ADAM_B2) * (g_v * g_v)
        m_hat = new_m / (1.0 - ADAM_B1 ** ADAM_STEP)
        v_hat = new_v / (1.0 - ADAM_B2 ** ADAM_STEP)
        delta_ref[...] = -ADAM_LR * (m_hat / (jnp.sqrt(v_hat) + ADAM_EPS) + ADAM_WD * w_ref[...])
        newm_ref[...] = new_m
        newv_ref[...] = new_v

    blk = pl.BlockSpec((tr, cols), lambda i: (i, 0))
    shape = jax.ShapeDtypeStruct((rows, cols), F32)
    return pl.pallas_call(
        body, name=name, grid=(rows // tr,),
        in_specs=[blk] * 4, out_specs=[blk] * 3, out_shape=[shape] * 3,
        compiler_params=_params("parallel"),
    )(w, g, m, v)


def _adamw_update(w, g, m, v):
    new_m = ADAM_B1 * m + (1.0 - ADAM_B1) * g
    new_v = ADAM_B2 * v + (1.0 - ADAM_B2) * (g * g)
    m_hat = new_m / (1.0 - ADAM_B1 ** ADAM_STEP)
    v_hat = new_v / (1.0 - ADAM_B2 ** ADAM_STEP)
    return -ADAM_LR * (m_hat / (jnp.sqrt(v_hat) + ADAM_EPS) + ADAM_WD * w), new_m, new_v


def _reduce_and_adamw(grad, received, me, w, m, v, name):
    _, rows, cols = grad.shape
    tr = rows // 2
    assert rows % 16 == 0

    def body(me_ref, g_ref, r_ref, w_ref, m_ref, v_ref, grad_ref, delta_ref, newm_ref, newv_ref, acc):
        p = pl.program_id(1)
        term = jnp.where(p == me_ref[0], g_ref[0], r_ref[0]).astype(F32)

        @pl.when(p == 0)
        def _():
            acc[...] = term

        @pl.when(p > 0)
        def _():
            acc[...] += term

        @pl.when(p == N_DEV - 1)
        def _():
            g = acc[...]
            grad_ref[...] = g
            delta_ref[...], newm_ref[...], newv_ref[...] = _adamw_update(w_ref[...], g, m_ref[...], v_ref[...])

    blk = (1, tr, cols)
    tile = pl.BlockSpec((tr, cols), lambda j, p, me_ref: (j, 0))
    shape = jax.ShapeDtypeStruct((rows, cols), F32)
    return pl.pallas_call(
        body, name=name,
        grid_spec=pltpu.PrefetchScalarGridSpec(
            num_scalar_prefetch=1, grid=(rows // tr, N_DEV),
            in_specs=[pl.BlockSpec(blk, lambda j, p, me_ref: (me_ref[0], j, 0)),
                      pl.BlockSpec(blk, lambda j, p, me_ref: (p, j, 0)), tile, tile, tile],
            out_specs=[tile] * 4,
            scratch_shapes=[pltpu.VMEM((tr, cols), F32)]),
        out_shape=[shape] * 4,
        compiler_params=_params("parallel", "arbitrary"),
    )(me, grad, received, w, m, v)


def _pack(pieces, rows):
    flat = jnp.concatenate([p.reshape(-1) for p in pieces])
    return jnp.pad(flat, (0, rows * LANES - flat.shape[0])).reshape(rows, LANES)


def _unpack(packed, shapes):
    flat = packed.reshape(-1)
    out, pos = [], 0
    for shape in shapes:
        size = 1
        for s in shape:
            size *= s
        out.append(flat[pos:pos + size].reshape(shape))
        pos += size
    return out


def _rows_for(count):
    return -(-count // (8 * LANES)) * 8


SMALL = ("norm_mix_pre", "conv_dw_b", "conv_ln_g", "conv_ln_b", "rel_bias", "norm_mix_post", "norm_ffn_pre",
         "ffn_dw_b", "norm_ffn_post")
SHARDED_SMALL = ("conv_dw_w", "ffn_dw_w")
LARGE = ("w_in", "w_out", "w_up", "w_down")
WEIGHTS = ("norm_mix_pre", "w_in", "conv_dw_w", "conv_dw_b", "conv_ln_g", "conv_ln_b", "rel_bias", "w_out",
           "norm_mix_post", "norm_ffn_pre", "w_up", "ffn_dw_w", "ffn_dw_b", "w_down", "norm_ffn_post")


def kernel(x, norm_mix_pre, w_in, conv_dw_w, conv_dw_b, conv_ln_g, conv_ln_b, rel_bias, w_out, norm_mix_post, norm_ffn_pre, w_up, ffn_dw_w, ffn_dw_b, w_down, norm_ffn_post, loss_target, m_norm_mix_pre, m_w_in, m_conv_dw_w, m_conv_dw_b, m_conv_ln_g, m_conv_ln_b, m_rel_bias, m_w_out, m_norm_mix_post, m_norm_ffn_pre, m_w_up, m_ffn_dw_w, m_ffn_dw_b, m_w_down, m_norm_ffn_post, v_norm_mix_pre, v_w_in, v_conv_dw_w, v_conv_dw_b, v_conv_ln_g, v_conv_ln_b, v_rel_bias, v_w_out, v_norm_mix_post, v_norm_ffn_pre, v_w_up, v_ffn_dw_w, v_ffn_dw_b, v_w_down, v_norm_ffn_post):
    weights = dict(norm_mix_pre=norm_mix_pre, w_in=w_in, conv_dw_w=conv_dw_w, conv_dw_b=conv_dw_b, conv_ln_g=conv_ln_g,
                   conv_ln_b=conv_ln_b, rel_bias=rel_bias, w_out=w_out, norm_mix_post=norm_mix_post,
                   norm_ffn_pre=norm_ffn_pre, w_up=w_up, ffn_dw_w=ffn_dw_w, ffn_dw_b=ffn_dw_b, w_down=w_down,
                   norm_ffn_post=norm_ffn_post)
    mom1 = dict(norm_mix_pre=m_norm_mix_pre, w_in=m_w_in, conv_dw_w=m_conv_dw_w, conv_dw_b=m_conv_dw_b,
                conv_ln_g=m_conv_ln_g, conv_ln_b=m_conv_ln_b, rel_bias=m_rel_bias, w_out=m_w_out,
                norm_mix_post=m_norm_mix_post, norm_ffn_pre=m_norm_ffn_pre, w_up=m_w_up, ffn_dw_w=m_ffn_dw_w,
                ffn_dw_b=m_ffn_dw_b, w_down=m_w_down, norm_ffn_post=m_norm_ffn_post)
    mom2 = dict(norm_mix_pre=v_norm_mix_pre, w_in=v_w_in, conv_dw_w=v_conv_dw_w, conv_dw_b=v_conv_dw_b,
                conv_ln_g=v_conv_ln_g, conv_ln_b=v_conv_ln_b, rel_bias=v_rel_bias, w_out=v_w_out,
                norm_mix_post=v_norm_mix_post, norm_ffn_pre=v_norm_ffn_pre, w_up=v_w_up, ffn_dw_w=v_ffn_dw_w,
                ffn_dw_b=v_ffn_dw_b, w_down=v_w_down, norm_ffn_post=v_norm_ffn_post)

    x2 = x[0]
    target = loss_target[0]
    t_rows = x2.shape[0]
    d = D_MODEL
    in_cols = 2 * CONV_WIDTH + 3 * ATTN_WIDTH
    my_x, my_y, my_c = _mesh_position()
    my_dev = 4 * my_x + 2 * my_y + my_c

    small_conv = _pack([conv_dw_w[0], ffn_dw_w[0]], 32)
    me = jnp.reshape(my_dev, (1,)).astype(jnp.int32)
    first_shards = [w_in[0].T.astype(BF16), small_conv]
    late_shards = [w_out[0].astype(BF16), w_up[0].T.astype(BF16), w_down[0].astype(BF16)]
    placed = _place_own(first_shards + late_shards, me)
    win_t, conv_g = _all_gather(first_shards, "all_gather_weights", placed=placed[:2])
    wout_gather = _scatter_start(late_shards[:1], placed[2:3], True, "gather_w_out_start", after=win_t)
    ffn_gather = _scatter_start(late_shards[1:], placed[3:], True, "gather_ffn_weights_start", after=wout_gather[4])
    late_token = ffn_gather[4]
    win_t = win_t.reshape(in_cols, d)
    conv_flat = conv_g.reshape(N_DEV, 32 * LANES)
    n_cw = CONV_K * (CONV_WIDTH // N_DEV)
    conv_w_full = conv_flat[:, :n_cw].reshape(N_DEV, CONV_K, CONV_WIDTH // N_DEV).transpose(1, 0, 2).reshape(CONV_K, CONV_WIDTH)
    ffn_w_full = conv_flat[:, n_cw:].reshape(N_DEV, FFN_K, 2 * D_FF // N_DEV).transpose(1, 0, 2).reshape(FFN_K, 2 * D_FF)

    u1, proj_a = _pre_norm_proj(x2, norm_mix_pre + late_token[0:1, 0:1], win_t, 2 * CONV_WIDTH, "pre_norm_proj_conv")
    qkv_t = _matmul(win_t, u1, mode="nt", m=3 * ATTN_WIDTH, n=t_rows, k=d, tm=512, tn=2048, tk=d,
                    out_dtype=BF16, name="proj_qkv", a_m0=2 * CONV_WIDTH)
    conv_c, conv_out = _conv_forward(proj_a, conv_w_full, conv_dw_b, conv_ln_g, conv_ln_b)
    o_t, attn_lse = _attn_forward(qkv_t, rel_bias[0])
    _, (wout_g,) = _scatter_wait(*wout_gather[:4], o_t, True, "gather_w_out_wait")
    wout_g = wout_g.reshape(d, d)
    mixed, h1, u2 = _matmul_rows(
        [(conv_out, "nn", CONV_WIDTH, 0), (o_t, "tn", ATTN_WIDTH, CONV_WIDTH)], wout_g, m=t_rows, n=d, tm=1024,
        name="out_proj_mid_forward", row_ins=[x2], vec_ins=[norm_mix_post, norm_ffn_pre], row_outs=[F32, F32, BF16],
        acc_outs=[], epilogue=_mid_forward_epilogue)
    _, (wup_t, wdown_g) = _scatter_wait(*ffn_gather[:4], u2, True, "gather_ffn_weights_wait")
    wup_t = wup_t.reshape(2 * D_FF, d)
    wdown_g = wdown_g.reshape(D_FF, d)
    hup = _matmul(u2, wup_t, mode="nt", m=t_rows, n=2 * D_FF, k=d, tm=2048, tn=1408, tk=d,
                  out_dtype=F32, name="ffn_up")
    act, ffn_gel, ffn_slope = _ffn_activation(hup, ffn_w_full, ffn_dw_b)
    dy, df, _, d_norm_ffn_post, loss = _matmul_rows(
        [(act, "nn", D_FF, 0)], wdown_g, m=t_rows, n=d, tm=1024, name="ffn_down_loss_backward",
        row_ins=[h1, target], vec_ins=[norm_ffn_post], row_outs=[F32, BF16], acc_outs=[(1, d), (1, d), (1, 1)],
        epilogue=_loss_epilogue)

    dact = _matmul(df, wdown_g, mode="nt", m=t_rows, n=D_FF, k=d, tm=2048, tn=1408, tk=d,
                   out_dtype=F32, name="ffn_down_dx")
    g_wdown = _matmul(act, df, mode="tn", m=D_FF, n=d, k=t_rows, tm=1408, tn=1024, tk=2048,
                      out_dtype=F32, name="ffn_down_dw")
    dhg, dhv, dwg, dwv, dbg, dbv = _ffn_backward(dact, ffn_gel, ffn_slope, hup, ffn_w_full)
    g_wup_t = _matmul(dhg, u2, mode="tn", m=D_FF, n=d, k=t_rows, tm=1408, tn=1024, tk=2048, out_dtype=F32,
                      name="ffn_up_dw_gate", out_rows=2 * D_FF)
    g_wup_t = _matmul(dhv, u2, mode="tn", m=D_FF, n=d, k=t_rows, tm=1408, tn=1024, tk=2048, out_dtype=F32,
                      name="ffn_up_dw_value", out_rows=2 * D_FF, out_m0=D_FF, into=g_wup_t)
    ffn_grads = [g_wup_t.reshape(N_DEV, 2 * D_FF // N_DEV, d), g_wdown.reshape(N_DEV, D_FF // N_DEV, d)]
    red_send, red_recv, ffn_grads, red_lands, red_token = _scatter_start(
        ffn_grads, [lax.empty(g.shape, F32) for g in ffn_grads], False, "reduce_ffn_grads_start")
    dh1, dmixed, d_norm_ffn_pre, d_norm_mix_post = _matmul_rows(
        [(dhg, "nn", D_FF, 0), (dhv, "nn", D_FF, D_FF)], wup_t, m=t_rows, n=d, tm=512, name="ffn_up_dx_mid_backward",
        row_ins=[dy, h1, mixed], vec_ins=[norm_ffn_pre + red_token[0:1, 0:1], norm_mix_post], row_outs=[F32, BF16],
        acc_outs=[(1, d), (1, d)], epilogue=_mid_backward_epilogue)
    dconv_out = _matmul(dmixed, wout_g, mode="nt", m=t_rows, n=CONV_WIDTH, k=d, tm=2048, tn=512, tk=d,
                        out_dtype=F32, name="out_proj_dx_conv")
    do_t = _matmul(wout_g, dmixed, mode="nt", m=ATTN_WIDTH, n=t_rows, k=d, tm=512, tn=2048, tk=d,
                   out_dtype=BF16, name="out_proj_dx_attn", a_m0=CONV_WIDTH)
    g_wout = _matmul(conv_out, dmixed, mode="tn", m=CONV_WIDTH, n=d, k=t_rows, tm=512, tn=1024, tk=2048, out_dtype=F32,
                     name="out_proj_dw_conv", out_rows=d)
    g_wout = _matmul(o_t, dmixed, mode="nn", m=ATTN_WIDTH, n=d, k=t_rows, tm=512, tn=1024, tk=2048, out_dtype=F32,
                     name="out_proj_dw_attn", out_rows=d, out_m0=CONV_WIDTH, into=g_wout)
    wout_handle = _scatter_start([g_wout.reshape(N_DEV, d // N_DEV, d)], [lax.empty((N_DEV, d // N_DEV, d), F32)],
                                 False, "reduce_w_out_grad_start")
    dproj_a, d_conv_w, d_conv_b, d_ln_g, d_ln_b = _conv_backward(
        dconv_out, conv_c, proj_a, conv_w_full, conv_ln_g + wout_handle[4][0:1, 0:1], conv_ln_b)
    dqkv_parts = _attn_backward(qkv_t, o_t, do_t, attn_lse, rel_bias[0])
    drel = dqkv_parts[3]
    g_win_t = _matmul(dproj_a, u1, mode="tn", m=2 * CONV_WIDTH, n=d, k=t_rows, tm=1024, tn=1024, tk=2048, out_dtype=BF16,
                      name="proj_dw_conv", out_rows=in_cols)
    for j, part in enumerate("qkv"):
        row0 = 2 * CONV_WIDTH + j * ATTN_WIDTH
        g_win_t = _matmul(dqkv_parts[j], u1, mode="nn", m=ATTN_WIDTH, n=d, k=t_rows, tm=512, tn=1024, tk=2048,
                          out_dtype=BF16, name="proj_dw_" + part, out_rows=in_cols, out_m0=row0, into=g_win_t)
    win_handle = _scatter_start([g_win_t.reshape(N_DEV, in_cols // N_DEV, d)],
                                [lax.empty((N_DEV, in_cols // N_DEV, d), BF16)], False, "reduce_w_in_grad_start")
    dx, d_norm_mix_pre = _matmul_rows(
        [(dproj_a, "nn", 2 * CONV_WIDTH, 0)]
        + [(dqkv_parts[j], "tn", ATTN_WIDTH, 2 * CONV_WIDTH + j * ATTN_WIDTH) for j in range(3)],
        win_t, m=t_rows, n=d, tm=1024, name="proj_dx_input_backward", row_ins=[dh1, x2],
        vec_ins=[norm_mix_pre + win_handle[4][0:1, 0:1]], row_outs=[F32], acc_outs=[(1, d)],
        epilogue=_input_backward_epilogue)

    small_grads = dict(norm_mix_pre=d_norm_mix_pre, conv_dw_b=d_conv_b, conv_ln_g=d_ln_g, conv_ln_b=d_ln_b,
                       rel_bias=drel[:, :2 * MAX_REL + 1], norm_mix_post=d_norm_mix_post, norm_ffn_pre=d_norm_ffn_pre,
                       ffn_dw_b=jnp.concatenate([dbg, dbv], axis=1), norm_ffn_post=d_norm_ffn_post)
    pieces = [small_grads[nm] for nm in SMALL] + [d_conv_w, jnp.concatenate([dwg, dwv], axis=1), loss]
    count = sum(p.size for p in pieces)
    (gathered_small,) = _all_gather([_pack(pieces, _rows_for(count))], "all_gather_small_grads")
    summed = _sum_devices(gathered_small)
    shapes = [weights[nm].shape for nm in SMALL] + [(CONV_K, CONV_WIDTH), (FFN_K, 2 * D_FF), (1, 1)]
    unpacked = _unpack(summed, shapes)
    grads = dict(zip(SMALL, unpacked[:len(SMALL)]))
    cw_shard, fw_shard = CONV_WIDTH // N_DEV, 2 * D_FF // N_DEV
    grads["conv_dw_w"] = lax.dynamic_slice_in_dim(unpacked[-3], my_dev * cw_shard, cw_shard, axis=1)[None]
    grads["ffn_dw_w"] = lax.dynamic_slice_in_dim(unpacked[-2], my_dev * fw_shard, fw_shard, axis=1)[None]
    total_loss = unpacked[-1].reshape(())

    me = jnp.reshape(my_dev, (1,)).astype(jnp.int32)
    delta, new_m, new_v = {}, {}, {}

    def finish(nm, send, recv, srcs, lands, after, transposed):
        srcs, lands = _scatter_wait(send, recv, srcs, lands, after, False, "reduce_" + nm + "_grad_wait")
        for name_a, src, land in zip(nm.split("_and_"), srcs, lands):
            flip = (lambda t: t.T) if transposed[name_a] else (lambda t: t)
            outs = _reduce_and_adamw(src, land, me, flip(weights[name_a][0]), flip(mom1[name_a][0]),
                                     flip(mom2[name_a][0]), "reduce_adamw_" + name_a)
            for store, arr in zip((grads, delta, new_m, new_v), outs):
                store[name_a] = flip(arr)[None]

    transposed = dict(w_in=True, w_out=False, w_up=True, w_down=False)
    finish("w_up_and_w_down", red_send, red_recv, ffn_grads, red_lands, dx, transposed)
    finish("w_out", *wout_handle[:4], dx, transposed)
    finish("w_in", *win_handle[:4], delta["w_up"], transposed)
    small_names = SMALL + SHARDED_SMALL
    small_count = sum(weights[nm].size for nm in small_names)
    small_rows = _rows_for(small_count)
    packed = [_pack([src[nm] for nm in small_names], small_rows) for src in (weights, grads, mom1, mom2)]
    outs = _adamw(*packed, "adamw_small")
    small_shapes = [weights[nm].shape for nm in small_names]
    for store, arr in zip((delta, new_m, new_v), outs):
        store.update(zip(small_names, _unpack(arr, small_shapes)))

    return (total_loss, dx[None], *[grads[nm] for nm in WEIGHTS], *[delta[nm] for nm in WEIGHTS],
            *[new_m[nm] for nm in WEIGHTS], *[new_v[nm] for nm in WEIGHTS])
```

```python
import jax
import jax.numpy as jnp
from jax import lax
from jax.experimental import pallas as pl
from jax.experimental.pallas import tpu as pltpu

F32 = jnp.float32
BF16 = jnp.bfloat16
MESH = pl.DeviceIdType.MESH
N_DEV = 8

EPS = 1e-6
NEG_INF = -1e30
D_MODEL = 1024
CONV_WIDTH = 512
ATTN_WIDTH = 512
N_HEADS = 8
HEAD_DIM = 64
CHUNK = 64
LEFT = 8 * CHUNK
QBLK = 2 * CHUNK
WIN = LEFT + QBLK
CONV_K = 31
CONV_HALO = 32
FFN_K = 3
FFN_HALO = 8
D_FF = 2816
MAX_REL = 128
SCALE = HEAD_DIM ** -0.5
ADAM_LR, ADAM_B1, ADAM_B2, ADAM_EPS, ADAM_WD, ADAM_STEP = 0.001, 0.9, 0.999, 1e-08, 0.01, 10

V7X_VMEM_BYTES = 64 * 2**20
VMEM_LIMIT_BYTES = V7X_VMEM_BYTES - 8 * 2**20
LANES = 128


def _params(*sem):
    return pltpu.CompilerParams(dimension_semantics=sem or None, vmem_limit_bytes=VMEM_LIMIT_BYTES)


_DOT_DIMS = {"nn": (((1,), (0,)), ((), ())), "nt": (((1,), (1,)), ((), ())), "tn": (((0,), (0,)), ((), ()))}


def _matmul(a, b, *, mode, m, n, k, tm, tn, tk, out_dtype, name, a_m0=0, b_n0=0, b_k0=0, add=None,
            out_rows=None, out_m0=0, into=None):
    tm, tn, tk = min(tm, m), min(tn, n), min(tk, k)
    out_rows = m if out_rows is None else out_rows
    assert m % tm == 0 and n % tn == 0 and k % tk == 0, (name, m, n, k, tm, tn, tk)
    assert a_m0 % tm == 0 and b_n0 % tn == 0 and b_k0 % tk == 0 and out_m0 % tm == 0, name
    am, bn, bk, om = a_m0 // tm, b_n0 // tn, b_k0 // tk, out_m0 // tm
    gk = k // tk
    dims = _DOT_DIMS[mode]

    if mode == "tn":
        a_spec = pl.BlockSpec((tk, tm), lambda i, j, kk: (kk, i + am))
    else:
        a_spec = pl.BlockSpec((tm, tk), lambda i, j, kk: (i + am, kk))
    if mode == "nt":
        b_spec = pl.BlockSpec((tn, tk), lambda i, j, kk: (j + bn, kk + bk))
    else:
        b_spec = pl.BlockSpec((tk, tn), lambda i, j, kk: (kk + bk, j + bn))
    o_spec = pl.BlockSpec((tm, tn), lambda i, j, kk: (i + om, j))
    in_specs = [a_spec, b_spec]
    operands = [a, b]
    if add is not None:
        assert out_rows == m
        in_specs.append(o_spec)
        operands.append(add)
    aliases = {}
    if into is not None:
        aliases = {len(operands): 0}
        in_specs.append(pl.BlockSpec(memory_space=pl.ANY))
        operands.append(into)

    def body(*refs):
        a_ref, b_ref = refs[0], refs[1]
        add_ref = refs[2] if add is not None else None
        o_ref = refs[len(operands)]
        part = lax.dot_general(a_ref[...].astype(BF16), b_ref[...].astype(BF16), dims,
                               preferred_element_type=F32)

        def finish(total):
            if add_ref is not None:
                total = total + add_ref[...]
            o_ref[...] = total.astype(out_dtype)

        if gk == 1:
            finish(part)
        else:
            acc_ref = refs[-1]
            kk = pl.program_id(2)

            @pl.when(kk == 0)
            def _():
                acc_ref[...] = part

            @pl.when(kk > 0)
            def _():
                acc_ref[...] += part

            @pl.when(kk == gk - 1)
            def _():
                finish(acc_ref[...])

    return pl.pallas_call(
        body, name=name,
        grid=(m // tm, n // tn, gk),
        in_specs=in_specs, out_specs=o_spec,
        out_shape=jax.ShapeDtypeStruct((out_rows, n), out_dtype),
        scratch_shapes=[pltpu.VMEM((tm, tn), F32)] if gk > 1 else [],
        input_output_aliases=aliases,
        compiler_params=_params("parallel", "parallel", "arbitrary"),
    )(*operands)


def _matmul_rows(pieces, b, *, m, n, tm, name, row_ins, vec_ins, row_outs, acc_outs, epilogue):
    tm = min(tm, m)
    assert m % tm == 0
    steps = m // tm
    in_specs, operands = [], []
    for a, mode, k, k0 in pieces:
        assert k0 % k == 0
        if mode == "tn":
            in_specs.append(pl.BlockSpec((k, tm), lambda i: (0, i)))
        else:
            in_specs.append(pl.BlockSpec((tm, k), lambda i: (i, 0)))
        in_specs.append(pl.BlockSpec((k, n), lambda i, blk=k0 // k: (blk, 0)))
        operands += [a, b]
    row = pl.BlockSpec((tm, n), lambda i: (i, 0))
    in_specs += [row] * len(row_ins) + [pl.BlockSpec((1, n), lambda i: (0, 0))] * len(vec_ins)
    operands += [*row_ins, *vec_ins]
    n_in = len(operands)

    def body(*refs):
        total = None
        for p, (_, mode, _, _) in enumerate(pieces):
            part = lax.dot_general(refs[2 * p][...], refs[2 * p + 1][...], _DOT_DIMS[mode], preferred_element_type=F32)
            total = part if total is None else total + part
        first = 2 * len(pieces)
        rows = refs[first:first + len(row_ins)]
        vecs = refs[first + len(row_ins):n_in]
        outs = refs[n_in:n_in + len(row_outs)]
        accs = refs[n_in + len(row_outs):]
        epilogue(total, rows, vecs, outs, accs, pl.program_id(0), steps)

    return pl.pallas_call(
        body, name=name, grid=(steps,),
        in_specs=in_specs,
        out_specs=[row] * len(row_outs) + [pl.BlockSpec(s, lambda i: (0, 0)) for s in acc_outs],
        out_shape=[jax.ShapeDtypeStruct((m, n), dt) for dt in row_outs]
        + [jax.ShapeDtypeStruct(s, F32) for s in acc_outs],
        compiler_params=_params("arbitrary" if acc_outs else "parallel"),
    )(*operands)


def _rms_hat(v):
    r = lax.rsqrt(jnp.mean(v * v, axis=-1, keepdims=True) + EPS)
    return v * r, r


def _rms_bwd(dn, hat, r):
    return r * (dn - hat * jnp.mean(dn * hat, axis=-1, keepdims=True))


def _sigmoid(v):
    return 1.0 / (1.0 + jnp.exp(-v))


_GELU_C = 0.7978845608028654


def _gelu(v):
    return 0.5 * v * (1.0 + jnp.tanh(_GELU_C * (v + 0.044715 * v * (v * v))))


def _gelu_parts(v):
    v2 = v * v
    t = jnp.tanh(_GELU_C * (v + 0.044715 * v * v2))
    cdf = 0.5 * (1.0 + t)
    dcdf = 0.5 * (1.0 - t * t) * _GELU_C * (1.0 + 3.0 * 0.044715 * v2)
    return v * cdf, cdf + v * dcdf


def _row_tile(t_rows, want):
    tile = min(want, t_rows)
    assert t_rows % tile == 0
    return tile


def _pre_norm_proj(x, g, w_t, n, name):
    t_rows, d = x.shape
    tm = _row_tile(t_rows, 1024)
    rest = w_t.shape[0] - n

    def body(x_ref, g_ref, w_ref, wrest_ref, u_ref, o_ref, ot_ref):
        hat, _ = _rms_hat(x_ref[...])
        u = (hat * g_ref[...]).astype(BF16)
        u_ref[...] = u
        o_ref[...] = lax.dot_general(u, w_ref[...], _DOT_DIMS["nt"], preferred_element_type=F32)
        ot_ref[...] = lax.dot_general(wrest_ref[...], u, _DOT_DIMS["nt"], preferred_element_type=F32).astype(BF16)

    return pl.pallas_call(
        body, name=name, grid=(t_rows // tm,),
        in_specs=[pl.BlockSpec((tm, d), lambda i: (i, 0)), pl.BlockSpec((1, d), lambda i: (0, 0)),
                  pl.BlockSpec((n, d), lambda i: (0, 0)), pl.BlockSpec((rest, d), lambda i: (0, 0))],
        out_specs=[pl.BlockSpec((tm, d), lambda i: (i, 0)), pl.BlockSpec((tm, n), lambda i: (i, 0)),
                   pl.BlockSpec((rest, tm), lambda i: (0, i))],
        out_shape=[jax.ShapeDtypeStruct((t_rows, d), BF16), jax.ShapeDtypeStruct((t_rows, n), F32),
                   jax.ShapeDtypeStruct((rest, t_rows), BF16)],
        compiler_params=_params("parallel"),
    )(x, g, w_t, w_t[n:])


def _zero_at_start(accs, step):
    @pl.when(step == 0)
    def _():
        for acc in accs:
            acc[...] = jnp.zeros_like(acc)


def _mid_forward_epilogue(mixed, rows, vecs, outs, accs, step, steps):
    (x_ref,), (gpost_ref, gpre_ref), (mixed_ref, h1_ref, u2_ref) = rows, vecs, outs
    mixed_ref[...] = mixed
    hat, _ = _rms_hat(mixed)
    h1 = x_ref[...] + hat * gpost_ref[...]
    h1_ref[...] = h1
    hat1, _ = _rms_hat(h1)
    u2_ref[...] = (hat1 * gpre_ref[...]).astype(BF16)


def _loss_epilogue(f, rows, vecs, outs, accs, step, steps):
    (h1_ref, tgt_ref), (g_ref,), (dy_ref, df_ref), (sq_ref, dg_ref, loss_ref) = rows, vecs, outs, accs
    _zero_at_start(accs, step)
    g = g_ref[...]
    d = f.shape[-1]
    hat, r = _rms_hat(f)
    err = h1_ref[...] + hat * g - tgt_ref[...]
    sq_ref[...] += jnp.sum(err * err, axis=0, keepdims=True)
    dy = err * (1.0 / d)
    dy_ref[...] = dy
    dg_ref[...] += jnp.sum(dy * hat, axis=0, keepdims=True)
    df_ref[...] = _rms_bwd(dy * g, hat, r).astype(BF16)

    @pl.when(step == steps - 1)
    def _():
        loss_ref[...] = (0.5 / d) * jnp.sum(sq_ref[...], axis=1, keepdims=True)


def _mid_backward_epilogue(du2, rows, vecs, outs, accs, step, steps):
    (dy_ref, h1_ref, mixed_ref), (gpre_ref, gpost_ref), (dh1_ref, dmixed_ref), (dgpre_ref, dgpost_ref) = rows, vecs, outs, accs
    _zero_at_start(accs, step)
    hat1, r1 = _rms_hat(h1_ref[...])
    dgpre_ref[...] += jnp.sum(du2 * hat1, axis=0, keepdims=True)
    dh1 = dy_ref[...] + _rms_bwd(du2 * gpre_ref[...], hat1, r1)
    dh1_ref[...] = dh1
    hatm, rm = _rms_hat(mixed_ref[...])
    dgpost_ref[...] += jnp.sum(dh1 * hatm, axis=0, keepdims=True)
    dmixed_ref[...] = _rms_bwd(dh1 * gpost_ref[...], hatm, rm).astype(BF16)


def _input_backward_epilogue(du1, rows, vecs, outs, accs, step, steps):
    (dh1_ref, x_ref), (g_ref,), (dx_ref,), (dg_ref,) = rows, vecs, outs, accs
    _zero_at_start(accs, step)
    hat, r = _rms_hat(x_ref[...])
    dg_ref[...] += jnp.sum(du1 * hat, axis=0, keepdims=True)
    dx_ref[...] = dh1_ref[...] + _rms_bwd(du1 * g_ref[...], hat, r)


CONV_STRIP = 32


def _glu(block):
    return block[:, :CONV_WIDTH] * _sigmoid(block[:, CONV_WIDTH:])


def _layer_norm_parts(c):
    mu = jnp.mean(c, axis=-1, keepdims=True)
    xc = c - mu
    r = lax.rsqrt(jnp.mean(xc * xc, axis=-1, keepdims=True) + EPS)
    return xc * r, r


CONV_WINDOW = 2 * CONV_STRIP
SHIFTED_ROWS = CONV_WINDOW - 8


def _shifted_copies(v, shifted):
    for s in range(1, 8):
        shifted[s] = v[s:s + SHIFTED_ROWS, :]


def _window_rows(v, shifted, start):
    s, a = start % 8, start - start % 8
    return v[a:a + CONV_STRIP, :] if s == 0 else shifted[s, a:a + CONV_STRIP, :]


def _conv_forward(proj_a, w, b, ln_g, ln_b):
    t_rows = proj_a.shape[0]
    tm = _row_tile(t_rows, 512)
    hb = tm // CONV_HALO
    cw = CONV_WIDTH

    def body(cur_ref, prev_ref, w_ref, b_ref, g_ref, beta_ref, c_ref, out_ref, hbuf, shifted):
        i = pl.program_id(0)
        hbuf[0:CONV_HALO, :] = jnp.where(i > 0, _glu(prev_ref[...]), 0.0)
        hbuf[CONV_HALO:, :] = _glu(cur_ref[...])

        def strip(s, carry):
            base = pl.multiple_of(s * CONV_STRIP, CONV_STRIP)
            v = hbuf[pl.ds(base, CONV_WINDOW), :]
            _shifted_copies(v, shifted)
            acc = jnp.broadcast_to(b_ref[...], (CONV_STRIP, cw))
            off = CONV_HALO - (CONV_K - 1)
            for kk in range(CONV_K):
                acc = acc + w_ref[kk:kk + 1, :] * _window_rows(v, shifted, off + kk)
            c_ref[pl.ds(base, CONV_STRIP), :] = acc
            hat, _ = _layer_norm_parts(acc)
            z = hat * g_ref[...] + beta_ref[...]
            out_ref[pl.ds(base, CONV_STRIP), :] = (z * _sigmoid(z)).astype(BF16)
            return carry

        lax.fori_loop(0, tm // CONV_STRIP, strip, 0)

    vec = pl.BlockSpec((1, cw), lambda i: (0, 0))
    return pl.pallas_call(
        body, name="conv_forward", grid=(t_rows // tm,),
        in_specs=[pl.BlockSpec((tm, 2 * cw), lambda i: (i, 0)),
                  pl.BlockSpec((CONV_HALO, 2 * cw), lambda i: (jnp.maximum(i * hb - 1, 0), 0)),
                  pl.BlockSpec((CONV_K, cw), lambda i: (0, 0)), vec, vec, vec],
        out_specs=[pl.BlockSpec((tm, cw), lambda i: (i, 0)), pl.BlockSpec((tm, cw), lambda i: (i, 0))],
        out_shape=[jax.ShapeDtypeStruct((t_rows, cw), F32), jax.ShapeDtypeStruct((t_rows, cw), BF16)],
        scratch_shapes=[pltpu.VMEM((tm + CONV_HALO, cw), F32), pltpu.VMEM((8, SHIFTED_ROWS, cw), F32)],
        compiler_params=_params("parallel"),
    )(proj_a, proj_a, w, b, ln_g, ln_b)


def _conv_backward(dout, c, proj_a, w, ln_g, ln_b):
    t_rows = c.shape[0]
    tm = _row_tile(t_rows, 512)
    hb = tm // CONV_HALO
    nt = t_rows // tm
    last_halo = t_rows // CONV_HALO - 1
    cw = CONV_WIDTH

    def body(dout_ref, dout_next_ref, c_ref, c_next_ref, cur_ref, prev_ref, w_ref, g_ref, beta_ref,
             dproj_ref, dw_ref, db_ref, dg_ref, dbeta_ref, hbuf, dcbuf, dwacc, h_shifted, d_shifted):
        i = pl.program_id(0)

        @pl.when(i == 0)
        def _():
            dwacc[...] = jnp.zeros_like(dwacc)
            db_ref[...] = jnp.zeros_like(db_ref)
            dg_ref[...] = jnp.zeros_like(dg_ref)
            dbeta_ref[...] = jnp.zeros_like(dbeta_ref)

        def ln_swish_backward(dout_v, c_v):
            hat, r = _layer_norm_parts(c_v)
            g = g_ref[...]
            z = hat * g + beta_ref[...]
            sg = _sigmoid(z)
            dz = dout_v * (sg * (1.0 + z * (1.0 - sg)))
            dhat = dz * g
            dc = r * (dhat - jnp.mean(dhat, axis=-1, keepdims=True)
                      - hat * jnp.mean(dhat * hat, axis=-1, keepdims=True))
            return dc, dz, hat

        dc, dz, hat = ln_swish_backward(dout_ref[...], c_ref[...])
        dg_ref[...] += jnp.sum(dz * hat, axis=0, keepdims=True)
        dbeta_ref[...] += jnp.sum(dz, axis=0, keepdims=True)
        db_ref[...] += jnp.sum(dc, axis=0, keepdims=True)
        dcbuf[0:tm, :] = dc
        dc_next, _, _ = ln_swish_backward(dout_next_ref[...], c_next_ref[...])
        dcbuf[tm:, :] = jnp.where(i < nt - 1, dc_next, 0.0)

        hbuf[0:CONV_HALO, :] = jnp.where(i > 0, _glu(prev_ref[...]), 0.0)
        hbuf[CONV_HALO:, :] = _glu(cur_ref[...])

        def strip(s, carry):
            base = pl.multiple_of(s * CONV_STRIP, CONV_STRIP)
            dv = dcbuf[pl.ds(base, CONV_WINDOW), :]
            hv = hbuf[pl.ds(base, CONV_WINDOW), :]
            _shifted_copies(dv, d_shifted)
            _shifted_copies(hv, h_shifted)
            dcs = dv[0:CONV_STRIP, :]
            dh = jnp.zeros((CONV_STRIP, cw), F32)
            off = CONV_HALO - (CONV_K - 1)
            for kk in range(CONV_K):
                back = CONV_K - 1 - kk
                dh = dh + w_ref[kk:kk + 1, :] * _window_rows(dv, d_shifted, back)
                prod = dcs * _window_rows(hv, h_shifted, off + kk)
                dwacc[kk] += jnp.sum(prod.reshape(CONV_STRIP // 8, 8, cw), axis=0)
            blk = cur_ref[pl.ds(base, CONV_STRIP), :]
            val, sg = blk[:, :cw], _sigmoid(blk[:, cw:])
            dproj_ref[pl.ds(base, CONV_STRIP), 0:cw] = (dh * sg).astype(BF16)
            dproj_ref[pl.ds(base, CONV_STRIP), cw:2 * cw] = (dh * val * sg * (1.0 - sg)).astype(BF16)
            return carry

        lax.fori_loop(0, tm // CONV_STRIP, strip, 0)

        @pl.when(i == nt - 1)
        def _():
            for kk in range(CONV_K):
                dw_ref[kk:kk + 1, :] = jnp.sum(dwacc[kk], axis=0, keepdims=True)

    vec = pl.BlockSpec((1, cw), lambda i: (0, 0))
    cur = lambda width: pl.BlockSpec((tm, width), lambda i: (i, 0))
    nxt = lambda width: pl.BlockSpec((CONV_HALO, width), lambda i: (jnp.minimum((i + 1) * hb, last_halo), 0))
    return pl.pallas_call(
        body, name="conv_backward", grid=(nt,),
        in_specs=[cur(cw), nxt(cw), cur(cw), nxt(cw), cur(2 * cw),
                  pl.BlockSpec((CONV_HALO, 2 * cw), lambda i: (jnp.maximum(i * hb - 1, 0), 0)),
                  pl.BlockSpec((CONV_K, cw), lambda i: (0, 0)), vec, vec],
        out_specs=[cur(2 * cw), pl.BlockSpec((CONV_K, cw), lambda i: (0, 0)), vec, vec, vec],
        out_shape=[jax.ShapeDtypeStruct((t_rows, 2 * cw), BF16), jax.ShapeDtypeStruct((CONV_K, cw), F32),
                   jax.ShapeDtypeStruct((1, cw), F32), jax.ShapeDtypeStruct((1, cw), F32),
                   jax.ShapeDtypeStruct((1, cw), F32)],
        scratch_shapes=[pltpu.VMEM((tm + CONV_HALO, cw), F32), pltpu.VMEM((tm + CONV_HALO, cw), F32),
                        pltpu.VMEM((CONV_K, 8, cw), F32), pltpu.VMEM((8, SHIFTED_ROWS, cw), F32),
                        pltpu.VMEM((8, SHIFTED_ROWS, cw), F32)],
        compiler_params=_params("arbitrary"),
    )(dout, dout, c, c, proj_a, proj_a, w, ln_g, ln_b)


def _attn_load_kv(kv_hbm, k_pad, v_pad, sem, t_cols):
    k_pad[:, 0:LEFT] = jnp.zeros((ATTN_WIDTH, LEFT), BF16)
    v_pad[:, 0:LEFT] = jnp.zeros((ATTN_WIDTH, LEFT), BF16)
    ck = pltpu.make_async_copy(kv_hbm.at[pl.ds(ATTN_WIDTH, ATTN_WIDTH), :], k_pad.at[:, pl.ds(LEFT, t_cols)], sem.at[0])
    cv = pltpu.make_async_copy(kv_hbm.at[pl.ds(2 * ATTN_WIDTH, ATTN_WIDTH), :], v_pad.at[:, pl.ds(LEFT, t_cols)], sem.at[1])
    ck.start()
    cv.start()
    ck.wait()
    cv.wait()


def _attn_build_bias(tab_ref, bias_t):
    row = lax.broadcasted_iota(jnp.int32, (LANES, LANES), 0)
    lane = lax.broadcasted_iota(jnp.int32, (LANES, LANES), 1)
    upper = lane >= row
    lane64 = lax.broadcasted_iota(jnp.int32, (CHUNK, LANES), 1)
    for h in range(N_HEADS):
        far = jnp.broadcast_to(tab_ref[h:h + 1, 2 * MAX_REL:2 * MAX_REL + 1], (LANES, LANES))
        hi = jnp.broadcast_to(tab_ref[h:h + 1, MAX_REL:2 * MAX_REL], (LANES, LANES))
        lo = jnp.broadcast_to(tab_ref[h:h + 1, 0:MAX_REL], (LANES, LANES))
        hi_d = pltpu.roll(hi, 0, 1, stride=1, stride_axis=0)
        lo_d = pltpu.roll(lo, 0, 1, stride=1, stride_axis=0)
        bias_t[h, 0:WIN - 2 * LANES, :] = jnp.broadcast_to(far[0:1, :], (WIN - 2 * LANES, LANES))
        bias_t[h, WIN - 2 * LANES:WIN - LANES, :] = jnp.where(upper, far, hi_d)
        bias_t[h, WIN - LANES:WIN, :] = jnp.where(upper, hi_d, lo_d)
        bias_t[h, 0:CHUNK, :] = jnp.where(lane64 < CHUNK, bias_t[h, 0:CHUNK, :], NEG_INF)
        bias_t[h, WIN - CHUNK:WIN, :] = jnp.where(lane64 >= CHUNK, bias_t[h, WIN - CHUNK:WIN, :], NEG_INF)


def _head_rows(h):
    return slice(h * HEAD_DIM, (h + 1) * HEAD_DIM)


def _attn_scores(k_pad, q_ref, s_buf, w0):
    for h in range(N_HEADS):
        q_h = q_ref[_head_rows(h), :] * jnp.asarray(SCALE, BF16)
        s_buf[h] = lax.dot_general(k_pad[_head_rows(h), pl.ds(w0, WIN)], q_h, _DOT_DIMS["tn"],
                                   preferred_element_type=F32)


def _attn_logits(s, bias, first_valid, key0=0):
    s = s + bias
    if first_valid is not None:
        s = jnp.where(lax.broadcasted_iota(jnp.int32, s.shape, 0) + key0 >= first_valid, s, NEG_INF)
    return s


def _attn_probs(s, bias_h, first_valid):
    s = _attn_logits(s, bias_h, first_valid)
    top = jnp.max(s, axis=0, keepdims=True)
    e = jnp.exp(s - top)
    total = jnp.sum(e, axis=0, keepdims=True)
    return e * (1.0 / total), top + jnp.log(total)


def _attn_by_padding(m, fn):
    @pl.when(m < LEFT // QBLK)
    def _():
        fn(LEFT - m * QBLK)

    @pl.when(m >= LEFT // QBLK)
    def _():
        fn(None)


def _attn_forward(qkv_t, rel_bias):
    t_cols = qkv_t.shape[1]
    steps = t_cols // QBLK

    def body(q_ref, kv_hbm, tab_ref, o_ref, lse_ref, k_pad, v_pad, bias_t, s_buf, p_buf, sem):
        m = pl.program_id(0)

        @pl.when(m == 0)
        def _():
            _attn_build_bias(tab_ref, bias_t)
            _attn_load_kv(kv_hbm, k_pad, v_pad, sem, t_cols)

        w0 = pl.multiple_of(m * QBLK, QBLK)
        _attn_scores(k_pad, q_ref, s_buf, w0)

        def softmax(first_valid):
            for h in range(N_HEADS):
                p, lse = _attn_probs(s_buf[h], bias_t[h], first_valid)
                p_buf[h] = p.astype(BF16)
                lse_ref[h:h + 1, :] = lse

        _attn_by_padding(m, softmax)
        for h in range(N_HEADS):
            o_h = lax.dot_general(v_pad[_head_rows(h), pl.ds(w0, WIN)], p_buf[h], _DOT_DIMS["nn"],
                                  preferred_element_type=F32)
            o_ref[_head_rows(h), :] = o_h.astype(BF16)

    return pl.pallas_call(
        body, name="attn_forward", grid=(steps,),
        in_specs=[pl.BlockSpec((ATTN_WIDTH, QBLK), lambda m: (0, m)),
                  pl.BlockSpec(memory_space=pl.ANY),
                  pl.BlockSpec((N_HEADS, 2 * MAX_REL + 1), lambda m: (0, 0))],
        out_specs=[pl.BlockSpec((ATTN_WIDTH, QBLK), lambda m: (0, m)), pl.BlockSpec((N_HEADS, QBLK), lambda m: (0, m))],
        out_shape=[jax.ShapeDtypeStruct((ATTN_WIDTH, t_cols), BF16), jax.ShapeDtypeStruct((N_HEADS, t_cols), F32)],
        scratch_shapes=[pltpu.VMEM((ATTN_WIDTH, LEFT + t_cols), BF16), pltpu.VMEM((ATTN_WIDTH, LEFT + t_cols), BF16),
                        pltpu.VMEM((N_HEADS, WIN, QBLK), F32), pltpu.VMEM((N_HEADS, WIN, QBLK), F32),
                        pltpu.VMEM((N_HEADS, WIN, QBLK), BF16), pltpu.SemaphoreType.DMA((2,))],
        compiler_params=_params("arbitrary"),
    )(qkv_t, qkv_t, rel_bias)


def _reverse_lanes(v, flip):
    out = jnp.zeros(v.shape, F32)
    rest = v
    for _ in range(3):
        piece = rest.astype(BF16)
        out = out + lax.dot_general(piece, flip, _DOT_DIMS["nn"], preferred_element_type=F32)
        rest = rest - piece.astype(F32)
    return out


def _attn_bias_grad(dbias_t, drel_ref):
    row = lax.broadcasted_iota(jnp.int32, (LANES, LANES), 0)
    lane = lax.broadcasted_iota(jnp.int32, (LANES, LANES), 1)
    flip = (row + lane == LANES - 1).astype(BF16)
    head = lax.broadcasted_iota(jnp.int32, (N_HEADS, LANES), 0)
    lane8 = lax.broadcasted_iota(jnp.int32, (N_HEADS, LANES), 1)
    upper_rev = jnp.zeros((N_HEADS, LANES), F32)
    lower_rev = jnp.zeros((N_HEADS, LANES), F32)
    far = jnp.zeros((N_HEADS, LANES), F32)
    for h in range(N_HEADS):
        def diagonals(block):
            skew = pltpu.roll(_reverse_lanes(block, flip), 0, 1, stride=1, stride_axis=0)
            pos = jnp.sum(jnp.where(lane >= row, skew, 0.0), axis=0, keepdims=True)
            neg = jnp.sum(jnp.where(lane < row, skew, 0.0), axis=0, keepdims=True)
            return pos, neg

        pos4, neg4 = diagonals(dbias_t[h, WIN - LANES:WIN, :])
        pos3, neg3 = diagonals(dbias_t[h, WIN - 2 * LANES:WIN - LANES, :])
        far_h = jnp.sum(jnp.sum(dbias_t[h, 0:WIN - 2 * LANES, :], axis=0, keepdims=True), axis=1, keepdims=True)
        far_h = far_h + jnp.sum(pos3, axis=1, keepdims=True)
        upper_rev = jnp.where(head == h, pos4 + neg3, upper_rev)
        lower_rev = jnp.where(head == h, neg4, lower_rev)
        far = jnp.where((head == h) & (lane8 == 0), far_h, far)
    drel_ref[:, 0:LANES] = _reverse_lanes(lower_rev, flip)
    drel_ref[:, LANES:2 * LANES] = _reverse_lanes(upper_rev, flip)
    drel_ref[:, 2 * LANES:3 * LANES] = far


def _attn_backward(qkv_t, o_t, do_t, lse, rel_bias):
    t_cols = qkv_t.shape[1]
    steps = t_cols // QBLK
    flush = LEFT // QBLK
    total = steps + flush

    def body(q_ref, o_ref, do_ref, lse_ref, kv_hbm, tab_ref, dq_ref, dk_ref, dv_ref, drel_ref,
             k_pad, v_pad, bias_t, dbias_t, dk_acc, dv_acc, s_buf, dp_buf, p_buf, ds_buf, sem):
        m = pl.program_id(0)

        @pl.when(m == 0)
        def _():
            _attn_build_bias(tab_ref, bias_t)
            _attn_load_kv(kv_hbm, k_pad, v_pad, sem, t_cols)
            dbias_t[...] = jnp.zeros_like(dbias_t)
            dk_acc[...] = jnp.zeros_like(dk_acc)
            dv_acc[...] = jnp.zeros_like(dv_acc)

        @pl.when(m < steps)
        def _():
            w0 = pl.multiple_of(m * QBLK, QBLK)
            _attn_scores(k_pad, q_ref, s_buf, w0)
            for h in range(N_HEADS):
                dp_buf[h] = lax.dot_general(v_pad[_head_rows(h), pl.ds(w0, WIN)], do_ref[_head_rows(h), :],
                                            _DOT_DIMS["tn"], preferred_element_type=F32)

            def softmax_backward(first_valid):
                for h in range(N_HEADS):
                    rows = _head_rows(h)
                    delta = jnp.sum(do_ref[rows, :].astype(F32) * o_ref[rows, :].astype(F32), axis=0, keepdims=True)
                    lse_h = lse_ref[h:h + 1, :]
                    for b in range(WIN // LANES):
                        keys = slice(b * LANES, (b + 1) * LANES)
                        s = _attn_logits(s_buf[h, keys, :], bias_t[h, keys, :], first_valid, b * LANES)
                        p = jnp.exp(s - lse_h)
                        ds = p * (dp_buf[h, keys, :] - delta)
                        dbias_t[h, keys, :] += ds
                        p_buf[h, keys, :] = p.astype(BF16)
                        ds_buf[h, keys, :] = (ds * SCALE).astype(BF16)

            _attn_by_padding(m, softmax_backward)
            for h in range(N_HEADS):
                rows = _head_rows(h)
                dq_h = lax.dot_general(k_pad[rows, pl.ds(w0, WIN)], ds_buf[h], _DOT_DIMS["nn"], preferred_element_type=F32)
                dq_ref[rows, :] = dq_h.astype(BF16)
                dk_h = lax.dot_general(q_ref[rows, :], ds_buf[h], _DOT_DIMS["nt"], preferred_element_type=F32)
                dv_h = lax.dot_general(do_ref[rows, :], p_buf[h], _DOT_DIMS["nt"], preferred_element_type=F32)
                for b in range(WIN // QBLK):
                    slot = pl.multiple_of(lax.rem(m + b, WIN // QBLK) * QBLK, QBLK)
                    dk_acc[rows, pl.ds(slot, QBLK)] += dk_h[:, b * QBLK:(b + 1) * QBLK]
                    dv_acc[rows, pl.ds(slot, QBLK)] += dv_h[:, b * QBLK:(b + 1) * QBLK]

        oldest = pl.multiple_of(lax.rem(m, WIN // QBLK) * QBLK, QBLK)
        dk_ref[...] = dk_acc[:, pl.ds(oldest, QBLK)].astype(BF16)
        dv_ref[...] = dv_acc[:, pl.ds(oldest, QBLK)].astype(BF16)
        dk_acc[:, pl.ds(oldest, QBLK)] = jnp.zeros((ATTN_WIDTH, QBLK), F32)
        dv_acc[:, pl.ds(oldest, QBLK)] = jnp.zeros((ATTN_WIDTH, QBLK), F32)

        @pl.when(m == total - 1)
        def _():
            _attn_bias_grad(dbias_t, drel_ref)

    qblk = pl.BlockSpec((ATTN_WIDTH, QBLK), lambda m: (0, jnp.minimum(m, steps - 1)))
    kblk = pl.BlockSpec((ATTN_WIDTH, QBLK), lambda m: (0, jnp.maximum(m - flush, 0)))
    dq, dk, dv, drel = pl.pallas_call(
        body, name="attn_backward", grid=(total,),
        in_specs=[qblk, qblk, qblk, pl.BlockSpec((N_HEADS, QBLK), lambda m: (0, jnp.minimum(m, steps - 1))),
                  pl.BlockSpec(memory_space=pl.ANY), pl.BlockSpec((N_HEADS, 2 * MAX_REL + 1), lambda m: (0, 0))],
        out_specs=[qblk, kblk, kblk, pl.BlockSpec((N_HEADS, 3 * LANES), lambda m: (0, 0))],
        out_shape=[jax.ShapeDtypeStruct((ATTN_WIDTH, t_cols), BF16)] * 3
        + [jax.ShapeDtypeStruct((N_HEADS, 3 * LANES), F32)],
        scratch_shapes=[pltpu.VMEM((ATTN_WIDTH, LEFT + t_cols), BF16), pltpu.VMEM((ATTN_WIDTH, LEFT + t_cols), BF16),
                        pltpu.VMEM((N_HEADS, WIN, QBLK), F32), pltpu.VMEM((N_HEADS, WIN, QBLK), F32),
                        pltpu.VMEM((ATTN_WIDTH, WIN), F32), pltpu.VMEM((ATTN_WIDTH, WIN), F32),
                        pltpu.VMEM((N_HEADS, WIN, QBLK), F32), pltpu.VMEM((N_HEADS, WIN, QBLK), F32),
                        pltpu.VMEM((N_HEADS, WIN, QBLK), BF16), pltpu.VMEM((N_HEADS, WIN, QBLK), BF16),
                        pltpu.SemaphoreType.DMA((2,))],
        compiler_params=_params("arbitrary"),
    )(qkv_t, o_t, do_t, lse, qkv_t, rel_bias)
    return dq, dk, dv, drel


FFN_TC = D_FF // 2
FFN_TR = 512


def _ffn_specs(t_rows, tr):
    nj = D_FF // FFN_TC
    hb = tr // FFN_HALO
    last_halo = t_rows // FFN_HALO - 1
    cur = lambda off: pl.BlockSpec((tr, FFN_TC), lambda j, i: (i, j + off))
    prev = lambda off: pl.BlockSpec((FFN_HALO, FFN_TC), lambda j, i: (jnp.maximum(i * hb - 1, 0), j + off))
    nxt = lambda off: pl.BlockSpec((FFN_HALO, FFN_TC), lambda j, i: (jnp.minimum((i + 1) * hb, last_halo), j + off))
    wspec = lambda off: pl.BlockSpec((FFN_K, FFN_TC), lambda j, i: (0, j + off))
    bspec = lambda off: pl.BlockSpec((1, FFN_TC), lambda j, i: (0, j + off))
    return nj, cur, prev, nxt, wspec, bspec


FFN_STRIP = 16


def _ffn_conv(win, w, b, rows):
    out = b + w[2] * win[FFN_HALO:FFN_HALO + rows, :]
    out = out + w[1] * win[FFN_HALO - 1:FFN_HALO - 1 + rows, :]
    return out + w[0] * win[FFN_HALO - 2:FFN_HALO - 2 + rows, :]


def _taps(w_ref):
    return [w_ref[kk:kk + 1, :] for kk in range(FFN_K)]


def _ffn_first_window(prev_ref, cur_ref, tile, rows):
    return jnp.concatenate([jnp.where(tile > 0, prev_ref[...], 0.0), cur_ref[0:rows, :]], axis=0)


def _fold8(v):
    return jnp.sum(v.reshape(v.shape[0] // 8, 8, v.shape[1]), axis=0)


def _ffn_activation(hup, w, b):
    t_rows = hup.shape[0]
    tr = _row_tile(t_rows, FFN_TR)
    nj, cur, prev, nxt, wspec, bspec = _ffn_specs(t_rows, tr)
    rs = FFN_STRIP

    def body(g_ref, gprev_ref, v_ref, vprev_ref, wg_ref, wv_ref, bg_ref, bv_ref, act_ref, gel_ref, slope_ref):
        i = pl.program_id(1)
        wg, wv, bg, bv = _taps(wg_ref), _taps(wv_ref), bg_ref[...], bv_ref[...]

        def emit(base, g_win, v_win):
            gel, dgel = _gelu_parts(_ffn_conv(g_win, wg, bg, rs))
            cv = _ffn_conv(v_win, wv, bv, rs)
            act_ref[pl.ds(base, rs), :] = (gel * cv).astype(BF16)
            gel_ref[pl.ds(base, rs), :] = gel
            slope_ref[pl.ds(base, rs), :] = cv * dgel

        def strip(s, carry):
            base = pl.multiple_of(s * rs, rs)
            emit(base, g_ref[pl.ds(base - FFN_HALO, rs + FFN_HALO), :], v_ref[pl.ds(base - FFN_HALO, rs + FFN_HALO), :])
            return carry

        emit(0, _ffn_first_window(gprev_ref, g_ref, i, rs), _ffn_first_window(vprev_ref, v_ref, i, rs))
        lax.fori_loop(1, tr // rs, strip, 0)

    return pl.pallas_call(
        body, name="ffn_activation", grid=(nj, t_rows // tr),
        in_specs=[cur(0), prev(0), cur(nj), prev(nj), wspec(0), wspec(nj), bspec(0), bspec(nj)],
        out_specs=[cur(0), cur(0), cur(0)],
        out_shape=[jax.ShapeDtypeStruct((t_rows, D_FF), BF16), jax.ShapeDtypeStruct((t_rows, D_FF), F32),
                   jax.ShapeDtypeStruct((t_rows, D_FF), F32)],
        compiler_params=_params("parallel", "parallel"),
    )(hup, hup, hup, hup, w, w, b, b)


def _ffn_backward(dact, gel, slope, hup, w):
    t_rows = hup.shape[0]
    tr = _row_tile(t_rows, FFN_TR)
    nj, cur, prev, nxt, wspec, bspec = _ffn_specs(t_rows, tr)
    ni = t_rows // tr
    rs = FFN_STRIP
    ns = tr // rs

    def body(da_ref, danext_ref, gel_ref, gelnext_ref, slope_ref, slopenext_ref, g_ref, gprev_ref, v_ref, vprev_ref,
             wg_ref, wv_ref, dhg_ref, dhv_ref, dwg_ref, dwv_ref, dbg_ref, dbv_ref, sums):
        i = pl.program_id(1)

        @pl.when(i == 0)
        def _():
            sums[...] = jnp.zeros_like(sums)

        wg, wv = _taps(wg_ref), _taps(wv_ref)
        da_after = jnp.where(i < ni - 1, danext_ref[...], 0.0)

        def strip_at(base, g_win, v_win, carry):
            da = da_ref[pl.ds(base, rs), :]
            dcg, dcv = da * slope_ref[pl.ds(base, rs), :], da * gel_ref[pl.ds(base, rs), :]
            out = []
            for half_i, (dc, after, taps, win, dh_ref) in enumerate(((dcg, carry[0], wg, g_win, dhg_ref),
                                                                   (dcv, carry[1], wv, v_win, dhv_ref))):
                ext = jnp.concatenate([dc, after], axis=0)
                dh = taps[2] * dc + taps[1] * ext[1:1 + rs, :] + taps[0] * ext[2:2 + rs, :]
                dh_ref[pl.ds(base, rs), :] = dh.astype(BF16)
                sums[4 * half_i] += _fold8(dc)
                for kk in range(FFN_K):
                    sums[4 * half_i + 1 + kk] += _fold8(dc * win[FFN_HALO - 2 + kk:FFN_HALO - 2 + kk + rs, :])
                out.append(dc[0:FFN_HALO, :])
            return tuple(out)

        def strip(s, carry):
            base = pl.multiple_of((ns - 1 - s) * rs, rs)
            return strip_at(base, g_ref[pl.ds(base - FFN_HALO, rs + FFN_HALO), :],
                            v_ref[pl.ds(base - FFN_HALO, rs + FFN_HALO), :], carry)

        carry = lax.fori_loop(0, ns - 1, strip, (da_after * slopenext_ref[...], da_after * gelnext_ref[...]))
        strip_at(0, _ffn_first_window(gprev_ref, g_ref, i, rs), _ffn_first_window(vprev_ref, v_ref, i, rs), carry)

        @pl.when(i == ni - 1)
        def _():
            for half_i, (db_ref, dw_ref) in enumerate(((dbg_ref, dwg_ref), (dbv_ref, dwv_ref))):
                db_ref[...] = jnp.sum(sums[4 * half_i], axis=0, keepdims=True)
                for kk in range(FFN_K):
                    dw_ref[kk:kk + 1, :] = jnp.sum(sums[4 * half_i + 1 + kk], axis=0, keepdims=True)

    half = jax.ShapeDtypeStruct((t_rows, D_FF), BF16)
    return pl.pallas_call(
        body, name="ffn_backward", grid=(nj, ni),
        in_specs=[cur(0), nxt(0), cur(0), nxt(0), cur(0), nxt(0), cur(0), prev(0), cur(nj), prev(nj),
                  wspec(0), wspec(nj)],
        out_specs=[cur(0), cur(0), wspec(0), wspec(0), bspec(0), bspec(0)],
        out_shape=[half, half, jax.ShapeDtypeStruct((FFN_K, D_FF), F32), jax.ShapeDtypeStruct((FFN_K, D_FF), F32),
                   jax.ShapeDtypeStruct((1, D_FF), F32), jax.ShapeDtypeStruct((1, D_FF), F32)],
        scratch_shapes=[pltpu.VMEM((2 * (1 + FFN_K), 8, FFN_TC), F32)],
        compiler_params=_params("parallel", "arbitrary"),
    )(dact, dact, gel, gel, slope, slope, hup, hup, hup, hup, w, w)


def _mesh_position():
    return lax.axis_index("x"), lax.axis_index("y"), lax.axis_index("c")


def _hbm_specs(n):
    return [pl.BlockSpec(memory_space=pl.ANY)] * n


def _all_gather(shards, name, placed=None):
    n = len(shards)

    def body(*refs):
        ins = refs[:n]
        outs = refs[2 * n:3 * n] if placed else refs[n:2 * n]
        send_sems, recv_sems, local_sems = refs[-3:]
        x, y, c = _mesh_position()
        me, sibling = (x, y, c), (x, y, 1 - c)
        chips = [(1 - x, y), (x, 1 - y), (1 - x, 1 - y)]

        def copy(a, slot, block, to, src=None):
            dst = outs[a].at[4 * block[0] + 2 * block[1] + block[2]]
            return pltpu.make_async_remote_copy(
                src_ref=dst if src is None else src, dst_ref=dst,
                send_sem=send_sems.at[a, slot], recv_sem=recv_sems.at[a, slot],
                device_id=to, device_id_type=MESH)

        started = []
        for a in range(0 if placed else n):
            mine = pltpu.make_async_copy(ins[a], outs[a].at[4 * x + 2 * y + c], local_sems.at[a])
            mine.start()
            started.append(mine)
        first = []
        for a in range(n):
            first.append(copy(a, 0, me, sibling, src=ins[a]))
            first += [copy(a, 1 + j, me, (*chip, c), src=ins[a]) for j, chip in enumerate(chips)]
        for cp in first:
            cp.start()
        passed = []
        for j, chip in enumerate(chips):
            for a in range(n):
                copy(a, 1 + j, (*chip, c), me).wait_recv()
                fwd = copy(a, 4 + j, (*chip, c), sibling)
                fwd.start()
                passed.append(fwd)
        for a in range(n):
            copy(a, 0, sibling, me).wait_recv()
            for j, chip in enumerate(chips):
                copy(a, 4 + j, (*chip, 1 - c), me).wait_recv()
        for cp in first + passed:
            cp.wait_send()
        for mine in started:
            mine.wait()

    operands = [*shards, *placed] if placed else list(shards)
    return pl.pallas_call(
        body, name=name,
        in_specs=_hbm_specs(len(operands)), out_specs=_hbm_specs(n),
        out_shape=[jax.ShapeDtypeStruct((N_DEV,) + s.shape, s.dtype) for s in shards],
        scratch_shapes=[pltpu.SemaphoreType.DMA((n, 7)), pltpu.SemaphoreType.DMA((n, 7)),
                        pltpu.SemaphoreType.DMA((n,))],
        input_output_aliases={n + a: a for a in range(n)} if placed else {},
        compiler_params=pltpu.CompilerParams(has_side_effects=True),
    )(*operands)


def _place_own(shards, me):
    n = len(shards)

    def body(me_ref, *refs):
        for src, dst in zip(refs[:n], refs[n:]):
            dst[0] = src[...]

    return pl.pallas_call(
        body, name="place_own_shards",
        grid_spec=pltpu.PrefetchScalarGridSpec(
            num_scalar_prefetch=1, grid=(1,),
            in_specs=[pl.BlockSpec(s.shape, lambda i, me_ref: (0, 0)) for s in shards],
            out_specs=[pl.BlockSpec((1,) + s.shape, lambda i, me_ref: (me_ref[0], 0, 0)) for s in shards]),
        out_shape=[jax.ShapeDtypeStruct((N_DEV,) + s.shape, s.dtype) for s in shards],
        compiler_params=_params("arbitrary"),
    )(me, *shards)


_FLIPS = [(dx, dy, dc) for dx in (0, 1) for dy in (0, 1) for dc in (0, 1)][1:]
_HBM = pl.BlockSpec(memory_space=pltpu.HBM)
_SEM = pl.BlockSpec(memory_space=pltpu.SEMAPHORE)
_DATAFLOW = pltpu.SideEffectType.DATAFLOW_SIDE_EFFECTING


def _scatter_copies(src_refs, land_refs, send_sems, recv_sems, gather):
    x, y, c = _mesh_position()
    me = 4 * x + 2 * y + c
    copies = []
    for a, (src, land) in enumerate(zip(src_refs, land_refs)):
        for k, (dx, dy, dc) in enumerate(_FLIPS):
            px, py, pc = (x + dx) % 2, (y + dy) % 2, (c + dc) % 2
            pair = a * len(_FLIPS) + k
            copies.append(pltpu.make_async_remote_copy(
                src_ref=src if gather else src.at[4 * px + 2 * py + pc], dst_ref=land.at[me],
                send_sem=send_sems[pair], recv_sem=recv_sems[pair],
                device_id=(px, py, pc), device_id_type=MESH))
    return copies


def _scatter_start(srcs, lands, gather, name, after=None):
    n = len(srcs)
    pairs = n * len(_FLIPS)
    extra = [] if after is None else [after]

    def body(*refs):
        src_refs, land_refs = refs[:n], refs[n:2 * n]
        first = 2 * n + len(extra)
        send_sems, recv_sems = refs[first:first + pairs], refs[first + pairs:first + 2 * pairs]
        token = refs[-1]
        for cp in _scatter_copies(src_refs, land_refs, send_sems, recv_sems, gather):
            cp.start()
        token[...] = jnp.zeros_like(token)

    arrays = [*srcs, *lands]
    sem = pltpu.SemaphoreType.DMA(())
    out = pl.pallas_call(
        body, name=name,
        out_shape=(*[sem] * (2 * pairs), *[pltpu.HBM(v.shape, v.dtype) for v in arrays],
                   jax.ShapeDtypeStruct((8, LANES), F32)),
        in_specs=[*[_HBM] * (2 * n), *[pl.BlockSpec(memory_space=pl.ANY)] * len(extra)],
        out_specs=(*[_SEM] * (2 * pairs), *[_HBM] * (2 * n), pl.BlockSpec(memory_space=pltpu.VMEM)),
        input_output_aliases={i: 2 * pairs + i for i in range(2 * n)},
        compiler_params=pltpu.CompilerParams(has_side_effects=_DATAFLOW),
    )(*[pltpu.with_memory_space_constraint(v, pltpu.HBM) for v in arrays], *extra)
    sems, rest = out[:2 * pairs], out[2 * pairs:]
    return list(sems[:pairs]), list(sems[pairs:]), list(rest[:n]), list(rest[n:2 * n]), rest[-1]


def _scatter_wait(send_sems, recv_sems, srcs, lands, after, gather, name):
    n = len(srcs)
    pairs = n * len(_FLIPS)

    def body(*refs):
        src_refs, land_refs = refs[:n], refs[n:2 * n]
        send_refs, recv_refs = refs[2 * n:2 * n + pairs], refs[2 * n + pairs:2 * n + 2 * pairs]
        for cp in _scatter_copies(src_refs, land_refs, send_refs, recv_refs, gather):
            cp.wait_send()
            cp.wait_recv()

    arrays = [*srcs, *lands]
    out = pl.pallas_call(
        body, name=name,
        out_shape=tuple(pltpu.HBM(v.shape, v.dtype) for v in arrays),
        in_specs=[*[_HBM] * (2 * n), *[_SEM] * (2 * pairs), pl.BlockSpec(memory_space=pl.ANY)],
        out_specs=tuple([_HBM] * (2 * n)),
        input_output_aliases={i: i for i in range(2 * n)},
        compiler_params=pltpu.CompilerParams(has_side_effects=_DATAFLOW),
    )(*arrays, *send_sems, *recv_sems, after)
    return list(out[:n]), list(out[n:])


def _sum_devices(gathered):
    _, rows, cols = gathered.shape

    def body(g_ref, o_ref):
        total = g_ref[0]
        for d in range(1, N_DEV):
            total = total + g_ref[d]
        o_ref[...] = total

    return pl.pallas_call(
        body, name="sum_small_grads",
        out_shape=jax.ShapeDtypeStruct((rows, cols), F32),
        compiler_params=_params(),
    )(gathered)


def _adamw(w, g, m, v, name):
    rows, cols = w.shape
    tr = rows
    for cand in (256, 128, 64, 32, 16, 8):
        if rows > cand and rows % cand == 0:
            tr = cand
            break

    def body(w_ref, g_ref, m_ref, v_ref, delta_ref, newm_ref, newv_ref):
        g_v = g_ref[...]
        new_m = ADAM_B1 * m_ref[...] + (1.0 - ADAM_B1) * g_v
        new_v = ADAM_B2 * v_ref[...] + (1.0 - ADAM_B2) * (g_v * g_v)
        m_hat = new_m / (1.0 - ADAM_B1 ** ADAM_STEP)
        v_hat = new_v / (1.0 - ADAM_B2 ** ADAM_STEP)
        delta_ref[...] = -ADAM_LR * (m_hat / (jnp.sqrt(v_hat) + ADAM_EPS) + ADAM_WD * w_ref[...])
        newm_ref[...] = new_m
        newv_ref[...] = new_v

    blk = pl.BlockSpec((tr, cols), lambda i: (i, 0))
    shape = jax.ShapeDtypeStruct((rows, cols), F32)
    return pl.pallas_call(
        body, name=name, grid=(rows // tr,),
        in_specs=[blk] * 4, out_specs=[blk] * 3, out_shape=[shape] * 3,
        compiler_params=_params("parallel"),
    )(w, g, m, v)


def _adamw_update(w, g, m, v):
    new_m = ADAM_B1 * m + (1.0 - ADAM_B1) * g
    new_v = ADAM_B2 * v + (1.0 - ADAM_B2) * (g * g)
    m_hat = new_m / (1.0 - ADAM_B1 ** ADAM_STEP)
    v_hat = new_v / (1.0 - ADAM_B2 ** ADAM_STEP)
    return -ADAM_LR * (m_hat / (jnp.sqrt(v_hat) + ADAM_EPS) + ADAM_WD * w), new_m, new_v


def _reduce_and_adamw(grad, received, me, w, m, v, name):
    _, rows, cols = grad.shape
    tr = rows // 2
    assert rows % 16 == 0

    def body(me_ref, g_ref, r_ref, w_ref, m_ref, v_ref, grad_ref, delta_ref, newm_ref, newv_ref, acc):
        p = pl.program_id(1)
        term = jnp.where(p == me_ref[0], g_ref[0], r_ref[0]).astype(F32)

        @pl.when(p == 0)
        def _():
            acc[...] = term

        @pl.when(p > 0)
        def _():
            acc[...] += term

        @pl.when(p == N_DEV - 1)
        def _():
            g = acc[...]
            grad_ref[...] = g
            delta_ref[...], newm_ref[...], newv_ref[...] = _adamw_update(w_ref[...], g, m_ref[...], v_ref[...])

    blk = (1, tr, cols)
    tile = pl.BlockSpec((tr, cols), lambda j, p, me_ref: (j, 0))
    shape = jax.ShapeDtypeStruct((rows, cols), F32)
    return pl.pallas_call(
        body, name=name,
        grid_spec=pltpu.PrefetchScalarGridSpec(
            num_scalar_prefetch=1, grid=(rows // tr, N_DEV),
            in_specs=[pl.BlockSpec(blk, lambda j, p, me_ref: (me_ref[0], j, 0)),
                      pl.BlockSpec(blk, lambda j, p, me_ref: (p, j, 0)), tile, tile, tile],
            out_specs=[tile] * 4,
            scratch_shapes=[pltpu.VMEM((tr, cols), F32)]),
        out_shape=[shape] * 4,
        compiler_params=_params("parallel", "arbitrary"),
    )(me, grad, received, w, m, v)


def _pack(pieces, rows):
    flat = jnp.concatenate([p.reshape(-1) for p in pieces])
    return jnp.pad(flat, (0, rows * LANES - flat.shape[0])).reshape(rows, LANES)


def _unpack(packed, shapes):
    flat = packed.reshape(-1)
    out, pos = [], 0
    for shape in shapes:
        size = 1
        for s in shape:
            size *= s
        out.append(flat[pos:pos + size].reshape(shape))
        pos += size
    return out


def _rows_for(count):
    return -(-count // (8 * LANES)) * 8


SMALL = ("norm_mix_pre", "conv_dw_b", "conv_ln_g", "conv_ln_b", "rel_bias", "norm_mix_post", "norm_ffn_pre",
         "ffn_dw_b", "norm_ffn_post")
SHARDED_SMALL = ("conv_dw_w", "ffn_dw_w")
LARGE = ("w_in", "w_out", "w_up", "w_down")
WEIGHTS = ("norm_mix_pre", "w_in", "conv_dw_w", "conv_dw_b", "conv_ln_g", "conv_ln_b", "rel_bias", "w_out",
           "norm_mix_post", "norm_ffn_pre", "w_up", "ffn_dw_w", "ffn_dw_b", "w_down", "norm_ffn_post")


def kernel(x, norm_mix_pre, w_in, conv_dw_w, conv_dw_b, conv_ln_g, conv_ln_b, rel_bias, w_out, norm_mix_post, norm_ffn_pre, w_up, ffn_dw_w, ffn_dw_b, w_down, norm_ffn_post, loss_target, m_norm_mix_pre, m_w_in, m_conv_dw_w, m_conv_dw_b, m_conv_ln_g, m_conv_ln_b, m_rel_bias, m_w_out, m_norm_mix_post, m_norm_ffn_pre, m_w_up, m_ffn_dw_w, m_ffn_dw_b, m_w_down, m_norm_ffn_post, v_norm_mix_pre, v_w_in, v_conv_dw_w, v_conv_dw_b, v_conv_ln_g, v_conv_ln_b, v_rel_bias, v_w_out, v_norm_mix_post, v_norm_ffn_pre, v_w_up, v_ffn_dw_w, v_ffn_dw_b, v_w_down, v_norm_ffn_post):
    weights = dict(norm_mix_pre=norm_mix_pre, w_in=w_in, conv_dw_w=conv_dw_w, conv_dw_b=conv_dw_b, conv_ln_g=conv_ln_g,
                   conv_ln_b=conv_ln_b, rel_bias=rel_bias, w_out=w_out, norm_mix_post=norm_mix_post,
                   norm_ffn_pre=norm_ffn_pre, w_up=w_up, ffn_dw_w=ffn_dw_w, ffn_dw_b=ffn_dw_b, w_down=w_down,
                   norm_ffn_post=norm_ffn_post)
    mom1 = dict(norm_mix_pre=m_norm_mix_pre, w_in=m_w_in, conv_dw_w=m_conv_dw_w, conv_dw_b=m_conv_dw_b,
                conv_ln_g=m_conv_ln_g, conv_ln_b=m_conv_ln_b, rel_bias=m_rel_bias, w_out=m_w_out,
                norm_mix_post=m_norm_mix_post, norm_ffn_pre=m_norm_ffn_pre, w_up=m_w_up, ffn_dw_w=m_ffn_dw_w,
                ffn_dw_b=m_ffn_dw_b, w_down=m_w_down, norm_ffn_post=m_norm_ffn_post)
    mom2 = dict(norm_mix_pre=v_norm_mix_pre, w_in=v_w_in, conv_dw_w=v_conv_dw_w, conv_dw_b=v_conv_dw_b,
                conv_ln_g=v_conv_ln_g, conv_ln_b=v_conv_ln_b, rel_bias=v_rel_bias, w_out=v_w_out,
                norm_mix_post=v_norm_mix_post, norm_ffn_pre=v_norm_ffn_pre, w_up=v_w_up, ffn_dw_w=v_ffn_dw_w,
                ffn_dw_b=v_ffn_dw_b, w_down=v_w_down, norm_ffn_post=v_norm_ffn_post)

    x2 = x[0]
    target = loss_target[0]
    t_rows = x2.shape[0]
    d = D_MODEL
    in_cols = 2 * CONV_WIDTH + 3 * ATTN_WIDTH
    my_x, my_y, my_c = _mesh_position()
    my_dev = 4 * my_x + 2 * my_y + my_c

    small_conv = _pack([conv_dw_w[0], ffn_dw_w[0]], 32)
    me = jnp.reshape(my_dev, (1,)).astype(jnp.int32)
    first_shards = [w_in[0].T.astype(BF16), small_conv]
    late_shards = [w_out[0].astype(BF16), w_up[0].T.astype(BF16), w_down[0].astype(BF16)]
    placed = _place_own(first_shards + late_shards, me)
    win_t, conv_g = _all_gather(first_shards, "all_gather_weights", placed=placed[:2])
    wout_gather = _scatter_start(late_shards[:1], placed[2:3], True, "gather_w_out_start", after=win_t)
    ffn_gather = _scatter_start(late_shards[1:], placed[3:], True, "gather_ffn_weights_start", after=wout_gather[4])
    late_token = ffn_gather[4]
    win_t = win_t.reshape(in_cols, d)
    conv_flat = conv_g.reshape(N_DEV, 32 * LANES)
    n_cw = CONV_K * (CONV_WIDTH // N_DEV)
    conv_w_full = conv_flat[:, :n_cw].reshape(N_DEV, CONV_K, CONV_WIDTH // N_DEV).transpose(1, 0, 2).reshape(CONV_K, CONV_WIDTH)
    ffn_w_full = conv_flat[:, n_cw:].reshape(N_DEV, FFN_K, 2 * D_FF // N_DEV).transpose(1, 0, 2).reshape(FFN_K, 2 * D_FF)

    u1, proj_a, qkv_t = _pre_norm_proj(x2, norm_mix_pre + late_token[0:1, 0:1], win_t, 2 * CONV_WIDTH, "pre_norm_proj")
    conv_c, conv_out = _conv_forward(proj_a, conv_w_full, conv_dw_b, conv_ln_g, conv_ln_b)
    o_t, attn_lse = _attn_forward(qkv_t, rel_bias[0])
    _, (wout_g,) = _scatter_wait(*wout_gather[:4], o_t, True, "gather_w_out_wait")
    wout_g = wout_g.reshape(d, d)
    mixed, h1, u2 = _matmul_rows(
        [(conv_out, "nn", CONV_WIDTH, 0), (o_t, "tn", ATTN_WIDTH, CONV_WIDTH)], wout_g, m=t_rows, n=d, tm=1024,
        name="out_proj_mid_forward", row_ins=[x2], vec_ins=[norm_mix_post, norm_ffn_pre], row_outs=[F32, F32, BF16],
        acc_outs=[], epilogue=_mid_forward_epilogue)
    _, (wup_t, wdown_g) = _scatter_wait(*ffn_gather[:4], u2, True, "gather_ffn_weights_wait")
    wup_t = wup_t.reshape(2 * D_FF, d)
    wdown_g = wdown_g.reshape(D_FF, d)
    hup = _matmul(u2, wup_t, mode="nt", m=t_rows, n=2 * D_FF, k=d, tm=2048, tn=1408, tk=d,
                  out_dtype=F32, name="ffn_up")
    act, ffn_gel, ffn_slope = _ffn_activation(hup, ffn_w_full, ffn_dw_b)
    dy, df, _, d_norm_ffn_post, loss = _matmul_rows(
        [(act, "nn", D_FF, 0)], wdown_g, m=t_rows, n=d, tm=1024, name="ffn_down_loss_backward",
        row_ins=[h1, target], vec_ins=[norm_ffn_post], row_outs=[F32, BF16], acc_outs=[(1, d), (1, d), (1, 1)],
        epilogue=_loss_epilogue)

    dact = _matmul(df, wdown_g, mode="nt", m=t_rows, n=D_FF, k=d, tm=2048, tn=1408, tk=d,
                   out_dtype=F32, name="ffn_down_dx")
    g_wdown = _matmul(act, df, mode="tn", m=D_FF, n=d, k=t_rows, tm=1408, tn=1024, tk=2048,
                      out_dtype=F32, name="ffn_down_dw")
    dhg, dhv, dwg, dwv, dbg, dbv = _ffn_backward(dact, ffn_gel, ffn_slope, hup, ffn_w_full)
    g_wup_t = _matmul(dhg, u2, mode="tn", m=D_FF, n=d, k=t_rows, tm=1408, tn=1024, tk=2048, out_dtype=F32,
                      name="ffn_up_dw_gate", out_rows=2 * D_FF)
    g_wup_t = _matmul(dhv, u2, mode="tn", m=D_FF, n=d, k=t_rows, tm=1408, tn=1024, tk=2048, out_dtype=F32,
                      name="ffn_up_dw_value", out_rows=2 * D_FF, out_m0=D_FF, into=g_wup_t)
    ffn_grads = [g_wup_t.reshape(N_DEV, 2 * D_FF // N_DEV, d), g_wdown.reshape(N_DEV, D_FF // N_DEV, d)]
    red_send, red_recv, ffn_grads, red_lands, red_token = _scatter_start(
        ffn_grads, [lax.empty(g.shape, F32) for g in ffn_grads], False, "reduce_ffn_grads_start")
    dh1, dmixed, d_norm_ffn_pre, d_norm_mix_post = _matmul_rows(
        [(dhg, "nn", D_FF, 0), (dhv, "nn", D_FF, D_FF)], wup_t, m=t_rows, n=d, tm=512, name="ffn_up_dx_mid_backward",
        row_ins=[dy, h1, mixed], vec_ins=[norm_ffn_pre + red_token[0:1, 0:1], norm_mix_post], row_outs=[F32, BF16],
        acc_outs=[(1, d), (1, d)], epilogue=_mid_backward_epilogue)
    dconv_out = _matmul(dmixed, wout_g, mode="nt", m=t_rows, n=CONV_WIDTH, k=d, tm=2048, tn=512, tk=d,
                        out_dtype=F32, name="out_proj_dx_conv")
    do_t = _matmul(wout_g, dmixed, mode="nt", m=ATTN_WIDTH, n=t_rows, k=d, tm=512, tn=2048, tk=d,
                   out_dtype=BF16, name="out_proj_dx_attn", a_m0=CONV_WIDTH)
    g_wout = _matmul(conv_out, dmixed, mode="tn", m=CONV_WIDTH, n=d, k=t_rows, tm=512, tn=1024, tk=2048, out_dtype=F32,
                     name="out_proj_dw_conv", out_rows=d)
    g_wout = _matmul(o_t, dmixed, mode="nn", m=ATTN_WIDTH, n=d, k=t_rows, tm=512, tn=1024, tk=2048, out_dtype=F32,
                     name="out_proj_dw_attn", out_rows=d, out_m0=CONV_WIDTH, into=g_wout)
    wout_handle = _scatter_start([g_wout.reshape(N_DEV, d // N_DEV, d)], [lax.empty((N_DEV, d // N_DEV, d), F32)],
                                 False, "reduce_w_out_grad_start")
    dproj_a, d_conv_w, d_conv_b, d_ln_g, d_ln_b = _conv_backward(
        dconv_out, conv_c, proj_a, conv_w_full, conv_ln_g + wout_handle[4][0:1, 0:1], conv_ln_b)
    dqkv_parts = _attn_backward(qkv_t, o_t, do_t, attn_lse, rel_bias[0])
    drel = dqkv_parts[3]
    g_win_t = _matmul(dproj_a, u1, mode="tn", m=2 * CONV_WIDTH, n=d, k=t_rows, tm=1024, tn=1024, tk=2048, out_dtype=BF16,
                      name="proj_dw_conv", out_rows=in_cols)
    for j, part in enumerate("qkv"):
        row0 = 2 * CONV_WIDTH + j * ATTN_WIDTH
        g_win_t = _matmul(dqkv_parts[j], u1, mode="nn", m=ATTN_WIDTH, n=d, k=t_rows, tm=512, tn=1024, tk=2048,
                          out_dtype=BF16, name="proj_dw_" + part, out_rows=in_cols, out_m0=row0, into=g_win_t)
    win_handle = _scatter_start([g_win_t.reshape(N_DEV, in_cols // N_DEV, d)],
                                [lax.empty((N_DEV, in_cols // N_DEV, d), BF16)], False, "reduce_w_in_grad_start")
    dx, d_norm_mix_pre = _matmul_rows(
        [(dproj_a, "nn", 2 * CONV_WIDTH, 0)]
        + [(dqkv_parts[j], "tn", ATTN_WIDTH, 2 * CONV_WIDTH + j * ATTN_WIDTH) for j in range(3)],
        win_t, m=t_rows, n=d, tm=1024, name="proj_dx_input_backward", row_ins=[dh1, x2],
        vec_ins=[norm_mix_pre + win_handle[4][0:1, 0:1]], row_outs=[F32], acc_outs=[(1, d)],
        epilogue=_input_backward_epilogue)

    small_grads = dict(norm_mix_pre=d_norm_mix_pre, conv_dw_b=d_conv_b, conv_ln_g=d_ln_g, conv_ln_b=d_ln_b,
                       rel_bias=drel[:, :2 * MAX_REL + 1], norm_mix_post=d_norm_mix_post, norm_ffn_pre=d_norm_ffn_pre,
                       ffn_dw_b=jnp.concatenate([dbg, dbv], axis=1), norm_ffn_post=d_norm_ffn_post)
    pieces = [small_grads[nm] for nm in SMALL] + [d_conv_w, jnp.concatenate([dwg, dwv], axis=1), loss]
    count = sum(p.size for p in pieces)
    (gathered_small,) = _all_gather([_pack(pieces, _rows_for(count))], "all_gather_small_grads")
    summed = _sum_devices(gathered_small)
    shapes = [weights[nm].shape for nm in SMALL] + [(CONV_K, CONV_WIDTH), (FFN_K, 2 * D_FF), (1, 1)]
    unpacked = _unpack(summed, shapes)
    grads = dict(zip(SMALL, unpacked[:len(SMALL)]))
    cw_shard, fw_shard = CONV_WIDTH // N_DEV, 2 * D_FF // N_DEV
    grads["conv_dw_w"] = lax.dynamic_slice_in_dim(unpacked[-3], my_dev * cw_shard, cw_shard, axis=1)[None]
    grads["ffn_dw_w"] = lax.dynamic_slice_in_dim(unpacked[-2], my_dev * fw_shard, fw_shard, axis=1)[None]
    total_loss = unpacked[-1].reshape(())

    me = jnp.reshape(my_dev, (1,)).astype(jnp.int32)
    delta, new_m, new_v = {}, {}, {}

    def finish(nm, send, recv, srcs, lands, after, transposed):
        srcs, lands = _scatter_wait(send, recv, srcs, lands, after, False, "reduce_" + nm + "_grad_wait")
        for name_a, src, land in zip(nm.split("_and_"), srcs, lands):
            flip = (lambda t: t.T) if transposed[name_a] else (lambda t: t)
            outs = _reduce_and_adamw(src, land, me, flip(weights[name_a][0]), flip(mom1[name_a][0]),
                                     flip(mom2[name_a][0]), "reduce_adamw_" + name_a)
            for store, arr in zip((grads, delta, new_m, new_v), outs):
                store[name_a] = flip(arr)[None]

    transposed = dict(w_in=True, w_out=False, w_up=True, w_down=False)
    finish("w_up_and_w_down", red_send, red_recv, ffn_grads, red_lands, dx, transposed)
    finish("w_out", *wout_handle[:4], dx, transposed)
    finish("w_in", *win_handle[:4], delta["w_up"], transposed)
    small_names = SMALL + SHARDED_SMALL
    small_count = sum(weights[nm].size for nm in small_names)
    small_rows = _rows_for(small_count)
    packed = [_pack([src[nm] for nm in small_names], small_rows) for src in (weights, grads, mom1, mom2)]
    outs = _adamw(*packed, "adamw_small")
    small_shapes = [weights[nm].shape for nm in small_names]
    for store, arr in zip((delta, new_m, new_v), outs):
        store.update(zip(small_names, _unpack(arr, small_shapes)))

    return (total_loss, dx[None], *[grads[nm] for nm in WEIGHTS], *[delta[nm] for nm in WEIGHTS],
            *[new_m[nm] for nm in WEIGHTS], *[new_v[nm] for nm in WEIGHTS])
```
